```python
import jax
import jax.numpy as jnp
from jax import lax
import numpy as np

D_MODEL = 1024
BATCH = 4
SEQ = 4096
DEPTH = 1

HEAD_DIM = 128
N_HEADS_DN = 4
N_HEADS_MOBA = 4
D_DN = N_HEADS_DN * HEAD_DIM
D_MOBA = N_HEADS_MOBA * HEAD_DIM
D_MIX = D_DN + D_MOBA
CONV_K = 4
DN_CHUNK = 64
MOBA_BLOCK = 256
MOBA_TOPK = 3
MOBA_Q_CHUNK = 64
ROPE_THETA = 10000.0
N_GROUPS = 4
EXPERTS_PER_GROUP = 4
N_EXPERTS = N_GROUPS * EXPERTS_PER_GROUP
TOPK_IN_GROUP = 2
D_EXPERT = 256
LN_EPS = 1e-5
RMS_EPS = 1e-6
NEG_INF = -1e30
DEEPNORM_ALPHA = (2 * DEPTH) ** 0.25
DEEPNORM_BETA = (8 * DEPTH) ** -0.25

IN_PROJ_SIZES = (D_DN, D_DN, D_DN, D_DN, N_HEADS_DN, N_HEADS_DN, D_MOBA, D_MOBA, D_MOBA)
IN_PROJ_SPLITS = tuple(sum(IN_PROJ_SIZES[:i + 1]) for i in range(len(IN_PROJ_SIZES) - 1))
D_IN_PROJ = sum(IN_PROJ_SIZES)

kernel_name = 'hybrid_deltanet_moba_hmoe_deepnorm'


def causal_depthwise_conv(u, w):
    k_width, channels = w.shape
    return lax.conv_general_dilated(
        u, w[:, None, :].astype(u.dtype), window_strides=(1,), padding=[(k_width - 1, 0)],
        dimension_numbers=('NWC', 'WIO', 'NWC'), feature_group_count=channels)


def l2_normalize(t, eps=1e-6):
    return t * lax.rsqrt(jnp.sum(t * t, axis=-1, keepdims=True) + eps)


def layer_norm(t, g, b):
    tf = t.astype(jnp.float32)
    mu = jnp.mean(tf, axis=-1, keepdims=True)
    var = jnp.mean(jnp.square(tf - mu), axis=-1, keepdims=True)
    y = (tf - mu) * lax.rsqrt(var + LN_EPS) * g.astype(jnp.float32) + b.astype(jnp.float32)
    return y.astype(t.dtype)


def apply_rope(t, positions):
    half = HEAD_DIM // 2
    inv_freq = ROPE_THETA ** (-jnp.arange(half, dtype=jnp.float32) / half)
    ang = positions.astype(jnp.float32)[:, None] * inv_freq[None, :]
    cos, sin = jnp.cos(ang), jnp.sin(ang)
    tf = t.astype(jnp.float32)
    t1, t2 = tf[..., :half], tf[..., half:]
    return jnp.concatenate([t1 * cos - t2 * sin, t2 * cos + t1 * sin], axis=-1).astype(t.dtype)


def gated_deltanet(q, k, v, z, b, a, conv_w, a_log, dt_bias, norm_w):
    bsz, seq, _ = q.shape
    dtype = q.dtype
    f32 = jnp.float32
    qkv = jax.nn.silu(causal_depthwise_conv(jnp.concatenate([q, k, v], axis=-1), conv_w))
    q, k, v = jnp.split(qkv.astype(f32), 3, axis=-1)

    def heads(t):
        return t.reshape(bsz, seq, N_HEADS_DN, HEAD_DIM).transpose(0, 2, 1, 3)

    q = l2_normalize(heads(q)) * (HEAD_DIM ** -0.5)
    k = l2_normalize(heads(k))
    v = heads(v)
    beta = jax.nn.sigmoid(b.astype(f32)).transpose(0, 2, 1)
    g = (-jnp.exp(a_log.astype(f32)) *
         jax.nn.softplus(a.astype(f32) + dt_bias.astype(f32))).transpose(0, 2, 1)

    nc = seq // DN_CHUNK

    def chunks(t):
        return t.reshape(bsz, N_HEADS_DN, nc, DN_CHUNK, *t.shape[3:])

    q, k, v, beta, g = chunks(q), chunks(k), chunks(v), chunks(beta), chunks(g)
    g = jnp.cumsum(g, axis=-1)
    idx = jnp.arange(DN_CHUNK)
    incl = idx[:, None] >= idx[None, :]
    strict = idx[:, None] > idx[None, :]
    decay = jnp.exp(jnp.where(incl, g[..., :, None] - g[..., None, :], -jnp.inf))
    k_beta = k * beta[..., None]
    v_beta = v * beta[..., None]
    a_mat = jnp.where(strict, jnp.einsum('bhncd,bhnsd->bhncs', k_beta, k) * decay, 0.0)
    lhs = a_mat + jnp.eye(DN_CHUNK, dtype=f32)
    w_c = lax.linalg.triangular_solve(lhs, k_beta * jnp.exp(g)[..., None], left_side=True,
                                      lower=True, unit_diagonal=True)
    u_c = lax.linalg.triangular_solve(lhs, v_beta, left_side=True, lower=True, unit_diagonal=True)
    qk = jnp.einsum('bhncd,bhnsd->bhncs', q, k) * decay
    q_dec = q * jnp.exp(g)[..., None]
    k_dec = k * jnp.exp(g[..., -1:] - g)[..., None]
    g_last = jnp.exp(g[..., -1])

    def step(state, inp):
        qk_c, qd_c, wc, uc, kd_c, gl_c = inp
        v_new = uc - jnp.einsum('bhck,bhkv->bhcv', wc, state)
        o_c = jnp.einsum('bhck,bhkv->bhcv', qd_c, state) + jnp.einsum('bhcs,bhsv->bhcv', qk_c, v_new)
        state = state * gl_c[..., None, None] + jnp.einsum('bhck,bhcv->bhkv', kd_c, v_new)
        return state, o_c

    s0 = jnp.zeros((bsz, N_HEADS_DN, HEAD_DIM, HEAD_DIM), f32)
    xs = tuple(jnp.moveaxis(t, 2, 0) for t in (qk, q_dec, w_c, u_c, k_dec, g_last))
    _, o = lax.scan(step, s0, xs)
    o = jnp.moveaxis(o, 0, 2).reshape(bsz, N_HEADS_DN, seq, HEAD_DIM).transpose(0, 2, 1, 3)
    o = o * lax.rsqrt(jnp.mean(o * o, axis=-1, keepdims=True) + RMS_EPS) * norm_w.astype(f32)
    o = o * jax.nn.silu(z.astype(f32).reshape(bsz, seq, N_HEADS_DN, HEAD_DIM))
    return o.reshape(bsz, seq, D_DN).astype(dtype)


def moba_attention(q, k, v):
    bsz, seq, _ = q.shape
    dtype = q.dtype

    def heads(t):
        return t.reshape(bsz, seq, N_HEADS_MOBA, HEAD_DIM).transpose(0, 2, 1, 3)

    pos = jnp.arange(seq)
    q = apply_rope(heads(q), pos) * (HEAD_DIM ** -0.5)
    k = apply_rope(heads(k), pos)
    v = heads(v)
    nb = -(-seq // MOBA_BLOCK)
    pad = nb * MOBA_BLOCK - seq
    k = jnp.pad(k, ((0, 0), (0, 0), (0, pad), (0, 0)))
    v = jnp.pad(v, ((0, 0), (0, 0), (0, pad), (0, 0)))
    kb = k.reshape(bsz, N_HEADS_MOBA, nb, MOBA_BLOCK, HEAD_DIM)
    vb = v.reshape(bsz, N_HEADS_MOBA, nb, MOBA_BLOCK, HEAD_DIM)
    kmean = jnp.mean(kb.astype(jnp.float32), axis=3)
    topk = min(MOBA_TOPK, nb)
    gather_blocks = jax.vmap(jax.vmap(lambda blk, ix: blk[ix]))
    blk_ids = jnp.arange(nb)
    n_chunks = seq // MOBA_Q_CHUNK

    def chunk(c):
        start = c * MOBA_Q_CHUNK
        qc = lax.dynamic_slice_in_dim(q, start, MOBA_Q_CHUNK, axis=2)
        qpos = start + jnp.arange(MOBA_Q_CHUNK)
        own = start // MOBA_BLOCK
        gate = jnp.einsum('bhqd,bhnd->bhqn', qc.astype(jnp.float32), kmean)
        gate = jnp.where(blk_ids < own, gate, NEG_INF)
        _, sel = lax.top_k(gate, topk)
        valid = sel < own
        k_sel = gather_blocks(kb, sel)
        v_sel = gather_blocks(vb, sel)
        s_sel = jnp.einsum('bhqd,bhqnkd->bhqnk', qc, k_sel).astype(jnp.float32)
        s_sel = jnp.where(valid[..., None], s_sel, NEG_INF)
        s_sel = s_sel.reshape(bsz, N_HEADS_MOBA, MOBA_Q_CHUNK, topk * MOBA_BLOCK)
        k_own = lax.dynamic_index_in_dim(kb, own, axis=2, keepdims=False)
        v_own = lax.dynamic_index_in_dim(vb, own, axis=2, keepdims=False)
        s_own = jnp.einsum('bhqd,bhkd->bhqk', qc, k_own).astype(jnp.float32)
        kpos = own * MOBA_BLOCK + jnp.arange(MOBA_BLOCK)
        s_own = jnp.where(kpos[None, :] <= qpos[:, None], s_own, NEG_INF)
        p = jax.nn.softmax(jnp.concatenate([s_sel, s_own], axis=-1), axis=-1).astype(dtype)
        p_sel = p[..., :topk * MOBA_BLOCK].reshape(bsz, N_HEADS_MOBA, MOBA_Q_CHUNK, topk, MOBA_BLOCK)
        p_own = p[..., topk * MOBA_BLOCK:]
        return (jnp.einsum('bhqnk,bhqnkd->bhqd', p_sel, v_sel) +
                jnp.einsum('bhqk,bhkd->bhqd', p_own, v_own))

    out = lax.map(chunk, jnp.arange(n_chunks))
    out = out.transpose(1, 0, 3, 2, 4).reshape(bsz, seq, D_MOBA)
    return out.astype(dtype)


def hierarchical_moe(h, router_w1, router_b1, router_w2, router_b2, w_gate, w_up, w_down):
    bsz, seq, d = h.shape
    xt = h.reshape(-1, d)
    f32 = jnp.float32
    p_group = jax.nn.softmax(jnp.einsum('nd,dg->ng', xt, router_w1).astype(f32)
                             + router_b1.astype(f32), axis=-1)
    pg, gsel = lax.top_k(p_group, 1)
    logits2 = jnp.einsum('nd,gde->nge', xt, router_w2).astype(f32) + router_b2.astype(f32)
    logits2 = jnp.einsum('nge,ng->ne', logits2, jax.nn.one_hot(gsel[:, 0], N_GROUPS, dtype=f32))
    pe, esel = lax.top_k(jax.nn.softmax(logits2, axis=-1), TOPK_IN_GROUP)
    weights = pg * (pe / jnp.sum(pe, axis=-1, keepdims=True))
    expert_id = gsel * EXPERTS_PER_GROUP + esel
    gates = jnp.einsum('nk,nke->ne', weights,
                       jax.nn.one_hot(expert_id, N_EXPERTS, dtype=f32)).astype(xt.dtype)
    y = jnp.zeros_like(xt)
    for e in range(N_EXPERTS):
        he = jax.nn.silu(xt @ w_gate[e]) * (xt @ w_up[e])
        y = y + gates[:, e:e + 1] * (he @ w_down[e])
    return y.reshape(bsz, seq, d)


def setup_inputs(seed: int = 0) -> dict:
    key = jax.random.key(seed)
    ks = jax.random.split(key, 20)
    f32 = jnp.float32
    x = jax.random.normal(ks[0], (BATCH, SEQ, D_MODEL), f32)
    col_scale = np.ones((D_IN_PROJ,), np.float32)
    col_scale[2 * D_DN:3 * D_DN] = DEEPNORM_BETA
    col_scale[D_IN_PROJ - D_MOBA:] = DEEPNORM_BETA
    w_in = (jax.random.normal(ks[1], (DEPTH, D_MODEL, D_IN_PROJ), f32) * D_MODEL ** -0.5
            * jnp.asarray(col_scale))
    conv_w = jax.random.normal(ks[2], (DEPTH, CONV_K, 3 * D_DN), f32) * CONV_K ** -0.5
    a_log = jnp.log(jax.random.uniform(ks[3], (DEPTH, N_HEADS_DN), f32, minval=1.0, maxval=16.0))
    dt_bias = 1.0 + 0.1 * jax.random.normal(ks[4], (DEPTH, N_HEADS_DN), f32)
    dn_norm_w = 1.0 + 0.02 * jax.random.normal(ks[5], (DEPTH, HEAD_DIM), f32)
    w_out = jax.random.normal(ks[6], (DEPTH, D_MIX, D_MODEL), f32) * D_MIX ** -0.5 * DEEPNORM_BETA
    ln1_g = 1.0 + 0.02 * jax.random.normal(ks[7], (DEPTH, D_MODEL), f32)
    ln1_b = 0.02 * jax.random.normal(ks[8], (DEPTH, D_MODEL), f32)
    router_w1 = jax.random.normal(ks[9], (DEPTH, D_MODEL, N_GROUPS), f32) * D_MODEL ** -0.5
    router_b1 = 0.01 * jax.random.normal(ks[10], (DEPTH, N_GROUPS), f32)
    router_w2 = (jax.random.normal(ks[11], (DEPTH, N_GROUPS, D_MODEL, EXPERTS_PER_GROUP), f32)
                 * D_MODEL ** -0.5)
    router_b2 = 0.01 * jax.random.normal(ks[12], (DEPTH, N_GROUPS, EXPERTS_PER_GROUP), f32)
    expert_w_gate = (jax.random.normal(ks[13], (DEPTH, N_EXPERTS, D_MODEL, D_EXPERT), f32)
                     * D_MODEL ** -0.5)
    expert_w_up = (jax.random.normal(ks[14], (DEPTH, N_EXPERTS, D_MODEL, D_EXPERT), f32)
                   * D_MODEL ** -0.5 * DEEPNORM_BETA)
    expert_w_down = (jax.random.normal(ks[15], (DEPTH, N_EXPERTS, D_EXPERT, D_MODEL), f32)
                     * D_EXPERT ** -0.5 * DEEPNORM_BETA)
    ln2_g = 1.0 + 0.02 * jax.random.normal(ks[16], (DEPTH, D_MODEL), f32)
    ln2_b = 0.02 * jax.random.normal(ks[17], (DEPTH, D_MODEL), f32)
    return {'x': x, 'w_in': w_in, 'conv_w': conv_w, 'a_log': a_log, 'dt_bias': dt_bias,
            'dn_norm_w': dn_norm_w, 'w_out': w_out, 'ln1_g': ln1_g, 'ln1_b': ln1_b,
            'router_w1': router_w1, 'router_b1': router_b1, 'router_w2': router_w2,
            'router_b2': router_b2, 'expert_w_gate': expert_w_gate, 'expert_w_up': expert_w_up,
            'expert_w_down': expert_w_down, 'ln2_g': ln2_g, 'ln2_b': ln2_b}


def reference(x, w_in, conv_w, a_log, dt_bias, dn_norm_w, w_out, ln1_g, ln1_b,
              router_w1, router_b1, router_w2, router_b2, expert_w_gate, expert_w_up,
              expert_w_down, ln2_g, ln2_b):
    for l in range(DEPTH):
        proj = jnp.einsum('btd,dc->btc', x, w_in[l])
        q_dn, k_dn, v_dn, z_dn, b_dn, a_dn, q_mb, k_mb, v_mb = jnp.split(proj, IN_PROJ_SPLITS, axis=-1)
        y_dn = gated_deltanet(q_dn, k_dn, v_dn, z_dn, b_dn, a_dn, conv_w[l], a_log[l],
                              dt_bias[l], dn_norm_w[l])
        y_mb = moba_attention(q_mb, k_mb, v_mb)
        mix = jnp.einsum('btc,cd->btd', jnp.concatenate([y_dn, y_mb], axis=-1), w_out[l])
        h = layer_norm(DEEPNORM_ALPHA * x + mix, ln1_g[l], ln1_b[l])
        ffn = hierarchical_moe(h, router_w1[l], router_b1[l], router_w2[l], router_b2[l],
                               expert_w_gate[l], expert_w_up[l], expert_w_down[l])
        x = layer_norm(DEEPNORM_ALPHA * h + ffn, ln2_g[l], ln2_b[l])
    return x
```

```python
import functools

import jax
import jax.numpy as jnp
from jax import lax
from jax.experimental import pallas as pl
from jax.experimental.pallas import tpu as pltpu

F32 = jnp.float32
BF16 = jnp.bfloat16

HEAD_DIM = 128
N_HEADS_DN = 4
N_HEADS_MOBA = 4
D_DN = N_HEADS_DN * HEAD_DIM
D_MOBA = N_HEADS_MOBA * HEAD_DIM
CONV_K = 4
DN_CHUNK = 64
MOBA_BLOCK = 256
MOBA_TOPK = 3
ROPE_THETA = 10000.0
N_GROUPS = 4
EXPERTS_PER_GROUP = 4
N_EXPERTS = N_GROUPS * EXPERTS_PER_GROUP
D_EXPERT = 256
LN_EPS = 1e-5
RMS_EPS = 1e-6
L2_EPS = 1e-6
NEG_INF = -1e30
DEPTH = 1
DEEPNORM_ALPHA = (2 * DEPTH) ** 0.25

LANES = 128
DN_TILE = 256
GATE_LANE0 = N_GROUPS
VMEM_LIMIT = 48 * 1024 * 1024


def _dot(a, b):
    return jnp.dot(a.astype(BF16), b.astype(BF16), preferred_element_type=F32)


def _dot_nt(a, b):
    return lax.dot_general(a.astype(BF16), b.astype(BF16), (((1,), (1,)), ((), ())),
                           preferred_element_type=F32)


def _split2(a):
    hi = a.astype(BF16)
    lo = (a - hi.astype(F32)).astype(BF16)
    return hi, lo


def _split3(a):
    hi = a.astype(BF16)
    r = a - hi.astype(F32)
    mid = r.astype(BF16)
    lo = (r - mid.astype(F32)).astype(BF16)
    return hi, mid, lo


def _dot3(a, b):
    ah, al = _split2(a)
    bh, bl = _split2(b)
    return (jnp.dot(ah, bh, preferred_element_type=F32) + jnp.dot(ah, bl, preferred_element_type=F32)
            + jnp.dot(al, bh, preferred_element_type=F32))


def _dot3_nt(a, b):
    ah, al = _split2(a)
    bh, bl = _split2(b)
    dn = (((1,), (1,)), ((), ()))
    return (lax.dot_general(ah, bh, dn, preferred_element_type=F32)
            + lax.dot_general(ah, bl, dn, preferred_element_type=F32)
            + lax.dot_general(al, bh, dn, preferred_element_type=F32))


def _dot_exact_lhs(a_bf16, b):
    bh, bm, bl = _split3(b)
    return (jnp.dot(a_bf16, bh, preferred_element_type=F32) + jnp.dot(a_bf16, bm, preferred_element_type=F32)
            + jnp.dot(a_bf16, bl, preferred_element_type=F32))


def _silu(x):
    return x * jax.nn.sigmoid(x)


def _softplus(x):
    return jnp.maximum(x, 0.0) + jnp.log1p(jnp.exp(-jnp.abs(x)))


def _layer_norm(t, g, b):
    mu = jnp.mean(t, axis=-1, keepdims=True)
    d = t - mu
    var = jnp.mean(d * d, axis=-1, keepdims=True)
    return d * lax.rsqrt(var + LN_EPS) * g + b


def _lane_pick(x, lane):
    ids = lax.broadcasted_iota(jnp.int32, x.shape, 1)
    return jnp.sum(jnp.where(ids == lane, x, 0.0), axis=1, keepdims=True)


def _in_proj_kernel(x_ref, w_ref, dn_ref, z_ref, ba_ref, mb_ref):
    xb = x_ref[...].astype(BF16)
    o0 = 3 * D_DN
    o1 = o0 + D_DN
    o2 = o1 + LANES
    dn_ref[...] = jnp.dot(xb, w_ref[:, 0:o0], preferred_element_type=F32)
    z_ref[...] = jnp.dot(xb, w_ref[:, o0:o1], preferred_element_type=F32)
    ba_ref[...] = jnp.dot(xb, w_ref[:, o1:o2], preferred_element_type=F32)
    mb_ref[...] = jnp.dot(xb, w_ref[:, o2:o2 + 3 * D_MOBA], preferred_element_type=F32)


def _in_proj(x2, w_all, tm):
    n, d = x2.shape
    wc = w_all.shape[1]
    return pl.pallas_call(
        _in_proj_kernel,
        out_shape=(jax.ShapeDtypeStruct((n, 3 * D_DN), F32), jax.ShapeDtypeStruct((n, D_DN), F32),
                   jax.ShapeDtypeStruct((n, LANES), F32), jax.ShapeDtypeStruct((n, 3 * D_MOBA), F32)),
        grid=(n // tm,),
        in_specs=[pl.BlockSpec((tm, d), lambda i: (i, 0)), pl.BlockSpec((d, wc), lambda i: (0, 0))],
        out_specs=(pl.BlockSpec((tm, 3 * D_DN), lambda i: (i, 0)), pl.BlockSpec((tm, D_DN), lambda i: (i, 0)),
                   pl.BlockSpec((tm, LANES), lambda i: (i, 0)), pl.BlockSpec((tm, 3 * D_MOBA), lambda i: (i, 0))),
        compiler_params=pltpu.CompilerParams(dimension_semantics=("parallel",), vmem_limit_bytes=VMEM_LIMIT),
        name="in_proj",
    )(x2, w_all)


def _inv_unit_lower(a, row, col):
    eye = (row == col).astype(F32)
    a8 = jnp.where((row >> 3) == (col >> 3), a, 0.0)
    a8_2 = _dot3(a8, a8)
    a8_4 = _dot3(a8_2, a8_2)
    x = _dot3(eye - a8, eye + a8_2)
    x = _dot3(x, eye + a8_4)
    s = 8
    while s < DN_CHUNK:
        sh = s.bit_length() - 1
        off = ((row >> (sh + 1)) == (col >> (sh + 1))) & ((row >> sh) != (col >> sh))
        y = _dot3(jnp.where(off, a, 0.0), x)
        x = x - _dot3(x, y)
        s *= 2
    return x


def _deltanet_kernel(q_ref, k_ref, v_ref, z_ref, ba_ref, cwq_ref, cwk_ref, cwv_ref, alog_ref, dtb_ref,
                     normw_ref, y_ref, cbq, cbk, cbv, s_ref):
    h = pl.program_id(1)
    t = pl.program_id(2)
    tt = DN_TILE
    nchunk = tt // DN_CHUNK

    @pl.when(t == 0)
    def _():
        zero8 = jnp.zeros((8, HEAD_DIM), F32)
        cbq[0:8, :] = zero8
        cbk[0:8, :] = zero8
        cbv[0:8, :] = zero8
        s_ref[...] = jnp.zeros_like(s_ref)

    def conv_silu(u_ref, cb, cw_ref):
        u = u_ref[0]
        cb[8:8 + tt, :] = u
        acc = cw_ref[CONV_K - 1:CONV_K, :] * u
        for s in range(1, CONV_K):
            acc = acc + cw_ref[CONV_K - 1 - s:CONV_K - s, :] * cb[8 - s:8 - s + tt, :]
        cb[0:8, :] = u[tt - 8:tt, :]
        return _silu(acc)

    q = conv_silu(q_ref, cbq, cwq_ref)
    k = conv_silu(k_ref, cbk, cwk_ref)
    v = conv_silu(v_ref, cbv, cwv_ref)
    q = q * lax.rsqrt(jnp.sum(q * q, axis=-1, keepdims=True) + L2_EPS) * (HEAD_DIM ** -0.5)
    k = k * lax.rsqrt(jnp.sum(k * k, axis=-1, keepdims=True) + L2_EPS)

    ba = ba_ref[0]
    beta = _lane_pick(jax.nn.sigmoid(ba), h)
    g_all = -jnp.exp(alog_ref[...]) * _softplus(ba + dtb_ref[...])

    row = lax.broadcasted_iota(jnp.int32, (tt, tt), 0)
    col = lax.broadcasted_iota(jnp.int32, (tt, tt), 1)
    same = (row >> 6) == (col >> 6)
    incl = same & (row >= col)
    strict = same & (row > col)

    gc_all = _dot_exact_lhs(incl.astype(BF16), g_all)
    gcc = _lane_pick(gc_all, h + N_HEADS_DN)
    gct = jnp.transpose(gc_all)
    sub = lax.broadcasted_iota(jnp.int32, gct.shape, 0)
    gcr = jnp.sum(jnp.where(sub == h + N_HEADS_DN, gct, 0.0), axis=0, keepdims=True)

    diff = gcc - gcr
    decay = jnp.where(incl, jnp.exp(jnp.where(incl, diff, 0.0)), 0.0)

    kb = k * beta
    vb = v * beta
    a_mat = jnp.where(strict, _dot_nt(kb, k) * decay, 0.0)
    qk = _dot_nt(q, k) * decay
    tinv = _inv_unit_lower(a_mat, row, col)

    eg = jnp.exp(gcc)
    wu = _dot(tinv, jnp.concatenate([kb * eg, vb], axis=1))
    qd = q * eg

    gl_rows = [gcc[(c + 1) * DN_CHUNK - 1:(c + 1) * DN_CHUNK, :] for c in range(nchunk)]
    gl_col = jnp.concatenate([jnp.broadcast_to(g, (DN_CHUNK, 1)) for g in gl_rows], axis=0)
    kdt = jnp.transpose(k * jnp.exp(gl_col - gcc))

    state = s_ref[...]
    outs = []
    for c in range(nchunk):
        lo, hi = c * DN_CHUNK, (c + 1) * DN_CHUNK
        wq = jnp.concatenate([wu[lo:hi, 0:HEAD_DIM], qd[lo:hi, :]], axis=0)
        r = _dot(wq, state)
        v_new = wu[lo:hi, HEAD_DIM:2 * HEAD_DIM] - r[0:DN_CHUNK, :]
        parts = []
        if lo > 0:
            parts.append(jnp.zeros((lo, HEAD_DIM), F32))
        parts.append(v_new)
        if hi < tt:
            parts.append(jnp.zeros((tt - hi, HEAD_DIM), F32))
        vz = jnp.concatenate(parts, axis=0).astype(BF16)
        outs.append(r[DN_CHUNK:2 * DN_CHUNK, :] + _dot(qk[lo:hi, :], vz))
        state = state * jnp.exp(gl_rows[c]) + _dot(kdt, vz)
    s_ref[...] = state

    o = jnp.concatenate(outs, axis=0)
    o = o * lax.rsqrt(jnp.mean(o * o, axis=-1, keepdims=True) + RMS_EPS) * normw_ref[...]
    y_ref[0] = (o * _silu(z_ref[0])).astype(y_ref.dtype)


def _deltanet(dn_qkv, z, ba, conv_w, alog_row, dtb_row, normw_row):
    bsz, seq, _ = dn_qkv.shape
    tt = DN_TILE
    nh = N_HEADS_DN

    def col_spec(off):
        return pl.BlockSpec((1, tt, HEAD_DIM), lambda b, h, t: (b, t, h + off))

    def cw_spec(off):
        return pl.BlockSpec((CONV_K, HEAD_DIM), lambda b, h, t: (0, h + off))

    row_spec = pl.BlockSpec((1, LANES), lambda b, h, t: (0, 0))
    return pl.pallas_call(
        _deltanet_kernel,
        out_shape=jax.ShapeDtypeStruct((bsz, seq, D_DN), BF16),
        grid=(bsz, nh, seq // tt),
        in_specs=[col_spec(0), col_spec(nh), col_spec(2 * nh), col_spec(0),
                  pl.BlockSpec((1, tt, LANES), lambda b, h, t: (b, t, 0)),
                  cw_spec(0), cw_spec(nh), cw_spec(2 * nh), row_spec, row_spec, row_spec],
        out_specs=col_spec(0),
        scratch_shapes=[pltpu.VMEM((8 + tt, HEAD_DIM), F32), pltpu.VMEM((8 + tt, HEAD_DIM), F32),
                        pltpu.VMEM((8 + tt, HEAD_DIM), F32), pltpu.VMEM((HEAD_DIM, HEAD_DIM), F32)],
        compiler_params=pltpu.CompilerParams(dimension_semantics=("parallel", "parallel", "arbitrary"),
                                             vmem_limit_bytes=VMEM_LIMIT),
        name="deltanet",
    )(dn_qkv, dn_qkv, dn_qkv, z, ba, conv_w, conv_w, conv_w, alog_row, dtb_row, normw_row)


def _moba_prep_kernel(x_ref, cos_ref, sin_ref, q_ref, k_ref, vt_ref, sel_ref, km_ref, *, nb, topk):
    j = pl.program_id(1)

    @pl.when(j == 0)
    def _():
        km_ref[...] = jnp.zeros_like(km_ref)

    cos = cos_ref[...]
    sin = sin_ref[...]
    half = HEAD_DIM // 2
    blk = lax.broadcasted_iota(jnp.int32, (nb, MOBA_BLOCK), 0)
    kmeans = []
    for h in range(N_HEADS_MOBA):
        qh = x_ref[0, :, h * HEAD_DIM:(h + 1) * HEAD_DIM]
        kh = x_ref[0, :, D_MOBA + h * HEAD_DIM:D_MOBA + (h + 1) * HEAD_DIM]
        qr = (qh * cos + pltpu.roll(qh, half, 1) * sin) * (HEAD_DIM ** -0.5)
        kr = kh * cos + pltpu.roll(kh, half, 1) * sin
        q_ref[0, :, h * HEAD_DIM:(h + 1) * HEAD_DIM] = qr.astype(q_ref.dtype)
        k_ref[0, :, h * HEAD_DIM:(h + 1) * HEAD_DIM] = kr.astype(k_ref.dtype)
        kmeans.append(jnp.mean(kr, axis=0, keepdims=True))

        gate = _dot3_nt(km_ref[:, h * HEAD_DIM:(h + 1) * HEAD_DIM], qr)
        gate = jnp.where(blk < j, gate, NEG_INF)
        rank = jnp.zeros(gate.shape, F32)
        for m in range(nb):
            gm = gate[m:m + 1, :]
            ahead = (gm > gate) | ((gm == gate) & (blk > m))
            rank = rank + jnp.where(ahead, 1.0, 0.0)
        sel = (blk < j) & (rank < topk)
        sel_ref[0, 0, h * nb:(h + 1) * nb, :] = jnp.where(sel, 1.0, 0.0)

    km_ref[pl.ds(j, 1), :] = jnp.concatenate(kmeans, axis=1)
    vt_ref[0, 0] = jnp.transpose(x_ref[0, :, 2 * D_MOBA:3 * D_MOBA]).astype(vt_ref.dtype)


def _moba_prep(mb_qkv, cos_t, sin_t):
    bsz, seq, _ = mb_qkv.shape
    nb = seq // MOBA_BLOCK
    topk = min(MOBA_TOPK, nb)
    kern = functools.partial(_moba_prep_kernel, nb=nb, topk=topk)
    tok_spec = pl.BlockSpec((1, MOBA_BLOCK, D_MOBA), lambda b, j: (b, j, 0))
    tab_spec = pl.BlockSpec((MOBA_BLOCK, HEAD_DIM), lambda b, j: (j, 0))
    return pl.pallas_call(
        kern,
        out_shape=(jax.ShapeDtypeStruct((bsz, seq, D_MOBA), BF16), jax.ShapeDtypeStruct((bsz, seq, D_MOBA), BF16),
                   jax.ShapeDtypeStruct((bsz, nb, D_MOBA, MOBA_BLOCK), BF16),
                   jax.ShapeDtypeStruct((bsz, nb, N_HEADS_MOBA * nb, MOBA_BLOCK), F32)),
        grid=(bsz, nb),
        in_specs=[pl.BlockSpec((1, MOBA_BLOCK, 3 * D_MOBA), lambda b, j: (b, j, 0)), tab_spec, tab_spec],
        out_specs=(tok_spec, tok_spec,
                   pl.BlockSpec((1, 1, D_MOBA, MOBA_BLOCK), lambda b, j: (b, j, 0, 0)),
                   pl.BlockSpec((1, 1, N_HEADS_MOBA * nb, MOBA_BLOCK), lambda b, j: (b, j, 0, 0))),
        scratch_shapes=[pltpu.VMEM((nb, D_MOBA), F32)],
        compiler_params=pltpu.CompilerParams(dimension_semantics=("parallel", "arbitrary"),
                                             vmem_limit_bytes=VMEM_LIMIT),
        name="moba_prep",
    )(mb_qkv, cos_t, sin_t)


def _moba_attn_kernel(q_ref, k_ref, vt_ref, sel_ref, o_ref):
    j = pl.program_id(2)
    q = q_ref[0]
    dn = (((1,), (1,)), ((), ()))
    ki = lax.broadcasted_iota(jnp.int32, (MOBA_BLOCK, MOBA_BLOCK), 0)
    qi = lax.broadcasted_iota(jnp.int32, (MOBA_BLOCK, MOBA_BLOCK), 1)

    s = lax.dot_general(k_ref[0, j], q, dn, preferred_element_type=F32)
    s = jnp.where(ki <= qi, s, NEG_INF)
    m = jnp.max(s, axis=0, keepdims=True)
    p = jnp.exp(s - m)
    l = jnp.sum(p, axis=0, keepdims=True)
    acc = jnp.dot(vt_ref[0, j], p.astype(BF16), preferred_element_type=F32)

    def body(n, carry):
        m, l, acc = carry
        s = lax.dot_general(k_ref[0, n], q, dn, preferred_element_type=F32)
        s = jnp.where(sel_ref[0, 0, pl.ds(n, 1), :] > 0.5, s, NEG_INF)
        m_new = jnp.maximum(m, jnp.max(s, axis=0, keepdims=True))
        alpha = jnp.exp(m - m_new)
        p = jnp.exp(s - m_new)
        l = alpha * l + jnp.sum(p, axis=0, keepdims=True)
        acc = acc * alpha + jnp.dot(vt_ref[0, n], p.astype(BF16), preferred_element_type=F32)
        return m_new, l, acc

    m, l, acc = lax.fori_loop(0, j, body, (m, l, acc))
    o_ref[0] = jnp.transpose(acc / l).astype(o_ref.dtype)


def _moba_attn(q_r, k_r, vt, sel):
    bsz, seq, _ = q_r.shape
    nb = seq // MOBA_BLOCK
    k4 = k_r.reshape(bsz, nb, MOBA_BLOCK, D_MOBA)
    tok_spec = pl.BlockSpec((1, MOBA_BLOCK, HEAD_DIM), lambda b, h, j: (b, j, h))
    return pl.pallas_call(
        _moba_attn_kernel,
        out_shape=jax.ShapeDtypeStruct((bsz, seq, D_MOBA), BF16),
        grid=(bsz, N_HEADS_MOBA, nb),
        in_specs=[tok_spec,
                  pl.BlockSpec((1, nb, MOBA_BLOCK, HEAD_DIM), lambda b, h, j: (b, 0, 0, h)),
                  pl.BlockSpec((1, nb, HEAD_DIM, MOBA_BLOCK), lambda b, h, j: (b, 0, h, 0)),
                  pl.BlockSpec((1, 1, nb, MOBA_BLOCK), lambda b, h, j: (b, j, h, 0))],
        out_specs=tok_spec,
        compiler_params=pltpu.CompilerParams(dimension_semantics=("parallel", "parallel", "arbitrary"),
                                             vmem_limit_bytes=VMEM_LIMIT),
        name="moba_attn",
    )(q_r, k4, vt, sel)


def _mix_route_kernel(ydn_ref, ymb_ref, x_ref, wo_ref, g_ref, b_ref, rw_ref, rb_ref, h_ref, hb_ref, gate_ref):
    mix = (jnp.dot(ydn_ref[...], wo_ref[0:D_DN, :], preferred_element_type=F32)
           + jnp.dot(ymb_ref[...], wo_ref[D_DN:D_DN + D_MOBA, :], preferred_element_type=F32))
    hval = _layer_norm(DEEPNORM_ALPHA * x_ref[...] + mix, g_ref[...], b_ref[...])
    h_ref[...] = hval
    hb_ref[...] = hval.astype(BF16)

    logits = _dot3(hval, rw_ref[...]) + rb_ref[...]
    lane = lax.broadcasted_iota(jnp.int32, logits.shape, 1)
    big = jnp.int32(LANES)

    def first_lane(mask):
        return jnp.min(jnp.where(mask, lane, big), axis=1, keepdims=True)

    is_g = lane < N_GROUPS
    m1 = jnp.max(jnp.where(is_g, logits, NEG_INF), axis=1, keepdims=True)
    s1 = jnp.sum(jnp.where(is_g, jnp.exp(logits - m1), 0.0), axis=1, keepdims=True)
    pg = 1.0 / s1
    gsel = first_lane(is_g & (logits == m1))

    in_grp = (lane >= GATE_LANE0) & (((lane - GATE_LANE0) >> 2) == gsel) & (lane < GATE_LANE0 + N_EXPERTS)
    m2 = jnp.max(jnp.where(in_grp, logits, NEG_INF), axis=1, keepdims=True)
    s2 = jnp.sum(jnp.where(in_grp, jnp.exp(logits - m2), 0.0), axis=1, keepdims=True)
    e1 = first_lane(in_grp & (logits == m2))
    rest = in_grp & (lane != e1)
    m2b = jnp.max(jnp.where(rest, logits, NEG_INF), axis=1, keepdims=True)
    e2 = first_lane(rest & (logits == m2b))
    pe1 = 1.0 / s2
    pe2 = jnp.exp(m2b - m2) / s2
    tot = pe1 + pe2
    gate_ref[...] = jnp.where(lane == e1, pg * (pe1 / tot), jnp.where(lane == e2, pg * (pe2 / tot), 0.0))


def _mix_route(y_dn, y_mb, x2, wo, g1, b1, rw, rb, tm):
    n, d = x2.shape
    row = lambda w: pl.BlockSpec((1, w), lambda i: (0, 0))
    return pl.pallas_call(
        _mix_route_kernel,
        out_shape=(jax.ShapeDtypeStruct((n, d), F32), jax.ShapeDtypeStruct((n, d), BF16),
                   jax.ShapeDtypeStruct((n, LANES), F32)),
        grid=(n // tm,),
        in_specs=[pl.BlockSpec((tm, D_DN), lambda i: (i, 0)), pl.BlockSpec((tm, D_MOBA), lambda i: (i, 0)),
                  pl.BlockSpec((tm, d), lambda i: (i, 0)), pl.BlockSpec((D_DN + D_MOBA, d), lambda i: (0, 0)),
                  row(d), row(d), pl.BlockSpec((d, LANES), lambda i: (0, 0)), row(LANES)],
        out_specs=(pl.BlockSpec((tm, d), lambda i: (i, 0)), pl.BlockSpec((tm, d), lambda i: (i, 0)),
                   pl.BlockSpec((tm, LANES), lambda i: (i, 0))),
        compiler_params=pltpu.CompilerParams(dimension_semantics=("parallel",), vmem_limit_bytes=VMEM_LIMIT),
        name="mix_route",
    )(y_dn, y_mb, x2, wo, g1, b1, rw, rb)


def _moe_kernel(hb_ref, h_ref, gate_ref, wg_ref, wu_ref, wd_ref, g_ref, b_ref, o_ref, acc_ref):
    e = pl.program_id(1)

    @pl.when(e == 0)
    def _():
        acc_ref[...] = jnp.zeros_like(acc_ref)

    hb = hb_ref[...]
    gcol = _lane_pick(gate_ref[...], e + GATE_LANE0)
    he = _silu(jnp.dot(hb, wg_ref[0], preferred_element_type=F32)) * jnp.dot(hb, wu_ref[0],
                                                                             preferred_element_type=F32)
    acc_ref[...] += jnp.dot((he * gcol).astype(BF16), wd_ref[0], preferred_element_type=F32)

    @pl.when(e == pl.num_programs(1) - 1)
    def _():
        o_ref[...] = _layer_norm(DEEPNORM_ALPHA * h_ref[...] + acc_ref[...], g_ref[...], b_ref[...])


def _moe(hb, hf, gates, wg, wu, wd, g2, b2, tm):
    n, d = hf.shape
    row = pl.BlockSpec((1, d), lambda i, e: (0, 0))
    return pl.pallas_call(
        _moe_kernel,
        out_shape=jax.ShapeDtypeStruct((n, d), F32),
        grid=(n // tm, N_EXPERTS),
        in_specs=[pl.BlockSpec((tm, d), lambda i, e: (i, 0)), pl.BlockSpec((tm, d), lambda i, e: (i, 0)),
                  pl.BlockSpec((tm, LANES), lambda i, e: (i, 0)),
                  pl.BlockSpec((1, d, D_EXPERT), lambda i, e: (e, 0, 0)),
                  pl.BlockSpec((1, d, D_EXPERT), lambda i, e: (e, 0, 0)),
                  pl.BlockSpec((1, D_EXPERT, d), lambda i, e: (e, 0, 0)), row, row],
        out_specs=pl.BlockSpec((tm, d), lambda i, e: (i, 0)),
        scratch_shapes=[pltpu.VMEM((tm, d), F32)],
        compiler_params=pltpu.CompilerParams(dimension_semantics=("parallel", "arbitrary"),
                                             vmem_limit_bytes=VMEM_LIMIT),
        name="moe",
    )(hb, hf, gates, wg, wu, wd, g2, b2)


def _pad_lanes(a, lane0=0):
    return jnp.zeros((1, LANES), F32).at[0, lane0:lane0 + a.shape[0]].set(a.astype(F32))


def _rope_tables(seq):
    half = HEAD_DIM // 2
    inv_freq = ROPE_THETA ** (-jnp.arange(half, dtype=F32) / half)
    ang = jnp.arange(seq).astype(F32)[:, None] * inv_freq[None, :]
    cos, sin = jnp.cos(ang), jnp.sin(ang)
    return jnp.concatenate([cos, cos], axis=-1), jnp.concatenate([-sin, sin], axis=-1)


def _layer(x, w_in, conv_w, a_log, dt_bias, dn_norm_w, w_out, ln1_g, ln1_b, router_w1, router_b1,
           router_w2, router_b2, w_gate, w_up, w_down, ln2_g, ln2_b):
    bsz, seq, d = x.shape
    n = bsz * seq
    x2 = x.reshape(n, d)

    o_z, o_b, o_mb = 3 * D_DN, 4 * D_DN, 4 * D_DN + 2 * N_HEADS_DN
    w_ba = jnp.pad(w_in[:, o_b:o_mb], ((0, 0), (0, LANES - 2 * N_HEADS_DN)))
    w_all = jnp.concatenate([w_in[:, :o_z], w_in[:, o_z:o_b], w_ba, w_in[:, o_mb:]], axis=1).astype(BF16)

    tm = min(512, n)
    dn_qkv, z, ba, mb_qkv = _in_proj(x2, w_all, tm)

    y_dn = _deltanet(dn_qkv.reshape(bsz, seq, 3 * D_DN), z.reshape(bsz, seq, D_DN), ba.reshape(bsz, seq, LANES),
                     conv_w, _pad_lanes(a_log, N_HEADS_DN), _pad_lanes(dt_bias, N_HEADS_DN),
                     dn_norm_w.astype(F32).reshape(1, HEAD_DIM))

    cos_t, sin_t = _rope_tables(seq)
    q_r, k_r, vt, sel = _moba_prep(mb_qkv.reshape(bsz, seq, 3 * D_MOBA), cos_t, sin_t)
    y_mb = _moba_attn(q_r, k_r, vt, sel)

    rw = jnp.concatenate([router_w1, jnp.transpose(router_w2, (1, 0, 2)).reshape(d, N_EXPERTS)], axis=1)
    rw = jnp.pad(rw, ((0, 0), (0, LANES - rw.shape[1])))
    rb = _pad_lanes(jnp.concatenate([router_b1, router_b2.reshape(-1)]))
    hf, hb, gates = _mix_route(y_dn.reshape(n, D_DN), y_mb.reshape(n, D_MOBA), x2, w_out.astype(BF16),
                               ln1_g.reshape(1, d), ln1_b.reshape(1, d), rw, rb, tm)

    out = _moe(hb, hf, gates, w_gate.astype(BF16), w_up.astype(BF16), w_down.astype(BF16),
               ln2_g.reshape(1, d), ln2_b.reshape(1, d), min(1024, n))
    return out.reshape(bsz, seq, d)


def kernel(x, w_in, conv_w, a_log, dt_bias, dn_norm_w, w_out, ln1_g, ln1_b, router_w1, router_b1, router_w2, router_b2, expert_w_gate, expert_w_up, expert_w_down, ln2_g, ln2_b):
    for l in range(DEPTH):
        x = _layer(x, w_in[l], conv_w[l], a_log[l], dt_bias[l], dn_norm_w[l], w_out[l], ln1_g[l], ln1_b[l],
                   router_w1[l], router_b1[l], router_w2[l], router_b2[l], expert_w_gate[l], expert_w_up[l],
                   expert_w_down[l], ln2_g[l], ln2_b[l])
    return x
```

```python
import functools

import jax
import jax.numpy as jnp
from jax import lax
from jax.experimental import pallas as pl
from jax.experimental.pallas import tpu as pltpu

F32 = jnp.float32
BF16 = jnp.bfloat16

HEAD_DIM = 128
N_HEADS_DN = 4
N_HEADS_MOBA = 4
D_DN = N_HEADS_DN * HEAD_DIM
D_MOBA = N_HEADS_MOBA * HEAD_DIM
CONV_K = 4
DN_CHUNK = 64
MOBA_BLOCK = 256
MOBA_TOPK = 3
ROPE_THETA = 10000.0
N_GROUPS = 4
EXPERTS_PER_GROUP = 4
N_EXPERTS = N_GROUPS * EXPERTS_PER_GROUP
D_EXPERT = 256
LN_EPS = 1e-5
RMS_EPS = 1e-6
L2_EPS = 1e-6
NEG_INF = -1e30
DEPTH = 1
DEEPNORM_ALPHA = (2 * DEPTH) ** 0.25

LANES = 128
DN_TILE = 256
DN_HEADS_PER_STEP = 4
GATE_LANE0 = N_GROUPS
VMEM_LIMIT = 48 * 1024 * 1024


def _dot(a, b):
    return jnp.dot(a.astype(BF16), b.astype(BF16), preferred_element_type=F32)


def _dot_nt(a, b):
    return lax.dot_general(a.astype(BF16), b.astype(BF16), (((1,), (1,)), ((), ())),
                           preferred_element_type=F32)


def _split2(a):
    hi = a.astype(BF16)
    lo = (a - hi.astype(F32)).astype(BF16)
    return hi, lo


def _split3(a):
    hi = a.astype(BF16)
    r = a - hi.astype(F32)
    mid = r.astype(BF16)
    lo = (r - mid.astype(F32)).astype(BF16)
    return hi, mid, lo


def _dot3(a, b):
    ah, al = _split2(a)
    bh, bl = _split2(b)
    return (jnp.dot(ah, bh, preferred_element_type=F32) + jnp.dot(ah, bl, preferred_element_type=F32)
            + jnp.dot(al, bh, preferred_element_type=F32))


def _dot3_nt(a, b):
    ah, al = _split2(a)
    bh, bl = _split2(b)
    dn = (((1,), (1,)), ((), ()))
    return (lax.dot_general(ah, bh, dn, preferred_element_type=F32)
            + lax.dot_general(ah, bl, dn, preferred_element_type=F32)
            + lax.dot_general(al, bh, dn, preferred_element_type=F32))


def _dot_exact_lhs(a_bf16, b):
    bh, bm, bl = _split3(b)
    return (jnp.dot(a_bf16, bh, preferred_element_type=F32) + jnp.dot(a_bf16, bm, preferred_element_type=F32)
            + jnp.dot(a_bf16, bl, preferred_element_type=F32))


def _silu(x):
    return x * jax.nn.sigmoid(x)


def _softplus(x):
    return jnp.maximum(x, 0.0) + jnp.log1p(jnp.exp(-jnp.abs(x)))


def _layer_norm(t, g, b):
    mu = jnp.mean(t, axis=-1, keepdims=True)
    d = t - mu
    var = jnp.mean(d * d, axis=-1, keepdims=True)
    return d * lax.rsqrt(var + LN_EPS) * g + b


def _lane_pick(x, lane):
    ids = lax.broadcasted_iota(jnp.int32, x.shape, 1)
    return jnp.sum(jnp.where(ids == lane, x, 0.0), axis=1, keepdims=True)


def _in_proj_kernel(x_ref, w_ref, dn_ref, z_ref, ba_ref, mb_ref):
    xb = x_ref[...].astype(BF16)
    o0 = 3 * D_DN
    o1 = o0 + D_DN
    o2 = o1 + LANES
    dn_ref[...] = jnp.dot(xb, w_ref[:, 0:o0], preferred_element_type=F32)
    z_ref[...] = jnp.dot(xb, w_ref[:, o0:o1], preferred_element_type=F32)
    ba_ref[...] = jnp.dot(xb, w_ref[:, o1:o2], preferred_element_type=F32)
    mb_ref[...] = jnp.dot(xb, w_ref[:, o2:o2 + 3 * D_MOBA], preferred_element_type=F32)


def _in_proj(x2, w_all, tm):
    n, d = x2.shape
    wc = w_all.shape[1]
    return pl.pallas_call(
        _in_proj_kernel,
        out_shape=(jax.ShapeDtypeStruct((n, 3 * D_DN), F32), jax.ShapeDtypeStruct((n, D_DN), F32),
                   jax.ShapeDtypeStruct((n, LANES), F32), jax.ShapeDtypeStruct((n, 3 * D_MOBA), F32)),
        grid=(n // tm,),
        in_specs=[pl.BlockSpec((tm, d), lambda i: (i, 0)), pl.BlockSpec((d, wc), lambda i: (0, 0))],
        out_specs=(pl.BlockSpec((tm, 3 * D_DN), lambda i: (i, 0)), pl.BlockSpec((tm, D_DN), lambda i: (i, 0)),
                   pl.BlockSpec((tm, LANES), lambda i: (i, 0)), pl.BlockSpec((tm, 3 * D_MOBA), lambda i: (i, 0))),
        compiler_params=pltpu.CompilerParams(dimension_semantics=("parallel",), vmem_limit_bytes=VMEM_LIMIT),
        name="in_proj",
    )(x2, w_all)


def _inv_unit_lower(a_list, row, col):
    eye = (row == col).astype(F32)
    d8 = (row >> 3) == (col >> 3)
    a8 = [jnp.where(d8, a, 0.0) for a in a_list]
    a8_2 = [_dot(a, a) for a in a8]
    a8_4 = [_dot(a, a) for a in a8_2]
    x = [_dot(eye - a, eye + a2) for a, a2 in zip(a8, a8_2)]
    x = [_dot(xi, eye + a4) for xi, a4 in zip(x, a8_4)]
    s = 8
    while s < DN_CHUNK:
        sh = s.bit_length() - 1
        off = ((row >> (sh + 1)) == (col >> (sh + 1))) & ((row >> sh) != (col >> sh))
        y = [_dot(jnp.where(off, a, 0.0), xi) for a, xi in zip(a_list, x)]
        x = [xi - _dot(xi, yi) for xi, yi in zip(x, y)]
        s *= 2
    return x


def _deltanet_heads(q, k, v, z, beta, gcc, gcr, state, normw, masks):
    row, col, incl, strict = masks
    nh = len(q)
    hs = range(nh)
    tt = q[0].shape[0]
    nchunk = tt // DN_CHUNK
    q = [x * lax.rsqrt(jnp.sum(x * x, axis=-1, keepdims=True) + L2_EPS) * (HEAD_DIM ** -0.5) for x in q]
    k = [x * lax.rsqrt(jnp.sum(x * x, axis=-1, keepdims=True) + L2_EPS) for x in k]

    decay = [jnp.where(incl, jnp.exp(jnp.where(incl, gcc[h] - gcr[h], 0.0)), 0.0) for h in hs]
    kb = [k[h] * beta[h] for h in hs]
    vb = [v[h] * beta[h] for h in hs]
    a_mat = [jnp.where(strict, _dot_nt(kb[h], k[h]) * decay[h], 0.0) for h in hs]
    qk = [_dot_nt(q[h], k[h]) * decay[h] for h in hs]
    tinv = _inv_unit_lower(a_mat, row, col)

    eg = [jnp.exp(g) for g in gcc]
    wu = [_dot(tinv[h], jnp.concatenate([kb[h] * eg[h], vb[h]], axis=1)) for h in hs]
    qd = [q[h] * eg[h] for h in hs]

    gl_rows = [[g[(c + 1) * DN_CHUNK - 1:(c + 1) * DN_CHUNK, :] for c in range(nchunk)] for g in gcc]
    gl_col = [jnp.concatenate([jnp.broadcast_to(g, (DN_CHUNK, 1)) for g in rows], axis=0) for rows in gl_rows]
    kdt = [jnp.transpose(k[h] * jnp.exp(gl_col[h] - gcc[h])) for h in hs]

    outs = [[] for _ in hs]
    for c in range(nchunk):
        lo, hi = c * DN_CHUNK, (c + 1) * DN_CHUNK
        r = [_dot(jnp.concatenate([wu[h][lo:hi, 0:HEAD_DIM], qd[h][lo:hi, :]], axis=0), state[h]) for h in hs]
        vz = []
        for h in hs:
            parts = []
            if lo > 0:
                parts.append(jnp.zeros((lo, HEAD_DIM), F32))
            parts.append(wu[h][lo:hi, HEAD_DIM:2 * HEAD_DIM] - r[h][0:DN_CHUNK, :])
            if hi < tt:
                parts.append(jnp.zeros((tt - hi, HEAD_DIM), F32))
            vz.append(jnp.concatenate(parts, axis=0).astype(BF16))
        for h in hs:
            outs[h].append(r[h][DN_CHUNK:2 * DN_CHUNK, :] + _dot(qk[h][lo:hi, :], vz[h]))
        state = [state[h] * jnp.exp(gl_rows[h][c]) + _dot(kdt[h], vz[h]) for h in hs]

    ys = []
    for h in hs:
        o = jnp.concatenate(outs[h], axis=0)
        o = o * lax.rsqrt(jnp.mean(o * o, axis=-1, keepdims=True) + RMS_EPS) * normw
        ys.append(o * _silu(z[h]))
    return ys, state


def _deltanet_kernel(q_ref, k_ref, v_ref, z_ref, ba_ref, cwq_ref, cwk_ref, cwv_ref, alog_ref, dtb_ref,
                     normw_ref, y_ref, cbq, cbk, cbv, s_ref, *, hb):
    hg = pl.program_id(1)
    t = pl.program_id(2)
    tt = DN_TILE

    @pl.when(t == 0)
    def _():
        zero8 = jnp.zeros((8, hb * HEAD_DIM), F32)
        cbq[0:8, :] = zero8
        cbk[0:8, :] = zero8
        cbv[0:8, :] = zero8
        s_ref[...] = jnp.zeros_like(s_ref)

    def conv_silu(u_ref, cb, cw_ref):
        u = u_ref[0]
        cb[8:8 + tt, :] = u
        acc = cw_ref[CONV_K - 1:CONV_K, :] * u
        for s in range(1, CONV_K):
            acc = acc + cw_ref[CONV_K - 1 - s:CONV_K - s, :] * cb[8 - s:8 - s + tt, :]
        cb[0:8, :] = u[tt - 8:tt, :]
        return _silu(acc)

    q_all = conv_silu(q_ref, cbq, cwq_ref)
    k_all = conv_silu(k_ref, cbk, cwk_ref)
    v_all = conv_silu(v_ref, cbv, cwv_ref)

    ba = ba_ref[0]
    beta_all = jax.nn.sigmoid(ba)
    g_all = -jnp.exp(alog_ref[...]) * _softplus(ba + dtb_ref[...])

    row = lax.broadcasted_iota(jnp.int32, (tt, tt), 0)
    col = lax.broadcasted_iota(jnp.int32, (tt, tt), 1)
    same = (row >> 6) == (col >> 6)
    incl = same & (row >= col)
    strict = same & (row > col)
    masks = (row, col, incl, strict)

    gc_all = _dot_exact_lhs(incl.astype(BF16), g_all)
    gct = jnp.transpose(gc_all)
    sub = lax.broadcasted_iota(jnp.int32, gct.shape, 0)

    sls = [slice(hh * HEAD_DIM, (hh + 1) * HEAD_DIM) for hh in range(hb)]
    heads = [hg * hb + hh for hh in range(hb)]
    beta = [_lane_pick(beta_all, h) for h in heads]
    gcc = [_lane_pick(gc_all, h + N_HEADS_DN) for h in heads]
    gcr = [jnp.sum(jnp.where(sub == h + N_HEADS_DN, gct, 0.0), axis=0, keepdims=True) for h in heads]
    ys, states = _deltanet_heads([q_all[:, sl] for sl in sls], [k_all[:, sl] for sl in sls],
                                 [v_all[:, sl] for sl in sls], [z_ref[0, :, sl] for sl in sls], beta, gcc, gcr,
                                 [s_ref[hh] for hh in range(hb)], normw_ref[...], masks)
    s_ref[...] = jnp.stack(states, axis=0)
    y_ref[0] = jnp.concatenate(ys, axis=1).astype(y_ref.dtype)


def _deltanet(dn_qkv, z, ba, conv_w, alog_row, dtb_row, normw_row):
    bsz, seq, _ = dn_qkv.shape
    tt = DN_TILE
    hb = DN_HEADS_PER_STEP
    ng = N_HEADS_DN // hb
    w = hb * HEAD_DIM

    def col_spec(off):
        return pl.BlockSpec((1, tt, w), lambda b, g, t: (b, t, g + off))

    def cw_spec(off):
        return pl.BlockSpec((CONV_K, w), lambda b, g, t: (0, g + off))

    row_spec = pl.BlockSpec((1, LANES), lambda b, g, t: (0, 0))
    return pl.pallas_call(
        functools.partial(_deltanet_kernel, hb=hb),
        out_shape=jax.ShapeDtypeStruct((bsz, seq, D_DN), BF16),
        grid=(bsz, ng, seq // tt),
        in_specs=[col_spec(0), col_spec(ng), col_spec(2 * ng), col_spec(0),
                  pl.BlockSpec((1, tt, LANES), lambda b, g, t: (b, t, 0)),
                  cw_spec(0), cw_spec(ng), cw_spec(2 * ng), row_spec, row_spec, row_spec],
        out_specs=col_spec(0),
        scratch_shapes=[pltpu.VMEM((8 + tt, w), F32), pltpu.VMEM((8 + tt, w), F32),
                        pltpu.VMEM((8 + tt, w), F32), pltpu.VMEM((hb, HEAD_DIM, HEAD_DIM), F32)],
        compiler_params=pltpu.CompilerParams(dimension_semantics=("parallel", "parallel", "arbitrary"),
                                             vmem_limit_bytes=VMEM_LIMIT),
        name="deltanet",
    )(dn_qkv, dn_qkv, dn_qkv, z, ba, conv_w, conv_w, conv_w, alog_row, dtb_row, normw_row)


def _moba_prep_kernel(x_ref, cos_ref, sin_ref, q_ref, k_ref, vt_ref, sel_ref, km_ref, *, nb, topk):
    j = pl.program_id(1)

    @pl.when(j == 0)
    def _():
        km_ref[...] = jnp.zeros_like(km_ref)

    cos = cos_ref[...]
    sin = sin_ref[...]
    half = HEAD_DIM // 2
    blk = lax.broadcasted_iota(jnp.int32, (nb, MOBA_BLOCK), 0)
    kmeans = []
    for h in range(N_HEADS_MOBA):
        qh = x_ref[0, :, h * HEAD_DIM:(h + 1) * HEAD_DIM]
        kh = x_ref[0, :, D_MOBA + h * HEAD_DIM:D_MOBA + (h + 1) * HEAD_DIM]
        qr = (qh * cos + pltpu.roll(qh, half, 1) * sin) * (HEAD_DIM ** -0.5)
        kr = kh * cos + pltpu.roll(kh, half, 1) * sin
        q_ref[0, :, h * HEAD_DIM:(h + 1) * HEAD_DIM] = qr.astype(q_ref.dtype)
        k_ref[0, :, h * HEAD_DIM:(h + 1) * HEAD_DIM] = kr.astype(k_ref.dtype)
        kmeans.append(jnp.mean(kr, axis=0, keepdims=True))

        gate = _dot3_nt(km_ref[:, h * HEAD_DIM:(h + 1) * HEAD_DIM], qr)
        gate = jnp.where(blk < j, gate, NEG_INF)
        rank = jnp.zeros(gate.shape, F32)
        for m in range(nb):
            gm = gate[m:m + 1, :]
            ahead = (gm > gate) | ((gm == gate) & (blk > m))
            rank = rank + jnp.where(ahead, 1.0, 0.0)
        sel = (blk < j) & (rank < topk)
        sel_ref[0, 0, h * nb:(h + 1) * nb, :] = jnp.where(sel, 1.0, 0.0)

    km_ref[pl.ds(j, 1), :] = jnp.concatenate(kmeans, axis=1)
    vt_ref[0, 0] = jnp.transpose(x_ref[0, :, 2 * D_MOBA:3 * D_MOBA]).astype(vt_ref.dtype)


def _moba_prep(mb_qkv, cos_t, sin_t):
    bsz, seq, _ = mb_qkv.shape
    nb = seq // MOBA_BLOCK
    topk = min(MOBA_TOPK, nb)
    kern = functools.partial(_moba_prep_kernel, nb=nb, topk=topk)
    tok_spec = pl.BlockSpec((1, MOBA_BLOCK, D_MOBA), lambda b, j: (b, j, 0))
    tab_spec = pl.BlockSpec((MOBA_BLOCK, HEAD_DIM), lambda b, j: (j, 0))
    return pl.pallas_call(
        kern,
        out_shape=(jax.ShapeDtypeStruct((bsz, seq, D_MOBA), BF16), jax.ShapeDtypeStruct((bsz, seq, D_MOBA), BF16),
                   jax.ShapeDtypeStruct((bsz, nb, D_MOBA, MOBA_BLOCK), BF16),
                   jax.ShapeDtypeStruct((bsz, nb, N_HEADS_MOBA * nb, MOBA_BLOCK), F32)),
        grid=(bsz, nb),
        in_specs=[pl.BlockSpec((1, MOBA_BLOCK, 3 * D_MOBA), lambda b, j: (b, j, 0)), tab_spec, tab_spec],
        out_specs=(tok_spec, tok_spec,
                   pl.BlockSpec((1, 1, D_MOBA, MOBA_BLOCK), lambda b, j: (b, j, 0, 0)),
                   pl.BlockSpec((1, 1, N_HEADS_MOBA * nb, MOBA_BLOCK), lambda b, j: (b, j, 0, 0))),
        scratch_shapes=[pltpu.VMEM((nb, D_MOBA), F32)],
        compiler_params=pltpu.CompilerParams(dimension_semantics=("parallel", "arbitrary"),
                                             vmem_limit_bytes=VMEM_LIMIT),
        name="moba_prep",
    )(mb_qkv, cos_t, sin_t)


def _moba_attn_kernel(q_ref, k_ref, vt_ref, sel_ref, o_ref, s_ref, so_ref, m_ref, l_ref, acc_ref, *, nb, cb):
    j = pl.program_id(2)
    q = q_ref[0]
    dn = (((1,), (1,)), ((), ()))
    blk = MOBA_BLOCK
    ki = lax.broadcasted_iota(jnp.int32, (blk, blk), 0)
    qi = lax.broadcasted_iota(jnp.int32, (blk, blk), 1)

    k_own = k_ref[0, pl.ds(pl.multiple_of(j * blk, blk), blk), :]
    s = lax.dot_general(k_own, q, dn, preferred_element_type=F32)
    s = jnp.where(ki <= qi, s, NEG_INF)
    so_ref[...] = s
    m_ref[...] = jnp.max(s, axis=0, keepdims=True)

    nchunk = nb // cb
    for c in range(nchunk):
        @pl.when(c * cb < j)
        def _(c=c):
            sc = lax.dot_general(k_ref[0, c * cb * blk:(c + 1) * cb * blk, :], q, dn, preferred_element_type=F32)
            mx = m_ref[...]
            for i in range(cb):
                n = c * cb + i
                si = jnp.where(sel_ref[0, 0, n:n + 1, :] > 0.5, sc[i * blk:(i + 1) * blk, :], NEG_INF)
                s_ref[n * blk:(n + 1) * blk, :] = si
                mx = jnp.maximum(mx, jnp.max(si, axis=0, keepdims=True))
            m_ref[...] = mx

    m = m_ref[...]
    p = jnp.exp(so_ref[...] - m)
    l_ref[...] = jnp.sum(p, axis=0, keepdims=True)
    acc_ref[...] = jnp.dot(vt_ref[0, j], p.astype(BF16), preferred_element_type=F32)

    for c in range(nchunk):
        @pl.when(c * cb < j)
        def _(c=c):
            lsum = l_ref[...]
            acc = acc_ref[...]
            for i in range(cb):
                n = c * cb + i
                p = jnp.exp(s_ref[n * blk:(n + 1) * blk, :] - m)
                lsum = lsum + jnp.sum(p, axis=0, keepdims=True)
                acc = acc + jnp.dot(vt_ref[0, n], p.astype(BF16), preferred_element_type=F32)
            l_ref[...] = lsum
            acc_ref[...] = acc

    o_ref[0] = jnp.transpose(acc_ref[...] / l_ref[...]).astype(o_ref.dtype)


def _moba_attn(q_r, k_r, vt, sel):
    bsz, seq, _ = q_r.shape
    nb = seq // MOBA_BLOCK
    cb = 2 if nb % 2 == 0 else 1
    tok_spec = pl.BlockSpec((1, MOBA_BLOCK, HEAD_DIM), lambda b, h, j: (b, j, h))
    return pl.pallas_call(
        functools.partial(_moba_attn_kernel, nb=nb, cb=cb),
        out_shape=jax.ShapeDtypeStruct((bsz, seq, D_MOBA), BF16),
        grid=(bsz, N_HEADS_MOBA, nb),
        in_specs=[tok_spec,
                  pl.BlockSpec((1, seq, HEAD_DIM), lambda b, h, j: (b, 0, h)),
                  pl.BlockSpec((1, nb, HEAD_DIM, MOBA_BLOCK), lambda b, h, j: (b, 0, h, 0)),
                  pl.BlockSpec((1, 1, nb, MOBA_BLOCK), lambda b, h, j: (b, j, h, 0))],
        out_specs=tok_spec,
        scratch_shapes=[pltpu.VMEM((seq, MOBA_BLOCK), F32), pltpu.VMEM((MOBA_BLOCK, MOBA_BLOCK), F32),
                        pltpu.VMEM((1, MOBA_BLOCK), F32), pltpu.VMEM((1, MOBA_BLOCK), F32),
                        pltpu.VMEM((HEAD_DIM, MOBA_BLOCK), F32)],
        compiler_params=pltpu.CompilerParams(dimension_semantics=("parallel", "parallel", "arbitrary"),
                                             vmem_limit_bytes=VMEM_LIMIT),
        name="moba_attn",
    )(q_r, k_r, vt, sel)


def _mix_route_kernel(ydn_ref, ymb_ref, x_ref, wo_ref, g_ref, b_ref, rw_ref, rb_ref, h_ref, hb_ref, gate_ref):
    mix = (jnp.dot(ydn_ref[...], wo_ref[0:D_DN, :], preferred_element_type=F32)
           + jnp.dot(ymb_ref[...], wo_ref[D_DN:D_DN + D_MOBA, :], preferred_element_type=F32))
    hval = _layer_norm(DEEPNORM_ALPHA * x_ref[...] + mix, g_ref[...], b_ref[...])
    h_ref[...] = hval
    hb_ref[...] = hval.astype(BF16)

    logits = _dot3(hval, rw_ref[...]) + rb_ref[...]
    lane = lax.broadcasted_iota(jnp.int32, logits.shape, 1)
    big = jnp.int32(LANES)

    def first_lane(mask):
        return jnp.min(jnp.where(mask, lane, big), axis=1, keepdims=True)

    is_g = lane < N_GROUPS
    m1 = jnp.max(jnp.where(is_g, logits, NEG_INF), axis=1, keepdims=True)
    s1 = jnp.sum(jnp.where(is_g, jnp.exp(logits - m1), 0.0), axis=1, keepdims=True)
    pg = 1.0 / s1
    gsel = first_lane(is_g & (logits == m1))

    in_grp = (lane >= GATE_LANE0) & (((lane - GATE_LANE0) >> 2) == gsel) & (lane < GATE_LANE0 + N_EXPERTS)
    m2 = jnp.max(jnp.where(in_grp, logits, NEG_INF), axis=1, keepdims=True)
    s2 = jnp.sum(jnp.where(in_grp, jnp.exp(logits - m2), 0.0), axis=1, keepdims=True)
    e1 = first_lane(in_grp & (logits == m2))
    rest = in_grp & (lane != e1)
    m2b = jnp.max(jnp.where(rest, logits, NEG_INF), axis=1, keepdims=True)
    e2 = first_lane(rest & (logits == m2b))
    pe1 = 1.0 / s2
    pe2 = jnp.exp(m2b - m2) / s2
    tot = pe1 + pe2
    gate_ref[...] = jnp.where(lane == e1, pg * (pe1 / tot), jnp.where(lane == e2, pg * (pe2 / tot), 0.0))


def _mix_route(y_dn, y_mb, x2, wo, g1, b1, rw, rb, tm):
    n, d = x2.shape
    row = lambda w: pl.BlockSpec((1, w), lambda i: (0, 0))
    return pl.pallas_call(
        _mix_route_kernel,
        out_shape=(jax.ShapeDtypeStruct((n, d), F32), jax.ShapeDtypeStruct((n, d), BF16),
                   jax.ShapeDtypeStruct((n, LANES), F32)),
        grid=(n // tm,),
        in_specs=[pl.BlockSpec((tm, D_DN), lambda i: (i, 0)), pl.BlockSpec((tm, D_MOBA), lambda i: (i, 0)),
                  pl.BlockSpec((tm, d), lambda i: (i, 0)), pl.BlockSpec((D_DN + D_MOBA, d), lambda i: (0, 0)),
                  row(d), row(d), pl.BlockSpec((d, LANES), lambda i: (0, 0)), row(LANES)],
        out_specs=(pl.BlockSpec((tm, d), lambda i: (i, 0)), pl.BlockSpec((tm, d), lambda i: (i, 0)),
                   pl.BlockSpec((tm, LANES), lambda i: (i, 0))),
        compiler_params=pltpu.CompilerParams(dimension_semantics=("parallel",), vmem_limit_bytes=VMEM_LIMIT),
        name="mix_route",
    )(y_dn, y_mb, x2, wo, g1, b1, rw, rb)


def _moe_kernel(hb_ref, h_ref, gate_ref, wg_ref, wu_ref, wd_ref, g_ref, b_ref, o_ref, acc_ref):
    e = pl.program_id(1)

    @pl.when(e == 0)
    def _():
        acc_ref[...] = jnp.zeros_like(acc_ref)

    hb = hb_ref[...]
    gcol = _lane_pick(gate_ref[...], e + GATE_LANE0)
    he = _silu(jnp.dot(hb, wg_ref[0], preferred_element_type=F32)) * jnp.dot(hb, wu_ref[0],
                                                                             preferred_element_type=F32)
    acc_ref[...] += jnp.dot((he * gcol).astype(BF16), wd_ref[0], preferred_element_type=F32)

    @pl.when(e == pl.num_programs(1) - 1)
    def _():
        o_ref[...] = _layer_norm(DEEPNORM_ALPHA * h_ref[...] + acc_ref[...], g_ref[...], b_ref[...])


def _moe(hb, hf, gates, wg, wu, wd, g2, b2, tm):
    n, d = hf.shape
    row = pl.BlockSpec((1, d), lambda i, e: (0, 0))
    return pl.pallas_call(
        _moe_kernel,
        out_shape=jax.ShapeDtypeStruct((n, d), F32),
        grid=(n // tm, N_EXPERTS),
        in_specs=[pl.BlockSpec((tm, d), lambda i, e: (i, 0)), pl.BlockSpec((tm, d), lambda i, e: (i, 0)),
                  pl.BlockSpec((tm, LANES), lambda i, e: (i, 0)),
                  pl.BlockSpec((1, d, D_EXPERT), lambda i, e: (e, 0, 0)),
                  pl.BlockSpec((1, d, D_EXPERT), lambda i, e: (e, 0, 0)),
                  pl.BlockSpec((1, D_EXPERT, d), lambda i, e: (e, 0, 0)), row, row],
        out_specs=pl.BlockSpec((tm, d), lambda i, e: (i, 0)),
        scratch_shapes=[pltpu.VMEM((tm, d), F32)],
        compiler_params=pltpu.CompilerParams(dimension_semantics=("parallel", "arbitrary"),
                                             vmem_limit_bytes=VMEM_LIMIT),
        name="moe",
    )(hb, hf, gates, wg, wu, wd, g2, b2)


def _pad_lanes(a, lane0=0):
    return jnp.zeros((1, LANES), F32).at[0, lane0:lane0 + a.shape[0]].set(a.astype(F32))


def _rope_tables(seq):
    half = HEAD_DIM // 2
    inv_freq = ROPE_THETA ** (-jnp.arange(half, dtype=F32) / half)
    ang = jnp.arange(seq).astype(F32)[:, None] * inv_freq[None, :]
    cos, sin = jnp.cos(ang), jnp.sin(ang)
    return jnp.concatenate([cos, cos], axis=-1), jnp.concatenate([-sin, sin], axis=-1)


def _layer(x, w_in, conv_w, a_log, dt_bias, dn_norm_w, w_out, ln1_g, ln1_b, router_w1, router_b1,
           router_w2, router_b2, w_gate, w_up, w_down, ln2_g, ln2_b):
    bsz, seq, d = x.shape
    n = bsz * seq
    x2 = x.reshape(n, d)

    o_z, o_b, o_mb = 3 * D_DN, 4 * D_DN, 4 * D_DN + 2 * N_HEADS_DN
    w_ba = jnp.pad(w_in[:, o_b:o_mb], ((0, 0), (0, LANES - 2 * N_HEADS_DN)))
    w_all = jnp.concatenate([w_in[:, :o_z], w_in[:, o_z:o_b], w_ba, w_in[:, o_mb:]], axis=1).astype(BF16)

    tm = min(512, n)
    dn_qkv, z, ba, mb_qkv = _in_proj(x2, w_all, tm)

    y_dn = _deltanet(dn_qkv.reshape(bsz, seq, 3 * D_DN), z.reshape(bsz, seq, D_DN), ba.reshape(bsz, seq, LANES),
                     conv_w, _pad_lanes(a_log, N_HEADS_DN), _pad_lanes(dt_bias, N_HEADS_DN),
                     dn_norm_w.astype(F32).reshape(1, HEAD_DIM))

    cos_t, sin_t = _rope_tables(seq)
    q_r, k_r, vt, sel = _moba_prep(mb_qkv.reshape(bsz, seq, 3 * D_MOBA), cos_t, sin_t)
    y_mb = _moba_attn(q_r, k_r, vt, sel)

    rw = jnp.concatenate([router_w1, jnp.transpose(router_w2, (1, 0, 2)).reshape(d, N_EXPERTS)], axis=1)
    rw = jnp.pad(rw, ((0, 0), (0, LANES - rw.shape[1])))
    rb = _pad_lanes(jnp.concatenate([router_b1, router_b2.reshape(-1)]))
    hf, hb, gates = _mix_route(y_dn.reshape(n, D_DN), y_mb.reshape(n, D_MOBA), x2, w_out.astype(BF16),
                               ln1_g.reshape(1, d), ln1_b.reshape(1, d), rw, rb, tm)

    out = _moe(hb, hf, gates, w_gate.astype(BF16), w_up.astype(BF16), w_down.astype(BF16),
               ln2_g.reshape(1, d), ln2_b.reshape(1, d), min(1024, n))
    return out.reshape(bsz, seq, d)


def kernel(x, w_in, conv_w, a_log, dt_bias, dn_norm_w, w_out, ln1_g, ln1_b, router_w1, router_b1, router_w2, router_b2, expert_w_gate, expert_w_up, expert_w_down, ln2_g, ln2_b):
    for l in range(DEPTH):
        x = _layer(x, w_in[l], conv_w[l], a_log[l], dt_bias[l], dn_norm_w[l], w_out[l], ln1_g[l], ln1_b[l],
                   router_w1[l], router_b1[l], router_w2[l], router_b2[l], expert_w_gate[l], expert_w_up[l],
                   expert_w_down[l], ln2_g[l], ln2_b[l])
    return x
```

```python
import functools

import jax
import jax.numpy as jnp
from jax import lax
from jax.experimental import pallas as pl
from jax.experimental.pallas import tpu as pltpu

F32 = jnp.float32
BF16 = jnp.bfloat16

HEAD_DIM = 128
N_HEADS_DN = 4
N_HEADS_MOBA = 4
D_DN = N_HEADS_DN * HEAD_DIM
D_MOBA = N_HEADS_MOBA * HEAD_DIM
CONV_K = 4
DN_CHUNK = 64
MOBA_BLOCK = 256
MOBA_TOPK = 3
ROPE_THETA = 10000.0
N_GROUPS = 4
EXPERTS_PER_GROUP = 4
N_EXPERTS = N_GROUPS * EXPERTS_PER_GROUP
D_EXPERT = 256
LN_EPS = 1e-5
RMS_EPS = 1e-6
L2_EPS = 1e-6
NEG_INF = -1e30
DEPTH = 1
DEEPNORM_ALPHA = (2 * DEPTH) ** 0.25

LANES = 128
DN_TILE = 256
DN_HEADS_PER_STEP = 4
GATE_LANE0 = N_GROUPS
VMEM_LIMIT = 48 * 1024 * 1024


def _dot(a, b):
    return jnp.dot(a.astype(BF16), b.astype(BF16), preferred_element_type=F32)


def _dot_nt(a, b):
    return lax.dot_general(a.astype(BF16), b.astype(BF16), (((1,), (1,)), ((), ())),
                           preferred_element_type=F32)


def _split2(a):
    hi = a.astype(BF16)
    lo = (a - hi.astype(F32)).astype(BF16)
    return hi, lo


def _split3(a):
    hi = a.astype(BF16)
    r = a - hi.astype(F32)
    mid = r.astype(BF16)
    lo = (r - mid.astype(F32)).astype(BF16)
    return hi, mid, lo


def _dot3(a, b):
    ah, al = _split2(a)
    bh, bl = _split2(b)
    return (jnp.dot(ah, bh, preferred_element_type=F32) + jnp.dot(ah, bl, preferred_element_type=F32)
            + jnp.dot(al, bh, preferred_element_type=F32))


def _dot3_nt(a, b):
    ah, al = _split2(a)
    bh, bl = _split2(b)
    dn = (((1,), (1,)), ((), ()))
    return (lax.dot_general(ah, bh, dn, preferred_element_type=F32)
            + lax.dot_general(ah, bl, dn, preferred_element_type=F32)
            + lax.dot_general(al, bh, dn, preferred_element_type=F32))


def _dot_exact_lhs(a_bf16, b):
    bh, bm, bl = _split3(b)
    return (jnp.dot(a_bf16, bh, preferred_element_type=F32) + jnp.dot(a_bf16, bm, preferred_element_type=F32)
            + jnp.dot(a_bf16, bl, preferred_element_type=F32))


def _silu(x):
    return x * jax.nn.sigmoid(x)


def _softplus(x):
    return jnp.maximum(x, 0.0) + jnp.log1p(jnp.exp(-jnp.abs(x)))


def _layer_norm(t, g, b):
    mu = jnp.mean(t, axis=-1, keepdims=True)
    d = t - mu
    var = jnp.mean(d * d, axis=-1, keepdims=True)
    return d * lax.rsqrt(var + LN_EPS) * g + b


def _lane_pick(x, lane):
    ids = lax.broadcasted_iota(jnp.int32, x.shape, 1)
    return jnp.sum(jnp.where(ids == lane, x, 0.0), axis=1, keepdims=True)


def _in_proj_kernel(x_ref, w_ref, dn_ref, z_ref, ba_ref, mb_ref):
    xb = x_ref[...].astype(BF16)
    o0 = 3 * D_DN
    o1 = o0 + D_DN
    o2 = o1 + LANES
    dn_ref[...] = jnp.dot(xb, w_ref[:, 0:o0], preferred_element_type=F32)
    z_ref[...] = jnp.dot(xb, w_ref[:, o0:o1], preferred_element_type=F32)
    ba_ref[...] = jnp.dot(xb, w_ref[:, o1:o2], preferred_element_type=F32)
    mb_ref[...] = jnp.dot(xb, w_ref[:, o2:o2 + 3 * D_MOBA], preferred_element_type=F32)


def _in_proj(x2, w_all, tm):
    n, d = x2.shape
    wc = w_all.shape[1]
    return pl.pallas_call(
        _in_proj_kernel,
        out_shape=(jax.ShapeDtypeStruct((n, 3 * D_DN), F32), jax.ShapeDtypeStruct((n, D_DN), F32),
                   jax.ShapeDtypeStruct((n, LANES), F32), jax.ShapeDtypeStruct((n, 3 * D_MOBA), F32)),
        grid=(n // tm,),
        in_specs=[pl.BlockSpec((tm, d), lambda i: (i, 0)), pl.BlockSpec((d, wc), lambda i: (0, 0))],
        out_specs=(pl.BlockSpec((tm, 3 * D_DN), lambda i: (i, 0)), pl.BlockSpec((tm, D_DN), lambda i: (i, 0)),
                   pl.BlockSpec((tm, LANES), lambda i: (i, 0)), pl.BlockSpec((tm, 3 * D_MOBA), lambda i: (i, 0))),
        compiler_params=pltpu.CompilerParams(dimension_semantics=("parallel",), vmem_limit_bytes=VMEM_LIMIT),
        name="in_proj",
    )(x2, w_all)


def _inv_unit_lower(a_list, row, col):
    eye = (row == col).astype(F32)
    d8 = (row >> 3) == (col >> 3)
    a8 = [jnp.where(d8, a, 0.0) for a in a_list]
    a8_2 = [_dot(a, a) for a in a8]
    a8_4 = [_dot(a, a) for a in a8_2]
    x = [_dot(eye - a, eye + a2) for a, a2 in zip(a8, a8_2)]
    x = [_dot(xi, eye + a4) for xi, a4 in zip(x, a8_4)]
    s = 8
    while s < DN_CHUNK:
        sh = s.bit_length() - 1
        off = ((row >> (sh + 1)) == (col >> (sh + 1))) & ((row >> sh) != (col >> sh))
        y = [_dot(jnp.where(off, a, 0.0), xi) for a, xi in zip(a_list, x)]
        x = [xi - _dot(xi, yi) for xi, yi in zip(x, y)]
        s *= 2
    return x


def _deltanet_heads(q, k, v, z, beta, gcc, gcr, state, normw, masks):
    row, col, incl, strict = masks
    nh = len(q)
    hs = range(nh)
    tt = q[0].shape[0]
    nchunk = tt // DN_CHUNK
    q = [x * lax.rsqrt(jnp.sum(x * x, axis=-1, keepdims=True) + L2_EPS) * (HEAD_DIM ** -0.5) for x in q]
    k = [x * lax.rsqrt(jnp.sum(x * x, axis=-1, keepdims=True) + L2_EPS) for x in k]

    decay = [jnp.where(incl, jnp.exp(jnp.where(incl, gcc[h] - gcr[h], 0.0)), 0.0) for h in hs]
    kb = [k[h] * beta[h] for h in hs]
    vb = [v[h] * beta[h] for h in hs]
    a_mat = [jnp.where(strict, _dot_nt(kb[h], k[h]) * decay[h], 0.0) for h in hs]
    qk = [_dot_nt(q[h], k[h]) * decay[h] for h in hs]
    tinv = _inv_unit_lower(a_mat, row, col)

    eg = [jnp.exp(g) for g in gcc]
    wu = [_dot(tinv[h], jnp.concatenate([kb[h] * eg[h], vb[h]], axis=1)) for h in hs]
    qd = [q[h] * eg[h] for h in hs]

    gl_rows = [[g[(c + 1) * DN_CHUNK - 1:(c + 1) * DN_CHUNK, :] for c in range(nchunk)] for g in gcc]
    gl_col = [jnp.concatenate([jnp.broadcast_to(g, (DN_CHUNK, 1)) for g in rows], axis=0) for rows in gl_rows]
    kdt = [jnp.transpose(k[h] * jnp.exp(gl_col[h] - gcc[h])) for h in hs]

    outs = [[] for _ in hs]
    for c in range(nchunk):
        lo, hi = c * DN_CHUNK, (c + 1) * DN_CHUNK
        r = [_dot(jnp.concatenate([wu[h][lo:hi, 0:HEAD_DIM], qd[h][lo:hi, :]], axis=0), state[h]) for h in hs]
        vz = []
        for h in hs:
            parts = []
            if lo > 0:
                parts.append(jnp.zeros((lo, HEAD_DIM), F32))
            parts.append(wu[h][lo:hi, HEAD_DIM:2 * HEAD_DIM] - r[h][0:DN_CHUNK, :])
            if hi < tt:
                parts.append(jnp.zeros((tt - hi, HEAD_DIM), F32))
            vz.append(jnp.concatenate(parts, axis=0).astype(BF16))
        for h in hs:
            outs[h].append(r[h][DN_CHUNK:2 * DN_CHUNK, :] + _dot(qk[h][lo:hi, :], vz[h]))
        state = [state[h] * jnp.exp(gl_rows[h][c]) + _dot(kdt[h], vz[h]) for h in hs]

    ys = []
    for h in hs:
        o = jnp.concatenate(outs[h], axis=0)
        o = o * lax.rsqrt(jnp.mean(o * o, axis=-1, keepdims=True) + RMS_EPS) * normw
        ys.append(o * _silu(z[h]))
    return ys, state


def _deltanet_kernel(q_ref, k_ref, v_ref, z_ref, ba_ref, cwq_ref, cwk_ref, cwv_ref, alog_ref, dtb_ref,
                     normw_ref, y_ref, cbq, cbk, cbv, s_ref, *, hb):
    hg = pl.program_id(1)
    t = pl.program_id(2)
    tt = DN_TILE

    @pl.when(t == 0)
    def _():
        zero8 = jnp.zeros((8, hb * HEAD_DIM), F32)
        cbq[0:8, :] = zero8
        cbk[0:8, :] = zero8
        cbv[0:8, :] = zero8
        s_ref[...] = jnp.zeros_like(s_ref)

    def conv_silu(u_ref, cb, cw_ref):
        u = u_ref[0]
        cb[8:8 + tt, :] = u
        acc = cw_ref[CONV_K - 1:CONV_K, :] * u
        for s in range(1, CONV_K):
            acc = acc + cw_ref[CONV_K - 1 - s:CONV_K - s, :] * cb[8 - s:8 - s + tt, :]
        cb[0:8, :] = u[tt - 8:tt, :]
        return _silu(acc)

    q_all = conv_silu(q_ref, cbq, cwq_ref)
    k_all = conv_silu(k_ref, cbk, cwk_ref)
    v_all = conv_silu(v_ref, cbv, cwv_ref)

    ba = ba_ref[0]
    beta_all = jax.nn.sigmoid(ba)
    g_all = -jnp.exp(alog_ref[...]) * _softplus(ba + dtb_ref[...])

    row = lax.broadcasted_iota(jnp.int32, (tt, tt), 0)
    col = lax.broadcasted_iota(jnp.int32, (tt, tt), 1)
    same = (row >> 6) == (col >> 6)
    incl = same & (row >= col)
    strict = same & (row > col)
    masks = (row, col, incl, strict)

    gc_all = _dot_exact_lhs(incl.astype(BF16), g_all)
    gct = jnp.transpose(gc_all)
    sub = lax.broadcasted_iota(jnp.int32, gct.shape, 0)

    sls = [slice(hh * HEAD_DIM, (hh + 1) * HEAD_DIM) for hh in range(hb)]
    heads = [hg * hb + hh for hh in range(hb)]
    beta = [_lane_pick(beta_all, h) for h in heads]
    gcc = [_lane_pick(gc_all, h + N_HEADS_DN) for h in heads]
    gcr = [jnp.sum(jnp.where(sub == h + N_HEADS_DN, gct, 0.0), axis=0, keepdims=True) for h in heads]
    ys, states = _deltanet_heads([q_all[:, sl] for sl in sls], [k_all[:, sl] for sl in sls],
                                 [v_all[:, sl] for sl in sls], [z_ref[0, :, sl] for sl in sls], beta, gcc, gcr,
                                 [s_ref[hh] for hh in range(hb)], normw_ref[...], masks)
    s_ref[...] = jnp.stack(states, axis=0)
    y_ref[0] = jnp.concatenate(ys, axis=1).astype(y_ref.dtype)


def _deltanet(dn_qkv, z, ba, conv_w, alog_row, dtb_row, normw_row):
    bsz, seq, _ = dn_qkv.shape
    tt = DN_TILE
    hb = DN_HEADS_PER_STEP
    ng = N_HEADS_DN // hb
    w = hb * HEAD_DIM

    def col_spec(off):
        return pl.BlockSpec((1, tt, w), lambda b, g, t: (b, t, g + off))

    def cw_spec(off):
        return pl.BlockSpec((CONV_K, w), lambda b, g, t: (0, g + off))

    row_spec = pl.BlockSpec((1, LANES), lambda b, g, t: (0, 0))
    return pl.pallas_call(
        functools.partial(_deltanet_kernel, hb=hb),
        out_shape=jax.ShapeDtypeStruct((bsz, seq, D_DN), BF16),
        grid=(bsz, ng, seq // tt),
        in_specs=[col_spec(0), col_spec(ng), col_spec(2 * ng), col_spec(0),
                  pl.BlockSpec((1, tt, LANES), lambda b, g, t: (b, t, 0)),
                  cw_spec(0), cw_spec(ng), cw_spec(2 * ng), row_spec, row_spec, row_spec],
        out_specs=col_spec(0),
        scratch_shapes=[pltpu.VMEM((8 + tt, w), F32), pltpu.VMEM((8 + tt, w), F32),
                        pltpu.VMEM((8 + tt, w), F32), pltpu.VMEM((hb, HEAD_DIM, HEAD_DIM), F32)],
        compiler_params=pltpu.CompilerParams(dimension_semantics=("parallel", "parallel", "arbitrary"),
                                             vmem_limit_bytes=VMEM_LIMIT),
        name="deltanet",
    )(dn_qkv, dn_qkv, dn_qkv, z, ba, conv_w, conv_w, conv_w, alog_row, dtb_row, normw_row)


def _moba_prep_kernel(x_ref, cos_ref, sin_ref, q_ref, k_ref, vt_ref, sel_ref, km_ref, *, nb, topk):
    j = pl.program_id(1)

    @pl.when(j == 0)
    def _():
        km_ref[...] = jnp.zeros_like(km_ref)

    cos = cos_ref[...]
    sin = sin_ref[...]
    half = HEAD_DIM // 2
    blk = lax.broadcasted_iota(jnp.int32, (nb, MOBA_BLOCK), 0)
    kmeans = []
    for h in range(N_HEADS_MOBA):
        qh = x_ref[0, :, h * HEAD_DIM:(h + 1) * HEAD_DIM]
        kh = x_ref[0, :, D_MOBA + h * HEAD_DIM:D_MOBA + (h + 1) * HEAD_DIM]
        qr = (qh * cos + pltpu.roll(qh, half, 1) * sin) * (HEAD_DIM ** -0.5)
        kr = kh * cos + pltpu.roll(kh, half, 1) * sin
        q_ref[0, :, h * HEAD_DIM:(h + 1) * HEAD_DIM] = qr.astype(q_ref.dtype)
        k_ref[0, :, h * HEAD_DIM:(h + 1) * HEAD_DIM] = kr.astype(k_ref.dtype)
        kmeans.append(jnp.mean(kr, axis=0, keepdims=True))

        gate = _dot3_nt(km_ref[:, h * HEAD_DIM:(h + 1) * HEAD_DIM], qr)
        gate = jnp.where(blk < j, gate, NEG_INF)
        rank = jnp.zeros(gate.shape, F32)
        for m in range(nb):
            gm = gate[m:m + 1, :]
            ahead = (gm > gate) | ((gm == gate) & (blk > m))
            rank = rank + jnp.where(ahead, 1.0, 0.0)
        sel = (blk < j) & (rank < topk)
        sel_ref[0, 0, h * nb:(h + 1) * nb, :] = jnp.where(sel, 1.0, 0.0)

    km_ref[pl.ds(j, 1), :] = jnp.concatenate(kmeans, axis=1)
    vt_ref[0, 0] = jnp.transpose(x_ref[0, :, 2 * D_MOBA:3 * D_MOBA]).astype(vt_ref.dtype)


def _moba_prep(mb_qkv, cos_t, sin_t):
    bsz, seq, _ = mb_qkv.shape
    nb = seq // MOBA_BLOCK
    topk = min(MOBA_TOPK, nb)
    kern = functools.partial(_moba_prep_kernel, nb=nb, topk=topk)
    tok_spec = pl.BlockSpec((1, MOBA_BLOCK, D_MOBA), lambda b, j: (b, j, 0))
    tab_spec = pl.BlockSpec((MOBA_BLOCK, HEAD_DIM), lambda b, j: (j, 0))
    return pl.pallas_call(
        kern,
        out_shape=(jax.ShapeDtypeStruct((bsz, seq, D_MOBA), BF16), jax.ShapeDtypeStruct((bsz, seq, D_MOBA), BF16),
                   jax.ShapeDtypeStruct((bsz, nb, D_MOBA, MOBA_BLOCK), BF16),
                   jax.ShapeDtypeStruct((bsz, nb, N_HEADS_MOBA * nb, MOBA_BLOCK), F32)),
        grid=(bsz, nb),
        in_specs=[pl.BlockSpec((1, MOBA_BLOCK, 3 * D_MOBA), lambda b, j: (b, j, 0)), tab_spec, tab_spec],
        out_specs=(tok_spec, tok_spec,
                   pl.BlockSpec((1, 1, D_MOBA, MOBA_BLOCK), lambda b, j: (b, j, 0, 0)),
                   pl.BlockSpec((1, 1, N_HEADS_MOBA * nb, MOBA_BLOCK), lambda b, j: (b, j, 0, 0))),
        scratch_shapes=[pltpu.VMEM((nb, D_MOBA), F32)],
        compiler_params=pltpu.CompilerParams(dimension_semantics=("parallel", "arbitrary"),
                                             vmem_limit_bytes=VMEM_LIMIT),
        name="moba_prep",
    )(mb_qkv, cos_t, sin_t)


def _moba_attn_kernel(q_ref, k_ref, vt_ref, sel_ref, o_ref, acc_ref, *, nb):
    j = pl.program_id(1)
    blk = MOBA_BLOCK
    cw = 2 * blk
    nh = N_HEADS_MOBA
    nc = (j + 1) // 2
    dn = (((1,), (1,)), ((), ()))
    hsl = [slice(h * HEAD_DIM, (h + 1) * HEAD_DIM) for h in range(nh)]
    qs = [q_ref[0, :, hsl[h]] for h in range(nh)]

    ki = lax.broadcasted_iota(jnp.int32, (blk, blk), 0)
    qi = lax.broadcasted_iota(jnp.int32, (blk, blk), 1)
    own = pl.ds(pl.multiple_of(j * blk, blk), blk)
    s_own = [jnp.where(ki <= qi, lax.dot_general(k_ref[0, own, hsl[h]], qs[h], dn, preferred_element_type=F32),
                       NEG_INF) for h in range(nh)]
    ms, ls = [], []
    for h in range(nh):
        m = jnp.max(s_own[h], axis=0, keepdims=True)
        p = jnp.exp(s_own[h] - m)
        ms.append(m)
        ls.append(jnp.sum(p, axis=0, keepdims=True))
        acc_ref[h] = jnp.dot(vt_ref[0, j, hsl[h], :], p.astype(BF16), preferred_element_type=F32)

    def body(c, carry):
        ms, ls = carry
        ms, ls = list(ms), list(ls)
        ss = []
        for h in range(nh):
            kc = k_ref[0, pl.ds(pl.multiple_of(c * cw, cw), cw), hsl[h]]
            s = lax.dot_general(kc, qs[h], dn, preferred_element_type=F32)
            parts = []
            for i in range(2):
                selrow = sel_ref[0, 0, pl.ds(h * nb + 2 * c + i, 1), :]
                parts.append(jnp.where(selrow > 0.5, s[i * blk:(i + 1) * blk, :], NEG_INF))
            ss.append(jnp.concatenate(parts, axis=0))
        for h in range(nh):
            m_new = jnp.maximum(ms[h], jnp.max(ss[h], axis=0, keepdims=True))
            alpha = jnp.exp(ms[h] - m_new)
            p = jnp.exp(ss[h] - m_new)
            ls[h] = alpha * ls[h] + jnp.sum(p, axis=0, keepdims=True)
            ms[h] = m_new
            pb = p.astype(BF16)
            pv = (jnp.dot(vt_ref[0, 2 * c, hsl[h], :], pb[0:blk, :], preferred_element_type=F32)
                  + jnp.dot(vt_ref[0, 2 * c + 1, hsl[h], :], pb[blk:cw, :], preferred_element_type=F32))
            acc_ref[h] = acc_ref[h] * alpha + pv
        return tuple(ms), tuple(ls)

    ms, ls = lax.fori_loop(0, nc, body, (tuple(ms), tuple(ls)))
    o_ref[0] = jnp.concatenate([jnp.transpose(acc_ref[h] / ls[h]) for h in range(nh)],
                               axis=1).astype(o_ref.dtype)


def _moba_attn(q_r, k_r, vt, sel):
    bsz, seq, _ = q_r.shape
    nb = seq // MOBA_BLOCK
    tok_spec = pl.BlockSpec((1, MOBA_BLOCK, D_MOBA), lambda b, j: (b, j, 0))
    return pl.pallas_call(
        functools.partial(_moba_attn_kernel, nb=nb),
        out_shape=jax.ShapeDtypeStruct((bsz, seq, D_MOBA), BF16),
        grid=(bsz, nb),
        in_specs=[tok_spec,
                  pl.BlockSpec((1, seq, D_MOBA), lambda b, j: (b, 0, 0)),
                  pl.BlockSpec((1, nb, D_MOBA, MOBA_BLOCK), lambda b, j: (b, 0, 0, 0)),
                  pl.BlockSpec((1, 1, N_HEADS_MOBA * nb, MOBA_BLOCK), lambda b, j: (b, j, 0, 0))],
        out_specs=tok_spec,
        scratch_shapes=[pltpu.VMEM((N_HEADS_MOBA, HEAD_DIM, MOBA_BLOCK), F32)],
        compiler_params=pltpu.CompilerParams(dimension_semantics=("parallel", "arbitrary"),
                                             vmem_limit_bytes=VMEM_LIMIT),
        name="moba_attn",
    )(q_r, k_r, vt, sel)


def _mix_route_kernel(ydn_ref, ymb_ref, x_ref, wo_ref, g_ref, b_ref, rw_ref, rb_ref, h_ref, hb_ref, gate_ref):
    mix = (jnp.dot(ydn_ref[...], wo_ref[0:D_DN, :], preferred_element_type=F32)
           + jnp.dot(ymb_ref[...], wo_ref[D_DN:D_DN + D_MOBA, :], preferred_element_type=F32))
    hval = _layer_norm(DEEPNORM_ALPHA * x_ref[...] + mix, g_ref[...], b_ref[...])
    h_ref[...] = hval
    hb_ref[...] = hval.astype(BF16)

    logits = _dot3(hval, rw_ref[...]) + rb_ref[...]
    lane = lax.broadcasted_iota(jnp.int32, logits.shape, 1)
    big = jnp.int32(LANES)

    def first_lane(mask):
        return jnp.min(jnp.where(mask, lane, big), axis=1, keepdims=True)

    is_g = lane < N_GROUPS
    m1 = jnp.max(jnp.where(is_g, logits, NEG_INF), axis=1, keepdims=True)
    s1 = jnp.sum(jnp.where(is_g, jnp.exp(logits - m1), 0.0), axis=1, keepdims=True)
    pg = 1.0 / s1
    gsel = first_lane(is_g & (logits == m1))

    in_grp = (lane >= GATE_LANE0) & (((lane - GATE_LANE0) >> 2) == gsel) & (lane < GATE_LANE0 + N_EXPERTS)
    m2 = jnp.max(jnp.where(in_grp, logits, NEG_INF), axis=1, keepdims=True)
    s2 = jnp.sum(jnp.where(in_grp, jnp.exp(logits - m2), 0.0), axis=1, keepdims=True)
    e1 = first_lane(in_grp & (logits == m2))
    rest = in_grp & (lane != e1)
    m2b = jnp.max(jnp.where(rest, logits, NEG_INF), axis=1, keepdims=True)
    e2 = first_lane(rest & (logits == m2b))
    pe1 = 1.0 / s2
    pe2 = jnp.exp(m2b - m2) / s2
    tot = pe1 + pe2
    gate_ref[...] = jnp.where(lane == e1, pg * (pe1 / tot), jnp.where(lane == e2, pg * (pe2 / tot), 0.0))


def _mix_route(y_dn, y_mb, x2, wo, g1, b1, rw, rb, tm):
    n, d = x2.shape
    row = lambda w: pl.BlockSpec((1, w), lambda i: (0, 0))
    return pl.pallas_call(
        _mix_route_kernel,
        out_shape=(jax.ShapeDtypeStruct((n, d), F32), jax.ShapeDtypeStruct((n, d), BF16),
                   jax.ShapeDtypeStruct((n, LANES), F32)),
        grid=(n // tm,),
        in_specs=[pl.BlockSpec((tm, D_DN), lambda i: (i, 0)), pl.BlockSpec((tm, D_MOBA), lambda i: (i, 0)),
                  pl.BlockSpec((tm, d), lambda i: (i, 0)), pl.BlockSpec((D_DN + D_MOBA, d), lambda i: (0, 0)),
                  row(d), row(d), pl.BlockSpec((d, LANES), lambda i: (0, 0)), row(LANES)],
        out_specs=(pl.BlockSpec((tm, d), lambda i: (i, 0)), pl.BlockSpec((tm, d), lambda i: (i, 0)),
                   pl.BlockSpec((tm, LANES), lambda i: (i, 0))),
        compiler_params=pltpu.CompilerParams(dimension_semantics=("parallel",), vmem_limit_bytes=VMEM_LIMIT),
        name="mix_route",
    )(y_dn, y_mb, x2, wo, g1, b1, rw, rb)


def _moe_kernel(hb_ref, h_ref, gate_ref, wg_ref, wu_ref, wd_ref, g_ref, b_ref, o_ref, acc_ref):
    e = pl.program_id(1)

    @pl.when(e == 0)
    def _():
        acc_ref[...] = jnp.zeros_like(acc_ref)

    hb = hb_ref[...]
    gcol = _lane_pick(gate_ref[...], e + GATE_LANE0)
    he = _silu(jnp.dot(hb, wg_ref[0], preferred_element_type=F32)) * jnp.dot(hb, wu_ref[0],
                                                                             preferred_element_type=F32)
    acc_ref[...] += jnp.dot((he * gcol).astype(BF16), wd_ref[0], preferred_element_type=F32)

    @pl.when(e == pl.num_programs(1) - 1)
    def _():
        o_ref[...] = _layer_norm(DEEPNORM_ALPHA * h_ref[...] + acc_ref[...], g_ref[...], b_ref[...])


def _moe(hb, hf, gates, wg, wu, wd, g2, b2, tm):
    n, d = hf.shape
    row = pl.BlockSpec((1, d), lambda i, e: (0, 0))
    return pl.pallas_call(
        _moe_kernel,
        out_shape=jax.ShapeDtypeStruct((n, d), F32),
        grid=(n // tm, N_EXPERTS),
        in_specs=[pl.BlockSpec((tm, d), lambda i, e: (i, 0)), pl.BlockSpec((tm, d), lambda i, e: (i, 0)),
                  pl.BlockSpec((tm, LANES), lambda i, e: (i, 0)),
                  pl.BlockSpec((1, d, D_EXPERT), lambda i, e: (e, 0, 0)),
                  pl.BlockSpec((1, d, D_EXPERT), lambda i, e: (e, 0, 0)),
                  pl.BlockSpec((1, D_EXPERT, d), lambda i, e: (e, 0, 0)), row, row],
        out_specs=pl.BlockSpec((tm, d), lambda i, e: (i, 0)),
        scratch_shapes=[pltpu.VMEM((tm, d), F32)],
        compiler_params=pltpu.CompilerParams(dimension_semantics=("parallel", "arbitrary"),
                                             vmem_limit_bytes=VMEM_LIMIT),
        name="moe",
    )(hb, hf, gates, wg, wu, wd, g2, b2)


def _pad_lanes(a, lane0=0):
    return jnp.zeros((1, LANES), F32).at[0, lane0:lane0 + a.shape[0]].set(a.astype(F32))


def _rope_tables(seq):
    half = HEAD_DIM // 2
    inv_freq = ROPE_THETA ** (-jnp.arange(half, dtype=F32) / half)
    ang = jnp.arange(seq).astype(F32)[:, None] * inv_freq[None, :]
    cos, sin = jnp.cos(ang), jnp.sin(ang)
    return jnp.concatenate([cos, cos], axis=-1), jnp.concatenate([-sin, sin], axis=-1)


def _layer(x, w_in, conv_w, a_log, dt_bias, dn_norm_w, w_out, ln1_g, ln1_b, router_w1, router_b1,
           router_w2, router_b2, w_gate, w_up, w_down, ln2_g, ln2_b):
    bsz, seq, d = x.shape
    n = bsz * seq
    x2 = x.reshape(n, d)

    o_z, o_b, o_mb = 3 * D_DN, 4 * D_DN, 4 * D_DN + 2 * N_HEADS_DN
    w_ba = jnp.pad(w_in[:, o_b:o_mb], ((0, 0), (0, LANES - 2 * N_HEADS_DN)))
    w_all = jnp.concatenate([w_in[:, :o_z], w_in[:, o_z:o_b], w_ba, w_in[:, o_mb:]], axis=1).astype(BF16)

    tm = min(512, n)
    dn_qkv, z, ba, mb_qkv = _in_proj(x2, w_all, tm)

    y_dn = _deltanet(dn_qkv.reshape(bsz, seq, 3 * D_DN), z.reshape(bsz, seq, D_DN), ba.reshape(bsz, seq, LANES),
                     conv_w, _pad_lanes(a_log, N_HEADS_DN), _pad_lanes(dt_bias, N_HEADS_DN),
                     dn_norm_w.astype(F32).reshape(1, HEAD_DIM))

    cos_t, sin_t = _rope_tables(seq)
    q_r, k_r, vt, sel = _moba_prep(mb_qkv.reshape(bsz, seq, 3 * D_MOBA), cos_t, sin_t)
    y_mb = _moba_attn(q_r, k_r, vt, sel)

    rw = jnp.concatenate([router_w1, jnp.transpose(router_w2, (1, 0, 2)).reshape(d, N_EXPERTS)], axis=1)
    rw = jnp.pad(rw, ((0, 0), (0, LANES - rw.shape[1])))
    rb = _pad_lanes(jnp.concatenate([router_b1, router_b2.reshape(-1)]))
    hf, hb, gates = _mix_route(y_dn.reshape(n, D_DN), y_mb.reshape(n, D_MOBA), x2, w_out.astype(BF16),
                               ln1_g.reshape(1, d), ln1_b.reshape(1, d), rw, rb, tm)

    out = _moe(hb, hf, gates, w_gate.astype(BF16), w_up.astype(BF16), w_down.astype(BF16),
               ln2_g.reshape(1, d), ln2_b.reshape(1, d), min(1024, n))
    return out.reshape(bsz, seq, d)


def kernel(x, w_in, conv_w, a_log, dt_bias, dn_norm_w, w_out, ln1_g, ln1_b, router_w1, router_b1, router_w2, router_b2, expert_w_gate, expert_w_up, expert_w_down, ln2_g, ln2_b):
    for l in range(DEPTH):
        x = _layer(x, w_in[l], conv_w[l], a_log[l], dt_bias[l], dn_norm_w[l], w_out[l], ln1_g[l], ln1_b[l],
                   router_w1[l], router_b1[l], router_w2[l], router_b2[l], expert_w_gate[l], expert_w_up[l],
                   expert_w_down[l], ln2_g[l], ln2_b[l])
    return x
```

```python
import functools

import jax
import jax.numpy as jnp
from jax import lax
from jax.experimental import pallas as pl
from jax.experimental.pallas import tpu as pltpu

F32 = jnp.float32
BF16 = jnp.bfloat16

HEAD_DIM = 128
N_HEADS_DN = 4
N_HEADS_MOBA = 4
D_DN = N_HEADS_DN * HEAD_DIM
D_MOBA = N_HEADS_MOBA * HEAD_DIM
CONV_K = 4
DN_CHUNK = 64
MOBA_BLOCK = 256
MOBA_TOPK = 3
ROPE_THETA = 10000.0
N_GROUPS = 4
EXPERTS_PER_GROUP = 4
N_EXPERTS = N_GROUPS * EXPERTS_PER_GROUP
D_EXPERT = 256
LN_EPS = 1e-5
RMS_EPS = 1e-6
L2_EPS = 1e-6
NEG_INF = -1e30
DEPTH = 1
DEEPNORM_ALPHA = (2 * DEPTH) ** 0.25

LANES = 128
DN_TILE = 256
DN_HEADS_PER_STEP = 4
GATE_LANE0 = N_GROUPS
N_PAIRS = EXPERTS_PER_GROUP * (EXPERTS_PER_GROUP - 1) // 2
N_BUCKETS = N_GROUPS * N_PAIRS
ROUTE_BUCKET, ROUTE_W_LO, ROUTE_W_HI = 0, 1, 2
MOE_TS = 1024
MOE_TM = 256
GRAN = 16
GRAN_SHIFT = 4
LROWS = -(-(MOE_TS + N_BUCKETS * (GRAN - 1)) // LANES) * LANES
LGRAN = LROWS // GRAN
VMEM_LIMIT = 48 * 1024 * 1024


def _dot(a, b):
    return jnp.dot(a.astype(BF16), b.astype(BF16), preferred_element_type=F32)


def _dot_nt(a, b):
    return lax.dot_general(a.astype(BF16), b.astype(BF16), (((1,), (1,)), ((), ())),
                           preferred_element_type=F32)


def _split2(a):
    hi = a.astype(BF16)
    lo = (a - hi.astype(F32)).astype(BF16)
    return hi, lo


def _split3(a):
    hi = a.astype(BF16)
    r = a - hi.astype(F32)
    mid = r.astype(BF16)
    lo = (r - mid.astype(F32)).astype(BF16)
    return hi, mid, lo


def _dot3(a, b):
    ah, al = _split2(a)
    bh, bl = _split2(b)
    return (jnp.dot(ah, bh, preferred_element_type=F32) + jnp.dot(ah, bl, preferred_element_type=F32)
            + jnp.dot(al, bh, preferred_element_type=F32))


def _dot3_nt(a, b):
    ah, al = _split2(a)
    bh, bl = _split2(b)
    dn = (((1,), (1,)), ((), ()))
    return (lax.dot_general(ah, bh, dn, preferred_element_type=F32)
            + lax.dot_general(ah, bl, dn, preferred_element_type=F32)
            + lax.dot_general(al, bh, dn, preferred_element_type=F32))


def _dot_exact_lhs(a_bf16, b):
    bh, bm, bl = _split3(b)
    return (jnp.dot(a_bf16, bh, preferred_element_type=F32) + jnp.dot(a_bf16, bm, preferred_element_type=F32)
            + jnp.dot(a_bf16, bl, preferred_element_type=F32))


def _silu(x):
    return x * jax.nn.sigmoid(x)


def _softplus(x):
    return jnp.maximum(x, 0.0) + jnp.log1p(jnp.exp(-jnp.abs(x)))


def _layer_norm(t, g, b):
    mu = jnp.mean(t, axis=-1, keepdims=True)
    d = t - mu
    var = jnp.mean(d * d, axis=-1, keepdims=True)
    return d * lax.rsqrt(var + LN_EPS) * g + b


def _lane_pick(x, lane):
    ids = lax.broadcasted_iota(jnp.int32, x.shape, 1)
    return jnp.sum(jnp.where(ids == lane, x, 0.0), axis=1, keepdims=True)


def _in_proj_kernel(x_ref, w_ref, dn_ref, z_ref, ba_ref, mb_ref):
    xb = x_ref[...].astype(BF16)
    o0 = 3 * D_DN
    o1 = o0 + D_DN
    o2 = o1 + LANES
    dn_ref[...] = jnp.dot(xb, w_ref[:, 0:o0], preferred_element_type=F32)
    z_ref[...] = jnp.dot(xb, w_ref[:, o0:o1], preferred_element_type=F32)
    ba_ref[...] = jnp.dot(xb, w_ref[:, o1:o2], preferred_element_type=F32)
    mb_ref[...] = jnp.dot(xb, w_ref[:, o2:o2 + 3 * D_MOBA], preferred_element_type=F32)


def _in_proj(x2, w_all, tm):
    n, d = x2.shape
    wc = w_all.shape[1]
    return pl.pallas_call(
        _in_proj_kernel,
        out_shape=(jax.ShapeDtypeStruct((n, 3 * D_DN), F32), jax.ShapeDtypeStruct((n, D_DN), F32),
                   jax.ShapeDtypeStruct((n, LANES), F32), jax.ShapeDtypeStruct((n, 3 * D_MOBA), F32)),
        grid=(n // tm,),
        in_specs=[pl.BlockSpec((tm, d), lambda i: (i, 0)), pl.BlockSpec((d, wc), lambda i: (0, 0))],
        out_specs=(pl.BlockSpec((tm, 3 * D_DN), lambda i: (i, 0)), pl.BlockSpec((tm, D_DN), lambda i: (i, 0)),
                   pl.BlockSpec((tm, LANES), lambda i: (i, 0)), pl.BlockSpec((tm, 3 * D_MOBA), lambda i: (i, 0))),
        compiler_params=pltpu.CompilerParams(dimension_semantics=("parallel",), vmem_limit_bytes=VMEM_LIMIT),
        name="in_proj",
    )(x2, w_all)


def _inv_unit_lower(a_list, row, col):
    eye = (row == col).astype(F32)
    d8 = (row >> 3) == (col >> 3)
    a8 = [jnp.where(d8, a, 0.0) for a in a_list]
    a8_2 = [_dot(a, a) for a in a8]
    a8_4 = [_dot(a, a) for a in a8_2]
    x = [_dot(eye - a, eye + a2) for a, a2 in zip(a8, a8_2)]
    x = [_dot(xi, eye + a4) for xi, a4 in zip(x, a8_4)]
    s = 8
    while s < DN_CHUNK:
        sh = s.bit_length() - 1
        off = ((row >> (sh + 1)) == (col >> (sh + 1))) & ((row >> sh) != (col >> sh))
        y = [_dot(jnp.where(off, a, 0.0), xi) for a, xi in zip(a_list, x)]
        x = [xi - _dot(xi, yi) for xi, yi in zip(x, y)]
        s *= 2
    return x


def _deltanet_heads(q, k, v, z, beta, gcc, gcr, state, normw, masks):
    row, col, incl, strict = masks
    nh = len(q)
    hs = range(nh)
    tt = q[0].shape[0]
    nchunk = tt // DN_CHUNK
    q = [x * lax.rsqrt(jnp.sum(x * x, axis=-1, keepdims=True) + L2_EPS) * (HEAD_DIM ** -0.5) for x in q]
    k = [x * lax.rsqrt(jnp.sum(x * x, axis=-1, keepdims=True) + L2_EPS) for x in k]

    decay = [jnp.where(incl, jnp.exp(jnp.where(incl, gcc[h] - gcr[h], 0.0)), 0.0) for h in hs]
    kb = [k[h] * beta[h] for h in hs]
    vb = [v[h] * beta[h] for h in hs]
    a_mat = [jnp.where(strict, _dot_nt(kb[h], k[h]) * decay[h], 0.0) for h in hs]
    qk = [_dot_nt(q[h], k[h]) * decay[h] for h in hs]
    tinv = _inv_unit_lower(a_mat, row, col)

    eg = [jnp.exp(g) for g in gcc]
    wu = [_dot(tinv[h], jnp.concatenate([kb[h] * eg[h], vb[h]], axis=1)) for h in hs]
    qd = [q[h] * eg[h] for h in hs]

    gl_rows = [[g[(c + 1) * DN_CHUNK - 1:(c + 1) * DN_CHUNK, :] for c in range(nchunk)] for g in gcc]
    gl_col = [jnp.concatenate([jnp.broadcast_to(g, (DN_CHUNK, 1)) for g in rows], axis=0) for rows in gl_rows]
    kdt = [jnp.transpose(k[h] * jnp.exp(gl_col[h] - gcc[h])) for h in hs]

    outs = [[] for _ in hs]
    for c in range(nchunk):
        lo, hi = c * DN_CHUNK, (c + 1) * DN_CHUNK
        r = [_dot(jnp.concatenate([wu[h][lo:hi, 0:HEAD_DIM], qd[h][lo:hi, :]], axis=0), state[h]) for h in hs]
        vz = []
        for h in hs:
            parts = []
            if lo > 0:
                parts.append(jnp.zeros((lo, HEAD_DIM), F32))
            parts.append(wu[h][lo:hi, HEAD_DIM:2 * HEAD_DIM] - r[h][0:DN_CHUNK, :])
            if hi < tt:
                parts.append(jnp.zeros((tt - hi, HEAD_DIM), F32))
            vz.append(jnp.concatenate(parts, axis=0).astype(BF16))
        for h in hs:
            outs[h].append(r[h][DN_CHUNK:2 * DN_CHUNK, :] + _dot(qk[h][lo:hi, :], vz[h]))
        state = [state[h] * jnp.exp(gl_rows[h][c]) + _dot(kdt[h], vz[h]) for h in hs]

    ys = []
    for h in hs:
        o = jnp.concatenate(outs[h], axis=0)
        o = o * lax.rsqrt(jnp.mean(o * o, axis=-1, keepdims=True) + RMS_EPS) * normw
        ys.append(o * _silu(z[h]))
    return ys, state


def _deltanet_kernel(q_ref, k_ref, v_ref, z_ref, ba_ref, cwq_ref, cwk_ref, cwv_ref, alog_ref, dtb_ref,
                     normw_ref, y_ref, cbq, cbk, cbv, s_ref, *, hb):
    hg = pl.program_id(1)
    t = pl.program_id(2)
    tt = DN_TILE

    @pl.when(t == 0)
    def _():
        zero8 = jnp.zeros((8, hb * HEAD_DIM), F32)
        cbq[0:8, :] = zero8
        cbk[0:8, :] = zero8
        cbv[0:8, :] = zero8
        s_ref[...] = jnp.zeros_like(s_ref)

    def conv_silu(u_ref, cb, cw_ref):
        u = u_ref[0]
        cb[8:8 + tt, :] = u
        acc = cw_ref[CONV_K - 1:CONV_K, :] * u
        for s in range(1, CONV_K):
            acc = acc + cw_ref[CONV_K - 1 - s:CONV_K - s, :] * cb[8 - s:8 - s + tt, :]
        cb[0:8, :] = u[tt - 8:tt, :]
        return _silu(acc)

    q_all = conv_silu(q_ref, cbq, cwq_ref)
    k_all = conv_silu(k_ref, cbk, cwk_ref)
    v_all = conv_silu(v_ref, cbv, cwv_ref)

    ba = ba_ref[0]
    beta_all = jax.nn.sigmoid(ba)
    g_all = -jnp.exp(alog_ref[...]) * _softplus(ba + dtb_ref[...])

    row = lax.broadcasted_iota(jnp.int32, (tt, tt), 0)
    col = lax.broadcasted_iota(jnp.int32, (tt, tt), 1)
    same = (row >> 6) == (col >> 6)
    incl = same & (row >= col)
    strict = same & (row > col)
    masks = (row, col, incl, strict)

    gc_all = _dot_exact_lhs(incl.astype(BF16), g_all)
    gct = jnp.transpose(gc_all)
    sub = lax.broadcasted_iota(jnp.int32, gct.shape, 0)

    sls = [slice(hh * HEAD_DIM, (hh + 1) * HEAD_DIM) for hh in range(hb)]
    heads = [hg * hb + hh for hh in range(hb)]
    beta = [_lane_pick(beta_all, h) for h in heads]
    gcc = [_lane_pick(gc_all, h + N_HEADS_DN) for h in heads]
    gcr = [jnp.sum(jnp.where(sub == h + N_HEADS_DN, gct, 0.0), axis=0, keepdims=True) for h in heads]
    ys, states = _deltanet_heads([q_all[:, sl] for sl in sls], [k_all[:, sl] for sl in sls],
                                 [v_all[:, sl] for sl in sls], [z_ref[0, :, sl] for sl in sls], beta, gcc, gcr,
                                 [s_ref[hh] for hh in range(hb)], normw_ref[...], masks)
    s_ref[...] = jnp.stack(states, axis=0)
    y_ref[0] = jnp.concatenate(ys, axis=1).astype(y_ref.dtype)


def _deltanet(dn_qkv, z, ba, conv_w, alog_row, dtb_row, normw_row):
    bsz, seq, _ = dn_qkv.shape
    tt = DN_TILE
    hb = DN_HEADS_PER_STEP
    ng = N_HEADS_DN // hb
    w = hb * HEAD_DIM

    def col_spec(off):
        return pl.BlockSpec((1, tt, w), lambda b, g, t: (b, t, g + off))

    def cw_spec(off):
        return pl.BlockSpec((CONV_K, w), lambda b, g, t: (0, g + off))

    row_spec = pl.BlockSpec((1, LANES), lambda b, g, t: (0, 0))
    return pl.pallas_call(
        functools.partial(_deltanet_kernel, hb=hb),
        out_shape=jax.ShapeDtypeStruct((bsz, seq, D_DN), BF16),
        grid=(bsz, ng, seq // tt),
        in_specs=[col_spec(0), col_spec(ng), col_spec(2 * ng), col_spec(0),
                  pl.BlockSpec((1, tt, LANES), lambda b, g, t: (b, t, 0)),
                  cw_spec(0), cw_spec(ng), cw_spec(2 * ng), row_spec, row_spec, row_spec],
        out_specs=col_spec(0),
        scratch_shapes=[pltpu.VMEM((8 + tt, w), F32), pltpu.VMEM((8 + tt, w), F32),
                        pltpu.VMEM((8 + tt, w), F32), pltpu.VMEM((hb, HEAD_DIM, HEAD_DIM), F32)],
        compiler_params=pltpu.CompilerParams(dimension_semantics=("parallel", "parallel", "arbitrary"),
                                             vmem_limit_bytes=VMEM_LIMIT),
        name="deltanet",
    )(dn_qkv, dn_qkv, dn_qkv, z, ba, conv_w, conv_w, conv_w, alog_row, dtb_row, normw_row)


def _moba_prep_kernel(x_ref, cos_ref, sin_ref, q_ref, k_ref, vt_ref, sel_ref, km_ref, *, nb, topk):
    j = pl.program_id(1)

    @pl.when(j == 0)
    def _():
        km_ref[...] = jnp.zeros_like(km_ref)

    cos = cos_ref[...]
    sin = sin_ref[...]
    half = HEAD_DIM // 2
    blk = lax.broadcasted_iota(jnp.int32, (nb, MOBA_BLOCK), 0)
    kmeans = []
    for h in range(N_HEADS_MOBA):
        qh = x_ref[0, :, h * HEAD_DIM:(h + 1) * HEAD_DIM]
        kh = x_ref[0, :, D_MOBA + h * HEAD_DIM:D_MOBA + (h + 1) * HEAD_DIM]
        qr = (qh * cos + pltpu.roll(qh, half, 1) * sin) * (HEAD_DIM ** -0.5)
        kr = kh * cos + pltpu.roll(kh, half, 1) * sin
        q_ref[0, :, h * HEAD_DIM:(h + 1) * HEAD_DIM] = qr.astype(q_ref.dtype)
        k_ref[0, :, h * HEAD_DIM:(h + 1) * HEAD_DIM] = kr.astype(k_ref.dtype)
        kmeans.append(jnp.mean(kr, axis=0, keepdims=True))

        gate = _dot3_nt(km_ref[:, h * HEAD_DIM:(h + 1) * HEAD_DIM], qr)
        gate = jnp.where(blk < j, gate, NEG_INF)
        rank = jnp.zeros(gate.shape, F32)
        for m in range(nb):
            gm = gate[m:m + 1, :]
            ahead = (gm > gate) | ((gm == gate) & (blk > m))
            rank = rank + jnp.where(ahead, 1.0, 0.0)
        sel = (blk < j) & (rank < topk)
        sel_ref[0, 0, h * nb:(h + 1) * nb, :] = jnp.where(sel, 1.0, 0.0)

    km_ref[pl.ds(j, 1), :] = jnp.concatenate(kmeans, axis=1)
    vt_ref[0, 0] = jnp.transpose(x_ref[0, :, 2 * D_MOBA:3 * D_MOBA]).astype(vt_ref.dtype)


def _moba_prep(mb_qkv, cos_t, sin_t):
    bsz, seq, _ = mb_qkv.shape
    nb = seq // MOBA_BLOCK
    topk = min(MOBA_TOPK, nb)
    kern = functools.partial(_moba_prep_kernel, nb=nb, topk=topk)
    tok_spec = pl.BlockSpec((1, MOBA_BLOCK, D_MOBA), lambda b, j: (b, j, 0))
    tab_spec = pl.BlockSpec((MOBA_BLOCK, HEAD_DIM), lambda b, j: (j, 0))
    return pl.pallas_call(
        kern,
        out_shape=(jax.ShapeDtypeStruct((bsz, seq, D_MOBA), BF16), jax.ShapeDtypeStruct((bsz, seq, D_MOBA), BF16),
                   jax.ShapeDtypeStruct((bsz, nb, D_MOBA, MOBA_BLOCK), BF16),
                   jax.ShapeDtypeStruct((bsz, nb, N_HEADS_MOBA * nb, MOBA_BLOCK), F32)),
        grid=(bsz, nb),
        in_specs=[pl.BlockSpec((1, MOBA_BLOCK, 3 * D_MOBA), lambda b, j: (b, j, 0)), tab_spec, tab_spec],
        out_specs=(tok_spec, tok_spec,
                   pl.BlockSpec((1, 1, D_MOBA, MOBA_BLOCK), lambda b, j: (b, j, 0, 0)),
                   pl.BlockSpec((1, 1, N_HEADS_MOBA * nb, MOBA_BLOCK), lambda b, j: (b, j, 0, 0))),
        scratch_shapes=[pltpu.VMEM((nb, D_MOBA), F32)],
        compiler_params=pltpu.CompilerParams(dimension_semantics=("parallel", "arbitrary"),
                                             vmem_limit_bytes=VMEM_LIMIT),
        name="moba_prep",
    )(mb_qkv, cos_t, sin_t)


def _moba_attn_kernel(q_ref, k_ref, vt_ref, sel_ref, o_ref, acc_ref, *, nb):
    j = pl.program_id(1)
    blk = MOBA_BLOCK
    cw = 2 * blk
    nh = N_HEADS_MOBA
    nc = (j + 1) // 2
    dn = (((1,), (1,)), ((), ()))
    hsl = [slice(h * HEAD_DIM, (h + 1) * HEAD_DIM) for h in range(nh)]
    qs = [q_ref[0, :, hsl[h]] for h in range(nh)]

    ki = lax.broadcasted_iota(jnp.int32, (blk, blk), 0)
    qi = lax.broadcasted_iota(jnp.int32, (blk, blk), 1)
    own = pl.ds(pl.multiple_of(j * blk, blk), blk)
    s_own = [jnp.where(ki <= qi, lax.dot_general(k_ref[0, own, hsl[h]], qs[h], dn, preferred_element_type=F32),
                       NEG_INF) for h in range(nh)]
    ms, ls = [], []
    for h in range(nh):
        m = jnp.max(s_own[h], axis=0, keepdims=True)
        p = jnp.exp(s_own[h] - m)
        ms.append(m)
        ls.append(jnp.sum(p, axis=0, keepdims=True))
        acc_ref[h] = jnp.dot(vt_ref[0, j, hsl[h], :], p.astype(BF16), preferred_element_type=F32)

    def body(c, carry):
        ms, ls = carry
        ms, ls = list(ms), list(ls)
        ss = []
        for h in range(nh):
            kc = k_ref[0, pl.ds(pl.multiple_of(c * cw, cw), cw), hsl[h]]
            s = lax.dot_general(kc, qs[h], dn, preferred_element_type=F32)
            parts = []
            for i in range(2):
                selrow = sel_ref[0, 0, pl.ds(h * nb + 2 * c + i, 1), :]
                parts.append(jnp.where(selrow > 0.5, s[i * blk:(i + 1) * blk, :], NEG_INF))
            ss.append(jnp.concatenate(parts, axis=0))
        for h in range(nh):
            m_new = jnp.maximum(ms[h], jnp.max(ss[h], axis=0, keepdims=True))
            alpha = jnp.exp(ms[h] - m_new)
            p = jnp.exp(ss[h] - m_new)
            ls[h] = alpha * ls[h] + jnp.sum(p, axis=0, keepdims=True)
            ms[h] = m_new
            pb = p.astype(BF16)
            pv = (jnp.dot(vt_ref[0, 2 * c, hsl[h], :], pb[0:blk, :], preferred_element_type=F32)
                  + jnp.dot(vt_ref[0, 2 * c + 1, hsl[h], :], pb[blk:cw, :], preferred_element_type=F32))
            acc_ref[h] = acc_ref[h] * alpha + pv
        return tuple(ms), tuple(ls)

    ms, ls = lax.fori_loop(0, nc, body, (tuple(ms), tuple(ls)))
    o_ref[0] = jnp.concatenate([jnp.transpose(acc_ref[h] / ls[h]) for h in range(nh)],
                               axis=1).astype(o_ref.dtype)


def _moba_attn(q_r, k_r, vt, sel):
    bsz, seq, _ = q_r.shape
    nb = seq // MOBA_BLOCK
    tok_spec = pl.BlockSpec((1, MOBA_BLOCK, D_MOBA), lambda b, j: (b, j, 0))
    return pl.pallas_call(
        functools.partial(_moba_attn_kernel, nb=nb),
        out_shape=jax.ShapeDtypeStruct((bsz, seq, D_MOBA), BF16),
        grid=(bsz, nb),
        in_specs=[tok_spec,
                  pl.BlockSpec((1, seq, D_MOBA), lambda b, j: (b, 0, 0)),
                  pl.BlockSpec((1, nb, D_MOBA, MOBA_BLOCK), lambda b, j: (b, 0, 0, 0)),
                  pl.BlockSpec((1, 1, N_HEADS_MOBA * nb, MOBA_BLOCK), lambda b, j: (b, j, 0, 0))],
        out_specs=tok_spec,
        scratch_shapes=[pltpu.VMEM((N_HEADS_MOBA, HEAD_DIM, MOBA_BLOCK), F32)],
        compiler_params=pltpu.CompilerParams(dimension_semantics=("parallel", "arbitrary"),
                                             vmem_limit_bytes=VMEM_LIMIT),
        name="moba_attn",
    )(q_r, k_r, vt, sel)


def _mix_route_kernel(ydn_ref, ymb_ref, x_ref, wo_ref, g_ref, b_ref, rw_ref, rb_ref, h_ref, hb_ref, route_ref,
                      cnt_ref):
    mix = (jnp.dot(ydn_ref[...], wo_ref[0:D_DN, :], preferred_element_type=F32)
           + jnp.dot(ymb_ref[...], wo_ref[D_DN:D_DN + D_MOBA, :], preferred_element_type=F32))
    hval = _layer_norm(DEEPNORM_ALPHA * x_ref[...] + mix, g_ref[...], b_ref[...])
    h_ref[...] = hval
    hb_ref[...] = hval.astype(BF16)

    logits = _dot3(hval, rw_ref[...]) + rb_ref[...]
    lane = lax.broadcasted_iota(jnp.int32, logits.shape, 1)
    big = jnp.int32(LANES)

    def first_lane(mask):
        return jnp.min(jnp.where(mask, lane, big), axis=1, keepdims=True)

    is_g = lane < N_GROUPS
    m1 = jnp.max(jnp.where(is_g, logits, NEG_INF), axis=1, keepdims=True)
    s1 = jnp.sum(jnp.where(is_g, jnp.exp(logits - m1), 0.0), axis=1, keepdims=True)
    pg = 1.0 / s1
    gsel = first_lane(is_g & (logits == m1))

    in_grp = (lane >= GATE_LANE0) & (((lane - GATE_LANE0) >> 2) == gsel) & (lane < GATE_LANE0 + N_EXPERTS)
    m2 = jnp.max(jnp.where(in_grp, logits, NEG_INF), axis=1, keepdims=True)
    s2 = jnp.sum(jnp.where(in_grp, jnp.exp(logits - m2), 0.0), axis=1, keepdims=True)
    e1 = first_lane(in_grp & (logits == m2))
    rest = in_grp & (lane != e1)
    m2b = jnp.max(jnp.where(rest, logits, NEG_INF), axis=1, keepdims=True)
    e2 = first_lane(rest & (logits == m2b))
    pe1 = 1.0 / s2
    pe2 = jnp.exp(m2b - m2) / s2
    tot = pe1 + pe2
    w1 = pg * (pe1 / tot)
    w2 = pg * (pe2 / tot)
    first_lo = e1 < e2
    lo = jnp.minimum(e1, e2)
    hi = jnp.maximum(e1, e2)
    a = (lo - GATE_LANE0) & (EXPERTS_PER_GROUP - 1)
    b = (hi - GATE_LANE0) & (EXPERTS_PER_GROUP - 1)
    bucket = gsel * N_PAIRS + ((a * (2 * EXPERTS_PER_GROUP - 1 - a)) >> 1) + (b - a - 1)
    route_ref[...] = jnp.where(lane == ROUTE_BUCKET, bucket.astype(F32),
                               jnp.where(lane == ROUTE_W_LO, jnp.where(first_lo, w1, w2),
                                         jnp.where(lane == ROUTE_W_HI, jnp.where(first_lo, w2, w1), 0.0)))
    cnt_ref[0] = jnp.sum(jnp.where(lane == bucket, 1.0, 0.0), axis=0, keepdims=True)


def _mix_route(y_dn, y_mb, x2, wo, g1, b1, rw, rb, tm):
    n, d = x2.shape
    row = lambda w: pl.BlockSpec((1, w), lambda i: (0, 0))
    return pl.pallas_call(
        _mix_route_kernel,
        out_shape=(jax.ShapeDtypeStruct((n, d), F32), jax.ShapeDtypeStruct((n, d), BF16),
                   jax.ShapeDtypeStruct((n, LANES), F32), jax.ShapeDtypeStruct((n // tm, 1, LANES), F32)),
        grid=(n // tm,),
        in_specs=[pl.BlockSpec((tm, D_DN), lambda i: (i, 0)), pl.BlockSpec((tm, D_MOBA), lambda i: (i, 0)),
                  pl.BlockSpec((tm, d), lambda i: (i, 0)), pl.BlockSpec((D_DN + D_MOBA, d), lambda i: (0, 0)),
                  row(d), row(d), pl.BlockSpec((d, LANES), lambda i: (0, 0)), row(LANES)],
        out_specs=(pl.BlockSpec((tm, d), lambda i: (i, 0)), pl.BlockSpec((tm, d), lambda i: (i, 0)),
                   pl.BlockSpec((tm, LANES), lambda i: (i, 0)), pl.BlockSpec((1, 1, LANES), lambda i: (i, 0, 0))),
        compiler_params=pltpu.CompilerParams(dimension_semantics=("parallel",), vmem_limit_bytes=VMEM_LIMIT),
        name="mix_route",
    )(y_dn, y_mb, x2, wo, g1, b1, rw, rb)


def _bucket_offsets_col(ohf):
    cnt = jnp.sum(ohf, axis=1, keepdims=True).astype(jnp.int32)
    pad = (((cnt + (GRAN - 1)) >> GRAN_SHIFT) << GRAN_SHIFT).astype(F32)
    r = lax.broadcasted_iota(jnp.int32, (LANES, LANES), 0)
    c = lax.broadcasted_iota(jnp.int32, (LANES, LANES), 1)
    before = jnp.where(c < r, 1.0, 0.0)
    return _dot(before, jnp.broadcast_to(pad, (LANES, LANES)))[:, 0:1]


def _moe_sort_kernel(gmap_ref, nvalid_ref, tail0_ref, taillen_ref, hb_ref, route_ref, lstrict_ref,
                     xg_ref, wsg_ref, xs_ref, ws_ref, zx_ref, zw_ref, sem):
    s = pl.program_id(0)
    ts = route_ref.shape[0]
    route = route_ref[...]
    rt = jnp.transpose(route)
    bucket_row = rt[ROUTE_BUCKET:ROUTE_BUCKET + 1, :].astype(jnp.int32)
    sub = lax.broadcasted_iota(jnp.int32, (LANES, ts), 0)
    ohf = jnp.where(sub == bucket_row, 1.0, 0.0)
    loff = _bucket_offsets_col(ohf)
    rank = lax.dot_general(ohf.astype(BF16), lstrict_ref[...], (((1,), (1,)), ((), ())),
                           preferred_element_type=F32)
    dest = jnp.sum(ohf * (loff + rank), axis=0, keepdims=True).astype(jnp.int32)
    rowi = lax.broadcasted_iota(jnp.int32, (LROWS, ts), 0)
    perm = jnp.where(rowi == dest, 1.0, 0.0).astype(BF16)
    xs_ref[...] = jnp.dot(perm, hb_ref[...], preferred_element_type=F32).astype(BF16)
    rh, rm, rl = _split3(route)
    ws_ref[...] = (jnp.dot(perm, rh, preferred_element_type=F32) + jnp.dot(perm, rm, preferred_element_type=F32)
                   + jnp.dot(perm, rl, preferred_element_type=F32))

    def copies(g):
        src = pl.ds(pl.multiple_of(g * GRAN, GRAN), GRAN)
        dst = pl.ds(pl.multiple_of(gmap_ref[s * LGRAN + g] * GRAN, GRAN), GRAN)
        return (pltpu.make_async_copy(xs_ref.at[src, :], xg_ref.at[dst, :], sem.at[0]),
                pltpu.make_async_copy(ws_ref.at[src, :], wsg_ref.at[dst, :], sem.at[1]))

    def fill_copies(b, i):
        dst = pl.ds(pl.multiple_of((tail0_ref[b] + i) * GRAN, GRAN), GRAN)
        return (pltpu.make_async_copy(zx_ref.at[0:GRAN, :], xg_ref.at[dst, :], sem.at[2]),
                pltpu.make_async_copy(zw_ref.at[0:GRAN, :], wsg_ref.at[dst, :], sem.at[3]))

    def unused_tile_copies(t):
        dst = pl.ds(pl.multiple_of(t * MOE_TM, MOE_TM), MOE_TM)
        return (pltpu.make_async_copy(zx_ref, xg_ref.at[dst, :], sem.at[2]),
                pltpu.make_async_copy(zw_ref, wsg_ref.at[dst, :], sem.at[3]))

    def run(fn):
        def step(g, carry):
            for cp in copies(g):
                fn(cp)
            return carry
        lax.fori_loop(0, nvalid_ref[s], step, 0)

    def run_fill(fn):
        for b in range(N_BUCKETS):
            def step(i, carry, b=b):
                for cp in fill_copies(b, i):
                    fn(cp)
                return carry
            lax.fori_loop(0, taillen_ref[b], step, 0)

        def tile_step(t, carry):
            for cp in unused_tile_copies(t):
                fn(cp)
            return carry
        lax.fori_loop(tail0_ref[N_BUCKETS], xg_ref.shape[0] // MOE_TM, tile_step, 0)

    @pl.when(s == 0)
    def _():
        zx_ref[...] = jnp.zeros_like(zx_ref)
        zw_ref[...] = jnp.zeros_like(zw_ref)
        run_fill(lambda cp: cp.start())

    run(lambda cp: cp.start())
    run(lambda cp: cp.wait())

    @pl.when(s == 0)
    def _():
        run_fill(lambda cp: cp.wait())


def _moe_sort(plan, hb, route, lstrict):
    n, d = hb.shape
    ts = MOE_TS
    rows = plan["n_tiles"] * MOE_TM
    return pl.pallas_call(
        _moe_sort_kernel,
        out_shape=(jax.ShapeDtypeStruct((rows, d), BF16), jax.ShapeDtypeStruct((rows, LANES), F32)),
        grid_spec=pltpu.PrefetchScalarGridSpec(
            num_scalar_prefetch=4,
            grid=(n // ts,),
            in_specs=[pl.BlockSpec((ts, d), lambda s, *_: (s, 0)), pl.BlockSpec((ts, LANES), lambda s, *_: (s, 0)),
                      pl.BlockSpec((ts, ts), lambda s, *_: (0, 0))],
            out_specs=(pl.BlockSpec(memory_space=pl.ANY), pl.BlockSpec(memory_space=pl.ANY)),
            scratch_shapes=[pltpu.VMEM((LROWS, d), BF16), pltpu.VMEM((LROWS, LANES), F32),
                            pltpu.VMEM((MOE_TM, d), BF16), pltpu.VMEM((MOE_TM, LANES), F32),
                            pltpu.SemaphoreType.DMA((4,))]),
        compiler_params=pltpu.CompilerParams(dimension_semantics=("arbitrary",), vmem_limit_bytes=VMEM_LIMIT),
        name="moe_sort",
    )(plan["gmap"], plan["nvalid"], plan["tail0"], plan["taillen"], hb, route, lstrict)


def _moe_expert_kernel(xt_ref, elo_ref, ehi_ref, valid_ref, x_ref, w_ref, wg0, wu0, wd0, wg1, wu1, wd1, o_ref):
    t = pl.program_id(0)

    @pl.when(valid_ref[t] > 0)
    def _():
        x = x_ref[...]
        w = w_ref[...]
        acc = None
        for wg, wu, wd, lane in ((wg0, wu0, wd0, ROUTE_W_LO), (wg1, wu1, wd1, ROUTE_W_HI)):
            he = (_silu(jnp.dot(x, wg[0], preferred_element_type=F32))
                  * jnp.dot(x, wu[0], preferred_element_type=F32) * w[:, lane:lane + 1])
            y = jnp.dot(he.astype(BF16), wd[0], preferred_element_type=F32)
            acc = y if acc is None else acc + y
        o_ref[...] = acc.astype(o_ref.dtype)

    @pl.when(valid_ref[t] == 0)
    def _():
        o_ref[...] = jnp.zeros_like(o_ref)


def _moe_experts(plan, xg, wsg, wg, wu, wd):
    rows, d = xg.shape
    tm = MOE_TM
    tok = lambda width: pl.BlockSpec((tm, width), lambda t, xt, elo, ehi, valid: (xt[t], 0))
    lo3 = lambda shape: pl.BlockSpec(shape, lambda t, xt, elo, ehi, valid: (elo[t], 0, 0))
    hi3 = lambda shape: pl.BlockSpec(shape, lambda t, xt, elo, ehi, valid: (ehi[t], 0, 0))
    return pl.pallas_call(
        _moe_expert_kernel,
        out_shape=jax.ShapeDtypeStruct((rows, d), BF16),
        grid_spec=pltpu.PrefetchScalarGridSpec(
            num_scalar_prefetch=4,
            grid=(rows // tm,),
            in_specs=[tok(d), tok(LANES),
                      lo3((1, d, D_EXPERT)), lo3((1, d, D_EXPERT)), lo3((1, D_EXPERT, d)),
                      hi3((1, d, D_EXPERT)), hi3((1, d, D_EXPERT)), hi3((1, D_EXPERT, d))],
            out_specs=pl.BlockSpec((tm, d), lambda t, *_: (t, 0))),
        compiler_params=pltpu.CompilerParams(dimension_semantics=("arbitrary",), vmem_limit_bytes=VMEM_LIMIT),
        name="moe_experts",
    )(plan["xtile"], plan["elo"], plan["ehi"], plan["valid"], xg, wsg, wg, wu, wd, wg, wu, wd)


def _moe_unsort_kernel(gmap_ref, og_ref, route_ref, h_ref, lstrict_ref, g_ref, b_ref, out_ref, ol_ref, sem):
    s = pl.program_id(0)
    ts = route_ref.shape[0]

    def copy(g):
        src = pl.ds(pl.multiple_of(gmap_ref[s * LGRAN + g] * GRAN, GRAN), GRAN)
        dst = pl.ds(pl.multiple_of(g * GRAN, GRAN), GRAN)
        return pltpu.make_async_copy(og_ref.at[src, :], ol_ref.at[dst, :], sem.at[0])

    def start(g, carry):
        copy(g).start()
        return carry

    def wait(g, carry):
        copy(g).wait()
        return carry

    lax.fori_loop(0, LGRAN, start, 0)

    route = route_ref[...]
    bucket_col = route[:, ROUTE_BUCKET:ROUTE_BUCKET + 1].astype(jnp.int32)
    lane = lax.broadcasted_iota(jnp.int32, (ts, LANES), 1)
    ohf = jnp.where(lane == bucket_col, 1.0, 0.0)
    cnt = jnp.sum(ohf, axis=0, keepdims=True).astype(jnp.int32)
    pad = (((cnt + (GRAN - 1)) >> GRAN_SHIFT) << GRAN_SHIFT).astype(F32)
    r = lax.broadcasted_iota(jnp.int32, (LANES, LANES), 0)
    c = lax.broadcasted_iota(jnp.int32, (LANES, LANES), 1)
    loff = _dot(jnp.broadcast_to(pad, (8, LANES)), jnp.where(r < c, 1.0, 0.0))[0:1, :]
    rank = jnp.dot(lstrict_ref[...], ohf.astype(BF16), preferred_element_type=F32)
    dest = jnp.sum(ohf * (loff + rank), axis=1, keepdims=True).astype(jnp.int32)
    lrow = lax.broadcasted_iota(jnp.int32, (ts, LROWS), 1)
    perm_t = jnp.where(lrow == dest, 1.0, 0.0).astype(BF16)

    lax.fori_loop(0, LGRAN, wait, 0)
    ffn = jnp.dot(perm_t, ol_ref[...], preferred_element_type=F32)
    out_ref[...] = _layer_norm(DEEPNORM_ALPHA * h_ref[...] + ffn, g_ref[...], b_ref[...])


def _moe_unsort(plan, og, route, hf, lstrict, g2, b2):
    n, d = hf.shape
    ts = MOE_TS
    row = pl.BlockSpec((1, d), lambda s, *_: (0, 0))
    return pl.pallas_call(
        _moe_unsort_kernel,
        out_shape=jax.ShapeDtypeStruct((n, d), F32),
        grid_spec=pltpu.PrefetchScalarGridSpec(
            num_scalar_prefetch=1,
            grid=(n // ts,),
            in_specs=[pl.BlockSpec(memory_space=pl.ANY), pl.BlockSpec((ts, LANES), lambda s, *_: (s, 0)),
                      pl.BlockSpec((ts, d), lambda s, *_: (s, 0)), pl.BlockSpec((ts, ts), lambda s, *_: (0, 0)),
                      row, row],
            out_specs=pl.BlockSpec((ts, d), lambda s, *_: (s, 0)),
            scratch_shapes=[pltpu.VMEM((LROWS, d), BF16), pltpu.SemaphoreType.DMA((1,))]),
        compiler_params=pltpu.CompilerParams(dimension_semantics=("arbitrary",), vmem_limit_bytes=VMEM_LIMIT),
        name="moe_unsort",
    )(plan["gmap_back"], og, route, hf, lstrict, g2, b2)


def _moe_plan(cnt_half, n):
    nsrc = n // MOE_TS
    i32 = jnp.int32
    cnt = cnt_half.reshape(nsrc, -1, LANES).sum(axis=1)[:, :N_BUCKETS].astype(i32)
    run_g = (cnt + GRAN - 1) // GRAN
    nvalid = run_g.sum(axis=1)
    loff_g = jnp.cumsum(run_g, axis=1) - run_g
    bucket_g = run_g.sum(axis=0)
    gpt = MOE_TM // GRAN
    btiles = (bucket_g + gpt - 1) // gpt
    tend = jnp.cumsum(btiles)
    tstart = tend - btiles
    gofs = tstart[None, :] * gpt + jnp.cumsum(run_g, axis=0) - run_g
    n_tiles = -(-(n + nsrc * N_BUCKETS * (GRAN - 1)) // MOE_TM) + N_BUCKETS + 1
    g = jnp.arange(LGRAN, dtype=i32)[None, :, None]
    ends = jnp.cumsum(run_g, axis=1)[:, None, :]
    b_of_g = jnp.minimum(jnp.sum(g >= ends, axis=2), N_BUCKETS - 1)
    gmap = (jnp.take_along_axis(gofs, b_of_g, axis=1) + jnp.arange(LGRAN, dtype=i32)[None, :]
            - jnp.take_along_axis(loff_g, b_of_g, axis=1))
    is_valid = jnp.arange(LGRAN, dtype=i32)[None, :] < nvalid[:, None]
    zero_gran = (n_tiles - 1) * gpt
    t = jnp.arange(n_tiles, dtype=i32)
    tb = jnp.minimum(jnp.sum(t[:, None] >= tend[None, :], axis=1), N_BUCKETS - 1)
    valid = (t < tend[-1]).astype(i32)
    pair_a = jnp.asarray([a for a in range(EXPERTS_PER_GROUP) for b in range(a + 1, EXPERTS_PER_GROUP)], i32)
    pair_b = jnp.asarray([b for a in range(EXPERTS_PER_GROUP) for b in range(a + 1, EXPERTS_PER_GROUP)], i32)
    grp = tb // N_PAIRS
    return {
        "n_tiles": n_tiles,
        "gmap": jnp.where(is_valid, gmap, 0).reshape(-1).astype(i32),
        "gmap_back": jnp.where(is_valid, gmap, zero_gran).reshape(-1).astype(i32),
        "nvalid": nvalid.astype(i32),
        "tail0": jnp.concatenate([tstart * gpt + bucket_g, tend[-1:]]).astype(i32),
        "taillen": (btiles * gpt - bucket_g).astype(i32),
        "xtile": jnp.where(valid > 0, t, 0).astype(i32),
        "elo": (grp * EXPERTS_PER_GROUP + pair_a[tb % N_PAIRS]).astype(i32),
        "ehi": (grp * EXPERTS_PER_GROUP + pair_b[tb % N_PAIRS]).astype(i32),
        "valid": valid,
    }


def _pad_lanes(a, lane0=0):
    return jnp.zeros((1, LANES), F32).at[0, lane0:lane0 + a.shape[0]].set(a.astype(F32))


def _rope_tables(seq):
    half = HEAD_DIM // 2
    inv_freq = ROPE_THETA ** (-jnp.arange(half, dtype=F32) / half)
    ang = jnp.arange(seq).astype(F32)[:, None] * inv_freq[None, :]
    cos, sin = jnp.cos(ang), jnp.sin(ang)
    return jnp.concatenate([cos, cos], axis=-1), jnp.concatenate([-sin, sin], axis=-1)


def _layer(x, w_in, conv_w, a_log, dt_bias, dn_norm_w, w_out, ln1_g, ln1_b, router_w1, router_b1,
           router_w2, router_b2, w_gate, w_up, w_down, ln2_g, ln2_b):
    bsz, seq, d = x.shape
    n = bsz * seq
    x2 = x.reshape(n, d)

    o_z, o_b, o_mb = 3 * D_DN, 4 * D_DN, 4 * D_DN + 2 * N_HEADS_DN
    w_ba = jnp.pad(w_in[:, o_b:o_mb], ((0, 0), (0, LANES - 2 * N_HEADS_DN)))
    w_all = jnp.concatenate([w_in[:, :o_z], w_in[:, o_z:o_b], w_ba, w_in[:, o_mb:]], axis=1).astype(BF16)

    tm = min(512, n)
    dn_qkv, z, ba, mb_qkv = _in_proj(x2, w_all, tm)

    y_dn = _deltanet(dn_qkv.reshape(bsz, seq, 3 * D_DN), z.reshape(bsz, seq, D_DN), ba.reshape(bsz, seq, LANES),
                     conv_w, _pad_lanes(a_log, N_HEADS_DN), _pad_lanes(dt_bias, N_HEADS_DN),
                     dn_norm_w.astype(F32).reshape(1, HEAD_DIM))

    cos_t, sin_t = _rope_tables(seq)
    q_r, k_r, vt, sel = _moba_prep(mb_qkv.reshape(bsz, seq, 3 * D_MOBA), cos_t, sin_t)
    y_mb = _moba_attn(q_r, k_r, vt, sel)

    rw = jnp.concatenate([router_w1, jnp.transpose(router_w2, (1, 0, 2)).reshape(d, N_EXPERTS)], axis=1)
    rw = jnp.pad(rw, ((0, 0), (0, LANES - rw.shape[1])))
    rb = _pad_lanes(jnp.concatenate([router_b1, router_b2.reshape(-1)]))
    hf, hb, route, cnt = _mix_route(y_dn.reshape(n, D_DN), y_mb.reshape(n, D_MOBA), x2, w_out.astype(BF16),
                                    ln1_g.reshape(1, d), ln1_b.reshape(1, d), rw, rb, tm)

    plan = _moe_plan(cnt, n)
    idx = jnp.arange(MOE_TS, dtype=jnp.int32)
    lstrict = (idx[None, :] < idx[:, None]).astype(BF16)
    xg, wsg = _moe_sort(plan, hb, route, lstrict)
    og = _moe_experts(plan, xg, wsg, w_gate.astype(BF16), w_up.astype(BF16), w_down.astype(BF16))
    out = _moe_unsort(plan, og, route, hf, lstrict, ln2_g.reshape(1, d), ln2_b.reshape(1, d))
    return out.reshape(bsz, seq, d)


def kernel(x, w_in, conv_w, a_log, dt_bias, dn_norm_w, w_out, ln1_g, ln1_b, router_w1, router_b1, router_w2, router_b2, expert_w_gate, expert_w_up, expert_w_down, ln2_g, ln2_b):
    for l in range(DEPTH):
        x = _layer(x, w_in[l], conv_w[l], a_log[l], dt_bias[l], dn_norm_w[l], w_out[l], ln1_g[l], ln1_b[l],
                   router_w1[l], router_b1[l], router_w2[l], router_b2[l], expert_w_gate[l], expert_w_up[l],
                   expert_w_down[l], ln2_g[l], ln2_b[l])
    return x
```

```python
import functools

import jax
import jax.numpy as jnp
from jax import lax
from jax.experimental import pallas as pl
from jax.experimental.pallas import tpu as pltpu

F32 = jnp.float32
BF16 = jnp.bfloat16

HEAD_DIM = 128
N_HEADS_DN = 4
N_HEADS_MOBA = 4
D_DN = N_HEADS_DN * HEAD_DIM
D_MOBA = N_HEADS_MOBA * HEAD_DIM
CONV_K = 4
DN_CHUNK = 64
MOBA_BLOCK = 256
MOBA_TOPK = 3
ROPE_THETA = 10000.0
N_GROUPS = 4
EXPERTS_PER_GROUP = 4
N_EXPERTS = N_GROUPS * EXPERTS_PER_GROUP
D_EXPERT = 256
LN_EPS = 1e-5
RMS_EPS = 1e-6
L2_EPS = 1e-6
NEG_INF = -1e30
DEPTH = 1
DEEPNORM_ALPHA = (2 * DEPTH) ** 0.25

LANES = 128
DN_TILE = 256
DN_HEADS_PER_STEP = 4
GATE_LANE0 = N_GROUPS
N_PAIRS = EXPERTS_PER_GROUP * (EXPERTS_PER_GROUP - 1) // 2
N_BUCKETS = N_GROUPS * N_PAIRS
ROUTE_BUCKET, ROUTE_W_LO, ROUTE_W_HI = 0, 1, 2
MOE_TS = 1024
MOE_TM = 256
GRAN = 16
GRAN_SHIFT = 4
LROWS = -(-(MOE_TS + N_BUCKETS * (GRAN - 1)) // LANES) * LANES
LGRAN = LROWS // GRAN
VMEM_LIMIT = 48 * 1024 * 1024


def _dot(a, b):
    return jnp.dot(a.astype(BF16), b.astype(BF16), preferred_element_type=F32)


def _dot_nt(a, b):
    return lax.dot_general(a.astype(BF16), b.astype(BF16), (((1,), (1,)), ((), ())),
                           preferred_element_type=F32)


def _split2(a):
    hi = a.astype(BF16)
    lo = (a - hi.astype(F32)).astype(BF16)
    return hi, lo


def _split3(a):
    hi = a.astype(BF16)
    r = a - hi.astype(F32)
    mid = r.astype(BF16)
    lo = (r - mid.astype(F32)).astype(BF16)
    return hi, mid, lo


def _dot3(a, b):
    ah, al = _split2(a)
    bh, bl = _split2(b)
    return (jnp.dot(ah, bh, preferred_element_type=F32) + jnp.dot(ah, bl, preferred_element_type=F32)
            + jnp.dot(al, bh, preferred_element_type=F32))


def _dot3_nt(a, b):
    ah, al = _split2(a)
    bh, bl = _split2(b)
    dn = (((1,), (1,)), ((), ()))
    return (lax.dot_general(ah, bh, dn, preferred_element_type=F32)
            + lax.dot_general(ah, bl, dn, preferred_element_type=F32)
            + lax.dot_general(al, bh, dn, preferred_element_type=F32))


def _dot_exact_lhs(a_bf16, b):
    bh, bm, bl = _split3(b)
    return (jnp.dot(a_bf16, bh, preferred_element_type=F32) + jnp.dot(a_bf16, bm, preferred_element_type=F32)
            + jnp.dot(a_bf16, bl, preferred_element_type=F32))


def _silu(x):
    return x * jax.nn.sigmoid(x)


def _softplus(x):
    return jnp.maximum(x, 0.0) + jnp.log1p(jnp.exp(-jnp.abs(x)))


def _layer_norm(t, g, b):
    mu = jnp.mean(t, axis=-1, keepdims=True)
    d = t - mu
    var = jnp.mean(d * d, axis=-1, keepdims=True)
    return d * lax.rsqrt(var + LN_EPS) * g + b


def _lane_pick(x, lane):
    ids = lax.broadcasted_iota(jnp.int32, x.shape, 1)
    return jnp.sum(jnp.where(ids == lane, x, 0.0), axis=1, keepdims=True)


def _in_proj_kernel(x_ref, w_ref, dn_ref, z_ref, ba_ref, mb_ref):
    xb = x_ref[...].astype(BF16)
    o0 = 3 * D_DN
    o1 = o0 + D_DN
    o2 = o1 + LANES
    dn_ref[...] = jnp.dot(xb, w_ref[:, 0:o0], preferred_element_type=F32)
    z_ref[...] = jnp.dot(xb, w_ref[:, o0:o1], preferred_element_type=F32)
    ba_ref[...] = jnp.dot(xb, w_ref[:, o1:o2], preferred_element_type=F32)
    mb_ref[...] = jnp.dot(xb, w_ref[:, o2:o2 + 3 * D_MOBA], preferred_element_type=F32)


def _in_proj(x2, w_all, tm):
    n, d = x2.shape
    wc = w_all.shape[1]
    return pl.pallas_call(
        _in_proj_kernel,
        out_shape=(jax.ShapeDtypeStruct((n, 3 * D_DN), F32), jax.ShapeDtypeStruct((n, D_DN), F32),
                   jax.ShapeDtypeStruct((n, LANES), F32), jax.ShapeDtypeStruct((n, 3 * D_MOBA), F32)),
        grid=(n // tm,),
        in_specs=[pl.BlockSpec((tm, d), lambda i: (i, 0)), pl.BlockSpec((d, wc), lambda i: (0, 0))],
        out_specs=(pl.BlockSpec((tm, 3 * D_DN), lambda i: (i, 0)), pl.BlockSpec((tm, D_DN), lambda i: (i, 0)),
                   pl.BlockSpec((tm, LANES), lambda i: (i, 0)), pl.BlockSpec((tm, 3 * D_MOBA), lambda i: (i, 0))),
        compiler_params=pltpu.CompilerParams(dimension_semantics=("parallel",), vmem_limit_bytes=VMEM_LIMIT),
        name="in_proj",
    )(x2, w_all)


def _inv_unit_lower(a_list, row, col):
    eye = (row == col).astype(F32)
    d8 = (row >> 3) == (col >> 3)
    a8 = [jnp.where(d8, a, 0.0) for a in a_list]
    a8_2 = [_dot(a, a) for a in a8]
    a8_4 = [_dot(a, a) for a in a8_2]
    x = [_dot(eye - a, eye + a2) for a, a2 in zip(a8, a8_2)]
    x = [_dot(xi, eye + a4) for xi, a4 in zip(x, a8_4)]
    s = 8
    while s < DN_CHUNK:
        sh = s.bit_length() - 1
        off = ((row >> (sh + 1)) == (col >> (sh + 1))) & ((row >> sh) != (col >> sh))
        y = [_dot(jnp.where(off, a, 0.0), xi) for a, xi in zip(a_list, x)]
        x = [xi - _dot(xi, yi) for xi, yi in zip(x, y)]
        s *= 2
    return x


def _deltanet_heads(q, k, v, z, beta, gcc, gcr, state, normw, masks):
    row, col, incl, strict = masks
    nh = len(q)
    hs = range(nh)
    tt = q[0].shape[0]
    nchunk = tt // DN_CHUNK
    q = [x * lax.rsqrt(jnp.sum(x * x, axis=-1, keepdims=True) + L2_EPS) * (HEAD_DIM ** -0.5) for x in q]
    k = [x * lax.rsqrt(jnp.sum(x * x, axis=-1, keepdims=True) + L2_EPS) for x in k]

    decay = [jnp.where(incl, jnp.exp(jnp.where(incl, gcc[h] - gcr[h], 0.0)), 0.0) for h in hs]
    kb = [k[h] * beta[h] for h in hs]
    vb = [v[h] * beta[h] for h in hs]
    a_mat = [jnp.where(strict, _dot_nt(kb[h], k[h]) * decay[h], 0.0) for h in hs]
    qk = [_dot_nt(q[h], k[h]) * decay[h] for h in hs]
    tinv = _inv_unit_lower(a_mat, row, col)

    eg = [jnp.exp(g) for g in gcc]
    wu = [_dot(tinv[h], jnp.concatenate([kb[h] * eg[h], vb[h]], axis=1)) for h in hs]
    qd = [q[h] * eg[h] for h in hs]

    gl_rows = [[g[(c + 1) * DN_CHUNK - 1:(c + 1) * DN_CHUNK, :] for c in range(nchunk)] for g in gcc]
    gl_col = [jnp.concatenate([jnp.broadcast_to(g, (DN_CHUNK, 1)) for g in rows], axis=0) for rows in gl_rows]
    kdt = [jnp.transpose(k[h] * jnp.exp(gl_col[h] - gcc[h])) for h in hs]

    outs = [[] for _ in hs]
    for c in range(nchunk):
        lo, hi = c * DN_CHUNK, (c + 1) * DN_CHUNK
        r = [_dot(jnp.concatenate([wu[h][lo:hi, 0:HEAD_DIM], qd[h][lo:hi, :]], axis=0), state[h]) for h in hs]
        vz = []
        for h in hs:
            parts = []
            if lo > 0:
                parts.append(jnp.zeros((lo, HEAD_DIM), F32))
            parts.append(wu[h][lo:hi, HEAD_DIM:2 * HEAD_DIM] - r[h][0:DN_CHUNK, :])
            if hi < tt:
                parts.append(jnp.zeros((tt - hi, HEAD_DIM), F32))
            vz.append(jnp.concatenate(parts, axis=0).astype(BF16))
        for h in hs:
            outs[h].append(r[h][DN_CHUNK:2 * DN_CHUNK, :] + _dot(qk[h][lo:hi, :], vz[h]))
        state = [state[h] * jnp.exp(gl_rows[h][c]) + _dot(kdt[h], vz[h]) for h in hs]

    ys = []
    for h in hs:
        o = jnp.concatenate(outs[h], axis=0)
        o = o * lax.rsqrt(jnp.mean(o * o, axis=-1, keepdims=True) + RMS_EPS) * normw
        ys.append(o * _silu(z[h]))
    return ys, state


def _deltanet_kernel(q_ref, k_ref, v_ref, z_ref, ba_ref, cwq_ref, cwk_ref, cwv_ref, alog_ref, dtb_ref,
                     normw_ref, y_ref, cbq, cbk, cbv, s_ref, *, hb):
    hg = pl.program_id(1)
    t = pl.program_id(2)
    tt = DN_TILE

    @pl.when(t == 0)
    def _():
        zero8 = jnp.zeros((8, hb * HEAD_DIM), F32)
        cbq[0:8, :] = zero8
        cbk[0:8, :] = zero8
        cbv[0:8, :] = zero8
        s_ref[...] = jnp.zeros_like(s_ref)

    def conv_silu(u_ref, cb, cw_ref):
        u = u_ref[0]
        cb[8:8 + tt, :] = u
        acc = cw_ref[CONV_K - 1:CONV_K, :] * u
        for s in range(1, CONV_K):
            acc = acc + cw_ref[CONV_K - 1 - s:CONV_K - s, :] * cb[8 - s:8 - s + tt, :]
        cb[0:8, :] = u[tt - 8:tt, :]
        return _silu(acc)

    q_all = conv_silu(q_ref, cbq, cwq_ref)
    k_all = conv_silu(k_ref, cbk, cwk_ref)
    v_all = conv_silu(v_ref, cbv, cwv_ref)

    ba = ba_ref[0]
    beta_all = jax.nn.sigmoid(ba)
    g_all = -jnp.exp(alog_ref[...]) * _softplus(ba + dtb_ref[...])

    row = lax.broadcasted_iota(jnp.int32, (tt, tt), 0)
    col = lax.broadcasted_iota(jnp.int32, (tt, tt), 1)
    same = (row >> 6) == (col >> 6)
    incl = same & (row >= col)
    strict = same & (row > col)
    masks = (row, col, incl, strict)

    gc_all = _dot_exact_lhs(incl.astype(BF16), g_all)
    gct = jnp.transpose(gc_all)
    sub = lax.broadcasted_iota(jnp.int32, gct.shape, 0)

    sls = [slice(hh * HEAD_DIM, (hh + 1) * HEAD_DIM) for hh in range(hb)]
    heads = [hg * hb + hh for hh in range(hb)]
    beta = [_lane_pick(beta_all, h) for h in heads]
    gcc = [_lane_pick(gc_all, h + N_HEADS_DN) for h in heads]
    gcr = [jnp.sum(jnp.where(sub == h + N_HEADS_DN, gct, 0.0), axis=0, keepdims=True) for h in heads]
    ys, states = _deltanet_heads([q_all[:, sl] for sl in sls], [k_all[:, sl] for sl in sls],
                                 [v_all[:, sl] for sl in sls], [z_ref[0, :, sl] for sl in sls], beta, gcc, gcr,
                                 [s_ref[hh] for hh in range(hb)], normw_ref[...], masks)
    s_ref[...] = jnp.stack(states, axis=0)
    y_ref[0] = jnp.concatenate(ys, axis=1).astype(y_ref.dtype)


def _deltanet(dn_qkv, z, ba, conv_w, alog_row, dtb_row, normw_row):
    bsz, seq, _ = dn_qkv.shape
    tt = DN_TILE
    hb = DN_HEADS_PER_STEP
    ng = N_HEADS_DN // hb
    w = hb * HEAD_DIM

    def col_spec(off):
        return pl.BlockSpec((1, tt, w), lambda b, g, t: (b, t, g + off))

    def cw_spec(off):
        return pl.BlockSpec((CONV_K, w), lambda b, g, t: (0, g + off))

    row_spec = pl.BlockSpec((1, LANES), lambda b, g, t: (0, 0))
    return pl.pallas_call(
        functools.partial(_deltanet_kernel, hb=hb),
        out_shape=jax.ShapeDtypeStruct((bsz, seq, D_DN), BF16),
        grid=(bsz, ng, seq // tt),
        in_specs=[col_spec(0), col_spec(ng), col_spec(2 * ng), col_spec(0),
                  pl.BlockSpec((1, tt, LANES), lambda b, g, t: (b, t, 0)),
                  cw_spec(0), cw_spec(ng), cw_spec(2 * ng), row_spec, row_spec, row_spec],
        out_specs=col_spec(0),
        scratch_shapes=[pltpu.VMEM((8 + tt, w), F32), pltpu.VMEM((8 + tt, w), F32),
                        pltpu.VMEM((8 + tt, w), F32), pltpu.VMEM((hb, HEAD_DIM, HEAD_DIM), F32)],
        compiler_params=pltpu.CompilerParams(dimension_semantics=("parallel", "parallel", "arbitrary"),
                                             vmem_limit_bytes=VMEM_LIMIT),
        name="deltanet",
    )(dn_qkv, dn_qkv, dn_qkv, z, ba, conv_w, conv_w, conv_w, alog_row, dtb_row, normw_row)


def _moba_prep_kernel(x_ref, cos_ref, sin_ref, q_ref, k_ref, vt_ref, sel_ref, km_ref, *, nb, topk):
    j = pl.program_id(1)

    @pl.when(j == 0)
    def _():
        km_ref[...] = jnp.zeros_like(km_ref)

    cos = cos_ref[...]
    sin = sin_ref[...]
    half = HEAD_DIM // 2
    blk = lax.broadcasted_iota(jnp.int32, (nb, MOBA_BLOCK), 0)
    kmeans = []
    for h in range(N_HEADS_MOBA):
        qh = x_ref[0, :, h * HEAD_DIM:(h + 1) * HEAD_DIM]
        kh = x_ref[0, :, D_MOBA + h * HEAD_DIM:D_MOBA + (h + 1) * HEAD_DIM]
        qr = (qh * cos + pltpu.roll(qh, half, 1) * sin) * (HEAD_DIM ** -0.5)
        kr = kh * cos + pltpu.roll(kh, half, 1) * sin
        q_ref[0, :, h * HEAD_DIM:(h + 1) * HEAD_DIM] = qr.astype(q_ref.dtype)
        k_ref[0, :, h * HEAD_DIM:(h + 1) * HEAD_DIM] = kr.astype(k_ref.dtype)
        kmeans.append(jnp.mean(kr, axis=0, keepdims=True))

        gate = _dot3_nt(km_ref[:, h * HEAD_DIM:(h + 1) * HEAD_DIM], qr)
        gate = jnp.where(blk < j, gate, NEG_INF)
        rank = jnp.zeros(gate.shape, F32)
        for m in range(nb):
            gm = gate[m:m + 1, :]
            ahead = (gm > gate) | ((gm == gate) & (blk > m))
            rank = rank + jnp.where(ahead, 1.0, 0.0)
        sel = (blk < j) & (rank < topk)
        sel_ref[0, 0, h * nb:(h + 1) * nb, :] = jnp.where(sel, 1.0, 0.0)

    km_ref[pl.ds(j, 1), :] = jnp.concatenate(kmeans, axis=1)
    vt_ref[0, 0] = jnp.transpose(x_ref[0, :, 2 * D_MOBA:3 * D_MOBA]).astype(vt_ref.dtype)


def _moba_prep(mb_qkv, cos_t, sin_t):
    bsz, seq, _ = mb_qkv.shape
    nb = seq // MOBA_BLOCK
    topk = min(MOBA_TOPK, nb)
    kern = functools.partial(_moba_prep_kernel, nb=nb, topk=topk)
    tok_spec = pl.BlockSpec((1, MOBA_BLOCK, D_MOBA), lambda b, j: (b, j, 0))
    tab_spec = pl.BlockSpec((MOBA_BLOCK, HEAD_DIM), lambda b, j: (j, 0))
    return pl.pallas_call(
        kern,
        out_shape=(jax.ShapeDtypeStruct((bsz, seq, D_MOBA), BF16), jax.ShapeDtypeStruct((bsz, seq, D_MOBA), BF16),
                   jax.ShapeDtypeStruct((bsz, nb, D_MOBA, MOBA_BLOCK), BF16),
                   jax.ShapeDtypeStruct((bsz, nb, N_HEADS_MOBA * nb, MOBA_BLOCK), F32)),
        grid=(bsz, nb),
        in_specs=[pl.BlockSpec((1, MOBA_BLOCK, 3 * D_MOBA), lambda b, j: (b, j, 0)), tab_spec, tab_spec],
        out_specs=(tok_spec, tok_spec,
                   pl.BlockSpec((1, 1, D_MOBA, MOBA_BLOCK), lambda b, j: (b, j, 0, 0)),
                   pl.BlockSpec((1, 1, N_HEADS_MOBA * nb, MOBA_BLOCK), lambda b, j: (b, j, 0, 0))),
        scratch_shapes=[pltpu.VMEM((nb, D_MOBA), F32)],
        compiler_params=pltpu.CompilerParams(dimension_semantics=("parallel", "arbitrary"),
                                             vmem_limit_bytes=VMEM_LIMIT),
        name="moba_prep",
    )(mb_qkv, cos_t, sin_t)


def _moba_attn_kernel(q_ref, k_ref, vt_ref, sel_ref, o_ref, acc_ref, *, nb):
    j = pl.program_id(1)
    blk = MOBA_BLOCK
    cw = 2 * blk
    nh = N_HEADS_MOBA
    nc = (j + 1) // 2
    dn = (((1,), (1,)), ((), ()))
    hsl = [slice(h * HEAD_DIM, (h + 1) * HEAD_DIM) for h in range(nh)]
    qs = [q_ref[0, :, hsl[h]] for h in range(nh)]

    ki = lax.broadcasted_iota(jnp.int32, (blk, blk), 0)
    qi = lax.broadcasted_iota(jnp.int32, (blk, blk), 1)
    own = pl.ds(pl.multiple_of(j * blk, blk), blk)
    s_own = [jnp.where(ki <= qi, lax.dot_general(k_ref[0, own, hsl[h]], qs[h], dn, preferred_element_type=F32),
                       NEG_INF) for h in range(nh)]
    ms, ls = [], []
    for h in range(nh):
        m = jnp.max(s_own[h], axis=0, keepdims=True)
        p = jnp.exp(s_own[h] - m)
        ms.append(m)
        ls.append(jnp.sum(p, axis=0, keepdims=True))
        acc_ref[h] = jnp.dot(vt_ref[0, j, hsl[h], :], p.astype(BF16), preferred_element_type=F32)

    def body(c, carry):
        ms, ls = carry
        ms, ls = list(ms), list(ls)
        ss = []
        for h in range(nh):
            kc = k_ref[0, pl.ds(pl.multiple_of(c * cw, cw), cw), hsl[h]]
            s = lax.dot_general(kc, qs[h], dn, preferred_element_type=F32)
            parts = []
            for i in range(2):
                selrow = sel_ref[0, 0, pl.ds(h * nb + 2 * c + i, 1), :]
                parts.append(jnp.where(selrow > 0.5, s[i * blk:(i + 1) * blk, :], NEG_INF))
            ss.append(jnp.concatenate(parts, axis=0))
        for h in range(nh):
            m_new = jnp.maximum(ms[h], jnp.max(ss[h], axis=0, keepdims=True))
            alpha = jnp.exp(ms[h] - m_new)
            p = jnp.exp(ss[h] - m_new)
            ls[h] = alpha * ls[h] + jnp.sum(p, axis=0, keepdims=True)
            ms[h] = m_new
            pb = p.astype(BF16)
            pv = (jnp.dot(vt_ref[0, 2 * c, hsl[h], :], pb[0:blk, :], preferred_element_type=F32)
                  + jnp.dot(vt_ref[0, 2 * c + 1, hsl[h], :], pb[blk:cw, :], preferred_element_type=F32))
            acc_ref[h] = acc_ref[h] * alpha + pv
        return tuple(ms), tuple(ls)

    ms, ls = lax.fori_loop(0, nc, body, (tuple(ms), tuple(ls)))
    o_ref[0] = jnp.concatenate([jnp.transpose(acc_ref[h] / ls[h]) for h in range(nh)],
                               axis=1).astype(o_ref.dtype)


def _moba_attn(q_r, k_r, vt, sel):
    bsz, seq, _ = q_r.shape
    nb = seq // MOBA_BLOCK
    tok_spec = pl.BlockSpec((1, MOBA_BLOCK, D_MOBA), lambda b, j: (b, j, 0))
    return pl.pallas_call(
        functools.partial(_moba_attn_kernel, nb=nb),
        out_shape=jax.ShapeDtypeStruct((bsz, seq, D_MOBA), BF16),
        grid=(bsz, nb),
        in_specs=[tok_spec,
                  pl.BlockSpec((1, seq, D_MOBA), lambda b, j: (b, 0, 0)),
                  pl.BlockSpec((1, nb, D_MOBA, MOBA_BLOCK), lambda b, j: (b, 0, 0, 0)),
                  pl.BlockSpec((1, 1, N_HEADS_MOBA * nb, MOBA_BLOCK), lambda b, j: (b, j, 0, 0))],
        out_specs=tok_spec,
        scratch_shapes=[pltpu.VMEM((N_HEADS_MOBA, HEAD_DIM, MOBA_BLOCK), F32)],
        compiler_params=pltpu.CompilerParams(dimension_semantics=("parallel", "arbitrary"),
                                             vmem_limit_bytes=VMEM_LIMIT),
        name="moba_attn",
    )(q_r, k_r, vt, sel)


def _mix_route_kernel(ydn_ref, ymb_ref, x_ref, wo_ref, g_ref, b_ref, rw_ref, rb_ref, h_ref, hb_ref, route_ref,
                      cnt_ref):
    mix = (jnp.dot(ydn_ref[...], wo_ref[0:D_DN, :], preferred_element_type=F32)
           + jnp.dot(ymb_ref[...], wo_ref[D_DN:D_DN + D_MOBA, :], preferred_element_type=F32))
    hval = _layer_norm(DEEPNORM_ALPHA * x_ref[...] + mix, g_ref[...], b_ref[...])
    h_ref[...] = hval
    hb_ref[...] = hval.astype(BF16)

    logits = _dot3(hval, rw_ref[...]) + rb_ref[...]
    lane = lax.broadcasted_iota(jnp.int32, logits.shape, 1)
    big = jnp.int32(LANES)

    def first_lane(mask):
        return jnp.min(jnp.where(mask, lane, big), axis=1, keepdims=True)

    is_g = lane < N_GROUPS
    m1 = jnp.max(jnp.where(is_g, logits, NEG_INF), axis=1, keepdims=True)
    s1 = jnp.sum(jnp.where(is_g, jnp.exp(logits - m1), 0.0), axis=1, keepdims=True)
    pg = 1.0 / s1
    gsel = first_lane(is_g & (logits == m1))

    in_grp = (lane >= GATE_LANE0) & (((lane - GATE_LANE0) >> 2) == gsel) & (lane < GATE_LANE0 + N_EXPERTS)
    m2 = jnp.max(jnp.where(in_grp, logits, NEG_INF), axis=1, keepdims=True)
    s2 = jnp.sum(jnp.where(in_grp, jnp.exp(logits - m2), 0.0), axis=1, keepdims=True)
    e1 = first_lane(in_grp & (logits == m2))
    rest = in_grp & (lane != e1)
    m2b = jnp.max(jnp.where(rest, logits, NEG_INF), axis=1, keepdims=True)
    e2 = first_lane(rest & (logits == m2b))
    pe1 = 1.0 / s2
    pe2 = jnp.exp(m2b - m2) / s2
    tot = pe1 + pe2
    w1 = pg * (pe1 / tot)
    w2 = pg * (pe2 / tot)
    first_lo = e1 < e2
    lo = jnp.minimum(e1, e2)
    hi = jnp.maximum(e1, e2)
    a = (lo - GATE_LANE0) & (EXPERTS_PER_GROUP - 1)
    b = (hi - GATE_LANE0) & (EXPERTS_PER_GROUP - 1)
    bucket = gsel * N_PAIRS + ((a * (2 * EXPERTS_PER_GROUP - 1 - a)) >> 1) + (b - a - 1)
    route_ref[...] = jnp.where(lane == ROUTE_BUCKET, bucket.astype(F32),
                               jnp.where(lane == ROUTE_W_LO, jnp.where(first_lo, w1, w2),
                                         jnp.where(lane == ROUTE_W_HI, jnp.where(first_lo, w2, w1), 0.0)))
    cnt_ref[0] = jnp.sum(jnp.where(lane == bucket, 1.0, 0.0), axis=0, keepdims=True)


def _mix_route(y_dn, y_mb, x2, wo, g1, b1, rw, rb, tm):
    n, d = x2.shape
    row = lambda w: pl.BlockSpec((1, w), lambda i: (0, 0))
    return pl.pallas_call(
        _mix_route_kernel,
        out_shape=(jax.ShapeDtypeStruct((n, d), F32), jax.ShapeDtypeStruct((n, d), BF16),
                   jax.ShapeDtypeStruct((n, LANES), F32), jax.ShapeDtypeStruct((n // tm, 1, LANES), F32)),
        grid=(n // tm,),
        in_specs=[pl.BlockSpec((tm, D_DN), lambda i: (i, 0)), pl.BlockSpec((tm, D_MOBA), lambda i: (i, 0)),
                  pl.BlockSpec((tm, d), lambda i: (i, 0)), pl.BlockSpec((D_DN + D_MOBA, d), lambda i: (0, 0)),
                  row(d), row(d), pl.BlockSpec((d, LANES), lambda i: (0, 0)), row(LANES)],
        out_specs=(pl.BlockSpec((tm, d), lambda i: (i, 0)), pl.BlockSpec((tm, d), lambda i: (i, 0)),
                   pl.BlockSpec((tm, LANES), lambda i: (i, 0)), pl.BlockSpec((1, 1, LANES), lambda i: (i, 0, 0))),
        compiler_params=pltpu.CompilerParams(dimension_semantics=("parallel",), vmem_limit_bytes=VMEM_LIMIT),
        name="mix_route",
    )(y_dn, y_mb, x2, wo, g1, b1, rw, rb)


def _bucket_offsets_col(ohf):
    cnt = jnp.sum(ohf, axis=1, keepdims=True).astype(jnp.int32)
    pad = (((cnt + (GRAN - 1)) >> GRAN_SHIFT) << GRAN_SHIFT).astype(F32)
    r = lax.broadcasted_iota(jnp.int32, (LANES, LANES), 0)
    c = lax.broadcasted_iota(jnp.int32, (LANES, LANES), 1)
    before = jnp.where(c < r, 1.0, 0.0)
    return _dot(before, jnp.broadcast_to(pad, (LANES, LANES)))[:, 0:1]


def _moe_sort_kernel(gmap_ref, nvalid_ref, tail0_ref, taillen_ref, hb_ref, route_ref, lstrict_ref,
                     xg_ref, wsg_ref, xs_ref, ws_ref, zx_ref, zw_ref, sem):
    s = pl.program_id(0)
    nsteps = pl.num_programs(0)
    slot = s & 1
    ts = route_ref.shape[0]
    route = route_ref[...]
    rt = jnp.transpose(route)
    bucket_row = rt[ROUTE_BUCKET:ROUTE_BUCKET + 1, :].astype(jnp.int32)
    sub = lax.broadcasted_iota(jnp.int32, (LANES, ts), 0)
    ohf = jnp.where(sub == bucket_row, 1.0, 0.0)
    loff = _bucket_offsets_col(ohf)
    rank = lax.dot_general(ohf.astype(BF16), lstrict_ref[...], (((1,), (1,)), ((), ())),
                           preferred_element_type=F32)
    dest = jnp.sum(ohf * (loff + rank), axis=0, keepdims=True).astype(jnp.int32)
    rowi = lax.broadcasted_iota(jnp.int32, (LROWS, ts), 0)
    perm = jnp.where(rowi == dest, 1.0, 0.0).astype(BF16)
    xs_ref[slot] = jnp.dot(perm, hb_ref[...], preferred_element_type=F32).astype(BF16)
    rh, rl = _split2(route)
    wparts = jnp.dot(perm, jnp.concatenate([rh, rl], axis=1), preferred_element_type=F32)
    ws_ref[slot] = wparts[:, 0:LANES] + wparts[:, LANES:2 * LANES]

    def copies(step, g):
        sl = step & 1
        src = pl.ds(pl.multiple_of(g * GRAN, GRAN), GRAN)
        dst = pl.ds(pl.multiple_of(gmap_ref[step * LGRAN + g] * GRAN, GRAN), GRAN)
        return (pltpu.make_async_copy(xs_ref.at[sl, src, :], xg_ref.at[dst, :], sem.at[0, sl]),
                pltpu.make_async_copy(ws_ref.at[sl, src, :], wsg_ref.at[dst, :], sem.at[1, sl]))

    def fill_copies(b, i):
        dst = pl.ds(pl.multiple_of((tail0_ref[b] + i) * GRAN, GRAN), GRAN)
        return (pltpu.make_async_copy(zx_ref.at[0:GRAN, :], xg_ref.at[dst, :], sem.at[2, 0]),
                pltpu.make_async_copy(zw_ref.at[0:GRAN, :], wsg_ref.at[dst, :], sem.at[2, 1]))

    def unused_tile_copies(t):
        dst = pl.ds(pl.multiple_of(t * MOE_TM, MOE_TM), MOE_TM)
        return (pltpu.make_async_copy(zx_ref, xg_ref.at[dst, :], sem.at[2, 0]),
                pltpu.make_async_copy(zw_ref, wsg_ref.at[dst, :], sem.at[2, 1]))

    def run(step, fn):
        def body(g, carry):
            for cp in copies(step, g):
                fn(cp)
            return carry
        lax.fori_loop(0, nvalid_ref[step], body, 0)

    def run_fill(fn):
        for b in range(N_BUCKETS):
            def body(i, carry, b=b):
                for cp in fill_copies(b, i):
                    fn(cp)
                return carry
            lax.fori_loop(0, taillen_ref[b], body, 0)

        def tile_body(t, carry):
            for cp in unused_tile_copies(t):
                fn(cp)
            return carry
        lax.fori_loop(tail0_ref[N_BUCKETS], xg_ref.shape[0] // MOE_TM, tile_body, 0)

    @pl.when(s == 0)
    def _():
        zx_ref[...] = jnp.zeros_like(zx_ref)
        zw_ref[...] = jnp.zeros_like(zw_ref)
        run_fill(lambda cp: cp.start())

    run(s, lambda cp: cp.start())

    @pl.when(s > 0)
    def _():
        run(s - 1, lambda cp: cp.wait())

    @pl.when(s == nsteps - 1)
    def _():
        run(s, lambda cp: cp.wait())
        run_fill(lambda cp: cp.wait())


def _moe_sort(plan, hb, route, lstrict):
    n, d = hb.shape
    ts = MOE_TS
    rows = plan["n_tiles"] * MOE_TM
    return pl.pallas_call(
        _moe_sort_kernel,
        out_shape=(jax.ShapeDtypeStruct((rows, d), BF16), jax.ShapeDtypeStruct((rows, LANES), F32)),
        grid_spec=pltpu.PrefetchScalarGridSpec(
            num_scalar_prefetch=4,
            grid=(n // ts,),
            in_specs=[pl.BlockSpec((ts, d), lambda s, *_: (s, 0)), pl.BlockSpec((ts, LANES), lambda s, *_: (s, 0)),
                      pl.BlockSpec((ts, ts), lambda s, *_: (0, 0))],
            out_specs=(pl.BlockSpec(memory_space=pl.ANY), pl.BlockSpec(memory_space=pl.ANY)),
            scratch_shapes=[pltpu.VMEM((2, LROWS, d), BF16), pltpu.VMEM((2, LROWS, LANES), F32),
                            pltpu.VMEM((MOE_TM, d), BF16), pltpu.VMEM((MOE_TM, LANES), F32),
                            pltpu.SemaphoreType.DMA((3, 2))]),
        compiler_params=pltpu.CompilerParams(dimension_semantics=("arbitrary",), vmem_limit_bytes=VMEM_LIMIT),
        name="moe_sort",
    )(plan["gmap"], plan["nvalid"], plan["tail0"], plan["taillen"], hb, route, lstrict)


def _moe_expert_kernel(xt_ref, elo_ref, ehi_ref, valid_ref, x_ref, w_ref, wg0, wu0, wd0, wg1, wu1, wd1, o_ref):
    t = pl.program_id(0)

    @pl.when(valid_ref[t] > 0)
    def _():
        x = x_ref[...]
        w = w_ref[...]
        gates = [jnp.dot(x, wg[0], preferred_element_type=F32) for wg in (wg0, wg1)]
        ups = [jnp.dot(x, wu[0], preferred_element_type=F32) for wu in (wu0, wu1)]
        hes = [(_silu(gates[i]) * ups[i] * w[:, lane:lane + 1]).astype(BF16)
               for i, lane in enumerate((ROUTE_W_LO, ROUTE_W_HI))]
        o_ref[...] = (jnp.dot(hes[0], wd0[0], preferred_element_type=F32)
                      + jnp.dot(hes[1], wd1[0], preferred_element_type=F32)).astype(o_ref.dtype)

    @pl.when(valid_ref[t] == 0)
    def _():
        o_ref[...] = jnp.zeros_like(o_ref)


def _moe_experts(plan, xg, wsg, wg, wu, wd):
    rows, d = xg.shape
    tm = MOE_TM
    tok = lambda width: pl.BlockSpec((tm, width), lambda t, xt, elo, ehi, valid: (xt[t], 0))
    lo3 = lambda shape: pl.BlockSpec(shape, lambda t, xt, elo, ehi, valid: (elo[t], 0, 0))
    hi3 = lambda shape: pl.BlockSpec(shape, lambda t, xt, elo, ehi, valid: (ehi[t], 0, 0))
    return pl.pallas_call(
        _moe_expert_kernel,
        out_shape=jax.ShapeDtypeStruct((rows, d), BF16),
        grid_spec=pltpu.PrefetchScalarGridSpec(
            num_scalar_prefetch=4,
            grid=(rows // tm,),
            in_specs=[tok(d), tok(LANES),
                      lo3((1, d, D_EXPERT)), lo3((1, d, D_EXPERT)), lo3((1, D_EXPERT, d)),
                      hi3((1, d, D_EXPERT)), hi3((1, d, D_EXPERT)), hi3((1, D_EXPERT, d))],
            out_specs=pl.BlockSpec((tm, d), lambda t, *_: (t, 0))),
        compiler_params=pltpu.CompilerParams(dimension_semantics=("arbitrary",), vmem_limit_bytes=VMEM_LIMIT),
        name="moe_experts",
    )(plan["xtile"], plan["elo"], plan["ehi"], plan["valid"], xg, wsg, wg, wu, wd, wg, wu, wd)


def _moe_unsort_kernel(gmap_ref, og_ref, route_ref, h_ref, lstrict_ref, g_ref, b_ref, out_ref, ol_ref, sem):
    s = pl.program_id(0)
    nsteps = pl.num_programs(0)
    slot = s & 1
    ts = route_ref.shape[0]

    def gather(step, fn):
        sl = step & 1

        def body(g, carry):
            src = pl.ds(pl.multiple_of(gmap_ref[step * LGRAN + g] * GRAN, GRAN), GRAN)
            dst = pl.ds(pl.multiple_of(g * GRAN, GRAN), GRAN)
            fn(pltpu.make_async_copy(og_ref.at[src, :], ol_ref.at[sl, dst, :], sem.at[sl]))
            return carry
        lax.fori_loop(0, LGRAN, body, 0)

    @pl.when(s == 0)
    def _():
        gather(s, lambda cp: cp.start())

    @pl.when(s + 1 < nsteps)
    def _():
        gather(s + 1, lambda cp: cp.start())

    route = route_ref[...]
    bucket_col = route[:, ROUTE_BUCKET:ROUTE_BUCKET + 1].astype(jnp.int32)
    lane = lax.broadcasted_iota(jnp.int32, (ts, LANES), 1)
    ohf = jnp.where(lane == bucket_col, 1.0, 0.0)
    cnt = jnp.sum(ohf, axis=0, keepdims=True).astype(jnp.int32)
    pad = (((cnt + (GRAN - 1)) >> GRAN_SHIFT) << GRAN_SHIFT).astype(F32)
    r = lax.broadcasted_iota(jnp.int32, (LANES, LANES), 0)
    c = lax.broadcasted_iota(jnp.int32, (LANES, LANES), 1)
    loff = _dot(jnp.broadcast_to(pad, (8, LANES)), jnp.where(r < c, 1.0, 0.0))[0:1, :]
    rank = jnp.dot(lstrict_ref[...], ohf.astype(BF16), preferred_element_type=F32)
    dest = jnp.sum(ohf * (loff + rank), axis=1, keepdims=True).astype(jnp.int32)
    lrow = lax.broadcasted_iota(jnp.int32, (ts, LROWS), 1)
    perm_t = jnp.where(lrow == dest, 1.0, 0.0).astype(BF16)

    gather(s, lambda cp: cp.wait())
    ffn = jnp.dot(perm_t, ol_ref[slot], preferred_element_type=F32)
    out_ref[...] = _layer_norm(DEEPNORM_ALPHA * h_ref[...] + ffn, g_ref[...], b_ref[...])


def _moe_unsort(plan, og, route, hf, lstrict, g2, b2):
    n, d = hf.shape
    ts = MOE_TS
    row = pl.BlockSpec((1, d), lambda s, *_: (0, 0))
    return pl.pallas_call(
        _moe_unsort_kernel,
        out_shape=jax.ShapeDtypeStruct((n, d), F32),
        grid_spec=pltpu.PrefetchScalarGridSpec(
            num_scalar_prefetch=1,
            grid=(n // ts,),
            in_specs=[pl.BlockSpec(memory_space=pl.ANY), pl.BlockSpec((ts, LANES), lambda s, *_: (s, 0)),
                      pl.BlockSpec((ts, d), lambda s, *_: (s, 0)), pl.BlockSpec((ts, ts), lambda s, *_: (0, 0)),
                      row, row],
            out_specs=pl.BlockSpec((ts, d), lambda s, *_: (s, 0)),
            scratch_shapes=[pltpu.VMEM((2, LROWS, d), BF16), pltpu.SemaphoreType.DMA((2,))]),
        compiler_params=pltpu.CompilerParams(dimension_semantics=("arbitrary",), vmem_limit_bytes=VMEM_LIMIT),
        name="moe_unsort",
    )(plan["gmap_back"], og, route, hf, lstrict, g2, b2)


def _moe_plan(cnt_half, n):
    nsrc = n // MOE_TS
    i32 = jnp.int32
    cnt = cnt_half.reshape(nsrc, -1, LANES).sum(axis=1)[:, :N_BUCKETS].astype(i32)
    run_g = (cnt + GRAN - 1) // GRAN
    nvalid = run_g.sum(axis=1)
    loff_g = jnp.cumsum(run_g, axis=1) - run_g
    bucket_g = run_g.sum(axis=0)
    gpt = MOE_TM // GRAN
    btiles = (bucket_g + gpt - 1) // gpt
    tend = jnp.cumsum(btiles)
    tstart = tend - btiles
    gofs = tstart[None, :] * gpt + jnp.cumsum(run_g, axis=0) - run_g
    n_tiles = -(-(n + nsrc * N_BUCKETS * (GRAN - 1)) // MOE_TM) + N_BUCKETS + 1
    g = jnp.arange(LGRAN, dtype=i32)[None, :, None]
    ends = jnp.cumsum(run_g, axis=1)[:, None, :]
    b_of_g = jnp.minimum(jnp.sum(g >= ends, axis=2), N_BUCKETS - 1)
    gmap = (jnp.take_along_axis(gofs, b_of_g, axis=1) + jnp.arange(LGRAN, dtype=i32)[None, :]
            - jnp.take_along_axis(loff_g, b_of_g, axis=1))
    is_valid = jnp.arange(LGRAN, dtype=i32)[None, :] < nvalid[:, None]
    zero_gran = (n_tiles - 1) * gpt
    t = jnp.arange(n_tiles, dtype=i32)
    tb = jnp.minimum(jnp.sum(t[:, None] >= tend[None, :], axis=1), N_BUCKETS - 1)
    valid = (t < tend[-1]).astype(i32)
    pair_a = jnp.asarray([a for a in range(EXPERTS_PER_GROUP) for b in range(a + 1, EXPERTS_PER_GROUP)], i32)
    pair_b = jnp.asarray([b for a in range(EXPERTS_PER_GROUP) for b in range(a + 1, EXPERTS_PER_GROUP)], i32)
    grp = tb // N_PAIRS
    return {
        "n_tiles": n_tiles,
        "gmap": jnp.where(is_valid, gmap, 0).reshape(-1).astype(i32),
        "gmap_back": jnp.where(is_valid, gmap, zero_gran).reshape(-1).astype(i32),
        "nvalid": nvalid.astype(i32),
        "tail0": jnp.concatenate([tstart * gpt + bucket_g, tend[-1:]]).astype(i32),
        "taillen": (btiles * gpt - bucket_g).astype(i32),
        "xtile": jnp.where(valid > 0, t, 0).astype(i32),
        "elo": (grp * EXPERTS_PER_GROUP + pair_a[tb % N_PAIRS]).astype(i32),
        "ehi": (grp * EXPERTS_PER_GROUP + pair_b[tb % N_PAIRS]).astype(i32),
        "valid": valid,
    }


def _pad_lanes(a, lane0=0):
    return jnp.zeros((1, LANES), F32).at[0, lane0:lane0 + a.shape[0]].set(a.astype(F32))


def _rope_tables(seq):
    half = HEAD_DIM // 2
    inv_freq = ROPE_THETA ** (-jnp.arange(half, dtype=F32) / half)
    ang = jnp.arange(seq).astype(F32)[:, None] * inv_freq[None, :]
    cos, sin = jnp.cos(ang), jnp.sin(ang)
    return jnp.concatenate([cos, cos], axis=-1), jnp.concatenate([-sin, sin], axis=-1)


def _layer(x, w_in, conv_w, a_log, dt_bias, dn_norm_w, w_out, ln1_g, ln1_b, router_w1, router_b1,
           router_w2, router_b2, w_gate, w_up, w_down, ln2_g, ln2_b):
    bsz, seq, d = x.shape
    n = bsz * seq
    x2 = x.reshape(n, d)

    o_z, o_b, o_mb = 3 * D_DN, 4 * D_DN, 4 * D_DN + 2 * N_HEADS_DN
    w_ba = jnp.pad(w_in[:, o_b:o_mb], ((0, 0), (0, LANES - 2 * N_HEADS_DN)))
    w_all = jnp.concatenate([w_in[:, :o_z], w_in[:, o_z:o_b], w_ba, w_in[:, o_mb:]], axis=1).astype(BF16)

    tm = min(512, n)
    dn_qkv, z, ba, mb_qkv = _in_proj(x2, w_all, tm)

    y_dn = _deltanet(dn_qkv.reshape(bsz, seq, 3 * D_DN), z.reshape(bsz, seq, D_DN), ba.reshape(bsz, seq, LANES),
                     conv_w, _pad_lanes(a_log, N_HEADS_DN), _pad_lanes(dt_bias, N_HEADS_DN),
                     dn_norm_w.astype(F32).reshape(1, HEAD_DIM))

    cos_t, sin_t = _rope_tables(seq)
    q_r, k_r, vt, sel = _moba_prep(mb_qkv.reshape(bsz, seq, 3 * D_MOBA), cos_t, sin_t)
    y_mb = _moba_attn(q_r, k_r, vt, sel)

    rw = jnp.concatenate([router_w1, jnp.transpose(router_w2, (1, 0, 2)).reshape(d, N_EXPERTS)], axis=1)
    rw = jnp.pad(rw, ((0, 0), (0, LANES - rw.shape[1])))
    rb = _pad_lanes(jnp.concatenate([router_b1, router_b2.reshape(-1)]))
    hf, hb, route, cnt = _mix_route(y_dn.reshape(n, D_DN), y_mb.reshape(n, D_MOBA), x2, w_out.astype(BF16),
                                    ln1_g.reshape(1, d), ln1_b.reshape(1, d), rw, rb, tm)

    plan = _moe_plan(cnt, n)
    idx = jnp.arange(MOE_TS, dtype=jnp.int32)
    lstrict = (idx[None, :] < idx[:, None]).astype(BF16)
    xg, wsg = _moe_sort(plan, hb, route, lstrict)
    og = _moe_experts(plan, xg, wsg, w_gate.astype(BF16), w_up.astype(BF16), w_down.astype(BF16))
    out = _moe_unsort(plan, og, route, hf, lstrict, ln2_g.reshape(1, d), ln2_b.reshape(1, d))
    return out.reshape(bsz, seq, d)


def kernel(x, w_in, conv_w, a_log, dt_bias, dn_norm_w, w_out, ln1_g, ln1_b, router_w1, router_b1, router_w2, router_b2, expert_w_gate, expert_w_up, expert_w_down, ln2_g, ln2_b):
    for l in range(DEPTH):
        x = _layer(x, w_in[l], conv_w[l], a_log[l], dt_bias[l], dn_norm_w[l], w_out[l], ln1_g[l], ln1_b[l],
                   router_w1[l], router_b1[l], router_w2[l], router_b2[l], expert_w_gate[l], expert_w_up[l],
                   expert_w_down[l], ln2_g[l], ln2_b[l])
    return x
```

```python
import functools

import jax
import jax.numpy as jnp
from jax import lax
from jax.experimental import pallas as pl
from jax.experimental.pallas import tpu as pltpu

F32 = jnp.float32
BF16 = jnp.bfloat16

HEAD_DIM = 128
N_HEADS_DN = 4
N_HEADS_MOBA = 4
D_DN = N_HEADS_DN * HEAD_DIM
D_MOBA = N_HEADS_MOBA * HEAD_DIM
CONV_K = 4
DN_CHUNK = 64
MOBA_BLOCK = 256
MOBA_TOPK = 3
ROPE_THETA = 10000.0
N_GROUPS = 4
EXPERTS_PER_GROUP = 4
N_EXPERTS = N_GROUPS * EXPERTS_PER_GROUP
D_EXPERT = 256
LN_EPS = 1e-5
RMS_EPS = 1e-6
L2_EPS = 1e-6
NEG_INF = -1e30
DEPTH = 1
DEEPNORM_ALPHA = (2 * DEPTH) ** 0.25

LANES = 128
DN_TILE = 256
DN_HEADS_PER_STEP = 4
GATE_LANE0 = N_GROUPS
N_PAIRS = EXPERTS_PER_GROUP * (EXPERTS_PER_GROUP - 1) // 2
N_BUCKETS = N_GROUPS * N_PAIRS
ROUTE_BUCKET, ROUTE_W_LO, ROUTE_W_HI = 0, 1, 2
MOE_TS = 512
MOE_TM = 256
GRAN = 16
GRAN_SHIFT = 4
LROWS = -(-(MOE_TS + N_BUCKETS * (GRAN - 1)) // LANES) * LANES
LGRAN = LROWS // GRAN
VMEM_LIMIT = 48 * 1024 * 1024


def _dot(a, b):
    return jnp.dot(a.astype(BF16), b.astype(BF16), preferred_element_type=F32)


def _dot_nt(a, b):
    return lax.dot_general(a.astype(BF16), b.astype(BF16), (((1,), (1,)), ((), ())),
                           preferred_element_type=F32)


def _split2(a):
    hi = a.astype(BF16)
    lo = (a - hi.astype(F32)).astype(BF16)
    return hi, lo


def _split3(a):
    hi = a.astype(BF16)
    r = a - hi.astype(F32)
    mid = r.astype(BF16)
    lo = (r - mid.astype(F32)).astype(BF16)
    return hi, mid, lo


def _dot3(a, b):
    ah, al = _split2(a)
    bh, bl = _split2(b)
    return (jnp.dot(ah, bh, preferred_element_type=F32) + jnp.dot(ah, bl, preferred_element_type=F32)
            + jnp.dot(al, bh, preferred_element_type=F32))


def _dot3_nt(a, b):
    ah, al = _split2(a)
    bh, bl = _split2(b)
    dn = (((1,), (1,)), ((), ()))
    return (lax.dot_general(ah, bh, dn, preferred_element_type=F32)
            + lax.dot_general(ah, bl, dn, preferred_element_type=F32)
            + lax.dot_general(al, bh, dn, preferred_element_type=F32))


def _dot_exact_lhs(a_bf16, b):
    bh, bm, bl = _split3(b)
    return (jnp.dot(a_bf16, bh, preferred_element_type=F32) + jnp.dot(a_bf16, bm, preferred_element_type=F32)
            + jnp.dot(a_bf16, bl, preferred_element_type=F32))


def _silu(x):
    return x * jax.nn.sigmoid(x)


def _softplus(x):
    return jnp.maximum(x, 0.0) + jnp.log1p(jnp.exp(-jnp.abs(x)))


def _layer_norm(t, g, b):
    mu = jnp.mean(t, axis=-1, keepdims=True)
    d = t - mu
    var = jnp.mean(d * d, axis=-1, keepdims=True)
    return d * lax.rsqrt(var + LN_EPS) * g + b


def _lane_pick(x, lane):
    ids = lax.broadcasted_iota(jnp.int32, x.shape, 1)
    return jnp.sum(jnp.where(ids == lane, x, 0.0), axis=1, keepdims=True)


def _in_proj_kernel(x_ref, w_ref, dn_ref, z_ref, ba_ref, mb_ref):
    xb = x_ref[...].astype(BF16)
    o0 = 3 * D_DN
    o1 = o0 + D_DN
    o2 = o1 + LANES
    dn_ref[...] = jnp.dot(xb, w_ref[:, 0:o0], preferred_element_type=F32)
    z_ref[...] = jnp.dot(xb, w_ref[:, o0:o1], preferred_element_type=F32)
    ba_ref[...] = jnp.dot(xb, w_ref[:, o1:o2], preferred_element_type=F32)
    mb_ref[...] = jnp.dot(xb, w_ref[:, o2:o2 + 3 * D_MOBA], preferred_element_type=F32)


def _in_proj(x2, w_all, tm):
    n, d = x2.shape
    wc = w_all.shape[1]
    return pl.pallas_call(
        _in_proj_kernel,
        out_shape=(jax.ShapeDtypeStruct((n, 3 * D_DN), F32), jax.ShapeDtypeStruct((n, D_DN), F32),
                   jax.ShapeDtypeStruct((n, LANES), F32), jax.ShapeDtypeStruct((n, 3 * D_MOBA), F32)),
        grid=(n // tm,),
        in_specs=[pl.BlockSpec((tm, d), lambda i: (i, 0)), pl.BlockSpec((d, wc), lambda i: (0, 0))],
        out_specs=(pl.BlockSpec((tm, 3 * D_DN), lambda i: (i, 0)), pl.BlockSpec((tm, D_DN), lambda i: (i, 0)),
                   pl.BlockSpec((tm, LANES), lambda i: (i, 0)), pl.BlockSpec((tm, 3 * D_MOBA), lambda i: (i, 0))),
        compiler_params=pltpu.CompilerParams(dimension_semantics=("parallel",), vmem_limit_bytes=VMEM_LIMIT),
        name="in_proj",
    )(x2, w_all)


def _inv_unit_lower(a_list, row, col):
    eye = (row == col).astype(F32)
    d8 = (row >> 3) == (col >> 3)
    a8 = [jnp.where(d8, a, 0.0) for a in a_list]
    a8_2 = [_dot(a, a) for a in a8]
    a8_4 = [_dot(a, a) for a in a8_2]
    x = [_dot(eye - a, eye + a2) for a, a2 in zip(a8, a8_2)]
    x = [_dot(xi, eye + a4) for xi, a4 in zip(x, a8_4)]
    s = 8
    while s < DN_CHUNK:
        sh = s.bit_length() - 1
        off = ((row >> (sh + 1)) == (col >> (sh + 1))) & ((row >> sh) != (col >> sh))
        y = [_dot(jnp.where(off, a, 0.0), xi) for a, xi in zip(a_list, x)]
        x = [xi - _dot(xi, yi) for xi, yi in zip(x, y)]
        s *= 2
    return x


def _deltanet_heads(q, k, v, z, beta, gcc, gcr, state, normw, masks):
    row, col, incl, strict = masks
    nh = len(q)
    hs = range(nh)
    tt = q[0].shape[0]
    nchunk = tt // DN_CHUNK
    q = [x * lax.rsqrt(jnp.sum(x * x, axis=-1, keepdims=True) + L2_EPS) * (HEAD_DIM ** -0.5) for x in q]
    k = [x * lax.rsqrt(jnp.sum(x * x, axis=-1, keepdims=True) + L2_EPS) for x in k]

    decay = [jnp.where(incl, jnp.exp(jnp.where(incl, gcc[h] - gcr[h], 0.0)), 0.0) for h in hs]
    kb = [k[h] * beta[h] for h in hs]
    vb = [v[h] * beta[h] for h in hs]
    a_mat = [jnp.where(strict, _dot_nt(kb[h], k[h]) * decay[h], 0.0) for h in hs]
    qk = [_dot_nt(q[h], k[h]) * decay[h] for h in hs]
    tinv = _inv_unit_lower(a_mat, row, col)

    eg = [jnp.exp(g) for g in gcc]
    wu = [_dot(tinv[h], jnp.concatenate([kb[h] * eg[h], vb[h]], axis=1)) for h in hs]
    qd = [q[h] * eg[h] for h in hs]

    gl_rows = [[g[(c + 1) * DN_CHUNK - 1:(c + 1) * DN_CHUNK, :] for c in range(nchunk)] for g in gcc]
    gl_col = [jnp.concatenate([jnp.broadcast_to(g, (DN_CHUNK, 1)) for g in rows], axis=0) for rows in gl_rows]
    kdt = [jnp.transpose(k[h] * jnp.exp(gl_col[h] - gcc[h])) for h in hs]

    outs = [[] for _ in hs]
    for c in range(nchunk):
        lo, hi = c * DN_CHUNK, (c + 1) * DN_CHUNK
        r = [_dot(jnp.concatenate([wu[h][lo:hi, 0:HEAD_DIM], qd[h][lo:hi, :]], axis=0), state[h]) for h in hs]
        vz = []
        for h in hs:
            parts = []
            if lo > 0:
                parts.append(jnp.zeros((lo, HEAD_DIM), F32))
            parts.append(wu[h][lo:hi, HEAD_DIM:2 * HEAD_DIM] - r[h][0:DN_CHUNK, :])
            if hi < tt:
                parts.append(jnp.zeros((tt - hi, HEAD_DIM), F32))
            vz.append(jnp.concatenate(parts, axis=0).astype(BF16))
        for h in hs:
            outs[h].append(r[h][DN_CHUNK:2 * DN_CHUNK, :] + _dot(qk[h][lo:hi, :], vz[h]))
        state = [state[h] * jnp.exp(gl_rows[h][c]) + _dot(kdt[h], vz[h]) for h in hs]

    ys = []
    for h in hs:
        o = jnp.concatenate(outs[h], axis=0)
        o = o * lax.rsqrt(jnp.mean(o * o, axis=-1, keepdims=True) + RMS_EPS) * normw
        ys.append(o * _silu(z[h]))
    return ys, state


def _deltanet_kernel(q_ref, k_ref, v_ref, z_ref, ba_ref, cwq_ref, cwk_ref, cwv_ref, alog_ref, dtb_ref,
                     normw_ref, y_ref, cbq, cbk, cbv, s_ref, *, hb):
    hg = pl.program_id(1)
    t = pl.program_id(2)
    tt = DN_TILE

    @pl.when(t == 0)
    def _():
        zero8 = jnp.zeros((8, hb * HEAD_DIM), F32)
        cbq[0:8, :] = zero8
        cbk[0:8, :] = zero8
        cbv[0:8, :] = zero8
        s_ref[...] = jnp.zeros_like(s_ref)

    def conv_silu(u_ref, cb, cw_ref):
        u = u_ref[0]
        cb[8:8 + tt, :] = u
        acc = cw_ref[CONV_K - 1:CONV_K, :] * u
        for s in range(1, CONV_K):
            acc = acc + cw_ref[CONV_K - 1 - s:CONV_K - s, :] * cb[8 - s:8 - s + tt, :]
        cb[0:8, :] = u[tt - 8:tt, :]
        return _silu(acc)

    q_all = conv_silu(q_ref, cbq, cwq_ref)
    k_all = conv_silu(k_ref, cbk, cwk_ref)
    v_all = conv_silu(v_ref, cbv, cwv_ref)

    ba = ba_ref[0]
    beta_all = jax.nn.sigmoid(ba)
    g_all = -jnp.exp(alog_ref[...]) * _softplus(ba + dtb_ref[...])

    row = lax.broadcasted_iota(jnp.int32, (tt, tt), 0)
    col = lax.broadcasted_iota(jnp.int32, (tt, tt), 1)
    same = (row >> 6) == (col >> 6)
    incl = same & (row >= col)
    strict = same & (row > col)
    masks = (row, col, incl, strict)

    gc_all = _dot_exact_lhs(incl.astype(BF16), g_all)
    gct = jnp.transpose(gc_all)
    sub = lax.broadcasted_iota(jnp.int32, gct.shape, 0)

    sls = [slice(hh * HEAD_DIM, (hh + 1) * HEAD_DIM) for hh in range(hb)]
    heads = [hg * hb + hh for hh in range(hb)]
    beta = [_lane_pick(beta_all, h) for h in heads]
    gcc = [_lane_pick(gc_all, h + N_HEADS_DN) for h in heads]
    gcr = [jnp.sum(jnp.where(sub == h + N_HEADS_DN, gct, 0.0), axis=0, keepdims=True) for h in heads]
    ys, states = _deltanet_heads([q_all[:, sl] for sl in sls], [k_all[:, sl] for sl in sls],
                                 [v_all[:, sl] for sl in sls], [z_ref[0, :, sl] for sl in sls], beta, gcc, gcr,
                                 [s_ref[hh] for hh in range(hb)], normw_ref[...], masks)
    s_ref[...] = jnp.stack(states, axis=0)
    y_ref[0] = jnp.concatenate(ys, axis=1).astype(y_ref.dtype)


def _deltanet(dn_qkv, z, ba, conv_w, alog_row, dtb_row, normw_row):
    bsz, seq, _ = dn_qkv.shape
    tt = DN_TILE
    hb = DN_HEADS_PER_STEP
    ng = N_HEADS_DN // hb
    w = hb * HEAD_DIM

    def col_spec(off):
        return pl.BlockSpec((1, tt, w), lambda b, g, t: (b, t, g + off))

    def cw_spec(off):
        return pl.BlockSpec((CONV_K, w), lambda b, g, t: (0, g + off))

    row_spec = pl.BlockSpec((1, LANES), lambda b, g, t: (0, 0))
    return pl.pallas_call(
        functools.partial(_deltanet_kernel, hb=hb),
        out_shape=jax.ShapeDtypeStruct((bsz, seq, D_DN), BF16),
        grid=(bsz, ng, seq // tt),
        in_specs=[col_spec(0), col_spec(ng), col_spec(2 * ng), col_spec(0),
                  pl.BlockSpec((1, tt, LANES), lambda b, g, t: (b, t, 0)),
                  cw_spec(0), cw_spec(ng), cw_spec(2 * ng), row_spec, row_spec, row_spec],
        out_specs=col_spec(0),
        scratch_shapes=[pltpu.VMEM((8 + tt, w), F32), pltpu.VMEM((8 + tt, w), F32),
                        pltpu.VMEM((8 + tt, w), F32), pltpu.VMEM((hb, HEAD_DIM, HEAD_DIM), F32)],
        compiler_params=pltpu.CompilerParams(dimension_semantics=("parallel", "parallel", "arbitrary"),
                                             vmem_limit_bytes=VMEM_LIMIT),
        name="deltanet",
    )(dn_qkv, dn_qkv, dn_qkv, z, ba, conv_w, conv_w, conv_w, alog_row, dtb_row, normw_row)


def _moba_prep_kernel(x_ref, cos_ref, sin_ref, q_ref, k_ref, vt_ref, sel_ref, km_ref, *, nb, topk):
    j = pl.program_id(1)

    @pl.when(j == 0)
    def _():
        km_ref[...] = jnp.zeros_like(km_ref)

    cos = cos_ref[...]
    sin = sin_ref[...]
    half = HEAD_DIM // 2
    blk = lax.broadcasted_iota(jnp.int32, (nb, MOBA_BLOCK), 0)
    kmeans = []
    for h in range(N_HEADS_MOBA):
        qh = x_ref[0, :, h * HEAD_DIM:(h + 1) * HEAD_DIM]
        kh = x_ref[0, :, D_MOBA + h * HEAD_DIM:D_MOBA + (h + 1) * HEAD_DIM]
        qr = (qh * cos + pltpu.roll(qh, half, 1) * sin) * (HEAD_DIM ** -0.5)
        kr = kh * cos + pltpu.roll(kh, half, 1) * sin
        q_ref[0, :, h * HEAD_DIM:(h + 1) * HEAD_DIM] = qr.astype(q_ref.dtype)
        k_ref[0, :, h * HEAD_DIM:(h + 1) * HEAD_DIM] = kr.astype(k_ref.dtype)
        kmeans.append(jnp.mean(kr, axis=0, keepdims=True))

        gate = _dot3_nt(km_ref[:, h * HEAD_DIM:(h + 1) * HEAD_DIM], qr)
        gate = jnp.where(blk < j, gate, NEG_INF)
        rank = jnp.zeros(gate.shape, F32)
        for m in range(nb):
            gm = gate[m:m + 1, :]
            ahead = (gm > gate) | ((gm == gate) & (blk > m))
            rank = rank + jnp.where(ahead, 1.0, 0.0)
        sel = (blk < j) & (rank < topk)
        sel_ref[0, 0, h * nb:(h + 1) * nb, :] = jnp.where(sel, 1.0, 0.0)

    km_ref[pl.ds(j, 1), :] = jnp.concatenate(kmeans, axis=1)
    vt_ref[0, 0] = jnp.transpose(x_ref[0, :, 2 * D_MOBA:3 * D_MOBA]).astype(vt_ref.dtype)


def _moba_prep(mb_qkv, cos_t, sin_t):
    bsz, seq, _ = mb_qkv.shape
    nb = seq // MOBA_BLOCK
    topk = min(MOBA_TOPK, nb)
    kern = functools.partial(_moba_prep_kernel, nb=nb, topk=topk)
    tok_spec = pl.BlockSpec((1, MOBA_BLOCK, D_MOBA), lambda b, j: (b, j, 0))
    tab_spec = pl.BlockSpec((MOBA_BLOCK, HEAD_DIM), lambda b, j: (j, 0))
    return pl.pallas_call(
        kern,
        out_shape=(jax.ShapeDtypeStruct((bsz, seq, D_MOBA), BF16), jax.ShapeDtypeStruct((bsz, seq, D_MOBA), BF16),
                   jax.ShapeDtypeStruct((bsz, nb, D_MOBA, MOBA_BLOCK), BF16),
                   jax.ShapeDtypeStruct((bsz, nb, N_HEADS_MOBA * nb, MOBA_BLOCK), F32)),
        grid=(bsz, nb),
        in_specs=[pl.BlockSpec((1, MOBA_BLOCK, 3 * D_MOBA), lambda b, j: (b, j, 0)), tab_spec, tab_spec],
        out_specs=(tok_spec, tok_spec,
                   pl.BlockSpec((1, 1, D_MOBA, MOBA_BLOCK), lambda b, j: (b, j, 0, 0)),
                   pl.BlockSpec((1, 1, N_HEADS_MOBA * nb, MOBA_BLOCK), lambda b, j: (b, j, 0, 0))),
        scratch_shapes=[pltpu.VMEM((nb, D_MOBA), F32)],
        compiler_params=pltpu.CompilerParams(dimension_semantics=("parallel", "arbitrary"),
                                             vmem_limit_bytes=VMEM_LIMIT),
        name="moba_prep",
    )(mb_qkv, cos_t, sin_t)


def _moba_attn_kernel(q_ref, k_ref, vt_ref, sel_ref, o_ref, acc_ref, *, nb):
    j = pl.program_id(1)
    blk = MOBA_BLOCK
    cw = 2 * blk
    nh = N_HEADS_MOBA
    nc = (j + 1) // 2
    dn = (((1,), (1,)), ((), ()))
    hsl = [slice(h * HEAD_DIM, (h + 1) * HEAD_DIM) for h in range(nh)]
    qs = [q_ref[0, :, hsl[h]] for h in range(nh)]

    ki = lax.broadcasted_iota(jnp.int32, (blk, blk), 0)
    qi = lax.broadcasted_iota(jnp.int32, (blk, blk), 1)
    own = pl.ds(pl.multiple_of(j * blk, blk), blk)
    s_own = [jnp.where(ki <= qi, lax.dot_general(k_ref[0, own, hsl[h]], qs[h], dn, preferred_element_type=F32),
                       NEG_INF) for h in range(nh)]
    ms, ls = [], []
    for h in range(nh):
        m = jnp.max(s_own[h], axis=0, keepdims=True)
        p = jnp.exp(s_own[h] - m)
        ms.append(m)
        ls.append(jnp.sum(p, axis=0, keepdims=True))
        acc_ref[h] = jnp.dot(vt_ref[0, j, hsl[h], :], p.astype(BF16), preferred_element_type=F32)

    def body(c, carry):
        ms, ls = carry
        ms, ls = list(ms), list(ls)
        ss = []
        for h in range(nh):
            kc = k_ref[0, pl.ds(pl.multiple_of(c * cw, cw), cw), hsl[h]]
            s = lax.dot_general(kc, qs[h], dn, preferred_element_type=F32)
            parts = []
            for i in range(2):
                selrow = sel_ref[0, 0, pl.ds(h * nb + 2 * c + i, 1), :]
                parts.append(jnp.where(selrow > 0.5, s[i * blk:(i + 1) * blk, :], NEG_INF))
            ss.append(jnp.concatenate(parts, axis=0))
        for h in range(nh):
            m_new = jnp.maximum(ms[h], jnp.max(ss[h], axis=0, keepdims=True))
            alpha = jnp.exp(ms[h] - m_new)
            p = jnp.exp(ss[h] - m_new)
            ls[h] = alpha * ls[h] + jnp.sum(p, axis=0, keepdims=True)
            ms[h] = m_new
            pb = p.astype(BF16)
            pv = (jnp.dot(vt_ref[0, 2 * c, hsl[h], :], pb[0:blk, :], preferred_element_type=F32)
                  + jnp.dot(vt_ref[0, 2 * c + 1, hsl[h], :], pb[blk:cw, :], preferred_element_type=F32))
            acc_ref[h] = acc_ref[h] * alpha + pv
        return tuple(ms), tuple(ls)

    ms, ls = lax.fori_loop(0, nc, body, (tuple(ms), tuple(ls)))
    o_ref[0] = jnp.concatenate([jnp.transpose(acc_ref[h] / ls[h]) for h in range(nh)],
                               axis=1).astype(o_ref.dtype)


def _moba_attn(q_r, k_r, vt, sel):
    bsz, seq, _ = q_r.shape
    nb = seq // MOBA_BLOCK
    tok_spec = pl.BlockSpec((1, MOBA_BLOCK, D_MOBA), lambda b, j: (b, j, 0))
    return pl.pallas_call(
        functools.partial(_moba_attn_kernel, nb=nb),
        out_shape=jax.ShapeDtypeStruct((bsz, seq, D_MOBA), BF16),
        grid=(bsz, nb),
        in_specs=[tok_spec,
                  pl.BlockSpec((1, seq, D_MOBA), lambda b, j: (b, 0, 0)),
                  pl.BlockSpec((1, nb, D_MOBA, MOBA_BLOCK), lambda b, j: (b, 0, 0, 0)),
                  pl.BlockSpec((1, 1, N_HEADS_MOBA * nb, MOBA_BLOCK), lambda b, j: (b, j, 0, 0))],
        out_specs=tok_spec,
        scratch_shapes=[pltpu.VMEM((N_HEADS_MOBA, HEAD_DIM, MOBA_BLOCK), F32)],
        compiler_params=pltpu.CompilerParams(dimension_semantics=("parallel", "arbitrary"),
                                             vmem_limit_bytes=VMEM_LIMIT),
        name="moba_attn",
    )(q_r, k_r, vt, sel)


def _mix_route_kernel(ydn_ref, ymb_ref, x_ref, wo_ref, g_ref, b_ref, rw_ref, rb_ref, h_ref, hb_ref, route_ref,
                      cnt_ref):
    mix = (jnp.dot(ydn_ref[...], wo_ref[0:D_DN, :], preferred_element_type=F32)
           + jnp.dot(ymb_ref[...], wo_ref[D_DN:D_DN + D_MOBA, :], preferred_element_type=F32))
    hval = _layer_norm(DEEPNORM_ALPHA * x_ref[...] + mix, g_ref[...], b_ref[...])
    h_ref[...] = hval
    hb_ref[...] = hval.astype(BF16)

    hh, hl = _split2(hval)
    wh, wl = _split2(rw_ref[...])
    both = jnp.dot(hh, jnp.concatenate([wh, wl], axis=1), preferred_element_type=F32)
    logits = (both[:, 0:LANES] + both[:, LANES:2 * LANES] + jnp.dot(hl, wh, preferred_element_type=F32)
              + rb_ref[...])
    lane = lax.broadcasted_iota(jnp.int32, logits.shape, 1)
    big = jnp.int32(LANES)

    def first_lane(mask):
        return jnp.min(jnp.where(mask, lane, big), axis=1, keepdims=True)

    is_g = lane < N_GROUPS
    m1 = jnp.max(jnp.where(is_g, logits, NEG_INF), axis=1, keepdims=True)
    s1 = jnp.sum(jnp.where(is_g, jnp.exp(logits - m1), 0.0), axis=1, keepdims=True)
    pg = 1.0 / s1
    gsel = first_lane(is_g & (logits == m1))

    in_grp = (lane >= GATE_LANE0) & (((lane - GATE_LANE0) >> 2) == gsel) & (lane < GATE_LANE0 + N_EXPERTS)
    m2 = jnp.max(jnp.where(in_grp, logits, NEG_INF), axis=1, keepdims=True)
    s2 = jnp.sum(jnp.where(in_grp, jnp.exp(logits - m2), 0.0), axis=1, keepdims=True)
    e1 = first_lane(in_grp & (logits == m2))
    rest = in_grp & (lane != e1)
    m2b = jnp.max(jnp.where(rest, logits, NEG_INF), axis=1, keepdims=True)
    e2 = first_lane(rest & (logits == m2b))
    pe1 = 1.0 / s2
    pe2 = jnp.exp(m2b - m2) / s2
    tot = pe1 + pe2
    w1 = pg * (pe1 / tot)
    w2 = pg * (pe2 / tot)
    first_lo = e1 < e2
    lo = jnp.minimum(e1, e2)
    hi = jnp.maximum(e1, e2)
    a = (lo - GATE_LANE0) & (EXPERTS_PER_GROUP - 1)
    b = (hi - GATE_LANE0) & (EXPERTS_PER_GROUP - 1)
    bucket = gsel * N_PAIRS + ((a * (2 * EXPERTS_PER_GROUP - 1 - a)) >> 1) + (b - a - 1)
    route_ref[...] = jnp.where(lane == ROUTE_BUCKET, bucket.astype(F32),
                               jnp.where(lane == ROUTE_W_LO, jnp.where(first_lo, w1, w2),
                                         jnp.where(lane == ROUTE_W_HI, jnp.where(first_lo, w2, w1), 0.0)))
    cnt_ref[0] = jnp.sum(jnp.where(lane == bucket, 1.0, 0.0), axis=0, keepdims=True)


def _mix_route(y_dn, y_mb, x2, wo, g1, b1, rw, rb, tm):
    n, d = x2.shape
    row = lambda w: pl.BlockSpec((1, w), lambda i: (0, 0))
    return pl.pallas_call(
        _mix_route_kernel,
        out_shape=(jax.ShapeDtypeStruct((n, d), F32), jax.ShapeDtypeStruct((n, d), BF16),
                   jax.ShapeDtypeStruct((n, LANES), F32), jax.ShapeDtypeStruct((n // tm, 1, LANES), F32)),
        grid=(n // tm,),
        in_specs=[pl.BlockSpec((tm, D_DN), lambda i: (i, 0)), pl.BlockSpec((tm, D_MOBA), lambda i: (i, 0)),
                  pl.BlockSpec((tm, d), lambda i: (i, 0)), pl.BlockSpec((D_DN + D_MOBA, d), lambda i: (0, 0)),
                  row(d), row(d), pl.BlockSpec((d, LANES), lambda i: (0, 0)), row(LANES)],
        out_specs=(pl.BlockSpec((tm, d), lambda i: (i, 0)), pl.BlockSpec((tm, d), lambda i: (i, 0)),
                   pl.BlockSpec((tm, LANES), lambda i: (i, 0)), pl.BlockSpec((1, 1, LANES), lambda i: (i, 0, 0))),
        compiler_params=pltpu.CompilerParams(dimension_semantics=("parallel",), vmem_limit_bytes=VMEM_LIMIT),
        name="mix_route",
    )(y_dn, y_mb, x2, wo, g1, b1, rw, rb)


def _bucket_offsets_col(ohf):
    cnt = jnp.sum(ohf, axis=1, keepdims=True).astype(jnp.int32)
    pad = (((cnt + (GRAN - 1)) >> GRAN_SHIFT) << GRAN_SHIFT).astype(F32)
    r = lax.broadcasted_iota(jnp.int32, (LANES, LANES), 0)
    c = lax.broadcasted_iota(jnp.int32, (LANES, LANES), 1)
    before = jnp.where(c < r, 1.0, 0.0)
    return _dot(before, jnp.broadcast_to(pad, (LANES, LANES)))[:, 0:1]


def _moe_sort_kernel(gmap_ref, nvalid_ref, tail0_ref, taillen_ref, hb_ref, route_ref, lstrict_ref,
                     xg_ref, wsg_ref, xs_ref, ws_ref, zx_ref, zw_ref, sem):
    s = pl.program_id(0)
    nsteps = pl.num_programs(0)
    slot = s & 1
    ts = route_ref.shape[0]
    route = route_ref[...]
    rt = jnp.transpose(route)
    bucket_row = rt[ROUTE_BUCKET:ROUTE_BUCKET + 1, :].astype(jnp.int32)
    sub = lax.broadcasted_iota(jnp.int32, (LANES, ts), 0)
    ohf = jnp.where(sub == bucket_row, 1.0, 0.0)
    loff = _bucket_offsets_col(ohf)
    rank = lax.dot_general(ohf.astype(BF16), lstrict_ref[...], (((1,), (1,)), ((), ())),
                           preferred_element_type=F32)
    dest = jnp.sum(ohf * (loff + rank), axis=0, keepdims=True).astype(jnp.int32)
    rowi = lax.broadcasted_iota(jnp.int32, (LROWS, ts), 0)
    perm = jnp.where(rowi == dest, 1.0, 0.0).astype(BF16)
    xs_ref[slot] = jnp.dot(perm, hb_ref[...], preferred_element_type=F32).astype(BF16)
    rh, rl = _split2(route)
    wparts = jnp.dot(perm, jnp.concatenate([rh, rl], axis=1), preferred_element_type=F32)
    ws_ref[slot] = wparts[:, 0:LANES] + wparts[:, LANES:2 * LANES]

    def copies(step, g):
        sl = step & 1
        src = pl.ds(pl.multiple_of(g * GRAN, GRAN), GRAN)
        dst = pl.ds(pl.multiple_of(gmap_ref[step * LGRAN + g] * GRAN, GRAN), GRAN)
        return (pltpu.make_async_copy(xs_ref.at[sl, src, :], xg_ref.at[dst, :], sem.at[0, sl]),
                pltpu.make_async_copy(ws_ref.at[sl, src, :], wsg_ref.at[dst, :], sem.at[1, sl]))

    def fill_copies(b, i):
        dst = pl.ds(pl.multiple_of((tail0_ref[b] + i) * GRAN, GRAN), GRAN)
        return (pltpu.make_async_copy(zx_ref.at[0:GRAN, :], xg_ref.at[dst, :], sem.at[2, 0]),
                pltpu.make_async_copy(zw_ref.at[0:GRAN, :], wsg_ref.at[dst, :], sem.at[2, 1]))

    def unused_tile_copies(t):
        dst = pl.ds(pl.multiple_of(t * MOE_TM, MOE_TM), MOE_TM)
        return (pltpu.make_async_copy(zx_ref, xg_ref.at[dst, :], sem.at[2, 0]),
                pltpu.make_async_copy(zw_ref, wsg_ref.at[dst, :], sem.at[2, 1]))

    def run(step, fn):
        def body(g, carry):
            for cp in copies(step, g):
                fn(cp)
            return carry
        lax.fori_loop(0, nvalid_ref[step], body, 0)

    def run_fill(fn):
        for b in range(N_BUCKETS):
            def body(i, carry, b=b):
                for cp in fill_copies(b, i):
                    fn(cp)
                return carry
            lax.fori_loop(0, taillen_ref[b], body, 0)

        def tile_body(t, carry):
            for cp in unused_tile_copies(t):
                fn(cp)
            return carry
        lax.fori_loop(tail0_ref[N_BUCKETS], xg_ref.shape[0] // MOE_TM, tile_body, 0)

    @pl.when(s == 0)
    def _():
        zx_ref[...] = jnp.zeros_like(zx_ref)
        zw_ref[...] = jnp.zeros_like(zw_ref)
        run_fill(lambda cp: cp.start())

    run(s, lambda cp: cp.start())

    @pl.when(s > 0)
    def _():
        run(s - 1, lambda cp: cp.wait())

    @pl.when(s == nsteps - 1)
    def _():
        run(s, lambda cp: cp.wait())
        run_fill(lambda cp: cp.wait())


def _moe_sort(plan, hb, route, lstrict):
    n, d = hb.shape
    ts = MOE_TS
    rows = plan["n_tiles"] * MOE_TM
    return pl.pallas_call(
        _moe_sort_kernel,
        out_shape=(jax.ShapeDtypeStruct((rows, d), BF16), jax.ShapeDtypeStruct((rows, LANES), F32)),
        grid_spec=pltpu.PrefetchScalarGridSpec(
            num_scalar_prefetch=4,
            grid=(n // ts,),
            in_specs=[pl.BlockSpec((ts, d), lambda s, *_: (s, 0)), pl.BlockSpec((ts, LANES), lambda s, *_: (s, 0)),
                      pl.BlockSpec((ts, ts), lambda s, *_: (0, 0))],
            out_specs=(pl.BlockSpec(memory_space=pl.ANY), pl.BlockSpec(memory_space=pl.ANY)),
            scratch_shapes=[pltpu.VMEM((2, LROWS, d), BF16), pltpu.VMEM((2, LROWS, LANES), F32),
                            pltpu.VMEM((MOE_TM, d), BF16), pltpu.VMEM((MOE_TM, LANES), F32),
                            pltpu.SemaphoreType.DMA((3, 2))]),
        compiler_params=pltpu.CompilerParams(dimension_semantics=("arbitrary",), vmem_limit_bytes=VMEM_LIMIT),
        name="moe_sort",
    )(plan["gmap"], plan["nvalid"], plan["tail0"], plan["taillen"], hb, route, lstrict)


def _moe_expert_kernel(xt_ref, elo_ref, ehi_ref, valid_ref, x_ref, w_ref, wg0, wu0, wd0, wg1, wu1, wd1, o_ref):
    t = pl.program_id(0)

    @pl.when(valid_ref[t] > 0)
    def _():
        x = x_ref[...]
        w = w_ref[...]
        gates = [jnp.dot(x, wg[0], preferred_element_type=F32) for wg in (wg0, wg1)]
        ups = [jnp.dot(x, wu[0], preferred_element_type=F32) for wu in (wu0, wu1)]
        hes = [(_silu(gates[i]) * ups[i] * w[:, lane:lane + 1]).astype(BF16)
               for i, lane in enumerate((ROUTE_W_LO, ROUTE_W_HI))]
        o_ref[...] = (jnp.dot(hes[0], wd0[0], preferred_element_type=F32)
                      + jnp.dot(hes[1], wd1[0], preferred_element_type=F32)).astype(o_ref.dtype)

    @pl.when(valid_ref[t] == 0)
    def _():
        o_ref[...] = jnp.zeros_like(o_ref)


def _moe_experts(plan, xg, wsg, wg, wu, wd):
    rows, d = xg.shape
    tm = MOE_TM
    tok = lambda width: pl.BlockSpec((tm, width), lambda t, xt, elo, ehi, valid: (xt[t], 0))
    lo3 = lambda shape: pl.BlockSpec(shape, lambda t, xt, elo, ehi, valid: (elo[t], 0, 0))
    hi3 = lambda shape: pl.BlockSpec(shape, lambda t, xt, elo, ehi, valid: (ehi[t], 0, 0))
    return pl.pallas_call(
        _moe_expert_kernel,
        out_shape=jax.ShapeDtypeStruct((rows, d), BF16),
        grid_spec=pltpu.PrefetchScalarGridSpec(
            num_scalar_prefetch=4,
            grid=(rows // tm,),
            in_specs=[tok(d), tok(LANES),
                      lo3((1, d, D_EXPERT)), lo3((1, d, D_EXPERT)), lo3((1, D_EXPERT, d)),
                      hi3((1, d, D_EXPERT)), hi3((1, d, D_EXPERT)), hi3((1, D_EXPERT, d))],
            out_specs=pl.BlockSpec((tm, d), lambda t, *_: (t, 0))),
        compiler_params=pltpu.CompilerParams(dimension_semantics=("arbitrary",), vmem_limit_bytes=VMEM_LIMIT),
        name="moe_experts",
    )(plan["xtile"], plan["elo"], plan["ehi"], plan["valid"], xg, wsg, wg, wu, wd, wg, wu, wd)


def _moe_unsort_kernel(gmap_ref, og_ref, route_ref, h_ref, lstrict_ref, g_ref, b_ref, out_ref, ol_ref, sem):
    s = pl.program_id(0)
    nsteps = pl.num_programs(0)
    slot = s & 1
    ts = route_ref.shape[0]

    def gather(step, fn):
        sl = step & 1

        def body(g, carry):
            src = pl.ds(pl.multiple_of(gmap_ref[step * LGRAN + g] * GRAN, GRAN), GRAN)
            dst = pl.ds(pl.multiple_of(g * GRAN, GRAN), GRAN)
            fn(pltpu.make_async_copy(og_ref.at[src, :], ol_ref.at[sl, dst, :], sem.at[sl]))
            return carry
        lax.fori_loop(0, LGRAN, body, 0)

    @pl.when(s == 0)
    def _():
        gather(s, lambda cp: cp.start())

    @pl.when(s + 1 < nsteps)
    def _():
        gather(s + 1, lambda cp: cp.start())

    route = route_ref[...]
    bucket_col = route[:, ROUTE_BUCKET:ROUTE_BUCKET + 1].astype(jnp.int32)
    lane = lax.broadcasted_iota(jnp.int32, (ts, LANES), 1)
    ohf = jnp.where(lane == bucket_col, 1.0, 0.0)
    cnt = jnp.sum(ohf, axis=0, keepdims=True).astype(jnp.int32)
    pad = (((cnt + (GRAN - 1)) >> GRAN_SHIFT) << GRAN_SHIFT).astype(F32)
    r = lax.broadcasted_iota(jnp.int32, (LANES, LANES), 0)
    c = lax.broadcasted_iota(jnp.int32, (LANES, LANES), 1)
    loff = _dot(jnp.broadcast_to(pad, (8, LANES)), jnp.where(r < c, 1.0, 0.0))[0:1, :]
    rank = jnp.dot(lstrict_ref[...], ohf.astype(BF16), preferred_element_type=F32)
    dest = jnp.sum(ohf * (loff + rank), axis=1, keepdims=True).astype(jnp.int32)
    lrow = lax.broadcasted_iota(jnp.int32, (ts, LROWS), 1)
    perm_t = jnp.where(lrow == dest, 1.0, 0.0).astype(BF16)

    gather(s, lambda cp: cp.wait())
    ffn = jnp.dot(perm_t, ol_ref[slot], preferred_element_type=F32)
    out_ref[...] = _layer_norm(DEEPNORM_ALPHA * h_ref[...] + ffn, g_ref[...], b_ref[...])


def _moe_unsort(plan, og, route, hf, lstrict, g2, b2):
    n, d = hf.shape
    ts = MOE_TS
    row = pl.BlockSpec((1, d), lambda s, *_: (0, 0))
    return pl.pallas_call(
        _moe_unsort_kernel,
        out_shape=jax.ShapeDtypeStruct((n, d), F32),
        grid_spec=pltpu.PrefetchScalarGridSpec(
            num_scalar_prefetch=1,
            grid=(n // ts,),
            in_specs=[pl.BlockSpec(memory_space=pl.ANY), pl.BlockSpec((ts, LANES), lambda s, *_: (s, 0)),
                      pl.BlockSpec((ts, d), lambda s, *_: (s, 0)), pl.BlockSpec((ts, ts), lambda s, *_: (0, 0)),
                      row, row],
            out_specs=pl.BlockSpec((ts, d), lambda s, *_: (s, 0)),
            scratch_shapes=[pltpu.VMEM((2, LROWS, d), BF16), pltpu.SemaphoreType.DMA((2,))]),
        compiler_params=pltpu.CompilerParams(dimension_semantics=("arbitrary",), vmem_limit_bytes=VMEM_LIMIT),
        name="moe_unsort",
    )(plan["gmap_back"], og, route, hf, lstrict, g2, b2)


def _moe_plan(cnt_half, n):
    nsrc = n // MOE_TS
    i32 = jnp.int32
    cnt = cnt_half.reshape(nsrc, -1, LANES).sum(axis=1)[:, :N_BUCKETS].astype(i32)
    run_g = (cnt + GRAN - 1) // GRAN
    nvalid = run_g.sum(axis=1)
    loff_g = jnp.cumsum(run_g, axis=1) - run_g
    bucket_g = run_g.sum(axis=0)
    gpt = MOE_TM // GRAN
    btiles = (bucket_g + gpt - 1) // gpt
    tend = jnp.cumsum(btiles)
    tstart = tend - btiles
    gofs = tstart[None, :] * gpt + jnp.cumsum(run_g, axis=0) - run_g
    n_tiles = -(-(n + nsrc * N_BUCKETS * (GRAN - 1)) // MOE_TM) + N_BUCKETS + 1
    g = jnp.arange(LGRAN, dtype=i32)[None, :, None]
    in_run = (g >= loff_g[:, None, :]) & (g < (loff_g + run_g)[:, None, :])
    gmap = jnp.arange(LGRAN, dtype=i32)[None, :] + jnp.sum(jnp.where(in_run, (gofs - loff_g)[:, None, :], 0), axis=2)
    is_valid = jnp.arange(LGRAN, dtype=i32)[None, :] < nvalid[:, None]
    zero_gran = (n_tiles - 1) * gpt
    t = jnp.arange(n_tiles, dtype=i32)
    tb = jnp.minimum(jnp.sum(t[:, None] >= tend[None, :], axis=1), N_BUCKETS - 1)
    valid = (t < tend[-1]).astype(i32)
    pairs = [(a, b) for a in range(EXPERTS_PER_GROUP) for b in range(a + 1, EXPERTS_PER_GROUP)]
    pidx = tb % N_PAIRS
    pair_a = sum(jnp.where(pidx == i, a, 0) for i, (a, _) in enumerate(pairs))
    pair_b = sum(jnp.where(pidx == i, b, 0) for i, (_, b) in enumerate(pairs))
    grp = tb // N_PAIRS
    return {
        "n_tiles": n_tiles,
        "gmap": jnp.where(is_valid, gmap, 0).reshape(-1).astype(i32),
        "gmap_back": jnp.where(is_valid, gmap, zero_gran).reshape(-1).astype(i32),
        "nvalid": nvalid.astype(i32),
        "tail0": jnp.concatenate([tstart * gpt + bucket_g, tend[-1:]]).astype(i32),
        "taillen": (btiles * gpt - bucket_g).astype(i32),
        "xtile": jnp.where(valid > 0, t, 0).astype(i32),
        "elo": (grp * EXPERTS_PER_GROUP + pair_a).astype(i32),
        "ehi": (grp * EXPERTS_PER_GROUP + pair_b).astype(i32),
        "valid": valid,
    }


def _pad_lanes(a, lane0=0):
    return jnp.zeros((1, LANES), F32).at[0, lane0:lane0 + a.shape[0]].set(a.astype(F32))


def _rope_tables(seq):
    half = HEAD_DIM // 2
    inv_freq = ROPE_THETA ** (-jnp.arange(half, dtype=F32) / half)
    ang = jnp.arange(seq).astype(F32)[:, None] * inv_freq[None, :]
    cos, sin = jnp.cos(ang), jnp.sin(ang)
    return jnp.concatenate([cos, cos], axis=-1), jnp.concatenate([-sin, sin], axis=-1)


def _layer(x, w_in, conv_w, a_log, dt_bias, dn_norm_w, w_out, ln1_g, ln1_b, router_w1, router_b1,
           router_w2, router_b2, w_gate, w_up, w_down, ln2_g, ln2_b):
    bsz, seq, d = x.shape
    n = bsz * seq
    x2 = x.reshape(n, d)

    o_z, o_b, o_mb = 3 * D_DN, 4 * D_DN, 4 * D_DN + 2 * N_HEADS_DN
    w_ba = jnp.pad(w_in[:, o_b:o_mb], ((0, 0), (0, LANES - 2 * N_HEADS_DN)))
    w_all = jnp.concatenate([w_in[:, :o_z], w_in[:, o_z:o_b], w_ba, w_in[:, o_mb:]], axis=1).astype(BF16)

    tm = min(512, n)
    dn_qkv, z, ba, mb_qkv = _in_proj(x2, w_all, tm)

    y_dn = _deltanet(dn_qkv.reshape(bsz, seq, 3 * D_DN), z.reshape(bsz, seq, D_DN), ba.reshape(bsz, seq, LANES),
                     conv_w, _pad_lanes(a_log, N_HEADS_DN), _pad_lanes(dt_bias, N_HEADS_DN),
                     dn_norm_w.astype(F32).reshape(1, HEAD_DIM))

    cos_t, sin_t = _rope_tables(seq)
    q_r, k_r, vt, sel = _moba_prep(mb_qkv.reshape(bsz, seq, 3 * D_MOBA), cos_t, sin_t)
    y_mb = _moba_attn(q_r, k_r, vt, sel)

    rw = jnp.concatenate([router_w1, jnp.transpose(router_w2, (1, 0, 2)).reshape(d, N_EXPERTS)], axis=1)
    rw = jnp.pad(rw, ((0, 0), (0, LANES - rw.shape[1])))
    rb = _pad_lanes(jnp.concatenate([router_b1, router_b2.reshape(-1)]))
    hf, hb, route, cnt = _mix_route(y_dn.reshape(n, D_DN), y_mb.reshape(n, D_MOBA), x2, w_out.astype(BF16),
                                    ln1_g.reshape(1, d), ln1_b.reshape(1, d), rw, rb, tm)

    plan = _moe_plan(cnt, n)
    idx = jnp.arange(MOE_TS, dtype=jnp.int32)
    lstrict = (idx[None, :] < idx[:, None]).astype(BF16)
    xg, wsg = _moe_sort(plan, hb, route, lstrict)
    og = _moe_experts(plan, xg, wsg, w_gate.astype(BF16), w_up.astype(BF16), w_down.astype(BF16))
    out = _moe_unsort(plan, og, route, hf, lstrict, ln2_g.reshape(1, d), ln2_b.reshape(1, d))
    return out.reshape(bsz, seq, d)


def kernel(x, w_in, conv_w, a_log, dt_bias, dn_norm_w, w_out, ln1_g, ln1_b, router_w1, router_b1, router_w2, router_b2, expert_w_gate, expert_w_up, expert_w_down, ln2_g, ln2_b):
    for l in range(DEPTH):
        x = _layer(x, w_in[l], conv_w[l], a_log[l], dt_bias[l], dn_norm_w[l], w_out[l], ln1_g[l], ln1_b[l],
                   router_w1[l], router_b1[l], router_w2[l], router_b2[l], expert_w_gate[l], expert_w_up[l],
                   expert_w_down[l], ln2_g[l], ln2_b[l])
    return x
```

```python
import functools

import jax
import jax.numpy as jnp
from jax import lax
from jax.experimental import pallas as pl
from jax.experimental.pallas import tpu as pltpu

F32 = jnp.float32
BF16 = jnp.bfloat16

HEAD_DIM = 128
N_HEADS_DN = 4
N_HEADS_MOBA = 4
D_DN = N_HEADS_DN * HEAD_DIM
D_MOBA = N_HEADS_MOBA * HEAD_DIM
CONV_K = 4
DN_CHUNK = 64
MOBA_BLOCK = 256
MOBA_TOPK = 3
ROPE_THETA = 10000.0
N_GROUPS = 4
EXPERTS_PER_GROUP = 4
N_EXPERTS = N_GROUPS * EXPERTS_PER_GROUP
D_EXPERT = 256
LN_EPS = 1e-5
RMS_EPS = 1e-6
L2_EPS = 1e-6
NEG_INF = -1e30
DEPTH = 1
DEEPNORM_ALPHA = (2 * DEPTH) ** 0.25

LANES = 128
DN_TILE = 256
DN_HEADS_PER_STEP = 4
GATE_LANE0 = N_GROUPS
N_PAIRS = EXPERTS_PER_GROUP * (EXPERTS_PER_GROUP - 1) // 2
N_BUCKETS = N_GROUPS * N_PAIRS
ROUTE_BUCKET, ROUTE_W_LO, ROUTE_W_HI = 0, 1, 2
MOE_TS = 512
MOE_TM = 256
GRAN = 16
GRAN_SHIFT = 4
LROWS = -(-(MOE_TS + N_BUCKETS * (GRAN - 1)) // LANES) * LANES
LGRAN = LROWS // GRAN
VMEM_LIMIT = 48 * 1024 * 1024


def _dot(a, b):
    return jnp.dot(a.astype(BF16), b.astype(BF16), preferred_element_type=F32)


def _dot_nt(a, b):
    return lax.dot_general(a.astype(BF16), b.astype(BF16), (((1,), (1,)), ((), ())),
                           preferred_element_type=F32)


def _split2(a):
    hi = a.astype(BF16)
    lo = (a - hi.astype(F32)).astype(BF16)
    return hi, lo


def _split3(a):
    hi = a.astype(BF16)
    r = a - hi.astype(F32)
    mid = r.astype(BF16)
    lo = (r - mid.astype(F32)).astype(BF16)
    return hi, mid, lo


def _dot3(a, b):
    ah, al = _split2(a)
    bh, bl = _split2(b)
    return (jnp.dot(ah, bh, preferred_element_type=F32) + jnp.dot(ah, bl, preferred_element_type=F32)
            + jnp.dot(al, bh, preferred_element_type=F32))


def _dot3_nt(a, b):
    ah, al = _split2(a)
    bh, bl = _split2(b)
    dn = (((1,), (1,)), ((), ()))
    return (lax.dot_general(ah, bh, dn, preferred_element_type=F32)
            + lax.dot_general(ah, bl, dn, preferred_element_type=F32)
            + lax.dot_general(al, bh, dn, preferred_element_type=F32))


def _dot_exact_lhs(a_bf16, b):
    bh, bm, bl = _split3(b)
    return (jnp.dot(a_bf16, bh, preferred_element_type=F32) + jnp.dot(a_bf16, bm, preferred_element_type=F32)
            + jnp.dot(a_bf16, bl, preferred_element_type=F32))


def _silu(x):
    return x * jax.nn.sigmoid(x)


def _softplus(x):
    return jnp.maximum(x, 0.0) + jnp.log1p(jnp.exp(-jnp.abs(x)))


def _layer_norm(t, g, b):
    mu = jnp.mean(t, axis=-1, keepdims=True)
    d = t - mu
    var = jnp.mean(d * d, axis=-1, keepdims=True)
    return d * lax.rsqrt(var + LN_EPS) * g + b


def _lane_pick(x, lane):
    ids = lax.broadcasted_iota(jnp.int32, x.shape, 1)
    return jnp.sum(jnp.where(ids == lane, x, 0.0), axis=1, keepdims=True)


def _in_proj_kernel(x_ref, w_ref, cw_ref, dn_ref, z_ref, ba_ref, mb_ref, cb_ref, *, tiles_per_seq):
    i = pl.program_id(0)
    tm = x_ref.shape[0]
    o0 = 3 * D_DN
    o1 = o0 + D_DN
    o2 = o1 + LANES

    @pl.when(i % tiles_per_seq == 0)
    def _():
        cb_ref[0:8, :] = jnp.zeros((8, o0), F32)

    xb = x_ref[...].astype(BF16)
    u = jnp.dot(xb, w_ref[:, 0:o0], preferred_element_type=F32)
    z_ref[...] = _silu(jnp.dot(xb, w_ref[:, o0:o1], preferred_element_type=F32))
    ba_ref[...] = jnp.dot(xb, w_ref[:, o1:o2], preferred_element_type=F32)
    mb_ref[...] = jnp.dot(xb, w_ref[:, o2:o2 + 3 * D_MOBA], preferred_element_type=F32)

    cb_ref[8:8 + tm, :] = u
    acc = cw_ref[CONV_K - 1:CONV_K, :] * u
    for s in range(1, CONV_K):
        acc = acc + cw_ref[CONV_K - 1 - s:CONV_K - s, :] * cb_ref[8 - s:8 - s + tm, :]
    cb_ref[0:8, :] = u[tm - 8:tm, :]
    qkv = _silu(acc)

    outs = []
    for h in range(2 * N_HEADS_DN):
        t = qkv[:, h * HEAD_DIM:(h + 1) * HEAD_DIM]
        t = t * lax.rsqrt(jnp.sum(t * t, axis=-1, keepdims=True) + L2_EPS)
        outs.append(t * (HEAD_DIM ** -0.5) if h < N_HEADS_DN else t)
    outs.append(qkv[:, 2 * D_DN:3 * D_DN])
    dn_ref[...] = jnp.concatenate(outs, axis=1)


def _in_proj(x2, w_all, conv_w, tm, seq):
    n, d = x2.shape
    wc = w_all.shape[1]
    return pl.pallas_call(
        functools.partial(_in_proj_kernel, tiles_per_seq=seq // tm),
        out_shape=(jax.ShapeDtypeStruct((n, 3 * D_DN), F32), jax.ShapeDtypeStruct((n, D_DN), F32),
                   jax.ShapeDtypeStruct((n, LANES), F32), jax.ShapeDtypeStruct((n, 3 * D_MOBA), F32)),
        grid=(n // tm,),
        in_specs=[pl.BlockSpec((tm, d), lambda i: (i, 0)), pl.BlockSpec((d, wc), lambda i: (0, 0)),
                  pl.BlockSpec((CONV_K, 3 * D_DN), lambda i: (0, 0))],
        out_specs=(pl.BlockSpec((tm, 3 * D_DN), lambda i: (i, 0)), pl.BlockSpec((tm, D_DN), lambda i: (i, 0)),
                   pl.BlockSpec((tm, LANES), lambda i: (i, 0)), pl.BlockSpec((tm, 3 * D_MOBA), lambda i: (i, 0))),
        scratch_shapes=[pltpu.VMEM((8 + tm, 3 * D_DN), F32)],
        compiler_params=pltpu.CompilerParams(dimension_semantics=("arbitrary",), vmem_limit_bytes=VMEM_LIMIT),
        name="in_proj",
    )(x2, w_all, conv_w)


def _inv_unit_lower(a_list, row, col):
    eye = (row == col).astype(F32)
    d8 = (row >> 3) == (col >> 3)
    a8 = [jnp.where(d8, a, 0.0) for a in a_list]
    a8_2 = [_dot(a, a) for a in a8]
    a8_4 = [_dot(a, a) for a in a8_2]
    x = [_dot(eye - a, eye + a2) for a, a2 in zip(a8, a8_2)]
    x = [_dot(xi, eye + a4) for xi, a4 in zip(x, a8_4)]
    s = 8
    while s < DN_CHUNK:
        sh = s.bit_length() - 1
        off = ((row >> (sh + 1)) == (col >> (sh + 1))) & ((row >> sh) != (col >> sh))
        y = [_dot(jnp.where(off, a, 0.0), xi) for a, xi in zip(a_list, x)]
        x = [xi - _dot(xi, yi) for xi, yi in zip(x, y)]
        s *= 2
    return x


def _deltanet_heads(q, k, v, z, beta, gcc, gcr, state, normw, masks):
    row, col, incl, strict = masks
    nh = len(q)
    hs = range(nh)
    tt = q[0].shape[0]
    nchunk = tt // DN_CHUNK
    decay = [jnp.where(incl, jnp.exp(jnp.where(incl, gcc[h] - gcr[h], 0.0)), 0.0) for h in hs]
    kb = [k[h] * beta[h] for h in hs]
    vb = [v[h] * beta[h] for h in hs]
    a_mat = [jnp.where(strict, _dot_nt(kb[h], k[h]) * decay[h], 0.0) for h in hs]
    qk = [_dot_nt(q[h], k[h]) * decay[h] for h in hs]
    tinv = _inv_unit_lower(a_mat, row, col)

    eg = [jnp.exp(g) for g in gcc]
    wu = [_dot(tinv[h], jnp.concatenate([kb[h] * eg[h], vb[h]], axis=1)) for h in hs]
    qd = [q[h] * eg[h] for h in hs]

    gl_rows = [[g[(c + 1) * DN_CHUNK - 1:(c + 1) * DN_CHUNK, :] for c in range(nchunk)] for g in gcc]
    gl_col = [jnp.concatenate([jnp.broadcast_to(g, (DN_CHUNK, 1)) for g in rows], axis=0) for rows in gl_rows]
    kdt = [jnp.transpose(k[h] * jnp.exp(gl_col[h] - gcc[h])) for h in hs]

    outs = [[] for _ in hs]
    for c in range(nchunk):
        lo, hi = c * DN_CHUNK, (c + 1) * DN_CHUNK
        r = [_dot(jnp.concatenate([wu[h][lo:hi, 0:HEAD_DIM], qd[h][lo:hi, :]], axis=0), state[h]) for h in hs]
        vz = []
        for h in hs:
            parts = []
            if lo > 0:
                parts.append(jnp.zeros((lo, HEAD_DIM), F32))
            parts.append(wu[h][lo:hi, HEAD_DIM:2 * HEAD_DIM] - r[h][0:DN_CHUNK, :])
            if hi < tt:
                parts.append(jnp.zeros((tt - hi, HEAD_DIM), F32))
            vz.append(jnp.concatenate(parts, axis=0).astype(BF16))
        for h in hs:
            outs[h].append(r[h][DN_CHUNK:2 * DN_CHUNK, :] + _dot(qk[h][lo:hi, :], vz[h]))
        state = [state[h] * jnp.exp(gl_rows[h][c]) + _dot(kdt[h], vz[h]) for h in hs]

    ys = []
    for h in hs:
        o = jnp.concatenate(outs[h], axis=0)
        o = o * lax.rsqrt(jnp.mean(o * o, axis=-1, keepdims=True) + RMS_EPS) * normw
        ys.append(o * z[h])
    return ys, state


def _deltanet_kernel(q_ref, k_ref, v_ref, z_ref, ba_ref, alog_ref, dtb_ref, normw_ref, y_ref, s_ref, *, hb):
    hg = pl.program_id(1)
    t = pl.program_id(2)
    tt = DN_TILE

    @pl.when(t == 0)
    def _():
        s_ref[...] = jnp.zeros_like(s_ref)

    q_all = q_ref[0]
    k_all = k_ref[0]
    v_all = v_ref[0]

    ba = ba_ref[0]
    beta_all = jax.nn.sigmoid(ba)
    g_all = -jnp.exp(alog_ref[...]) * _softplus(ba + dtb_ref[...])

    row = lax.broadcasted_iota(jnp.int32, (tt, tt), 0)
    col = lax.broadcasted_iota(jnp.int32, (tt, tt), 1)
    same = (row >> 6) == (col >> 6)
    incl = same & (row >= col)
    strict = same & (row > col)
    masks = (row, col, incl, strict)

    gc_all = _dot_exact_lhs(incl.astype(BF16), g_all)
    gct = jnp.transpose(gc_all)
    sub = lax.broadcasted_iota(jnp.int32, gct.shape, 0)

    sls = [slice(hh * HEAD_DIM, (hh + 1) * HEAD_DIM) for hh in range(hb)]
    heads = [hg * hb + hh for hh in range(hb)]
    beta = [_lane_pick(beta_all, h) for h in heads]
    gcc = [_lane_pick(gc_all, h + N_HEADS_DN) for h in heads]
    gcr = [jnp.sum(jnp.where(sub == h + N_HEADS_DN, gct, 0.0), axis=0, keepdims=True) for h in heads]
    ys, states = _deltanet_heads([q_all[:, sl] for sl in sls], [k_all[:, sl] for sl in sls],
                                 [v_all[:, sl] for sl in sls], [z_ref[0, :, sl] for sl in sls], beta, gcc, gcr,
                                 [s_ref[hh] for hh in range(hb)], normw_ref[...], masks)
    s_ref[...] = jnp.stack(states, axis=0)
    y_ref[0] = jnp.concatenate(ys, axis=1).astype(y_ref.dtype)


def _deltanet(dn_qkv, z, ba, alog_row, dtb_row, normw_row):
    bsz, seq, _ = dn_qkv.shape
    tt = DN_TILE
    hb = DN_HEADS_PER_STEP
    ng = N_HEADS_DN // hb
    w = hb * HEAD_DIM

    def col_spec(off):
        return pl.BlockSpec((1, tt, w), lambda b, g, t: (b, t, g + off))

    row_spec = pl.BlockSpec((1, LANES), lambda b, g, t: (0, 0))
    return pl.pallas_call(
        functools.partial(_deltanet_kernel, hb=hb),
        out_shape=jax.ShapeDtypeStruct((bsz, seq, D_DN), BF16),
        grid=(bsz, ng, seq // tt),
        in_specs=[col_spec(0), col_spec(ng), col_spec(2 * ng), col_spec(0),
                  pl.BlockSpec((1, tt, LANES), lambda b, g, t: (b, t, 0)), row_spec, row_spec, row_spec],
        out_specs=col_spec(0),
        scratch_shapes=[pltpu.VMEM((hb, HEAD_DIM, HEAD_DIM), F32)],
        compiler_params=pltpu.CompilerParams(dimension_semantics=("parallel", "parallel", "arbitrary"),
                                             vmem_limit_bytes=VMEM_LIMIT),
        name="deltanet",
    )(dn_qkv, dn_qkv, dn_qkv, z, ba, alog_row, dtb_row, normw_row)


def _moba_prep_kernel(x_ref, cos_ref, sin_ref, q_ref, k_ref, vt_ref, sel_ref, km_ref, *, nb, topk):
    j = pl.program_id(1)

    @pl.when(j == 0)
    def _():
        km_ref[...] = jnp.zeros_like(km_ref)

    cos = cos_ref[...]
    sin = sin_ref[...]
    half = HEAD_DIM // 2
    blk = lax.broadcasted_iota(jnp.int32, (nb, MOBA_BLOCK), 0)
    kmeans = []
    for h in range(N_HEADS_MOBA):
        qh = x_ref[0, :, h * HEAD_DIM:(h + 1) * HEAD_DIM]
        kh = x_ref[0, :, D_MOBA + h * HEAD_DIM:D_MOBA + (h + 1) * HEAD_DIM]
        qr = (qh * cos + pltpu.roll(qh, half, 1) * sin) * (HEAD_DIM ** -0.5)
        kr = kh * cos + pltpu.roll(kh, half, 1) * sin
        q_ref[0, :, h * HEAD_DIM:(h + 1) * HEAD_DIM] = qr.astype(q_ref.dtype)
        k_ref[0, :, h * HEAD_DIM:(h + 1) * HEAD_DIM] = kr.astype(k_ref.dtype)
        kmeans.append(jnp.mean(kr, axis=0, keepdims=True))

        gate = _dot3_nt(km_ref[:, h * HEAD_DIM:(h + 1) * HEAD_DIM], qr)
        gate = jnp.where(blk < j, gate, NEG_INF)
        rank = jnp.zeros(gate.shape, F32)
        for m in range(nb):
            gm = gate[m:m + 1, :]
            ahead = (gm > gate) | ((gm == gate) & (blk > m))
            rank = rank + jnp.where(ahead, 1.0, 0.0)
        sel = (blk < j) & (rank < topk)
        sel_ref[0, 0, h * nb:(h + 1) * nb, :] = jnp.where(sel, 1.0, 0.0)

    km_ref[pl.ds(j, 1), :] = jnp.concatenate(kmeans, axis=1)
    vt_ref[0, 0] = jnp.transpose(x_ref[0, :, 2 * D_MOBA:3 * D_MOBA]).astype(vt_ref.dtype)


def _moba_prep(mb_qkv, cos_t, sin_t):
    bsz, seq, _ = mb_qkv.shape
    nb = seq // MOBA_BLOCK
    topk = min(MOBA_TOPK, nb)
    kern = functools.partial(_moba_prep_kernel, nb=nb, topk=topk)
    tok_spec = pl.BlockSpec((1, MOBA_BLOCK, D_MOBA), lambda b, j: (b, j, 0))
    tab_spec = pl.BlockSpec((MOBA_BLOCK, HEAD_DIM), lambda b, j: (j, 0))
    return pl.pallas_call(
        kern,
        out_shape=(jax.ShapeDtypeStruct((bsz, seq, D_MOBA), BF16), jax.ShapeDtypeStruct((bsz, seq, D_MOBA), BF16),
                   jax.ShapeDtypeStruct((bsz, nb, D_MOBA, MOBA_BLOCK), BF16),
                   jax.ShapeDtypeStruct((bsz, nb, N_HEADS_MOBA * nb, MOBA_BLOCK), F32)),
        grid=(bsz, nb),
        in_specs=[pl.BlockSpec((1, MOBA_BLOCK, 3 * D_MOBA), lambda b, j: (b, j, 0)), tab_spec, tab_spec],
        out_specs=(tok_spec, tok_spec,
                   pl.BlockSpec((1, 1, D_MOBA, MOBA_BLOCK), lambda b, j: (b, j, 0, 0)),
                   pl.BlockSpec((1, 1, N_HEADS_MOBA * nb, MOBA_BLOCK), lambda b, j: (b, j, 0, 0))),
        scratch_shapes=[pltpu.VMEM((nb, D_MOBA), F32)],
        compiler_params=pltpu.CompilerParams(dimension_semantics=("parallel", "arbitrary"),
                                             vmem_limit_bytes=VMEM_LIMIT),
        name="moba_prep",
    )(mb_qkv, cos_t, sin_t)


def _moba_attn_kernel(q_ref, k_ref, vt_ref, sel_ref, o_ref, acc_ref, *, nb):
    j = pl.program_id(1)
    blk = MOBA_BLOCK
    cw = 2 * blk
    nh = N_HEADS_MOBA
    nc = (j + 1) // 2
    dn = (((1,), (1,)), ((), ()))
    hsl = [slice(h * HEAD_DIM, (h + 1) * HEAD_DIM) for h in range(nh)]
    qs = [q_ref[0, :, hsl[h]] for h in range(nh)]

    ki = lax.broadcasted_iota(jnp.int32, (blk, blk), 0)
    qi = lax.broadcasted_iota(jnp.int32, (blk, blk), 1)
    own = pl.ds(pl.multiple_of(j * blk, blk), blk)
    s_own = [jnp.where(ki <= qi, lax.dot_general(k_ref[0, own, hsl[h]], qs[h], dn, preferred_element_type=F32),
                       NEG_INF) for h in range(nh)]
    ms, ls = [], []
    for h in range(nh):
        m = jnp.max(s_own[h], axis=0, keepdims=True)
        p = jnp.exp(s_own[h] - m)
        ms.append(m)
        ls.append(jnp.sum(p, axis=0, keepdims=True))
        acc_ref[h] = jnp.dot(vt_ref[0, j, hsl[h], :], p.astype(BF16), preferred_element_type=F32)

    def body(c, carry):
        ms, ls = carry
        ms, ls = list(ms), list(ls)
        ss = []
        for h in range(nh):
            kc = k_ref[0, pl.ds(pl.multiple_of(c * cw, cw), cw), hsl[h]]
            s = lax.dot_general(kc, qs[h], dn, preferred_element_type=F32)
            parts = []
            for i in range(2):
                selrow = sel_ref[0, 0, pl.ds(h * nb + 2 * c + i, 1), :]
                parts.append(jnp.where(selrow > 0.5, s[i * blk:(i + 1) * blk, :], NEG_INF))
            ss.append(jnp.concatenate(parts, axis=0))
        for h in range(nh):
            m_new = jnp.maximum(ms[h], jnp.max(ss[h], axis=0, keepdims=True))
            alpha = jnp.exp(ms[h] - m_new)
            p = jnp.exp(ss[h] - m_new)
            ls[h] = alpha * ls[h] + jnp.sum(p, axis=0, keepdims=True)
            ms[h] = m_new
            pb = p.astype(BF16)
            pv = (jnp.dot(vt_ref[0, 2 * c, hsl[h], :], pb[0:blk, :], preferred_element_type=F32)
                  + jnp.dot(vt_ref[0, 2 * c + 1, hsl[h], :], pb[blk:cw, :], preferred_element_type=F32))
            acc_ref[h] = acc_ref[h] * alpha + pv
        return tuple(ms), tuple(ls)

    ms, ls = lax.fori_loop(0, nc, body, (tuple(ms), tuple(ls)))
    o_ref[0] = jnp.concatenate([jnp.transpose(acc_ref[h] / ls[h]) for h in range(nh)],
                               axis=1).astype(o_ref.dtype)


def _moba_attn(q_r, k_r, vt, sel):
    bsz, seq, _ = q_r.shape
    nb = seq // MOBA_BLOCK
    tok_spec = pl.BlockSpec((1, MOBA_BLOCK, D_MOBA), lambda b, j: (b, j, 0))
    return pl.pallas_call(
        functools.partial(_moba_attn_kernel, nb=nb),
        out_shape=jax.ShapeDtypeStruct((bsz, seq, D_MOBA), BF16),
        grid=(bsz, nb),
        in_specs=[tok_spec,
                  pl.BlockSpec((1, seq, D_MOBA), lambda b, j: (b, 0, 0)),
                  pl.BlockSpec((1, nb, D_MOBA, MOBA_BLOCK), lambda b, j: (b, 0, 0, 0)),
                  pl.BlockSpec((1, 1, N_HEADS_MOBA * nb, MOBA_BLOCK), lambda b, j: (b, j, 0, 0))],
        out_specs=tok_spec,
        scratch_shapes=[pltpu.VMEM((N_HEADS_MOBA, HEAD_DIM, MOBA_BLOCK), F32)],
        compiler_params=pltpu.CompilerParams(dimension_semantics=("parallel", "arbitrary"),
                                             vmem_limit_bytes=VMEM_LIMIT),
        name="moba_attn",
    )(q_r, k_r, vt, sel)


def _mix_route_kernel(ydn_ref, ymb_ref, x_ref, wo_ref, g_ref, b_ref, rw_ref, rb_ref, h_ref, hb_ref, route_ref,
                      cnt_ref):
    mix = (jnp.dot(ydn_ref[...], wo_ref[0:D_DN, :], preferred_element_type=F32)
           + jnp.dot(ymb_ref[...], wo_ref[D_DN:D_DN + D_MOBA, :], preferred_element_type=F32))
    hval = _layer_norm(DEEPNORM_ALPHA * x_ref[...] + mix, g_ref[...], b_ref[...])
    h_ref[...] = hval
    hb_ref[...] = hval.astype(BF16)

    hh, hl = _split2(hval)
    wh, wl = _split2(rw_ref[...])
    both = jnp.dot(hh, jnp.concatenate([wh, wl], axis=1), preferred_element_type=F32)
    logits = (both[:, 0:LANES] + both[:, LANES:2 * LANES] + jnp.dot(hl, wh, preferred_element_type=F32)
              + rb_ref[...])
    lane = lax.broadcasted_iota(jnp.int32, logits.shape, 1)
    big = jnp.int32(LANES)

    def first_lane(mask):
        return jnp.min(jnp.where(mask, lane, big), axis=1, keepdims=True)

    is_g = lane < N_GROUPS
    m1 = jnp.max(jnp.where(is_g, logits, NEG_INF), axis=1, keepdims=True)
    s1 = jnp.sum(jnp.where(is_g, jnp.exp(logits - m1), 0.0), axis=1, keepdims=True)
    pg = 1.0 / s1
    gsel = first_lane(is_g & (logits == m1))

    in_grp = (lane >= GATE_LANE0) & (((lane - GATE_LANE0) >> 2) == gsel) & (lane < GATE_LANE0 + N_EXPERTS)
    m2 = jnp.max(jnp.where(in_grp, logits, NEG_INF), axis=1, keepdims=True)
    s2 = jnp.sum(jnp.where(in_grp, jnp.exp(logits - m2), 0.0), axis=1, keepdims=True)
    e1 = first_lane(in_grp & (logits == m2))
    rest = in_grp & (lane != e1)
    m2b = jnp.max(jnp.where(rest, logits, NEG_INF), axis=1, keepdims=True)
    e2 = first_lane(rest & (logits == m2b))
    pe1 = 1.0 / s2
    pe2 = jnp.exp(m2b - m2) / s2
    tot = pe1 + pe2
    w1 = pg * (pe1 / tot)
    w2 = pg * (pe2 / tot)
    first_lo = e1 < e2
    lo = jnp.minimum(e1, e2)
    hi = jnp.maximum(e1, e2)
    a = (lo - GATE_LANE0) & (EXPERTS_PER_GROUP - 1)
    b = (hi - GATE_LANE0) & (EXPERTS_PER_GROUP - 1)
    bucket = gsel * N_PAIRS + ((a * (2 * EXPERTS_PER_GROUP - 1 - a)) >> 1) + (b - a - 1)
    route_ref[...] = jnp.where(lane == ROUTE_BUCKET, bucket.astype(F32),
                               jnp.where(lane == ROUTE_W_LO, jnp.where(first_lo, w1, w2),
                                         jnp.where(lane == ROUTE_W_HI, jnp.where(first_lo, w2, w1), 0.0)))
    cnt_ref[0] = jnp.sum(jnp.where(lane == bucket, 1.0, 0.0), axis=0, keepdims=True)


def _mix_route(y_dn, y_mb, x2, wo, g1, b1, rw, rb, tm):
    n, d = x2.shape
    row = lambda w: pl.BlockSpec((1, w), lambda i: (0, 0))
    return pl.pallas_call(
        _mix_route_kernel,
        out_shape=(jax.ShapeDtypeStruct((n, d), F32), jax.ShapeDtypeStruct((n, d), BF16),
                   jax.ShapeDtypeStruct((n, LANES), F32), jax.ShapeDtypeStruct((n // tm, 1, LANES), F32)),
        grid=(n // tm,),
        in_specs=[pl.BlockSpec((tm, D_DN), lambda i: (i, 0)), pl.BlockSpec((tm, D_MOBA), lambda i: (i, 0)),
                  pl.BlockSpec((tm, d), lambda i: (i, 0)), pl.BlockSpec((D_DN + D_MOBA, d), lambda i: (0, 0)),
                  row(d), row(d), pl.BlockSpec((d, LANES), lambda i: (0, 0)), row(LANES)],
        out_specs=(pl.BlockSpec((tm, d), lambda i: (i, 0)), pl.BlockSpec((tm, d), lambda i: (i, 0)),
                   pl.BlockSpec((tm, LANES), lambda i: (i, 0)), pl.BlockSpec((1, 1, LANES), lambda i: (i, 0, 0))),
        compiler_params=pltpu.CompilerParams(dimension_semantics=("parallel",), vmem_limit_bytes=VMEM_LIMIT),
        name="mix_route",
    )(y_dn, y_mb, x2, wo, g1, b1, rw, rb)


def _bucket_offsets_col(ohf):
    cnt = jnp.sum(ohf, axis=1, keepdims=True).astype(jnp.int32)
    pad = (((cnt + (GRAN - 1)) >> GRAN_SHIFT) << GRAN_SHIFT).astype(F32)
    r = lax.broadcasted_iota(jnp.int32, (LANES, LANES), 0)
    c = lax.broadcasted_iota(jnp.int32, (LANES, LANES), 1)
    before = jnp.where(c < r, 1.0, 0.0)
    return _dot(before, jnp.broadcast_to(pad, (LANES, LANES)))[:, 0:1]


def _moe_sort_kernel(gmap_ref, nvalid_ref, tail0_ref, taillen_ref, hb_ref, route_ref, lstrict_ref,
                     xg_ref, wsg_ref, xs_ref, ws_ref, zx_ref, zw_ref, sem):
    s = pl.program_id(0)
    nsteps = pl.num_programs(0)
    slot = s & 1
    ts = route_ref.shape[0]
    route = route_ref[...]
    rt = jnp.transpose(route)
    bucket_row = rt[ROUTE_BUCKET:ROUTE_BUCKET + 1, :].astype(jnp.int32)
    sub = lax.broadcasted_iota(jnp.int32, (LANES, ts), 0)
    ohf = jnp.where(sub == bucket_row, 1.0, 0.0)
    loff = _bucket_offsets_col(ohf)
    rank = lax.dot_general(ohf.astype(BF16), lstrict_ref[...], (((1,), (1,)), ((), ())),
                           preferred_element_type=F32)
    dest = jnp.sum(ohf * (loff + rank), axis=0, keepdims=True).astype(jnp.int32)
    rowi = lax.broadcasted_iota(jnp.int32, (LROWS, ts), 0)
    perm = jnp.where(rowi == dest, 1.0, 0.0).astype(BF16)
    xs_ref[slot] = jnp.dot(perm, hb_ref[...], preferred_element_type=F32).astype(BF16)
    rh, rl = _split2(route)
    wparts = jnp.dot(perm, jnp.concatenate([rh, rl], axis=1), preferred_element_type=F32)
    ws_ref[slot] = wparts[:, 0:LANES] + wparts[:, LANES:2 * LANES]

    def copies(step, g):
        sl = step & 1
        src = pl.ds(pl.multiple_of(g * GRAN, GRAN), GRAN)
        dst = pl.ds(pl.multiple_of(gmap_ref[step * LGRAN + g] * GRAN, GRAN), GRAN)
        return (pltpu.make_async_copy(xs_ref.at[sl, src, :], xg_ref.at[dst, :], sem.at[0, sl]),
                pltpu.make_async_copy(ws_ref.at[sl, src, :], wsg_ref.at[dst, :], sem.at[1, sl]))

    def fill_copies(b, i):
        dst = pl.ds(pl.multiple_of((tail0_ref[b] + i) * GRAN, GRAN), GRAN)
        return (pltpu.make_async_copy(zx_ref.at[0:GRAN, :], xg_ref.at[dst, :], sem.at[2, 0]),
                pltpu.make_async_copy(zw_ref.at[0:GRAN, :], wsg_ref.at[dst, :], sem.at[2, 1]))

    def unused_tile_copies(t):
        dst = pl.ds(pl.multiple_of(t * MOE_TM, MOE_TM), MOE_TM)
        return (pltpu.make_async_copy(zx_ref, xg_ref.at[dst, :], sem.at[2, 0]),
                pltpu.make_async_copy(zw_ref, wsg_ref.at[dst, :], sem.at[2, 1]))

    def run(step, fn):
        def body(g, carry):
            for cp in copies(step, g):
                fn(cp)
            return carry
        lax.fori_loop(0, nvalid_ref[step], body, 0)

    def run_fill(fn):
        for b in range(N_BUCKETS):
            def body(i, carry, b=b):
                for cp in fill_copies(b, i):
                    fn(cp)
                return carry
            lax.fori_loop(0, taillen_ref[b], body, 0)

        def tile_body(t, carry):
            for cp in unused_tile_copies(t):
                fn(cp)
            return carry
        lax.fori_loop(tail0_ref[N_BUCKETS], xg_ref.shape[0] // MOE_TM, tile_body, 0)

    @pl.when(s == 0)
    def _():
        zx_ref[...] = jnp.zeros_like(zx_ref)
        zw_ref[...] = jnp.zeros_like(zw_ref)
        run_fill(lambda cp: cp.start())

    run(s, lambda cp: cp.start())

    @pl.when(s > 0)
    def _():
        run(s - 1, lambda cp: cp.wait())

    @pl.when(s == nsteps - 1)
    def _():
        run(s, lambda cp: cp.wait())
        run_fill(lambda cp: cp.wait())


def _moe_sort(plan, hb, route, lstrict):
    n, d = hb.shape
    ts = MOE_TS
    rows = plan["n_tiles"] * MOE_TM
    return pl.pallas_call(
        _moe_sort_kernel,
        out_shape=(jax.ShapeDtypeStruct((rows, d), BF16), jax.ShapeDtypeStruct((rows, LANES), F32)),
        grid_spec=pltpu.PrefetchScalarGridSpec(
            num_scalar_prefetch=4,
            grid=(n // ts,),
            in_specs=[pl.BlockSpec((ts, d), lambda s, *_: (s, 0)), pl.BlockSpec((ts, LANES), lambda s, *_: (s, 0)),
                      pl.BlockSpec((ts, ts), lambda s, *_: (0, 0))],
            out_specs=(pl.BlockSpec(memory_space=pl.ANY), pl.BlockSpec(memory_space=pl.ANY)),
            scratch_shapes=[pltpu.VMEM((2, LROWS, d), BF16), pltpu.VMEM((2, LROWS, LANES), F32),
                            pltpu.VMEM((MOE_TM, d), BF16), pltpu.VMEM((MOE_TM, LANES), F32),
                            pltpu.SemaphoreType.DMA((3, 2))]),
        compiler_params=pltpu.CompilerParams(dimension_semantics=("arbitrary",), vmem_limit_bytes=VMEM_LIMIT),
        name="moe_sort",
    )(plan["gmap"], plan["nvalid"], plan["tail0"], plan["taillen"], hb, route, lstrict)


def _moe_expert_kernel(xt_ref, elo_ref, ehi_ref, valid_ref, x_ref, w_ref, wg0, wu0, wd0, wg1, wu1, wd1, o_ref):
    t = pl.program_id(0)

    @pl.when(valid_ref[t] > 0)
    def _():
        x = x_ref[...]
        w = w_ref[...]
        gates = [jnp.dot(x, wg[0], preferred_element_type=F32) for wg in (wg0, wg1)]
        ups = [jnp.dot(x, wu[0], preferred_element_type=F32) for wu in (wu0, wu1)]
        hes = [(_silu(gates[i]) * ups[i] * w[:, lane:lane + 1]).astype(BF16)
               for i, lane in enumerate((ROUTE_W_LO, ROUTE_W_HI))]
        o_ref[...] = (jnp.dot(hes[0], wd0[0], preferred_element_type=F32)
                      + jnp.dot(hes[1], wd1[0], preferred_element_type=F32)).astype(o_ref.dtype)

    @pl.when(valid_ref[t] == 0)
    def _():
        o_ref[...] = jnp.zeros_like(o_ref)


def _moe_experts(plan, xg, wsg, wg, wu, wd):
    rows, d = xg.shape
    tm = MOE_TM
    tok = lambda width: pl.BlockSpec((tm, width), lambda t, xt, elo, ehi, valid: (xt[t], 0))
    lo3 = lambda shape: pl.BlockSpec(shape, lambda t, xt, elo, ehi, valid: (elo[t], 0, 0))
    hi3 = lambda shape: pl.BlockSpec(shape, lambda t, xt, elo, ehi, valid: (ehi[t], 0, 0))
    return pl.pallas_call(
        _moe_expert_kernel,
        out_shape=jax.ShapeDtypeStruct((rows, d), BF16),
        grid_spec=pltpu.PrefetchScalarGridSpec(
            num_scalar_prefetch=4,
            grid=(rows // tm,),
            in_specs=[tok(d), tok(LANES),
                      lo3((1, d, D_EXPERT)), lo3((1, d, D_EXPERT)), lo3((1, D_EXPERT, d)),
                      hi3((1, d, D_EXPERT)), hi3((1, d, D_EXPERT)), hi3((1, D_EXPERT, d))],
            out_specs=pl.BlockSpec((tm, d), lambda t, *_: (t, 0))),
        compiler_params=pltpu.CompilerParams(dimension_semantics=("arbitrary",), vmem_limit_bytes=VMEM_LIMIT),
        name="moe_experts",
    )(plan["xtile"], plan["elo"], plan["ehi"], plan["valid"], xg, wsg, wg, wu, wd, wg, wu, wd)


def _moe_unsort_kernel(gmap_ref, og_ref, route_ref, h_ref, lstrict_ref, g_ref, b_ref, out_ref, ol_ref, sem):
    s = pl.program_id(0)
    nsteps = pl.num_programs(0)
    slot = s & 1
    ts = route_ref.shape[0]

    def gather(step, fn):
        sl = step & 1

        def body(g, carry):
            src = pl.ds(pl.multiple_of(gmap_ref[step * LGRAN + g] * GRAN, GRAN), GRAN)
            dst = pl.ds(pl.multiple_of(g * GRAN, GRAN), GRAN)
            fn(pltpu.make_async_copy(og_ref.at[src, :], ol_ref.at[sl, dst, :], sem.at[sl]))
            return carry
        lax.fori_loop(0, LGRAN, body, 0)

    @pl.when(s == 0)
    def _():
        gather(s, lambda cp: cp.start())

    @pl.when(s + 1 < nsteps)
    def _():
        gather(s + 1, lambda cp: cp.start())

    route = route_ref[...]
    bucket_col = route[:, ROUTE_BUCKET:ROUTE_BUCKET + 1].astype(jnp.int32)
    lane = lax.broadcasted_iota(jnp.int32, (ts, LANES), 1)
    ohf = jnp.where(lane == bucket_col, 1.0, 0.0)
    cnt = jnp.sum(ohf, axis=0, keepdims=True).astype(jnp.int32)
    pad = (((cnt + (GRAN - 1)) >> GRAN_SHIFT) << GRAN_SHIFT).astype(F32)
    r = lax.broadcasted_iota(jnp.int32, (LANES, LANES), 0)
    c = lax.broadcasted_iota(jnp.int32, (LANES, LANES), 1)
    loff = _dot(jnp.broadcast_to(pad, (8, LANES)), jnp.where(r < c, 1.0, 0.0))[0:1, :]
    rank = jnp.dot(lstrict_ref[...], ohf.astype(BF16), preferred_element_type=F32)
    dest = jnp.sum(ohf * (loff + rank), axis=1, keepdims=True).astype(jnp.int32)
    lrow = lax.broadcasted_iota(jnp.int32, (ts, LROWS), 1)
    perm_t = jnp.where(lrow == dest, 1.0, 0.0).astype(BF16)

    gather(s, lambda cp: cp.wait())
    ffn = jnp.dot(perm_t, ol_ref[slot], preferred_element_type=F32)
    out_ref[...] = _layer_norm(DEEPNORM_ALPHA * h_ref[...] + ffn, g_ref[...], b_ref[...])


def _moe_unsort(plan, og, route, hf, lstrict, g2, b2):
    n, d = hf.shape
    ts = MOE_TS
    row = pl.BlockSpec((1, d), lambda s, *_: (0, 0))
    return pl.pallas_call(
        _moe_unsort_kernel,
        out_shape=jax.ShapeDtypeStruct((n, d), F32),
        grid_spec=pltpu.PrefetchScalarGridSpec(
            num_scalar_prefetch=1,
            grid=(n // ts,),
            in_specs=[pl.BlockSpec(memory_space=pl.ANY), pl.BlockSpec((ts, LANES), lambda s, *_: (s, 0)),
                      pl.BlockSpec((ts, d), lambda s, *_: (s, 0)), pl.BlockSpec((ts, ts), lambda s, *_: (0, 0)),
                      row, row],
            out_specs=pl.BlockSpec((ts, d), lambda s, *_: (s, 0)),
            scratch_shapes=[pltpu.VMEM((2, LROWS, d), BF16), pltpu.SemaphoreType.DMA((2,))]),
        compiler_params=pltpu.CompilerParams(dimension_semantics=("arbitrary",), vmem_limit_bytes=VMEM_LIMIT),
        name="moe_unsort",
    )(plan["gmap_back"], og, route, hf, lstrict, g2, b2)


def _moe_plan(cnt_half, n):
    nsrc = n // MOE_TS
    i32 = jnp.int32
    cnt = cnt_half.reshape(nsrc, -1, LANES).sum(axis=1)[:, :N_BUCKETS].astype(i32)
    run_g = (cnt + GRAN - 1) // GRAN
    nvalid = run_g.sum(axis=1)
    loff_g = jnp.cumsum(run_g, axis=1) - run_g
    bucket_g = run_g.sum(axis=0)
    gpt = MOE_TM // GRAN
    btiles = (bucket_g + gpt - 1) // gpt
    tend = jnp.cumsum(btiles)
    tstart = tend - btiles
    gofs = tstart[None, :] * gpt + jnp.cumsum(run_g, axis=0) - run_g
    n_tiles = -(-(n + nsrc * N_BUCKETS * (GRAN - 1)) // MOE_TM) + N_BUCKETS + 1
    g = jnp.arange(LGRAN, dtype=i32)[None, :, None]
    in_run = (g >= loff_g[:, None, :]) & (g < (loff_g + run_g)[:, None, :])
    gmap = jnp.arange(LGRAN, dtype=i32)[None, :] + jnp.sum(jnp.where(in_run, (gofs - loff_g)[:, None, :], 0), axis=2)
    is_valid = jnp.arange(LGRAN, dtype=i32)[None, :] < nvalid[:, None]
    zero_gran = (n_tiles - 1) * gpt
    t = jnp.arange(n_tiles, dtype=i32)
    tb = jnp.minimum(jnp.sum(t[:, None] >= tend[None, :], axis=1), N_BUCKETS - 1)
    valid = (t < tend[-1]).astype(i32)
    pairs = [(a, b) for a in range(EXPERTS_PER_GROUP) for b in range(a + 1, EXPERTS_PER_GROUP)]
    pidx = tb % N_PAIRS
    pair_a = sum(jnp.where(pidx == i, a, 0) for i, (a, _) in enumerate(pairs))
    pair_b = sum(jnp.where(pidx == i, b, 0) for i, (_, b) in enumerate(pairs))
    grp = tb // N_PAIRS
    return {
        "n_tiles": n_tiles,
        "gmap": jnp.where(is_valid, gmap, 0).reshape(-1).astype(i32),
        "gmap_back": jnp.where(is_valid, gmap, zero_gran).reshape(-1).astype(i32),
        "nvalid": nvalid.astype(i32),
        "tail0": jnp.concatenate([tstart * gpt + bucket_g, tend[-1:]]).astype(i32),
        "taillen": (btiles * gpt - bucket_g).astype(i32),
        "xtile": jnp.where(valid > 0, t, 0).astype(i32),
        "elo": (grp * EXPERTS_PER_GROUP + pair_a).astype(i32),
        "ehi": (grp * EXPERTS_PER_GROUP + pair_b).astype(i32),
        "valid": valid,
    }


def _pad_lanes(a, lane0=0):
    return jnp.zeros((1, LANES), F32).at[0, lane0:lane0 + a.shape[0]].set(a.astype(F32))


def _rope_tables(seq):
    half = HEAD_DIM // 2
    inv_freq = ROPE_THETA ** (-jnp.arange(half, dtype=F32) / half)
    ang = jnp.arange(seq).astype(F32)[:, None] * inv_freq[None, :]
    cos, sin = jnp.cos(ang), jnp.sin(ang)
    return jnp.concatenate([cos, cos], axis=-1), jnp.concatenate([-sin, sin], axis=-1)


def _layer(x, w_in, conv_w, a_log, dt_bias, dn_norm_w, w_out, ln1_g, ln1_b, router_w1, router_b1,
           router_w2, router_b2, w_gate, w_up, w_down, ln2_g, ln2_b):
    bsz, seq, d = x.shape
    n = bsz * seq
    x2 = x.reshape(n, d)

    o_z, o_b, o_mb = 3 * D_DN, 4 * D_DN, 4 * D_DN + 2 * N_HEADS_DN
    w_ba = jnp.pad(w_in[:, o_b:o_mb], ((0, 0), (0, LANES - 2 * N_HEADS_DN)))
    w_all = jnp.concatenate([w_in[:, :o_z], w_in[:, o_z:o_b], w_ba, w_in[:, o_mb:]], axis=1).astype(BF16)

    tm = min(512, n)
    dn_qkv, z, ba, mb_qkv = _in_proj(x2, w_all, conv_w, tm, seq)

    y_dn = _deltanet(dn_qkv.reshape(bsz, seq, 3 * D_DN), z.reshape(bsz, seq, D_DN), ba.reshape(bsz, seq, LANES),
                     _pad_lanes(a_log, N_HEADS_DN), _pad_lanes(dt_bias, N_HEADS_DN),
                     dn_norm_w.astype(F32).reshape(1, HEAD_DIM))

    cos_t, sin_t = _rope_tables(seq)
    q_r, k_r, vt, sel = _moba_prep(mb_qkv.reshape(bsz, seq, 3 * D_MOBA), cos_t, sin_t)
    y_mb = _moba_attn(q_r, k_r, vt, sel)

    rw = jnp.concatenate([router_w1, jnp.transpose(router_w2, (1, 0, 2)).reshape(d, N_EXPERTS)], axis=1)
    rw = jnp.pad(rw, ((0, 0), (0, LANES - rw.shape[1])))
    rb = _pad_lanes(jnp.concatenate([router_b1, router_b2.reshape(-1)]))
    hf, hb, route, cnt = _mix_route(y_dn.reshape(n, D_DN), y_mb.reshape(n, D_MOBA), x2, w_out.astype(BF16),
                                    ln1_g.reshape(1, d), ln1_b.reshape(1, d), rw, rb, tm)

    plan = _moe_plan(cnt, n)
    idx = jnp.arange(MOE_TS, dtype=jnp.int32)
    lstrict = (idx[None, :] < idx[:, None]).astype(BF16)
    xg, wsg = _moe_sort(plan, hb, route, lstrict)
    og = _moe_experts(plan, xg, wsg, w_gate.astype(BF16), w_up.astype(BF16), w_down.astype(BF16))
    out = _moe_unsort(plan, og, route, hf, lstrict, ln2_g.reshape(1, d), ln2_b.reshape(1, d))
    return out.reshape(bsz, seq, d)


def kernel(x, w_in, conv_w, a_log, dt_bias, dn_norm_w, w_out, ln1_g, ln1_b, router_w1, router_b1, router_w2, router_b2, expert_w_gate, expert_w_up, expert_w_down, ln2_g, ln2_b):
    for l in range(DEPTH):
        x = _layer(x, w_in[l], conv_w[l], a_log[l], dt_bias[l], dn_norm_w[l], w_out[l], ln1_g[l], ln1_b[l],
                   router_w1[l], router_b1[l], router_w2[l], router_b2[l], expert_w_gate[l], expert_w_up[l],
                   expert_w_down[l], ln2_g[l], ln2_b[l])
    return x
```

```python
import functools

import jax
import jax.numpy as jnp
from jax import lax
from jax.experimental import pallas as pl
from jax.experimental.pallas import tpu as pltpu

F32 = jnp.float32
BF16 = jnp.bfloat16

HEAD_DIM = 128
N_HEADS_DN = 4
N_HEADS_MOBA = 4
D_DN = N_HEADS_DN * HEAD_DIM
D_MOBA = N_HEADS_MOBA * HEAD_DIM
CONV_K = 4
DN_CHUNK = 64
MOBA_BLOCK = 256
MOBA_TOPK = 3
ROPE_THETA = 10000.0
N_GROUPS = 4
EXPERTS_PER_GROUP = 4
N_EXPERTS = N_GROUPS * EXPERTS_PER_GROUP
D_EXPERT = 256
LN_EPS = 1e-5
RMS_EPS = 1e-6
L2_EPS = 1e-6
NEG_INF = -1e30
LOG2E = 1.4426950408889634
DEPTH = 1
DEEPNORM_ALPHA = (2 * DEPTH) ** 0.25

LANES = 128
DN_TILE = 256
DN_HEADS_PER_STEP = 4
GATE_LANE0 = N_GROUPS
N_PAIRS = EXPERTS_PER_GROUP * (EXPERTS_PER_GROUP - 1) // 2
N_BUCKETS = N_GROUPS * N_PAIRS
ROUTE_BUCKET, ROUTE_W_LO, ROUTE_W_HI = 0, 1, 2
MOE_TS = 512
MOE_TM = 256
GRAN = 16
GRAN_SHIFT = 4
LROWS = -(-(MOE_TS + N_BUCKETS * (GRAN - 1)) // LANES) * LANES
LGRAN = LROWS // GRAN
VMEM_LIMIT = 48 * 1024 * 1024


def _dot(a, b):
    return jnp.dot(a.astype(BF16), b.astype(BF16), preferred_element_type=F32)


def _dot_nt(a, b):
    return lax.dot_general(a.astype(BF16), b.astype(BF16), (((1,), (1,)), ((), ())),
                           preferred_element_type=F32)


def _split2(a):
    hi = a.astype(BF16)
    lo = (a - hi.astype(F32)).astype(BF16)
    return hi, lo


def _split3(a):
    hi = a.astype(BF16)
    r = a - hi.astype(F32)
    mid = r.astype(BF16)
    lo = (r - mid.astype(F32)).astype(BF16)
    return hi, mid, lo


def _dot3(a, b):
    ah, al = _split2(a)
    bh, bl = _split2(b)
    return (jnp.dot(ah, bh, preferred_element_type=F32) + jnp.dot(ah, bl, preferred_element_type=F32)
            + jnp.dot(al, bh, preferred_element_type=F32))


def _dot3_nt(a, b):
    ah, al = _split2(a)
    bh, bl = _split2(b)
    dn = (((1,), (1,)), ((), ()))
    return (lax.dot_general(ah, bh, dn, preferred_element_type=F32)
            + lax.dot_general(ah, bl, dn, preferred_element_type=F32)
            + lax.dot_general(al, bh, dn, preferred_element_type=F32))


def _dot_exact_lhs(a_bf16, b):
    bh, bm, bl = _split3(b)
    return (jnp.dot(a_bf16, bh, preferred_element_type=F32) + jnp.dot(a_bf16, bm, preferred_element_type=F32)
            + jnp.dot(a_bf16, bl, preferred_element_type=F32))


def _silu(x):
    return x * jax.nn.sigmoid(x)


def _softplus(x):
    return jnp.maximum(x, 0.0) + jnp.log1p(jnp.exp(-jnp.abs(x)))


def _layer_norm(t, g, b):
    mu = jnp.mean(t, axis=-1, keepdims=True)
    d = t - mu
    var = jnp.mean(d * d, axis=-1, keepdims=True)
    return d * lax.rsqrt(var + LN_EPS) * g + b


def _lane_pick(x, lane):
    ids = lax.broadcasted_iota(jnp.int32, x.shape, 1)
    return jnp.sum(jnp.where(ids == lane, x, 0.0), axis=1, keepdims=True)


def _in_proj_kernel(x_ref, w_ref, cw_ref, dn_ref, z_ref, ba_ref, mb_ref, cb_ref, *, tiles_per_seq):
    i = pl.program_id(0)
    tm = x_ref.shape[0]
    o0 = 3 * D_DN
    o1 = o0 + D_DN
    o2 = o1 + LANES

    @pl.when(i % tiles_per_seq == 0)
    def _():
        cb_ref[0:8, :] = jnp.zeros((8, o0), F32)

    xb = x_ref[...].astype(BF16)
    u = jnp.dot(xb, w_ref[:, 0:o0], preferred_element_type=F32)
    z_ref[...] = _silu(jnp.dot(xb, w_ref[:, o0:o1], preferred_element_type=F32))
    ba_ref[...] = jnp.dot(xb, w_ref[:, o1:o2], preferred_element_type=F32)
    mb_ref[...] = jnp.dot(xb, w_ref[:, o2:o2 + 3 * D_MOBA], preferred_element_type=F32)

    cb_ref[8:8 + tm, :] = u
    acc = cw_ref[CONV_K - 1:CONV_K, :] * u
    for s in range(1, CONV_K):
        acc = acc + cw_ref[CONV_K - 1 - s:CONV_K - s, :] * cb_ref[8 - s:8 - s + tm, :]
    cb_ref[0:8, :] = u[tm - 8:tm, :]
    qkv = _silu(acc)

    outs = []
    for h in range(2 * N_HEADS_DN):
        t = qkv[:, h * HEAD_DIM:(h + 1) * HEAD_DIM]
        t = t * lax.rsqrt(jnp.sum(t * t, axis=-1, keepdims=True) + L2_EPS)
        outs.append(t * (HEAD_DIM ** -0.5) if h < N_HEADS_DN else t)
    outs.append(qkv[:, 2 * D_DN:3 * D_DN])
    dn_ref[...] = jnp.concatenate(outs, axis=1)


def _in_proj(x2, w_all, conv_w, tm, seq):
    n, d = x2.shape
    wc = w_all.shape[1]
    return pl.pallas_call(
        functools.partial(_in_proj_kernel, tiles_per_seq=seq // tm),
        out_shape=(jax.ShapeDtypeStruct((n, 3 * D_DN), F32), jax.ShapeDtypeStruct((n, D_DN), F32),
                   jax.ShapeDtypeStruct((n, LANES), F32), jax.ShapeDtypeStruct((n, 3 * D_MOBA), F32)),
        grid=(n // tm,),
        in_specs=[pl.BlockSpec((tm, d), lambda i: (i, 0)), pl.BlockSpec((d, wc), lambda i: (0, 0)),
                  pl.BlockSpec((CONV_K, 3 * D_DN), lambda i: (0, 0))],
        out_specs=(pl.BlockSpec((tm, 3 * D_DN), lambda i: (i, 0)), pl.BlockSpec((tm, D_DN), lambda i: (i, 0)),
                   pl.BlockSpec((tm, LANES), lambda i: (i, 0)), pl.BlockSpec((tm, 3 * D_MOBA), lambda i: (i, 0))),
        scratch_shapes=[pltpu.VMEM((8 + tm, 3 * D_DN), F32)],
        compiler_params=pltpu.CompilerParams(dimension_semantics=("arbitrary",), vmem_limit_bytes=VMEM_LIMIT),
        name="in_proj",
    )(x2, w_all, conv_w)


def _inv_unit_lower(a_list, row, col):
    eye = (row == col).astype(F32)
    d8 = (row >> 3) == (col >> 3)
    a8 = [jnp.where(d8, a, 0.0) for a in a_list]
    a8_2 = [_dot(a, a) for a in a8]
    a8_4 = [_dot(a, a) for a in a8_2]
    x = [_dot(eye - a, eye + a2) for a, a2 in zip(a8, a8_2)]
    x = [_dot(xi, eye + a4) for xi, a4 in zip(x, a8_4)]
    s = 8
    while s < DN_CHUNK:
        sh = s.bit_length() - 1
        off = ((row >> (sh + 1)) == (col >> (sh + 1))) & ((row >> sh) != (col >> sh))
        y = [_dot(jnp.where(off, a, 0.0), xi) for a, xi in zip(a_list, x)]
        x = [xi - _dot(xi, yi) for xi, yi in zip(x, y)]
        s *= 2
    return x


def _deltanet_heads(q, k, v, z, beta, gcc, gcr, state, normw, masks):
    row, col, incl, strict = masks
    nh = len(q)
    hs = range(nh)
    tt = q[0].shape[0]
    nchunk = tt // DN_CHUNK
    decay = [jnp.where(incl, jnp.exp(jnp.where(incl, gcc[h] - gcr[h], 0.0)), 0.0) for h in hs]
    kb = [k[h] * beta[h] for h in hs]
    vb = [v[h] * beta[h] for h in hs]
    a_mat = [jnp.where(strict, _dot_nt(kb[h], k[h]) * decay[h], 0.0) for h in hs]
    qk = [_dot_nt(q[h], k[h]) * decay[h] for h in hs]
    tinv = _inv_unit_lower(a_mat, row, col)

    eg = [jnp.exp(g) for g in gcc]
    wu = [_dot(tinv[h], jnp.concatenate([kb[h] * eg[h], vb[h]], axis=1)) for h in hs]
    qd = [q[h] * eg[h] for h in hs]

    gl_rows = [[g[(c + 1) * DN_CHUNK - 1:(c + 1) * DN_CHUNK, :] for c in range(nchunk)] for g in gcc]
    gl_col = [jnp.concatenate([jnp.broadcast_to(g, (DN_CHUNK, 1)) for g in rows], axis=0) for rows in gl_rows]
    kdt = [jnp.transpose(k[h] * jnp.exp(gl_col[h] - gcc[h])) for h in hs]

    outs = [[] for _ in hs]
    for c in range(nchunk):
        lo, hi = c * DN_CHUNK, (c + 1) * DN_CHUNK
        r = [_dot(jnp.concatenate([wu[h][lo:hi, 0:HEAD_DIM], qd[h][lo:hi, :]], axis=0), state[h]) for h in hs]
        vz = []
        for h in hs:
            parts = []
            if lo > 0:
                parts.append(jnp.zeros((lo, HEAD_DIM), F32))
            parts.append(wu[h][lo:hi, HEAD_DIM:2 * HEAD_DIM] - r[h][0:DN_CHUNK, :])
            if hi < tt:
                parts.append(jnp.zeros((tt - hi, HEAD_DIM), F32))
            vz.append(jnp.concatenate(parts, axis=0).astype(BF16))
        for h in hs:
            outs[h].append(r[h][DN_CHUNK:2 * DN_CHUNK, :] + _dot(qk[h][lo:hi, :], vz[h]))
        state = [state[h] * jnp.exp(gl_rows[h][c]) + _dot(kdt[h], vz[h]) for h in hs]

    ys = []
    for h in hs:
        o = jnp.concatenate(outs[h], axis=0)
        o = o * lax.rsqrt(jnp.mean(o * o, axis=-1, keepdims=True) + RMS_EPS) * normw
        ys.append(o * z[h])
    return ys, state


def _deltanet_kernel(q_ref, k_ref, v_ref, z_ref, ba_ref, alog_ref, dtb_ref, normw_ref, y_ref, s_ref, *, hb):
    hg = pl.program_id(1)
    t = pl.program_id(2)
    tt = DN_TILE

    @pl.when(t == 0)
    def _():
        s_ref[...] = jnp.zeros_like(s_ref)

    q_all = q_ref[0]
    k_all = k_ref[0]
    v_all = v_ref[0]

    ba = ba_ref[0]
    beta_all = jax.nn.sigmoid(ba)
    g_all = -jnp.exp(alog_ref[...]) * _softplus(ba + dtb_ref[...])

    row = lax.broadcasted_iota(jnp.int32, (tt, tt), 0)
    col = lax.broadcasted_iota(jnp.int32, (tt, tt), 1)
    same = (row >> 6) == (col >> 6)
    incl = same & (row >= col)
    strict = same & (row > col)
    masks = (row, col, incl, strict)

    gc_all = _dot_exact_lhs(incl.astype(BF16), g_all)
    gct = jnp.transpose(gc_all)
    sub = lax.broadcasted_iota(jnp.int32, gct.shape, 0)

    sls = [slice(hh * HEAD_DIM, (hh + 1) * HEAD_DIM) for hh in range(hb)]
    heads = [hg * hb + hh for hh in range(hb)]
    beta = [_lane_pick(beta_all, h) for h in heads]
    gcc = [_lane_pick(gc_all, h + N_HEADS_DN) for h in heads]
    gcr = [jnp.sum(jnp.where(sub == h + N_HEADS_DN, gct, 0.0), axis=0, keepdims=True) for h in heads]
    ys, states = _deltanet_heads([q_all[:, sl] for sl in sls], [k_all[:, sl] for sl in sls],
                                 [v_all[:, sl] for sl in sls], [z_ref[0, :, sl] for sl in sls], beta, gcc, gcr,
                                 [s_ref[hh] for hh in range(hb)], normw_ref[...], masks)
    s_ref[...] = jnp.stack(states, axis=0)
    y_ref[0] = jnp.concatenate(ys, axis=1).astype(y_ref.dtype)


def _deltanet(dn_qkv, z, ba, alog_row, dtb_row, normw_row):
    bsz, seq, _ = dn_qkv.shape
    tt = DN_TILE
    hb = DN_HEADS_PER_STEP
    ng = N_HEADS_DN // hb
    w = hb * HEAD_DIM

    def col_spec(off):
        return pl.BlockSpec((1, tt, w), lambda b, g, t: (b, t, g + off))

    row_spec = pl.BlockSpec((1, LANES), lambda b, g, t: (0, 0))
    return pl.pallas_call(
        functools.partial(_deltanet_kernel, hb=hb),
        out_shape=jax.ShapeDtypeStruct((bsz, seq, D_DN), BF16),
        grid=(bsz, ng, seq // tt),
        in_specs=[col_spec(0), col_spec(ng), col_spec(2 * ng), col_spec(0),
                  pl.BlockSpec((1, tt, LANES), lambda b, g, t: (b, t, 0)), row_spec, row_spec, row_spec],
        out_specs=col_spec(0),
        scratch_shapes=[pltpu.VMEM((hb, HEAD_DIM, HEAD_DIM), F32)],
        compiler_params=pltpu.CompilerParams(dimension_semantics=("parallel", "parallel", "arbitrary"),
                                             vmem_limit_bytes=VMEM_LIMIT),
        name="deltanet",
    )(dn_qkv, dn_qkv, dn_qkv, z, ba, alog_row, dtb_row, normw_row)


def _moba_prep_kernel(x_ref, cos_ref, sin_ref, q_ref, k_ref, vt_ref, sel_ref, km_ref, *, nb, topk):
    j = pl.program_id(1)

    @pl.when(j == 0)
    def _():
        km_ref[...] = jnp.zeros_like(km_ref)

    cos = cos_ref[...]
    sin = sin_ref[...]
    half = HEAD_DIM // 2
    blk = lax.broadcasted_iota(jnp.int32, (nb, MOBA_BLOCK), 0)
    kmeans = []
    for h in range(N_HEADS_MOBA):
        qh = x_ref[0, :, h * HEAD_DIM:(h + 1) * HEAD_DIM]
        kh = x_ref[0, :, D_MOBA + h * HEAD_DIM:D_MOBA + (h + 1) * HEAD_DIM]
        qr = (qh * cos + pltpu.roll(qh, half, 1) * sin) * (HEAD_DIM ** -0.5)
        kr = kh * cos + pltpu.roll(kh, half, 1) * sin
        q_ref[0, :, h * HEAD_DIM:(h + 1) * HEAD_DIM] = (qr * LOG2E).astype(q_ref.dtype)
        k_ref[0, :, h * HEAD_DIM:(h + 1) * HEAD_DIM] = kr.astype(k_ref.dtype)
        kmeans.append(jnp.mean(kr, axis=0, keepdims=True))

        gate = _dot3_nt(km_ref[:, h * HEAD_DIM:(h + 1) * HEAD_DIM], qr)
        gate = jnp.where(blk < j, gate, NEG_INF)
        rank = jnp.zeros(gate.shape, F32)
        for m in range(nb):
            gm = gate[m:m + 1, :]
            ahead = (gm > gate) | ((gm == gate) & (blk > m))
            rank = rank + jnp.where(ahead, 1.0, 0.0)
        sel = (blk < j) & (rank < topk)
        sel_ref[0, 0, h * nb:(h + 1) * nb, :] = jnp.where(sel, 1.0, 0.0)

    km_ref[pl.ds(j, 1), :] = jnp.concatenate(kmeans, axis=1)
    vt_ref[0, 0] = jnp.transpose(x_ref[0, :, 2 * D_MOBA:3 * D_MOBA]).astype(vt_ref.dtype)


def _moba_prep(mb_qkv, cos_t, sin_t):
    bsz, seq, _ = mb_qkv.shape
    nb = seq // MOBA_BLOCK
    topk = min(MOBA_TOPK, nb)
    kern = functools.partial(_moba_prep_kernel, nb=nb, topk=topk)
    tok_spec = pl.BlockSpec((1, MOBA_BLOCK, D_MOBA), lambda b, j: (b, j, 0))
    tab_spec = pl.BlockSpec((MOBA_BLOCK, HEAD_DIM), lambda b, j: (j, 0))
    return pl.pallas_call(
        kern,
        out_shape=(jax.ShapeDtypeStruct((bsz, seq, D_MOBA), BF16), jax.ShapeDtypeStruct((bsz, seq, D_MOBA), BF16),
                   jax.ShapeDtypeStruct((bsz, nb, D_MOBA, MOBA_BLOCK), BF16),
                   jax.ShapeDtypeStruct((bsz, nb, N_HEADS_MOBA * nb, MOBA_BLOCK), F32)),
        grid=(bsz, nb),
        in_specs=[pl.BlockSpec((1, MOBA_BLOCK, 3 * D_MOBA), lambda b, j: (b, j, 0)), tab_spec, tab_spec],
        out_specs=(tok_spec, tok_spec,
                   pl.BlockSpec((1, 1, D_MOBA, MOBA_BLOCK), lambda b, j: (b, j, 0, 0)),
                   pl.BlockSpec((1, 1, N_HEADS_MOBA * nb, MOBA_BLOCK), lambda b, j: (b, j, 0, 0))),
        scratch_shapes=[pltpu.VMEM((nb, D_MOBA), F32)],
        compiler_params=pltpu.CompilerParams(dimension_semantics=("parallel", "arbitrary"),
                                             vmem_limit_bytes=VMEM_LIMIT),
        name="moba_prep",
    )(mb_qkv, cos_t, sin_t)


def _moba_attn_kernel(q_ref, k_ref, vt_ref, sel_ref, o_ref, acc_ref, *, nb):
    j = pl.program_id(1)
    blk = MOBA_BLOCK
    cw = 2 * blk
    nh = N_HEADS_MOBA
    nc = (j + 1) // 2
    dn = (((1,), (1,)), ((), ()))
    hsl = [slice(h * HEAD_DIM, (h + 1) * HEAD_DIM) for h in range(nh)]
    qs = [q_ref[0, :, hsl[h]] for h in range(nh)]

    ki = lax.broadcasted_iota(jnp.int32, (blk, blk), 0)
    qi = lax.broadcasted_iota(jnp.int32, (blk, blk), 1)
    own = pl.ds(pl.multiple_of(j * blk, blk), blk)
    s_own = [jnp.where(ki <= qi, lax.dot_general(k_ref[0, own, hsl[h]], qs[h], dn, preferred_element_type=F32),
                       NEG_INF) for h in range(nh)]

    def scores(c, h):
        kc = k_ref[0, pl.ds(pl.multiple_of(c * cw, cw), cw), hsl[h]]
        s = lax.dot_general(kc, qs[h], dn, preferred_element_type=F32)
        parts = []
        for i in range(2):
            selrow = sel_ref[0, 0, pl.ds(h * nb + 2 * c + i, 1), :]
            parts.append(jnp.where(selrow > 0.5, s[i * blk:(i + 1) * blk, :], NEG_INF))
        return jnp.concatenate(parts, axis=0)

    ones8 = jnp.ones((8, cw), BF16)
    ms, ls = [], []
    for h in range(nh):
        m = jnp.max(s_own[h], axis=0, keepdims=True)
        pb = jnp.exp2(s_own[h] - m).astype(BF16)
        ms.append(m)
        ls.append(jnp.dot(ones8[:, 0:blk], pb, preferred_element_type=F32)[0:1, :])
        acc_ref[h] = jnp.dot(vt_ref[0, j, hsl[h], :], pb, preferred_element_type=F32)

    def softmax_pv(c, h, s, m, l):
        m_new = jnp.maximum(m, jnp.max(s, axis=0, keepdims=True))
        alpha = jnp.exp2(m - m_new)
        pb = jnp.exp2(s - m_new).astype(BF16)
        l = alpha * l + jnp.dot(ones8, pb, preferred_element_type=F32)[0:1, :]
        pv = (jnp.dot(vt_ref[0, 2 * c, hsl[h], :], pb[0:blk, :], preferred_element_type=F32)
              + jnp.dot(vt_ref[0, 2 * c + 1, hsl[h], :], pb[blk:cw, :], preferred_element_type=F32))
        acc_ref[h] = acc_ref[h] * alpha + pv
        return m_new, l

    def body(pair, carry):
        ms, ls = (list(t) for t in carry)
        c0 = 2 * pair
        s0 = [scores(c0, h) for h in range(nh)]
        s1 = []
        for h in range(nh):
            s1.append(scores(c0 + 1, h))
            ms[h], ls[h] = softmax_pv(c0, h, s0[h], ms[h], ls[h])
        for h in range(nh):
            ms[h], ls[h] = softmax_pv(c0 + 1, h, s1[h], ms[h], ls[h])
        return tuple(ms), tuple(ls)

    ms, ls = lax.fori_loop(0, (nc + 1) // 2, body, (tuple(ms), tuple(ls)))
    o_ref[0] = jnp.concatenate([jnp.transpose(acc_ref[h] / ls[h]) for h in range(nh)],
                               axis=1).astype(o_ref.dtype)


def _moba_attn(q_r, k_r, vt, sel):
    bsz, seq, _ = q_r.shape
    nb = seq // MOBA_BLOCK
    tok_spec = pl.BlockSpec((1, MOBA_BLOCK, D_MOBA), lambda b, j: (b, j, 0))
    return pl.pallas_call(
        functools.partial(_moba_attn_kernel, nb=nb),
        out_shape=jax.ShapeDtypeStruct((bsz, seq, D_MOBA), BF16),
        grid=(bsz, nb),
        in_specs=[tok_spec,
                  pl.BlockSpec((1, seq, D_MOBA), lambda b, j: (b, 0, 0)),
                  pl.BlockSpec((1, nb, D_MOBA, MOBA_BLOCK), lambda b, j: (b, 0, 0, 0)),
                  pl.BlockSpec((1, 1, N_HEADS_MOBA * nb, MOBA_BLOCK), lambda b, j: (b, j, 0, 0))],
        out_specs=tok_spec,
        scratch_shapes=[pltpu.VMEM((N_HEADS_MOBA, HEAD_DIM, MOBA_BLOCK), F32)],
        compiler_params=pltpu.CompilerParams(dimension_semantics=("parallel", "arbitrary"),
                                             vmem_limit_bytes=VMEM_LIMIT),
        name="moba_attn",
    )(q_r, k_r, vt, sel)


def _mix_route_kernel(ydn_ref, ymb_ref, x_ref, wo_ref, g_ref, b_ref, rw_ref, rb_ref, h_ref, hb_ref, route_ref,
                      cnt_ref):
    mix = (jnp.dot(ydn_ref[...], wo_ref[0:D_DN, :], preferred_element_type=F32)
           + jnp.dot(ymb_ref[...], wo_ref[D_DN:D_DN + D_MOBA, :], preferred_element_type=F32))
    hval = _layer_norm(DEEPNORM_ALPHA * x_ref[...] + mix, g_ref[...], b_ref[...])
    h_ref[...] = hval
    hb_ref[...] = hval.astype(BF16)

    hh, hl = _split2(hval)
    wh, wl = _split2(rw_ref[...])
    both = jnp.dot(hh, jnp.concatenate([wh, wl], axis=1), preferred_element_type=F32)
    logits = (both[:, 0:LANES] + both[:, LANES:2 * LANES] + jnp.dot(hl, wh, preferred_element_type=F32)
              + rb_ref[...])
    lane = lax.broadcasted_iota(jnp.int32, logits.shape, 1)
    big = jnp.int32(LANES)

    def first_lane(mask):
        return jnp.min(jnp.where(mask, lane, big), axis=1, keepdims=True)

    is_g = lane < N_GROUPS
    m1 = jnp.max(jnp.where(is_g, logits, NEG_INF), axis=1, keepdims=True)
    s1 = jnp.sum(jnp.where(is_g, jnp.exp(logits - m1), 0.0), axis=1, keepdims=True)
    pg = 1.0 / s1
    gsel = first_lane(is_g & (logits == m1))

    in_grp = (lane >= GATE_LANE0) & (((lane - GATE_LANE0) >> 2) == gsel) & (lane < GATE_LANE0 + N_EXPERTS)
    m2 = jnp.max(jnp.where(in_grp, logits, NEG_INF), axis=1, keepdims=True)
    s2 = jnp.sum(jnp.where(in_grp, jnp.exp(logits - m2), 0.0), axis=1, keepdims=True)
    e1 = first_lane(in_grp & (logits == m2))
    rest = in_grp & (lane != e1)
    m2b = jnp.max(jnp.where(rest, logits, NEG_INF), axis=1, keepdims=True)
    e2 = first_lane(rest & (logits == m2b))
    pe1 = 1.0 / s2
    pe2 = jnp.exp(m2b - m2) / s2
    tot = pe1 + pe2
    w1 = pg * (pe1 / tot)
    w2 = pg * (pe2 / tot)
    first_lo = e1 < e2
    lo = jnp.minimum(e1, e2)
    hi = jnp.maximum(e1, e2)
    a = (lo - GATE_LANE0) & (EXPERTS_PER_GROUP - 1)
    b = (hi - GATE_LANE0) & (EXPERTS_PER_GROUP - 1)
    bucket = gsel * N_PAIRS + ((a * (2 * EXPERTS_PER_GROUP - 1 - a)) >> 1) + (b - a - 1)
    route_ref[...] = jnp.where(lane == ROUTE_BUCKET, bucket.astype(F32),
                               jnp.where(lane == ROUTE_W_LO, jnp.where(first_lo, w1, w2),
                                         jnp.where(lane == ROUTE_W_HI, jnp.where(first_lo, w2, w1), 0.0)))
    cnt_ref[0] = jnp.sum(jnp.where(lane == bucket, 1.0, 0.0), axis=0, keepdims=True)


def _mix_route(y_dn, y_mb, x2, wo, g1, b1, rw, rb, tm):
    n, d = x2.shape
    row = lambda w: pl.BlockSpec((1, w), lambda i: (0, 0))
    return pl.pallas_call(
        _mix_route_kernel,
        out_shape=(jax.ShapeDtypeStruct((n, d), F32), jax.ShapeDtypeStruct((n, d), BF16),
                   jax.ShapeDtypeStruct((n, LANES), F32), jax.ShapeDtypeStruct((n // tm, 1, LANES), F32)),
        grid=(n // tm,),
        in_specs=[pl.BlockSpec((tm, D_DN), lambda i: (i, 0)), pl.BlockSpec((tm, D_MOBA), lambda i: (i, 0)),
                  pl.BlockSpec((tm, d), lambda i: (i, 0)), pl.BlockSpec((D_DN + D_MOBA, d), lambda i: (0, 0)),
                  row(d), row(d), pl.BlockSpec((d, LANES), lambda i: (0, 0)), row(LANES)],
        out_specs=(pl.BlockSpec((tm, d), lambda i: (i, 0)), pl.BlockSpec((tm, d), lambda i: (i, 0)),
                   pl.BlockSpec((tm, LANES), lambda i: (i, 0)), pl.BlockSpec((1, 1, LANES), lambda i: (i, 0, 0))),
        compiler_params=pltpu.CompilerParams(dimension_semantics=("parallel",), vmem_limit_bytes=VMEM_LIMIT),
        name="mix_route",
    )(y_dn, y_mb, x2, wo, g1, b1, rw, rb)


def _bucket_offsets_col(ohf):
    cnt = jnp.sum(ohf, axis=1, keepdims=True).astype(jnp.int32)
    pad = (((cnt + (GRAN - 1)) >> GRAN_SHIFT) << GRAN_SHIFT).astype(F32)
    r = lax.broadcasted_iota(jnp.int32, (LANES, LANES), 0)
    c = lax.broadcasted_iota(jnp.int32, (LANES, LANES), 1)
    before = jnp.where(c < r, 1.0, 0.0)
    return _dot(before, jnp.broadcast_to(pad, (LANES, LANES)))[:, 0:1]


def _moe_sort_kernel(gmap_ref, nvalid_ref, tail0_ref, taillen_ref, hb_ref, route_ref, lstrict_ref,
                     xg_ref, wsg_ref, xs_ref, ws_ref, zx_ref, zw_ref, sem):
    s = pl.program_id(0)
    nsteps = pl.num_programs(0)
    slot = s & 1
    ts = route_ref.shape[0]
    route = route_ref[...]
    rt = jnp.transpose(route)
    bucket_row = rt[ROUTE_BUCKET:ROUTE_BUCKET + 1, :].astype(jnp.int32)
    sub = lax.broadcasted_iota(jnp.int32, (LANES, ts), 0)
    ohf = jnp.where(sub == bucket_row, 1.0, 0.0)
    loff = _bucket_offsets_col(ohf)
    rank = lax.dot_general(ohf.astype(BF16), lstrict_ref[...], (((1,), (1,)), ((), ())),
                           preferred_element_type=F32)
    dest = jnp.sum(ohf * (loff + rank), axis=0, keepdims=True).astype(jnp.int32)
    rowi = lax.broadcasted_iota(jnp.int32, (LROWS, ts), 0)
    perm = jnp.where(rowi == dest, 1.0, 0.0).astype(BF16)
    xs_ref[slot] = jnp.dot(perm, hb_ref[...], preferred_element_type=F32).astype(BF16)
    rh, rl = _split2(route)
    wparts = jnp.dot(perm, jnp.concatenate([rh, rl], axis=1), preferred_element_type=F32)
    ws_ref[slot] = wparts[:, 0:LANES] + wparts[:, LANES:2 * LANES]

    def copies(step, g):
        sl = step & 1
        src = pl.ds(pl.multiple_of(g * GRAN, GRAN), GRAN)
        dst = pl.ds(pl.multiple_of(gmap_ref[step * LGRAN + g] * GRAN, GRAN), GRAN)
        return (pltpu.make_async_copy(xs_ref.at[sl, src, :], xg_ref.at[dst, :], sem.at[0, sl]),
                pltpu.make_async_copy(ws_ref.at[sl, src, :], wsg_ref.at[dst, :], sem.at[1, sl]))

    def fill_copies(b, i):
        dst = pl.ds(pl.multiple_of((tail0_ref[b] + i) * GRAN, GRAN), GRAN)
        return (pltpu.make_async_copy(zx_ref.at[0:GRAN, :], xg_ref.at[dst, :], sem.at[2, 0]),
                pltpu.make_async_copy(zw_ref.at[0:GRAN, :], wsg_ref.at[dst, :], sem.at[2, 1]))

    def unused_tile_copies(t):
        dst = pl.ds(pl.multiple_of(t * MOE_TM, MOE_TM), MOE_TM)
        return (pltpu.make_async_copy(zx_ref, xg_ref.at[dst, :], sem.at[2, 0]),
                pltpu.make_async_copy(zw_ref, wsg_ref.at[dst, :], sem.at[2, 1]))

    def run(step, fn):
        def body(g, carry):
            for cp in copies(step, g):
                fn(cp)
            return carry
        lax.fori_loop(0, nvalid_ref[step], body, 0)

    def run_fill(fn):
        for b in range(N_BUCKETS):
            def body(i, carry, b=b):
                for cp in fill_copies(b, i):
                    fn(cp)
                return carry
            lax.fori_loop(0, taillen_ref[b], body, 0)

        def tile_body(t, carry):
            for cp in unused_tile_copies(t):
                fn(cp)
            return carry
        lax.fori_loop(tail0_ref[N_BUCKETS], xg_ref.shape[0] // MOE_TM, tile_body, 0)

    @pl.when(s == 0)
    def _():
        zx_ref[...] = jnp.zeros_like(zx_ref)
        zw_ref[...] = jnp.zeros_like(zw_ref)
        run_fill(lambda cp: cp.start())

    run(s, lambda cp: cp.start())

    @pl.when(s > 0)
    def _():
        run(s - 1, lambda cp: cp.wait())

    @pl.when(s == nsteps - 1)
    def _():
        run(s, lambda cp: cp.wait())
        run_fill(lambda cp: cp.wait())


def _moe_sort(plan, hb, route, lstrict):
    n, d = hb.shape
    ts = MOE_TS
    rows = plan["n_tiles"] * MOE_TM
    return pl.pallas_call(
        _moe_sort_kernel,
        out_shape=(jax.ShapeDtypeStruct((rows, d), BF16), jax.ShapeDtypeStruct((rows, LANES), F32)),
        grid_spec=pltpu.PrefetchScalarGridSpec(
            num_scalar_prefetch=4,
            grid=(n // ts,),
            in_specs=[pl.BlockSpec((ts, d), lambda s, *_: (s, 0)), pl.BlockSpec((ts, LANES), lambda s, *_: (s, 0)),
                      pl.BlockSpec((ts, ts), lambda s, *_: (0, 0))],
            out_specs=(pl.BlockSpec(memory_space=pl.ANY), pl.BlockSpec(memory_space=pl.ANY)),
            scratch_shapes=[pltpu.VMEM((2, LROWS, d), BF16), pltpu.VMEM((2, LROWS, LANES), F32),
                            pltpu.VMEM((MOE_TM, d), BF16), pltpu.VMEM((MOE_TM, LANES), F32),
                            pltpu.SemaphoreType.DMA((3, 2))]),
        compiler_params=pltpu.CompilerParams(dimension_semantics=("arbitrary",), vmem_limit_bytes=VMEM_LIMIT),
        name="moe_sort",
    )(plan["gmap"], plan["nvalid"], plan["tail0"], plan["taillen"], hb, route, lstrict)


def _moe_expert_kernel(xt_ref, elo_ref, ehi_ref, valid_ref, x_ref, w_ref, wg0, wu0, wd0, wg1, wu1, wd1, o_ref):
    t = pl.program_id(0)

    @pl.when(valid_ref[t] > 0)
    def _():
        x = x_ref[...]
        w = w_ref[...]
        gates = [jnp.dot(x, wg[0], preferred_element_type=F32) for wg in (wg0, wg1)]
        ups = [jnp.dot(x, wu[0], preferred_element_type=F32) for wu in (wu0, wu1)]
        hes = [(_silu(gates[i]) * ups[i] * w[:, lane:lane + 1]).astype(BF16)
               for i, lane in enumerate((ROUTE_W_LO, ROUTE_W_HI))]
        o_ref[...] = (jnp.dot(hes[0], wd0[0], preferred_element_type=F32)
                      + jnp.dot(hes[1], wd1[0], preferred_element_type=F32)).astype(o_ref.dtype)

    @pl.when(valid_ref[t] == 0)
    def _():
        o_ref[...] = jnp.zeros_like(o_ref)


def _moe_experts(plan, xg, wsg, wg, wu, wd):
    rows, d = xg.shape
    tm = MOE_TM
    tok = lambda width: pl.BlockSpec((tm, width), lambda t, xt, elo, ehi, valid: (xt[t], 0))
    lo3 = lambda shape: pl.BlockSpec(shape, lambda t, xt, elo, ehi, valid: (elo[t], 0, 0))
    hi3 = lambda shape: pl.BlockSpec(shape, lambda t, xt, elo, ehi, valid: (ehi[t], 0, 0))
    return pl.pallas_call(
        _moe_expert_kernel,
        out_shape=jax.ShapeDtypeStruct((rows, d), BF16),
        grid_spec=pltpu.PrefetchScalarGridSpec(
            num_scalar_prefetch=4,
            grid=(rows // tm,),
            in_specs=[tok(d), tok(LANES),
                      lo3((1, d, D_EXPERT)), lo3((1, d, D_EXPERT)), lo3((1, D_EXPERT, d)),
                      hi3((1, d, D_EXPERT)), hi3((1, d, D_EXPERT)), hi3((1, D_EXPERT, d))],
            out_specs=pl.BlockSpec((tm, d), lambda t, *_: (t, 0))),
        compiler_params=pltpu.CompilerParams(dimension_semantics=("arbitrary",), vmem_limit_bytes=VMEM_LIMIT),
        name="moe_experts",
    )(plan["xtile"], plan["elo"], plan["ehi"], plan["valid"], xg, wsg, wg, wu, wd, wg, wu, wd)


def _moe_unsort_kernel(gmap_ref, og_ref, route_ref, h_ref, lstrict_ref, g_ref, b_ref, out_ref, ol_ref, sem):
    s = pl.program_id(0)
    nsteps = pl.num_programs(0)
    slot = s & 1
    ts = route_ref.shape[0]

    def gather(step, fn):
        sl = step & 1

        def body(g, carry):
            src = pl.ds(pl.multiple_of(gmap_ref[step * LGRAN + g] * GRAN, GRAN), GRAN)
            dst = pl.ds(pl.multiple_of(g * GRAN, GRAN), GRAN)
            fn(pltpu.make_async_copy(og_ref.at[src, :], ol_ref.at[sl, dst, :], sem.at[sl]))
            return carry
        lax.fori_loop(0, LGRAN, body, 0)

    @pl.when(s == 0)
    def _():
        gather(s, lambda cp: cp.start())

    @pl.when(s + 1 < nsteps)
    def _():
        gather(s + 1, lambda cp: cp.start())

    route = route_ref[...]
    bucket_col = route[:, ROUTE_BUCKET:ROUTE_BUCKET + 1].astype(jnp.int32)
    lane = lax.broadcasted_iota(jnp.int32, (ts, LANES), 1)
    ohf = jnp.where(lane == bucket_col, 1.0, 0.0)
    cnt = jnp.sum(ohf, axis=0, keepdims=True).astype(jnp.int32)
    pad = (((cnt + (GRAN - 1)) >> GRAN_SHIFT) << GRAN_SHIFT).astype(F32)
    r = lax.broadcasted_iota(jnp.int32, (LANES, LANES), 0)
    c = lax.broadcasted_iota(jnp.int32, (LANES, LANES), 1)
    loff = _dot(jnp.broadcast_to(pad, (8, LANES)), jnp.where(r < c, 1.0, 0.0))[0:1, :]
    rank = jnp.dot(lstrict_ref[...], ohf.astype(BF16), preferred_element_type=F32)
    dest = jnp.sum(ohf * (loff + rank), axis=1, keepdims=True).astype(jnp.int32)
    lrow = lax.broadcasted_iota(jnp.int32, (ts, LROWS), 1)
    perm_t = jnp.where(lrow == dest, 1.0, 0.0).astype(BF16)

    gather(s, lambda cp: cp.wait())
    ffn = jnp.dot(perm_t, ol_ref[slot], preferred_element_type=F32)
    out_ref[...] = _layer_norm(DEEPNORM_ALPHA * h_ref[...] + ffn, g_ref[...], b_ref[...])


def _moe_unsort(plan, og, route, hf, lstrict, g2, b2):
    n, d = hf.shape
    ts = MOE_TS
    row = pl.BlockSpec((1, d), lambda s, *_: (0, 0))
    return pl.pallas_call(
        _moe_unsort_kernel,
        out_shape=jax.ShapeDtypeStruct((n, d), F32),
        grid_spec=pltpu.PrefetchScalarGridSpec(
            num_scalar_prefetch=1,
            grid=(n // ts,),
            in_specs=[pl.BlockSpec(memory_space=pl.ANY), pl.BlockSpec((ts, LANES), lambda s, *_: (s, 0)),
                      pl.BlockSpec((ts, d), lambda s, *_: (s, 0)), pl.BlockSpec((ts, ts), lambda s, *_: (0, 0)),
                      row, row],
            out_specs=pl.BlockSpec((ts, d), lambda s, *_: (s, 0)),
            scratch_shapes=[pltpu.VMEM((2, LROWS, d), BF16), pltpu.SemaphoreType.DMA((2,))]),
        compiler_params=pltpu.CompilerParams(dimension_semantics=("arbitrary",), vmem_limit_bytes=VMEM_LIMIT),
        name="moe_unsort",
    )(plan["gmap_back"], og, route, hf, lstrict, g2, b2)


def _moe_plan(cnt_half, n):
    nsrc = n // MOE_TS
    i32 = jnp.int32
    cnt = cnt_half.reshape(nsrc, -1, LANES).sum(axis=1)[:, :N_BUCKETS].astype(i32)
    run_g = (cnt + GRAN - 1) // GRAN
    nvalid = run_g.sum(axis=1)
    loff_g = jnp.cumsum(run_g, axis=1) - run_g
    bucket_g = run_g.sum(axis=0)
    gpt = MOE_TM // GRAN
    btiles = (bucket_g + gpt - 1) // gpt
    tend = jnp.cumsum(btiles)
    tstart = tend - btiles
    gofs = tstart[None, :] * gpt + jnp.cumsum(run_g, axis=0) - run_g
    n_tiles = -(-(n + nsrc * N_BUCKETS * (GRAN - 1)) // MOE_TM) + N_BUCKETS + 1
    g = jnp.arange(LGRAN, dtype=i32)[None, :, None]
    in_run = (g >= loff_g[:, None, :]) & (g < (loff_g + run_g)[:, None, :])
    gmap = jnp.arange(LGRAN, dtype=i32)[None, :] + jnp.sum(jnp.where(in_run, (gofs - loff_g)[:, None, :], 0), axis=2)
    is_valid = jnp.arange(LGRAN, dtype=i32)[None, :] < nvalid[:, None]
    zero_gran = (n_tiles - 1) * gpt
    t = jnp.arange(n_tiles, dtype=i32)
    tb = jnp.minimum(jnp.sum(t[:, None] >= tend[None, :], axis=1), N_BUCKETS - 1)
    valid = (t < tend[-1]).astype(i32)
    pairs = [(a, b) for a in range(EXPERTS_PER_GROUP) for b in range(a + 1, EXPERTS_PER_GROUP)]
    pidx = tb % N_PAIRS
    pair_a = sum(jnp.where(pidx == i, a, 0) for i, (a, _) in enumerate(pairs))
    pair_b = sum(jnp.where(pidx == i, b, 0) for i, (_, b) in enumerate(pairs))
    grp = tb // N_PAIRS
    return {
        "n_tiles": n_tiles,
        "gmap": jnp.where(is_valid, gmap, 0).reshape(-1).astype(i32),
        "gmap_back": jnp.where(is_valid, gmap, zero_gran).reshape(-1).astype(i32),
        "nvalid": nvalid.astype(i32),
        "tail0": jnp.concatenate([tstart * gpt + bucket_g, tend[-1:]]).astype(i32),
        "taillen": (btiles * gpt - bucket_g).astype(i32),
        "xtile": jnp.where(valid > 0, t, 0).astype(i32),
        "elo": (grp * EXPERTS_PER_GROUP + pair_a).astype(i32),
        "ehi": (grp * EXPERTS_PER_GROUP + pair_b).astype(i32),
        "valid": valid,
    }


def _pad_lanes(a, lane0=0):
    return jnp.zeros((1, LANES), F32).at[0, lane0:lane0 + a.shape[0]].set(a.astype(F32))


def _rope_tables(seq):
    half = HEAD_DIM // 2
    inv_freq = ROPE_THETA ** (-jnp.arange(half, dtype=F32) / half)
    ang = jnp.arange(seq).astype(F32)[:, None] * inv_freq[None, :]
    cos, sin = jnp.cos(ang), jnp.sin(ang)
    return jnp.concatenate([cos, cos], axis=-1), jnp.concatenate([-sin, sin], axis=-1)


def _layer(x, w_in, conv_w, a_log, dt_bias, dn_norm_w, w_out, ln1_g, ln1_b, router_w1, router_b1,
           router_w2, router_b2, w_gate, w_up, w_down, ln2_g, ln2_b):
    bsz, seq, d = x.shape
    n = bsz * seq
    x2 = x.reshape(n, d)

    o_z, o_b, o_mb = 3 * D_DN, 4 * D_DN, 4 * D_DN + 2 * N_HEADS_DN
    w_ba = jnp.pad(w_in[:, o_b:o_mb], ((0, 0), (0, LANES - 2 * N_HEADS_DN)))
    w_all = jnp.concatenate([w_in[:, :o_z], w_in[:, o_z:o_b], w_ba, w_in[:, o_mb:]], axis=1).astype(BF16)

    tm = min(512, n)
    dn_qkv, z, ba, mb_qkv = _in_proj(x2, w_all, conv_w, tm, seq)

    y_dn = _deltanet(dn_qkv.reshape(bsz, seq, 3 * D_DN), z.reshape(bsz, seq, D_DN), ba.reshape(bsz, seq, LANES),
                     _pad_lanes(a_log, N_HEADS_DN), _pad_lanes(dt_bias, N_HEADS_DN),
                     dn_norm_w.astype(F32).reshape(1, HEAD_DIM))

    cos_t, sin_t = _rope_tables(seq)
    q_r, k_r, vt, sel = _moba_prep(mb_qkv.reshape(bsz, seq, 3 * D_MOBA), cos_t, sin_t)
    y_mb = _moba_attn(q_r, k_r, vt, sel)

    rw = jnp.concatenate([router_w1, jnp.transpose(router_w2, (1, 0, 2)).reshape(d, N_EXPERTS)], axis=1)
    rw = jnp.pad(rw, ((0, 0), (0, LANES - rw.shape[1])))
    rb = _pad_lanes(jnp.concatenate([router_b1, router_b2.reshape(-1)]))
    hf, hb, route, cnt = _mix_route(y_dn.reshape(n, D_DN), y_mb.reshape(n, D_MOBA), x2, w_out.astype(BF16),
                                    ln1_g.reshape(1, d), ln1_b.reshape(1, d), rw, rb, tm)

    plan = _moe_plan(cnt, n)
    idx = jnp.arange(MOE_TS, dtype=jnp.int32)
    lstrict = (idx[None, :] < idx[:, None]).astype(BF16)
    xg, wsg = _moe_sort(plan, hb, route, lstrict)
    og = _moe_experts(plan, xg, wsg, w_gate.astype(BF16), w_up.astype(BF16), w_down.astype(BF16))
    out = _moe_unsort(plan, og, route, hf, lstrict, ln2_g.reshape(1, d), ln2_b.reshape(1, d))
    return out.reshape(bsz, seq, d)


def kernel(x, w_in, conv_w, a_log, dt_bias, dn_norm_w, w_out, ln1_g, ln1_b, router_w1, router_b1, router_w2, router_b2, expert_w_gate, expert_w_up, expert_w_down, ln2_g, ln2_b):
    for l in range(DEPTH):
        x = _layer(x, w_in[l], conv_w[l], a_log[l], dt_bias[l], dn_norm_w[l], w_out[l], ln1_g[l], ln1_b[l],
                   router_w1[l], router_b1[l], router_w2[l], router_b2[l], expert_w_gate[l], expert_w_up[l],
                   expert_w_down[l], ln2_g[l], ln2_b[l])
    return x
```

```python
import functools

import jax
import jax.numpy as jnp
from jax import lax
from jax.experimental import pallas as pl
from jax.experimental.pallas import tpu as pltpu

F32 = jnp.float32
BF16 = jnp.bfloat16

HEAD_DIM = 128
N_HEADS_DN = 4
N_HEADS_MOBA = 4
D_DN = N_HEADS_DN * HEAD_DIM
D_MOBA = N_HEADS_MOBA * HEAD_DIM
CONV_K = 4
DN_CHUNK = 64
MOBA_BLOCK = 256
MOBA_TOPK = 3
ROPE_THETA = 10000.0
N_GROUPS = 4
EXPERTS_PER_GROUP = 4
N_EXPERTS = N_GROUPS * EXPERTS_PER_GROUP
D_EXPERT = 256
LN_EPS = 1e-5
RMS_EPS = 1e-6
L2_EPS = 1e-6
NEG_INF = -1e30
LOG2E = 1.4426950408889634
DEPTH = 1
DEEPNORM_ALPHA = (2 * DEPTH) ** 0.25

LANES = 128
DN_TILE = 256
DN_HEADS_PER_STEP = 4
GATE_LANE0 = N_GROUPS
N_PAIRS = EXPERTS_PER_GROUP * (EXPERTS_PER_GROUP - 1) // 2
N_BUCKETS = N_GROUPS * N_PAIRS
ROUTE_BUCKET, ROUTE_W_LO, ROUTE_W_HI = 0, 1, 2
MOE_TS = 512
MOE_TM = 512
GRAN = 16
GRAN_SHIFT = 4
LROWS = -(-(MOE_TS + N_BUCKETS * (GRAN - 1)) // LANES) * LANES
LGRAN = LROWS // GRAN
VMEM_LIMIT = 48 * 1024 * 1024


def _dot(a, b):
    return jnp.dot(a.astype(BF16), b.astype(BF16), preferred_element_type=F32)


def _dot_nt(a, b):
    return lax.dot_general(a.astype(BF16), b.astype(BF16), (((1,), (1,)), ((), ())),
                           preferred_element_type=F32)


def _split2(a):
    hi = a.astype(BF16)
    lo = (a - hi.astype(F32)).astype(BF16)
    return hi, lo


def _split3(a):
    hi = a.astype(BF16)
    r = a - hi.astype(F32)
    mid = r.astype(BF16)
    lo = (r - mid.astype(F32)).astype(BF16)
    return hi, mid, lo


def _dot3(a, b):
    ah, al = _split2(a)
    bh, bl = _split2(b)
    return (jnp.dot(ah, bh, preferred_element_type=F32) + jnp.dot(ah, bl, preferred_element_type=F32)
            + jnp.dot(al, bh, preferred_element_type=F32))


def _dot3_nt(a, b):
    ah, al = _split2(a)
    bh, bl = _split2(b)
    dn = (((1,), (1,)), ((), ()))
    return (lax.dot_general(ah, bh, dn, preferred_element_type=F32)
            + lax.dot_general(ah, bl, dn, preferred_element_type=F32)
            + lax.dot_general(al, bh, dn, preferred_element_type=F32))


def _dot_exact_lhs(a_bf16, b):
    bh, bm, bl = _split3(b)
    return (jnp.dot(a_bf16, bh, preferred_element_type=F32) + jnp.dot(a_bf16, bm, preferred_element_type=F32)
            + jnp.dot(a_bf16, bl, preferred_element_type=F32))


def _silu(x):
    return x * jax.nn.sigmoid(x)


def _softplus(x):
    return jnp.maximum(x, 0.0) + jnp.log1p(jnp.exp(-jnp.abs(x)))


def _layer_norm(t, g, b):
    mu = jnp.mean(t, axis=-1, keepdims=True)
    d = t - mu
    var = jnp.mean(d * d, axis=-1, keepdims=True)
    return d * lax.rsqrt(var + LN_EPS) * g + b


def _lane_pick(x, lane):
    ids = lax.broadcasted_iota(jnp.int32, x.shape, 1)
    return jnp.sum(jnp.where(ids == lane, x, 0.0), axis=1, keepdims=True)


def _in_proj_kernel(x_ref, w_ref, cw_ref, dn_ref, z_ref, ba_ref, mb_ref, cb_ref, *, tiles_per_seq):
    i = pl.program_id(0)
    tm = x_ref.shape[0]
    o0 = 3 * D_DN
    o1 = o0 + D_DN
    o2 = o1 + LANES

    @pl.when(i % tiles_per_seq == 0)
    def _():
        cb_ref[0:8, :] = jnp.zeros((8, o0), F32)

    xb = x_ref[...].astype(BF16)
    u = jnp.dot(xb, w_ref[:, 0:o0], preferred_element_type=F32)
    z_ref[...] = _silu(jnp.dot(xb, w_ref[:, o0:o1], preferred_element_type=F32))
    ba_ref[...] = jnp.dot(xb, w_ref[:, o1:o2], preferred_element_type=F32)
    mb_ref[...] = jnp.dot(xb, w_ref[:, o2:o2 + 3 * D_MOBA], preferred_element_type=F32)

    cb_ref[8:8 + tm, :] = u
    acc = cw_ref[CONV_K - 1:CONV_K, :] * u
    for s in range(1, CONV_K):
        acc = acc + cw_ref[CONV_K - 1 - s:CONV_K - s, :] * cb_ref[8 - s:8 - s + tm, :]
    cb_ref[0:8, :] = u[tm - 8:tm, :]
    qkv = _silu(acc)

    outs = []
    for h in range(2 * N_HEADS_DN):
        t = qkv[:, h * HEAD_DIM:(h + 1) * HEAD_DIM]
        t = t * lax.rsqrt(jnp.sum(t * t, axis=-1, keepdims=True) + L2_EPS)
        outs.append(t * (HEAD_DIM ** -0.5) if h < N_HEADS_DN else t)
    outs.append(qkv[:, 2 * D_DN:3 * D_DN])
    dn_ref[...] = jnp.concatenate(outs, axis=1)


def _in_proj(x2, w_all, conv_w, tm, seq):
    n, d = x2.shape
    wc = w_all.shape[1]
    return pl.pallas_call(
        functools.partial(_in_proj_kernel, tiles_per_seq=seq // tm),
        out_shape=(jax.ShapeDtypeStruct((n, 3 * D_DN), F32), jax.ShapeDtypeStruct((n, D_DN), F32),
                   jax.ShapeDtypeStruct((n, LANES), F32), jax.ShapeDtypeStruct((n, 3 * D_MOBA), F32)),
        grid=(n // tm,),
        in_specs=[pl.BlockSpec((tm, d), lambda i: (i, 0)), pl.BlockSpec((d, wc), lambda i: (0, 0)),
                  pl.BlockSpec((CONV_K, 3 * D_DN), lambda i: (0, 0))],
        out_specs=(pl.BlockSpec((tm, 3 * D_DN), lambda i: (i, 0)), pl.BlockSpec((tm, D_DN), lambda i: (i, 0)),
                   pl.BlockSpec((tm, LANES), lambda i: (i, 0)), pl.BlockSpec((tm, 3 * D_MOBA), lambda i: (i, 0))),
        scratch_shapes=[pltpu.VMEM((8 + tm, 3 * D_DN), F32)],
        compiler_params=pltpu.CompilerParams(dimension_semantics=("arbitrary",), vmem_limit_bytes=VMEM_LIMIT),
        name="in_proj",
    )(x2, w_all, conv_w)


def _inv_unit_lower(a_list, row, col):
    eye = (row == col).astype(F32)
    d8 = (row >> 3) == (col >> 3)
    a8 = [jnp.where(d8, a, 0.0) for a in a_list]
    a8_2 = [_dot(a, a) for a in a8]
    a8_4 = [_dot(a, a) for a in a8_2]
    x = [_dot(eye - a, eye + a2) for a, a2 in zip(a8, a8_2)]
    x = [_dot(xi, eye + a4) for xi, a4 in zip(x, a8_4)]
    s = 8
    while s < DN_CHUNK:
        sh = s.bit_length() - 1
        off = ((row >> (sh + 1)) == (col >> (sh + 1))) & ((row >> sh) != (col >> sh))
        y = [_dot(jnp.where(off, a, 0.0), xi) for a, xi in zip(a_list, x)]
        x = [xi - _dot(xi, yi) for xi, yi in zip(x, y)]
        s *= 2
    return x


def _deltanet_heads(q, k, v, z, beta, gcc, gcr, state, normw, masks):
    row, col, incl, strict = masks
    nh = len(q)
    hs = range(nh)
    tt = q[0].shape[0]
    nchunk = tt // DN_CHUNK
    decay = [jnp.where(incl, jnp.exp(jnp.where(incl, gcc[h] - gcr[h], 0.0)), 0.0) for h in hs]
    kb = [k[h] * beta[h] for h in hs]
    vb = [v[h] * beta[h] for h in hs]
    a_mat = [jnp.where(strict, _dot_nt(kb[h], k[h]) * decay[h], 0.0) for h in hs]
    qk = [_dot_nt(q[h], k[h]) * decay[h] for h in hs]
    tinv = _inv_unit_lower(a_mat, row, col)

    eg = [jnp.exp(g) for g in gcc]
    wu = [_dot(tinv[h], jnp.concatenate([kb[h] * eg[h], vb[h]], axis=1)) for h in hs]
    qd = [q[h] * eg[h] for h in hs]

    gl_rows = [[g[(c + 1) * DN_CHUNK - 1:(c + 1) * DN_CHUNK, :] for c in range(nchunk)] for g in gcc]
    gl_col = [jnp.concatenate([jnp.broadcast_to(g, (DN_CHUNK, 1)) for g in rows], axis=0) for rows in gl_rows]
    kdt = [jnp.transpose(k[h] * jnp.exp(gl_col[h] - gcc[h])) for h in hs]

    outs = [[] for _ in hs]
    for c in range(nchunk):
        lo, hi = c * DN_CHUNK, (c + 1) * DN_CHUNK
        r = [_dot(jnp.concatenate([wu[h][lo:hi, 0:HEAD_DIM], qd[h][lo:hi, :]], axis=0), state[h]) for h in hs]
        vz = []
        for h in hs:
            parts = []
            if lo > 0:
                parts.append(jnp.zeros((lo, HEAD_DIM), F32))
            parts.append(wu[h][lo:hi, HEAD_DIM:2 * HEAD_DIM] - r[h][0:DN_CHUNK, :])
            if hi < tt:
                parts.append(jnp.zeros((tt - hi, HEAD_DIM), F32))
            vz.append(jnp.concatenate(parts, axis=0).astype(BF16))
        for h in hs:
            outs[h].append(r[h][DN_CHUNK:2 * DN_CHUNK, :] + _dot(qk[h][lo:hi, :], vz[h]))
        state = [state[h] * jnp.exp(gl_rows[h][c]) + _dot(kdt[h], vz[h]) for h in hs]

    ys = []
    for h in hs:
        o = jnp.concatenate(outs[h], axis=0)
        o = o * lax.rsqrt(jnp.mean(o * o, axis=-1, keepdims=True) + RMS_EPS) * normw
        ys.append(o * z[h])
    return ys, state


def _deltanet_kernel(q_ref, k_ref, v_ref, z_ref, ba_ref, alog_ref, dtb_ref, normw_ref, y_ref, s_ref, *, hb):
    hg = pl.program_id(1)
    t = pl.program_id(2)
    tt = DN_TILE

    @pl.when(t == 0)
    def _():
        s_ref[...] = jnp.zeros_like(s_ref)

    q_all = q_ref[0]
    k_all = k_ref[0]
    v_all = v_ref[0]

    ba = ba_ref[0]
    beta_all = jax.nn.sigmoid(ba)
    g_all = -jnp.exp(alog_ref[...]) * _softplus(ba + dtb_ref[...])

    row = lax.broadcasted_iota(jnp.int32, (tt, tt), 0)
    col = lax.broadcasted_iota(jnp.int32, (tt, tt), 1)
    same = (row >> 6) == (col >> 6)
    incl = same & (row >= col)
    strict = same & (row > col)
    masks = (row, col, incl, strict)

    gc_all = _dot_exact_lhs(incl.astype(BF16), g_all)
    gct = jnp.transpose(gc_all)
    sub = lax.broadcasted_iota(jnp.int32, gct.shape, 0)

    sls = [slice(hh * HEAD_DIM, (hh + 1) * HEAD_DIM) for hh in range(hb)]
    heads = [hg * hb + hh for hh in range(hb)]
    beta = [_lane_pick(beta_all, h) for h in heads]
    gcc = [_lane_pick(gc_all, h + N_HEADS_DN) for h in heads]
    gcr = [jnp.sum(jnp.where(sub == h + N_HEADS_DN, gct, 0.0), axis=0, keepdims=True) for h in heads]
    ys, states = _deltanet_heads([q_all[:, sl] for sl in sls], [k_all[:, sl] for sl in sls],
                                 [v_all[:, sl] for sl in sls], [z_ref[0, :, sl] for sl in sls], beta, gcc, gcr,
                                 [s_ref[hh] for hh in range(hb)], normw_ref[...], masks)
    s_ref[...] = jnp.stack(states, axis=0)
    y_ref[0] = jnp.concatenate(ys, axis=1).astype(y_ref.dtype)


def _deltanet(dn_qkv, z, ba, alog_row, dtb_row, normw_row):
    bsz, seq, _ = dn_qkv.shape
    tt = DN_TILE
    hb = DN_HEADS_PER_STEP
    ng = N_HEADS_DN // hb
    w = hb * HEAD_DIM

    def col_spec(off):
        return pl.BlockSpec((1, tt, w), lambda b, g, t: (b, t, g + off))

    row_spec = pl.BlockSpec((1, LANES), lambda b, g, t: (0, 0))
    return pl.pallas_call(
        functools.partial(_deltanet_kernel, hb=hb),
        out_shape=jax.ShapeDtypeStruct((bsz, seq, D_DN), BF16),
        grid=(bsz, ng, seq // tt),
        in_specs=[col_spec(0), col_spec(ng), col_spec(2 * ng), col_spec(0),
                  pl.BlockSpec((1, tt, LANES), lambda b, g, t: (b, t, 0)), row_spec, row_spec, row_spec],
        out_specs=col_spec(0),
        scratch_shapes=[pltpu.VMEM((hb, HEAD_DIM, HEAD_DIM), F32)],
        compiler_params=pltpu.CompilerParams(dimension_semantics=("parallel", "parallel", "arbitrary"),
                                             vmem_limit_bytes=VMEM_LIMIT),
        name="deltanet",
    )(dn_qkv, dn_qkv, dn_qkv, z, ba, alog_row, dtb_row, normw_row)


def _moba_prep_kernel(x_ref, cos_ref, sin_ref, q_ref, k_ref, vt_ref, sel_ref, km_ref, *, nb, topk):
    j = pl.program_id(1)

    @pl.when(j == 0)
    def _():
        km_ref[...] = jnp.zeros_like(km_ref)

    cos = cos_ref[...]
    sin = sin_ref[...]
    half = HEAD_DIM // 2
    blk = lax.broadcasted_iota(jnp.int32, (nb, MOBA_BLOCK), 0)
    kmeans = []
    for h in range(N_HEADS_MOBA):
        qh = x_ref[0, :, h * HEAD_DIM:(h + 1) * HEAD_DIM]
        kh = x_ref[0, :, D_MOBA + h * HEAD_DIM:D_MOBA + (h + 1) * HEAD_DIM]
        qr = (qh * cos + pltpu.roll(qh, half, 1) * sin) * (HEAD_DIM ** -0.5)
        kr = kh * cos + pltpu.roll(kh, half, 1) * sin
        q_ref[0, :, h * HEAD_DIM:(h + 1) * HEAD_DIM] = (qr * LOG2E).astype(q_ref.dtype)
        k_ref[0, :, h * HEAD_DIM:(h + 1) * HEAD_DIM] = kr.astype(k_ref.dtype)
        kmeans.append(jnp.mean(kr, axis=0, keepdims=True))

        gate = _dot3_nt(km_ref[:, h * HEAD_DIM:(h + 1) * HEAD_DIM], qr)
        gate = jnp.where(blk < j, gate, NEG_INF)
        rank = jnp.zeros(gate.shape, F32)
        for m in range(nb):
            gm = gate[m:m + 1, :]
            ahead = (gm > gate) | ((gm == gate) & (blk > m))
            rank = rank + jnp.where(ahead, 1.0, 0.0)
        sel = (blk < j) & (rank < topk)
        sel_ref[0, 0, h * nb:(h + 1) * nb, :] = jnp.where(sel, 1.0, 0.0)

    km_ref[pl.ds(j, 1), :] = jnp.concatenate(kmeans, axis=1)
    vt_ref[0, 0] = jnp.transpose(x_ref[0, :, 2 * D_MOBA:3 * D_MOBA]).astype(vt_ref.dtype)


def _moba_prep(mb_qkv, cos_t, sin_t):
    bsz, seq, _ = mb_qkv.shape
    nb = seq // MOBA_BLOCK
    topk = min(MOBA_TOPK, nb)
    kern = functools.partial(_moba_prep_kernel, nb=nb, topk=topk)
    tok_spec = pl.BlockSpec((1, MOBA_BLOCK, D_MOBA), lambda b, j: (b, j, 0))
    tab_spec = pl.BlockSpec((MOBA_BLOCK, HEAD_DIM), lambda b, j: (j, 0))
    return pl.pallas_call(
        kern,
        out_shape=(jax.ShapeDtypeStruct((bsz, seq, D_MOBA), BF16), jax.ShapeDtypeStruct((bsz, seq, D_MOBA), BF16),
                   jax.ShapeDtypeStruct((bsz, nb, D_MOBA, MOBA_BLOCK), BF16),
                   jax.ShapeDtypeStruct((bsz, nb, N_HEADS_MOBA * nb, MOBA_BLOCK), F32)),
        grid=(bsz, nb),
        in_specs=[pl.BlockSpec((1, MOBA_BLOCK, 3 * D_MOBA), lambda b, j: (b, j, 0)), tab_spec, tab_spec],
        out_specs=(tok_spec, tok_spec,
                   pl.BlockSpec((1, 1, D_MOBA, MOBA_BLOCK), lambda b, j: (b, j, 0, 0)),
                   pl.BlockSpec((1, 1, N_HEADS_MOBA * nb, MOBA_BLOCK), lambda b, j: (b, j, 0, 0))),
        scratch_shapes=[pltpu.VMEM((nb, D_MOBA), F32)],
        compiler_params=pltpu.CompilerParams(dimension_semantics=("parallel", "arbitrary"),
                                             vmem_limit_bytes=VMEM_LIMIT),
        name="moba_prep",
    )(mb_qkv, cos_t, sin_t)


def _moba_attn_kernel(q_ref, k_ref, vt_ref, sel_ref, o_ref, acc_ref, *, nb):
    j = pl.program_id(1)
    blk = MOBA_BLOCK
    cw = 2 * blk
    nh = N_HEADS_MOBA
    nc = (j + 1) // 2
    dn = (((1,), (1,)), ((), ()))
    hsl = [slice(h * HEAD_DIM, (h + 1) * HEAD_DIM) for h in range(nh)]
    qs = [q_ref[0, :, hsl[h]] for h in range(nh)]

    ki = lax.broadcasted_iota(jnp.int32, (blk, blk), 0)
    qi = lax.broadcasted_iota(jnp.int32, (blk, blk), 1)
    own = pl.ds(pl.multiple_of(j * blk, blk), blk)
    s_own = [jnp.where(ki <= qi, lax.dot_general(k_ref[0, own, hsl[h]], qs[h], dn, preferred_element_type=F32),
                       NEG_INF) for h in range(nh)]

    def scores(c, h):
        kc = k_ref[0, pl.ds(pl.multiple_of(c * cw, cw), cw), hsl[h]]
        s = lax.dot_general(kc, qs[h], dn, preferred_element_type=F32)
        parts = []
        for i in range(2):
            selrow = sel_ref[0, 0, pl.ds(h * nb + 2 * c + i, 1), :]
            parts.append(jnp.where(selrow > 0.5, s[i * blk:(i + 1) * blk, :], NEG_INF))
        return jnp.concatenate(parts, axis=0)

    ones8 = jnp.ones((8, cw), BF16)
    ms, ls = [], []
    for h in range(nh):
        m = jnp.max(s_own[h], axis=0, keepdims=True)
        pb = jnp.exp2(s_own[h] - m).astype(BF16)
        ms.append(m)
        ls.append(jnp.dot(ones8[:, 0:blk], pb, preferred_element_type=F32)[0:1, :])
        acc_ref[h] = jnp.dot(vt_ref[0, j, hsl[h], :], pb, preferred_element_type=F32)

    def softmax_pv(c, h, s, m, l):
        m_new = jnp.maximum(m, jnp.max(s, axis=0, keepdims=True))
        alpha = jnp.exp2(m - m_new)
        pb = jnp.exp2(s - m_new).astype(BF16)
        l = alpha * l + jnp.dot(ones8, pb, preferred_element_type=F32)[0:1, :]
        pv = (jnp.dot(vt_ref[0, 2 * c, hsl[h], :], pb[0:blk, :], preferred_element_type=F32)
              + jnp.dot(vt_ref[0, 2 * c + 1, hsl[h], :], pb[blk:cw, :], preferred_element_type=F32))
        acc_ref[h] = acc_ref[h] * alpha + pv
        return m_new, l

    def body(pair, carry):
        ms, ls = (list(t) for t in carry)
        c0 = 2 * pair
        s0 = [scores(c0, h) for h in range(nh)]
        s1 = []
        for h in range(nh):
            s1.append(scores(c0 + 1, h))
            ms[h], ls[h] = softmax_pv(c0, h, s0[h], ms[h], ls[h])
        for h in range(nh):
            ms[h], ls[h] = softmax_pv(c0 + 1, h, s1[h], ms[h], ls[h])
        return tuple(ms), tuple(ls)

    ms, ls = lax.fori_loop(0, (nc + 1) // 2, body, (tuple(ms), tuple(ls)))
    o_ref[0] = jnp.concatenate([jnp.transpose(acc_ref[h] / ls[h]) for h in range(nh)],
                               axis=1).astype(o_ref.dtype)


def _moba_attn(q_r, k_r, vt, sel):
    bsz, seq, _ = q_r.shape
    nb = seq // MOBA_BLOCK
    tok_spec = pl.BlockSpec((1, MOBA_BLOCK, D_MOBA), lambda b, j: (b, j, 0))
    return pl.pallas_call(
        functools.partial(_moba_attn_kernel, nb=nb),
        out_shape=jax.ShapeDtypeStruct((bsz, seq, D_MOBA), BF16),
        grid=(bsz, nb),
        in_specs=[tok_spec,
                  pl.BlockSpec((1, seq, D_MOBA), lambda b, j: (b, 0, 0)),
                  pl.BlockSpec((1, nb, D_MOBA, MOBA_BLOCK), lambda b, j: (b, 0, 0, 0)),
                  pl.BlockSpec((1, 1, N_HEADS_MOBA * nb, MOBA_BLOCK), lambda b, j: (b, j, 0, 0))],
        out_specs=tok_spec,
        scratch_shapes=[pltpu.VMEM((N_HEADS_MOBA, HEAD_DIM, MOBA_BLOCK), F32)],
        compiler_params=pltpu.CompilerParams(dimension_semantics=("parallel", "arbitrary"),
                                             vmem_limit_bytes=VMEM_LIMIT),
        name="moba_attn",
    )(q_r, k_r, vt, sel)


def _mix_route_kernel(ydn_ref, ymb_ref, x_ref, wo_ref, g_ref, b_ref, rw_ref, rb_ref, h_ref, hb_ref, route_ref,
                      cnt_ref):
    mix = (jnp.dot(ydn_ref[...], wo_ref[0:D_DN, :], preferred_element_type=F32)
           + jnp.dot(ymb_ref[...], wo_ref[D_DN:D_DN + D_MOBA, :], preferred_element_type=F32))
    hval = _layer_norm(DEEPNORM_ALPHA * x_ref[...] + mix, g_ref[...], b_ref[...])
    h_ref[...] = hval
    hb_ref[...] = hval.astype(BF16)

    hh, hl = _split2(hval)
    wh, wl = _split2(rw_ref[...])
    both = jnp.dot(hh, jnp.concatenate([wh, wl], axis=1), preferred_element_type=F32)
    logits = (both[:, 0:LANES] + both[:, LANES:2 * LANES] + jnp.dot(hl, wh, preferred_element_type=F32)
              + rb_ref[...])
    lane = lax.broadcasted_iota(jnp.int32, logits.shape, 1)
    big = jnp.int32(LANES)

    def first_lane(mask):
        return jnp.min(jnp.where(mask, lane, big), axis=1, keepdims=True)

    is_g = lane < N_GROUPS
    m1 = jnp.max(jnp.where(is_g, logits, NEG_INF), axis=1, keepdims=True)
    s1 = jnp.sum(jnp.where(is_g, jnp.exp(logits - m1), 0.0), axis=1, keepdims=True)
    pg = 1.0 / s1
    gsel = first_lane(is_g & (logits == m1))

    in_grp = (lane >= GATE_LANE0) & (((lane - GATE_LANE0) >> 2) == gsel) & (lane < GATE_LANE0 + N_EXPERTS)
    m2 = jnp.max(jnp.where(in_grp, logits, NEG_INF), axis=1, keepdims=True)
    s2 = jnp.sum(jnp.where(in_grp, jnp.exp(logits - m2), 0.0), axis=1, keepdims=True)
    e1 = first_lane(in_grp & (logits == m2))
    rest = in_grp & (lane != e1)
    m2b = jnp.max(jnp.where(rest, logits, NEG_INF), axis=1, keepdims=True)
    e2 = first_lane(rest & (logits == m2b))
    pe1 = 1.0 / s2
    pe2 = jnp.exp(m2b - m2) / s2
    tot = pe1 + pe2
    w1 = pg * (pe1 / tot)
    w2 = pg * (pe2 / tot)
    first_lo = e1 < e2
    lo = jnp.minimum(e1, e2)
    hi = jnp.maximum(e1, e2)
    a = (lo - GATE_LANE0) & (EXPERTS_PER_GROUP - 1)
    b = (hi - GATE_LANE0) & (EXPERTS_PER_GROUP - 1)
    bucket = gsel * N_PAIRS + ((a * (2 * EXPERTS_PER_GROUP - 1 - a)) >> 1) + (b - a - 1)
    route_ref[...] = jnp.where(lane == ROUTE_BUCKET, bucket.astype(F32),
                               jnp.where(lane == ROUTE_W_LO, jnp.where(first_lo, w1, w2),
                                         jnp.where(lane == ROUTE_W_HI, jnp.where(first_lo, w2, w1), 0.0)))
    cnt_ref[0] = jnp.sum(jnp.where(lane == bucket, 1.0, 0.0), axis=0, keepdims=True)


def _mix_route(y_dn, y_mb, x2, wo, g1, b1, rw, rb, tm):
    n, d = x2.shape
    row = lambda w: pl.BlockSpec((1, w), lambda i: (0, 0))
    return pl.pallas_call(
        _mix_route_kernel,
        out_shape=(jax.ShapeDtypeStruct((n, d), F32), jax.ShapeDtypeStruct((n, d), BF16),
                   jax.ShapeDtypeStruct((n, LANES), F32), jax.ShapeDtypeStruct((n // tm, 1, LANES), F32)),
        grid=(n // tm,),
        in_specs=[pl.BlockSpec((tm, D_DN), lambda i: (i, 0)), pl.BlockSpec((tm, D_MOBA), lambda i: (i, 0)),
                  pl.BlockSpec((tm, d), lambda i: (i, 0)), pl.BlockSpec((D_DN + D_MOBA, d), lambda i: (0, 0)),
                  row(d), row(d), pl.BlockSpec((d, LANES), lambda i: (0, 0)), row(LANES)],
        out_specs=(pl.BlockSpec((tm, d), lambda i: (i, 0)), pl.BlockSpec((tm, d), lambda i: (i, 0)),
                   pl.BlockSpec((tm, LANES), lambda i: (i, 0)), pl.BlockSpec((1, 1, LANES), lambda i: (i, 0, 0))),
        compiler_params=pltpu.CompilerParams(dimension_semantics=("parallel",), vmem_limit_bytes=VMEM_LIMIT),
        name="mix_route",
    )(y_dn, y_mb, x2, wo, g1, b1, rw, rb)


def _bucket_offsets_col(ohf):
    cnt = jnp.sum(ohf, axis=1, keepdims=True).astype(jnp.int32)
    pad = (((cnt + (GRAN - 1)) >> GRAN_SHIFT) << GRAN_SHIFT).astype(F32)
    r = lax.broadcasted_iota(jnp.int32, (LANES, LANES), 0)
    c = lax.broadcasted_iota(jnp.int32, (LANES, LANES), 1)
    before = jnp.where(c < r, 1.0, 0.0)
    return _dot(before, jnp.broadcast_to(pad, (LANES, LANES)))[:, 0:1]


def _moe_sort_kernel(gmap_ref, nvalid_ref, tail0_ref, taillen_ref, hb_ref, route_ref, lstrict_ref,
                     xg_ref, wsg_ref, xs_ref, ws_ref, zx_ref, zw_ref, sem):
    s = pl.program_id(0)
    nsteps = pl.num_programs(0)
    slot = s & 1
    ts = route_ref.shape[0]
    route = route_ref[...]
    rt = jnp.transpose(route)
    bucket_row = rt[ROUTE_BUCKET:ROUTE_BUCKET + 1, :].astype(jnp.int32)
    sub = lax.broadcasted_iota(jnp.int32, (LANES, ts), 0)
    ohf = jnp.where(sub == bucket_row, 1.0, 0.0)
    loff = _bucket_offsets_col(ohf)
    rank = lax.dot_general(ohf.astype(BF16), lstrict_ref[...], (((1,), (1,)), ((), ())),
                           preferred_element_type=F32)
    dest = jnp.sum(ohf * (loff + rank), axis=0, keepdims=True).astype(jnp.int32)
    rowi = lax.broadcasted_iota(jnp.int32, (LROWS, ts), 0)
    perm = jnp.where(rowi == dest, 1.0, 0.0).astype(BF16)
    xs_ref[slot] = jnp.dot(perm, hb_ref[...], preferred_element_type=F32).astype(BF16)
    rh, rl = _split2(route)
    wparts = jnp.dot(perm, jnp.concatenate([rh, rl], axis=1), preferred_element_type=F32)
    ws_ref[slot] = wparts[:, 0:LANES] + wparts[:, LANES:2 * LANES]

    def copies(step, g):
        sl = step & 1
        src = pl.ds(pl.multiple_of(g * GRAN, GRAN), GRAN)
        dst = pl.ds(pl.multiple_of(gmap_ref[step * LGRAN + g] * GRAN, GRAN), GRAN)
        return (pltpu.make_async_copy(xs_ref.at[sl, src, :], xg_ref.at[dst, :], sem.at[0, sl]),
                pltpu.make_async_copy(ws_ref.at[sl, src, :], wsg_ref.at[dst, :], sem.at[1, sl]))

    def fill_copies(b, i):
        dst = pl.ds(pl.multiple_of((tail0_ref[b] + i) * GRAN, GRAN), GRAN)
        return (pltpu.make_async_copy(zx_ref.at[0:GRAN, :], xg_ref.at[dst, :], sem.at[2, 0]),
                pltpu.make_async_copy(zw_ref.at[0:GRAN, :], wsg_ref.at[dst, :], sem.at[2, 1]))

    def unused_tile_copies(t):
        dst = pl.ds(pl.multiple_of(t * MOE_TM, MOE_TM), MOE_TM)
        return (pltpu.make_async_copy(zx_ref, xg_ref.at[dst, :], sem.at[2, 0]),
                pltpu.make_async_copy(zw_ref, wsg_ref.at[dst, :], sem.at[2, 1]))

    def run(step, fn):
        def body(g, carry):
            for cp in copies(step, g):
                fn(cp)
            return carry
        lax.fori_loop(0, nvalid_ref[step], body, 0)

    def run_fill(fn):
        for b in range(N_BUCKETS):
            def body(i, carry, b=b):
                for cp in fill_copies(b, i):
                    fn(cp)
                return carry
            lax.fori_loop(0, taillen_ref[b], body, 0)

        def tile_body(t, carry):
            for cp in unused_tile_copies(t):
                fn(cp)
            return carry
        lax.fori_loop(tail0_ref[N_BUCKETS], xg_ref.shape[0] // MOE_TM, tile_body, 0)

    @pl.when(s == 0)
    def _():
        zx_ref[...] = jnp.zeros_like(zx_ref)
        zw_ref[...] = jnp.zeros_like(zw_ref)
        run_fill(lambda cp: cp.start())

    run(s, lambda cp: cp.start())

    @pl.when(s > 0)
    def _():
        run(s - 1, lambda cp: cp.wait())

    @pl.when(s == nsteps - 1)
    def _():
        run(s, lambda cp: cp.wait())
        run_fill(lambda cp: cp.wait())


def _moe_sort(plan, hb, route, lstrict):
    n, d = hb.shape
    ts = MOE_TS
    rows = plan["n_tiles"] * MOE_TM
    return pl.pallas_call(
        _moe_sort_kernel,
        out_shape=(jax.ShapeDtypeStruct((rows, d), BF16), jax.ShapeDtypeStruct((rows, LANES), F32)),
        grid_spec=pltpu.PrefetchScalarGridSpec(
            num_scalar_prefetch=4,
            grid=(n // ts,),
            in_specs=[pl.BlockSpec((ts, d), lambda s, *_: (s, 0)), pl.BlockSpec((ts, LANES), lambda s, *_: (s, 0)),
                      pl.BlockSpec((ts, ts), lambda s, *_: (0, 0))],
            out_specs=(pl.BlockSpec(memory_space=pl.ANY), pl.BlockSpec(memory_space=pl.ANY)),
            scratch_shapes=[pltpu.VMEM((2, LROWS, d), BF16), pltpu.VMEM((2, LROWS, LANES), F32),
                            pltpu.VMEM((MOE_TM, d), BF16), pltpu.VMEM((MOE_TM, LANES), F32),
                            pltpu.SemaphoreType.DMA((3, 2))]),
        compiler_params=pltpu.CompilerParams(dimension_semantics=("arbitrary",), vmem_limit_bytes=VMEM_LIMIT),
        name="moe_sort",
    )(plan["gmap"], plan["nvalid"], plan["tail0"], plan["taillen"], hb, route, lstrict)


def _moe_expert_kernel(xt_ref, elo_ref, ehi_ref, valid_ref, x_ref, w_ref, wg0, wu0, wd0, wg1, wu1, wd1, o_ref):
    t = pl.program_id(0)

    @pl.when(valid_ref[t] > 0)
    def _():
        x = x_ref[...]
        w = w_ref[...]
        gates = [jnp.dot(x, wg[0].astype(BF16), preferred_element_type=F32) for wg in (wg0, wg1)]
        ups = [jnp.dot(x, wu[0].astype(BF16), preferred_element_type=F32) for wu in (wu0, wu1)]
        hes = [(_silu(gates[i]) * ups[i] * w[:, lane:lane + 1]).astype(BF16)
               for i, lane in enumerate((ROUTE_W_LO, ROUTE_W_HI))]
        o_ref[...] = (jnp.dot(hes[0], wd0[0].astype(BF16), preferred_element_type=F32)
                      + jnp.dot(hes[1], wd1[0].astype(BF16), preferred_element_type=F32)).astype(o_ref.dtype)

    @pl.when(valid_ref[t] == 0)
    def _():
        o_ref[...] = jnp.zeros_like(o_ref)


def _moe_experts(plan, xg, wsg, wg, wu, wd):
    rows, d = xg.shape
    tm = MOE_TM
    tok = lambda width: pl.BlockSpec((tm, width), lambda t, xt, elo, ehi, valid: (xt[t], 0))
    lo3 = lambda shape: pl.BlockSpec(shape, lambda t, xt, elo, ehi, valid: (elo[t], 0, 0))
    hi3 = lambda shape: pl.BlockSpec(shape, lambda t, xt, elo, ehi, valid: (ehi[t], 0, 0))
    return pl.pallas_call(
        _moe_expert_kernel,
        out_shape=jax.ShapeDtypeStruct((rows, d), BF16),
        grid_spec=pltpu.PrefetchScalarGridSpec(
            num_scalar_prefetch=4,
            grid=(rows // tm,),
            in_specs=[tok(d), tok(LANES),
                      lo3((1, d, D_EXPERT)), lo3((1, d, D_EXPERT)), lo3((1, D_EXPERT, d)),
                      hi3((1, d, D_EXPERT)), hi3((1, d, D_EXPERT)), hi3((1, D_EXPERT, d))],
            out_specs=pl.BlockSpec((tm, d), lambda t, *_: (t, 0))),
        compiler_params=pltpu.CompilerParams(dimension_semantics=("arbitrary",), vmem_limit_bytes=VMEM_LIMIT),
        name="moe_experts",
    )(plan["xtile"], plan["elo"], plan["ehi"], plan["valid"], xg, wsg, wg, wu, wd, wg, wu, wd)


def _moe_unsort_kernel(gmap_ref, og_ref, route_ref, h_ref, lstrict_ref, g_ref, b_ref, out_ref, ol_ref, sem):
    s = pl.program_id(0)
    nsteps = pl.num_programs(0)
    slot = s & 1
    ts = route_ref.shape[0]

    def gather(step, fn):
        sl = step & 1

        def body(g, carry):
            src = pl.ds(pl.multiple_of(gmap_ref[step * LGRAN + g] * GRAN, GRAN), GRAN)
            dst = pl.ds(pl.multiple_of(g * GRAN, GRAN), GRAN)
            fn(pltpu.make_async_copy(og_ref.at[src, :], ol_ref.at[sl, dst, :], sem.at[sl]))
            return carry
        lax.fori_loop(0, LGRAN, body, 0)

    @pl.when(s == 0)
    def _():
        gather(s, lambda cp: cp.start())

    @pl.when(s + 1 < nsteps)
    def _():
        gather(s + 1, lambda cp: cp.start())

    route = route_ref[...]
    bucket_col = route[:, ROUTE_BUCKET:ROUTE_BUCKET + 1].astype(jnp.int32)
    lane = lax.broadcasted_iota(jnp.int32, (ts, LANES), 1)
    ohf = jnp.where(lane == bucket_col, 1.0, 0.0)
    cnt = jnp.sum(ohf, axis=0, keepdims=True).astype(jnp.int32)
    pad = (((cnt + (GRAN - 1)) >> GRAN_SHIFT) << GRAN_SHIFT).astype(F32)
    r = lax.broadcasted_iota(jnp.int32, (LANES, LANES), 0)
    c = lax.broadcasted_iota(jnp.int32, (LANES, LANES), 1)
    loff = _dot(jnp.broadcast_to(pad, (8, LANES)), jnp.where(r < c, 1.0, 0.0))[0:1, :]
    rank = jnp.dot(lstrict_ref[...], ohf.astype(BF16), preferred_element_type=F32)
    dest = jnp.sum(ohf * (loff + rank), axis=1, keepdims=True).astype(jnp.int32)
    lrow = lax.broadcasted_iota(jnp.int32, (ts, LROWS), 1)
    perm_t = jnp.where(lrow == dest, 1.0, 0.0).astype(BF16)

    gather(s, lambda cp: cp.wait())
    ffn = jnp.dot(perm_t, ol_ref[slot], preferred_element_type=F32)
    out_ref[...] = _layer_norm(DEEPNORM_ALPHA * h_ref[...] + ffn, g_ref[...], b_ref[...])


def _moe_unsort(plan, og, route, hf, lstrict, g2, b2):
    n, d = hf.shape
    ts = MOE_TS
    row = pl.BlockSpec((1, d), lambda s, *_: (0, 0))
    return pl.pallas_call(
        _moe_unsort_kernel,
        out_shape=jax.ShapeDtypeStruct((n, d), F32),
        grid_spec=pltpu.PrefetchScalarGridSpec(
            num_scalar_prefetch=1,
            grid=(n // ts,),
            in_specs=[pl.BlockSpec(memory_space=pl.ANY), pl.BlockSpec((ts, LANES), lambda s, *_: (s, 0)),
                      pl.BlockSpec((ts, d), lambda s, *_: (s, 0)), pl.BlockSpec((ts, ts), lambda s, *_: (0, 0)),
                      row, row],
            out_specs=pl.BlockSpec((ts, d), lambda s, *_: (s, 0)),
            scratch_shapes=[pltpu.VMEM((2, LROWS, d), BF16), pltpu.SemaphoreType.DMA((2,))]),
        compiler_params=pltpu.CompilerParams(dimension_semantics=("arbitrary",), vmem_limit_bytes=VMEM_LIMIT),
        name="moe_unsort",
    )(plan["gmap_back"], og, route, hf, lstrict, g2, b2)


def _moe_plan(cnt_half, n):
    nsrc = n // MOE_TS
    i32 = jnp.int32
    cnt = cnt_half.reshape(nsrc, -1, LANES).sum(axis=1)[:, :N_BUCKETS].astype(i32)
    run_g = (cnt + GRAN - 1) // GRAN
    nvalid = run_g.sum(axis=1)
    loff_g = jnp.cumsum(run_g, axis=1) - run_g
    bucket_g = run_g.sum(axis=0)
    gpt = MOE_TM // GRAN
    btiles = (bucket_g + gpt - 1) // gpt
    tend = jnp.cumsum(btiles)
    tstart = tend - btiles
    gofs = tstart[None, :] * gpt + jnp.cumsum(run_g, axis=0) - run_g
    n_tiles = -(-(n + nsrc * N_BUCKETS * (GRAN - 1)) // MOE_TM) + N_BUCKETS + 1
    g = jnp.arange(LGRAN, dtype=i32)[None, :, None]
    in_run = (g >= loff_g[:, None, :]) & (g < (loff_g + run_g)[:, None, :])
    gmap = jnp.arange(LGRAN, dtype=i32)[None, :] + jnp.sum(jnp.where(in_run, (gofs - loff_g)[:, None, :], 0), axis=2)
    is_valid = jnp.arange(LGRAN, dtype=i32)[None, :] < nvalid[:, None]
    zero_gran = (n_tiles - 1) * gpt
    t = jnp.arange(n_tiles, dtype=i32)
    tb = jnp.minimum(jnp.sum(t[:, None] >= tend[None, :], axis=1), N_BUCKETS - 1)
    valid = (t < tend[-1]).astype(i32)
    pairs = [(a, b) for a in range(EXPERTS_PER_GROUP) for b in range(a + 1, EXPERTS_PER_GROUP)]
    pidx = tb % N_PAIRS
    pair_a = sum(jnp.where(pidx == i, a, 0) for i, (a, _) in enumerate(pairs))
    pair_b = sum(jnp.where(pidx == i, b, 0) for i, (_, b) in enumerate(pairs))
    grp = tb // N_PAIRS
    return {
        "n_tiles": n_tiles,
        "gmap": jnp.where(is_valid, gmap, 0).reshape(-1).astype(i32),
        "gmap_back": jnp.where(is_valid, gmap, zero_gran).reshape(-1).astype(i32),
        "nvalid": nvalid.astype(i32),
        "tail0": jnp.concatenate([tstart * gpt + bucket_g, tend[-1:]]).astype(i32),
        "taillen": (btiles * gpt - bucket_g).astype(i32),
        "xtile": jnp.where(valid > 0, t, 0).astype(i32),
        "elo": (grp * EXPERTS_PER_GROUP + pair_a).astype(i32),
        "ehi": (grp * EXPERTS_PER_GROUP + pair_b).astype(i32),
        "valid": valid,
    }


def _pad_lanes(a, lane0=0):
    return jnp.zeros((1, LANES), F32).at[0, lane0:lane0 + a.shape[0]].set(a.astype(F32))


def _rope_tables(seq):
    half = HEAD_DIM // 2
    inv_freq = ROPE_THETA ** (-jnp.arange(half, dtype=F32) / half)
    ang = jnp.arange(seq).astype(F32)[:, None] * inv_freq[None, :]
    cos, sin = jnp.cos(ang), jnp.sin(ang)
    return jnp.concatenate([cos, cos], axis=-1), jnp.concatenate([-sin, sin], axis=-1)


def _layer(x, w_in, conv_w, a_log, dt_bias, dn_norm_w, w_out, ln1_g, ln1_b, router_w1, router_b1,
           router_w2, router_b2, w_gate, w_up, w_down, ln2_g, ln2_b):
    bsz, seq, d = x.shape
    n = bsz * seq
    x2 = x.reshape(n, d)

    o_z, o_b, o_mb = 3 * D_DN, 4 * D_DN, 4 * D_DN + 2 * N_HEADS_DN
    w_ba = jnp.pad(w_in[:, o_b:o_mb], ((0, 0), (0, LANES - 2 * N_HEADS_DN)))
    w_all = jnp.concatenate([w_in[:, :o_z], w_in[:, o_z:o_b], w_ba, w_in[:, o_mb:]], axis=1).astype(BF16)

    tm = min(512, n)
    dn_qkv, z, ba, mb_qkv = _in_proj(x2, w_all, conv_w, tm, seq)

    y_dn = _deltanet(dn_qkv.reshape(bsz, seq, 3 * D_DN), z.reshape(bsz, seq, D_DN), ba.reshape(bsz, seq, LANES),
                     _pad_lanes(a_log, N_HEADS_DN), _pad_lanes(dt_bias, N_HEADS_DN),
                     dn_norm_w.astype(F32).reshape(1, HEAD_DIM))

    cos_t, sin_t = _rope_tables(seq)
    q_r, k_r, vt, sel = _moba_prep(mb_qkv.reshape(bsz, seq, 3 * D_MOBA), cos_t, sin_t)
    y_mb = _moba_attn(q_r, k_r, vt, sel)

    rw = jnp.concatenate([router_w1, jnp.transpose(router_w2, (1, 0, 2)).reshape(d, N_EXPERTS)], axis=1)
    rw = jnp.pad(rw, ((0, 0), (0, LANES - rw.shape[1])))
    rb = _pad_lanes(jnp.concatenate([router_b1, router_b2.reshape(-1)]))
    hf, hb, route, cnt = _mix_route(y_dn.reshape(n, D_DN), y_mb.reshape(n, D_MOBA), x2, w_out.astype(BF16),
                                    ln1_g.reshape(1, d), ln1_b.reshape(1, d), rw, rb, tm)

    plan = _moe_plan(cnt, n)
    idx = jnp.arange(MOE_TS, dtype=jnp.int32)
    lstrict = (idx[None, :] < idx[:, None]).astype(BF16)
    xg, wsg = _moe_sort(plan, hb, route, lstrict)
    og = _moe_experts(plan, xg, wsg, w_gate, w_up, w_down)
    out = _moe_unsort(plan, og, route, hf, lstrict, ln2_g.reshape(1, d), ln2_b.reshape(1, d))
    return out.reshape(bsz, seq, d)


def kernel(x, w_in, conv_w, a_log, dt_bias, dn_norm_w, w_out, ln1_g, ln1_b, router_w1, router_b1, router_w2, router_b2, expert_w_gate, expert_w_up, expert_w_down, ln2_g, ln2_b):
    for l in range(DEPTH):
        x = _layer(x, w_in[l], conv_w[l], a_log[l], dt_bias[l], dn_norm_w[l], w_out[l], ln1_g[l], ln1_b[l],
                   router_w1[l], router_b1[l], router_w2[l], router_b2[l], expert_w_gate[l], expert_w_up[l],
                   expert_w_down[l], ln2_g[l], ln2_b[l])
    return x
```

```python
import functools

import jax
import jax.numpy as jnp
from jax import lax
from jax.experimental import pallas as pl
from jax.experimental.pallas import tpu as pltpu

F32 = jnp.float32
BF16 = jnp.bfloat16

HEAD_DIM = 128
N_HEADS_DN = 4
N_HEADS_MOBA = 4
D_DN = N_HEADS_DN * HEAD_DIM
D_MOBA = N_HEADS_MOBA * HEAD_DIM
CONV_K = 4
DN_CHUNK = 64
MOBA_BLOCK = 256
MOBA_TOPK = 3
ROPE_THETA = 10000.0
N_GROUPS = 4
EXPERTS_PER_GROUP = 4
N_EXPERTS = N_GROUPS * EXPERTS_PER_GROUP
D_EXPERT = 256
LN_EPS = 1e-5
RMS_EPS = 1e-6
L2_EPS = 1e-6
NEG_INF = -1e30
LOG2E = 1.4426950408889634
DEPTH = 1
DEEPNORM_ALPHA = (2 * DEPTH) ** 0.25

LANES = 128
DN_TILE = 256
DN_HEADS_PER_STEP = 4
GATE_LANE0 = N_GROUPS
N_PAIRS = EXPERTS_PER_GROUP * (EXPERTS_PER_GROUP - 1) // 2
N_BUCKETS = N_GROUPS * N_PAIRS
ROUTE_BUCKET, ROUTE_W_LO, ROUTE_W_HI = 0, 1, 2
MOE_TS = 512
MOE_TM = 512
GRAN = 16
GRAN_SHIFT = 4
LROWS = -(-(MOE_TS + N_BUCKETS * (GRAN - 1)) // LANES) * LANES
LGRAN = LROWS // GRAN
VMEM_LIMIT = 48 * 1024 * 1024


def _dot(a, b):
    return jnp.dot(a.astype(BF16), b.astype(BF16), preferred_element_type=F32)


def _dot_nt(a, b):
    return lax.dot_general(a.astype(BF16), b.astype(BF16), (((1,), (1,)), ((), ())),
                           preferred_element_type=F32)


def _split2(a):
    hi = a.astype(BF16)
    lo = (a - hi.astype(F32)).astype(BF16)
    return hi, lo


def _split3(a):
    hi = a.astype(BF16)
    r = a - hi.astype(F32)
    mid = r.astype(BF16)
    lo = (r - mid.astype(F32)).astype(BF16)
    return hi, mid, lo


def _dot3(a, b):
    ah, al = _split2(a)
    bh, bl = _split2(b)
    return (jnp.dot(ah, bh, preferred_element_type=F32) + jnp.dot(ah, bl, preferred_element_type=F32)
            + jnp.dot(al, bh, preferred_element_type=F32))


def _dot3_nt(a, b):
    ah, al = _split2(a)
    bh, bl = _split2(b)
    dn = (((1,), (1,)), ((), ()))
    return (lax.dot_general(ah, bh, dn, preferred_element_type=F32)
            + lax.dot_general(ah, bl, dn, preferred_element_type=F32)
            + lax.dot_general(al, bh, dn, preferred_element_type=F32))


def _dot_exact_lhs(a_bf16, b):
    bh, bm, bl = _split3(b)
    return (jnp.dot(a_bf16, bh, preferred_element_type=F32) + jnp.dot(a_bf16, bm, preferred_element_type=F32)
            + jnp.dot(a_bf16, bl, preferred_element_type=F32))


def _silu(x):
    return x * jax.nn.sigmoid(x)


def _softplus(x):
    return jnp.maximum(x, 0.0) + jnp.log1p(jnp.exp(-jnp.abs(x)))


def _layer_norm(t, g, b):
    mu = jnp.mean(t, axis=-1, keepdims=True)
    d = t - mu
    var = jnp.mean(d * d, axis=-1, keepdims=True)
    return d * lax.rsqrt(var + LN_EPS) * g + b


def _lane_pick(x, lane):
    ids = lax.broadcasted_iota(jnp.int32, x.shape, 1)
    return jnp.sum(jnp.where(ids == lane, x, 0.0), axis=1, keepdims=True)


def _in_proj_kernel(x_ref, w_ref, cw_ref, dn_ref, z_ref, ba_ref, mb_ref, cb_ref, *, tiles_per_seq):
    i = pl.program_id(0)
    tm = x_ref.shape[0]
    o0 = 3 * D_DN
    o1 = o0 + D_DN
    o2 = o1 + LANES

    @pl.when(i % tiles_per_seq == 0)
    def _():
        cb_ref[0:8, :] = jnp.zeros((8, o0), F32)

    xb = x_ref[...].astype(BF16)
    u = jnp.dot(xb, w_ref[:, 0:o0], preferred_element_type=F32)
    z_ref[...] = _silu(jnp.dot(xb, w_ref[:, o0:o1], preferred_element_type=F32))
    ba_ref[...] = jnp.dot(xb, w_ref[:, o1:o2], preferred_element_type=F32)
    mb_ref[...] = jnp.dot(xb, w_ref[:, o2:o2 + 3 * D_MOBA], preferred_element_type=F32)

    cb_ref[8:8 + tm, :] = u
    acc = cw_ref[CONV_K - 1:CONV_K, :] * u
    for s in range(1, CONV_K):
        acc = acc + cw_ref[CONV_K - 1 - s:CONV_K - s, :] * cb_ref[8 - s:8 - s + tm, :]
    cb_ref[0:8, :] = u[tm - 8:tm, :]
    qkv = _silu(acc)

    outs = []
    for h in range(2 * N_HEADS_DN):
        t = qkv[:, h * HEAD_DIM:(h + 1) * HEAD_DIM]
        t = t * lax.rsqrt(jnp.sum(t * t, axis=-1, keepdims=True) + L2_EPS)
        outs.append(t * (HEAD_DIM ** -0.5) if h < N_HEADS_DN else t)
    outs.append(qkv[:, 2 * D_DN:3 * D_DN])
    dn_ref[...] = jnp.concatenate(outs, axis=1)


def _in_proj(x2, w_all, conv_w, tm, seq):
    n, d = x2.shape
    wc = w_all.shape[1]
    return pl.pallas_call(
        functools.partial(_in_proj_kernel, tiles_per_seq=seq // tm),
        out_shape=(jax.ShapeDtypeStruct((n, 3 * D_DN), F32), jax.ShapeDtypeStruct((n, D_DN), F32),
                   jax.ShapeDtypeStruct((n, LANES), F32), jax.ShapeDtypeStruct((n, 3 * D_MOBA), F32)),
        grid=(n // tm,),
        in_specs=[pl.BlockSpec((tm, d), lambda i: (i, 0)), pl.BlockSpec((d, wc), lambda i: (0, 0)),
                  pl.BlockSpec((CONV_K, 3 * D_DN), lambda i: (0, 0))],
        out_specs=(pl.BlockSpec((tm, 3 * D_DN), lambda i: (i, 0)), pl.BlockSpec((tm, D_DN), lambda i: (i, 0)),
                   pl.BlockSpec((tm, LANES), lambda i: (i, 0)), pl.BlockSpec((tm, 3 * D_MOBA), lambda i: (i, 0))),
        scratch_shapes=[pltpu.VMEM((8 + tm, 3 * D_DN), F32)],
        compiler_params=pltpu.CompilerParams(dimension_semantics=("arbitrary",), vmem_limit_bytes=VMEM_LIMIT),
        name="in_proj",
    )(x2, w_all, conv_w)


def _deltanet_kernel(q_ref, k_ref, v_ref, z_ref, ba_ref, alog_ref, dtb_ref, normw_ref, y_ref,
                     s_ref, wq_s, u_s, qk_s, kdt_s, egl_s, *, hb):
    hg = pl.program_id(1)
    t = pl.program_id(2)
    tt = DN_TILE
    nchunk = tt // DN_CHUNK
    hs = range(hb)

    @pl.when(t == 0)
    def _():
        for ref in (s_ref, wq_s, u_s, qk_s, kdt_s, egl_s):
            ref[...] = jnp.zeros_like(ref)

    rd = t & 1
    wr = 1 - rd
    state = [s_ref[h] for h in hs]
    outs = [[] for _ in hs]

    pend = {}

    def chain_a(c):
        pend["r"] = [jnp.dot(wq_s[rd, h, c], state[h].astype(BF16), preferred_element_type=F32) for h in hs]

    def chain_b(c):
        lo, hi = c * DN_CHUNK, (c + 1) * DN_CHUNK
        r = pend["r"]
        vz = []
        for h in hs:
            parts = []
            if lo > 0:
                parts.append(jnp.zeros((lo, HEAD_DIM), F32))
            parts.append(u_s[rd, h, lo:hi, :] - r[h][0:DN_CHUNK, :])
            if hi < tt:
                parts.append(jnp.zeros((tt - hi, HEAD_DIM), F32))
            vz.append(jnp.concatenate(parts, axis=0).astype(BF16))
        for h in hs:
            outs[h].append(r[h][DN_CHUNK:2 * DN_CHUNK, :]
                           + jnp.dot(qk_s[rd, h, lo:hi, :], vz[h], preferred_element_type=F32))
        for h in hs:
            state[h] = (state[h] * egl_s[rd, h, 8 * c:8 * c + 1, :]
                        + jnp.dot(kdt_s[rd, h], vz[h], preferred_element_type=F32))

    chain_a(0)

    q_all = q_ref[0]
    k_all = k_ref[0]
    v_all = v_ref[0]

    ba = ba_ref[0]
    beta_all = jax.nn.sigmoid(ba)
    g_all = -jnp.exp(alog_ref[...]) * _softplus(ba + dtb_ref[...])

    row = lax.broadcasted_iota(jnp.int32, (tt, tt), 0)
    col = lax.broadcasted_iota(jnp.int32, (tt, tt), 1)
    same = (row >> 6) == (col >> 6)
    incl = same & (row >= col)
    strict = same & (row > col)

    gc_all = _dot_exact_lhs(incl.astype(BF16), g_all)
    gct = jnp.transpose(gc_all)
    sub = lax.broadcasted_iota(jnp.int32, gct.shape, 0)
    sls = [slice(hh * HEAD_DIM, (hh + 1) * HEAD_DIM) for hh in hs]
    heads = [hg * hb + hh for hh in hs]
    q = [q_all[:, sl] for sl in sls]
    k = [k_all[:, sl] for sl in sls]
    v = [v_all[:, sl] for sl in sls]
    beta = [_lane_pick(beta_all, h) for h in heads]
    gcc = [_lane_pick(gc_all, h + N_HEADS_DN) for h in heads]
    gcr = [jnp.sum(jnp.where(sub == h + N_HEADS_DN, gct, 0.0), axis=0, keepdims=True) for h in heads]
    chain_b(0)

    decay = [jnp.where(incl, jnp.exp(jnp.where(incl, gcc[h] - gcr[h], 0.0)), 0.0) for h in hs]
    kb = [k[h] * beta[h] for h in hs]
    vb = [v[h] * beta[h] for h in hs]
    a_mat = [jnp.where(strict, _dot_nt(kb[h], k[h]) * decay[h], 0.0) for h in hs]
    chain_a(1)
    qk = [_dot_nt(q[h], k[h]) * decay[h] for h in hs]
    eye = (row == col).astype(F32)
    d8 = (row >> 3) == (col >> 3)
    a8 = [jnp.where(d8, a, 0.0) for a in a_mat]
    chain_b(1)
    a8_2 = [_dot(a, a) for a in a8]
    chain_a(2)
    a8_4 = [_dot(a, a) for a in a8_2]
    chain_b(2)
    x = [_dot(eye - a, eye + a2) for a, a2 in zip(a8, a8_2)]
    chain_a(3)
    x = [_dot(xi, eye + a4) for xi, a4 in zip(x, a8_4)]
    chain_b(3)
    s = 8
    while s < DN_CHUNK:
        sh = s.bit_length() - 1
        off = ((row >> (sh + 1)) == (col >> (sh + 1))) & ((row >> sh) != (col >> sh))
        y = [_dot(jnp.where(off, a, 0.0), xi) for a, xi in zip(a_mat, x)]
        x = [xi - _dot(xi, yi) for xi, yi in zip(x, y)]
        s *= 2
    tinv = x
    eg = [jnp.exp(g) for g in gcc]
    wu = [_dot(tinv[h], jnp.concatenate([kb[h] * eg[h], vb[h]], axis=1)) for h in hs]
    qd = [q[h] * eg[h] for h in hs]
    gl_rows = [[g[(c + 1) * DN_CHUNK - 1:(c + 1) * DN_CHUNK, :] for c in range(nchunk)] for g in gcc]
    gl_col = [jnp.concatenate([jnp.broadcast_to(g, (DN_CHUNK, 1)) for g in rows], axis=0) for rows in gl_rows]
    kdt = [jnp.transpose(k[h] * jnp.exp(gl_col[h] - gcc[h])) for h in hs]

    ys = []
    for h in hs:
        o = jnp.concatenate(outs[h], axis=0)
        o = o * lax.rsqrt(jnp.mean(o * o, axis=-1, keepdims=True) + RMS_EPS) * normw_ref[...]
        ys.append(o * z_ref[0, :, sls[h]])
    y_ref[0] = jnp.concatenate(ys, axis=1).astype(y_ref.dtype)
    s_ref[...] = jnp.stack(state, axis=0)

    for h in hs:
        for c in range(nchunk):
            lo, hi = c * DN_CHUNK, (c + 1) * DN_CHUNK
            wq_s[wr, h, c] = jnp.concatenate([wu[h][lo:hi, 0:HEAD_DIM], qd[h][lo:hi, :]], axis=0).astype(BF16)
            egl_s[wr, h, 8 * c:8 * c + 8, :] = jnp.broadcast_to(jnp.exp(gl_rows[h][c]), (8, HEAD_DIM))
        u_s[wr, h] = wu[h][:, HEAD_DIM:2 * HEAD_DIM]
        qk_s[wr, h] = qk[h].astype(BF16)
        kdt_s[wr, h] = kdt[h].astype(BF16)


def _deltanet(dn_qkv, z, ba, alog_row, dtb_row, normw_row):
    bsz, seq, _ = dn_qkv.shape
    tt = DN_TILE
    nt = seq // tt
    hb = DN_HEADS_PER_STEP
    ng = N_HEADS_DN // hb
    w = hb * HEAD_DIM
    nchunk = tt // DN_CHUNK

    def cur_spec(off, width):
        return pl.BlockSpec((1, tt, width), lambda b, g, t: (b, jnp.minimum(t, nt - 1), g + off))

    prev_spec = pl.BlockSpec((1, tt, w), lambda b, g, t: (b, jnp.maximum(t - 1, 0), g))
    row_spec = pl.BlockSpec((1, LANES), lambda b, g, t: (0, 0))
    return pl.pallas_call(
        functools.partial(_deltanet_kernel, hb=hb),
        out_shape=jax.ShapeDtypeStruct((bsz, seq, D_DN), BF16),
        grid=(bsz, ng, nt + 1),
        in_specs=[cur_spec(0, w), cur_spec(ng, w), cur_spec(2 * ng, w), prev_spec,
                  pl.BlockSpec((1, tt, LANES), lambda b, g, t: (b, jnp.minimum(t, nt - 1), 0)),
                  row_spec, row_spec, row_spec],
        out_specs=prev_spec,
        scratch_shapes=[pltpu.VMEM((hb, HEAD_DIM, HEAD_DIM), F32),
                        pltpu.VMEM((2, hb, nchunk, 2 * DN_CHUNK, HEAD_DIM), BF16),
                        pltpu.VMEM((2, hb, tt, HEAD_DIM), F32),
                        pltpu.VMEM((2, hb, tt, tt), BF16),
                        pltpu.VMEM((2, hb, HEAD_DIM, tt), BF16),
                        pltpu.VMEM((2, hb, 8 * nchunk, HEAD_DIM), F32)],
        compiler_params=pltpu.CompilerParams(dimension_semantics=("parallel", "parallel", "arbitrary"),
                                             vmem_limit_bytes=VMEM_LIMIT),
        name="deltanet",
    )(dn_qkv, dn_qkv, dn_qkv, z, ba, alog_row, dtb_row, normw_row)


def _moba_prep_kernel(x_ref, cos_ref, sin_ref, q_ref, k_ref, vt_ref, sel_ref, km_ref, *, nb, topk):
    j = pl.program_id(1)

    @pl.when(j == 0)
    def _():
        km_ref[...] = jnp.zeros_like(km_ref)

    cos = cos_ref[...]
    sin = sin_ref[...]
    half = HEAD_DIM // 2
    blk = lax.broadcasted_iota(jnp.int32, (nb, MOBA_BLOCK), 0)
    kmeans = []
    for h in range(N_HEADS_MOBA):
        qh = x_ref[0, :, h * HEAD_DIM:(h + 1) * HEAD_DIM]
        kh = x_ref[0, :, D_MOBA + h * HEAD_DIM:D_MOBA + (h + 1) * HEAD_DIM]
        qr = (qh * cos + pltpu.roll(qh, half, 1) * sin) * (HEAD_DIM ** -0.5)
        kr = kh * cos + pltpu.roll(kh, half, 1) * sin
        q_ref[0, :, h * HEAD_DIM:(h + 1) * HEAD_DIM] = (qr * LOG2E).astype(q_ref.dtype)
        k_ref[0, :, h * HEAD_DIM:(h + 1) * HEAD_DIM] = kr.astype(k_ref.dtype)
        kmeans.append(jnp.mean(kr, axis=0, keepdims=True))

        gate = _dot3_nt(km_ref[:, h * HEAD_DIM:(h + 1) * HEAD_DIM], qr)
        gate = jnp.where(blk < j, gate, NEG_INF)
        rank = jnp.zeros(gate.shape, F32)
        for m in range(nb):
            gm = gate[m:m + 1, :]
            ahead = (gm > gate) | ((gm == gate) & (blk > m))
            rank = rank + jnp.where(ahead, 1.0, 0.0)
        sel = (blk < j) & (rank < topk)
        sel_ref[0, 0, h * nb:(h + 1) * nb, :] = jnp.where(sel, 1.0, 0.0)

    km_ref[pl.ds(j, 1), :] = jnp.concatenate(kmeans, axis=1)
    vt_ref[0, 0] = jnp.transpose(x_ref[0, :, 2 * D_MOBA:3 * D_MOBA]).astype(vt_ref.dtype)


def _moba_prep(mb_qkv, cos_t, sin_t):
    bsz, seq, _ = mb_qkv.shape
    nb = seq // MOBA_BLOCK
    topk = min(MOBA_TOPK, nb)
    kern = functools.partial(_moba_prep_kernel, nb=nb, topk=topk)
    tok_spec = pl.BlockSpec((1, MOBA_BLOCK, D_MOBA), lambda b, j: (b, j, 0))
    tab_spec = pl.BlockSpec((MOBA_BLOCK, HEAD_DIM), lambda b, j: (j, 0))
    return pl.pallas_call(
        kern,
        out_shape=(jax.ShapeDtypeStruct((bsz, seq, D_MOBA), BF16), jax.ShapeDtypeStruct((bsz, seq, D_MOBA), BF16),
                   jax.ShapeDtypeStruct((bsz, nb, D_MOBA, MOBA_BLOCK), BF16),
                   jax.ShapeDtypeStruct((bsz, nb, N_HEADS_MOBA * nb, MOBA_BLOCK), F32)),
        grid=(bsz, nb),
        in_specs=[pl.BlockSpec((1, MOBA_BLOCK, 3 * D_MOBA), lambda b, j: (b, j, 0)), tab_spec, tab_spec],
        out_specs=(tok_spec, tok_spec,
                   pl.BlockSpec((1, 1, D_MOBA, MOBA_BLOCK), lambda b, j: (b, j, 0, 0)),
                   pl.BlockSpec((1, 1, N_HEADS_MOBA * nb, MOBA_BLOCK), lambda b, j: (b, j, 0, 0))),
        scratch_shapes=[pltpu.VMEM((nb, D_MOBA), F32)],
        compiler_params=pltpu.CompilerParams(dimension_semantics=("parallel", "arbitrary"),
                                             vmem_limit_bytes=VMEM_LIMIT),
        name="moba_prep",
    )(mb_qkv, cos_t, sin_t)


def _moba_attn_kernel(q_ref, k_ref, vt_ref, sel_ref, o_ref, acc_ref, *, nb):
    j = pl.program_id(1)
    blk = MOBA_BLOCK
    cw = 2 * blk
    nh = N_HEADS_MOBA
    nc = (j + 1) // 2
    dn = (((1,), (1,)), ((), ()))
    hsl = [slice(h * HEAD_DIM, (h + 1) * HEAD_DIM) for h in range(nh)]
    qs = [q_ref[0, :, hsl[h]] for h in range(nh)]

    ki = lax.broadcasted_iota(jnp.int32, (blk, blk), 0)
    qi = lax.broadcasted_iota(jnp.int32, (blk, blk), 1)
    own = pl.ds(pl.multiple_of(j * blk, blk), blk)
    s_own = [jnp.where(ki <= qi, lax.dot_general(k_ref[0, own, hsl[h]], qs[h], dn, preferred_element_type=F32),
                       NEG_INF) for h in range(nh)]

    def scores(c, h):
        kc = k_ref[0, pl.ds(pl.multiple_of(c * cw, cw), cw), hsl[h]]
        s = lax.dot_general(kc, qs[h], dn, preferred_element_type=F32)
        parts = []
        for i in range(2):
            selrow = sel_ref[0, 0, pl.ds(h * nb + 2 * c + i, 1), :]
            parts.append(jnp.where(selrow > 0.5, s[i * blk:(i + 1) * blk, :], NEG_INF))
        return jnp.concatenate(parts, axis=0)

    ones8 = jnp.ones((8, cw), BF16)
    ms, ls = [], []
    for h in range(nh):
        m = jnp.max(s_own[h], axis=0, keepdims=True)
        pb = jnp.exp2(s_own[h] - m).astype(BF16)
        ms.append(m)
        ls.append(jnp.dot(ones8[:, 0:blk], pb, preferred_element_type=F32)[0:1, :])
        acc_ref[h] = jnp.dot(vt_ref[0, j, hsl[h], :], pb, preferred_element_type=F32)

    def softmax_pv(c, h, s, m, l):
        m_new = jnp.maximum(m, jnp.max(s, axis=0, keepdims=True))
        alpha = jnp.exp2(m - m_new)
        pb = jnp.exp2(s - m_new).astype(BF16)
        l = alpha * l + jnp.dot(ones8, pb, preferred_element_type=F32)[0:1, :]
        pv = (jnp.dot(vt_ref[0, 2 * c, hsl[h], :], pb[0:blk, :], preferred_element_type=F32)
              + jnp.dot(vt_ref[0, 2 * c + 1, hsl[h], :], pb[blk:cw, :], preferred_element_type=F32))
        acc_ref[h] = acc_ref[h] * alpha + pv
        return m_new, l

    def body(pair, carry):
        ms, ls = (list(t) for t in carry)
        c0 = 2 * pair
        s0 = [scores(c0, h) for h in range(nh)]
        s1 = []
        for h in range(nh):
            s1.append(scores(c0 + 1, h))
            ms[h], ls[h] = softmax_pv(c0, h, s0[h], ms[h], ls[h])
        for h in range(nh):
            ms[h], ls[h] = softmax_pv(c0 + 1, h, s1[h], ms[h], ls[h])
        return tuple(ms), tuple(ls)

    ms, ls = lax.fori_loop(0, (nc + 1) // 2, body, (tuple(ms), tuple(ls)))
    o_ref[0] = jnp.concatenate([jnp.transpose(acc_ref[h] / ls[h]) for h in range(nh)],
                               axis=1).astype(o_ref.dtype)


def _moba_attn(q_r, k_r, vt, sel):
    bsz, seq, _ = q_r.shape
    nb = seq // MOBA_BLOCK
    tok_spec = pl.BlockSpec((1, MOBA_BLOCK, D_MOBA), lambda b, j: (b, j, 0))
    return pl.pallas_call(
        functools.partial(_moba_attn_kernel, nb=nb),
        out_shape=jax.ShapeDtypeStruct((bsz, seq, D_MOBA), BF16),
        grid=(bsz, nb),
        in_specs=[tok_spec,
                  pl.BlockSpec((1, seq, D_MOBA), lambda b, j: (b, 0, 0)),
                  pl.BlockSpec((1, nb, D_MOBA, MOBA_BLOCK), lambda b, j: (b, 0, 0, 0)),
                  pl.BlockSpec((1, 1, N_HEADS_MOBA * nb, MOBA_BLOCK), lambda b, j: (b, j, 0, 0))],
        out_specs=tok_spec,
        scratch_shapes=[pltpu.VMEM((N_HEADS_MOBA, HEAD_DIM, MOBA_BLOCK), F32)],
        compiler_params=pltpu.CompilerParams(dimension_semantics=("parallel", "arbitrary"),
                                             vmem_limit_bytes=VMEM_LIMIT),
        name="moba_attn",
    )(q_r, k_r, vt, sel)


def _mix_route_kernel(ydn_ref, ymb_ref, x_ref, wo_ref, g_ref, b_ref, rw_ref, rb_ref, h_ref, hb_ref, route_ref,
                      cnt_ref):
    mix = (jnp.dot(ydn_ref[...], wo_ref[0:D_DN, :], preferred_element_type=F32)
           + jnp.dot(ymb_ref[...], wo_ref[D_DN:D_DN + D_MOBA, :], preferred_element_type=F32))
    hval = _layer_norm(DEEPNORM_ALPHA * x_ref[...] + mix, g_ref[...], b_ref[...])
    h_ref[...] = hval
    hb_ref[...] = hval.astype(BF16)

    hh, hl = _split2(hval)
    wh, wl = _split2(rw_ref[...])
    both = jnp.dot(hh, jnp.concatenate([wh, wl], axis=1), preferred_element_type=F32)
    logits = (both[:, 0:LANES] + both[:, LANES:2 * LANES] + jnp.dot(hl, wh, preferred_element_type=F32)
              + rb_ref[...])
    lane = lax.broadcasted_iota(jnp.int32, logits.shape, 1)
    big = jnp.int32(LANES)

    def first_lane(mask):
        return jnp.min(jnp.where(mask, lane, big), axis=1, keepdims=True)

    is_g = lane < N_GROUPS
    m1 = jnp.max(jnp.where(is_g, logits, NEG_INF), axis=1, keepdims=True)
    s1 = jnp.sum(jnp.where(is_g, jnp.exp(logits - m1), 0.0), axis=1, keepdims=True)
    pg = 1.0 / s1
    gsel = first_lane(is_g & (logits == m1))

    in_grp = (lane >= GATE_LANE0) & (((lane - GATE_LANE0) >> 2) == gsel) & (lane < GATE_LANE0 + N_EXPERTS)
    m2 = jnp.max(jnp.where(in_grp, logits, NEG_INF), axis=1, keepdims=True)
    s2 = jnp.sum(jnp.where(in_grp, jnp.exp(logits - m2), 0.0), axis=1, keepdims=True)
    e1 = first_lane(in_grp & (logits == m2))
    rest = in_grp & (lane != e1)
    m2b = jnp.max(jnp.where(rest, logits, NEG_INF), axis=1, keepdims=True)
    e2 = first_lane(rest & (logits == m2b))
    pe1 = 1.0 / s2
    pe2 = jnp.exp(m2b - m2) / s2
    tot = pe1 + pe2
    w1 = pg * (pe1 / tot)
    w2 = pg * (pe2 / tot)
    first_lo = e1 < e2
    lo = jnp.minimum(e1, e2)
    hi = jnp.maximum(e1, e2)
    a = (lo - GATE_LANE0) & (EXPERTS_PER_GROUP - 1)
    b = (hi - GATE_LANE0) & (EXPERTS_PER_GROUP - 1)
    bucket = gsel * N_PAIRS + ((a * (2 * EXPERTS_PER_GROUP - 1 - a)) >> 1) + (b - a - 1)
    route_ref[...] = jnp.where(lane == ROUTE_BUCKET, bucket.astype(F32),
                               jnp.where(lane == ROUTE_W_LO, jnp.where(first_lo, w1, w2),
                                         jnp.where(lane == ROUTE_W_HI, jnp.where(first_lo, w2, w1), 0.0)))
    cnt_ref[0] = jnp.sum(jnp.where(lane == bucket, 1.0, 0.0), axis=0, keepdims=True)


def _mix_route(y_dn, y_mb, x2, wo, g1, b1, rw, rb, tm):
    n, d = x2.shape
    row = lambda w: pl.BlockSpec((1, w), lambda i: (0, 0))
    return pl.pallas_call(
        _mix_route_kernel,
        out_shape=(jax.ShapeDtypeStruct((n, d), F32), jax.ShapeDtypeStruct((n, d), BF16),
                   jax.ShapeDtypeStruct((n, LANES), F32), jax.ShapeDtypeStruct((n // tm, 1, LANES), F32)),
        grid=(n // tm,),
        in_specs=[pl.BlockSpec((tm, D_DN), lambda i: (i, 0)), pl.BlockSpec((tm, D_MOBA), lambda i: (i, 0)),
                  pl.BlockSpec((tm, d), lambda i: (i, 0)), pl.BlockSpec((D_DN + D_MOBA, d), lambda i: (0, 0)),
                  row(d), row(d), pl.BlockSpec((d, LANES), lambda i: (0, 0)), row(LANES)],
        out_specs=(pl.BlockSpec((tm, d), lambda i: (i, 0)), pl.BlockSpec((tm, d), lambda i: (i, 0)),
                   pl.BlockSpec((tm, LANES), lambda i: (i, 0)), pl.BlockSpec((1, 1, LANES), lambda i: (i, 0, 0))),
        compiler_params=pltpu.CompilerParams(dimension_semantics=("parallel",), vmem_limit_bytes=VMEM_LIMIT),
        name="mix_route",
    )(y_dn, y_mb, x2, wo, g1, b1, rw, rb)


def _bucket_offsets_col(ohf):
    cnt = jnp.sum(ohf, axis=1, keepdims=True).astype(jnp.int32)
    pad = (((cnt + (GRAN - 1)) >> GRAN_SHIFT) << GRAN_SHIFT).astype(F32)
    r = lax.broadcasted_iota(jnp.int32, (LANES, LANES), 0)
    c = lax.broadcasted_iota(jnp.int32, (LANES, LANES), 1)
    before = jnp.where(c < r, 1.0, 0.0)
    return _dot(before, jnp.broadcast_to(pad, (LANES, LANES)))[:, 0:1]


def _moe_sort_kernel(gmap_ref, nvalid_ref, tail0_ref, taillen_ref, hb_ref, route_ref, lstrict_ref,
                     xg_ref, wsg_ref, xs_ref, ws_ref, zx_ref, zw_ref, sem):
    s = pl.program_id(0)
    nsteps = pl.num_programs(0)
    slot = s & 1
    ts = route_ref.shape[0]
    route = route_ref[...]
    rt = jnp.transpose(route)
    bucket_row = rt[ROUTE_BUCKET:ROUTE_BUCKET + 1, :].astype(jnp.int32)
    sub = lax.broadcasted_iota(jnp.int32, (LANES, ts), 0)
    ohf = jnp.where(sub == bucket_row, 1.0, 0.0)
    loff = _bucket_offsets_col(ohf)
    rank = lax.dot_general(ohf.astype(BF16), lstrict_ref[...], (((1,), (1,)), ((), ())),
                           preferred_element_type=F32)
    dest = jnp.sum(ohf * (loff + rank), axis=0, keepdims=True).astype(jnp.int32)
    rowi = lax.broadcasted_iota(jnp.int32, (LROWS, ts), 0)
    perm = jnp.where(rowi == dest, 1.0, 0.0).astype(BF16)
    xs_ref[slot] = jnp.dot(perm, hb_ref[...], preferred_element_type=F32).astype(BF16)
    rh, rl = _split2(route)
    wparts = jnp.dot(perm, jnp.concatenate([rh, rl], axis=1), preferred_element_type=F32)
    ws_ref[slot] = wparts[:, 0:LANES] + wparts[:, LANES:2 * LANES]

    def copies(step, g):
        sl = step & 1
        src = pl.ds(pl.multiple_of(g * GRAN, GRAN), GRAN)
        dst = pl.ds(pl.multiple_of(gmap_ref[step * LGRAN + g] * GRAN, GRAN), GRAN)
        return (pltpu.make_async_copy(xs_ref.at[sl, src, :], xg_ref.at[dst, :], sem.at[0, sl]),
                pltpu.make_async_copy(ws_ref.at[sl, src, :], wsg_ref.at[dst, :], sem.at[1, sl]))

    def fill_copies(b, i):
        dst = pl.ds(pl.multiple_of((tail0_ref[b] + i) * GRAN, GRAN), GRAN)
        return (pltpu.make_async_copy(zx_ref.at[0:GRAN, :], xg_ref.at[dst, :], sem.at[2, 0]),
                pltpu.make_async_copy(zw_ref.at[0:GRAN, :], wsg_ref.at[dst, :], sem.at[2, 1]))

    def unused_tile_copies(t):
        dst = pl.ds(pl.multiple_of(t * MOE_TM, MOE_TM), MOE_TM)
        return (pltpu.make_async_copy(zx_ref, xg_ref.at[dst, :], sem.at[2, 0]),
                pltpu.make_async_copy(zw_ref, wsg_ref.at[dst, :], sem.at[2, 1]))

    def run(step, fn):
        def body(g, carry):
            for cp in copies(step, g):
                fn(cp)
            return carry
        lax.fori_loop(0, nvalid_ref[step], body, 0)

    def run_fill(fn):
        for b in range(N_BUCKETS):
            def body(i, carry, b=b):
                for cp in fill_copies(b, i):
                    fn(cp)
                return carry
            lax.fori_loop(0, taillen_ref[b], body, 0)

        def tile_body(t, carry):
            for cp in unused_tile_copies(t):
                fn(cp)
            return carry
        lax.fori_loop(tail0_ref[N_BUCKETS], xg_ref.shape[0] // MOE_TM, tile_body, 0)

    @pl.when(s == 0)
    def _():
        zx_ref[...] = jnp.zeros_like(zx_ref)
        zw_ref[...] = jnp.zeros_like(zw_ref)
        run_fill(lambda cp: cp.start())

    run(s, lambda cp: cp.start())

    @pl.when(s > 0)
    def _():
        run(s - 1, lambda cp: cp.wait())

    @pl.when(s == nsteps - 1)
    def _():
        run(s, lambda cp: cp.wait())
        run_fill(lambda cp: cp.wait())


def _moe_sort(plan, hb, route, lstrict):
    n, d = hb.shape
    ts = MOE_TS
    rows = plan["n_tiles"] * MOE_TM
    return pl.pallas_call(
        _moe_sort_kernel,
        out_shape=(jax.ShapeDtypeStruct((rows, d), BF16), jax.ShapeDtypeStruct((rows, LANES), F32)),
        grid_spec=pltpu.PrefetchScalarGridSpec(
            num_scalar_prefetch=4,
            grid=(n // ts,),
            in_specs=[pl.BlockSpec((ts, d), lambda s, *_: (s, 0)), pl.BlockSpec((ts, LANES), lambda s, *_: (s, 0)),
                      pl.BlockSpec((ts, ts), lambda s, *_: (0, 0))],
            out_specs=(pl.BlockSpec(memory_space=pl.ANY), pl.BlockSpec(memory_space=pl.ANY)),
            scratch_shapes=[pltpu.VMEM((2, LROWS, d), BF16), pltpu.VMEM((2, LROWS, LANES), F32),
                            pltpu.VMEM((MOE_TM, d), BF16), pltpu.VMEM((MOE_TM, LANES), F32),
                            pltpu.SemaphoreType.DMA((3, 2))]),
        compiler_params=pltpu.CompilerParams(dimension_semantics=("arbitrary",), vmem_limit_bytes=VMEM_LIMIT),
        name="moe_sort",
    )(plan["gmap"], plan["nvalid"], plan["tail0"], plan["taillen"], hb, route, lstrict)


def _moe_expert_kernel(xt_ref, elo_ref, ehi_ref, valid_ref, x_ref, w_ref, wg0, wu0, wd0, wg1, wu1, wd1, o_ref):
    t = pl.program_id(0)

    @pl.when(valid_ref[t] > 0)
    def _():
        x = x_ref[...]
        w = w_ref[...]
        gates = [jnp.dot(x, wg[0].astype(BF16), preferred_element_type=F32) for wg in (wg0, wg1)]
        ups = [jnp.dot(x, wu[0].astype(BF16), preferred_element_type=F32) for wu in (wu0, wu1)]
        hes = [(_silu(gates[i]) * ups[i] * w[:, lane:lane + 1]).astype(BF16)
               for i, lane in enumerate((ROUTE_W_LO, ROUTE_W_HI))]
        o_ref[...] = (jnp.dot(hes[0], wd0[0].astype(BF16), preferred_element_type=F32)
                      + jnp.dot(hes[1], wd1[0].astype(BF16), preferred_element_type=F32)).astype(o_ref.dtype)

    @pl.when(valid_ref[t] == 0)
    def _():
        o_ref[...] = jnp.zeros_like(o_ref)


def _moe_experts(plan, xg, wsg, wg, wu, wd):
    rows, d = xg.shape
    tm = MOE_TM
    tok = lambda width: pl.BlockSpec((tm, width), lambda t, xt, elo, ehi, valid: (xt[t], 0))
    lo3 = lambda shape: pl.BlockSpec(shape, lambda t, xt, elo, ehi, valid: (elo[t], 0, 0))
    hi3 = lambda shape: pl.BlockSpec(shape, lambda t, xt, elo, ehi, valid: (ehi[t], 0, 0))
    return pl.pallas_call(
        _moe_expert_kernel,
        out_shape=jax.ShapeDtypeStruct((rows, d), BF16),
        grid_spec=pltpu.PrefetchScalarGridSpec(
            num_scalar_prefetch=4,
            grid=(rows // tm,),
            in_specs=[tok(d), tok(LANES),
                      lo3((1, d, D_EXPERT)), lo3((1, d, D_EXPERT)), lo3((1, D_EXPERT, d)),
                      hi3((1, d, D_EXPERT)), hi3((1, d, D_EXPERT)), hi3((1, D_EXPERT, d))],
            out_specs=pl.BlockSpec((tm, d), lambda t, *_: (t, 0))),
        compiler_params=pltpu.CompilerParams(dimension_semantics=("arbitrary",), vmem_limit_bytes=VMEM_LIMIT),
        name="moe_experts",
    )(plan["xtile"], plan["elo"], plan["ehi"], plan["valid"], xg, wsg, wg, wu, wd, wg, wu, wd)


def _moe_unsort_kernel(gmap_ref, og_ref, route_ref, h_ref, lstrict_ref, g_ref, b_ref, out_ref, ol_ref, sem):
    s = pl.program_id(0)
    nsteps = pl.num_programs(0)
    slot = s & 1
    ts = route_ref.shape[0]

    def gather(step, fn):
        sl = step & 1

        def body(g, carry):
            src = pl.ds(pl.multiple_of(gmap_ref[step * LGRAN + g] * GRAN, GRAN), GRAN)
            dst = pl.ds(pl.multiple_of(g * GRAN, GRAN), GRAN)
            fn(pltpu.make_async_copy(og_ref.at[src, :], ol_ref.at[sl, dst, :], sem.at[sl]))
            return carry
        lax.fori_loop(0, LGRAN, body, 0)

    @pl.when(s == 0)
    def _():
        gather(s, lambda cp: cp.start())

    @pl.when(s + 1 < nsteps)
    def _():
        gather(s + 1, lambda cp: cp.start())

    route = route_ref[...]
    bucket_col = route[:, ROUTE_BUCKET:ROUTE_BUCKET + 1].astype(jnp.int32)
    lane = lax.broadcasted_iota(jnp.int32, (ts, LANES), 1)
    ohf = jnp.where(lane == bucket_col, 1.0, 0.0)
    cnt = jnp.sum(ohf, axis=0, keepdims=True).astype(jnp.int32)
    pad = (((cnt + (GRAN - 1)) >> GRAN_SHIFT) << GRAN_SHIFT).astype(F32)
    r = lax.broadcasted_iota(jnp.int32, (LANES, LANES), 0)
    c = lax.broadcasted_iota(jnp.int32, (LANES, LANES), 1)
    loff = _dot(jnp.broadcast_to(pad, (8, LANES)), jnp.where(r < c, 1.0, 0.0))[0:1, :]
    rank = jnp.dot(lstrict_ref[...], ohf.astype(BF16), preferred_element_type=F32)
    dest = jnp.sum(ohf * (loff + rank), axis=1, keepdims=True).astype(jnp.int32)
    lrow = lax.broadcasted_iota(jnp.int32, (ts, LROWS), 1)
    perm_t = jnp.where(lrow == dest, 1.0, 0.0).astype(BF16)

    gather(s, lambda cp: cp.wait())
    ffn = jnp.dot(perm_t, ol_ref[slot], preferred_element_type=F32)
    out_ref[...] = _layer_norm(DEEPNORM_ALPHA * h_ref[...] + ffn, g_ref[...], b_ref[...])


def _moe_unsort(plan, og, route, hf, lstrict, g2, b2):
    n, d = hf.shape
    ts = MOE_TS
    row = pl.BlockSpec((1, d), lambda s, *_: (0, 0))
    return pl.pallas_call(
        _moe_unsort_kernel,
        out_shape=jax.ShapeDtypeStruct((n, d), F32),
        grid_spec=pltpu.PrefetchScalarGridSpec(
            num_scalar_prefetch=1,
            grid=(n // ts,),
            in_specs=[pl.BlockSpec(memory_space=pl.ANY), pl.BlockSpec((ts, LANES), lambda s, *_: (s, 0)),
                      pl.BlockSpec((ts, d), lambda s, *_: (s, 0)), pl.BlockSpec((ts, ts), lambda s, *_: (0, 0)),
                      row, row],
            out_specs=pl.BlockSpec((ts, d), lambda s, *_: (s, 0)),
            scratch_shapes=[pltpu.VMEM((2, LROWS, d), BF16), pltpu.SemaphoreType.DMA((2,))]),
        compiler_params=pltpu.CompilerParams(dimension_semantics=("arbitrary",), vmem_limit_bytes=VMEM_LIMIT),
        name="moe_unsort",
    )(plan["gmap_back"], og, route, hf, lstrict, g2, b2)


def _moe_plan(cnt_half, n):
    nsrc = n // MOE_TS
    i32 = jnp.int32
    cnt = cnt_half.reshape(nsrc, -1, LANES).sum(axis=1)[:, :N_BUCKETS].astype(i32)
    run_g = (cnt + GRAN - 1) // GRAN
    nvalid = run_g.sum(axis=1)
    loff_g = jnp.cumsum(run_g, axis=1) - run_g
    bucket_g = run_g.sum(axis=0)
    gpt = MOE_TM // GRAN
    btiles = (bucket_g + gpt - 1) // gpt
    tend = jnp.cumsum(btiles)
    tstart = tend - btiles
    gofs = tstart[None, :] * gpt + jnp.cumsum(run_g, axis=0) - run_g
    n_tiles = -(-(n + nsrc * N_BUCKETS * (GRAN - 1)) // MOE_TM) + N_BUCKETS + 1
    g = jnp.arange(LGRAN, dtype=i32)[None, :, None]
    in_run = (g >= loff_g[:, None, :]) & (g < (loff_g + run_g)[:, None, :])
    gmap = jnp.arange(LGRAN, dtype=i32)[None, :] + jnp.sum(jnp.where(in_run, (gofs - loff_g)[:, None, :], 0), axis=2)
    is_valid = jnp.arange(LGRAN, dtype=i32)[None, :] < nvalid[:, None]
    zero_gran = (n_tiles - 1) * gpt
    t = jnp.arange(n_tiles, dtype=i32)
    tb = jnp.minimum(jnp.sum(t[:, None] >= tend[None, :], axis=1), N_BUCKETS - 1)
    valid = (t < tend[-1]).astype(i32)
    pairs = [(a, b) for a in range(EXPERTS_PER_GROUP) for b in range(a + 1, EXPERTS_PER_GROUP)]
    pidx = tb % N_PAIRS
    pair_a = sum(jnp.where(pidx == i, a, 0) for i, (a, _) in enumerate(pairs))
    pair_b = sum(jnp.where(pidx == i, b, 0) for i, (_, b) in enumerate(pairs))
    grp = tb // N_PAIRS
    return {
        "n_tiles": n_tiles,
        "gmap": jnp.where(is_valid, gmap, 0).reshape(-1).astype(i32),
        "gmap_back": jnp.where(is_valid, gmap, zero_gran).reshape(-1).astype(i32),
        "nvalid": nvalid.astype(i32),
        "tail0": jnp.concatenate([tstart * gpt + bucket_g, tend[-1:]]).astype(i32),
        "taillen": (btiles * gpt - bucket_g).astype(i32),
        "xtile": jnp.where(valid > 0, t, 0).astype(i32),
        "elo": (grp * EXPERTS_PER_GROUP + pair_a).astype(i32),
        "ehi": (grp * EXPERTS_PER_GROUP + pair_b).astype(i32),
        "valid": valid,
    }


def _pad_lanes(a, lane0=0):
    return jnp.zeros((1, LANES), F32).at[0, lane0:lane0 + a.shape[0]].set(a.astype(F32))


def _rope_tables(seq):
    half = HEAD_DIM // 2
    inv_freq = ROPE_THETA ** (-jnp.arange(half, dtype=F32) / half)
    ang = jnp.arange(seq).astype(F32)[:, None] * inv_freq[None, :]
    cos, sin = jnp.cos(ang), jnp.sin(ang)
    return jnp.concatenate([cos, cos], axis=-1), jnp.concatenate([-sin, sin], axis=-1)


def _layer(x, w_in, conv_w, a_log, dt_bias, dn_norm_w, w_out, ln1_g, ln1_b, router_w1, router_b1,
           router_w2, router_b2, w_gate, w_up, w_down, ln2_g, ln2_b):
    bsz, seq, d = x.shape
    n = bsz * seq
    x2 = x.reshape(n, d)

    o_z, o_b, o_mb = 3 * D_DN, 4 * D_DN, 4 * D_DN + 2 * N_HEADS_DN
    w_ba = jnp.pad(w_in[:, o_b:o_mb], ((0, 0), (0, LANES - 2 * N_HEADS_DN)))
    w_all = jnp.concatenate([w_in[:, :o_z], w_in[:, o_z:o_b], w_ba, w_in[:, o_mb:]], axis=1).astype(BF16)

    tm = min(512, n)
    dn_qkv, z, ba, mb_qkv = _in_proj(x2, w_all, conv_w, tm, seq)

    y_dn = _deltanet(dn_qkv.reshape(bsz, seq, 3 * D_DN), z.reshape(bsz, seq, D_DN), ba.reshape(bsz, seq, LANES),
                     _pad_lanes(a_log, N_HEADS_DN), _pad_lanes(dt_bias, N_HEADS_DN),
                     dn_norm_w.astype(F32).reshape(1, HEAD_DIM))

    cos_t, sin_t = _rope_tables(seq)
    q_r, k_r, vt, sel = _moba_prep(mb_qkv.reshape(bsz, seq, 3 * D_MOBA), cos_t, sin_t)
    y_mb = _moba_attn(q_r, k_r, vt, sel)

    rw = jnp.concatenate([router_w1, jnp.transpose(router_w2, (1, 0, 2)).reshape(d, N_EXPERTS)], axis=1)
    rw = jnp.pad(rw, ((0, 0), (0, LANES - rw.shape[1])))
    rb = _pad_lanes(jnp.concatenate([router_b1, router_b2.reshape(-1)]))
    hf, hb, route, cnt = _mix_route(y_dn.reshape(n, D_DN), y_mb.reshape(n, D_MOBA), x2, w_out.astype(BF16),
                                    ln1_g.reshape(1, d), ln1_b.reshape(1, d), rw, rb, tm)

    plan = _moe_plan(cnt, n)
    idx = jnp.arange(MOE_TS, dtype=jnp.int32)
    lstrict = (idx[None, :] < idx[:, None]).astype(BF16)
    xg, wsg = _moe_sort(plan, hb, route, lstrict)
    og = _moe_experts(plan, xg, wsg, w_gate, w_up, w_down)
    out = _moe_unsort(plan, og, route, hf, lstrict, ln2_g.reshape(1, d), ln2_b.reshape(1, d))
    return out.reshape(bsz, seq, d)


def kernel(x, w_in, conv_w, a_log, dt_bias, dn_norm_w, w_out, ln1_g, ln1_b, router_w1, router_b1, router_w2, router_b2, expert_w_gate, expert_w_up, expert_w_down, ln2_g, ln2_b):
    for l in range(DEPTH):
        x = _layer(x, w_in[l], conv_w[l], a_log[l], dt_bias[l], dn_norm_w[l], w_out[l], ln1_g[l], ln1_b[l],
                   router_w1[l], router_b1[l], router_w2[l], router_b2[l], expert_w_gate[l], expert_w_up[l],
                   expert_w_down[l], ln2_g[l], ln2_b[l])
    return x
```

```python
import functools

import jax
import jax.numpy as jnp
from jax import lax
from jax.experimental import pallas as pl
from jax.experimental.pallas import tpu as pltpu

F32 = jnp.float32
BF16 = jnp.bfloat16

HEAD_DIM = 128
N_HEADS_DN = 4
N_HEADS_MOBA = 4
D_DN = N_HEADS_DN * HEAD_DIM
D_MOBA = N_HEADS_MOBA * HEAD_DIM
CONV_K = 4
DN_CHUNK = 64
MOBA_BLOCK = 256
MOBA_TOPK = 3
ROPE_THETA = 10000.0
N_GROUPS = 4
EXPERTS_PER_GROUP = 4
N_EXPERTS = N_GROUPS * EXPERTS_PER_GROUP
D_EXPERT = 256
LN_EPS = 1e-5
RMS_EPS = 1e-6
L2_EPS = 1e-6
NEG_INF = -1e30
LOG2E = 1.4426950408889634
DEPTH = 1
DEEPNORM_ALPHA = (2 * DEPTH) ** 0.25

LANES = 128
DN_TILE = 256
DN_HEADS_PER_STEP = 4
GATE_LANE0 = N_GROUPS
N_PAIRS = EXPERTS_PER_GROUP * (EXPERTS_PER_GROUP - 1) // 2
N_BUCKETS = N_GROUPS * N_PAIRS
ROUTE_BUCKET, ROUTE_W_LO, ROUTE_W_HI = 0, 1, 2
MOE_TS = 512
MOE_TM = 512
GRAN = 16
GRAN_SHIFT = 4
LROWS = -(-(MOE_TS + N_BUCKETS * (GRAN - 1)) // LANES) * LANES
LGRAN = LROWS // GRAN
VMEM_LIMIT = 48 * 1024 * 1024


def _dot(a, b):
    return jnp.dot(a.astype(BF16), b.astype(BF16), preferred_element_type=F32)


def _dot_nt(a, b):
    return lax.dot_general(a.astype(BF16), b.astype(BF16), (((1,), (1,)), ((), ())),
                           preferred_element_type=F32)


def _split2(a):
    hi = a.astype(BF16)
    lo = (a - hi.astype(F32)).astype(BF16)
    return hi, lo


def _split3(a):
    hi = a.astype(BF16)
    r = a - hi.astype(F32)
    mid = r.astype(BF16)
    lo = (r - mid.astype(F32)).astype(BF16)
    return hi, mid, lo


def _dot3(a, b):
    ah, al = _split2(a)
    bh, bl = _split2(b)
    return (jnp.dot(ah, bh, preferred_element_type=F32) + jnp.dot(ah, bl, preferred_element_type=F32)
            + jnp.dot(al, bh, preferred_element_type=F32))


def _dot3_nt(a, b):
    ah, al = _split2(a)
    bh, bl = _split2(b)
    dn = (((1,), (1,)), ((), ()))
    return (lax.dot_general(ah, bh, dn, preferred_element_type=F32)
            + lax.dot_general(ah, bl, dn, preferred_element_type=F32)
            + lax.dot_general(al, bh, dn, preferred_element_type=F32))


def _dot_exact_lhs(a_bf16, b):
    bh, bm, bl = _split3(b)
    return (jnp.dot(a_bf16, bh, preferred_element_type=F32) + jnp.dot(a_bf16, bm, preferred_element_type=F32)
            + jnp.dot(a_bf16, bl, preferred_element_type=F32))


def _silu(x):
    return x * jax.nn.sigmoid(x)


def _softplus(x):
    return jnp.maximum(x, 0.0) + jnp.log1p(jnp.exp(-jnp.abs(x)))


def _layer_norm(t, g, b):
    mu = jnp.mean(t, axis=-1, keepdims=True)
    d = t - mu
    var = jnp.mean(d * d, axis=-1, keepdims=True)
    return d * lax.rsqrt(var + LN_EPS) * g + b


def _lane_pick(x, lane):
    ids = lax.broadcasted_iota(jnp.int32, x.shape, 1)
    return jnp.sum(jnp.where(ids == lane, x, 0.0), axis=1, keepdims=True)


def _in_proj_kernel(x_ref, w_ref, cw_ref, dn_ref, z_ref, ba_ref, mb_ref, cb_ref, *, tiles_per_seq):
    i = pl.program_id(0)
    tm = x_ref.shape[0]
    o0 = 3 * D_DN
    o1 = o0 + D_DN
    o2 = o1 + LANES

    @pl.when(i % tiles_per_seq == 0)
    def _():
        cb_ref[0:8, :] = jnp.zeros((8, o0), F32)

    xb = x_ref[...].astype(BF16)
    u = jnp.dot(xb, w_ref[:, 0:o0], preferred_element_type=F32)
    z_ref[...] = _silu(jnp.dot(xb, w_ref[:, o0:o1], preferred_element_type=F32))
    ba_ref[...] = jnp.dot(xb, w_ref[:, o1:o2], preferred_element_type=F32)
    mb_ref[...] = jnp.dot(xb, w_ref[:, o2:o2 + 3 * D_MOBA], preferred_element_type=F32).astype(mb_ref.dtype)

    cb_ref[8:8 + tm, :] = u
    acc = cw_ref[CONV_K - 1:CONV_K, :] * u
    for s in range(1, CONV_K):
        acc = acc + cw_ref[CONV_K - 1 - s:CONV_K - s, :] * cb_ref[8 - s:8 - s + tm, :]
    cb_ref[0:8, :] = u[tm - 8:tm, :]
    qkv = _silu(acc)

    outs = []
    for h in range(2 * N_HEADS_DN):
        t = qkv[:, h * HEAD_DIM:(h + 1) * HEAD_DIM]
        t = t * lax.rsqrt(jnp.sum(t * t, axis=-1, keepdims=True) + L2_EPS)
        outs.append(t * (HEAD_DIM ** -0.5) if h < N_HEADS_DN else t)
    outs.append(qkv[:, 2 * D_DN:3 * D_DN])
    dn_ref[...] = jnp.concatenate(outs, axis=1)


def _in_proj(x2, w_all, conv_w, tm, seq):
    n, d = x2.shape
    wc = w_all.shape[1]
    return pl.pallas_call(
        functools.partial(_in_proj_kernel, tiles_per_seq=seq // tm),
        out_shape=(jax.ShapeDtypeStruct((n, 3 * D_DN), F32), jax.ShapeDtypeStruct((n, D_DN), F32),
                   jax.ShapeDtypeStruct((n, LANES), F32), jax.ShapeDtypeStruct((n, 3 * D_MOBA), BF16)),
        grid=(n // tm,),
        in_specs=[pl.BlockSpec((tm, d), lambda i: (i, 0)), pl.BlockSpec((d, wc), lambda i: (0, 0)),
                  pl.BlockSpec((CONV_K, 3 * D_DN), lambda i: (0, 0))],
        out_specs=(pl.BlockSpec((tm, 3 * D_DN), lambda i: (i, 0)), pl.BlockSpec((tm, D_DN), lambda i: (i, 0)),
                   pl.BlockSpec((tm, LANES), lambda i: (i, 0)), pl.BlockSpec((tm, 3 * D_MOBA), lambda i: (i, 0))),
        scratch_shapes=[pltpu.VMEM((8 + tm, 3 * D_DN), F32)],
        compiler_params=pltpu.CompilerParams(dimension_semantics=("arbitrary",), vmem_limit_bytes=VMEM_LIMIT),
        name="in_proj",
    )(x2, w_all, conv_w)


def _deltanet_kernel(q_ref, k_ref, v_ref, z_ref, ba_ref, alog_ref, dtb_ref, normw_ref, y_ref,
                     s_ref, wq_s, u_s, qk_s, kdt_s, egl_s, *, hb):
    hg = pl.program_id(1)
    t = pl.program_id(2)
    tt = DN_TILE
    nchunk = tt // DN_CHUNK
    hs = range(hb)

    @pl.when(t == 0)
    def _():
        for ref in (s_ref, wq_s, u_s, qk_s, kdt_s, egl_s):
            ref[...] = jnp.zeros_like(ref)

    rd = t & 1
    wr = 1 - rd
    state = [s_ref[h] for h in hs]
    outs = [[] for _ in hs]

    pend = {}

    def chain_a(c):
        pend["r"] = [jnp.dot(wq_s[rd, h, c], state[h].astype(BF16), preferred_element_type=F32) for h in hs]

    def chain_b(c):
        lo, hi = c * DN_CHUNK, (c + 1) * DN_CHUNK
        r = pend["r"]
        vz = []
        for h in hs:
            parts = []
            if lo > 0:
                parts.append(jnp.zeros((lo, HEAD_DIM), F32))
            parts.append(u_s[rd, h, lo:hi, :] - r[h][0:DN_CHUNK, :])
            if hi < tt:
                parts.append(jnp.zeros((tt - hi, HEAD_DIM), F32))
            vz.append(jnp.concatenate(parts, axis=0).astype(BF16))
        for h in hs:
            outs[h].append(r[h][DN_CHUNK:2 * DN_CHUNK, :]
                           + jnp.dot(qk_s[rd, h, lo:hi, :], vz[h], preferred_element_type=F32))
        for h in hs:
            state[h] = (state[h] * egl_s[rd, h, 8 * c:8 * c + 1, :]
                        + jnp.dot(kdt_s[rd, h], vz[h], preferred_element_type=F32))

    chain_a(0)

    q_all = q_ref[0]
    k_all = k_ref[0]
    v_all = v_ref[0]

    ba = ba_ref[0]
    beta_all = jax.nn.sigmoid(ba)
    g_all = -jnp.exp(alog_ref[...]) * _softplus(ba + dtb_ref[...])

    row = lax.broadcasted_iota(jnp.int32, (tt, tt), 0)
    col = lax.broadcasted_iota(jnp.int32, (tt, tt), 1)
    same = (row >> 6) == (col >> 6)
    incl = same & (row >= col)
    strict = same & (row > col)

    gc_all = _dot_exact_lhs(incl.astype(BF16), g_all)
    gct = jnp.transpose(gc_all)
    sub = lax.broadcasted_iota(jnp.int32, gct.shape, 0)
    sls = [slice(hh * HEAD_DIM, (hh + 1) * HEAD_DIM) for hh in hs]
    heads = [hg * hb + hh for hh in hs]
    q = [q_all[:, sl] for sl in sls]
    k = [k_all[:, sl] for sl in sls]
    v = [v_all[:, sl] for sl in sls]
    beta = [_lane_pick(beta_all, h) for h in heads]
    gcc = [_lane_pick(gc_all, h + N_HEADS_DN) for h in heads]
    gcr = [jnp.sum(jnp.where(sub == h + N_HEADS_DN, gct, 0.0), axis=0, keepdims=True) for h in heads]
    chain_b(0)

    decay = [jnp.where(incl, jnp.exp(jnp.where(incl, gcc[h] - gcr[h], 0.0)), 0.0) for h in hs]
    kb = [k[h] * beta[h] for h in hs]
    vb = [v[h] * beta[h] for h in hs]
    a_mat = [jnp.where(strict, _dot_nt(kb[h], k[h]) * decay[h], 0.0) for h in hs]
    chain_a(1)
    qk = [_dot_nt(q[h], k[h]) * decay[h] for h in hs]
    eye = (row == col).astype(F32)
    d8 = (row >> 3) == (col >> 3)
    a8 = [jnp.where(d8, a, 0.0) for a in a_mat]
    chain_b(1)
    a8_2 = [_dot(a, a) for a in a8]
    chain_a(2)
    a8_4 = [_dot(a, a) for a in a8_2]
    chain_b(2)
    x = [_dot(eye - a, eye + a2) for a, a2 in zip(a8, a8_2)]
    chain_a(3)
    x = [_dot(xi, eye + a4) for xi, a4 in zip(x, a8_4)]
    chain_b(3)
    s = 8
    while s < DN_CHUNK:
        sh = s.bit_length() - 1
        off = ((row >> (sh + 1)) == (col >> (sh + 1))) & ((row >> sh) != (col >> sh))
        y = [_dot(jnp.where(off, a, 0.0), xi) for a, xi in zip(a_mat, x)]
        x = [xi - _dot(xi, yi) for xi, yi in zip(x, y)]
        s *= 2
    tinv = x
    eg = [jnp.exp(g) for g in gcc]
    wu = [_dot(tinv[h], jnp.concatenate([kb[h] * eg[h], vb[h]], axis=1)) for h in hs]
    qd = [q[h] * eg[h] for h in hs]
    gl_rows = [[g[(c + 1) * DN_CHUNK - 1:(c + 1) * DN_CHUNK, :] for c in range(nchunk)] for g in gcc]
    gl_col = [jnp.concatenate([jnp.broadcast_to(g, (DN_CHUNK, 1)) for g in rows], axis=0) for rows in gl_rows]
    kdt = [jnp.transpose(k[h] * jnp.exp(gl_col[h] - gcc[h])) for h in hs]

    ys = []
    for h in hs:
        o = jnp.concatenate(outs[h], axis=0)
        o = o * lax.rsqrt(jnp.mean(o * o, axis=-1, keepdims=True) + RMS_EPS) * normw_ref[...]
        ys.append(o * z_ref[0, :, sls[h]])
    y_ref[0] = jnp.concatenate(ys, axis=1).astype(y_ref.dtype)
    s_ref[...] = jnp.stack(state, axis=0)

    for h in hs:
        for c in range(nchunk):
            lo, hi = c * DN_CHUNK, (c + 1) * DN_CHUNK
            wq_s[wr, h, c] = jnp.concatenate([wu[h][lo:hi, 0:HEAD_DIM], qd[h][lo:hi, :]], axis=0).astype(BF16)
            egl_s[wr, h, 8 * c:8 * c + 8, :] = jnp.broadcast_to(jnp.exp(gl_rows[h][c]), (8, HEAD_DIM))
        u_s[wr, h] = wu[h][:, HEAD_DIM:2 * HEAD_DIM]
        qk_s[wr, h] = qk[h].astype(BF16)
        kdt_s[wr, h] = kdt[h].astype(BF16)


def _deltanet(dn_qkv, z, ba, alog_row, dtb_row, normw_row):
    bsz, seq, _ = dn_qkv.shape
    tt = DN_TILE
    nt = seq // tt
    hb = DN_HEADS_PER_STEP
    ng = N_HEADS_DN // hb
    w = hb * HEAD_DIM
    nchunk = tt // DN_CHUNK

    def cur_spec(off, width):
        return pl.BlockSpec((1, tt, width), lambda b, g, t: (b, jnp.minimum(t, nt - 1), g + off))

    prev_spec = pl.BlockSpec((1, tt, w), lambda b, g, t: (b, jnp.maximum(t - 1, 0), g))
    row_spec = pl.BlockSpec((1, LANES), lambda b, g, t: (0, 0))
    return pl.pallas_call(
        functools.partial(_deltanet_kernel, hb=hb),
        out_shape=jax.ShapeDtypeStruct((bsz, seq, D_DN), BF16),
        grid=(bsz, ng, nt + 1),
        in_specs=[cur_spec(0, w), cur_spec(ng, w), cur_spec(2 * ng, w), prev_spec,
                  pl.BlockSpec((1, tt, LANES), lambda b, g, t: (b, jnp.minimum(t, nt - 1), 0)),
                  row_spec, row_spec, row_spec],
        out_specs=prev_spec,
        scratch_shapes=[pltpu.VMEM((hb, HEAD_DIM, HEAD_DIM), F32),
                        pltpu.VMEM((2, hb, nchunk, 2 * DN_CHUNK, HEAD_DIM), BF16),
                        pltpu.VMEM((2, hb, tt, HEAD_DIM), F32),
                        pltpu.VMEM((2, hb, tt, tt), BF16),
                        pltpu.VMEM((2, hb, HEAD_DIM, tt), BF16),
                        pltpu.VMEM((2, hb, 8 * nchunk, HEAD_DIM), F32)],
        compiler_params=pltpu.CompilerParams(dimension_semantics=("parallel", "parallel", "arbitrary"),
                                             vmem_limit_bytes=VMEM_LIMIT),
        name="deltanet",
    )(dn_qkv, dn_qkv, dn_qkv, z, ba, alog_row, dtb_row, normw_row)


def _moba_prep_kernel(x_ref, cos_ref, sin_ref, q_ref, k_ref, vt_ref, sel_ref, km_ref, *, nb, topk):
    j = pl.program_id(1)

    @pl.when(j == 0)
    def _():
        km_ref[...] = jnp.zeros_like(km_ref)

    cos = cos_ref[...]
    sin = sin_ref[...]
    half = HEAD_DIM // 2
    blk = lax.broadcasted_iota(jnp.int32, (nb, MOBA_BLOCK), 0)
    kmeans = []
    for h in range(N_HEADS_MOBA):
        qh = x_ref[0, :, h * HEAD_DIM:(h + 1) * HEAD_DIM].astype(F32)
        kh = x_ref[0, :, D_MOBA + h * HEAD_DIM:D_MOBA + (h + 1) * HEAD_DIM].astype(F32)
        qr = (qh * cos + pltpu.roll(qh, half, 1) * sin) * (HEAD_DIM ** -0.5)
        kr = kh * cos + pltpu.roll(kh, half, 1) * sin
        q_ref[0, :, h * HEAD_DIM:(h + 1) * HEAD_DIM] = (qr * LOG2E).astype(q_ref.dtype)
        k_ref[0, :, h * HEAD_DIM:(h + 1) * HEAD_DIM] = kr.astype(k_ref.dtype)
        kmeans.append(jnp.mean(kr, axis=0, keepdims=True))

        gate = _dot3_nt(km_ref[:, h * HEAD_DIM:(h + 1) * HEAD_DIM], qr)
        gate = jnp.where(blk < j, gate, NEG_INF)
        rank = jnp.zeros(gate.shape, F32)
        for m in range(nb):
            gm = gate[m:m + 1, :]
            ahead = (gm > gate) | ((gm == gate) & (blk > m))
            rank = rank + jnp.where(ahead, 1.0, 0.0)
        sel = (blk < j) & (rank < topk)
        sel_ref[0, 0, h * nb:(h + 1) * nb, :] = jnp.where(sel, 1.0, 0.0)

    km_ref[pl.ds(j, 1), :] = jnp.concatenate(kmeans, axis=1)
    vt_ref[0, 0] = jnp.transpose(x_ref[0, :, 2 * D_MOBA:3 * D_MOBA].astype(F32)).astype(vt_ref.dtype)


def _moba_prep(mb_qkv, cos_t, sin_t):
    bsz, seq, _ = mb_qkv.shape
    nb = seq // MOBA_BLOCK
    topk = min(MOBA_TOPK, nb)
    kern = functools.partial(_moba_prep_kernel, nb=nb, topk=topk)
    tok_spec = pl.BlockSpec((1, MOBA_BLOCK, D_MOBA), lambda b, j: (b, j, 0))
    tab_spec = pl.BlockSpec((MOBA_BLOCK, HEAD_DIM), lambda b, j: (j, 0))
    return pl.pallas_call(
        kern,
        out_shape=(jax.ShapeDtypeStruct((bsz, seq, D_MOBA), BF16), jax.ShapeDtypeStruct((bsz, seq, D_MOBA), BF16),
                   jax.ShapeDtypeStruct((bsz, nb, D_MOBA, MOBA_BLOCK), BF16),
                   jax.ShapeDtypeStruct((bsz, nb, N_HEADS_MOBA * nb, MOBA_BLOCK), F32)),
        grid=(bsz, nb),
        in_specs=[pl.BlockSpec((1, MOBA_BLOCK, 3 * D_MOBA), lambda b, j: (b, j, 0)), tab_spec, tab_spec],
        out_specs=(tok_spec, tok_spec,
                   pl.BlockSpec((1, 1, D_MOBA, MOBA_BLOCK), lambda b, j: (b, j, 0, 0)),
                   pl.BlockSpec((1, 1, N_HEADS_MOBA * nb, MOBA_BLOCK), lambda b, j: (b, j, 0, 0))),
        scratch_shapes=[pltpu.VMEM((nb, D_MOBA), F32)],
        compiler_params=pltpu.CompilerParams(dimension_semantics=("parallel", "arbitrary"),
                                             vmem_limit_bytes=VMEM_LIMIT),
        name="moba_prep",
    )(mb_qkv, cos_t, sin_t)


def _moba_attn_kernel(q_ref, k_ref, vt_ref, sel_ref, o_ref, acc_ref, *, nb):
    j = pl.program_id(1)
    blk = MOBA_BLOCK
    cw = 2 * blk
    nh = N_HEADS_MOBA
    nc = (j + 1) // 2
    dn = (((1,), (1,)), ((), ()))
    hsl = [slice(h * HEAD_DIM, (h + 1) * HEAD_DIM) for h in range(nh)]
    qs = [q_ref[0, :, hsl[h]] for h in range(nh)]

    ki = lax.broadcasted_iota(jnp.int32, (blk, blk), 0)
    qi = lax.broadcasted_iota(jnp.int32, (blk, blk), 1)
    own = pl.ds(pl.multiple_of(j * blk, blk), blk)
    s_own = [jnp.where(ki <= qi, lax.dot_general(k_ref[0, own, hsl[h]], qs[h], dn, preferred_element_type=F32),
                       NEG_INF) for h in range(nh)]

    def scores(c, h):
        kc = k_ref[0, pl.ds(pl.multiple_of(c * cw, cw), cw), hsl[h]]
        s = lax.dot_general(kc, qs[h], dn, preferred_element_type=F32)
        parts = []
        for i in range(2):
            selrow = sel_ref[0, 0, pl.ds(h * nb + 2 * c + i, 1), :]
            parts.append(jnp.where(selrow > 0.5, s[i * blk:(i + 1) * blk, :], NEG_INF))
        return jnp.concatenate(parts, axis=0)

    ones8 = jnp.ones((8, cw), BF16)
    ms, ls = [], []
    for h in range(nh):
        m = jnp.max(s_own[h], axis=0, keepdims=True)
        pb = jnp.exp2(s_own[h] - m).astype(BF16)
        ms.append(m)
        ls.append(jnp.dot(ones8[:, 0:blk], pb, preferred_element_type=F32)[0:1, :])
        acc_ref[h] = jnp.dot(vt_ref[0, j, hsl[h], :], pb, preferred_element_type=F32)

    def softmax_pv(c, h, s, m, l):
        m_new = jnp.maximum(m, jnp.max(s, axis=0, keepdims=True))
        alpha = jnp.exp2(m - m_new)
        pb = jnp.exp2(s - m_new).astype(BF16)
        l = alpha * l + jnp.dot(ones8, pb, preferred_element_type=F32)[0:1, :]
        pv = (jnp.dot(vt_ref[0, 2 * c, hsl[h], :], pb[0:blk, :], preferred_element_type=F32)
              + jnp.dot(vt_ref[0, 2 * c + 1, hsl[h], :], pb[blk:cw, :], preferred_element_type=F32))
        acc_ref[h] = acc_ref[h] * alpha + pv
        return m_new, l

    def body(pair, carry):
        ms, ls = (list(t) for t in carry)
        c0 = 2 * pair
        s0 = [scores(c0, h) for h in range(nh)]
        s1 = []
        for h in range(nh):
            s1.append(scores(c0 + 1, h))
            ms[h], ls[h] = softmax_pv(c0, h, s0[h], ms[h], ls[h])
        for h in range(nh):
            ms[h], ls[h] = softmax_pv(c0 + 1, h, s1[h], ms[h], ls[h])
        return tuple(ms), tuple(ls)

    ms, ls = lax.fori_loop(0, (nc + 1) // 2, body, (tuple(ms), tuple(ls)))
    o_ref[0] = jnp.concatenate([jnp.transpose(acc_ref[h] / ls[h]) for h in range(nh)],
                               axis=1).astype(o_ref.dtype)


def _moba_attn(q_r, k_r, vt, sel):
    bsz, seq, _ = q_r.shape
    nb = seq // MOBA_BLOCK
    tok_spec = pl.BlockSpec((1, MOBA_BLOCK, D_MOBA), lambda b, j: (b, j, 0))
    return pl.pallas_call(
        functools.partial(_moba_attn_kernel, nb=nb),
        out_shape=jax.ShapeDtypeStruct((bsz, seq, D_MOBA), BF16),
        grid=(bsz, nb),
        in_specs=[tok_spec,
                  pl.BlockSpec((1, seq, D_MOBA), lambda b, j: (b, 0, 0)),
                  pl.BlockSpec((1, nb, D_MOBA, MOBA_BLOCK), lambda b, j: (b, 0, 0, 0)),
                  pl.BlockSpec((1, 1, N_HEADS_MOBA * nb, MOBA_BLOCK), lambda b, j: (b, j, 0, 0))],
        out_specs=tok_spec,
        scratch_shapes=[pltpu.VMEM((N_HEADS_MOBA, HEAD_DIM, MOBA_BLOCK), F32)],
        compiler_params=pltpu.CompilerParams(dimension_semantics=("parallel", "arbitrary"),
                                             vmem_limit_bytes=VMEM_LIMIT),
        name="moba_attn",
    )(q_r, k_r, vt, sel)


def _route_record(logits):
    lane = lax.broadcasted_iota(jnp.int32, logits.shape, 1)
    big = jnp.int32(LANES)

    def first_lane(mask):
        return jnp.min(jnp.where(mask, lane, big), axis=1, keepdims=True)

    is_g = lane < N_GROUPS
    m1 = jnp.max(jnp.where(is_g, logits, NEG_INF), axis=1, keepdims=True)
    s1 = jnp.sum(jnp.where(is_g, jnp.exp(logits - m1), 0.0), axis=1, keepdims=True)
    pg = 1.0 / s1
    gsel = first_lane(is_g & (logits == m1))

    in_grp = (lane >= GATE_LANE0) & (((lane - GATE_LANE0) >> 2) == gsel) & (lane < GATE_LANE0 + N_EXPERTS)
    m2 = jnp.max(jnp.where(in_grp, logits, NEG_INF), axis=1, keepdims=True)
    s2 = jnp.sum(jnp.where(in_grp, jnp.exp(logits - m2), 0.0), axis=1, keepdims=True)
    e1 = first_lane(in_grp & (logits == m2))
    rest = in_grp & (lane != e1)
    m2b = jnp.max(jnp.where(rest, logits, NEG_INF), axis=1, keepdims=True)
    e2 = first_lane(rest & (logits == m2b))
    pe1 = 1.0 / s2
    pe2 = jnp.exp(m2b - m2) / s2
    tot = pe1 + pe2
    w1 = pg * (pe1 / tot)
    w2 = pg * (pe2 / tot)
    first_lo = e1 < e2
    lo = jnp.minimum(e1, e2)
    hi = jnp.maximum(e1, e2)
    a = (lo - GATE_LANE0) & (EXPERTS_PER_GROUP - 1)
    b = (hi - GATE_LANE0) & (EXPERTS_PER_GROUP - 1)
    bucket = gsel * N_PAIRS + ((a * (2 * EXPERTS_PER_GROUP - 1 - a)) >> 1) + (b - a - 1)
    record = jnp.where(lane == ROUTE_BUCKET, bucket.astype(F32),
                       jnp.where(lane == ROUTE_W_LO, jnp.where(first_lo, w1, w2),
                                 jnp.where(lane == ROUTE_W_HI, jnp.where(first_lo, w2, w1), 0.0)))
    return record, jnp.sum(jnp.where(lane == bucket, 1.0, 0.0), axis=0, keepdims=True)


def _mix_route_kernel(ydn_ref, ymb_ref, x_ref, wo_ref, g_ref, b_ref, rw_ref, rb_ref, h_ref, hb_ref, route_ref,
                      cnt_ref):
    nsub = cnt_ref.shape[0]
    rows = [slice(i * MOE_TS, (i + 1) * MOE_TS) for i in range(nsub)]
    wh, wl = _split2(rw_ref[...])
    wcat = jnp.concatenate([wh, wl], axis=1)

    mixes = [jnp.dot(jnp.concatenate([ydn_ref[r, :], ymb_ref[r, :]], axis=1), wo_ref[...],
                     preferred_element_type=F32) for r in rows]
    logits = []
    for r, mix in zip(rows, mixes):
        hval = _layer_norm(DEEPNORM_ALPHA * x_ref[r, :] + mix, g_ref[...], b_ref[...])
        h_ref[r, :] = hval
        hb_ref[r, :] = hval.astype(BF16)
        hh, hl = _split2(hval)
        both = jnp.dot(hh, wcat, preferred_element_type=F32)
        logits.append(both[:, 0:LANES] + both[:, LANES:2 * LANES]
                      + jnp.dot(hl, wh, preferred_element_type=F32) + rb_ref[...])
    for i, r in enumerate(rows):
        record, counts = _route_record(logits[i])
        route_ref[r, :] = record
        cnt_ref[i] = counts


def _mix_route(y_dn, y_mb, x2, wo, g1, b1, rw, rb, tm):
    n, d = x2.shape
    nsub = tm // MOE_TS
    row = lambda w: pl.BlockSpec((1, w), lambda i: (0, 0))
    return pl.pallas_call(
        _mix_route_kernel,
        out_shape=(jax.ShapeDtypeStruct((n, d), F32), jax.ShapeDtypeStruct((n, d), BF16),
                   jax.ShapeDtypeStruct((n, LANES), F32), jax.ShapeDtypeStruct((n // MOE_TS, 1, LANES), F32)),
        grid=(n // tm,),
        in_specs=[pl.BlockSpec((tm, D_DN), lambda i: (i, 0)), pl.BlockSpec((tm, D_MOBA), lambda i: (i, 0)),
                  pl.BlockSpec((tm, d), lambda i: (i, 0)), pl.BlockSpec((D_DN + D_MOBA, d), lambda i: (0, 0)),
                  row(d), row(d), pl.BlockSpec((d, LANES), lambda i: (0, 0)), row(LANES)],
        out_specs=(pl.BlockSpec((tm, d), lambda i: (i, 0)), pl.BlockSpec((tm, d), lambda i: (i, 0)),
                   pl.BlockSpec((tm, LANES), lambda i: (i, 0)), pl.BlockSpec((nsub, 1, LANES), lambda i: (i, 0, 0))),
        compiler_params=pltpu.CompilerParams(dimension_semantics=("parallel",), vmem_limit_bytes=VMEM_LIMIT),
        name="mix_route",
    )(y_dn, y_mb, x2, wo, g1, b1, rw, rb)


def _bucket_offsets_col(ohf):
    cnt = jnp.sum(ohf, axis=1, keepdims=True).astype(jnp.int32)
    pad = (((cnt + (GRAN - 1)) >> GRAN_SHIFT) << GRAN_SHIFT).astype(F32)
    r = lax.broadcasted_iota(jnp.int32, (LANES, LANES), 0)
    c = lax.broadcasted_iota(jnp.int32, (LANES, LANES), 1)
    before = jnp.where(c < r, 1.0, 0.0)
    return _dot(before, jnp.broadcast_to(pad, (LANES, LANES)))[:, 0:1]


def _moe_sort_kernel(gmap_ref, nvalid_ref, tail0_ref, taillen_ref, hb_ref, route_ref, lstrict_ref,
                     xg_ref, wsg_ref, xs_ref, ws_ref, zx_ref, zw_ref, sem):
    s = pl.program_id(0)
    nsteps = pl.num_programs(0)
    slot = s & 1
    ts = route_ref.shape[0]
    route = route_ref[...]
    rt = jnp.transpose(route)
    bucket_row = rt[ROUTE_BUCKET:ROUTE_BUCKET + 1, :].astype(jnp.int32)
    sub = lax.broadcasted_iota(jnp.int32, (LANES, ts), 0)
    ohf = jnp.where(sub == bucket_row, 1.0, 0.0)
    loff = _bucket_offsets_col(ohf)
    rank = lax.dot_general(ohf.astype(BF16), lstrict_ref[...], (((1,), (1,)), ((), ())),
                           preferred_element_type=F32)
    dest = jnp.sum(ohf * (loff + rank), axis=0, keepdims=True).astype(jnp.int32)
    rowi = lax.broadcasted_iota(jnp.int32, (LROWS, ts), 0)
    perm = jnp.where(rowi == dest, 1.0, 0.0).astype(BF16)
    xs_ref[slot] = jnp.dot(perm, hb_ref[...], preferred_element_type=F32).astype(BF16)
    rh, rl = _split2(route)
    wparts = jnp.dot(perm, jnp.concatenate([rh, rl], axis=1), preferred_element_type=F32)
    ws_ref[slot] = wparts[:, 0:LANES] + wparts[:, LANES:2 * LANES]

    def copies(step, g):
        sl = step & 1
        src = pl.ds(pl.multiple_of(g * GRAN, GRAN), GRAN)
        dst = pl.ds(pl.multiple_of(gmap_ref[step * LGRAN + g] * GRAN, GRAN), GRAN)
        return (pltpu.make_async_copy(xs_ref.at[sl, src, :], xg_ref.at[dst, :], sem.at[0, sl]),
                pltpu.make_async_copy(ws_ref.at[sl, src, :], wsg_ref.at[dst, :], sem.at[1, sl]))

    def fill_copies(b, i):
        dst = pl.ds(pl.multiple_of((tail0_ref[b] + i) * GRAN, GRAN), GRAN)
        return (pltpu.make_async_copy(zx_ref.at[0:GRAN, :], xg_ref.at[dst, :], sem.at[2, 0]),
                pltpu.make_async_copy(zw_ref.at[0:GRAN, :], wsg_ref.at[dst, :], sem.at[2, 1]))

    def unused_tile_copies(t):
        dst = pl.ds(pl.multiple_of(t * MOE_TM, MOE_TM), MOE_TM)
        return (pltpu.make_async_copy(zx_ref, xg_ref.at[dst, :], sem.at[2, 0]),
                pltpu.make_async_copy(zw_ref, wsg_ref.at[dst, :], sem.at[2, 1]))

    def run(step, fn):
        def body(g, carry):
            for cp in copies(step, g):
                fn(cp)
            return carry
        lax.fori_loop(0, nvalid_ref[step], body, 0)

    def run_fill(fn):
        for b in range(N_BUCKETS):
            def body(i, carry, b=b):
                for cp in fill_copies(b, i):
                    fn(cp)
                return carry
            lax.fori_loop(0, taillen_ref[b], body, 0)

        def tile_body(t, carry):
            for cp in unused_tile_copies(t):
                fn(cp)
            return carry
        lax.fori_loop(tail0_ref[N_BUCKETS], xg_ref.shape[0] // MOE_TM, tile_body, 0)

    @pl.when(s == 0)
    def _():
        zx_ref[...] = jnp.zeros_like(zx_ref)
        zw_ref[...] = jnp.zeros_like(zw_ref)
        run_fill(lambda cp: cp.start())

    run(s, lambda cp: cp.start())

    @pl.when(s > 0)
    def _():
        run(s - 1, lambda cp: cp.wait())

    @pl.when(s == nsteps - 1)
    def _():
        run(s, lambda cp: cp.wait())
        run_fill(lambda cp: cp.wait())


def _moe_sort(plan, hb, route, lstrict):
    n, d = hb.shape
    ts = MOE_TS
    rows = plan["n_tiles"] * MOE_TM
    return pl.pallas_call(
        _moe_sort_kernel,
        out_shape=(jax.ShapeDtypeStruct((rows, d), BF16), jax.ShapeDtypeStruct((rows, LANES), F32)),
        grid_spec=pltpu.PrefetchScalarGridSpec(
            num_scalar_prefetch=4,
            grid=(n // ts,),
            in_specs=[pl.BlockSpec((ts, d), lambda s, *_: (s, 0)), pl.BlockSpec((ts, LANES), lambda s, *_: (s, 0)),
                      pl.BlockSpec((ts, ts), lambda s, *_: (0, 0))],
            out_specs=(pl.BlockSpec(memory_space=pl.ANY), pl.BlockSpec(memory_space=pl.ANY)),
            scratch_shapes=[pltpu.VMEM((2, LROWS, d), BF16), pltpu.VMEM((2, LROWS, LANES), F32),
                            pltpu.VMEM((MOE_TM, d), BF16), pltpu.VMEM((MOE_TM, LANES), F32),
                            pltpu.SemaphoreType.DMA((3, 2))]),
        compiler_params=pltpu.CompilerParams(dimension_semantics=("arbitrary",), vmem_limit_bytes=VMEM_LIMIT),
        name="moe_sort",
    )(plan["gmap"], plan["nvalid"], plan["tail0"], plan["taillen"], hb, route, lstrict)


def _moe_expert_kernel(xt_ref, elo_ref, ehi_ref, valid_ref, x_ref, w_ref, wg0, wu0, wd0, wg1, wu1, wd1, o_ref):
    t = pl.program_id(0)

    @pl.when(valid_ref[t] > 0)
    def _():
        x = x_ref[...]
        w = w_ref[...]
        gates = [jnp.dot(x, wg[0].astype(BF16), preferred_element_type=F32) for wg in (wg0, wg1)]
        ups = [jnp.dot(x, wu[0].astype(BF16), preferred_element_type=F32) for wu in (wu0, wu1)]
        hes = [(_silu(gates[i]) * ups[i] * w[:, lane:lane + 1]).astype(BF16)
               for i, lane in enumerate((ROUTE_W_LO, ROUTE_W_HI))]
        o_ref[...] = (jnp.dot(hes[0], wd0[0].astype(BF16), preferred_element_type=F32)
                      + jnp.dot(hes[1], wd1[0].astype(BF16), preferred_element_type=F32)).astype(o_ref.dtype)

    @pl.when(valid_ref[t] == 0)
    def _():
        o_ref[...] = jnp.zeros_like(o_ref)


def _moe_experts(plan, xg, wsg, wg, wu, wd):
    rows, d = xg.shape
    tm = MOE_TM
    tok = lambda width: pl.BlockSpec((tm, width), lambda t, xt, elo, ehi, valid: (xt[t], 0))
    lo3 = lambda shape: pl.BlockSpec(shape, lambda t, xt, elo, ehi, valid: (elo[t], 0, 0))
    hi3 = lambda shape: pl.BlockSpec(shape, lambda t, xt, elo, ehi, valid: (ehi[t], 0, 0))
    return pl.pallas_call(
        _moe_expert_kernel,
        out_shape=jax.ShapeDtypeStruct((rows, d), BF16),
        grid_spec=pltpu.PrefetchScalarGridSpec(
            num_scalar_prefetch=4,
            grid=(rows // tm,),
            in_specs=[tok(d), tok(LANES),
                      lo3((1, d, D_EXPERT)), lo3((1, d, D_EXPERT)), lo3((1, D_EXPERT, d)),
                      hi3((1, d, D_EXPERT)), hi3((1, d, D_EXPERT)), hi3((1, D_EXPERT, d))],
            out_specs=pl.BlockSpec((tm, d), lambda t, *_: (t, 0))),
        compiler_params=pltpu.CompilerParams(dimension_semantics=("arbitrary",), vmem_limit_bytes=VMEM_LIMIT),
        name="moe_experts",
    )(plan["xtile"], plan["elo"], plan["ehi"], plan["valid"], xg, wsg, wg, wu, wd, wg, wu, wd)


def _moe_unsort_kernel(gmap_ref, og_ref, route_ref, h_ref, lstrict_ref, g_ref, b_ref, out_ref, ol_ref, sem):
    s = pl.program_id(0)
    nsteps = pl.num_programs(0)
    slot = s & 1
    ts = route_ref.shape[0]

    def gather(step, fn):
        sl = step & 1

        def body(g, carry):
            src = pl.ds(pl.multiple_of(gmap_ref[step * LGRAN + g] * GRAN, GRAN), GRAN)
            dst = pl.ds(pl.multiple_of(g * GRAN, GRAN), GRAN)
            fn(pltpu.make_async_copy(og_ref.at[src, :], ol_ref.at[sl, dst, :], sem.at[sl]))
            return carry
        lax.fori_loop(0, LGRAN, body, 0)

    @pl.when(s == 0)
    def _():
        gather(s, lambda cp: cp.start())

    @pl.when(s + 1 < nsteps)
    def _():
        gather(s + 1, lambda cp: cp.start())

    route = route_ref[...]
    bucket_col = route[:, ROUTE_BUCKET:ROUTE_BUCKET + 1].astype(jnp.int32)
    lane = lax.broadcasted_iota(jnp.int32, (ts, LANES), 1)
    ohf = jnp.where(lane == bucket_col, 1.0, 0.0)
    cnt = jnp.sum(ohf, axis=0, keepdims=True).astype(jnp.int32)
    pad = (((cnt + (GRAN - 1)) >> GRAN_SHIFT) << GRAN_SHIFT).astype(F32)
    r = lax.broadcasted_iota(jnp.int32, (LANES, LANES), 0)
    c = lax.broadcasted_iota(jnp.int32, (LANES, LANES), 1)
    loff = _dot(jnp.broadcast_to(pad, (8, LANES)), jnp.where(r < c, 1.0, 0.0))[0:1, :]
    rank = jnp.dot(lstrict_ref[...], ohf.astype(BF16), preferred_element_type=F32)
    dest = jnp.sum(ohf * (loff + rank), axis=1, keepdims=True).astype(jnp.int32)
    lrow = lax.broadcasted_iota(jnp.int32, (ts, LROWS), 1)
    perm_t = jnp.where(lrow == dest, 1.0, 0.0).astype(BF16)

    gather(s, lambda cp: cp.wait())
    ffn = jnp.dot(perm_t, ol_ref[slot], preferred_element_type=F32)
    out_ref[...] = _layer_norm(DEEPNORM_ALPHA * h_ref[...] + ffn, g_ref[...], b_ref[...])


def _moe_unsort(plan, og, route, hf, lstrict, g2, b2):
    n, d = hf.shape
    ts = MOE_TS
    row = pl.BlockSpec((1, d), lambda s, *_: (0, 0))
    return pl.pallas_call(
        _moe_unsort_kernel,
        out_shape=jax.ShapeDtypeStruct((n, d), F32),
        grid_spec=pltpu.PrefetchScalarGridSpec(
            num_scalar_prefetch=1,
            grid=(n // ts,),
            in_specs=[pl.BlockSpec(memory_space=pl.ANY), pl.BlockSpec((ts, LANES), lambda s, *_: (s, 0)),
                      pl.BlockSpec((ts, d), lambda s, *_: (s, 0)), pl.BlockSpec((ts, ts), lambda s, *_: (0, 0)),
                      row, row],
            out_specs=pl.BlockSpec((ts, d), lambda s, *_: (s, 0)),
            scratch_shapes=[pltpu.VMEM((2, LROWS, d), BF16), pltpu.SemaphoreType.DMA((2,))]),
        compiler_params=pltpu.CompilerParams(dimension_semantics=("arbitrary",), vmem_limit_bytes=VMEM_LIMIT),
        name="moe_unsort",
    )(plan["gmap_back"], og, route, hf, lstrict, g2, b2)


def _moe_plan(cnt_half, n):
    nsrc = n // MOE_TS
    i32 = jnp.int32
    cnt = cnt_half.reshape(nsrc, -1, LANES).sum(axis=1)[:, :N_BUCKETS].astype(i32)
    run_g = (cnt + GRAN - 1) // GRAN
    nvalid = run_g.sum(axis=1)
    loff_g = jnp.cumsum(run_g, axis=1) - run_g
    bucket_g = run_g.sum(axis=0)
    gpt = MOE_TM // GRAN
    btiles = (bucket_g + gpt - 1) // gpt
    tend = jnp.cumsum(btiles)
    tstart = tend - btiles
    gofs = tstart[None, :] * gpt + jnp.cumsum(run_g, axis=0) - run_g
    n_tiles = -(-(n + nsrc * N_BUCKETS * (GRAN - 1)) // MOE_TM) + N_BUCKETS + 1
    g = jnp.arange(LGRAN, dtype=i32)[None, :, None]
    in_run = (g >= loff_g[:, None, :]) & (g < (loff_g + run_g)[:, None, :])
    gmap = jnp.arange(LGRAN, dtype=i32)[None, :] + jnp.sum(jnp.where(in_run, (gofs - loff_g)[:, None, :], 0), axis=2)
    is_valid = jnp.arange(LGRAN, dtype=i32)[None, :] < nvalid[:, None]
    zero_gran = (n_tiles - 1) * gpt
    t = jnp.arange(n_tiles, dtype=i32)
    tb = jnp.minimum(jnp.sum(t[:, None] >= tend[None, :], axis=1), N_BUCKETS - 1)
    valid = (t < tend[-1]).astype(i32)
    pairs = [(a, b) for a in range(EXPERTS_PER_GROUP) for b in range(a + 1, EXPERTS_PER_GROUP)]
    pidx = tb % N_PAIRS
    pair_a = sum(jnp.where(pidx == i, a, 0) for i, (a, _) in enumerate(pairs))
    pair_b = sum(jnp.where(pidx == i, b, 0) for i, (_, b) in enumerate(pairs))
    grp = tb // N_PAIRS
    return {
        "n_tiles": n_tiles,
        "gmap": jnp.where(is_valid, gmap, 0).reshape(-1).astype(i32),
        "gmap_back": jnp.where(is_valid, gmap, zero_gran).reshape(-1).astype(i32),
        "nvalid": nvalid.astype(i32),
        "tail0": jnp.concatenate([tstart * gpt + bucket_g, tend[-1:]]).astype(i32),
        "taillen": (btiles * gpt - bucket_g).astype(i32),
        "xtile": jnp.where(valid > 0, t, 0).astype(i32),
        "elo": (grp * EXPERTS_PER_GROUP + pair_a).astype(i32),
        "ehi": (grp * EXPERTS_PER_GROUP + pair_b).astype(i32),
        "valid": valid,
    }


def _pad_lanes(a, lane0=0):
    return jnp.zeros((1, LANES), F32).at[0, lane0:lane0 + a.shape[0]].set(a.astype(F32))


def _rope_tables(seq):
    half = HEAD_DIM // 2
    inv_freq = ROPE_THETA ** (-jnp.arange(half, dtype=F32) / half)
    ang = jnp.arange(seq).astype(F32)[:, None] * inv_freq[None, :]
    cos, sin = jnp.cos(ang), jnp.sin(ang)
    return jnp.concatenate([cos, cos], axis=-1), jnp.concatenate([-sin, sin], axis=-1)


def _layer(x, w_in, conv_w, a_log, dt_bias, dn_norm_w, w_out, ln1_g, ln1_b, router_w1, router_b1,
           router_w2, router_b2, w_gate, w_up, w_down, ln2_g, ln2_b):
    bsz, seq, d = x.shape
    n = bsz * seq
    x2 = x.reshape(n, d)

    o_z, o_b, o_mb = 3 * D_DN, 4 * D_DN, 4 * D_DN + 2 * N_HEADS_DN
    w_ba = jnp.pad(w_in[:, o_b:o_mb], ((0, 0), (0, LANES - 2 * N_HEADS_DN)))
    w_all = jnp.concatenate([w_in[:, :o_z], w_in[:, o_z:o_b], w_ba, w_in[:, o_mb:]], axis=1).astype(BF16)

    tm = min(512, n)
    dn_qkv, z, ba, mb_qkv = _in_proj(x2, w_all, conv_w, tm, seq)

    y_dn = _deltanet(dn_qkv.reshape(bsz, seq, 3 * D_DN), z.reshape(bsz, seq, D_DN), ba.reshape(bsz, seq, LANES),
                     _pad_lanes(a_log, N_HEADS_DN), _pad_lanes(dt_bias, N_HEADS_DN),
                     dn_norm_w.astype(F32).reshape(1, HEAD_DIM))

    cos_t, sin_t = _rope_tables(seq)
    q_r, k_r, vt, sel = _moba_prep(mb_qkv.reshape(bsz, seq, 3 * D_MOBA), cos_t, sin_t)
    y_mb = _moba_attn(q_r, k_r, vt, sel)

    rw = jnp.concatenate([router_w1, jnp.transpose(router_w2, (1, 0, 2)).reshape(d, N_EXPERTS)], axis=1)
    rw = jnp.pad(rw, ((0, 0), (0, LANES - rw.shape[1])))
    rb = _pad_lanes(jnp.concatenate([router_b1, router_b2.reshape(-1)]))
    hf, hb, route, cnt = _mix_route(y_dn.reshape(n, D_DN), y_mb.reshape(n, D_MOBA), x2, w_out.astype(BF16),
                                    ln1_g.reshape(1, d), ln1_b.reshape(1, d), rw, rb, min(2 * MOE_TS, n))

    plan = _moe_plan(cnt, n)
    idx = jnp.arange(MOE_TS, dtype=jnp.int32)
    lstrict = (idx[None, :] < idx[:, None]).astype(BF16)
    xg, wsg = _moe_sort(plan, hb, route, lstrict)
    og = _moe_experts(plan, xg, wsg, w_gate, w_up, w_down)
    out = _moe_unsort(plan, og, route, hf, lstrict, ln2_g.reshape(1, d), ln2_b.reshape(1, d))
    return out.reshape(bsz, seq, d)


def kernel(x, w_in, conv_w, a_log, dt_bias, dn_norm_w, w_out, ln1_g, ln1_b, router_w1, router_b1, router_w2, router_b2, expert_w_gate, expert_w_up, expert_w_down, ln2_g, ln2_b):
    for l in range(DEPTH):
        x = _layer(x, w_in[l], conv_w[l], a_log[l], dt_bias[l], dn_norm_w[l], w_out[l], ln1_g[l], ln1_b[l],
                   router_w1[l], router_b1[l], router_w2[l], router_b2[l], expert_w_gate[l], expert_w_up[l],
                   expert_w_down[l], ln2_g[l], ln2_b[l])
    return x
```

```python
import functools

import jax
import jax.numpy as jnp
from jax import lax
from jax.experimental import pallas as pl
from jax.experimental.pallas import tpu as pltpu

F32 = jnp.float32
BF16 = jnp.bfloat16

HEAD_DIM = 128
N_HEADS_DN = 4
N_HEADS_MOBA = 4
D_DN = N_HEADS_DN * HEAD_DIM
D_MOBA = N_HEADS_MOBA * HEAD_DIM
CONV_K = 4
DN_CHUNK = 64
MOBA_BLOCK = 256
MOBA_TOPK = 3
MOBA_UNROLL = 4
ROPE_THETA = 10000.0
N_GROUPS = 4
EXPERTS_PER_GROUP = 4
N_EXPERTS = N_GROUPS * EXPERTS_PER_GROUP
D_EXPERT = 256
LN_EPS = 1e-5
RMS_EPS = 1e-6
L2_EPS = 1e-6
NEG_INF = -1e30
LOG2E = 1.4426950408889634
DEPTH = 1
DEEPNORM_ALPHA = (2 * DEPTH) ** 0.25

LANES = 128
DN_TILE = 256
DN_HEADS_PER_STEP = 4
GATE_LANE0 = N_GROUPS
N_PAIRS = EXPERTS_PER_GROUP * (EXPERTS_PER_GROUP - 1) // 2
N_BUCKETS = N_GROUPS * N_PAIRS
ROUTE_BUCKET, ROUTE_W_LO, ROUTE_W_HI = 0, 1, 2
MOE_TS = 512
MOE_TM = 512
GRAN = 16
GRAN_SHIFT = 4
LROWS = -(-(MOE_TS + N_BUCKETS * (GRAN - 1)) // LANES) * LANES
LGRAN = LROWS // GRAN
PERM_CHUNKS = 4
VMEM_LIMIT = 48 * 1024 * 1024


def _dot(a, b):
    return jnp.dot(a.astype(BF16), b.astype(BF16), preferred_element_type=F32)


def _dot_nt(a, b):
    return lax.dot_general(a.astype(BF16), b.astype(BF16), (((1,), (1,)), ((), ())),
                           preferred_element_type=F32)


def _split2(a):
    hi = a.astype(BF16)
    lo = (a - hi.astype(F32)).astype(BF16)
    return hi, lo


def _split3(a):
    hi = a.astype(BF16)
    r = a - hi.astype(F32)
    mid = r.astype(BF16)
    lo = (r - mid.astype(F32)).astype(BF16)
    return hi, mid, lo


def _dot3(a, b):
    ah, al = _split2(a)
    bh, bl = _split2(b)
    return (jnp.dot(ah, bh, preferred_element_type=F32) + jnp.dot(ah, bl, preferred_element_type=F32)
            + jnp.dot(al, bh, preferred_element_type=F32))


def _dot3_nt(a, b):
    ah, al = _split2(a)
    bh, bl = _split2(b)
    dn = (((1,), (1,)), ((), ()))
    return (lax.dot_general(ah, bh, dn, preferred_element_type=F32)
            + lax.dot_general(ah, bl, dn, preferred_element_type=F32)
            + lax.dot_general(al, bh, dn, preferred_element_type=F32))


def _dot_exact_lhs(a_bf16, b):
    bh, bm, bl = _split3(b)
    return (jnp.dot(a_bf16, bh, preferred_element_type=F32) + jnp.dot(a_bf16, bm, preferred_element_type=F32)
            + jnp.dot(a_bf16, bl, preferred_element_type=F32))


def _silu(x):
    return x * jax.nn.sigmoid(x)


def _softplus(x):
    return jnp.maximum(x, 0.0) + jnp.log1p(jnp.exp(-jnp.abs(x)))


def _layer_norm(t, g, b):
    mu = jnp.mean(t, axis=-1, keepdims=True)
    d = t - mu
    var = jnp.mean(d * d, axis=-1, keepdims=True)
    return d * lax.rsqrt(var + LN_EPS) * g + b


def _lane_pick(x, lane):
    ids = lax.broadcasted_iota(jnp.int32, x.shape, 1)
    return jnp.sum(jnp.where(ids == lane, x, 0.0), axis=1, keepdims=True)


def _in_proj_kernel(x_ref, w_ref, cw_ref, dn_ref, z_ref, ba_ref, mb_ref, cb_ref, *, tiles_per_seq):
    i = pl.program_id(0)
    tm = x_ref.shape[0]
    o0 = 3 * D_DN
    o1 = o0 + D_DN
    o2 = o1 + LANES

    @pl.when(i % tiles_per_seq == 0)
    def _():
        cb_ref[0:8, :] = jnp.zeros((8, o0), F32)

    xb = x_ref[...].astype(BF16)
    u = jnp.dot(xb, w_ref[:, 0:o0], preferred_element_type=F32)
    z_ref[...] = _silu(jnp.dot(xb, w_ref[:, o0:o1], preferred_element_type=F32))
    ba_ref[...] = jnp.dot(xb, w_ref[:, o1:o2], preferred_element_type=F32)
    mb_ref[...] = jnp.dot(xb, w_ref[:, o2:o2 + 3 * D_MOBA], preferred_element_type=F32).astype(mb_ref.dtype)

    cb_ref[8:8 + tm, :] = u
    acc = cw_ref[CONV_K - 1:CONV_K, :] * u
    for s in range(1, CONV_K):
        acc = acc + cw_ref[CONV_K - 1 - s:CONV_K - s, :] * cb_ref[8 - s:8 - s + tm, :]
    cb_ref[0:8, :] = u[tm - 8:tm, :]
    qkv = _silu(acc)

    outs = []
    for h in range(2 * N_HEADS_DN):
        t = qkv[:, h * HEAD_DIM:(h + 1) * HEAD_DIM]
        t = t * lax.rsqrt(jnp.sum(t * t, axis=-1, keepdims=True) + L2_EPS)
        outs.append(t * (HEAD_DIM ** -0.5) if h < N_HEADS_DN else t)
    outs.append(qkv[:, 2 * D_DN:3 * D_DN])
    dn_ref[...] = jnp.concatenate(outs, axis=1)


def _in_proj(x2, w_all, conv_w, tm, seq):
    n, d = x2.shape
    wc = w_all.shape[1]
    return pl.pallas_call(
        functools.partial(_in_proj_kernel, tiles_per_seq=seq // tm),
        out_shape=(jax.ShapeDtypeStruct((n, 3 * D_DN), F32), jax.ShapeDtypeStruct((n, D_DN), F32),
                   jax.ShapeDtypeStruct((n, LANES), F32), jax.ShapeDtypeStruct((n, 3 * D_MOBA), BF16)),
        grid=(n // tm,),
        in_specs=[pl.BlockSpec((tm, d), lambda i: (i, 0)), pl.BlockSpec((d, wc), lambda i: (0, 0)),
                  pl.BlockSpec((CONV_K, 3 * D_DN), lambda i: (0, 0))],
        out_specs=(pl.BlockSpec((tm, 3 * D_DN), lambda i: (i, 0)), pl.BlockSpec((tm, D_DN), lambda i: (i, 0)),
                   pl.BlockSpec((tm, LANES), lambda i: (i, 0)), pl.BlockSpec((tm, 3 * D_MOBA), lambda i: (i, 0))),
        scratch_shapes=[pltpu.VMEM((8 + tm, 3 * D_DN), F32)],
        compiler_params=pltpu.CompilerParams(dimension_semantics=("arbitrary",), vmem_limit_bytes=VMEM_LIMIT),
        name="in_proj",
    )(x2, w_all, conv_w)


def _deltanet_kernel(q_ref, k_ref, v_ref, z_ref, ba_ref, alog_ref, dtb_ref, normw_ref, y_ref,
                     s_ref, wq_s, u_s, qk_s, kdt_s, egl_s, *, hb):
    hg = pl.program_id(1)
    t = pl.program_id(2)
    tt = DN_TILE
    nchunk = tt // DN_CHUNK
    hs = range(hb)

    @pl.when(t == 0)
    def _():
        for ref in (s_ref, wq_s, u_s, qk_s, kdt_s, egl_s):
            ref[...] = jnp.zeros_like(ref)

    rd = t & 1
    wr = 1 - rd
    state = [s_ref[h] for h in hs]
    outs = [[] for _ in hs]

    pend = {}

    def chain_a(c):
        pend["r"] = [jnp.dot(wq_s[rd, h, c], state[h].astype(BF16), preferred_element_type=F32) for h in hs]

    def chain_b(c):
        lo, hi = c * DN_CHUNK, (c + 1) * DN_CHUNK
        r = pend["r"]
        vz = []
        for h in hs:
            parts = []
            if lo > 0:
                parts.append(jnp.zeros((lo, HEAD_DIM), F32))
            parts.append(u_s[rd, h, lo:hi, :] - r[h][0:DN_CHUNK, :])
            if hi < tt:
                parts.append(jnp.zeros((tt - hi, HEAD_DIM), F32))
            vz.append(jnp.concatenate(parts, axis=0).astype(BF16))
        for h in hs:
            outs[h].append(r[h][DN_CHUNK:2 * DN_CHUNK, :]
                           + jnp.dot(qk_s[rd, h, lo:hi, :], vz[h], preferred_element_type=F32))
        for h in hs:
            state[h] = (state[h] * egl_s[rd, h, 8 * c:8 * c + 1, :]
                        + jnp.dot(kdt_s[rd, h], vz[h], preferred_element_type=F32))

    chain_a(0)

    q_all = q_ref[0]
    k_all = k_ref[0]
    v_all = v_ref[0]

    ba = ba_ref[0]
    beta_all = jax.nn.sigmoid(ba)
    g_all = -jnp.exp(alog_ref[...]) * _softplus(ba + dtb_ref[...])

    row = lax.broadcasted_iota(jnp.int32, (tt, tt), 0)
    col = lax.broadcasted_iota(jnp.int32, (tt, tt), 1)
    same = (row >> 6) == (col >> 6)
    incl = same & (row >= col)
    strict = same & (row > col)

    gc_all = _dot_exact_lhs(incl.astype(BF16), g_all)
    gct = jnp.transpose(gc_all)
    sub = lax.broadcasted_iota(jnp.int32, gct.shape, 0)
    sls = [slice(hh * HEAD_DIM, (hh + 1) * HEAD_DIM) for hh in hs]
    heads = [hg * hb + hh for hh in hs]
    q = [q_all[:, sl] for sl in sls]
    k = [k_all[:, sl] for sl in sls]
    v = [v_all[:, sl] for sl in sls]
    beta = [_lane_pick(beta_all, h) for h in heads]
    gcc = [_lane_pick(gc_all, h + N_HEADS_DN) for h in heads]
    gcr = [jnp.sum(jnp.where(sub == h + N_HEADS_DN, gct, 0.0), axis=0, keepdims=True) for h in heads]
    chain_b(0)

    decay = [jnp.where(incl, jnp.exp(jnp.where(incl, gcc[h] - gcr[h], 0.0)), 0.0) for h in hs]
    kb = [k[h] * beta[h] for h in hs]
    vb = [v[h] * beta[h] for h in hs]
    a_mat = [jnp.where(strict, _dot_nt(kb[h], k[h]) * decay[h], 0.0) for h in hs]
    chain_a(1)
    qk = [_dot_nt(q[h], k[h]) * decay[h] for h in hs]
    eye = (row == col).astype(F32)
    d8 = (row >> 3) == (col >> 3)
    a8 = [jnp.where(d8, a, 0.0) for a in a_mat]
    chain_b(1)
    a8_2 = [_dot(a, a) for a in a8]
    chain_a(2)
    a8_4 = [_dot(a, a) for a in a8_2]
    chain_b(2)
    x = [_dot(eye - a, eye + a2) for a, a2 in zip(a8, a8_2)]
    chain_a(3)
    x = [_dot(xi, eye + a4) for xi, a4 in zip(x, a8_4)]
    chain_b(3)
    s = 8
    while s < DN_CHUNK:
        sh = s.bit_length() - 1
        off = ((row >> (sh + 1)) == (col >> (sh + 1))) & ((row >> sh) != (col >> sh))
        y = [_dot(jnp.where(off, a, 0.0), xi) for a, xi in zip(a_mat, x)]
        x = [xi - _dot(xi, yi) for xi, yi in zip(x, y)]
        s *= 2
    tinv = x
    eg = [jnp.exp(g) for g in gcc]
    wu = [_dot(tinv[h], jnp.concatenate([kb[h] * eg[h], vb[h]], axis=1)) for h in hs]
    qd = [q[h] * eg[h] for h in hs]
    gl_rows = [[g[(c + 1) * DN_CHUNK - 1:(c + 1) * DN_CHUNK, :] for c in range(nchunk)] for g in gcc]
    gl_col = [jnp.concatenate([jnp.broadcast_to(g, (DN_CHUNK, 1)) for g in rows], axis=0) for rows in gl_rows]
    kdt = [jnp.transpose(k[h] * jnp.exp(gl_col[h] - gcc[h])) for h in hs]

    ys = []
    for h in hs:
        o = jnp.concatenate(outs[h], axis=0)
        o = o * lax.rsqrt(jnp.mean(o * o, axis=-1, keepdims=True) + RMS_EPS) * normw_ref[...]
        ys.append(o * z_ref[0, :, sls[h]])
    y_ref[0] = jnp.concatenate(ys, axis=1).astype(y_ref.dtype)
    s_ref[...] = jnp.stack(state, axis=0)

    for h in hs:
        for c in range(nchunk):
            lo, hi = c * DN_CHUNK, (c + 1) * DN_CHUNK
            wq_s[wr, h, c] = jnp.concatenate([wu[h][lo:hi, 0:HEAD_DIM], qd[h][lo:hi, :]], axis=0).astype(BF16)
            egl_s[wr, h, 8 * c:8 * c + 8, :] = jnp.broadcast_to(jnp.exp(gl_rows[h][c]), (8, HEAD_DIM))
        u_s[wr, h] = wu[h][:, HEAD_DIM:2 * HEAD_DIM]
        qk_s[wr, h] = qk[h].astype(BF16)
        kdt_s[wr, h] = kdt[h].astype(BF16)


def _deltanet(dn_qkv, z, ba, alog_row, dtb_row, normw_row):
    bsz, seq, _ = dn_qkv.shape
    tt = DN_TILE
    nt = seq // tt
    hb = DN_HEADS_PER_STEP
    ng = N_HEADS_DN // hb
    w = hb * HEAD_DIM
    nchunk = tt // DN_CHUNK

    def cur_spec(off, width):
        return pl.BlockSpec((1, tt, width), lambda b, g, t: (b, jnp.minimum(t, nt - 1), g + off))

    prev_spec = pl.BlockSpec((1, tt, w), lambda b, g, t: (b, jnp.maximum(t - 1, 0), g))
    row_spec = pl.BlockSpec((1, LANES), lambda b, g, t: (0, 0))
    return pl.pallas_call(
        functools.partial(_deltanet_kernel, hb=hb),
        out_shape=jax.ShapeDtypeStruct((bsz, seq, D_DN), BF16),
        grid=(bsz, ng, nt + 1),
        in_specs=[cur_spec(0, w), cur_spec(ng, w), cur_spec(2 * ng, w), prev_spec,
                  pl.BlockSpec((1, tt, LANES), lambda b, g, t: (b, jnp.minimum(t, nt - 1), 0)),
                  row_spec, row_spec, row_spec],
        out_specs=prev_spec,
        scratch_shapes=[pltpu.VMEM((hb, HEAD_DIM, HEAD_DIM), F32),
                        pltpu.VMEM((2, hb, nchunk, 2 * DN_CHUNK, HEAD_DIM), BF16),
                        pltpu.VMEM((2, hb, tt, HEAD_DIM), F32),
                        pltpu.VMEM((2, hb, tt, tt), BF16),
                        pltpu.VMEM((2, hb, HEAD_DIM, tt), BF16),
                        pltpu.VMEM((2, hb, 8 * nchunk, HEAD_DIM), F32)],
        compiler_params=pltpu.CompilerParams(dimension_semantics=("parallel", "parallel", "arbitrary"),
                                             vmem_limit_bytes=VMEM_LIMIT),
        name="deltanet",
    )(dn_qkv, dn_qkv, dn_qkv, z, ba, alog_row, dtb_row, normw_row)


def _moba_prep_kernel(x_ref, cos_ref, sin_ref, q_ref, k_ref, vt_ref, sel_ref, km_ref, *, nb, topk):
    j = pl.program_id(1)

    @pl.when(j == 0)
    def _():
        km_ref[...] = jnp.zeros_like(km_ref)

    cos = cos_ref[...]
    sin = sin_ref[...]
    half = HEAD_DIM // 2
    blk = lax.broadcasted_iota(jnp.int32, (nb, MOBA_BLOCK), 0)
    kmeans = []
    for h in range(N_HEADS_MOBA):
        qh = x_ref[0, :, h * HEAD_DIM:(h + 1) * HEAD_DIM].astype(F32)
        kh = x_ref[0, :, D_MOBA + h * HEAD_DIM:D_MOBA + (h + 1) * HEAD_DIM].astype(F32)
        qr = (qh * cos + pltpu.roll(qh, half, 1) * sin) * (HEAD_DIM ** -0.5)
        kr = kh * cos + pltpu.roll(kh, half, 1) * sin
        q_ref[0, :, h * HEAD_DIM:(h + 1) * HEAD_DIM] = (qr * LOG2E).astype(q_ref.dtype)
        k_ref[0, :, h * HEAD_DIM:(h + 1) * HEAD_DIM] = kr.astype(k_ref.dtype)
        kmeans.append(jnp.mean(kr, axis=0, keepdims=True))

        gate = _dot3_nt(km_ref[:, h * HEAD_DIM:(h + 1) * HEAD_DIM], qr)
        gate = jnp.where(blk < j, gate, NEG_INF)
        rank = jnp.zeros(gate.shape, F32)
        for m in range(nb):
            gm = gate[m:m + 1, :]
            ahead = (gm > gate) | ((gm == gate) & (blk > m))
            rank = rank + jnp.where(ahead, 1.0, 0.0)
        sel = (blk < j) & (rank < topk)
        sel_ref[0, 0, h * nb:(h + 1) * nb, :] = jnp.where(sel, 1.0, 0.0)

    km_ref[pl.ds(j, 1), :] = jnp.concatenate(kmeans, axis=1)
    vt_ref[0, 0] = jnp.transpose(x_ref[0, :, 2 * D_MOBA:3 * D_MOBA].astype(F32)).astype(vt_ref.dtype)


def _moba_prep(mb_qkv, cos_t, sin_t):
    bsz, seq, _ = mb_qkv.shape
    nb = seq // MOBA_BLOCK
    topk = min(MOBA_TOPK, nb)
    kern = functools.partial(_moba_prep_kernel, nb=nb, topk=topk)
    tok_spec = pl.BlockSpec((1, MOBA_BLOCK, D_MOBA), lambda b, j: (b, j, 0))
    tab_spec = pl.BlockSpec((MOBA_BLOCK, HEAD_DIM), lambda b, j: (j, 0))
    return pl.pallas_call(
        kern,
        out_shape=(jax.ShapeDtypeStruct((bsz, seq, D_MOBA), BF16), jax.ShapeDtypeStruct((bsz, seq, D_MOBA), BF16),
                   jax.ShapeDtypeStruct((bsz, nb, D_MOBA, MOBA_BLOCK), BF16),
                   jax.ShapeDtypeStruct((bsz, nb, N_HEADS_MOBA * nb, MOBA_BLOCK), F32)),
        grid=(bsz, nb),
        in_specs=[pl.BlockSpec((1, MOBA_BLOCK, 3 * D_MOBA), lambda b, j: (b, j, 0)), tab_spec, tab_spec],
        out_specs=(tok_spec, tok_spec,
                   pl.BlockSpec((1, 1, D_MOBA, MOBA_BLOCK), lambda b, j: (b, j, 0, 0)),
                   pl.BlockSpec((1, 1, N_HEADS_MOBA * nb, MOBA_BLOCK), lambda b, j: (b, j, 0, 0))),
        scratch_shapes=[pltpu.VMEM((nb, D_MOBA), F32)],
        compiler_params=pltpu.CompilerParams(dimension_semantics=("parallel", "arbitrary"),
                                             vmem_limit_bytes=VMEM_LIMIT),
        name="moba_prep",
    )(mb_qkv, cos_t, sin_t)


def _moba_attn_kernel(q_ref, k_ref, vt_ref, sel_ref, o_ref, acc_ref, *, nb):
    j = pl.program_id(1)
    blk = MOBA_BLOCK
    nh = N_HEADS_MOBA
    dn = (((1,), (1,)), ((), ()))
    hsl = [slice(h * HEAD_DIM, (h + 1) * HEAD_DIM) for h in range(nh)]
    qs = [q_ref[0, :, hsl[h]] for h in range(nh)]

    ki = lax.broadcasted_iota(jnp.int32, (blk, blk), 0)
    qi = lax.broadcasted_iota(jnp.int32, (blk, blk), 1)
    own = pl.ds(pl.multiple_of(j * blk, blk), blk)
    s_own = [jnp.where(ki <= qi, lax.dot_general(k_ref[0, own, hsl[h]], qs[h], dn, preferred_element_type=F32),
                       NEG_INF) for h in range(nh)]

    def scores(n, h):
        kn = k_ref[0, pl.ds(pl.multiple_of(n * blk, blk), blk), hsl[h]]
        s = lax.dot_general(kn, qs[h], dn, preferred_element_type=F32)
        return jnp.where(sel_ref[0, 0, pl.ds(h * nb + n, 1), :] > 0.5, s, NEG_INF)

    ones8 = jnp.ones((8, blk), BF16)

    def softmax_pv(n, h, s, m, l):
        m_new = jnp.maximum(m, jnp.max(s, axis=0, keepdims=True))
        pb = jnp.exp2(s - m_new).astype(BF16)
        psum = jnp.dot(ones8, pb, preferred_element_type=F32)[0:1, :]
        pv = jnp.dot(vt_ref[0, n, hsl[h], :], pb, preferred_element_type=F32)
        return m_new, psum, pv

    ms, ls = [], []
    for h in range(nh):
        m, psum, pv = softmax_pv(j, h, s_own[h], jnp.full((1, blk), NEG_INF, F32), None)
        ms.append(m)
        ls.append(psum)
        acc_ref[h] = pv

    def body(step, carry):
        ms, ls = (list(t) for t in carry)
        n0 = MOBA_UNROLL * step
        cur = [scores(n0, h) for h in range(nh)]
        for i in range(MOBA_UNROLL):
            nxt = []
            for h in range(nh):
                if i + 1 < MOBA_UNROLL:
                    nxt.append(scores(n0 + i + 1, h))
                m_new, psum, pv = softmax_pv(n0 + i, h, cur[h], ms[h], ls[h])
                alpha = jnp.exp2(ms[h] - m_new)
                ls[h] = alpha * ls[h] + psum
                ms[h] = m_new
                acc_ref[h] = acc_ref[h] * alpha + pv
            cur = nxt
        return tuple(ms), tuple(ls)

    ms, ls = lax.fori_loop(0, (j + MOBA_UNROLL - 1) // MOBA_UNROLL, body, (tuple(ms), tuple(ls)))
    o_ref[0] = jnp.concatenate([jnp.transpose(acc_ref[h] / ls[h]) for h in range(nh)],
                               axis=1).astype(o_ref.dtype)


def _moba_attn(q_r, k_r, vt, sel):
    bsz, seq, _ = q_r.shape
    nb = seq // MOBA_BLOCK
    tok_spec = pl.BlockSpec((1, MOBA_BLOCK, D_MOBA), lambda b, j: (b, j, 0))
    return pl.pallas_call(
        functools.partial(_moba_attn_kernel, nb=nb),
        out_shape=jax.ShapeDtypeStruct((bsz, seq, D_MOBA), BF16),
        grid=(bsz, nb),
        in_specs=[tok_spec,
                  pl.BlockSpec((1, seq, D_MOBA), lambda b, j: (b, 0, 0)),
                  pl.BlockSpec((1, nb, D_MOBA, MOBA_BLOCK), lambda b, j: (b, 0, 0, 0)),
                  pl.BlockSpec((1, 1, N_HEADS_MOBA * nb, MOBA_BLOCK), lambda b, j: (b, j, 0, 0))],
        out_specs=tok_spec,
        scratch_shapes=[pltpu.VMEM((N_HEADS_MOBA, HEAD_DIM, MOBA_BLOCK), F32)],
        compiler_params=pltpu.CompilerParams(dimension_semantics=("parallel", "arbitrary"),
                                             vmem_limit_bytes=VMEM_LIMIT),
        name="moba_attn",
    )(q_r, k_r, vt, sel)


def _route_record(logits):
    lane = lax.broadcasted_iota(jnp.int32, logits.shape, 1)
    big = jnp.int32(LANES)

    def first_lane(mask):
        return jnp.min(jnp.where(mask, lane, big), axis=1, keepdims=True)

    is_g = lane < N_GROUPS
    m1 = jnp.max(jnp.where(is_g, logits, NEG_INF), axis=1, keepdims=True)
    s1 = jnp.sum(jnp.where(is_g, jnp.exp(logits - m1), 0.0), axis=1, keepdims=True)
    pg = 1.0 / s1
    gsel = first_lane(is_g & (logits == m1))

    in_grp = (lane >= GATE_LANE0) & (((lane - GATE_LANE0) >> 2) == gsel) & (lane < GATE_LANE0 + N_EXPERTS)
    m2 = jnp.max(jnp.where(in_grp, logits, NEG_INF), axis=1, keepdims=True)
    s2 = jnp.sum(jnp.where(in_grp, jnp.exp(logits - m2), 0.0), axis=1, keepdims=True)
    e1 = first_lane(in_grp & (logits == m2))
    rest = in_grp & (lane != e1)
    m2b = jnp.max(jnp.where(rest, logits, NEG_INF), axis=1, keepdims=True)
    e2 = first_lane(rest & (logits == m2b))
    pe1 = 1.0 / s2
    pe2 = jnp.exp(m2b - m2) / s2
    tot = pe1 + pe2
    w1 = pg * (pe1 / tot)
    w2 = pg * (pe2 / tot)
    first_lo = e1 < e2
    lo = jnp.minimum(e1, e2)
    hi = jnp.maximum(e1, e2)
    a = (lo - GATE_LANE0) & (EXPERTS_PER_GROUP - 1)
    b = (hi - GATE_LANE0) & (EXPERTS_PER_GROUP - 1)
    bucket = gsel * N_PAIRS + ((a * (2 * EXPERTS_PER_GROUP - 1 - a)) >> 1) + (b - a - 1)
    record = jnp.where(lane == ROUTE_BUCKET, bucket.astype(F32),
                       jnp.where(lane == ROUTE_W_LO, jnp.where(first_lo, w1, w2),
                                 jnp.where(lane == ROUTE_W_HI, jnp.where(first_lo, w2, w1), 0.0)))
    return record, jnp.sum(jnp.where(lane == bucket, 1.0, 0.0), axis=0, keepdims=True)


def _mix_route_kernel(ydn_ref, ymb_ref, x_ref, wo_ref, g_ref, b_ref, rw_ref, rb_ref, h_ref, hb_ref, route_ref,
                      cnt_ref):
    nsub = cnt_ref.shape[0]
    rows = [slice(i * MOE_TS, (i + 1) * MOE_TS) for i in range(nsub)]
    wh, wl = _split2(rw_ref[...])
    wcat = jnp.concatenate([wh, wl], axis=1)

    mixes = [jnp.dot(jnp.concatenate([ydn_ref[r, :], ymb_ref[r, :]], axis=1), wo_ref[...],
                     preferred_element_type=F32) for r in rows]
    logits = []
    for r, mix in zip(rows, mixes):
        hval = _layer_norm(DEEPNORM_ALPHA * x_ref[r, :] + mix, g_ref[...], b_ref[...])
        h_ref[r, :] = hval
        hb_ref[r, :] = hval.astype(BF16)
        hh, hl = _split2(hval)
        both = jnp.dot(hh, wcat, preferred_element_type=F32)
        logits.append(both[:, 0:LANES] + both[:, LANES:2 * LANES]
                      + jnp.dot(hl, wh, preferred_element_type=F32) + rb_ref[...])
    for i, r in enumerate(rows):
        record, counts = _route_record(logits[i])
        route_ref[r, :] = record
        cnt_ref[i] = counts


def _mix_route(y_dn, y_mb, x2, wo, g1, b1, rw, rb, tm):
    n, d = x2.shape
    nsub = tm // MOE_TS
    row = lambda w: pl.BlockSpec((1, w), lambda i: (0, 0))
    return pl.pallas_call(
        _mix_route_kernel,
        out_shape=(jax.ShapeDtypeStruct((n, d), F32), jax.ShapeDtypeStruct((n, d), BF16),
                   jax.ShapeDtypeStruct((n, LANES), F32), jax.ShapeDtypeStruct((n // MOE_TS, 1, LANES), F32)),
        grid=(n // tm,),
        in_specs=[pl.BlockSpec((tm, D_DN), lambda i: (i, 0)), pl.BlockSpec((tm, D_MOBA), lambda i: (i, 0)),
                  pl.BlockSpec((tm, d), lambda i: (i, 0)), pl.BlockSpec((D_DN + D_MOBA, d), lambda i: (0, 0)),
                  row(d), row(d), pl.BlockSpec((d, LANES), lambda i: (0, 0)), row(LANES)],
        out_specs=(pl.BlockSpec((tm, d), lambda i: (i, 0)), pl.BlockSpec((tm, d), lambda i: (i, 0)),
                   pl.BlockSpec((tm, LANES), lambda i: (i, 0)), pl.BlockSpec((nsub, 1, LANES), lambda i: (i, 0, 0))),
        compiler_params=pltpu.CompilerParams(dimension_semantics=("parallel",), vmem_limit_bytes=VMEM_LIMIT),
        name="mix_route",
    )(y_dn, y_mb, x2, wo, g1, b1, rw, rb)


def _bucket_offsets_col(ohf):
    cnt = jnp.sum(ohf, axis=1, keepdims=True).astype(jnp.int32)
    pad = (((cnt + (GRAN - 1)) >> GRAN_SHIFT) << GRAN_SHIFT).astype(F32)
    r = lax.broadcasted_iota(jnp.int32, (LANES, LANES), 0)
    c = lax.broadcasted_iota(jnp.int32, (LANES, LANES), 1)
    before = jnp.where(c < r, 1.0, 0.0)
    return _dot(before, jnp.broadcast_to(pad, (LANES, LANES)))[:, 0:1]


def _moe_sort_kernel(gmap_ref, nvalid_ref, tail0_ref, taillen_ref, hb_ref, route_ref, lstrict_ref,
                     xg_ref, wsg_ref, xs_ref, ws_ref, zx_ref, zw_ref, sem):
    s = pl.program_id(0)
    nsteps = pl.num_programs(0)
    slot = s & 1
    ts = route_ref.shape[0]
    route = route_ref[...]
    rt = jnp.transpose(route)
    bucket_row = rt[ROUTE_BUCKET:ROUTE_BUCKET + 1, :].astype(jnp.int32)
    sub = lax.broadcasted_iota(jnp.int32, (LANES, ts), 0)
    ohf = jnp.where(sub == bucket_row, 1.0, 0.0)
    loff = _bucket_offsets_col(ohf)
    rank = lax.dot_general(ohf.astype(BF16), lstrict_ref[...], (((1,), (1,)), ((), ())),
                           preferred_element_type=F32)
    dest = jnp.sum(ohf * (loff + rank), axis=0, keepdims=True).astype(jnp.int32)
    rh, rl = _split2(route)
    wcat = jnp.concatenate([rh, rl], axis=1)
    rc = LROWS // PERM_CHUNKS
    for c in range(PERM_CHUNKS):
        rowi = lax.broadcasted_iota(jnp.int32, (rc, ts), 0) + c * rc
        perm = jnp.where(rowi == dest, 1.0, 0.0).astype(BF16)
        xs_ref[slot, c * rc:(c + 1) * rc, :] = jnp.dot(perm, hb_ref[...], preferred_element_type=F32).astype(BF16)
        wparts = jnp.dot(perm, wcat, preferred_element_type=F32)
        ws_ref[slot, c * rc:(c + 1) * rc, :] = wparts[:, 0:LANES] + wparts[:, LANES:2 * LANES]

    def copies(step, g):
        sl = step & 1
        src = pl.ds(pl.multiple_of(g * GRAN, GRAN), GRAN)
        dst = pl.ds(pl.multiple_of(gmap_ref[step * LGRAN + g] * GRAN, GRAN), GRAN)
        return (pltpu.make_async_copy(xs_ref.at[sl, src, :], xg_ref.at[dst, :], sem.at[0, sl]),
                pltpu.make_async_copy(ws_ref.at[sl, src, :], wsg_ref.at[dst, :], sem.at[1, sl]))

    def fill_copies(b, i):
        dst = pl.ds(pl.multiple_of((tail0_ref[b] + i) * GRAN, GRAN), GRAN)
        return (pltpu.make_async_copy(zx_ref.at[0:GRAN, :], xg_ref.at[dst, :], sem.at[2, 0]),
                pltpu.make_async_copy(zw_ref.at[0:GRAN, :], wsg_ref.at[dst, :], sem.at[2, 1]))

    def unused_tile_copies(t):
        dst = pl.ds(pl.multiple_of(t * MOE_TM, MOE_TM), MOE_TM)
        return (pltpu.make_async_copy(zx_ref, xg_ref.at[dst, :], sem.at[2, 0]),
                pltpu.make_async_copy(zw_ref, wsg_ref.at[dst, :], sem.at[2, 1]))

    def run(step, fn):
        def body(g, carry):
            for cp in copies(step, g):
                fn(cp)
            return carry
        lax.fori_loop(0, nvalid_ref[step], body, 0)

    def run_fill(fn):
        for b in range(N_BUCKETS):
            def body(i, carry, b=b):
                for cp in fill_copies(b, i):
                    fn(cp)
                return carry
            lax.fori_loop(0, taillen_ref[b], body, 0)

        def tile_body(t, carry):
            for cp in unused_tile_copies(t):
                fn(cp)
            return carry
        lax.fori_loop(tail0_ref[N_BUCKETS], xg_ref.shape[0] // MOE_TM, tile_body, 0)

    @pl.when(s == 0)
    def _():
        zx_ref[...] = jnp.zeros_like(zx_ref)
        zw_ref[...] = jnp.zeros_like(zw_ref)
        run_fill(lambda cp: cp.start())

    run(s, lambda cp: cp.start())

    @pl.when(s > 0)
    def _():
        run(s - 1, lambda cp: cp.wait())

    @pl.when(s == nsteps - 1)
    def _():
        run(s, lambda cp: cp.wait())
        run_fill(lambda cp: cp.wait())


def _moe_sort(plan, hb, route, lstrict):
    n, d = hb.shape
    ts = MOE_TS
    rows = plan["n_tiles"] * MOE_TM
    return pl.pallas_call(
        _moe_sort_kernel,
        out_shape=(jax.ShapeDtypeStruct((rows, d), BF16), jax.ShapeDtypeStruct((rows, LANES), F32)),
        grid_spec=pltpu.PrefetchScalarGridSpec(
            num_scalar_prefetch=4,
            grid=(n // ts,),
            in_specs=[pl.BlockSpec((ts, d), lambda s, *_: (s, 0)), pl.BlockSpec((ts, LANES), lambda s, *_: (s, 0)),
                      pl.BlockSpec((ts, ts), lambda s, *_: (0, 0))],
            out_specs=(pl.BlockSpec(memory_space=pl.ANY), pl.BlockSpec(memory_space=pl.ANY)),
            scratch_shapes=[pltpu.VMEM((2, LROWS, d), BF16), pltpu.VMEM((2, LROWS, LANES), F32),
                            pltpu.VMEM((MOE_TM, d), BF16), pltpu.VMEM((MOE_TM, LANES), F32),
                            pltpu.SemaphoreType.DMA((3, 2))]),
        compiler_params=pltpu.CompilerParams(dimension_semantics=("arbitrary",), vmem_limit_bytes=VMEM_LIMIT),
        name="moe_sort",
    )(plan["gmap"], plan["nvalid"], plan["tail0"], plan["taillen"], hb, route, lstrict)


def _moe_expert_kernel(xt_ref, elo_ref, ehi_ref, valid_ref, x_ref, w_ref, wg0, wu0, wd0, wg1, wu1, wd1, o_ref):
    t = pl.program_id(0)

    @pl.when(valid_ref[t] > 0)
    def _():
        x = x_ref[...]
        w = w_ref[...]
        gates = [jnp.dot(x, wg[0].astype(BF16), preferred_element_type=F32) for wg in (wg0, wg1)]
        ups = [jnp.dot(x, wu[0].astype(BF16), preferred_element_type=F32) for wu in (wu0, wu1)]
        hes = [(_silu(gates[i]) * ups[i] * w[:, lane:lane + 1]).astype(BF16)
               for i, lane in enumerate((ROUTE_W_LO, ROUTE_W_HI))]
        o_ref[...] = (jnp.dot(hes[0], wd0[0].astype(BF16), preferred_element_type=F32)
                      + jnp.dot(hes[1], wd1[0].astype(BF16), preferred_element_type=F32)).astype(o_ref.dtype)

    @pl.when(valid_ref[t] == 0)
    def _():
        o_ref[...] = jnp.zeros_like(o_ref)


def _moe_experts(plan, xg, wsg, wg, wu, wd):
    rows, d = xg.shape
    tm = MOE_TM
    tok = lambda width: pl.BlockSpec((tm, width), lambda t, xt, elo, ehi, valid: (xt[t], 0))
    lo3 = lambda shape: pl.BlockSpec(shape, lambda t, xt, elo, ehi, valid: (elo[t], 0, 0))
    hi3 = lambda shape: pl.BlockSpec(shape, lambda t, xt, elo, ehi, valid: (ehi[t], 0, 0))
    return pl.pallas_call(
        _moe_expert_kernel,
        out_shape=jax.ShapeDtypeStruct((rows, d), BF16),
        grid_spec=pltpu.PrefetchScalarGridSpec(
            num_scalar_prefetch=4,
            grid=(rows // tm,),
            in_specs=[tok(d), tok(LANES),
                      lo3((1, d, D_EXPERT)), lo3((1, d, D_EXPERT)), lo3((1, D_EXPERT, d)),
                      hi3((1, d, D_EXPERT)), hi3((1, d, D_EXPERT)), hi3((1, D_EXPERT, d))],
            out_specs=pl.BlockSpec((tm, d), lambda t, *_: (t, 0))),
        compiler_params=pltpu.CompilerParams(dimension_semantics=("arbitrary",), vmem_limit_bytes=VMEM_LIMIT),
        name="moe_experts",
    )(plan["xtile"], plan["elo"], plan["ehi"], plan["valid"], xg, wsg, wg, wu, wd, wg, wu, wd)


def _moe_unsort_kernel(gmap_ref, og_ref, route_ref, h_ref, lstrict_ref, g_ref, b_ref, out_ref, ol_ref, sem):
    s = pl.program_id(0)
    nsteps = pl.num_programs(0)
    slot = s & 1
    ts = route_ref.shape[0]

    def gather(step, fn):
        sl = step & 1

        def body(g, carry):
            src = pl.ds(pl.multiple_of(gmap_ref[step * LGRAN + g] * GRAN, GRAN), GRAN)
            dst = pl.ds(pl.multiple_of(g * GRAN, GRAN), GRAN)
            fn(pltpu.make_async_copy(og_ref.at[src, :], ol_ref.at[sl, dst, :], sem.at[sl]))
            return carry
        lax.fori_loop(0, LGRAN, body, 0)

    @pl.when(s == 0)
    def _():
        gather(s, lambda cp: cp.start())

    @pl.when(s + 1 < nsteps)
    def _():
        gather(s + 1, lambda cp: cp.start())

    route = route_ref[...]
    bucket_col = route[:, ROUTE_BUCKET:ROUTE_BUCKET + 1].astype(jnp.int32)
    lane = lax.broadcasted_iota(jnp.int32, (ts, LANES), 1)
    ohf = jnp.where(lane == bucket_col, 1.0, 0.0)
    cnt = jnp.sum(ohf, axis=0, keepdims=True).astype(jnp.int32)
    pad = (((cnt + (GRAN - 1)) >> GRAN_SHIFT) << GRAN_SHIFT).astype(F32)
    r = lax.broadcasted_iota(jnp.int32, (LANES, LANES), 0)
    c = lax.broadcasted_iota(jnp.int32, (LANES, LANES), 1)
    loff = _dot(jnp.broadcast_to(pad, (8, LANES)), jnp.where(r < c, 1.0, 0.0))[0:1, :]
    rank = jnp.dot(lstrict_ref[...], ohf.astype(BF16), preferred_element_type=F32)
    dest = jnp.sum(ohf * (loff + rank), axis=1, keepdims=True).astype(jnp.int32)
    gather(s, lambda cp: cp.wait())
    tc = ts // PERM_CHUNKS
    lrow = lax.broadcasted_iota(jnp.int32, (tc, LROWS), 1)
    for c in range(PERM_CHUNKS):
        rows = slice(c * tc, (c + 1) * tc)
        perm_t = jnp.where(lrow == dest[rows, :], 1.0, 0.0).astype(BF16)
        ffn = jnp.dot(perm_t, ol_ref[slot], preferred_element_type=F32)
        out_ref[rows, :] = _layer_norm(DEEPNORM_ALPHA * h_ref[rows, :] + ffn, g_ref[...], b_ref[...])


def _moe_unsort(plan, og, route, hf, lstrict, g2, b2):
    n, d = hf.shape
    ts = MOE_TS
    row = pl.BlockSpec((1, d), lambda s, *_: (0, 0))
    return pl.pallas_call(
        _moe_unsort_kernel,
        out_shape=jax.ShapeDtypeStruct((n, d), F32),
        grid_spec=pltpu.PrefetchScalarGridSpec(
            num_scalar_prefetch=1,
            grid=(n // ts,),
            in_specs=[pl.BlockSpec(memory_space=pl.ANY), pl.BlockSpec((ts, LANES), lambda s, *_: (s, 0)),
                      pl.BlockSpec((ts, d), lambda s, *_: (s, 0)), pl.BlockSpec((ts, ts), lambda s, *_: (0, 0)),
                      row, row],
            out_specs=pl.BlockSpec((ts, d), lambda s, *_: (s, 0)),
            scratch_shapes=[pltpu.VMEM((2, LROWS, d), BF16), pltpu.SemaphoreType.DMA((2,))]),
        compiler_params=pltpu.CompilerParams(dimension_semantics=("arbitrary",), vmem_limit_bytes=VMEM_LIMIT),
        name="moe_unsort",
    )(plan["gmap_back"], og, route, hf, lstrict, g2, b2)


def _moe_plan(cnt_half, n):
    nsrc = n // MOE_TS
    i32 = jnp.int32
    cnt = cnt_half.reshape(nsrc, -1, LANES).sum(axis=1)[:, :N_BUCKETS].astype(i32)
    run_g = (cnt + GRAN - 1) // GRAN
    nvalid = run_g.sum(axis=1)
    loff_g = jnp.cumsum(run_g, axis=1) - run_g
    bucket_g = run_g.sum(axis=0)
    gpt = MOE_TM // GRAN
    btiles = (bucket_g + gpt - 1) // gpt
    tend = jnp.cumsum(btiles)
    tstart = tend - btiles
    gofs = tstart[None, :] * gpt + jnp.cumsum(run_g, axis=0) - run_g
    n_tiles = -(-(n + nsrc * N_BUCKETS * (GRAN - 1)) // MOE_TM) + N_BUCKETS + 1
    g = jnp.arange(LGRAN, dtype=i32)[None, :, None]
    in_run = (g >= loff_g[:, None, :]) & (g < (loff_g + run_g)[:, None, :])
    gmap = jnp.arange(LGRAN, dtype=i32)[None, :] + jnp.sum(jnp.where(in_run, (gofs - loff_g)[:, None, :], 0), axis=2)
    is_valid = jnp.arange(LGRAN, dtype=i32)[None, :] < nvalid[:, None]
    zero_gran = (n_tiles - 1) * gpt
    t = jnp.arange(n_tiles, dtype=i32)
    tb = jnp.minimum(jnp.sum(t[:, None] >= tend[None, :], axis=1), N_BUCKETS - 1)
    valid = (t < tend[-1]).astype(i32)
    pairs = [(a, b) for a in range(EXPERTS_PER_GROUP) for b in range(a + 1, EXPERTS_PER_GROUP)]
    pidx = tb % N_PAIRS
    pair_a = sum(jnp.where(pidx == i, a, 0) for i, (a, _) in enumerate(pairs))
    pair_b = sum(jnp.where(pidx == i, b, 0) for i, (_, b) in enumerate(pairs))
    grp = tb // N_PAIRS
    return {
        "n_tiles": n_tiles,
        "gmap": jnp.where(is_valid, gmap, 0).reshape(-1).astype(i32),
        "gmap_back": jnp.where(is_valid, gmap, zero_gran).reshape(-1).astype(i32),
        "nvalid": nvalid.astype(i32),
        "tail0": jnp.concatenate([tstart * gpt + bucket_g, tend[-1:]]).astype(i32),
        "taillen": (btiles * gpt - bucket_g).astype(i32),
        "xtile": jnp.where(valid > 0, t, 0).astype(i32),
        "elo": (grp * EXPERTS_PER_GROUP + pair_a).astype(i32),
        "ehi": (grp * EXPERTS_PER_GROUP + pair_b).astype(i32),
        "valid": valid,
    }


def _pad_lanes(a, lane0=0):
    return jnp.zeros((1, LANES), F32).at[0, lane0:lane0 + a.shape[0]].set(a.astype(F32))


def _rope_tables(seq):
    half = HEAD_DIM // 2
    inv_freq = ROPE_THETA ** (-jnp.arange(half, dtype=F32) / half)
    ang = jnp.arange(seq).astype(F32)[:, None] * inv_freq[None, :]
    cos, sin = jnp.cos(ang), jnp.sin(ang)
    return jnp.concatenate([cos, cos], axis=-1), jnp.concatenate([-sin, sin], axis=-1)


def _layer(x, w_in, conv_w, a_log, dt_bias, dn_norm_w, w_out, ln1_g, ln1_b, router_w1, router_b1,
           router_w2, router_b2, w_gate, w_up, w_down, ln2_g, ln2_b):
    bsz, seq, d = x.shape
    n = bsz * seq
    x2 = x.reshape(n, d)

    o_z, o_b, o_mb = 3 * D_DN, 4 * D_DN, 4 * D_DN + 2 * N_HEADS_DN
    w_ba = jnp.pad(w_in[:, o_b:o_mb], ((0, 0), (0, LANES - 2 * N_HEADS_DN)))
    w_all = jnp.concatenate([w_in[:, :o_z], w_in[:, o_z:o_b], w_ba, w_in[:, o_mb:]], axis=1).astype(BF16)

    tm = min(512, n)
    dn_qkv, z, ba, mb_qkv = _in_proj(x2, w_all, conv_w, tm, seq)

    y_dn = _deltanet(dn_qkv.reshape(bsz, seq, 3 * D_DN), z.reshape(bsz, seq, D_DN), ba.reshape(bsz, seq, LANES),
                     _pad_lanes(a_log, N_HEADS_DN), _pad_lanes(dt_bias, N_HEADS_DN),
                     dn_norm_w.astype(F32).reshape(1, HEAD_DIM))

    cos_t, sin_t = _rope_tables(seq)
    q_r, k_r, vt, sel = _moba_prep(mb_qkv.reshape(bsz, seq, 3 * D_MOBA), cos_t, sin_t)
    y_mb = _moba_attn(q_r, k_r, vt, sel)

    rw = jnp.concatenate([router_w1, jnp.transpose(router_w2, (1, 0, 2)).reshape(d, N_EXPERTS)], axis=1)
    rw = jnp.pad(rw, ((0, 0), (0, LANES - rw.shape[1])))
    rb = _pad_lanes(jnp.concatenate([router_b1, router_b2.reshape(-1)]))
    hf, hb, route, cnt = _mix_route(y_dn.reshape(n, D_DN), y_mb.reshape(n, D_MOBA), x2, w_out.astype(BF16),
                                    ln1_g.reshape(1, d), ln1_b.reshape(1, d), rw, rb, min(2 * MOE_TS, n))

    plan = _moe_plan(cnt, n)
    idx = jnp.arange(MOE_TS, dtype=jnp.int32)
    lstrict = (idx[None, :] < idx[:, None]).astype(BF16)
    xg, wsg = _moe_sort(plan, hb, route, lstrict)
    og = _moe_experts(plan, xg, wsg, w_gate, w_up, w_down)
    out = _moe_unsort(plan, og, route, hf, lstrict, ln2_g.reshape(1, d), ln2_b.reshape(1, d))
    return out.reshape(bsz, seq, d)


def kernel(x, w_in, conv_w, a_log, dt_bias, dn_norm_w, w_out, ln1_g, ln1_b, router_w1, router_b1, router_w2, router_b2, expert_w_gate, expert_w_up, expert_w_down, ln2_g, ln2_b):
    for l in range(DEPTH):
        x = _layer(x, w_in[l], conv_w[l], a_log[l], dt_bias[l], dn_norm_w[l], w_out[l], ln1_g[l], ln1_b[l],
                   router_w1[l], router_b1[l], router_w2[l], router_b2[l], expert_w_gate[l], expert_w_up[l],
                   expert_w_down[l], ln2_g[l], ln2_b[l])
    return x
```

```python
import functools

import jax
import jax.numpy as jnp
from jax import lax
from jax.experimental import pallas as pl
from jax.experimental.pallas import tpu as pltpu

F32 = jnp.float32
BF16 = jnp.bfloat16

HEAD_DIM = 128
N_HEADS_DN = 4
N_HEADS_MOBA = 4
D_DN = N_HEADS_DN * HEAD_DIM
D_MOBA = N_HEADS_MOBA * HEAD_DIM
CONV_K = 4
DN_CHUNK = 64
MOBA_BLOCK = 256
MOBA_TOPK = 3
MOBA_UNROLL = 4
PREP_BLOCKS = 2
ROPE_THETA = 10000.0
N_GROUPS = 4
EXPERTS_PER_GROUP = 4
N_EXPERTS = N_GROUPS * EXPERTS_PER_GROUP
D_EXPERT = 256
LN_EPS = 1e-5
RMS_EPS = 1e-6
L2_EPS = 1e-6
NEG_INF = -1e30
LOG2E = 1.4426950408889634
DEPTH = 1
DEEPNORM_ALPHA = (2 * DEPTH) ** 0.25

LANES = 128
DN_TILE = 256
DN_HEADS_PER_STEP = 4
DN_SEQS_PER_STEP = 2
GATE_LANE0 = N_GROUPS
N_PAIRS = EXPERTS_PER_GROUP * (EXPERTS_PER_GROUP - 1) // 2
N_BUCKETS = N_GROUPS * N_PAIRS
ROUTE_BUCKET, ROUTE_W_LO, ROUTE_W_HI = 0, 1, 2
MOE_TS = 512
MOE_TM = 512
GRAN = 16
GRAN_SHIFT = 4
LROWS = -(-(MOE_TS + N_BUCKETS * (GRAN - 1)) // LANES) * LANES
LGRAN = LROWS // GRAN
PERM_CHUNKS = 4
VMEM_LIMIT = 48 * 1024 * 1024


def _dot(a, b):
    return jnp.dot(a.astype(BF16), b.astype(BF16), preferred_element_type=F32)


def _dot_nt(a, b):
    return lax.dot_general(a.astype(BF16), b.astype(BF16), (((1,), (1,)), ((), ())),
                           preferred_element_type=F32)


def _split2(a):
    hi = a.astype(BF16)
    lo = (a - hi.astype(F32)).astype(BF16)
    return hi, lo


def _split3(a):
    hi = a.astype(BF16)
    r = a - hi.astype(F32)
    mid = r.astype(BF16)
    lo = (r - mid.astype(F32)).astype(BF16)
    return hi, mid, lo


def _dot3(a, b):
    ah, al = _split2(a)
    bh, bl = _split2(b)
    return (jnp.dot(ah, bh, preferred_element_type=F32) + jnp.dot(ah, bl, preferred_element_type=F32)
            + jnp.dot(al, bh, preferred_element_type=F32))


def _dot3_nt(a, b):
    ah, al = _split2(a)
    bh, bl = _split2(b)
    dn = (((1,), (1,)), ((), ()))
    return (lax.dot_general(ah, bh, dn, preferred_element_type=F32)
            + lax.dot_general(ah, bl, dn, preferred_element_type=F32)
            + lax.dot_general(al, bh, dn, preferred_element_type=F32))


def _dot_exact_lhs(a_bf16, b):
    bh, bm, bl = _split3(b)
    return (jnp.dot(a_bf16, bh, preferred_element_type=F32) + jnp.dot(a_bf16, bm, preferred_element_type=F32)
            + jnp.dot(a_bf16, bl, preferred_element_type=F32))


def _silu(x):
    return x * jax.nn.sigmoid(x)


def _softplus(x):
    return jnp.maximum(x, 0.0) + jnp.log1p(jnp.exp(-jnp.abs(x)))


def _layer_norm(t, g, b):
    mu = jnp.mean(t, axis=-1, keepdims=True)
    d = t - mu
    var = jnp.mean(d * d, axis=-1, keepdims=True)
    return d * lax.rsqrt(var + LN_EPS) * g + b


def _lane_pick(x, lane):
    ids = lax.broadcasted_iota(jnp.int32, x.shape, 1)
    return jnp.sum(jnp.where(ids == lane, x, 0.0), axis=1, keepdims=True)


def _in_proj_kernel(x_ref, w_ref, cw_ref, dn_ref, z_ref, ba_ref, mb_ref, cb_ref, *, tiles_per_seq):
    i = pl.program_id(0)
    tm = x_ref.shape[0]
    o0 = 3 * D_DN
    o1 = o0 + D_DN
    o2 = o1 + LANES

    @pl.when(i % tiles_per_seq == 0)
    def _():
        cb_ref[0:8, :] = jnp.zeros((8, o0), F32)

    xb = x_ref[...].astype(BF16)
    for part in range(3):
        cs = slice(part * D_DN, (part + 1) * D_DN)
        u = jnp.dot(xb, w_ref[:, cs], preferred_element_type=F32)
        cb_ref[8:8 + tm, cs] = u
        acc = cw_ref[CONV_K - 1:CONV_K, cs] * u
        for s in range(1, CONV_K):
            acc = acc + cw_ref[CONV_K - 1 - s:CONV_K - s, cs] * cb_ref[8 - s:8 - s + tm, cs]
        cb_ref[0:8, cs] = u[tm - 8:tm, :]
        act = _silu(acc)
        if part < 2:
            outs = []
            for h in range(N_HEADS_DN):
                t = act[:, h * HEAD_DIM:(h + 1) * HEAD_DIM]
                t = t * lax.rsqrt(jnp.sum(t * t, axis=-1, keepdims=True) + L2_EPS)
                outs.append(t * (HEAD_DIM ** -0.5) if part == 0 else t)
            act = jnp.concatenate(outs, axis=1)
        dn_ref[:, cs] = act
    z_ref[...] = _silu(jnp.dot(xb, w_ref[:, o0:o1], preferred_element_type=F32))
    ba_ref[...] = jnp.dot(xb, w_ref[:, o1:o2], preferred_element_type=F32)
    mb_ref[...] = jnp.dot(xb, w_ref[:, o2:o2 + 3 * D_MOBA], preferred_element_type=F32).astype(mb_ref.dtype)


def _in_proj(x2, w_all, conv_w, tm, seq):
    n, d = x2.shape
    wc = w_all.shape[1]
    return pl.pallas_call(
        functools.partial(_in_proj_kernel, tiles_per_seq=seq // tm),
        out_shape=(jax.ShapeDtypeStruct((n, 3 * D_DN), F32), jax.ShapeDtypeStruct((n, D_DN), F32),
                   jax.ShapeDtypeStruct((n, LANES), F32), jax.ShapeDtypeStruct((n, 3 * D_MOBA), BF16)),
        grid=(n // tm,),
        in_specs=[pl.BlockSpec((tm, d), lambda i: (i, 0)), pl.BlockSpec((d, wc), lambda i: (0, 0)),
                  pl.BlockSpec((CONV_K, 3 * D_DN), lambda i: (0, 0))],
        out_specs=(pl.BlockSpec((tm, 3 * D_DN), lambda i: (i, 0)), pl.BlockSpec((tm, D_DN), lambda i: (i, 0)),
                   pl.BlockSpec((tm, LANES), lambda i: (i, 0)), pl.BlockSpec((tm, 3 * D_MOBA), lambda i: (i, 0))),
        scratch_shapes=[pltpu.VMEM((8 + tm, 3 * D_DN), F32)],
        compiler_params=pltpu.CompilerParams(dimension_semantics=("arbitrary",), vmem_limit_bytes=VMEM_LIMIT),
        name="in_proj",
    )(x2, w_all, conv_w)


def _deltanet_kernel(q_ref, k_ref, v_ref, z_ref, ba_ref, alog_ref, dtb_ref, normw_ref, y_ref,
                     s_ref, wq_s, u_s, qk_s, kdt_s, egl_s, *, hb):
    hg = pl.program_id(1)
    t = pl.program_id(2)
    tt = DN_TILE
    nchunk = tt // DN_CHUNK
    nbat = q_ref.shape[0]
    hs = range(nbat * hb)

    @pl.when(t == 0)
    def _():
        for ref in (s_ref, wq_s, u_s, qk_s, kdt_s, egl_s):
            ref[...] = jnp.zeros_like(ref)

    rd = t & 1
    wr = 1 - rd
    state = [s_ref[h] for h in hs]
    outs = [[] for _ in hs]

    pend = {}

    def chain_a(c):
        pend["r"] = [jnp.dot(wq_s[rd, h, c], state[h].astype(BF16), preferred_element_type=F32) for h in hs]

    def chain_b(c):
        lo, hi = c * DN_CHUNK, (c + 1) * DN_CHUNK
        r = pend["r"]
        vz = []
        for h in hs:
            parts = []
            if lo > 0:
                parts.append(jnp.zeros((lo, HEAD_DIM), F32))
            parts.append(u_s[rd, h, lo:hi, :] - r[h][0:DN_CHUNK, :])
            if hi < tt:
                parts.append(jnp.zeros((tt - hi, HEAD_DIM), F32))
            vz.append(jnp.concatenate(parts, axis=0).astype(BF16))
        for h in hs:
            outs[h].append(r[h][DN_CHUNK:2 * DN_CHUNK, :]
                           + jnp.dot(qk_s[rd, h, lo:hi, :], vz[h], preferred_element_type=F32))
        for h in hs:
            state[h] = (state[h] * egl_s[rd, h, 8 * c:8 * c + 1, :]
                        + jnp.dot(kdt_s[rd, h], vz[h], preferred_element_type=F32))

    chain_a(0)

    bas = [ba_ref[bb] for bb in range(nbat)]
    beta_all = [jax.nn.sigmoid(ba) for ba in bas]
    g_all = [-jnp.exp(alog_ref[...]) * _softplus(ba + dtb_ref[...]) for ba in bas]

    row = lax.broadcasted_iota(jnp.int32, (tt, tt), 0)
    col = lax.broadcasted_iota(jnp.int32, (tt, tt), 1)
    same = (row >> 6) == (col >> 6)
    incl = same & (row >= col)
    strict = same & (row > col)

    incl_b = incl.astype(BF16)
    gc_all = [_dot_exact_lhs(incl_b, g) for g in g_all]
    gct = [jnp.transpose(g) for g in gc_all]
    sub = lax.broadcasted_iota(jnp.int32, gct[0].shape, 0)
    bat = [vh // hb for vh in hs]
    sls = [slice((vh % hb) * HEAD_DIM, (vh % hb + 1) * HEAD_DIM) for vh in hs]
    heads = [hg * hb + vh % hb for vh in hs]
    q = [q_ref[bat[vh], :, sls[vh]] for vh in hs]
    k = [k_ref[bat[vh], :, sls[vh]] for vh in hs]
    v = [v_ref[bat[vh], :, sls[vh]] for vh in hs]
    beta = [_lane_pick(beta_all[bat[vh]], heads[vh]) for vh in hs]
    gcc = [_lane_pick(gc_all[bat[vh]], heads[vh] + N_HEADS_DN) for vh in hs]
    gcr = [jnp.sum(jnp.where(sub == heads[vh] + N_HEADS_DN, gct[bat[vh]], 0.0), axis=0, keepdims=True)
           for vh in hs]
    chain_b(0)

    decay = [jnp.where(incl, jnp.exp(jnp.where(incl, gcc[h] - gcr[h], 0.0)), 0.0) for h in hs]
    kb = [k[h] * beta[h] for h in hs]
    vb = [v[h] * beta[h] for h in hs]
    a_mat = [jnp.where(strict, _dot_nt(kb[h], k[h]) * decay[h], 0.0) for h in hs]
    chain_a(1)
    qk = [_dot_nt(q[h], k[h]) * decay[h] for h in hs]
    eye = (row == col).astype(F32)
    d8 = (row >> 3) == (col >> 3)
    a8 = [jnp.where(d8, a, 0.0) for a in a_mat]
    chain_b(1)
    a8_2 = [_dot(a, a) for a in a8]
    chain_a(2)
    a8_4 = [_dot(a, a) for a in a8_2]
    chain_b(2)
    x = [_dot(eye - a, eye + a2) for a, a2 in zip(a8, a8_2)]
    chain_a(3)
    x = [_dot(xi, eye + a4) for xi, a4 in zip(x, a8_4)]
    chain_b(3)
    s = 8
    while s < DN_CHUNK:
        sh = s.bit_length() - 1
        off = ((row >> (sh + 1)) == (col >> (sh + 1))) & ((row >> sh) != (col >> sh))
        y = [_dot(jnp.where(off, a, 0.0), xi) for a, xi in zip(a_mat, x)]
        x = [xi - _dot(xi, yi) for xi, yi in zip(x, y)]
        s *= 2
    tinv = x
    eg = [jnp.exp(g) for g in gcc]
    wu = [_dot(tinv[h], jnp.concatenate([kb[h] * eg[h], vb[h]], axis=1)) for h in hs]
    qd = [q[h] * eg[h] for h in hs]
    gl_rows = [[g[(c + 1) * DN_CHUNK - 1:(c + 1) * DN_CHUNK, :] for c in range(nchunk)] for g in gcc]
    gl_col = [jnp.concatenate([jnp.broadcast_to(g, (DN_CHUNK, 1)) for g in rows], axis=0) for rows in gl_rows]
    kdt = [jnp.transpose(k[h] * jnp.exp(gl_col[h] - gcc[h])) for h in hs]

    ys = []
    for h in hs:
        o = jnp.concatenate(outs[h], axis=0)
        o = o * lax.rsqrt(jnp.mean(o * o, axis=-1, keepdims=True) + RMS_EPS) * normw_ref[...]
        ys.append(o * z_ref[bat[h], :, sls[h]])
    for bb in range(nbat):
        y_ref[bb] = jnp.concatenate(ys[bb * hb:(bb + 1) * hb], axis=1).astype(y_ref.dtype)
    s_ref[...] = jnp.stack(state, axis=0)

    for h in hs:
        for c in range(nchunk):
            lo, hi = c * DN_CHUNK, (c + 1) * DN_CHUNK
            wq_s[wr, h, c] = jnp.concatenate([wu[h][lo:hi, 0:HEAD_DIM], qd[h][lo:hi, :]], axis=0).astype(BF16)
            egl_s[wr, h, 8 * c:8 * c + 8, :] = jnp.broadcast_to(jnp.exp(gl_rows[h][c]), (8, HEAD_DIM))
        u_s[wr, h] = wu[h][:, HEAD_DIM:2 * HEAD_DIM]
        qk_s[wr, h] = qk[h].astype(BF16)
        kdt_s[wr, h] = kdt[h].astype(BF16)


def _deltanet(dn_qkv, z, ba, alog_row, dtb_row, normw_row):
    bsz, seq, _ = dn_qkv.shape
    tt = DN_TILE
    nt = seq // tt
    hb = DN_HEADS_PER_STEP
    nbat = DN_SEQS_PER_STEP if bsz % DN_SEQS_PER_STEP == 0 else 1
    nv = nbat * hb
    ng = N_HEADS_DN // hb
    w = hb * HEAD_DIM
    nchunk = tt // DN_CHUNK

    def cur_spec(off, width):
        return pl.BlockSpec((nbat, tt, width), lambda b, g, t: (b, jnp.minimum(t, nt - 1), g + off))

    prev_spec = pl.BlockSpec((nbat, tt, w), lambda b, g, t: (b, jnp.maximum(t - 1, 0), g))
    row_spec = pl.BlockSpec((1, LANES), lambda b, g, t: (0, 0))
    return pl.pallas_call(
        functools.partial(_deltanet_kernel, hb=hb),
        out_shape=jax.ShapeDtypeStruct((bsz, seq, D_DN), BF16),
        grid=(bsz // nbat, ng, nt + 1),
        in_specs=[cur_spec(0, w), cur_spec(ng, w), cur_spec(2 * ng, w), prev_spec,
                  pl.BlockSpec((nbat, tt, LANES), lambda b, g, t: (b, jnp.minimum(t, nt - 1), 0)),
                  row_spec, row_spec, row_spec],
        out_specs=prev_spec,
        scratch_shapes=[pltpu.VMEM((nv, HEAD_DIM, HEAD_DIM), F32),
                        pltpu.VMEM((2, nv, nchunk, 2 * DN_CHUNK, HEAD_DIM), BF16),
                        pltpu.VMEM((2, nv, tt, HEAD_DIM), F32),
                        pltpu.VMEM((2, nv, tt, tt), BF16),
                        pltpu.VMEM((2, nv, HEAD_DIM, tt), BF16),
                        pltpu.VMEM((2, nv, 8 * nchunk, HEAD_DIM), F32)],
        compiler_params=pltpu.CompilerParams(dimension_semantics=("parallel", "parallel", "arbitrary"),
                                             vmem_limit_bytes=VMEM_LIMIT),
        name="deltanet",
    )(dn_qkv, dn_qkv, dn_qkv, z, ba, alog_row, dtb_row, normw_row)


def _moba_prep_kernel(x_ref, cos_ref, sin_ref, q_ref, k_ref, vt_ref, sel_ref, km_ref, *, nb, topk, spb):
    step = pl.program_id(1)

    @pl.when(step == 0)
    def _():
        km_ref[...] = jnp.zeros_like(km_ref)

    half = HEAD_DIM // 2
    blk = lax.broadcasted_iota(jnp.int32, (nb, MOBA_BLOCK), 0)
    kmrow = lax.broadcasted_iota(jnp.int32, (nb, D_MOBA), 0)
    km = km_ref[...]
    for sb in range(spb):
        j = step * spb + sb
        rs = slice(sb * MOBA_BLOCK, (sb + 1) * MOBA_BLOCK)
        cos = cos_ref[rs, :]
        sin = sin_ref[rs, :]
        kmeans = []
        for h in range(N_HEADS_MOBA):
            hs = slice(h * HEAD_DIM, (h + 1) * HEAD_DIM)
            qh = x_ref[0, rs, hs].astype(F32)
            kh = x_ref[0, rs, D_MOBA + h * HEAD_DIM:D_MOBA + (h + 1) * HEAD_DIM].astype(F32)
            qr = (qh * cos + pltpu.roll(qh, half, 1) * sin) * (HEAD_DIM ** -0.5)
            kr = kh * cos + pltpu.roll(kh, half, 1) * sin
            q_ref[0, rs, hs] = (qr * LOG2E).astype(q_ref.dtype)
            k_ref[0, rs, hs] = kr.astype(k_ref.dtype)
            kmeans.append(jnp.mean(kr, axis=0, keepdims=True))

            gate = _dot3_nt(km[:, hs], qr)
            gate = jnp.where(blk < j, gate, NEG_INF)
            rank = jnp.zeros(gate.shape, F32)
            for m in range(nb):
                gm = gate[m:m + 1, :]
                ahead = (gm > gate) | ((gm == gate) & (blk > m))
                rank = rank + jnp.where(ahead, 1.0, 0.0)
            sel = (blk < j) & (rank < topk)
            sel_ref[0, sb, h * nb:(h + 1) * nb, :] = jnp.where(sel, 1.0, 0.0)

        km = jnp.where(kmrow == j, jnp.concatenate(kmeans, axis=1), km)
        vt_ref[0, sb] = jnp.transpose(x_ref[0, rs, 2 * D_MOBA:3 * D_MOBA].astype(F32)).astype(vt_ref.dtype)
    km_ref[...] = km


def _moba_prep(mb_qkv, cos_t, sin_t):
    bsz, seq, _ = mb_qkv.shape
    nb = seq // MOBA_BLOCK
    topk = min(MOBA_TOPK, nb)
    spb = PREP_BLOCKS if nb % PREP_BLOCKS == 0 else 1
    rows = spb * MOBA_BLOCK
    kern = functools.partial(_moba_prep_kernel, nb=nb, topk=topk, spb=spb)
    tok_spec = pl.BlockSpec((1, rows, D_MOBA), lambda b, j: (b, j, 0))
    tab_spec = pl.BlockSpec((rows, HEAD_DIM), lambda b, j: (j, 0))
    return pl.pallas_call(
        kern,
        out_shape=(jax.ShapeDtypeStruct((bsz, seq, D_MOBA), BF16), jax.ShapeDtypeStruct((bsz, seq, D_MOBA), BF16),
                   jax.ShapeDtypeStruct((bsz, nb, D_MOBA, MOBA_BLOCK), BF16),
                   jax.ShapeDtypeStruct((bsz, nb, N_HEADS_MOBA * nb, MOBA_BLOCK), F32)),
        grid=(bsz, nb // spb),
        in_specs=[pl.BlockSpec((1, rows, 3 * D_MOBA), lambda b, j: (b, j, 0)), tab_spec, tab_spec],
        out_specs=(tok_spec, tok_spec,
                   pl.BlockSpec((1, spb, D_MOBA, MOBA_BLOCK), lambda b, j: (b, j, 0, 0)),
                   pl.BlockSpec((1, spb, N_HEADS_MOBA * nb, MOBA_BLOCK), lambda b, j: (b, j, 0, 0))),
        scratch_shapes=[pltpu.VMEM((nb, D_MOBA), F32)],
        compiler_params=pltpu.CompilerParams(dimension_semantics=("parallel", "arbitrary"),
                                             vmem_limit_bytes=VMEM_LIMIT),
        name="moba_prep",
    )(mb_qkv, cos_t, sin_t)


def _moba_attn_kernel(q_ref, k_ref, vt_ref, sel_ref, o_ref, acc_ref, *, nb):
    j = pl.program_id(1)
    blk = MOBA_BLOCK
    nh = N_HEADS_MOBA
    dn = (((1,), (1,)), ((), ()))
    hsl = [slice(h * HEAD_DIM, (h + 1) * HEAD_DIM) for h in range(nh)]
    qs = [q_ref[0, :, hsl[h]] for h in range(nh)]

    ki = lax.broadcasted_iota(jnp.int32, (blk, blk), 0)
    qi = lax.broadcasted_iota(jnp.int32, (blk, blk), 1)
    own = pl.ds(pl.multiple_of(j * blk, blk), blk)
    s_own = [jnp.where(ki <= qi, lax.dot_general(k_ref[0, own, hsl[h]], qs[h], dn, preferred_element_type=F32),
                       NEG_INF) for h in range(nh)]

    def scores(n, h):
        kn = k_ref[0, pl.ds(pl.multiple_of(n * blk, blk), blk), hsl[h]]
        s = lax.dot_general(kn, qs[h], dn, preferred_element_type=F32)
        return jnp.where(sel_ref[0, 0, pl.ds(h * nb + n, 1), :] > 0.5, s, NEG_INF)

    ones8 = jnp.ones((8, blk), BF16)

    def softmax_pv(n, h, s, m, l):
        m_new = jnp.maximum(m, jnp.max(s, axis=0, keepdims=True))
        pb = jnp.exp2(s - m_new).astype(BF16)
        psum = jnp.dot(ones8, pb, preferred_element_type=F32)[0:1, :]
        pv = jnp.dot(vt_ref[0, n, hsl[h], :], pb, preferred_element_type=F32)
        return m_new, psum, pv

    ms, ls = [], []
    for h in range(nh):
        m, psum, pv = softmax_pv(j, h, s_own[h], jnp.full((1, blk), NEG_INF, F32), None)
        ms.append(m)
        ls.append(psum)
        acc_ref[h] = pv

    def body(step, carry):
        ms, ls = (list(t) for t in carry)
        n0 = MOBA_UNROLL * step
        cur = [scores(n0, h) for h in range(nh)]
        for i in range(MOBA_UNROLL):
            nxt = []
            for h in range(nh):
                if i + 1 < MOBA_UNROLL:
                    nxt.append(scores(n0 + i + 1, h))
                m_new, psum, pv = softmax_pv(n0 + i, h, cur[h], ms[h], ls[h])
                alpha = jnp.exp2(ms[h] - m_new)
                ls[h] = alpha * ls[h] + psum
                ms[h] = m_new
                acc_ref[h] = acc_ref[h] * alpha + pv
            cur = nxt
        return tuple(ms), tuple(ls)

    ms, ls = lax.fori_loop(0, (j + MOBA_UNROLL - 1) // MOBA_UNROLL, body, (tuple(ms), tuple(ls)))
    o_ref[0] = jnp.concatenate([jnp.transpose(acc_ref[h] / ls[h]) for h in range(nh)],
                               axis=1).astype(o_ref.dtype)


def _moba_attn(q_r, k_r, vt, sel):
    bsz, seq, _ = q_r.shape
    nb = seq // MOBA_BLOCK
    assert nb % MOBA_UNROLL == 0, (nb, MOBA_UNROLL)
    tok_spec = pl.BlockSpec((1, MOBA_BLOCK, D_MOBA), lambda b, j: (b, j, 0))
    return pl.pallas_call(
        functools.partial(_moba_attn_kernel, nb=nb),
        out_shape=jax.ShapeDtypeStruct((bsz, seq, D_MOBA), BF16),
        grid=(bsz, nb),
        in_specs=[tok_spec,
                  pl.BlockSpec((1, seq, D_MOBA), lambda b, j: (b, 0, 0)),
                  pl.BlockSpec((1, nb, D_MOBA, MOBA_BLOCK), lambda b, j: (b, 0, 0, 0)),
                  pl.BlockSpec((1, 1, N_HEADS_MOBA * nb, MOBA_BLOCK), lambda b, j: (b, j, 0, 0))],
        out_specs=tok_spec,
        scratch_shapes=[pltpu.VMEM((N_HEADS_MOBA, HEAD_DIM, MOBA_BLOCK), F32)],
        compiler_params=pltpu.CompilerParams(dimension_semantics=("parallel", "arbitrary"),
                                             vmem_limit_bytes=VMEM_LIMIT),
        name="moba_attn",
    )(q_r, k_r, vt, sel)


def _route_record(logits):
    lane = lax.broadcasted_iota(jnp.int32, logits.shape, 1)
    big = jnp.int32(LANES)

    def first_lane(mask):
        return jnp.min(jnp.where(mask, lane, big), axis=1, keepdims=True)

    is_g = lane < N_GROUPS
    m1 = jnp.max(jnp.where(is_g, logits, NEG_INF), axis=1, keepdims=True)
    s1 = jnp.sum(jnp.where(is_g, jnp.exp(logits - m1), 0.0), axis=1, keepdims=True)
    pg = 1.0 / s1
    gsel = first_lane(is_g & (logits == m1))

    in_grp = (lane >= GATE_LANE0) & (((lane - GATE_LANE0) >> 2) == gsel) & (lane < GATE_LANE0 + N_EXPERTS)
    m2 = jnp.max(jnp.where(in_grp, logits, NEG_INF), axis=1, keepdims=True)
    s2 = jnp.sum(jnp.where(in_grp, jnp.exp(logits - m2), 0.0), axis=1, keepdims=True)
    e1 = first_lane(in_grp & (logits == m2))
    rest = in_grp & (lane != e1)
    m2b = jnp.max(jnp.where(rest, logits, NEG_INF), axis=1, keepdims=True)
    e2 = first_lane(rest & (logits == m2b))
    pe1 = 1.0 / s2
    pe2 = jnp.exp(m2b - m2) / s2
    tot = pe1 + pe2
    w1 = pg * (pe1 / tot)
    w2 = pg * (pe2 / tot)
    first_lo = e1 < e2
    lo = jnp.minimum(e1, e2)
    hi = jnp.maximum(e1, e2)
    a = (lo - GATE_LANE0) & (EXPERTS_PER_GROUP - 1)
    b = (hi - GATE_LANE0) & (EXPERTS_PER_GROUP - 1)
    bucket = gsel * N_PAIRS + ((a * (2 * EXPERTS_PER_GROUP - 1 - a)) >> 1) + (b - a - 1)
    record = jnp.where(lane == ROUTE_BUCKET, bucket.astype(F32),
                       jnp.where(lane == ROUTE_W_LO, jnp.where(first_lo, w1, w2),
                                 jnp.where(lane == ROUTE_W_HI, jnp.where(first_lo, w2, w1), 0.0)))
    return record, jnp.sum(jnp.where(lane == bucket, 1.0, 0.0), axis=0, keepdims=True)


def _mix_route_kernel(ydn_ref, ymb_ref, x_ref, wo_ref, g_ref, b_ref, rw_ref, rb_ref, h_ref, hb_ref, route_ref,
                      cnt_ref):
    nsub = cnt_ref.shape[0]
    rows = [slice(i * MOE_TS, (i + 1) * MOE_TS) for i in range(nsub)]
    wh, wl = _split2(rw_ref[...])
    wcat = jnp.concatenate([wh, wl], axis=1)

    mixes = [jnp.dot(jnp.concatenate([ydn_ref[r, :], ymb_ref[r, :]], axis=1), wo_ref[...],
                     preferred_element_type=F32) for r in rows]
    logits = []
    for r, mix in zip(rows, mixes):
        hval = _layer_norm(DEEPNORM_ALPHA * x_ref[r, :] + mix, g_ref[...], b_ref[...])
        h_ref[r, :] = hval
        hb_ref[r, :] = hval.astype(BF16)
        hh, hl = _split2(hval)
        both = jnp.dot(hh, wcat, preferred_element_type=F32)
        logits.append(both[:, 0:LANES] + both[:, LANES:2 * LANES]
                      + jnp.dot(hl, wh, preferred_element_type=F32) + rb_ref[...])
    for i, r in enumerate(rows):
        record, counts = _route_record(logits[i])
        route_ref[r, :] = record
        cnt_ref[i] = counts


def _mix_route(y_dn, y_mb, x2, wo, g1, b1, rw, rb, tm):
    n, d = x2.shape
    nsub = tm // MOE_TS
    row = lambda w: pl.BlockSpec((1, w), lambda i: (0, 0))
    return pl.pallas_call(
        _mix_route_kernel,
        out_shape=(jax.ShapeDtypeStruct((n, d), F32), jax.ShapeDtypeStruct((n, d), BF16),
                   jax.ShapeDtypeStruct((n, LANES), F32), jax.ShapeDtypeStruct((n // MOE_TS, 1, LANES), F32)),
        grid=(n // tm,),
        in_specs=[pl.BlockSpec((tm, D_DN), lambda i: (i, 0)), pl.BlockSpec((tm, D_MOBA), lambda i: (i, 0)),
                  pl.BlockSpec((tm, d), lambda i: (i, 0)), pl.BlockSpec((D_DN + D_MOBA, d), lambda i: (0, 0)),
                  row(d), row(d), pl.BlockSpec((d, LANES), lambda i: (0, 0)), row(LANES)],
        out_specs=(pl.BlockSpec((tm, d), lambda i: (i, 0)), pl.BlockSpec((tm, d), lambda i: (i, 0)),
                   pl.BlockSpec((tm, LANES), lambda i: (i, 0)), pl.BlockSpec((nsub, 1, LANES), lambda i: (i, 0, 0))),
        compiler_params=pltpu.CompilerParams(dimension_semantics=("parallel",), vmem_limit_bytes=VMEM_LIMIT),
        name="mix_route",
    )(y_dn, y_mb, x2, wo, g1, b1, rw, rb)


def _bucket_offsets_col(ohf):
    cnt = jnp.sum(ohf, axis=1, keepdims=True).astype(jnp.int32)
    pad = (((cnt + (GRAN - 1)) >> GRAN_SHIFT) << GRAN_SHIFT).astype(F32)
    r = lax.broadcasted_iota(jnp.int32, (LANES, LANES), 0)
    c = lax.broadcasted_iota(jnp.int32, (LANES, LANES), 1)
    before = jnp.where(c < r, 1.0, 0.0)
    return _dot(before, jnp.broadcast_to(pad, (LANES, LANES)))[:, 0:1]


def _moe_sort_kernel(gmap_ref, nvalid_ref, tail0_ref, taillen_ref, hb_ref, route_ref, lstrict_ref,
                     xg_ref, wsg_ref, xs_ref, ws_ref, zx_ref, zw_ref, sem):
    s = pl.program_id(0)
    nsteps = pl.num_programs(0)
    slot = s & 1
    ts = route_ref.shape[0]
    route = route_ref[...]
    rt = jnp.transpose(route)
    bucket_row = rt[ROUTE_BUCKET:ROUTE_BUCKET + 1, :].astype(jnp.int32)
    sub = lax.broadcasted_iota(jnp.int32, (LANES, ts), 0)
    ohf = jnp.where(sub == bucket_row, 1.0, 0.0)
    loff = _bucket_offsets_col(ohf)
    rank = lax.dot_general(ohf.astype(BF16), lstrict_ref[...], (((1,), (1,)), ((), ())),
                           preferred_element_type=F32)
    dest = jnp.sum(ohf * (loff + rank), axis=0, keepdims=True).astype(jnp.int32)
    rh, rl = _split2(route)
    wcat = jnp.concatenate([rh, rl], axis=1)
    rc = LROWS // PERM_CHUNKS
    for c in range(PERM_CHUNKS):
        rowi = lax.broadcasted_iota(jnp.int32, (rc, ts), 0) + c * rc
        perm = jnp.where(rowi == dest, 1.0, 0.0).astype(BF16)
        xs_ref[slot, c * rc:(c + 1) * rc, :] = jnp.dot(perm, hb_ref[...], preferred_element_type=F32).astype(BF16)
        wparts = jnp.dot(perm, wcat, preferred_element_type=F32)
        ws_ref[slot, c * rc:(c + 1) * rc, :] = wparts[:, 0:LANES] + wparts[:, LANES:2 * LANES]

    def copies(step, g):
        sl = step & 1
        src = pl.ds(pl.multiple_of(g * GRAN, GRAN), GRAN)
        dst = pl.ds(pl.multiple_of(gmap_ref[step * LGRAN + g] * GRAN, GRAN), GRAN)
        return (pltpu.make_async_copy(xs_ref.at[sl, src, :], xg_ref.at[dst, :], sem.at[0, sl]),
                pltpu.make_async_copy(ws_ref.at[sl, src, :], wsg_ref.at[dst, :], sem.at[1, sl]))

    def fill_copies(b, i):
        dst = pl.ds(pl.multiple_of((tail0_ref[b] + i) * GRAN, GRAN), GRAN)
        return (pltpu.make_async_copy(zx_ref.at[0:GRAN, :], xg_ref.at[dst, :], sem.at[2, 0]),
                pltpu.make_async_copy(zw_ref.at[0:GRAN, :], wsg_ref.at[dst, :], sem.at[2, 1]))

    def unused_tile_copies(t):
        dst = pl.ds(pl.multiple_of(t * MOE_TM, MOE_TM), MOE_TM)
        return (pltpu.make_async_copy(zx_ref, xg_ref.at[dst, :], sem.at[2, 0]),
                pltpu.make_async_copy(zw_ref, wsg_ref.at[dst, :], sem.at[2, 1]))

    def run(step, fn):
        def body(g, carry):
            for cp in copies(step, g):
                fn(cp)
            return carry
        lax.fori_loop(0, nvalid_ref[step], body, 0)

    def run_fill(fn):
        for b in range(N_BUCKETS):
            def body(i, carry, b=b):
                for cp in fill_copies(b, i):
                    fn(cp)
                return carry
            lax.fori_loop(0, taillen_ref[b], body, 0)

        def tile_body(t, carry):
            for cp in unused_tile_copies(t):
                fn(cp)
            return carry
        lax.fori_loop(tail0_ref[N_BUCKETS], xg_ref.shape[0] // MOE_TM, tile_body, 0)

    @pl.when(s == 0)
    def _():
        zx_ref[...] = jnp.zeros_like(zx_ref)
        zw_ref[...] = jnp.zeros_like(zw_ref)
        run_fill(lambda cp: cp.start())

    run(s, lambda cp: cp.start())

    @pl.when(s > 0)
    def _():
        run(s - 1, lambda cp: cp.wait())

    @pl.when(s == nsteps - 1)
    def _():
        run(s, lambda cp: cp.wait())
        run_fill(lambda cp: cp.wait())


def _moe_sort(plan, hb, route, lstrict):
    n, d = hb.shape
    ts = MOE_TS
    rows = plan["n_tiles"] * MOE_TM
    return pl.pallas_call(
        _moe_sort_kernel,
        out_shape=(jax.ShapeDtypeStruct((rows, d), BF16), jax.ShapeDtypeStruct((rows, LANES), F32)),
        grid_spec=pltpu.PrefetchScalarGridSpec(
            num_scalar_prefetch=4,
            grid=(n // ts,),
            in_specs=[pl.BlockSpec((ts, d), lambda s, *_: (s, 0)), pl.BlockSpec((ts, LANES), lambda s, *_: (s, 0)),
                      pl.BlockSpec((ts, ts), lambda s, *_: (0, 0))],
            out_specs=(pl.BlockSpec(memory_space=pl.ANY), pl.BlockSpec(memory_space=pl.ANY)),
            scratch_shapes=[pltpu.VMEM((2, LROWS, d), BF16), pltpu.VMEM((2, LROWS, LANES), F32),
                            pltpu.VMEM((MOE_TM, d), BF16), pltpu.VMEM((MOE_TM, LANES), F32),
                            pltpu.SemaphoreType.DMA((3, 2))]),
        compiler_params=pltpu.CompilerParams(dimension_semantics=("arbitrary",), vmem_limit_bytes=VMEM_LIMIT),
        name="moe_sort",
    )(plan["gmap"], plan["nvalid"], plan["tail0"], plan["taillen"], hb, route, lstrict)


def _moe_expert_kernel(xt_ref, elo_ref, ehi_ref, valid_ref, x_ref, w_ref, wg0, wu0, wd0, wg1, wu1, wd1, o_ref):
    t = pl.program_id(0)

    @pl.when(valid_ref[t] > 0)
    def _():
        x = x_ref[...]
        w = w_ref[...]
        gates = [jnp.dot(x, wg[0].astype(BF16), preferred_element_type=F32) for wg in (wg0, wg1)]
        ups = [jnp.dot(x, wu[0].astype(BF16), preferred_element_type=F32) for wu in (wu0, wu1)]
        hes = [(_silu(gates[i]) * ups[i] * w[:, lane:lane + 1]).astype(BF16)
               for i, lane in enumerate((ROUTE_W_LO, ROUTE_W_HI))]
        o_ref[...] = (jnp.dot(hes[0], wd0[0].astype(BF16), preferred_element_type=F32)
                      + jnp.dot(hes[1], wd1[0].astype(BF16), preferred_element_type=F32)).astype(o_ref.dtype)

    @pl.when(valid_ref[t] == 0)
    def _():
        o_ref[...] = jnp.zeros_like(o_ref)


def _moe_experts(plan, xg, wsg, wg, wu, wd):
    rows, d = xg.shape
    tm = MOE_TM
    tok = lambda width: pl.BlockSpec((tm, width), lambda t, xt, elo, ehi, valid: (xt[t], 0))
    lo3 = lambda shape: pl.BlockSpec(shape, lambda t, xt, elo, ehi, valid: (elo[t], 0, 0))
    hi3 = lambda shape: pl.BlockSpec(shape, lambda t, xt, elo, ehi, valid: (ehi[t], 0, 0))
    return pl.pallas_call(
        _moe_expert_kernel,
        out_shape=jax.ShapeDtypeStruct((rows, d), BF16),
        grid_spec=pltpu.PrefetchScalarGridSpec(
            num_scalar_prefetch=4,
            grid=(rows // tm,),
            in_specs=[tok(d), tok(LANES),
                      lo3((1, d, D_EXPERT)), lo3((1, d, D_EXPERT)), lo3((1, D_EXPERT, d)),
                      hi3((1, d, D_EXPERT)), hi3((1, d, D_EXPERT)), hi3((1, D_EXPERT, d))],
            out_specs=pl.BlockSpec((tm, d), lambda t, *_: (t, 0))),
        compiler_params=pltpu.CompilerParams(dimension_semantics=("arbitrary",), vmem_limit_bytes=VMEM_LIMIT),
        name="moe_experts",
    )(plan["xtile"], plan["elo"], plan["ehi"], plan["valid"], xg, wsg, wg, wu, wd, wg, wu, wd)


def _moe_unsort_kernel(gmap_ref, og_ref, route_ref, h_ref, lstrict_ref, g_ref, b_ref, out_ref, ol_ref, sem):
    s = pl.program_id(0)
    nsteps = pl.num_programs(0)
    slot = s & 1
    ts = route_ref.shape[0]

    def gather(step, fn):
        sl = step & 1

        def body(g, carry):
            src = pl.ds(pl.multiple_of(gmap_ref[step * LGRAN + g] * GRAN, GRAN), GRAN)
            dst = pl.ds(pl.multiple_of(g * GRAN, GRAN), GRAN)
            fn(pltpu.make_async_copy(og_ref.at[src, :], ol_ref.at[sl, dst, :], sem.at[sl]))
            return carry
        lax.fori_loop(0, LGRAN, body, 0)

    @pl.when(s == 0)
    def _():
        gather(s, lambda cp: cp.start())

    @pl.when(s + 1 < nsteps)
    def _():
        gather(s + 1, lambda cp: cp.start())

    route = route_ref[...]
    bucket_col = route[:, ROUTE_BUCKET:ROUTE_BUCKET + 1].astype(jnp.int32)
    lane = lax.broadcasted_iota(jnp.int32, (ts, LANES), 1)
    ohf = jnp.where(lane == bucket_col, 1.0, 0.0)
    cnt = jnp.sum(ohf, axis=0, keepdims=True).astype(jnp.int32)
    pad = (((cnt + (GRAN - 1)) >> GRAN_SHIFT) << GRAN_SHIFT).astype(F32)
    r = lax.broadcasted_iota(jnp.int32, (LANES, LANES), 0)
    c = lax.broadcasted_iota(jnp.int32, (LANES, LANES), 1)
    loff = _dot(jnp.broadcast_to(pad, (8, LANES)), jnp.where(r < c, 1.0, 0.0))[0:1, :]
    rank = jnp.dot(lstrict_ref[...], ohf.astype(BF16), preferred_element_type=F32)
    dest = jnp.sum(ohf * (loff + rank), axis=1, keepdims=True).astype(jnp.int32)
    gather(s, lambda cp: cp.wait())
    tc = ts // PERM_CHUNKS
    lrow = lax.broadcasted_iota(jnp.int32, (tc, LROWS), 1)
    for c in range(PERM_CHUNKS):
        rows = slice(c * tc, (c + 1) * tc)
        perm_t = jnp.where(lrow == dest[rows, :], 1.0, 0.0).astype(BF16)
        ffn = jnp.dot(perm_t, ol_ref[slot], preferred_element_type=F32)
        out_ref[rows, :] = _layer_norm(DEEPNORM_ALPHA * h_ref[rows, :] + ffn, g_ref[...], b_ref[...])


def _moe_unsort(plan, og, route, hf, lstrict, g2, b2):
    n, d = hf.shape
    ts = MOE_TS
    row = pl.BlockSpec((1, d), lambda s, *_: (0, 0))
    return pl.pallas_call(
        _moe_unsort_kernel,
        out_shape=jax.ShapeDtypeStruct((n, d), F32),
        grid_spec=pltpu.PrefetchScalarGridSpec(
            num_scalar_prefetch=1,
            grid=(n // ts,),
            in_specs=[pl.BlockSpec(memory_space=pl.ANY), pl.BlockSpec((ts, LANES), lambda s, *_: (s, 0)),
                      pl.BlockSpec((ts, d), lambda s, *_: (s, 0)), pl.BlockSpec((ts, ts), lambda s, *_: (0, 0)),
                      row, row],
            out_specs=pl.BlockSpec((ts, d), lambda s, *_: (s, 0)),
            scratch_shapes=[pltpu.VMEM((2, LROWS, d), BF16), pltpu.SemaphoreType.DMA((2,))]),
        compiler_params=pltpu.CompilerParams(dimension_semantics=("arbitrary",), vmem_limit_bytes=VMEM_LIMIT),
        name="moe_unsort",
    )(plan["gmap_back"], og, route, hf, lstrict, g2, b2)


def _moe_plan(cnt_half, n):
    nsrc = n // MOE_TS
    i32 = jnp.int32
    cnt = cnt_half.reshape(nsrc, -1, LANES).sum(axis=1)[:, :N_BUCKETS].astype(i32)
    run_g = (cnt + GRAN - 1) // GRAN
    nvalid = run_g.sum(axis=1)
    loff_g = jnp.cumsum(run_g, axis=1) - run_g
    bucket_g = run_g.sum(axis=0)
    gpt = MOE_TM // GRAN
    btiles = (bucket_g + gpt - 1) // gpt
    tend = jnp.cumsum(btiles)
    tstart = tend - btiles
    gofs = tstart[None, :] * gpt + jnp.cumsum(run_g, axis=0) - run_g
    n_tiles = -(-(n + nsrc * N_BUCKETS * (GRAN - 1)) // MOE_TM) + N_BUCKETS + 1
    g = jnp.arange(LGRAN, dtype=i32)[None, :, None]
    in_run = (g >= loff_g[:, None, :]) & (g < (loff_g + run_g)[:, None, :])
    gmap = jnp.arange(LGRAN, dtype=i32)[None, :] + jnp.sum(jnp.where(in_run, (gofs - loff_g)[:, None, :], 0), axis=2)
    is_valid = jnp.arange(LGRAN, dtype=i32)[None, :] < nvalid[:, None]
    zero_gran = (n_tiles - 1) * gpt
    t = jnp.arange(n_tiles, dtype=i32)
    tb = jnp.minimum(jnp.sum(t[:, None] >= tend[None, :], axis=1), N_BUCKETS - 1)
    valid = (t < tend[-1]).astype(i32)
    pairs = [(a, b) for a in range(EXPERTS_PER_GROUP) for b in range(a + 1, EXPERTS_PER_GROUP)]
    pidx = tb % N_PAIRS
    pair_a = sum(jnp.where(pidx == i, a, 0) for i, (a, _) in enumerate(pairs))
    pair_b = sum(jnp.where(pidx == i, b, 0) for i, (_, b) in enumerate(pairs))
    grp = tb // N_PAIRS
    return {
        "n_tiles": n_tiles,
        "gmap": jnp.where(is_valid, gmap, 0).reshape(-1).astype(i32),
        "gmap_back": jnp.where(is_valid, gmap, zero_gran).reshape(-1).astype(i32),
        "nvalid": nvalid.astype(i32),
        "tail0": jnp.concatenate([tstart * gpt + bucket_g, tend[-1:]]).astype(i32),
        "taillen": (btiles * gpt - bucket_g).astype(i32),
        "xtile": jnp.where(valid > 0, t, 0).astype(i32),
        "elo": (grp * EXPERTS_PER_GROUP + pair_a).astype(i32),
        "ehi": (grp * EXPERTS_PER_GROUP + pair_b).astype(i32),
        "valid": valid,
    }


def _pad_lanes(a, lane0=0):
    return jnp.zeros((1, LANES), F32).at[0, lane0:lane0 + a.shape[0]].set(a.astype(F32))


def _rope_tables(seq):
    half = HEAD_DIM // 2
    inv_freq = ROPE_THETA ** (-jnp.arange(half, dtype=F32) / half)
    ang = jnp.arange(seq).astype(F32)[:, None] * inv_freq[None, :]
    cos, sin = jnp.cos(ang), jnp.sin(ang)
    return jnp.concatenate([cos, cos], axis=-1), jnp.concatenate([-sin, sin], axis=-1)


def _layer(x, w_in, conv_w, a_log, dt_bias, dn_norm_w, w_out, ln1_g, ln1_b, router_w1, router_b1,
           router_w2, router_b2, w_gate, w_up, w_down, ln2_g, ln2_b):
    bsz, seq, d = x.shape
    n = bsz * seq
    x2 = x.reshape(n, d)

    o_z, o_b, o_mb = 3 * D_DN, 4 * D_DN, 4 * D_DN + 2 * N_HEADS_DN
    w_ba = jnp.pad(w_in[:, o_b:o_mb], ((0, 0), (0, LANES - 2 * N_HEADS_DN)))
    w_all = jnp.concatenate([w_in[:, :o_z], w_in[:, o_z:o_b], w_ba, w_in[:, o_mb:]], axis=1).astype(BF16)

    tm = min(512, n)
    dn_qkv, z, ba, mb_qkv = _in_proj(x2, w_all, conv_w, tm, seq)

    y_dn = _deltanet(dn_qkv.reshape(bsz, seq, 3 * D_DN), z.reshape(bsz, seq, D_DN), ba.reshape(bsz, seq, LANES),
                     _pad_lanes(a_log, N_HEADS_DN), _pad_lanes(dt_bias, N_HEADS_DN),
                     dn_norm_w.astype(F32).reshape(1, HEAD_DIM))

    cos_t, sin_t = _rope_tables(seq)
    q_r, k_r, vt, sel = _moba_prep(mb_qkv.reshape(bsz, seq, 3 * D_MOBA), cos_t, sin_t)
    y_mb = _moba_attn(q_r, k_r, vt, sel)

    rw = jnp.concatenate([router_w1, jnp.transpose(router_w2, (1, 0, 2)).reshape(d, N_EXPERTS)], axis=1)
    rw = jnp.pad(rw, ((0, 0), (0, LANES - rw.shape[1])))
    rb = _pad_lanes(jnp.concatenate([router_b1, router_b2.reshape(-1)]))
    hf, hb, route, cnt = _mix_route(y_dn.reshape(n, D_DN), y_mb.reshape(n, D_MOBA), x2, w_out.astype(BF16),
                                    ln1_g.reshape(1, d), ln1_b.reshape(1, d), rw, rb, min(2 * MOE_TS, n))

    plan = _moe_plan(cnt, n)
    idx = jnp.arange(MOE_TS, dtype=jnp.int32)
    lstrict = (idx[None, :] < idx[:, None]).astype(BF16)
    xg, wsg = _moe_sort(plan, hb, route, lstrict)
    og = _moe_experts(plan, xg, wsg, w_gate, w_up, w_down)
    out = _moe_unsort(plan, og, route, hf, lstrict, ln2_g.reshape(1, d), ln2_b.reshape(1, d))
    return out.reshape(bsz, seq, d)


def kernel(x, w_in, conv_w, a_log, dt_bias, dn_norm_w, w_out, ln1_g, ln1_b, router_w1, router_b1, router_w2, router_b2, expert_w_gate, expert_w_up, expert_w_down, ln2_g, ln2_b):
    for l in range(DEPTH):
        x = _layer(x, w_in[l], conv_w[l], a_log[l], dt_bias[l], dn_norm_w[l], w_out[l], ln1_g[l], ln1_b[l],
                   router_w1[l], router_b1[l], router_w2[l], router_b2[l], expert_w_gate[l], expert_w_up[l],
                   expert_w_down[l], ln2_g[l], ln2_b[l])
    return x
```

```python
import functools

import jax
import jax.numpy as jnp
from jax import lax
from jax.experimental import pallas as pl
from jax.experimental.pallas import tpu as pltpu

F32 = jnp.float32
BF16 = jnp.bfloat16

HEAD_DIM = 128
N_HEADS_DN = 4
N_HEADS_MOBA = 4
D_DN = N_HEADS_DN * HEAD_DIM
D_MOBA = N_HEADS_MOBA * HEAD_DIM
CONV_K = 4
DN_CHUNK = 64
MOBA_BLOCK = 256
MOBA_TOPK = 3
MOBA_UNROLL = 4
PREP_BLOCKS = 2
ROPE_THETA = 10000.0
N_GROUPS = 4
EXPERTS_PER_GROUP = 4
N_EXPERTS = N_GROUPS * EXPERTS_PER_GROUP
D_EXPERT = 256
LN_EPS = 1e-5
RMS_EPS = 1e-6
L2_EPS = 1e-6
NEG_INF = -1e30
LOG2E = 1.4426950408889634
DEPTH = 1
DEEPNORM_ALPHA = (2 * DEPTH) ** 0.25

LANES = 128
IN_PROJ_TM = 512
IN_PROJ_GROUP = 512
DN_TILE = 256
DN_HEADS_PER_STEP = 4
DN_SEQS_PER_STEP = 2
GATE_LANE0 = N_GROUPS
N_PAIRS = EXPERTS_PER_GROUP * (EXPERTS_PER_GROUP - 1) // 2
N_BUCKETS = N_GROUPS * N_PAIRS
ROUTE_BUCKET, ROUTE_W_LO, ROUTE_W_HI = 0, 1, 2
MOE_TS = 512
MOE_TM = 512
GRAN = 16
GRAN_SHIFT = 4
LROWS = -(-(MOE_TS + N_BUCKETS * (GRAN - 1)) // LANES) * LANES
LGRAN = LROWS // GRAN
PERM_CHUNKS = 4
VMEM_LIMIT = 48 * 1024 * 1024


def _dot(a, b):
    return jnp.dot(a.astype(BF16), b.astype(BF16), preferred_element_type=F32)


def _dot_nt(a, b):
    return lax.dot_general(a.astype(BF16), b.astype(BF16), (((1,), (1,)), ((), ())),
                           preferred_element_type=F32)


def _split2(a):
    hi = a.astype(BF16)
    lo = (a - hi.astype(F32)).astype(BF16)
    return hi, lo


def _split3(a):
    hi = a.astype(BF16)
    r = a - hi.astype(F32)
    mid = r.astype(BF16)
    lo = (r - mid.astype(F32)).astype(BF16)
    return hi, mid, lo


def _dot3(a, b):
    ah, al = _split2(a)
    bh, bl = _split2(b)
    return (jnp.dot(ah, bh, preferred_element_type=F32) + jnp.dot(ah, bl, preferred_element_type=F32)
            + jnp.dot(al, bh, preferred_element_type=F32))


def _dot3_nt(a, b):
    ah, al = _split2(a)
    bh, bl = _split2(b)
    dn = (((1,), (1,)), ((), ()))
    return (lax.dot_general(ah, bh, dn, preferred_element_type=F32)
            + lax.dot_general(ah, bl, dn, preferred_element_type=F32)
            + lax.dot_general(al, bh, dn, preferred_element_type=F32))


def _dot_exact_lhs(a_bf16, b):
    bh, bm, bl = _split3(b)
    return (jnp.dot(a_bf16, bh, preferred_element_type=F32) + jnp.dot(a_bf16, bm, preferred_element_type=F32)
            + jnp.dot(a_bf16, bl, preferred_element_type=F32))


def _silu(x):
    return x * jax.nn.sigmoid(x)


def _softplus(x):
    return jnp.maximum(x, 0.0) + jnp.log1p(jnp.exp(-jnp.abs(x)))


def _layer_norm(t, g, b):
    mu = jnp.mean(t, axis=-1, keepdims=True)
    d = t - mu
    var = jnp.mean(d * d, axis=-1, keepdims=True)
    return d * lax.rsqrt(var + LN_EPS) * g + b


def _lane_pick(x, lane):
    ids = lax.broadcasted_iota(jnp.int32, x.shape, 1)
    return jnp.sum(jnp.where(ids == lane, x, 0.0), axis=1, keepdims=True)


def _in_proj_kernel(x_ref, w_ref, cw_ref, dn_ref, z_ref, ba_ref, mb_ref, cb_ref, *, tiles_per_seq):
    i = pl.program_id(0)
    tm = x_ref.shape[0]
    o0 = 3 * D_DN
    o1 = o0 + D_DN
    o2 = o1 + LANES

    @pl.when(i % tiles_per_seq == 0)
    def _():
        cb_ref[0:8, :] = jnp.zeros((8, o0), F32)

    xb = x_ref[...].astype(BF16)
    gw = IN_PROJ_GROUP
    for grp in range(3 * D_DN // gw):
        cs = slice(grp * gw, (grp + 1) * gw)
        u = jnp.dot(xb, w_ref[:, cs], preferred_element_type=F32)
        mb_ref[:, cs] = jnp.dot(xb, w_ref[:, o2 + grp * gw:o2 + (grp + 1) * gw],
                                preferred_element_type=F32).astype(mb_ref.dtype)
        cb_ref[8:8 + tm, cs] = u
        acc = cw_ref[CONV_K - 1:CONV_K, cs] * u
        for s in range(1, CONV_K):
            acc = acc + cw_ref[CONV_K - 1 - s:CONV_K - s, cs] * cb_ref[8 - s:8 - s + tm, cs]
        cb_ref[0:8, cs] = u[tm - 8:tm, :]
        act = _silu(acc)
        if grp * gw < 2 * D_DN:
            outs = []
            for h in range(gw // HEAD_DIM):
                t = act[:, h * HEAD_DIM:(h + 1) * HEAD_DIM]
                t = t * lax.rsqrt(jnp.sum(t * t, axis=-1, keepdims=True) + L2_EPS)
                outs.append(t * (HEAD_DIM ** -0.5) if grp * gw < D_DN else t)
            act = jnp.concatenate(outs, axis=1)
        dn_ref[:, cs] = act
    z_ref[...] = _silu(jnp.dot(xb, w_ref[:, o0:o1], preferred_element_type=F32))
    ba_ref[...] = jnp.dot(xb, w_ref[:, o1:o2], preferred_element_type=F32)


def _in_proj(x2, w_all, conv_w, tm, seq):
    n, d = x2.shape
    wc = w_all.shape[1]
    return pl.pallas_call(
        functools.partial(_in_proj_kernel, tiles_per_seq=seq // tm),
        out_shape=(jax.ShapeDtypeStruct((n, 3 * D_DN), F32), jax.ShapeDtypeStruct((n, D_DN), F32),
                   jax.ShapeDtypeStruct((n, LANES), F32), jax.ShapeDtypeStruct((n, 3 * D_MOBA), BF16)),
        grid=(n // tm,),
        in_specs=[pl.BlockSpec((tm, d), lambda i: (i, 0)),
                  pl.BlockSpec((d, wc), lambda i: (0, 0)),
                  pl.BlockSpec((CONV_K, 3 * D_DN), lambda i: (0, 0))],
        out_specs=(pl.BlockSpec((tm, 3 * D_DN), lambda i: (i, 0)), pl.BlockSpec((tm, D_DN), lambda i: (i, 0)),
                   pl.BlockSpec((tm, LANES), lambda i: (i, 0)), pl.BlockSpec((tm, 3 * D_MOBA), lambda i: (i, 0))),
        scratch_shapes=[pltpu.VMEM((8 + tm, 3 * D_DN), F32)],
        compiler_params=pltpu.CompilerParams(dimension_semantics=("arbitrary",), vmem_limit_bytes=VMEM_LIMIT),
        name="in_proj",
    )(x2, w_all, conv_w)


def _deltanet_kernel(q_ref, k_ref, v_ref, z_ref, ba_ref, alog_ref, dtb_ref, normw_ref, y_ref,
                     s_ref, wq_s, u_s, qk_s, kdt_s, egl_s, *, hb):
    hg = pl.program_id(1)
    t = pl.program_id(2)
    tt = DN_TILE
    nchunk = tt // DN_CHUNK
    nbat = q_ref.shape[0]
    hs = range(nbat * hb)

    @pl.when(t == 0)
    def _():
        for ref in (s_ref, wq_s, u_s, qk_s, kdt_s, egl_s):
            ref[...] = jnp.zeros_like(ref)

    rd = t & 1
    wr = 1 - rd
    state = [s_ref[h] for h in hs]
    outs = [[] for _ in hs]

    pend = {}

    def chain_a(c):
        pend["r"] = [jnp.dot(wq_s[rd, h, c], state[h].astype(BF16), preferred_element_type=F32) for h in hs]

    def chain_b(c):
        lo, hi = c * DN_CHUNK, (c + 1) * DN_CHUNK
        r = pend["r"]
        vz = []
        for h in hs:
            parts = []
            if lo > 0:
                parts.append(jnp.zeros((lo, HEAD_DIM), F32))
            parts.append(u_s[rd, h, lo:hi, :] - r[h][0:DN_CHUNK, :])
            if hi < tt:
                parts.append(jnp.zeros((tt - hi, HEAD_DIM), F32))
            vz.append(jnp.concatenate(parts, axis=0).astype(BF16))
        for h in hs:
            outs[h].append(r[h][DN_CHUNK:2 * DN_CHUNK, :]
                           + jnp.dot(qk_s[rd, h, lo:hi, :], vz[h], preferred_element_type=F32))
        for h in hs:
            state[h] = (state[h] * egl_s[rd, h, 8 * c:8 * c + 1, :]
                        + jnp.dot(kdt_s[rd, h], vz[h], preferred_element_type=F32))

    chain_a(0)

    bas = [ba_ref[bb] for bb in range(nbat)]
    beta_all = [jax.nn.sigmoid(ba) for ba in bas]
    g_all = [-jnp.exp(alog_ref[...]) * _softplus(ba + dtb_ref[...]) for ba in bas]

    row = lax.broadcasted_iota(jnp.int32, (tt, tt), 0)
    col = lax.broadcasted_iota(jnp.int32, (tt, tt), 1)
    same = (row >> 6) == (col >> 6)
    incl = same & (row >= col)
    strict = same & (row > col)

    incl_b = incl.astype(BF16)
    gc_all = [_dot_exact_lhs(incl_b, g) for g in g_all]
    gct = [jnp.transpose(g) for g in gc_all]
    sub = lax.broadcasted_iota(jnp.int32, gct[0].shape, 0)
    bat = [vh // hb for vh in hs]
    sls = [slice((vh % hb) * HEAD_DIM, (vh % hb + 1) * HEAD_DIM) for vh in hs]
    heads = [hg * hb + vh % hb for vh in hs]
    q = [q_ref[bat[vh], :, sls[vh]] for vh in hs]
    k = [k_ref[bat[vh], :, sls[vh]] for vh in hs]
    v = [v_ref[bat[vh], :, sls[vh]] for vh in hs]
    beta = [_lane_pick(beta_all[bat[vh]], heads[vh]) for vh in hs]
    gcc = [_lane_pick(gc_all[bat[vh]], heads[vh] + N_HEADS_DN) for vh in hs]
    gcr = [jnp.sum(jnp.where(sub == heads[vh] + N_HEADS_DN, gct[bat[vh]], 0.0), axis=0, keepdims=True)
           for vh in hs]
    chain_b(0)

    decay = [jnp.where(incl, jnp.exp(jnp.where(incl, gcc[h] - gcr[h], 0.0)), 0.0) for h in hs]
    kb = [k[h] * beta[h] for h in hs]
    vb = [v[h] * beta[h] for h in hs]
    a_mat = [jnp.where(strict, _dot_nt(kb[h], k[h]) * decay[h], 0.0) for h in hs]
    chain_a(1)
    qk = [_dot_nt(q[h], k[h]) * decay[h] for h in hs]
    eye = (row == col).astype(F32)
    d8 = (row >> 3) == (col >> 3)
    a8 = [jnp.where(d8, a, 0.0) for a in a_mat]
    chain_b(1)
    a8_2 = [_dot(a, a) for a in a8]
    chain_a(2)
    a8_4 = [_dot(a, a) for a in a8_2]
    chain_b(2)
    x = [_dot(eye - a, eye + a2) for a, a2 in zip(a8, a8_2)]
    chain_a(3)
    x = [_dot(xi, eye + a4) for xi, a4 in zip(x, a8_4)]
    chain_b(3)
    s = 8
    while s < DN_CHUNK:
        sh = s.bit_length() - 1
        off = ((row >> (sh + 1)) == (col >> (sh + 1))) & ((row >> sh) != (col >> sh))
        y = [_dot(jnp.where(off, a, 0.0), xi) for a, xi in zip(a_mat, x)]
        x = [xi - _dot(xi, yi) for xi, yi in zip(x, y)]
        s *= 2
    tinv = x
    eg = [jnp.exp(g) for g in gcc]
    wu = [_dot(tinv[h], jnp.concatenate([kb[h] * eg[h], vb[h]], axis=1)) for h in hs]
    qd = [q[h] * eg[h] for h in hs]
    gl_rows = [[g[(c + 1) * DN_CHUNK - 1:(c + 1) * DN_CHUNK, :] for c in range(nchunk)] for g in gcc]
    gl_col = [jnp.concatenate([jnp.broadcast_to(g, (DN_CHUNK, 1)) for g in rows], axis=0) for rows in gl_rows]
    kdt = [jnp.transpose(k[h] * jnp.exp(gl_col[h] - gcc[h])) for h in hs]

    ys = []
    for h in hs:
        o = jnp.concatenate(outs[h], axis=0)
        o = o * lax.rsqrt(jnp.mean(o * o, axis=-1, keepdims=True) + RMS_EPS) * normw_ref[...]
        ys.append(o * z_ref[bat[h], :, sls[h]])
    for bb in range(nbat):
        y_ref[bb] = jnp.concatenate(ys[bb * hb:(bb + 1) * hb], axis=1).astype(y_ref.dtype)
    s_ref[...] = jnp.stack(state, axis=0)

    for h in hs:
        for c in range(nchunk):
            lo, hi = c * DN_CHUNK, (c + 1) * DN_CHUNK
            wq_s[wr, h, c] = jnp.concatenate([wu[h][lo:hi, 0:HEAD_DIM], qd[h][lo:hi, :]], axis=0).astype(BF16)
            egl_s[wr, h, 8 * c:8 * c + 8, :] = jnp.broadcast_to(jnp.exp(gl_rows[h][c]), (8, HEAD_DIM))
        u_s[wr, h] = wu[h][:, HEAD_DIM:2 * HEAD_DIM]
        qk_s[wr, h] = qk[h].astype(BF16)
        kdt_s[wr, h] = kdt[h].astype(BF16)


def _deltanet(dn_qkv, z, ba, alog_row, dtb_row, normw_row):
    bsz, seq, _ = dn_qkv.shape
    tt = DN_TILE
    nt = seq // tt
    hb = DN_HEADS_PER_STEP
    nbat = DN_SEQS_PER_STEP if bsz % DN_SEQS_PER_STEP == 0 else 1
    nv = nbat * hb
    ng = N_HEADS_DN // hb
    w = hb * HEAD_DIM
    nchunk = tt // DN_CHUNK

    def cur_spec(off, width):
        return pl.BlockSpec((nbat, tt, width), lambda b, g, t: (b, jnp.minimum(t, nt - 1), g + off))

    prev_spec = pl.BlockSpec((nbat, tt, w), lambda b, g, t: (b, jnp.maximum(t - 1, 0), g))
    row_spec = pl.BlockSpec((1, LANES), lambda b, g, t: (0, 0))
    return pl.pallas_call(
        functools.partial(_deltanet_kernel, hb=hb),
        out_shape=jax.ShapeDtypeStruct((bsz, seq, D_DN), BF16),
        grid=(bsz // nbat, ng, nt + 1),
        in_specs=[cur_spec(0, w), cur_spec(ng, w), cur_spec(2 * ng, w), prev_spec,
                  pl.BlockSpec((nbat, tt, LANES), lambda b, g, t: (b, jnp.minimum(t, nt - 1), 0)),
                  row_spec, row_spec, row_spec],
        out_specs=prev_spec,
        scratch_shapes=[pltpu.VMEM((nv, HEAD_DIM, HEAD_DIM), F32),
                        pltpu.VMEM((2, nv, nchunk, 2 * DN_CHUNK, HEAD_DIM), BF16),
                        pltpu.VMEM((2, nv, tt, HEAD_DIM), F32),
                        pltpu.VMEM((2, nv, tt, tt), BF16),
                        pltpu.VMEM((2, nv, HEAD_DIM, tt), BF16),
                        pltpu.VMEM((2, nv, 8 * nchunk, HEAD_DIM), F32)],
        compiler_params=pltpu.CompilerParams(dimension_semantics=("parallel", "parallel", "arbitrary"),
                                             vmem_limit_bytes=VMEM_LIMIT),
        name="deltanet",
    )(dn_qkv, dn_qkv, dn_qkv, z, ba, alog_row, dtb_row, normw_row)


def _moba_prep_kernel(x_ref, cos_ref, sin_ref, q_ref, k_ref, vt_ref, sel_ref, km_ref, *, nb, topk, spb):
    step = pl.program_id(1)

    @pl.when(step == 0)
    def _():
        km_ref[...] = jnp.zeros_like(km_ref)

    half = HEAD_DIM // 2
    blk = lax.broadcasted_iota(jnp.int32, (nb, MOBA_BLOCK), 0)
    kmrow = lax.broadcasted_iota(jnp.int32, (nb, D_MOBA), 0)
    km = km_ref[...]
    for sb in range(spb):
        j = step * spb + sb
        rs = slice(sb * MOBA_BLOCK, (sb + 1) * MOBA_BLOCK)
        cos = cos_ref[rs, :]
        sin = sin_ref[rs, :]
        kmeans = []
        for h in range(N_HEADS_MOBA):
            hs = slice(h * HEAD_DIM, (h + 1) * HEAD_DIM)
            qh = x_ref[0, rs, hs].astype(F32)
            kh = x_ref[0, rs, D_MOBA + h * HEAD_DIM:D_MOBA + (h + 1) * HEAD_DIM].astype(F32)
            qr = (qh * cos + pltpu.roll(qh, half, 1) * sin) * (HEAD_DIM ** -0.5)
            kr = kh * cos + pltpu.roll(kh, half, 1) * sin
            q_ref[0, rs, hs] = (qr * LOG2E).astype(q_ref.dtype)
            k_ref[0, rs, hs] = kr.astype(k_ref.dtype)
            kmeans.append(jnp.mean(kr, axis=0, keepdims=True))

            gate = _dot3_nt(km[:, hs], qr)
            gate = jnp.where(blk < j, gate, NEG_INF)
            rank = jnp.zeros(gate.shape, F32)
            for m in range(nb):
                gm = gate[m:m + 1, :]
                ahead = (gm > gate) | ((gm == gate) & (blk > m))
                rank = rank + jnp.where(ahead, 1.0, 0.0)
            sel = (blk < j) & (rank < topk)
            sel_ref[0, sb, h * nb:(h + 1) * nb, :] = jnp.where(sel, 1.0, 0.0)

        km = jnp.where(kmrow == j, jnp.concatenate(kmeans, axis=1), km)
        vt_ref[0, sb] = jnp.transpose(x_ref[0, rs, 2 * D_MOBA:3 * D_MOBA].astype(F32)).astype(vt_ref.dtype)
    km_ref[...] = km


def _moba_prep(mb_qkv, cos_t, sin_t):
    bsz, seq, _ = mb_qkv.shape
    nb = seq // MOBA_BLOCK
    topk = min(MOBA_TOPK, nb)
    spb = PREP_BLOCKS if nb % PREP_BLOCKS == 0 else 1
    rows = spb * MOBA_BLOCK
    kern = functools.partial(_moba_prep_kernel, nb=nb, topk=topk, spb=spb)
    tok_spec = pl.BlockSpec((1, rows, D_MOBA), lambda b, j: (b, j, 0))
    tab_spec = pl.BlockSpec((rows, HEAD_DIM), lambda b, j: (j, 0))
    return pl.pallas_call(
        kern,
        out_shape=(jax.ShapeDtypeStruct((bsz, seq, D_MOBA), BF16), jax.ShapeDtypeStruct((bsz, seq, D_MOBA), BF16),
                   jax.ShapeDtypeStruct((bsz, nb, D_MOBA, MOBA_BLOCK), BF16),
                   jax.ShapeDtypeStruct((bsz, nb, N_HEADS_MOBA * nb, MOBA_BLOCK), F32)),
        grid=(bsz, nb // spb),
        in_specs=[pl.BlockSpec((1, rows, 3 * D_MOBA), lambda b, j: (b, j, 0)), tab_spec, tab_spec],
        out_specs=(tok_spec, tok_spec,
                   pl.BlockSpec((1, spb, D_MOBA, MOBA_BLOCK), lambda b, j: (b, j, 0, 0)),
                   pl.BlockSpec((1, spb, N_HEADS_MOBA * nb, MOBA_BLOCK), lambda b, j: (b, j, 0, 0))),
        scratch_shapes=[pltpu.VMEM((nb, D_MOBA), F32)],
        compiler_params=pltpu.CompilerParams(dimension_semantics=("parallel", "arbitrary"),
                                             vmem_limit_bytes=VMEM_LIMIT),
        name="moba_prep",
    )(mb_qkv, cos_t, sin_t)


def _moba_attn_kernel(q_ref, k_ref, vt_ref, sel_ref, o_ref, acc_ref, *, nb):
    j = pl.program_id(1)
    blk = MOBA_BLOCK
    nh = N_HEADS_MOBA
    dn = (((1,), (1,)), ((), ()))
    hsl = [slice(h * HEAD_DIM, (h + 1) * HEAD_DIM) for h in range(nh)]
    qs = [q_ref[0, :, hsl[h]] for h in range(nh)]

    ki = lax.broadcasted_iota(jnp.int32, (blk, blk), 0)
    qi = lax.broadcasted_iota(jnp.int32, (blk, blk), 1)
    own = pl.ds(pl.multiple_of(j * blk, blk), blk)
    s_own = [jnp.where(ki <= qi, lax.dot_general(k_ref[0, own, hsl[h]], qs[h], dn, preferred_element_type=F32),
                       NEG_INF) for h in range(nh)]

    def scores(n, h):
        kn = k_ref[0, pl.ds(pl.multiple_of(n * blk, blk), blk), hsl[h]]
        s = lax.dot_general(kn, qs[h], dn, preferred_element_type=F32)
        return jnp.where(sel_ref[0, 0, pl.ds(h * nb + n, 1), :] > 0.5, s, NEG_INF)

    ones8 = jnp.ones((8, blk), BF16)

    def softmax_pv(n, h, s, m, l):
        m_new = jnp.maximum(m, jnp.max(s, axis=0, keepdims=True))
        pb = jnp.exp2(s - m_new).astype(BF16)
        psum = jnp.dot(ones8, pb, preferred_element_type=F32)[0:1, :]
        pv = jnp.dot(vt_ref[0, n, hsl[h], :], pb, preferred_element_type=F32)
        return m_new, psum, pv

    ms, ls = [], []
    for h in range(nh):
        m, psum, pv = softmax_pv(j, h, s_own[h], jnp.full((1, blk), NEG_INF, F32), None)
        ms.append(m)
        ls.append(psum)
        acc_ref[h] = pv

    def body(step, carry):
        ms, ls = (list(t) for t in carry)
        n0 = MOBA_UNROLL * step
        cur = [scores(n0, h) for h in range(nh)]
        for i in range(MOBA_UNROLL):
            nxt = []
            for h in range(nh):
                if i + 1 < MOBA_UNROLL:
                    nxt.append(scores(n0 + i + 1, h))
                m_new, psum, pv = softmax_pv(n0 + i, h, cur[h], ms[h], ls[h])
                alpha = jnp.exp2(ms[h] - m_new)
                ls[h] = alpha * ls[h] + psum
                ms[h] = m_new
                acc_ref[h] = acc_ref[h] * alpha + pv
            cur = nxt
        return tuple(ms), tuple(ls)

    ms, ls = lax.fori_loop(0, (j + MOBA_UNROLL - 1) // MOBA_UNROLL, body, (tuple(ms), tuple(ls)))
    o_ref[0] = jnp.concatenate([jnp.transpose(acc_ref[h] / ls[h]) for h in range(nh)],
                               axis=1).astype(o_ref.dtype)


def _moba_attn(q_r, k_r, vt, sel):
    bsz, seq, _ = q_r.shape
    nb = seq // MOBA_BLOCK
    assert nb % MOBA_UNROLL == 0, (nb, MOBA_UNROLL)
    tok_spec = pl.BlockSpec((1, MOBA_BLOCK, D_MOBA), lambda b, j: (b, j, 0))
    return pl.pallas_call(
        functools.partial(_moba_attn_kernel, nb=nb),
        out_shape=jax.ShapeDtypeStruct((bsz, seq, D_MOBA), BF16),
        grid=(bsz, nb),
        in_specs=[tok_spec,
                  pl.BlockSpec((1, seq, D_MOBA), lambda b, j: (b, 0, 0)),
                  pl.BlockSpec((1, nb, D_MOBA, MOBA_BLOCK), lambda b, j: (b, 0, 0, 0)),
                  pl.BlockSpec((1, 1, N_HEADS_MOBA * nb, MOBA_BLOCK), lambda b, j: (b, j, 0, 0))],
        out_specs=tok_spec,
        scratch_shapes=[pltpu.VMEM((N_HEADS_MOBA, HEAD_DIM, MOBA_BLOCK), F32)],
        compiler_params=pltpu.CompilerParams(dimension_semantics=("parallel", "arbitrary"),
                                             vmem_limit_bytes=VMEM_LIMIT),
        name="moba_attn",
    )(q_r, k_r, vt, sel)


def _route_record(logits):
    lane = lax.broadcasted_iota(jnp.int32, logits.shape, 1)
    big = jnp.int32(LANES)

    def first_lane(mask):
        return jnp.min(jnp.where(mask, lane, big), axis=1, keepdims=True)

    is_g = lane < N_GROUPS
    m1 = jnp.max(jnp.where(is_g, logits, NEG_INF), axis=1, keepdims=True)
    s1 = jnp.sum(jnp.where(is_g, jnp.exp(logits - m1), 0.0), axis=1, keepdims=True)
    pg = 1.0 / s1
    gsel = first_lane(is_g & (logits == m1))

    in_grp = (lane >= GATE_LANE0) & (((lane - GATE_LANE0) >> 2) == gsel) & (lane < GATE_LANE0 + N_EXPERTS)
    m2 = jnp.max(jnp.where(in_grp, logits, NEG_INF), axis=1, keepdims=True)
    s2 = jnp.sum(jnp.where(in_grp, jnp.exp(logits - m2), 0.0), axis=1, keepdims=True)
    e1 = first_lane(in_grp & (logits == m2))
    rest = in_grp & (lane != e1)
    m2b = jnp.max(jnp.where(rest, logits, NEG_INF), axis=1, keepdims=True)
    e2 = first_lane(rest & (logits == m2b))
    pe1 = 1.0 / s2
    pe2 = jnp.exp(m2b - m2) / s2
    tot = pe1 + pe2
    w1 = pg * (pe1 / tot)
    w2 = pg * (pe2 / tot)
    first_lo = e1 < e2
    lo = jnp.minimum(e1, e2)
    hi = jnp.maximum(e1, e2)
    a = (lo - GATE_LANE0) & (EXPERTS_PER_GROUP - 1)
    b = (hi - GATE_LANE0) & (EXPERTS_PER_GROUP - 1)
    bucket = gsel * N_PAIRS + ((a * (2 * EXPERTS_PER_GROUP - 1 - a)) >> 1) + (b - a - 1)
    record = jnp.where(lane == ROUTE_BUCKET, bucket.astype(F32),
                       jnp.where(lane == ROUTE_W_LO, jnp.where(first_lo, w1, w2),
                                 jnp.where(lane == ROUTE_W_HI, jnp.where(first_lo, w2, w1), 0.0)))
    return record, jnp.sum(jnp.where(lane == bucket, 1.0, 0.0), axis=0, keepdims=True)


def _mix_route_kernel(ydn_ref, ymb_ref, x_ref, wo_ref, g_ref, b_ref, rw_ref, rb_ref, h_ref, hb_ref, route_ref,
                      cnt_ref):
    nsub = cnt_ref.shape[0]
    rows = [slice(i * MOE_TS, (i + 1) * MOE_TS) for i in range(nsub)]
    wh, wl = _split2(rw_ref[...])
    wcat = jnp.concatenate([wh, wl], axis=1)

    mixes = [jnp.dot(jnp.concatenate([ydn_ref[r, :], ymb_ref[r, :]], axis=1), wo_ref[...],
                     preferred_element_type=F32) for r in rows]
    logits = []
    for r, mix in zip(rows, mixes):
        hval = _layer_norm(DEEPNORM_ALPHA * x_ref[r, :] + mix, g_ref[...], b_ref[...])
        h_ref[r, :] = hval
        hb_ref[r, :] = hval.astype(BF16)
        hh, hl = _split2(hval)
        both = jnp.dot(hh, wcat, preferred_element_type=F32)
        logits.append(both[:, 0:LANES] + both[:, LANES:2 * LANES]
                      + jnp.dot(hl, wh, preferred_element_type=F32) + rb_ref[...])
    for i, r in enumerate(rows):
        record, counts = _route_record(logits[i])
        route_ref[r, :] = record
        cnt_ref[i] = counts


def _mix_route(y_dn, y_mb, x2, wo, g1, b1, rw, rb, tm):
    n, d = x2.shape
    nsub = tm // MOE_TS
    row = lambda w: pl.BlockSpec((1, w), lambda i: (0, 0))
    return pl.pallas_call(
        _mix_route_kernel,
        out_shape=(jax.ShapeDtypeStruct((n, d), F32), jax.ShapeDtypeStruct((n, d), BF16),
                   jax.ShapeDtypeStruct((n, LANES), F32), jax.ShapeDtypeStruct((n // MOE_TS, 1, LANES), F32)),
        grid=(n // tm,),
        in_specs=[pl.BlockSpec((tm, D_DN), lambda i: (i, 0)), pl.BlockSpec((tm, D_MOBA), lambda i: (i, 0)),
                  pl.BlockSpec((tm, d), lambda i: (i, 0)), pl.BlockSpec((D_DN + D_MOBA, d), lambda i: (0, 0)),
                  row(d), row(d), pl.BlockSpec((d, LANES), lambda i: (0, 0)), row(LANES)],
        out_specs=(pl.BlockSpec((tm, d), lambda i: (i, 0)), pl.BlockSpec((tm, d), lambda i: (i, 0)),
                   pl.BlockSpec((tm, LANES), lambda i: (i, 0)), pl.BlockSpec((nsub, 1, LANES), lambda i: (i, 0, 0))),
        compiler_params=pltpu.CompilerParams(dimension_semantics=("parallel",), vmem_limit_bytes=VMEM_LIMIT),
        name="mix_route",
    )(y_dn, y_mb, x2, wo, g1, b1, rw, rb)


def _bucket_offsets_col(ohf):
    cnt = jnp.sum(ohf, axis=1, keepdims=True).astype(jnp.int32)
    pad = (((cnt + (GRAN - 1)) >> GRAN_SHIFT) << GRAN_SHIFT).astype(F32)
    r = lax.broadcasted_iota(jnp.int32, (LANES, LANES), 0)
    c = lax.broadcasted_iota(jnp.int32, (LANES, LANES), 1)
    before = jnp.where(c < r, 1.0, 0.0)
    return _dot(before, jnp.broadcast_to(pad, (LANES, LANES)))[:, 0:1]


def _moe_sort_kernel(gmap_ref, nvalid_ref, tail0_ref, taillen_ref, hb_ref, route_ref, lstrict_ref,
                     xg_ref, wsg_ref, xs_ref, ws_ref, zx_ref, zw_ref, sem):
    s = pl.program_id(0)
    nsteps = pl.num_programs(0)
    slot = s & 1
    ts = route_ref.shape[0]
    route = route_ref[...]
    rt = jnp.transpose(route)
    bucket_row = rt[ROUTE_BUCKET:ROUTE_BUCKET + 1, :].astype(jnp.int32)
    sub = lax.broadcasted_iota(jnp.int32, (LANES, ts), 0)
    ohf = jnp.where(sub == bucket_row, 1.0, 0.0)
    loff = _bucket_offsets_col(ohf)
    rank = lax.dot_general(ohf.astype(BF16), lstrict_ref[...], (((1,), (1,)), ((), ())),
                           preferred_element_type=F32)
    dest = jnp.sum(ohf * (loff + rank), axis=0, keepdims=True).astype(jnp.int32)
    rh, rl = _split2(route)
    wcat = jnp.concatenate([rh, rl], axis=1)
    rc = LROWS // PERM_CHUNKS
    for c in range(PERM_CHUNKS):
        rowi = lax.broadcasted_iota(jnp.int32, (rc, ts), 0) + c * rc
        perm = jnp.where(rowi == dest, 1.0, 0.0).astype(BF16)
        xs_ref[slot, c * rc:(c + 1) * rc, :] = jnp.dot(perm, hb_ref[...], preferred_element_type=F32).astype(BF16)
        wparts = jnp.dot(perm, wcat, preferred_element_type=F32)
        ws_ref[slot, c * rc:(c + 1) * rc, :] = wparts[:, 0:LANES] + wparts[:, LANES:2 * LANES]

    def copies(step, g):
        sl = step & 1
        src = pl.ds(pl.multiple_of(g * GRAN, GRAN), GRAN)
        dst = pl.ds(pl.multiple_of(gmap_ref[step * LGRAN + g] * GRAN, GRAN), GRAN)
        return (pltpu.make_async_copy(xs_ref.at[sl, src, :], xg_ref.at[dst, :], sem.at[0, sl]),
                pltpu.make_async_copy(ws_ref.at[sl, src, :], wsg_ref.at[dst, :], sem.at[1, sl]))

    def fill_copies(b, i):
        dst = pl.ds(pl.multiple_of((tail0_ref[b] + i) * GRAN, GRAN), GRAN)
        return (pltpu.make_async_copy(zx_ref.at[0:GRAN, :], xg_ref.at[dst, :], sem.at[2, 0]),
                pltpu.make_async_copy(zw_ref.at[0:GRAN, :], wsg_ref.at[dst, :], sem.at[2, 1]))

    def unused_tile_copies(t):
        dst = pl.ds(pl.multiple_of(t * MOE_TM, MOE_TM), MOE_TM)
        return (pltpu.make_async_copy(zx_ref, xg_ref.at[dst, :], sem.at[2, 0]),
                pltpu.make_async_copy(zw_ref, wsg_ref.at[dst, :], sem.at[2, 1]))

    def run(step, fn):
        def body(g, carry):
            for cp in copies(step, g):
                fn(cp)
            return carry
        lax.fori_loop(0, nvalid_ref[step], body, 0)

    def run_fill(fn):
        for b in range(N_BUCKETS):
            def body(i, carry, b=b):
                for cp in fill_copies(b, i):
                    fn(cp)
                return carry
            lax.fori_loop(0, taillen_ref[b], body, 0)

        def tile_body(t, carry):
            for cp in unused_tile_copies(t):
                fn(cp)
            return carry
        lax.fori_loop(tail0_ref[N_BUCKETS], xg_ref.shape[0] // MOE_TM, tile_body, 0)

    @pl.when(s == 0)
    def _():
        zx_ref[...] = jnp.zeros_like(zx_ref)
        zw_ref[...] = jnp.zeros_like(zw_ref)
        run_fill(lambda cp: cp.start())

    run(s, lambda cp: cp.start())

    @pl.when(s > 0)
    def _():
        run(s - 1, lambda cp: cp.wait())

    @pl.when(s == nsteps - 1)
    def _():
        run(s, lambda cp: cp.wait())
        run_fill(lambda cp: cp.wait())


def _moe_sort(plan, hb, route, lstrict):
    n, d = hb.shape
    ts = MOE_TS
    rows = plan["n_tiles"] * MOE_TM
    return pl.pallas_call(
        _moe_sort_kernel,
        out_shape=(jax.ShapeDtypeStruct((rows, d), BF16), jax.ShapeDtypeStruct((rows, LANES), F32)),
        grid_spec=pltpu.PrefetchScalarGridSpec(
            num_scalar_prefetch=4,
            grid=(n // ts,),
            in_specs=[pl.BlockSpec((ts, d), lambda s, *_: (s, 0)), pl.BlockSpec((ts, LANES), lambda s, *_: (s, 0)),
                      pl.BlockSpec((ts, ts), lambda s, *_: (0, 0))],
            out_specs=(pl.BlockSpec(memory_space=pl.ANY), pl.BlockSpec(memory_space=pl.ANY)),
            scratch_shapes=[pltpu.VMEM((2, LROWS, d), BF16), pltpu.VMEM((2, LROWS, LANES), F32),
                            pltpu.VMEM((MOE_TM, d), BF16), pltpu.VMEM((MOE_TM, LANES), F32),
                            pltpu.SemaphoreType.DMA((3, 2))]),
        compiler_params=pltpu.CompilerParams(dimension_semantics=("arbitrary",), vmem_limit_bytes=VMEM_LIMIT),
        name="moe_sort",
    )(plan["gmap"], plan["nvalid"], plan["tail0"], plan["taillen"], hb, route, lstrict)


def _moe_expert_kernel(xt_ref, elo_ref, ehi_ref, valid_ref, x_ref, w_ref, wg0, wu0, wd0, wg1, wu1, wd1, o_ref):
    t = pl.program_id(0)

    @pl.when(valid_ref[t] > 0)
    def _():
        x = x_ref[...]
        w = w_ref[...]
        gates = [jnp.dot(x, wg[0].astype(BF16), preferred_element_type=F32) for wg in (wg0, wg1)]
        ups = [jnp.dot(x, wu[0].astype(BF16), preferred_element_type=F32) for wu in (wu0, wu1)]
        hes = [(_silu(gates[i]) * ups[i] * w[:, lane:lane + 1]).astype(BF16)
               for i, lane in enumerate((ROUTE_W_LO, ROUTE_W_HI))]
        o_ref[...] = (jnp.dot(hes[0], wd0[0].astype(BF16), preferred_element_type=F32)
                      + jnp.dot(hes[1], wd1[0].astype(BF16), preferred_element_type=F32)).astype(o_ref.dtype)

    @pl.when(valid_ref[t] == 0)
    def _():
        o_ref[...] = jnp.zeros_like(o_ref)


def _moe_experts(plan, xg, wsg, wg, wu, wd):
    rows, d = xg.shape
    tm = MOE_TM
    tok = lambda width: pl.BlockSpec((tm, width), lambda t, xt, elo, ehi, valid: (xt[t], 0))
    lo3 = lambda shape: pl.BlockSpec(shape, lambda t, xt, elo, ehi, valid: (elo[t], 0, 0))
    hi3 = lambda shape: pl.BlockSpec(shape, lambda t, xt, elo, ehi, valid: (ehi[t], 0, 0))
    return pl.pallas_call(
        _moe_expert_kernel,
        out_shape=jax.ShapeDtypeStruct((rows, d), BF16),
        grid_spec=pltpu.PrefetchScalarGridSpec(
            num_scalar_prefetch=4,
            grid=(rows // tm,),
            in_specs=[tok(d), tok(LANES),
                      lo3((1, d, D_EXPERT)), lo3((1, d, D_EXPERT)), lo3((1, D_EXPERT, d)),
                      hi3((1, d, D_EXPERT)), hi3((1, d, D_EXPERT)), hi3((1, D_EXPERT, d))],
            out_specs=pl.BlockSpec((tm, d), lambda t, *_: (t, 0))),
        compiler_params=pltpu.CompilerParams(dimension_semantics=("arbitrary",), vmem_limit_bytes=VMEM_LIMIT),
        name="moe_experts",
    )(plan["xtile"], plan["elo"], plan["ehi"], plan["valid"], xg, wsg, wg, wu, wd, wg, wu, wd)


def _moe_unsort_kernel(gmap_ref, og_ref, route_ref, h_ref, lstrict_ref, g_ref, b_ref, out_ref, ol_ref, sem):
    s = pl.program_id(0)
    nsteps = pl.num_programs(0)
    slot = s & 1
    ts = route_ref.shape[0]

    def gather(step, fn):
        sl = step & 1

        def body(g, carry):
            src = pl.ds(pl.multiple_of(gmap_ref[step * LGRAN + g] * GRAN, GRAN), GRAN)
            dst = pl.ds(pl.multiple_of(g * GRAN, GRAN), GRAN)
            fn(pltpu.make_async_copy(og_ref.at[src, :], ol_ref.at[sl, dst, :], sem.at[sl]))
            return carry
        lax.fori_loop(0, LGRAN, body, 0)

    @pl.when(s == 0)
    def _():
        gather(s, lambda cp: cp.start())

    @pl.when(s + 1 < nsteps)
    def _():
        gather(s + 1, lambda cp: cp.start())

    route = route_ref[...]
    bucket_col = route[:, ROUTE_BUCKET:ROUTE_BUCKET + 1].astype(jnp.int32)
    lane = lax.broadcasted_iota(jnp.int32, (ts, LANES), 1)
    ohf = jnp.where(lane == bucket_col, 1.0, 0.0)
    cnt = jnp.sum(ohf, axis=0, keepdims=True).astype(jnp.int32)
    pad = (((cnt + (GRAN - 1)) >> GRAN_SHIFT) << GRAN_SHIFT).astype(F32)
    r = lax.broadcasted_iota(jnp.int32, (LANES, LANES), 0)
    c = lax.broadcasted_iota(jnp.int32, (LANES, LANES), 1)
    loff = _dot(jnp.broadcast_to(pad, (8, LANES)), jnp.where(r < c, 1.0, 0.0))[0:1, :]
    rank = jnp.dot(lstrict_ref[...], ohf.astype(BF16), preferred_element_type=F32)
    dest = jnp.sum(ohf * (loff + rank), axis=1, keepdims=True).astype(jnp.int32)
    gather(s, lambda cp: cp.wait())
    tc = ts // PERM_CHUNKS
    lrow = lax.broadcasted_iota(jnp.int32, (tc, LROWS), 1)
    for c in range(PERM_CHUNKS):
        rows = slice(c * tc, (c + 1) * tc)
        perm_t = jnp.where(lrow == dest[rows, :], 1.0, 0.0).astype(BF16)
        ffn = jnp.dot(perm_t, ol_ref[slot], preferred_element_type=F32)
        out_ref[rows, :] = _layer_norm(DEEPNORM_ALPHA * h_ref[rows, :] + ffn, g_ref[...], b_ref[...])


def _moe_unsort(plan, og, route, hf, lstrict, g2, b2):
    n, d = hf.shape
    ts = MOE_TS
    row = pl.BlockSpec((1, d), lambda s, *_: (0, 0))
    return pl.pallas_call(
        _moe_unsort_kernel,
        out_shape=jax.ShapeDtypeStruct((n, d), F32),
        grid_spec=pltpu.PrefetchScalarGridSpec(
            num_scalar_prefetch=1,
            grid=(n // ts,),
            in_specs=[pl.BlockSpec(memory_space=pl.ANY), pl.BlockSpec((ts, LANES), lambda s, *_: (s, 0)),
                      pl.BlockSpec((ts, d), lambda s, *_: (s, 0)), pl.BlockSpec((ts, ts), lambda s, *_: (0, 0)),
                      row, row],
            out_specs=pl.BlockSpec((ts, d), lambda s, *_: (s, 0)),
            scratch_shapes=[pltpu.VMEM((2, LROWS, d), BF16), pltpu.SemaphoreType.DMA((2,))]),
        compiler_params=pltpu.CompilerParams(dimension_semantics=("arbitrary",), vmem_limit_bytes=VMEM_LIMIT),
        name="moe_unsort",
    )(plan["gmap_back"], og, route, hf, lstrict, g2, b2)


def _moe_plan(cnt_half, n):
    nsrc = n // MOE_TS
    i32 = jnp.int32
    cnt = cnt_half.reshape(nsrc, -1, LANES).sum(axis=1)[:, :N_BUCKETS].astype(i32)
    run_g = (cnt + GRAN - 1) // GRAN
    nvalid = run_g.sum(axis=1)
    loff_g = jnp.cumsum(run_g, axis=1) - run_g
    bucket_g = run_g.sum(axis=0)
    gpt = MOE_TM // GRAN
    btiles = (bucket_g + gpt - 1) // gpt
    tend = jnp.cumsum(btiles)
    tstart = tend - btiles
    gofs = tstart[None, :] * gpt + jnp.cumsum(run_g, axis=0) - run_g
    n_tiles = -(-(n + nsrc * N_BUCKETS * (GRAN - 1)) // MOE_TM) + N_BUCKETS + 1
    g = jnp.arange(LGRAN, dtype=i32)[None, :, None]
    in_run = (g >= loff_g[:, None, :]) & (g < (loff_g + run_g)[:, None, :])
    gmap = jnp.arange(LGRAN, dtype=i32)[None, :] + jnp.sum(jnp.where(in_run, (gofs - loff_g)[:, None, :], 0), axis=2)
    is_valid = jnp.arange(LGRAN, dtype=i32)[None, :] < nvalid[:, None]
    zero_gran = (n_tiles - 1) * gpt
    t = jnp.arange(n_tiles, dtype=i32)
    tb = jnp.minimum(jnp.sum(t[:, None] >= tend[None, :], axis=1), N_BUCKETS - 1)
    valid = (t < tend[-1]).astype(i32)
    pairs = [(a, b) for a in range(EXPERTS_PER_GROUP) for b in range(a + 1, EXPERTS_PER_GROUP)]
    pidx = tb % N_PAIRS
    pair_a = sum(jnp.where(pidx == i, a, 0) for i, (a, _) in enumerate(pairs))
    pair_b = sum(jnp.where(pidx == i, b, 0) for i, (_, b) in enumerate(pairs))
    grp = tb // N_PAIRS
    return {
        "n_tiles": n_tiles,
        "gmap": jnp.where(is_valid, gmap, 0).reshape(-1).astype(i32),
        "gmap_back": jnp.where(is_valid, gmap, zero_gran).reshape(-1).astype(i32),
        "nvalid": nvalid.astype(i32),
        "tail0": jnp.concatenate([tstart * gpt + bucket_g, tend[-1:]]).astype(i32),
        "taillen": (btiles * gpt - bucket_g).astype(i32),
        "xtile": jnp.where(valid > 0, t, 0).astype(i32),
        "elo": (grp * EXPERTS_PER_GROUP + pair_a).astype(i32),
        "ehi": (grp * EXPERTS_PER_GROUP + pair_b).astype(i32),
        "valid": valid,
    }


def _pad_lanes(a, lane0=0):
    return jnp.zeros((1, LANES), F32).at[0, lane0:lane0 + a.shape[0]].set(a.astype(F32))


def _rope_tables(seq):
    half = HEAD_DIM // 2
    inv_freq = ROPE_THETA ** (-jnp.arange(half, dtype=F32) / half)
    ang = jnp.arange(seq).astype(F32)[:, None] * inv_freq[None, :]
    cos, sin = jnp.cos(ang), jnp.sin(ang)
    return jnp.concatenate([cos, cos], axis=-1), jnp.concatenate([-sin, sin], axis=-1)


def _layer(x, w_in, conv_w, a_log, dt_bias, dn_norm_w, w_out, ln1_g, ln1_b, router_w1, router_b1,
           router_w2, router_b2, w_gate, w_up, w_down, ln2_g, ln2_b):
    bsz, seq, d = x.shape
    n = bsz * seq
    x2 = x.reshape(n, d)

    o_z, o_b, o_mb = 3 * D_DN, 4 * D_DN, 4 * D_DN + 2 * N_HEADS_DN
    w_ba = jnp.pad(w_in[:, o_b:o_mb], ((0, 0), (0, LANES - 2 * N_HEADS_DN)))
    w_all = jnp.concatenate([w_in[:, :o_z], w_in[:, o_z:o_b], w_ba, w_in[:, o_mb:]], axis=1).astype(BF16)

    dn_qkv, z, ba, mb_qkv = _in_proj(x2, w_all, conv_w, min(IN_PROJ_TM, seq), seq)

    y_dn = _deltanet(dn_qkv.reshape(bsz, seq, 3 * D_DN), z.reshape(bsz, seq, D_DN), ba.reshape(bsz, seq, LANES),
                     _pad_lanes(a_log, N_HEADS_DN), _pad_lanes(dt_bias, N_HEADS_DN),
                     dn_norm_w.astype(F32).reshape(1, HEAD_DIM))

    cos_t, sin_t = _rope_tables(seq)
    q_r, k_r, vt, sel = _moba_prep(mb_qkv.reshape(bsz, seq, 3 * D_MOBA), cos_t, sin_t)
    y_mb = _moba_attn(q_r, k_r, vt, sel)

    rw = jnp.concatenate([router_w1, jnp.transpose(router_w2, (1, 0, 2)).reshape(d, N_EXPERTS)], axis=1)
    rw = jnp.pad(rw, ((0, 0), (0, LANES - rw.shape[1])))
    rb = _pad_lanes(jnp.concatenate([router_b1, router_b2.reshape(-1)]))
    hf, hb, route, cnt = _mix_route(y_dn.reshape(n, D_DN), y_mb.reshape(n, D_MOBA), x2, w_out.astype(BF16),
                                    ln1_g.reshape(1, d), ln1_b.reshape(1, d), rw, rb, min(2 * MOE_TS, n))

    plan = _moe_plan(cnt, n)
    idx = jnp.arange(MOE_TS, dtype=jnp.int32)
    lstrict = (idx[None, :] < idx[:, None]).astype(BF16)
    xg, wsg = _moe_sort(plan, hb, route, lstrict)
    og = _moe_experts(plan, xg, wsg, w_gate, w_up, w_down)
    out = _moe_unsort(plan, og, route, hf, lstrict, ln2_g.reshape(1, d), ln2_b.reshape(1, d))
    return out.reshape(bsz, seq, d)


def kernel(x, w_in, conv_w, a_log, dt_bias, dn_norm_w, w_out, ln1_g, ln1_b, router_w1, router_b1, router_w2, router_b2, expert_w_gate, expert_w_up, expert_w_down, ln2_g, ln2_b):
    for l in range(DEPTH):
        x = _layer(x, w_in[l], conv_w[l], a_log[l], dt_bias[l], dn_norm_w[l], w_out[l], ln1_g[l], ln1_b[l],
                   router_w1[l], router_b1[l], router_w2[l], router_b2[l], expert_w_gate[l], expert_w_up[l],
                   expert_w_down[l], ln2_g[l], ln2_b[l])
    return x
```

```python
import functools

import jax
import jax.numpy as jnp
from jax import lax
from jax.experimental import pallas as pl
from jax.experimental.pallas import tpu as pltpu

F32 = jnp.float32
BF16 = jnp.bfloat16

HEAD_DIM = 128
N_HEADS_DN = 4
N_HEADS_MOBA = 4
D_DN = N_HEADS_DN * HEAD_DIM
D_MOBA = N_HEADS_MOBA * HEAD_DIM
CONV_K = 4
DN_CHUNK = 64
MOBA_BLOCK = 256
MOBA_TOPK = 3
MOBA_UNROLL = 4
ROPE_THETA = 10000.0
N_GROUPS = 4
EXPERTS_PER_GROUP = 4
N_EXPERTS = N_GROUPS * EXPERTS_PER_GROUP
D_EXPERT = 256
LN_EPS = 1e-5
RMS_EPS = 1e-6
L2_EPS = 1e-6
NEG_INF = -1e30
LOG2E = 1.4426950408889634
DEPTH = 1
DEEPNORM_ALPHA = (2 * DEPTH) ** 0.25

LANES = 128
IN_PROJ_TM = 512
IN_PROJ_GROUP = 512
DN_TILE = 256
DN_HEADS_PER_STEP = 4
DN_SEQS_PER_STEP = 2
GATE_LANE0 = N_GROUPS
N_PAIRS = EXPERTS_PER_GROUP * (EXPERTS_PER_GROUP - 1) // 2
N_BUCKETS = N_GROUPS * N_PAIRS
ROUTE_BUCKET, ROUTE_W_LO, ROUTE_W_HI = 0, 1, 2
MOE_TS = 512
MOE_TM = 512
GRAN = 16
GRAN_SHIFT = 4
LROWS = -(-(MOE_TS + N_BUCKETS * (GRAN - 1)) // LANES) * LANES
LGRAN = LROWS // GRAN
PERM_CHUNKS = 4
VMEM_LIMIT = 48 * 1024 * 1024


def _dot(a, b):
    return jnp.dot(a.astype(BF16), b.astype(BF16), preferred_element_type=F32)


def _dot_nt(a, b):
    return lax.dot_general(a.astype(BF16), b.astype(BF16), (((1,), (1,)), ((), ())),
                           preferred_element_type=F32)


def _split2(a):
    hi = a.astype(BF16)
    lo = (a - hi.astype(F32)).astype(BF16)
    return hi, lo


def _split3(a):
    hi = a.astype(BF16)
    r = a - hi.astype(F32)
    mid = r.astype(BF16)
    lo = (r - mid.astype(F32)).astype(BF16)
    return hi, mid, lo


def _dot3(a, b):
    ah, al = _split2(a)
    bh, bl = _split2(b)
    return (jnp.dot(ah, bh, preferred_element_type=F32) + jnp.dot(ah, bl, preferred_element_type=F32)
            + jnp.dot(al, bh, preferred_element_type=F32))


def _dot3_nt(a, b):
    ah, al = _split2(a)
    bh, bl = _split2(b)
    dn = (((1,), (1,)), ((), ()))
    return (lax.dot_general(ah, bh, dn, preferred_element_type=F32)
            + lax.dot_general(ah, bl, dn, preferred_element_type=F32)
            + lax.dot_general(al, bh, dn, preferred_element_type=F32))


def _dot_exact_lhs(a_bf16, b):
    bh, bm, bl = _split3(b)
    return (jnp.dot(a_bf16, bh, preferred_element_type=F32) + jnp.dot(a_bf16, bm, preferred_element_type=F32)
            + jnp.dot(a_bf16, bl, preferred_element_type=F32))


def _silu(x):
    return x * jax.nn.sigmoid(x)


def _softplus(x):
    return jnp.maximum(x, 0.0) + jnp.log1p(jnp.exp(-jnp.abs(x)))


def _layer_norm(t, g, b):
    mu = jnp.mean(t, axis=-1, keepdims=True)
    d = t - mu
    var = jnp.mean(d * d, axis=-1, keepdims=True)
    return d * lax.rsqrt(var + LN_EPS) * g + b


def _lane_pick(x, lane):
    ids = lax.broadcasted_iota(jnp.int32, x.shape, 1)
    return jnp.sum(jnp.where(ids == lane, x, 0.0), axis=1, keepdims=True)


def _in_proj_kernel(x_ref, w_ref, cw_ref, cos_ref, sin_ref, dn_ref, z_ref, ba_ref, q_ref, k_ref, vt_ref, sel_ref,
                    cb_ref, km_ref, *, tiles_per_seq, nb, topk):
    i = pl.program_id(0)
    tm = x_ref.shape[0]
    o0 = 3 * D_DN
    o1 = o0 + D_DN
    o2 = o1 + LANES
    seq_tile = i % tiles_per_seq
    spb = tm // MOBA_BLOCK

    @pl.when(seq_tile == 0)
    def _():
        cb_ref[0:8, :] = jnp.zeros((8, o0), F32)
        km_ref[...] = jnp.zeros_like(km_ref)

    half = HEAD_DIM // 2
    heads = [slice(h * HEAD_DIM, (h + 1) * HEAD_DIM) for h in range(N_HEADS_MOBA)]
    rows = [slice(sb * MOBA_BLOCK, (sb + 1) * MOBA_BLOCK) for sb in range(spb)]

    def rope(t, rs):
        return t * cos_ref[rs, :] + pltpu.roll(t, half, 1) * sin_ref[rs, :]

    xb = x_ref[...].astype(BF16)
    gw = IN_PROJ_GROUP
    qr = None
    for grp in range(3 * D_DN // gw):
        cs = slice(grp * gw, (grp + 1) * gw)
        u = jnp.dot(xb, w_ref[:, cs], preferred_element_type=F32)
        mb = jnp.dot(xb, w_ref[:, o2 + grp * gw:o2 + (grp + 1) * gw], preferred_element_type=F32)
        cb_ref[8:8 + tm, cs] = u
        acc = cw_ref[CONV_K - 1:CONV_K, cs] * u
        for s in range(1, CONV_K):
            acc = acc + cw_ref[CONV_K - 1 - s:CONV_K - s, cs] * cb_ref[8 - s:8 - s + tm, cs]
        cb_ref[0:8, cs] = u[tm - 8:tm, :]
        act = _silu(acc)
        if grp * gw < 2 * D_DN:
            outs = []
            for h in range(gw // HEAD_DIM):
                t = act[:, h * HEAD_DIM:(h + 1) * HEAD_DIM]
                t = t * lax.rsqrt(jnp.sum(t * t, axis=-1, keepdims=True) + L2_EPS)
                outs.append(t * (HEAD_DIM ** -0.5) if grp * gw < D_DN else t)
            act = jnp.concatenate(outs, axis=1)
        dn_ref[:, cs] = act

        if grp == 0:
            qr = [[rope(mb[rs, hs], rs) * (HEAD_DIM ** -0.5) for hs in heads] for rs in rows]
            for sb, rs in enumerate(rows):
                q_ref[rs, :] = jnp.concatenate([q * LOG2E for q in qr[sb]], axis=1).astype(q_ref.dtype)
        elif grp == 1:
            blk = lax.broadcasted_iota(jnp.int32, (nb, MOBA_BLOCK), 0)
            kmrow = lax.broadcasted_iota(jnp.int32, (nb, D_MOBA), 0)
            km = km_ref[...]
            for sb, rs in enumerate(rows):
                j = seq_tile * spb + sb
                kr = [rope(mb[rs, hs], rs) for hs in heads]
                k_ref[rs, :] = jnp.concatenate(kr, axis=1).astype(k_ref.dtype)
                for h, hs in enumerate(heads):
                    gate = _dot3_nt(km[:, hs], qr[sb][h])
                    gate = jnp.where(blk < j, gate, NEG_INF)
                    rank = jnp.zeros(gate.shape, F32)
                    for m in range(nb):
                        gm = gate[m:m + 1, :]
                        ahead = (gm > gate) | ((gm == gate) & (blk > m))
                        rank = rank + jnp.where(ahead, 1.0, 0.0)
                    sel = (blk < j) & (rank < topk)
                    sel_ref[sb, h * nb:(h + 1) * nb, :] = jnp.where(sel, 1.0, 0.0)
                kmean = jnp.concatenate([jnp.mean(t, axis=0, keepdims=True) for t in kr], axis=1)
                km = jnp.where(kmrow == j, kmean, km)
            km_ref[...] = km
        else:
            for sb, rs in enumerate(rows):
                vt_ref[sb] = jnp.transpose(mb[rs, :]).astype(vt_ref.dtype)
    z_ref[...] = _silu(jnp.dot(xb, w_ref[:, o0:o1], preferred_element_type=F32))
    ba_ref[...] = jnp.dot(xb, w_ref[:, o1:o2], preferred_element_type=F32)


def _in_proj(x2, w_all, conv_w, cos_t, sin_t, tm, seq):
    n, d = x2.shape
    wc = w_all.shape[1]
    nb = seq // MOBA_BLOCK
    spb = tm // MOBA_BLOCK
    tps = seq // tm
    kern = functools.partial(_in_proj_kernel, tiles_per_seq=tps, nb=nb, topk=min(MOBA_TOPK, nb))
    tok = lambda width: pl.BlockSpec((tm, width), lambda i: (i, 0))
    tab = pl.BlockSpec((tm, HEAD_DIM), lambda i: (i % tps, 0))
    return pl.pallas_call(
        kern,
        out_shape=(jax.ShapeDtypeStruct((n, 3 * D_DN), F32), jax.ShapeDtypeStruct((n, D_DN), F32),
                   jax.ShapeDtypeStruct((n, LANES), F32),
                   jax.ShapeDtypeStruct((n, D_MOBA), BF16), jax.ShapeDtypeStruct((n, D_MOBA), BF16),
                   jax.ShapeDtypeStruct((n // MOBA_BLOCK, D_MOBA, MOBA_BLOCK), BF16),
                   jax.ShapeDtypeStruct((n // MOBA_BLOCK, N_HEADS_MOBA * nb, MOBA_BLOCK), F32)),
        grid=(n // tm,),
        in_specs=[tok(d), pl.BlockSpec((d, wc), lambda i: (0, 0)),
                  pl.BlockSpec((CONV_K, 3 * D_DN), lambda i: (0, 0)), tab, tab],
        out_specs=(tok(3 * D_DN), tok(D_DN), tok(LANES), tok(D_MOBA), tok(D_MOBA),
                   pl.BlockSpec((spb, D_MOBA, MOBA_BLOCK), lambda i: (i, 0, 0)),
                   pl.BlockSpec((spb, N_HEADS_MOBA * nb, MOBA_BLOCK), lambda i: (i, 0, 0))),
        scratch_shapes=[pltpu.VMEM((8 + tm, 3 * D_DN), F32), pltpu.VMEM((nb, D_MOBA), F32)],
        compiler_params=pltpu.CompilerParams(dimension_semantics=("arbitrary",), vmem_limit_bytes=VMEM_LIMIT),
        name="in_proj",
    )(x2, w_all, conv_w, cos_t, sin_t)


def _deltanet_kernel(q_ref, k_ref, v_ref, z_ref, ba_ref, alog_ref, dtb_ref, normw_ref, y_ref,
                     s_ref, wq_s, u_s, qk_s, kdt_s, egl_s, *, hb):
    hg = pl.program_id(1)
    t = pl.program_id(2)
    tt = DN_TILE
    nchunk = tt // DN_CHUNK
    nbat = q_ref.shape[0]
    hs = range(nbat * hb)

    @pl.when(t == 0)
    def _():
        for ref in (s_ref, wq_s, u_s, qk_s, kdt_s, egl_s):
            ref[...] = jnp.zeros_like(ref)

    rd = t & 1
    wr = 1 - rd
    state = [s_ref[h] for h in hs]
    outs = [[] for _ in hs]

    pend = {}

    def chain_a(c):
        pend["r"] = [jnp.dot(wq_s[rd, h, c], state[h].astype(BF16), preferred_element_type=F32) for h in hs]

    def chain_b(c):
        lo, hi = c * DN_CHUNK, (c + 1) * DN_CHUNK
        r = pend["r"]
        vz = []
        for h in hs:
            parts = []
            if lo > 0:
                parts.append(jnp.zeros((lo, HEAD_DIM), F32))
            parts.append(u_s[rd, h, lo:hi, :] - r[h][0:DN_CHUNK, :])
            if hi < tt:
                parts.append(jnp.zeros((tt - hi, HEAD_DIM), F32))
            vz.append(jnp.concatenate(parts, axis=0).astype(BF16))
        for h in hs:
            outs[h].append(r[h][DN_CHUNK:2 * DN_CHUNK, :]
                           + jnp.dot(qk_s[rd, h, lo:hi, :], vz[h], preferred_element_type=F32))
        for h in hs:
            state[h] = (state[h] * egl_s[rd, h, 8 * c:8 * c + 1, :]
                        + jnp.dot(kdt_s[rd, h], vz[h], preferred_element_type=F32))

    chain_a(0)

    bas = [ba_ref[bb] for bb in range(nbat)]
    beta_all = [jax.nn.sigmoid(ba) for ba in bas]
    g_all = [-jnp.exp(alog_ref[...]) * _softplus(ba + dtb_ref[...]) for ba in bas]

    row = lax.broadcasted_iota(jnp.int32, (tt, tt), 0)
    col = lax.broadcasted_iota(jnp.int32, (tt, tt), 1)
    same = (row >> 6) == (col >> 6)
    incl = same & (row >= col)
    strict = same & (row > col)

    incl_b = incl.astype(BF16)
    gc_all = [_dot_exact_lhs(incl_b, g) for g in g_all]
    gct = [jnp.transpose(g) for g in gc_all]
    sub = lax.broadcasted_iota(jnp.int32, gct[0].shape, 0)
    bat = [vh // hb for vh in hs]
    sls = [slice((vh % hb) * HEAD_DIM, (vh % hb + 1) * HEAD_DIM) for vh in hs]
    heads = [hg * hb + vh % hb for vh in hs]
    q = [q_ref[bat[vh], :, sls[vh]] for vh in hs]
    k = [k_ref[bat[vh], :, sls[vh]] for vh in hs]
    v = [v_ref[bat[vh], :, sls[vh]] for vh in hs]
    beta = [_lane_pick(beta_all[bat[vh]], heads[vh]) for vh in hs]
    gcc = [_lane_pick(gc_all[bat[vh]], heads[vh] + N_HEADS_DN) for vh in hs]
    gcr = [jnp.sum(jnp.where(sub == heads[vh] + N_HEADS_DN, gct[bat[vh]], 0.0), axis=0, keepdims=True)
           for vh in hs]
    chain_b(0)

    decay = [jnp.where(incl, jnp.exp(jnp.where(incl, gcc[h] - gcr[h], 0.0)), 0.0) for h in hs]
    kb = [k[h] * beta[h] for h in hs]
    vb = [v[h] * beta[h] for h in hs]
    a_mat = [jnp.where(strict, _dot_nt(kb[h], k[h]) * decay[h], 0.0) for h in hs]
    chain_a(1)
    qk = [_dot_nt(q[h], k[h]) * decay[h] for h in hs]
    eye = (row == col).astype(F32)
    d8 = (row >> 3) == (col >> 3)
    a8 = [jnp.where(d8, a, 0.0) for a in a_mat]
    chain_b(1)
    a8_2 = [_dot(a, a) for a in a8]
    chain_a(2)
    a8_4 = [_dot(a, a) for a in a8_2]
    chain_b(2)
    x = [_dot(eye - a, eye + a2) for a, a2 in zip(a8, a8_2)]
    chain_a(3)
    x = [_dot(xi, eye + a4) for xi, a4 in zip(x, a8_4)]
    chain_b(3)
    s = 8
    while s < DN_CHUNK:
        sh = s.bit_length() - 1
        off = ((row >> (sh + 1)) == (col >> (sh + 1))) & ((row >> sh) != (col >> sh))
        y = [_dot(jnp.where(off, a, 0.0), xi) for a, xi in zip(a_mat, x)]
        x = [xi - _dot(xi, yi) for xi, yi in zip(x, y)]
        s *= 2
    tinv = x
    eg = [jnp.exp(g) for g in gcc]
    wu = [_dot(tinv[h], jnp.concatenate([kb[h] * eg[h], vb[h]], axis=1)) for h in hs]
    qd = [q[h] * eg[h] for h in hs]
    gl_rows = [[g[(c + 1) * DN_CHUNK - 1:(c + 1) * DN_CHUNK, :] for c in range(nchunk)] for g in gcc]
    gl_col = [jnp.concatenate([jnp.broadcast_to(g, (DN_CHUNK, 1)) for g in rows], axis=0) for rows in gl_rows]
    kdt = [jnp.transpose(k[h] * jnp.exp(gl_col[h] - gcc[h])) for h in hs]

    ys = []
    for h in hs:
        o = jnp.concatenate(outs[h], axis=0)
        o = o * lax.rsqrt(jnp.mean(o * o, axis=-1, keepdims=True) + RMS_EPS) * normw_ref[...]
        ys.append(o * z_ref[bat[h], :, sls[h]])
    for bb in range(nbat):
        y_ref[bb] = jnp.concatenate(ys[bb * hb:(bb + 1) * hb], axis=1).astype(y_ref.dtype)
    s_ref[...] = jnp.stack(state, axis=0)

    for h in hs:
        for c in range(nchunk):
            lo, hi = c * DN_CHUNK, (c + 1) * DN_CHUNK
            wq_s[wr, h, c] = jnp.concatenate([wu[h][lo:hi, 0:HEAD_DIM], qd[h][lo:hi, :]], axis=0).astype(BF16)
            egl_s[wr, h, 8 * c:8 * c + 8, :] = jnp.broadcast_to(jnp.exp(gl_rows[h][c]), (8, HEAD_DIM))
        u_s[wr, h] = wu[h][:, HEAD_DIM:2 * HEAD_DIM]
        qk_s[wr, h] = qk[h].astype(BF16)
        kdt_s[wr, h] = kdt[h].astype(BF16)


def _deltanet(dn_qkv, z, ba, alog_row, dtb_row, normw_row):
    bsz, seq, _ = dn_qkv.shape
    tt = DN_TILE
    nt = seq // tt
    hb = DN_HEADS_PER_STEP
    nbat = DN_SEQS_PER_STEP if bsz % DN_SEQS_PER_STEP == 0 else 1
    nv = nbat * hb
    ng = N_HEADS_DN // hb
    w = hb * HEAD_DIM
    nchunk = tt // DN_CHUNK

    def cur_spec(off, width):
        return pl.BlockSpec((nbat, tt, width), lambda b, g, t: (b, jnp.minimum(t, nt - 1), g + off))

    prev_spec = pl.BlockSpec((nbat, tt, w), lambda b, g, t: (b, jnp.maximum(t - 1, 0), g))
    row_spec = pl.BlockSpec((1, LANES), lambda b, g, t: (0, 0))
    return pl.pallas_call(
        functools.partial(_deltanet_kernel, hb=hb),
        out_shape=jax.ShapeDtypeStruct((bsz, seq, D_DN), BF16),
        grid=(bsz // nbat, ng, nt + 1),
        in_specs=[cur_spec(0, w), cur_spec(ng, w), cur_spec(2 * ng, w), prev_spec,
                  pl.BlockSpec((nbat, tt, LANES), lambda b, g, t: (b, jnp.minimum(t, nt - 1), 0)),
                  row_spec, row_spec, row_spec],
        out_specs=prev_spec,
        scratch_shapes=[pltpu.VMEM((nv, HEAD_DIM, HEAD_DIM), F32),
                        pltpu.VMEM((2, nv, nchunk, 2 * DN_CHUNK, HEAD_DIM), BF16),
                        pltpu.VMEM((2, nv, tt, HEAD_DIM), F32),
                        pltpu.VMEM((2, nv, tt, tt), BF16),
                        pltpu.VMEM((2, nv, HEAD_DIM, tt), BF16),
                        pltpu.VMEM((2, nv, 8 * nchunk, HEAD_DIM), F32)],
        compiler_params=pltpu.CompilerParams(dimension_semantics=("parallel", "parallel", "arbitrary"),
                                             vmem_limit_bytes=VMEM_LIMIT),
        name="deltanet",
    )(dn_qkv, dn_qkv, dn_qkv, z, ba, alog_row, dtb_row, normw_row)


def _moba_attn_kernel(q_ref, k_ref, vt_ref, sel_ref, o_ref, acc_ref, *, nb):
    j = pl.program_id(1)
    blk = MOBA_BLOCK
    nh = N_HEADS_MOBA
    dn = (((1,), (1,)), ((), ()))
    hsl = [slice(h * HEAD_DIM, (h + 1) * HEAD_DIM) for h in range(nh)]
    qs = [q_ref[0, :, hsl[h]] for h in range(nh)]

    ki = lax.broadcasted_iota(jnp.int32, (blk, blk), 0)
    qi = lax.broadcasted_iota(jnp.int32, (blk, blk), 1)
    own = pl.ds(pl.multiple_of(j * blk, blk), blk)
    s_own = [jnp.where(ki <= qi, lax.dot_general(k_ref[0, own, hsl[h]], qs[h], dn, preferred_element_type=F32),
                       NEG_INF) for h in range(nh)]

    def scores(n, h):
        kn = k_ref[0, pl.ds(pl.multiple_of(n * blk, blk), blk), hsl[h]]
        s = lax.dot_general(kn, qs[h], dn, preferred_element_type=F32)
        return jnp.where(sel_ref[0, 0, pl.ds(h * nb + n, 1), :] > 0.5, s, NEG_INF)

    ones8 = jnp.ones((8, blk), BF16)

    def softmax_pv(n, h, s, m, l):
        m_new = jnp.maximum(m, jnp.max(s, axis=0, keepdims=True))
        pb = jnp.exp2(s - m_new).astype(BF16)
        psum = jnp.dot(ones8, pb, preferred_element_type=F32)[0:1, :]
        pv = jnp.dot(vt_ref[0, n, hsl[h], :], pb, preferred_element_type=F32)
        return m_new, psum, pv

    ms, ls = [], []
    for h in range(nh):
        m, psum, pv = softmax_pv(j, h, s_own[h], jnp.full((1, blk), NEG_INF, F32), None)
        ms.append(m)
        ls.append(psum)
        acc_ref[h] = pv

    def body(step, carry):
        ms, ls = (list(t) for t in carry)
        n0 = MOBA_UNROLL * step
        cur = [scores(n0, h) for h in range(nh)]
        for i in range(MOBA_UNROLL):
            nxt = []
            for h in range(nh):
                if i + 1 < MOBA_UNROLL:
                    nxt.append(scores(n0 + i + 1, h))
                m_new, psum, pv = softmax_pv(n0 + i, h, cur[h], ms[h], ls[h])
                alpha = jnp.exp2(ms[h] - m_new)
                ls[h] = alpha * ls[h] + psum
                ms[h] = m_new
                acc_ref[h] = acc_ref[h] * alpha + pv
            cur = nxt
        return tuple(ms), tuple(ls)

    ms, ls = lax.fori_loop(0, (j + MOBA_UNROLL - 1) // MOBA_UNROLL, body, (tuple(ms), tuple(ls)))
    o_ref[0] = jnp.concatenate([jnp.transpose(acc_ref[h] / ls[h]) for h in range(nh)],
                               axis=1).astype(o_ref.dtype)


def _moba_attn(q_r, k_r, vt, sel):
    bsz, seq, _ = q_r.shape
    nb = seq // MOBA_BLOCK
    assert nb % MOBA_UNROLL == 0, (nb, MOBA_UNROLL)
    tok_spec = pl.BlockSpec((1, MOBA_BLOCK, D_MOBA), lambda b, j: (b, j, 0))
    return pl.pallas_call(
        functools.partial(_moba_attn_kernel, nb=nb),
        out_shape=jax.ShapeDtypeStruct((bsz, seq, D_MOBA), BF16),
        grid=(bsz, nb),
        in_specs=[tok_spec,
                  pl.BlockSpec((1, seq, D_MOBA), lambda b, j: (b, 0, 0)),
                  pl.BlockSpec((1, nb, D_MOBA, MOBA_BLOCK), lambda b, j: (b, 0, 0, 0)),
                  pl.BlockSpec((1, 1, N_HEADS_MOBA * nb, MOBA_BLOCK), lambda b, j: (b, j, 0, 0))],
        out_specs=tok_spec,
        scratch_shapes=[pltpu.VMEM((N_HEADS_MOBA, HEAD_DIM, MOBA_BLOCK), F32)],
        compiler_params=pltpu.CompilerParams(dimension_semantics=("parallel", "arbitrary"),
                                             vmem_limit_bytes=VMEM_LIMIT),
        name="moba_attn",
    )(q_r, k_r, vt, sel)


def _route_record(logits):
    lane = lax.broadcasted_iota(jnp.int32, logits.shape, 1)
    big = jnp.int32(LANES)

    def first_lane(mask):
        return jnp.min(jnp.where(mask, lane, big), axis=1, keepdims=True)

    is_g = lane < N_GROUPS
    m1 = jnp.max(jnp.where(is_g, logits, NEG_INF), axis=1, keepdims=True)
    s1 = jnp.sum(jnp.where(is_g, jnp.exp(logits - m1), 0.0), axis=1, keepdims=True)
    pg = 1.0 / s1
    gsel = first_lane(is_g & (logits == m1))

    in_grp = (lane >= GATE_LANE0) & (((lane - GATE_LANE0) >> 2) == gsel) & (lane < GATE_LANE0 + N_EXPERTS)
    m2 = jnp.max(jnp.where(in_grp, logits, NEG_INF), axis=1, keepdims=True)
    s2 = jnp.sum(jnp.where(in_grp, jnp.exp(logits - m2), 0.0), axis=1, keepdims=True)
    e1 = first_lane(in_grp & (logits == m2))
    rest = in_grp & (lane != e1)
    m2b = jnp.max(jnp.where(rest, logits, NEG_INF), axis=1, keepdims=True)
    e2 = first_lane(rest & (logits == m2b))
    pe1 = 1.0 / s2
    pe2 = jnp.exp(m2b - m2) / s2
    tot = pe1 + pe2
    w1 = pg * (pe1 / tot)
    w2 = pg * (pe2 / tot)
    first_lo = e1 < e2
    lo = jnp.minimum(e1, e2)
    hi = jnp.maximum(e1, e2)
    a = (lo - GATE_LANE0) & (EXPERTS_PER_GROUP - 1)
    b = (hi - GATE_LANE0) & (EXPERTS_PER_GROUP - 1)
    bucket = gsel * N_PAIRS + ((a * (2 * EXPERTS_PER_GROUP - 1 - a)) >> 1) + (b - a - 1)
    record = jnp.where(lane == ROUTE_BUCKET, bucket.astype(F32),
                       jnp.where(lane == ROUTE_W_LO, jnp.where(first_lo, w1, w2),
                                 jnp.where(lane == ROUTE_W_HI, jnp.where(first_lo, w2, w1), 0.0)))
    return record, jnp.sum(jnp.where(lane == bucket, 1.0, 0.0), axis=0, keepdims=True)


def _mix_route_kernel(ydn_ref, ymb_ref, x_ref, wo_ref, g_ref, b_ref, rw_ref, rb_ref, h_ref, hb_ref, route_ref,
                      cnt_ref):
    nsub = cnt_ref.shape[0]
    rows = [slice(i * MOE_TS, (i + 1) * MOE_TS) for i in range(nsub)]
    wh, wl = _split2(rw_ref[...])
    wcat = jnp.concatenate([wh, wl], axis=1)

    mixes = [jnp.dot(jnp.concatenate([ydn_ref[r, :], ymb_ref[r, :]], axis=1), wo_ref[...],
                     preferred_element_type=F32) for r in rows]
    logits = []
    for r, mix in zip(rows, mixes):
        hval = _layer_norm(DEEPNORM_ALPHA * x_ref[r, :] + mix, g_ref[...], b_ref[...])
        h_ref[r, :] = hval
        hb_ref[r, :] = hval.astype(BF16)
        hh, hl = _split2(hval)
        both = jnp.dot(hh, wcat, preferred_element_type=F32)
        logits.append(both[:, 0:LANES] + both[:, LANES:2 * LANES]
                      + jnp.dot(hl, wh, preferred_element_type=F32) + rb_ref[...])
    for i, r in enumerate(rows):
        record, counts = _route_record(logits[i])
        route_ref[r, :] = record
        cnt_ref[i] = counts


def _mix_route(y_dn, y_mb, x2, wo, g1, b1, rw, rb, tm):
    n, d = x2.shape
    nsub = tm // MOE_TS
    row = lambda w: pl.BlockSpec((1, w), lambda i: (0, 0))
    return pl.pallas_call(
        _mix_route_kernel,
        out_shape=(jax.ShapeDtypeStruct((n, d), F32), jax.ShapeDtypeStruct((n, d), BF16),
                   jax.ShapeDtypeStruct((n, LANES), F32), jax.ShapeDtypeStruct((n // MOE_TS, 1, LANES), F32)),
        grid=(n // tm,),
        in_specs=[pl.BlockSpec((tm, D_DN), lambda i: (i, 0)), pl.BlockSpec((tm, D_MOBA), lambda i: (i, 0)),
                  pl.BlockSpec((tm, d), lambda i: (i, 0)), pl.BlockSpec((D_DN + D_MOBA, d), lambda i: (0, 0)),
                  row(d), row(d), pl.BlockSpec((d, LANES), lambda i: (0, 0)), row(LANES)],
        out_specs=(pl.BlockSpec((tm, d), lambda i: (i, 0)), pl.BlockSpec((tm, d), lambda i: (i, 0)),
                   pl.BlockSpec((tm, LANES), lambda i: (i, 0)), pl.BlockSpec((nsub, 1, LANES), lambda i: (i, 0, 0))),
        compiler_params=pltpu.CompilerParams(dimension_semantics=("parallel",), vmem_limit_bytes=VMEM_LIMIT),
        name="mix_route",
    )(y_dn, y_mb, x2, wo, g1, b1, rw, rb)


def _bucket_offsets_col(ohf):
    cnt = jnp.sum(ohf, axis=1, keepdims=True).astype(jnp.int32)
    pad = (((cnt + (GRAN - 1)) >> GRAN_SHIFT) << GRAN_SHIFT).astype(F32)
    r = lax.broadcasted_iota(jnp.int32, (LANES, LANES), 0)
    c = lax.broadcasted_iota(jnp.int32, (LANES, LANES), 1)
    before = jnp.where(c < r, 1.0, 0.0)
    return _dot(before, jnp.broadcast_to(pad, (LANES, LANES)))[:, 0:1]


def _moe_sort_kernel(gmap_ref, nvalid_ref, tail0_ref, taillen_ref, hb_ref, route_ref, lstrict_ref,
                     xg_ref, wsg_ref, xs_ref, ws_ref, zx_ref, zw_ref, sem):
    s = pl.program_id(0)
    nsteps = pl.num_programs(0)
    slot = s & 1
    ts = route_ref.shape[0]
    route = route_ref[...]
    rt = jnp.transpose(route)
    bucket_row = rt[ROUTE_BUCKET:ROUTE_BUCKET + 1, :].astype(jnp.int32)
    sub = lax.broadcasted_iota(jnp.int32, (LANES, ts), 0)
    ohf = jnp.where(sub == bucket_row, 1.0, 0.0)
    loff = _bucket_offsets_col(ohf)
    rank = lax.dot_general(ohf.astype(BF16), lstrict_ref[...], (((1,), (1,)), ((), ())),
                           preferred_element_type=F32)
    dest = jnp.sum(ohf * (loff + rank), axis=0, keepdims=True).astype(jnp.int32)
    rh, rl = _split2(route)
    wcat = jnp.concatenate([rh, rl], axis=1)
    rc = LROWS // PERM_CHUNKS
    for c in range(PERM_CHUNKS):
        rowi = lax.broadcasted_iota(jnp.int32, (rc, ts), 0) + c * rc
        perm = jnp.where(rowi == dest, 1.0, 0.0).astype(BF16)
        xs_ref[slot, c * rc:(c + 1) * rc, :] = jnp.dot(perm, hb_ref[...], preferred_element_type=F32).astype(BF16)
        wparts = jnp.dot(perm, wcat, preferred_element_type=F32)
        ws_ref[slot, c * rc:(c + 1) * rc, :] = wparts[:, 0:LANES] + wparts[:, LANES:2 * LANES]

    def copies(step, g):
        sl = step & 1
        src = pl.ds(pl.multiple_of(g * GRAN, GRAN), GRAN)
        dst = pl.ds(pl.multiple_of(gmap_ref[step * LGRAN + g] * GRAN, GRAN), GRAN)
        return (pltpu.make_async_copy(xs_ref.at[sl, src, :], xg_ref.at[dst, :], sem.at[0, sl]),
                pltpu.make_async_copy(ws_ref.at[sl, src, :], wsg_ref.at[dst, :], sem.at[1, sl]))

    def fill_copies(b, i):
        dst = pl.ds(pl.multiple_of((tail0_ref[b] + i) * GRAN, GRAN), GRAN)
        return (pltpu.make_async_copy(zx_ref.at[0:GRAN, :], xg_ref.at[dst, :], sem.at[2, 0]),
                pltpu.make_async_copy(zw_ref.at[0:GRAN, :], wsg_ref.at[dst, :], sem.at[2, 1]))

    def unused_tile_copies(t):
        dst = pl.ds(pl.multiple_of(t * MOE_TM, MOE_TM), MOE_TM)
        return (pltpu.make_async_copy(zx_ref, xg_ref.at[dst, :], sem.at[2, 0]),
                pltpu.make_async_copy(zw_ref, wsg_ref.at[dst, :], sem.at[2, 1]))

    def run(step, fn):
        def body(g, carry):
            for cp in copies(step, g):
                fn(cp)
            return carry
        lax.fori_loop(0, nvalid_ref[step], body, 0)

    def run_fill(fn):
        for b in range(N_BUCKETS):
            def body(i, carry, b=b):
                for cp in fill_copies(b, i):
                    fn(cp)
                return carry
            lax.fori_loop(0, taillen_ref[b], body, 0)

        def tile_body(t, carry):
            for cp in unused_tile_copies(t):
                fn(cp)
            return carry
        lax.fori_loop(tail0_ref[N_BUCKETS], xg_ref.shape[0] // MOE_TM, tile_body, 0)

    @pl.when(s == 0)
    def _():
        zx_ref[...] = jnp.zeros_like(zx_ref)
        zw_ref[...] = jnp.zeros_like(zw_ref)
        run_fill(lambda cp: cp.start())

    run(s, lambda cp: cp.start())

    @pl.when(s > 0)
    def _():
        run(s - 1, lambda cp: cp.wait())

    @pl.when(s == nsteps - 1)
    def _():
        run(s, lambda cp: cp.wait())
        run_fill(lambda cp: cp.wait())


def _moe_sort(plan, hb, route, lstrict):
    n, d = hb.shape
    ts = MOE_TS
    rows = plan["n_tiles"] * MOE_TM
    return pl.pallas_call(
        _moe_sort_kernel,
        out_shape=(jax.ShapeDtypeStruct((rows, d), BF16), jax.ShapeDtypeStruct((rows, LANES), F32)),
        grid_spec=pltpu.PrefetchScalarGridSpec(
            num_scalar_prefetch=4,
            grid=(n // ts,),
            in_specs=[pl.BlockSpec((ts, d), lambda s, *_: (s, 0)), pl.BlockSpec((ts, LANES), lambda s, *_: (s, 0)),
                      pl.BlockSpec((ts, ts), lambda s, *_: (0, 0))],
            out_specs=(pl.BlockSpec(memory_space=pl.ANY), pl.BlockSpec(memory_space=pl.ANY)),
            scratch_shapes=[pltpu.VMEM((2, LROWS, d), BF16), pltpu.VMEM((2, LROWS, LANES), F32),
                            pltpu.VMEM((MOE_TM, d), BF16), pltpu.VMEM((MOE_TM, LANES), F32),
                            pltpu.SemaphoreType.DMA((3, 2))]),
        compiler_params=pltpu.CompilerParams(dimension_semantics=("arbitrary",), vmem_limit_bytes=VMEM_LIMIT),
        name="moe_sort",
    )(plan["gmap"], plan["nvalid"], plan["tail0"], plan["taillen"], hb, route, lstrict)


def _moe_expert_kernel(xt_ref, elo_ref, ehi_ref, valid_ref, x_ref, w_ref, wg0, wu0, wd0, wg1, wu1, wd1, o_ref):
    t = pl.program_id(0)

    @pl.when(valid_ref[t] > 0)
    def _():
        x = x_ref[...]
        w = w_ref[...]
        gates = [jnp.dot(x, wg[0].astype(BF16), preferred_element_type=F32) for wg in (wg0, wg1)]
        ups = [jnp.dot(x, wu[0].astype(BF16), preferred_element_type=F32) for wu in (wu0, wu1)]
        hes = [(_silu(gates[i]) * ups[i] * w[:, lane:lane + 1]).astype(BF16)
               for i, lane in enumerate((ROUTE_W_LO, ROUTE_W_HI))]
        o_ref[...] = (jnp.dot(hes[0], wd0[0].astype(BF16), preferred_element_type=F32)
                      + jnp.dot(hes[1], wd1[0].astype(BF16), preferred_element_type=F32)).astype(o_ref.dtype)

    @pl.when(valid_ref[t] == 0)
    def _():
        o_ref[...] = jnp.zeros_like(o_ref)


def _moe_experts(plan, xg, wsg, wg, wu, wd):
    rows, d = xg.shape
    tm = MOE_TM
    tok = lambda width: pl.BlockSpec((tm, width), lambda t, xt, elo, ehi, valid: (xt[t], 0))
    lo3 = lambda shape: pl.BlockSpec(shape, lambda t, xt, elo, ehi, valid: (elo[t], 0, 0))
    hi3 = lambda shape: pl.BlockSpec(shape, lambda t, xt, elo, ehi, valid: (ehi[t], 0, 0))
    return pl.pallas_call(
        _moe_expert_kernel,
        out_shape=jax.ShapeDtypeStruct((rows, d), BF16),
        grid_spec=pltpu.PrefetchScalarGridSpec(
            num_scalar_prefetch=4,
            grid=(rows // tm,),
            in_specs=[tok(d), tok(LANES),
                      lo3((1, d, D_EXPERT)), lo3((1, d, D_EXPERT)), lo3((1, D_EXPERT, d)),
                      hi3((1, d, D_EXPERT)), hi3((1, d, D_EXPERT)), hi3((1, D_EXPERT, d))],
            out_specs=pl.BlockSpec((tm, d), lambda t, *_: (t, 0))),
        compiler_params=pltpu.CompilerParams(dimension_semantics=("arbitrary",), vmem_limit_bytes=VMEM_LIMIT),
        name="moe_experts",
    )(plan["xtile"], plan["elo"], plan["ehi"], plan["valid"], xg, wsg, wg, wu, wd, wg, wu, wd)


def _moe_unsort_kernel(gmap_ref, og_ref, route_ref, h_ref, lstrict_ref, g_ref, b_ref, out_ref, ol_ref, sem):
    s = pl.program_id(0)
    nsteps = pl.num_programs(0)
    slot = s & 1
    ts = route_ref.shape[0]

    def gather(step, fn):
        sl = step & 1

        def body(g, carry):
            src = pl.ds(pl.multiple_of(gmap_ref[step * LGRAN + g] * GRAN, GRAN), GRAN)
            dst = pl.ds(pl.multiple_of(g * GRAN, GRAN), GRAN)
            fn(pltpu.make_async_copy(og_ref.at[src, :], ol_ref.at[sl, dst, :], sem.at[sl]))
            return carry
        lax.fori_loop(0, LGRAN, body, 0)

    @pl.when(s == 0)
    def _():
        gather(s, lambda cp: cp.start())

    @pl.when(s + 1 < nsteps)
    def _():
        gather(s + 1, lambda cp: cp.start())

    route = route_ref[...]
    bucket_col = route[:, ROUTE_BUCKET:ROUTE_BUCKET + 1].astype(jnp.int32)
    lane = lax.broadcasted_iota(jnp.int32, (ts, LANES), 1)
    ohf = jnp.where(lane == bucket_col, 1.0, 0.0)
    cnt = jnp.sum(ohf, axis=0, keepdims=True).astype(jnp.int32)
    pad = (((cnt + (GRAN - 1)) >> GRAN_SHIFT) << GRAN_SHIFT).astype(F32)
    r = lax.broadcasted_iota(jnp.int32, (LANES, LANES), 0)
    c = lax.broadcasted_iota(jnp.int32, (LANES, LANES), 1)
    loff = _dot(jnp.broadcast_to(pad, (8, LANES)), jnp.where(r < c, 1.0, 0.0))[0:1, :]
    rank = jnp.dot(lstrict_ref[...], ohf.astype(BF16), preferred_element_type=F32)
    dest = jnp.sum(ohf * (loff + rank), axis=1, keepdims=True).astype(jnp.int32)
    gather(s, lambda cp: cp.wait())
    tc = ts // PERM_CHUNKS
    lrow = lax.broadcasted_iota(jnp.int32, (tc, LROWS), 1)
    for c in range(PERM_CHUNKS):
        rows = slice(c * tc, (c + 1) * tc)
        perm_t = jnp.where(lrow == dest[rows, :], 1.0, 0.0).astype(BF16)
        ffn = jnp.dot(perm_t, ol_ref[slot], preferred_element_type=F32)
        out_ref[rows, :] = _layer_norm(DEEPNORM_ALPHA * h_ref[rows, :] + ffn, g_ref[...], b_ref[...])


def _moe_unsort(plan, og, route, hf, lstrict, g2, b2):
    n, d = hf.shape
    ts = MOE_TS
    row = pl.BlockSpec((1, d), lambda s, *_: (0, 0))
    return pl.pallas_call(
        _moe_unsort_kernel,
        out_shape=jax.ShapeDtypeStruct((n, d), F32),
        grid_spec=pltpu.PrefetchScalarGridSpec(
            num_scalar_prefetch=1,
            grid=(n // ts,),
            in_specs=[pl.BlockSpec(memory_space=pl.ANY), pl.BlockSpec((ts, LANES), lambda s, *_: (s, 0)),
                      pl.BlockSpec((ts, d), lambda s, *_: (s, 0)), pl.BlockSpec((ts, ts), lambda s, *_: (0, 0)),
                      row, row],
            out_specs=pl.BlockSpec((ts, d), lambda s, *_: (s, 0)),
            scratch_shapes=[pltpu.VMEM((2, LROWS, d), BF16), pltpu.SemaphoreType.DMA((2,))]),
        compiler_params=pltpu.CompilerParams(dimension_semantics=("arbitrary",), vmem_limit_bytes=VMEM_LIMIT),
        name="moe_unsort",
    )(plan["gmap_back"], og, route, hf, lstrict, g2, b2)


def _moe_plan(cnt_half, n):
    nsrc = n // MOE_TS
    i32 = jnp.int32
    cnt = cnt_half.reshape(nsrc, -1, LANES).sum(axis=1)[:, :N_BUCKETS].astype(i32)
    run_g = (cnt + GRAN - 1) // GRAN
    nvalid = run_g.sum(axis=1)
    loff_g = jnp.cumsum(run_g, axis=1) - run_g
    bucket_g = run_g.sum(axis=0)
    gpt = MOE_TM // GRAN
    btiles = (bucket_g + gpt - 1) // gpt
    tend = jnp.cumsum(btiles)
    tstart = tend - btiles
    gofs = tstart[None, :] * gpt + jnp.cumsum(run_g, axis=0) - run_g
    n_tiles = -(-(n + nsrc * N_BUCKETS * (GRAN - 1)) // MOE_TM) + N_BUCKETS + 1
    g = jnp.arange(LGRAN, dtype=i32)[None, :, None]
    in_run = (g >= loff_g[:, None, :]) & (g < (loff_g + run_g)[:, None, :])
    gmap = jnp.arange(LGRAN, dtype=i32)[None, :] + jnp.sum(jnp.where(in_run, (gofs - loff_g)[:, None, :], 0), axis=2)
    is_valid = jnp.arange(LGRAN, dtype=i32)[None, :] < nvalid[:, None]
    zero_gran = (n_tiles - 1) * gpt
    t = jnp.arange(n_tiles, dtype=i32)
    tb = jnp.minimum(jnp.sum(t[:, None] >= tend[None, :], axis=1), N_BUCKETS - 1)
    valid = (t < tend[-1]).astype(i32)
    pairs = [(a, b) for a in range(EXPERTS_PER_GROUP) for b in range(a + 1, EXPERTS_PER_GROUP)]
    pidx = tb % N_PAIRS
    pair_a = sum(jnp.where(pidx == i, a, 0) for i, (a, _) in enumerate(pairs))
    pair_b = sum(jnp.where(pidx == i, b, 0) for i, (_, b) in enumerate(pairs))
    grp = tb // N_PAIRS
    return {
        "n_tiles": n_tiles,
        "gmap": jnp.where(is_valid, gmap, 0).reshape(-1).astype(i32),
        "gmap_back": jnp.where(is_valid, gmap, zero_gran).reshape(-1).astype(i32),
        "nvalid": nvalid.astype(i32),
        "tail0": jnp.concatenate([tstart * gpt + bucket_g, tend[-1:]]).astype(i32),
        "taillen": (btiles * gpt - bucket_g).astype(i32),
        "xtile": jnp.where(valid > 0, t, 0).astype(i32),
        "elo": (grp * EXPERTS_PER_GROUP + pair_a).astype(i32),
        "ehi": (grp * EXPERTS_PER_GROUP + pair_b).astype(i32),
        "valid": valid,
    }


def _pad_lanes(a, lane0=0):
    return jnp.zeros((1, LANES), F32).at[0, lane0:lane0 + a.shape[0]].set(a.astype(F32))


def _rope_tables(seq):
    half = HEAD_DIM // 2
    inv_freq = ROPE_THETA ** (-jnp.arange(half, dtype=F32) / half)
    ang = jnp.arange(seq).astype(F32)[:, None] * inv_freq[None, :]
    cos, sin = jnp.cos(ang), jnp.sin(ang)
    return jnp.concatenate([cos, cos], axis=-1), jnp.concatenate([-sin, sin], axis=-1)


def _layer(x, w_in, conv_w, a_log, dt_bias, dn_norm_w, w_out, ln1_g, ln1_b, router_w1, router_b1,
           router_w2, router_b2, w_gate, w_up, w_down, ln2_g, ln2_b):
    bsz, seq, d = x.shape
    n = bsz * seq
    x2 = x.reshape(n, d)

    o_z, o_b, o_mb = 3 * D_DN, 4 * D_DN, 4 * D_DN + 2 * N_HEADS_DN
    w_ba = jnp.pad(w_in[:, o_b:o_mb], ((0, 0), (0, LANES - 2 * N_HEADS_DN)))
    w_all = jnp.concatenate([w_in[:, :o_z], w_in[:, o_z:o_b], w_ba, w_in[:, o_mb:]], axis=1).astype(BF16)

    cos_t, sin_t = _rope_tables(seq)
    nb = seq // MOBA_BLOCK
    dn_qkv, z, ba, q_r, k_r, vt, sel = _in_proj(x2, w_all, conv_w, cos_t, sin_t, min(IN_PROJ_TM, seq), seq)

    y_dn = _deltanet(dn_qkv.reshape(bsz, seq, 3 * D_DN), z.reshape(bsz, seq, D_DN), ba.reshape(bsz, seq, LANES),
                     _pad_lanes(a_log, N_HEADS_DN), _pad_lanes(dt_bias, N_HEADS_DN),
                     dn_norm_w.astype(F32).reshape(1, HEAD_DIM))

    q_r, k_r = q_r.reshape(bsz, seq, D_MOBA), k_r.reshape(bsz, seq, D_MOBA)
    vt = vt.reshape(bsz, nb, D_MOBA, MOBA_BLOCK)
    sel = sel.reshape(bsz, nb, N_HEADS_MOBA * nb, MOBA_BLOCK)
    y_mb = _moba_attn(q_r, k_r, vt, sel)

    rw = jnp.concatenate([router_w1, jnp.transpose(router_w2, (1, 0, 2)).reshape(d, N_EXPERTS)], axis=1)
    rw = jnp.pad(rw, ((0, 0), (0, LANES - rw.shape[1])))
    rb = _pad_lanes(jnp.concatenate([router_b1, router_b2.reshape(-1)]))
    hf, hb, route, cnt = _mix_route(y_dn.reshape(n, D_DN), y_mb.reshape(n, D_MOBA), x2, w_out.astype(BF16),
                                    ln1_g.reshape(1, d), ln1_b.reshape(1, d), rw, rb, min(2 * MOE_TS, n))

    plan = _moe_plan(cnt, n)
    idx = jnp.arange(MOE_TS, dtype=jnp.int32)
    lstrict = (idx[None, :] < idx[:, None]).astype(BF16)
    xg, wsg = _moe_sort(plan, hb, route, lstrict)
    og = _moe_experts(plan, xg, wsg, w_gate, w_up, w_down)
    out = _moe_unsort(plan, og, route, hf, lstrict, ln2_g.reshape(1, d), ln2_b.reshape(1, d))
    return out.reshape(bsz, seq, d)


def kernel(x, w_in, conv_w, a_log, dt_bias, dn_norm_w, w_out, ln1_g, ln1_b, router_w1, router_b1, router_w2, router_b2, expert_w_gate, expert_w_up, expert_w_down, ln2_g, ln2_b):
    for l in range(DEPTH):
        x = _layer(x, w_in[l], conv_w[l], a_log[l], dt_bias[l], dn_norm_w[l], w_out[l], ln1_g[l], ln1_b[l],
                   router_w1[l], router_b1[l], router_w2[l], router_b2[l], expert_w_gate[l], expert_w_up[l],
                   expert_w_down[l], ln2_g[l], ln2_b[l])
    return x
```

```python
import functools

import jax
import jax.numpy as jnp
from jax import lax
from jax.experimental import pallas as pl
from jax.experimental.pallas import tpu as pltpu

F32 = jnp.float32
BF16 = jnp.bfloat16

HEAD_DIM = 128
N_HEADS_DN = 4
N_HEADS_MOBA = 4
D_DN = N_HEADS_DN * HEAD_DIM
D_MOBA = N_HEADS_MOBA * HEAD_DIM
CONV_K = 4
DN_CHUNK = 64
MOBA_BLOCK = 256
MOBA_TOPK = 3
MOBA_UNROLL = 4
ROPE_THETA = 10000.0
N_GROUPS = 4
EXPERTS_PER_GROUP = 4
N_EXPERTS = N_GROUPS * EXPERTS_PER_GROUP
D_EXPERT = 256
LN_EPS = 1e-5
RMS_EPS = 1e-6
L2_EPS = 1e-6
NEG_INF = -1e30
LOG2E = 1.4426950408889634
DEPTH = 1
DEEPNORM_ALPHA = (2 * DEPTH) ** 0.25

LANES = 128
IN_PROJ_TM = 512
IN_PROJ_GROUP = 512
DN_TILE = 256
DN_HEADS_PER_STEP = 4
DN_SEQS_PER_STEP = 2
GATE_LANE0 = N_GROUPS
N_PAIRS = EXPERTS_PER_GROUP * (EXPERTS_PER_GROUP - 1) // 2
N_BUCKETS = N_GROUPS * N_PAIRS
ROUTE_BUCKET, ROUTE_W_LO, ROUTE_W_HI = 0, 1, 2
MOE_TS = 512
MOE_TM = 512
GRAN = 16
GRAN_SHIFT = 4
LROWS = -(-(MOE_TS + N_BUCKETS * (GRAN - 1)) // LANES) * LANES
LGRAN = LROWS // GRAN
PERM_CHUNKS = 4
VMEM_LIMIT = 48 * 1024 * 1024


def _dot(a, b):
    return jnp.dot(a.astype(BF16), b.astype(BF16), preferred_element_type=F32)


def _dot_nt(a, b):
    return lax.dot_general(a.astype(BF16), b.astype(BF16), (((1,), (1,)), ((), ())),
                           preferred_element_type=F32)


def _split2(a):
    hi = a.astype(BF16)
    lo = (a - hi.astype(F32)).astype(BF16)
    return hi, lo


def _split3(a):
    hi = a.astype(BF16)
    r = a - hi.astype(F32)
    mid = r.astype(BF16)
    lo = (r - mid.astype(F32)).astype(BF16)
    return hi, mid, lo


def _dot3(a, b):
    ah, al = _split2(a)
    bh, bl = _split2(b)
    return (jnp.dot(ah, bh, preferred_element_type=F32) + jnp.dot(ah, bl, preferred_element_type=F32)
            + jnp.dot(al, bh, preferred_element_type=F32))


def _dot3_nt(a, b):
    ah, al = _split2(a)
    bh, bl = _split2(b)
    dn = (((1,), (1,)), ((), ()))
    return (lax.dot_general(ah, bh, dn, preferred_element_type=F32)
            + lax.dot_general(ah, bl, dn, preferred_element_type=F32)
            + lax.dot_general(al, bh, dn, preferred_element_type=F32))


def _dot_exact_lhs(a_bf16, b):
    bh, bm, bl = _split3(b)
    return (jnp.dot(a_bf16, bh, preferred_element_type=F32) + jnp.dot(a_bf16, bm, preferred_element_type=F32)
            + jnp.dot(a_bf16, bl, preferred_element_type=F32))


def _silu(x):
    return x * jax.nn.sigmoid(x)


def _softplus(x):
    return jnp.maximum(x, 0.0) + jnp.log1p(jnp.exp(-jnp.abs(x)))


def _layer_norm(t, g, b):
    mu = jnp.mean(t, axis=-1, keepdims=True)
    d = t - mu
    var = jnp.mean(d * d, axis=-1, keepdims=True)
    return d * lax.rsqrt(var + LN_EPS) * g + b


def _lane_pick(x, lane):
    ids = lax.broadcasted_iota(jnp.int32, x.shape, 1)
    return jnp.sum(jnp.where(ids == lane, x, 0.0), axis=1, keepdims=True)


def _in_proj_kernel(x_ref, wdn_ref, wba_ref, wmb_ref, cw_ref, cos_ref, sin_ref, dn_ref, z_ref, ba_ref, q_ref, k_ref,
                    vt_ref, sel_ref, cb_ref, km_ref, *, tiles_per_seq, nb, topk):
    i = pl.program_id(0)
    tm = x_ref.shape[0]
    o0 = 3 * D_DN
    seq_tile = i % tiles_per_seq
    spb = tm // MOBA_BLOCK

    @pl.when(seq_tile == 0)
    def _():
        cb_ref[0:8, :] = jnp.zeros((8, o0), F32)
        km_ref[...] = jnp.zeros_like(km_ref)

    half = HEAD_DIM // 2
    heads = [slice(h * HEAD_DIM, (h + 1) * HEAD_DIM) for h in range(N_HEADS_MOBA)]
    rows = [slice(sb * MOBA_BLOCK, (sb + 1) * MOBA_BLOCK) for sb in range(spb)]

    def rope(t, rs):
        return t * cos_ref[rs, :] + pltpu.roll(t, half, 1) * sin_ref[rs, :]

    xb = x_ref[...].astype(BF16)
    gw = IN_PROJ_GROUP
    qr = None
    for grp in range(3 * D_DN // gw):
        cs = slice(grp * gw, (grp + 1) * gw)
        u = jnp.dot(xb, wdn_ref[:, cs], preferred_element_type=F32)
        mb = jnp.dot(xb, wmb_ref[:, cs], preferred_element_type=F32)
        cb_ref[8:8 + tm, cs] = u
        acc = cw_ref[CONV_K - 1:CONV_K, cs] * u
        for s in range(1, CONV_K):
            acc = acc + cw_ref[CONV_K - 1 - s:CONV_K - s, cs] * cb_ref[8 - s:8 - s + tm, cs]
        cb_ref[0:8, cs] = u[tm - 8:tm, :]
        act = _silu(acc)
        if grp * gw < 2 * D_DN:
            outs = []
            for h in range(gw // HEAD_DIM):
                t = act[:, h * HEAD_DIM:(h + 1) * HEAD_DIM]
                t = t * lax.rsqrt(jnp.sum(t * t, axis=-1, keepdims=True) + L2_EPS)
                outs.append(t * (HEAD_DIM ** -0.5) if grp * gw < D_DN else t)
            act = jnp.concatenate(outs, axis=1)
        dn_ref[:, cs] = act

        if grp == 0:
            qr = [[rope(mb[rs, hs], rs) * (HEAD_DIM ** -0.5) for hs in heads] for rs in rows]
            for sb, rs in enumerate(rows):
                q_ref[rs, :] = jnp.concatenate([q * LOG2E for q in qr[sb]], axis=1).astype(q_ref.dtype)
        elif grp == 1:
            blk = lax.broadcasted_iota(jnp.int32, (nb, MOBA_BLOCK), 0)
            kmrow = lax.broadcasted_iota(jnp.int32, (nb, D_MOBA), 0)
            km = km_ref[...]
            for sb, rs in enumerate(rows):
                j = seq_tile * spb + sb
                kr = [rope(mb[rs, hs], rs) for hs in heads]
                k_ref[rs, :] = jnp.concatenate(kr, axis=1).astype(k_ref.dtype)
                for h, hs in enumerate(heads):
                    gate = _dot3_nt(km[:, hs], qr[sb][h])
                    gate = jnp.where(blk < j, gate, NEG_INF)
                    rank = jnp.zeros(gate.shape, F32)
                    for m in range(nb):
                        gm = gate[m:m + 1, :]
                        ahead = (gm > gate) | ((gm == gate) & (blk > m))
                        rank = rank + jnp.where(ahead, 1.0, 0.0)
                    sel = (blk < j) & (rank < topk)
                    sel_ref[sb, h * nb:(h + 1) * nb, :] = jnp.where(sel, 1.0, 0.0)
                kmean = jnp.concatenate([jnp.mean(t, axis=0, keepdims=True) for t in kr], axis=1)
                km = jnp.where(kmrow == j, kmean, km)
            km_ref[...] = km
        else:
            for sb, rs in enumerate(rows):
                vt_ref[sb] = jnp.transpose(mb[rs, :]).astype(vt_ref.dtype)
    z_ref[...] = _silu(jnp.dot(xb, wdn_ref[:, o0:o0 + D_DN], preferred_element_type=F32))
    ba_ref[...] = jnp.dot(xb, wba_ref[...], preferred_element_type=F32)


def _in_proj(x2, w_dn, w_ba, w_mb, conv_w, cos_t, sin_t, tm, seq):
    n, d = x2.shape
    const = lambda a: pl.BlockSpec(a.shape, lambda i: (0, 0))
    nb = seq // MOBA_BLOCK
    spb = tm // MOBA_BLOCK
    tps = seq // tm
    kern = functools.partial(_in_proj_kernel, tiles_per_seq=tps, nb=nb, topk=min(MOBA_TOPK, nb))
    tok = lambda width: pl.BlockSpec((tm, width), lambda i: (i, 0))
    tab = pl.BlockSpec((tm, HEAD_DIM), lambda i: (i % tps, 0))
    return pl.pallas_call(
        kern,
        out_shape=(jax.ShapeDtypeStruct((n, 3 * D_DN), F32), jax.ShapeDtypeStruct((n, D_DN), F32),
                   jax.ShapeDtypeStruct((n, LANES), F32),
                   jax.ShapeDtypeStruct((n, D_MOBA), BF16), jax.ShapeDtypeStruct((n, D_MOBA), BF16),
                   jax.ShapeDtypeStruct((n // MOBA_BLOCK, D_MOBA, MOBA_BLOCK), BF16),
                   jax.ShapeDtypeStruct((n // MOBA_BLOCK, N_HEADS_MOBA * nb, MOBA_BLOCK), F32)),
        grid=(n // tm,),
        in_specs=[tok(d), const(w_dn), const(w_ba), const(w_mb), const(conv_w), tab, tab],
        out_specs=(tok(3 * D_DN), tok(D_DN), tok(LANES), tok(D_MOBA), tok(D_MOBA),
                   pl.BlockSpec((spb, D_MOBA, MOBA_BLOCK), lambda i: (i, 0, 0)),
                   pl.BlockSpec((spb, N_HEADS_MOBA * nb, MOBA_BLOCK), lambda i: (i, 0, 0))),
        scratch_shapes=[pltpu.VMEM((8 + tm, 3 * D_DN), F32), pltpu.VMEM((nb, D_MOBA), F32)],
        compiler_params=pltpu.CompilerParams(dimension_semantics=("arbitrary",), vmem_limit_bytes=VMEM_LIMIT),
        name="in_proj",
    )(x2, w_dn, w_ba, w_mb, conv_w, cos_t, sin_t)


def _deltanet_kernel(q_ref, k_ref, v_ref, z_ref, ba_ref, alog_ref, dtb_ref, normw_ref, y_ref,
                     s_ref, wq_s, u_s, qk_s, kdt_s, egl_s, *, hb):
    hg = pl.program_id(1)
    t = pl.program_id(2)
    tt = DN_TILE
    nchunk = tt // DN_CHUNK
    nbat = q_ref.shape[0]
    hs = range(nbat * hb)

    @pl.when(t == 0)
    def _():
        for ref in (s_ref, wq_s, u_s, qk_s, kdt_s, egl_s):
            ref[...] = jnp.zeros_like(ref)

    rd = t & 1
    wr = 1 - rd
    state = [s_ref[h] for h in hs]
    outs = [[] for _ in hs]

    pend = {}

    def chain_a(c):
        pend["r"] = [jnp.dot(wq_s[rd, h, c], state[h].astype(BF16), preferred_element_type=F32) for h in hs]

    def chain_b(c):
        lo, hi = c * DN_CHUNK, (c + 1) * DN_CHUNK
        r = pend["r"]
        vz = []
        for h in hs:
            parts = []
            if lo > 0:
                parts.append(jnp.zeros((lo, HEAD_DIM), F32))
            parts.append(u_s[rd, h, lo:hi, :] - r[h][0:DN_CHUNK, :])
            if hi < tt:
                parts.append(jnp.zeros((tt - hi, HEAD_DIM), F32))
            vz.append(jnp.concatenate(parts, axis=0).astype(BF16))
        for h in hs:
            outs[h].append(r[h][DN_CHUNK:2 * DN_CHUNK, :]
                           + jnp.dot(qk_s[rd, h, lo:hi, :], vz[h], preferred_element_type=F32))
        for h in hs:
            state[h] = (state[h] * egl_s[rd, h, 8 * c:8 * c + 1, :]
                        + jnp.dot(kdt_s[rd, h], vz[h], preferred_element_type=F32))

    chain_a(0)

    bas = [ba_ref[bb] for bb in range(nbat)]
    beta_all = [jax.nn.sigmoid(ba) for ba in bas]
    g_all = [-jnp.exp(alog_ref[...]) * _softplus(ba + dtb_ref[...]) for ba in bas]

    row = lax.broadcasted_iota(jnp.int32, (tt, tt), 0)
    col = lax.broadcasted_iota(jnp.int32, (tt, tt), 1)
    same = (row >> 6) == (col >> 6)
    incl = same & (row >= col)
    strict = same & (row > col)

    incl_b = incl.astype(BF16)
    gc_all = [_dot_exact_lhs(incl_b, g) for g in g_all]
    gct = [jnp.transpose(g) for g in gc_all]
    sub = lax.broadcasted_iota(jnp.int32, gct[0].shape, 0)
    bat = [vh // hb for vh in hs]
    sls = [slice((vh % hb) * HEAD_DIM, (vh % hb + 1) * HEAD_DIM) for vh in hs]
    heads = [hg * hb + vh % hb for vh in hs]
    q = [q_ref[bat[vh], :, sls[vh]] for vh in hs]
    k = [k_ref[bat[vh], :, sls[vh]] for vh in hs]
    v = [v_ref[bat[vh], :, sls[vh]] for vh in hs]
    beta = [_lane_pick(beta_all[bat[vh]], heads[vh]) for vh in hs]
    gcc = [_lane_pick(gc_all[bat[vh]], heads[vh] + N_HEADS_DN) for vh in hs]
    gcr = [jnp.sum(jnp.where(sub == heads[vh] + N_HEADS_DN, gct[bat[vh]], 0.0), axis=0, keepdims=True)
           for vh in hs]
    chain_b(0)

    decay = [jnp.where(incl, jnp.exp(jnp.where(incl, gcc[h] - gcr[h], 0.0)), 0.0) for h in hs]
    kb = [k[h] * beta[h] for h in hs]
    vb = [v[h] * beta[h] for h in hs]
    a_mat = [jnp.where(strict, _dot_nt(kb[h], k[h]) * decay[h], 0.0) for h in hs]
    chain_a(1)
    qk = [_dot_nt(q[h], k[h]) * decay[h] for h in hs]
    eye = (row == col).astype(F32)
    d8 = (row >> 3) == (col >> 3)
    a8 = [jnp.where(d8, a, 0.0) for a in a_mat]
    chain_b(1)
    a8_2 = [_dot(a, a) for a in a8]
    chain_a(2)
    a8_4 = [_dot(a, a) for a in a8_2]
    chain_b(2)
    x = [_dot(eye - a, eye + a2) for a, a2 in zip(a8, a8_2)]
    chain_a(3)
    x = [_dot(xi, eye + a4) for xi, a4 in zip(x, a8_4)]
    chain_b(3)
    s = 8
    while s < DN_CHUNK:
        sh = s.bit_length() - 1
        off = ((row >> (sh + 1)) == (col >> (sh + 1))) & ((row >> sh) != (col >> sh))
        y = [_dot(jnp.where(off, a, 0.0), xi) for a, xi in zip(a_mat, x)]
        x = [xi - _dot(xi, yi) for xi, yi in zip(x, y)]
        s *= 2
    tinv = x
    eg = [jnp.exp(g) for g in gcc]
    wu = [_dot(tinv[h], jnp.concatenate([kb[h] * eg[h], vb[h]], axis=1)) for h in hs]
    qd = [q[h] * eg[h] for h in hs]
    gl_rows = [[g[(c + 1) * DN_CHUNK - 1:(c + 1) * DN_CHUNK, :] for c in range(nchunk)] for g in gcc]
    gl_col = [jnp.concatenate([jnp.broadcast_to(g, (DN_CHUNK, 1)) for g in rows], axis=0) for rows in gl_rows]
    kdt = [jnp.transpose(k[h] * jnp.exp(gl_col[h] - gcc[h])) for h in hs]

    ys = []
    for h in hs:
        o = jnp.concatenate(outs[h], axis=0)
        o = o * lax.rsqrt(jnp.mean(o * o, axis=-1, keepdims=True) + RMS_EPS) * normw_ref[...]
        ys.append(o * z_ref[bat[h], :, sls[h]])
    for bb in range(nbat):
        y_ref[bb] = jnp.concatenate(ys[bb * hb:(bb + 1) * hb], axis=1).astype(y_ref.dtype)
    s_ref[...] = jnp.stack(state, axis=0)

    for h in hs:
        for c in range(nchunk):
            lo, hi = c * DN_CHUNK, (c + 1) * DN_CHUNK
            wq_s[wr, h, c] = jnp.concatenate([wu[h][lo:hi, 0:HEAD_DIM], qd[h][lo:hi, :]], axis=0).astype(BF16)
            egl_s[wr, h, 8 * c:8 * c + 8, :] = jnp.broadcast_to(jnp.exp(gl_rows[h][c]), (8, HEAD_DIM))
        u_s[wr, h] = wu[h][:, HEAD_DIM:2 * HEAD_DIM]
        qk_s[wr, h] = qk[h].astype(BF16)
        kdt_s[wr, h] = kdt[h].astype(BF16)


def _deltanet(dn_qkv, z, ba, alog_row, dtb_row, normw_row):
    bsz, seq, _ = dn_qkv.shape
    tt = DN_TILE
    nt = seq // tt
    hb = DN_HEADS_PER_STEP
    nbat = DN_SEQS_PER_STEP if bsz % DN_SEQS_PER_STEP == 0 else 1
    nv = nbat * hb
    ng = N_HEADS_DN // hb
    w = hb * HEAD_DIM
    nchunk = tt // DN_CHUNK

    def cur_spec(off, width):
        return pl.BlockSpec((nbat, tt, width), lambda b, g, t: (b, jnp.minimum(t, nt - 1), g + off))

    prev_spec = pl.BlockSpec((nbat, tt, w), lambda b, g, t: (b, jnp.maximum(t - 1, 0), g))
    row_spec = pl.BlockSpec((1, LANES), lambda b, g, t: (0, 0))
    return pl.pallas_call(
        functools.partial(_deltanet_kernel, hb=hb),
        out_shape=jax.ShapeDtypeStruct((bsz, seq, D_DN), BF16),
        grid=(bsz // nbat, ng, nt + 1),
        in_specs=[cur_spec(0, w), cur_spec(ng, w), cur_spec(2 * ng, w), prev_spec,
                  pl.BlockSpec((nbat, tt, LANES), lambda b, g, t: (b, jnp.minimum(t, nt - 1), 0)),
                  row_spec, row_spec, row_spec],
        out_specs=prev_spec,
        scratch_shapes=[pltpu.VMEM((nv, HEAD_DIM, HEAD_DIM), F32),
                        pltpu.VMEM((2, nv, nchunk, 2 * DN_CHUNK, HEAD_DIM), BF16),
                        pltpu.VMEM((2, nv, tt, HEAD_DIM), F32),
                        pltpu.VMEM((2, nv, tt, tt), BF16),
                        pltpu.VMEM((2, nv, HEAD_DIM, tt), BF16),
                        pltpu.VMEM((2, nv, 8 * nchunk, HEAD_DIM), F32)],
        compiler_params=pltpu.CompilerParams(dimension_semantics=("parallel", "parallel", "arbitrary"),
                                             vmem_limit_bytes=VMEM_LIMIT),
        name="deltanet",
    )(dn_qkv, dn_qkv, dn_qkv, z, ba, alog_row, dtb_row, normw_row)


def _moba_attn_kernel(q_ref, k_ref, vt_ref, sel_ref, o_ref, acc_ref, *, nb):
    j = pl.program_id(1)
    blk = MOBA_BLOCK
    nh = N_HEADS_MOBA
    dn = (((1,), (1,)), ((), ()))
    hsl = [slice(h * HEAD_DIM, (h + 1) * HEAD_DIM) for h in range(nh)]
    qs = [q_ref[0, :, hsl[h]] for h in range(nh)]

    ki = lax.broadcasted_iota(jnp.int32, (blk, blk), 0)
    qi = lax.broadcasted_iota(jnp.int32, (blk, blk), 1)
    own = pl.ds(pl.multiple_of(j * blk, blk), blk)
    s_own = [jnp.where(ki <= qi, lax.dot_general(k_ref[0, own, hsl[h]], qs[h], dn, preferred_element_type=F32),
                       NEG_INF) for h in range(nh)]

    def scores(n, h):
        kn = k_ref[0, pl.ds(pl.multiple_of(n * blk, blk), blk), hsl[h]]
        s = lax.dot_general(kn, qs[h], dn, preferred_element_type=F32)
        return jnp.where(sel_ref[0, 0, pl.ds(h * nb + n, 1), :] > 0.5, s, NEG_INF)

    ones8 = jnp.ones((8, blk), BF16)

    def softmax_pv(n, h, s, m, l):
        m_new = jnp.maximum(m, jnp.max(s, axis=0, keepdims=True))
        pb = jnp.exp2(s - m_new).astype(BF16)
        psum = jnp.dot(ones8, pb, preferred_element_type=F32)[0:1, :]
        pv = jnp.dot(vt_ref[0, n, hsl[h], :], pb, preferred_element_type=F32)
        return m_new, psum, pv

    ms, ls = [], []
    for h in range(nh):
        m, psum, pv = softmax_pv(j, h, s_own[h], jnp.full((1, blk), NEG_INF, F32), None)
        ms.append(m)
        ls.append(psum)
        acc_ref[h] = pv

    def body(step, carry):
        ms, ls = (list(t) for t in carry)
        n0 = MOBA_UNROLL * step
        cur = [scores(n0, h) for h in range(nh)]
        for i in range(MOBA_UNROLL):
            nxt = []
            for h in range(nh):
                if i + 1 < MOBA_UNROLL:
                    nxt.append(scores(n0 + i + 1, h))
                m_new, psum, pv = softmax_pv(n0 + i, h, cur[h], ms[h], ls[h])
                alpha = jnp.exp2(ms[h] - m_new)
                ls[h] = alpha * ls[h] + psum
                ms[h] = m_new
                acc_ref[h] = acc_ref[h] * alpha + pv
            cur = nxt
        return tuple(ms), tuple(ls)

    ms, ls = lax.fori_loop(0, (j + MOBA_UNROLL - 1) // MOBA_UNROLL, body, (tuple(ms), tuple(ls)))
    o_ref[0] = jnp.concatenate([jnp.transpose(acc_ref[h] / ls[h]) for h in range(nh)],
                               axis=1).astype(o_ref.dtype)


def _moba_attn(q_r, k_r, vt, sel):
    bsz, seq, _ = q_r.shape
    nb = seq // MOBA_BLOCK
    assert nb % MOBA_UNROLL == 0, (nb, MOBA_UNROLL)
    tok_spec = pl.BlockSpec((1, MOBA_BLOCK, D_MOBA), lambda b, j: (b, j, 0))
    return pl.pallas_call(
        functools.partial(_moba_attn_kernel, nb=nb),
        out_shape=jax.ShapeDtypeStruct((bsz, seq, D_MOBA), BF16),
        grid=(bsz, nb),
        in_specs=[tok_spec,
                  pl.BlockSpec((1, seq, D_MOBA), lambda b, j: (b, 0, 0)),
                  pl.BlockSpec((1, nb, D_MOBA, MOBA_BLOCK), lambda b, j: (b, 0, 0, 0)),
                  pl.BlockSpec((1, 1, N_HEADS_MOBA * nb, MOBA_BLOCK), lambda b, j: (b, j, 0, 0))],
        out_specs=tok_spec,
        scratch_shapes=[pltpu.VMEM((N_HEADS_MOBA, HEAD_DIM, MOBA_BLOCK), F32)],
        compiler_params=pltpu.CompilerParams(dimension_semantics=("parallel", "arbitrary"),
                                             vmem_limit_bytes=VMEM_LIMIT),
        name="moba_attn",
    )(q_r, k_r, vt, sel)


def _route_record(logits):
    lane = lax.broadcasted_iota(jnp.int32, logits.shape, 1)
    big = jnp.int32(LANES)

    def first_lane(mask):
        return jnp.min(jnp.where(mask, lane, big), axis=1, keepdims=True)

    is_g = lane < N_GROUPS
    m1 = jnp.max(jnp.where(is_g, logits, NEG_INF), axis=1, keepdims=True)
    s1 = jnp.sum(jnp.where(is_g, jnp.exp(logits - m1), 0.0), axis=1, keepdims=True)
    pg = 1.0 / s1
    gsel = first_lane(is_g & (logits == m1))

    in_grp = (lane >= GATE_LANE0) & (((lane - GATE_LANE0) >> 2) == gsel) & (lane < GATE_LANE0 + N_EXPERTS)
    m2 = jnp.max(jnp.where(in_grp, logits, NEG_INF), axis=1, keepdims=True)
    s2 = jnp.sum(jnp.where(in_grp, jnp.exp(logits - m2), 0.0), axis=1, keepdims=True)
    e1 = first_lane(in_grp & (logits == m2))
    rest = in_grp & (lane != e1)
    m2b = jnp.max(jnp.where(rest, logits, NEG_INF), axis=1, keepdims=True)
    e2 = first_lane(rest & (logits == m2b))
    pe1 = 1.0 / s2
    pe2 = jnp.exp(m2b - m2) / s2
    tot = pe1 + pe2
    w1 = pg * (pe1 / tot)
    w2 = pg * (pe2 / tot)
    first_lo = e1 < e2
    lo = jnp.minimum(e1, e2)
    hi = jnp.maximum(e1, e2)
    a = (lo - GATE_LANE0) & (EXPERTS_PER_GROUP - 1)
    b = (hi - GATE_LANE0) & (EXPERTS_PER_GROUP - 1)
    bucket = gsel * N_PAIRS + ((a * (2 * EXPERTS_PER_GROUP - 1 - a)) >> 1) + (b - a - 1)
    record = jnp.where(lane == ROUTE_BUCKET, bucket.astype(F32),
                       jnp.where(lane == ROUTE_W_LO, jnp.where(first_lo, w1, w2),
                                 jnp.where(lane == ROUTE_W_HI, jnp.where(first_lo, w2, w1), 0.0)))
    return record, jnp.sum(jnp.where(lane == bucket, 1.0, 0.0), axis=0, keepdims=True)


def _mix_route_kernel(ydn_ref, ymb_ref, x_ref, wo_ref, g_ref, b_ref, rw_ref, rb_ref, h_ref, hb_ref, route_ref,
                      cnt_ref):
    nsub = cnt_ref.shape[0]
    rows = [slice(i * MOE_TS, (i + 1) * MOE_TS) for i in range(nsub)]
    wh, wl = _split2(rw_ref[...])
    wcat = jnp.concatenate([wh, wl], axis=1)

    mixes = [jnp.dot(jnp.concatenate([ydn_ref[r, :], ymb_ref[r, :]], axis=1), wo_ref[...],
                     preferred_element_type=F32) for r in rows]
    logits = []
    for r, mix in zip(rows, mixes):
        hval = _layer_norm(DEEPNORM_ALPHA * x_ref[r, :] + mix, g_ref[...], b_ref[...])
        h_ref[r, :] = hval
        hb_ref[r, :] = hval.astype(BF16)
        hh, hl = _split2(hval)
        both = jnp.dot(hh, wcat, preferred_element_type=F32)
        logits.append(both[:, 0:LANES] + both[:, LANES:2 * LANES]
                      + jnp.dot(hl, wh, preferred_element_type=F32) + rb_ref[...])
    for i, r in enumerate(rows):
        record, counts = _route_record(logits[i])
        route_ref[r, :] = record
        cnt_ref[i] = counts


def _mix_route(y_dn, y_mb, x2, wo, g1, b1, rw, rb, tm):
    n, d = x2.shape
    nsub = tm // MOE_TS
    row = lambda w: pl.BlockSpec((1, w), lambda i: (0, 0))
    return pl.pallas_call(
        _mix_route_kernel,
        out_shape=(jax.ShapeDtypeStruct((n, d), F32), jax.ShapeDtypeStruct((n, d), BF16),
                   jax.ShapeDtypeStruct((n, LANES), F32), jax.ShapeDtypeStruct((n // MOE_TS, 1, LANES), F32)),
        grid=(n // tm,),
        in_specs=[pl.BlockSpec((tm, D_DN), lambda i: (i, 0)), pl.BlockSpec((tm, D_MOBA), lambda i: (i, 0)),
                  pl.BlockSpec((tm, d), lambda i: (i, 0)), pl.BlockSpec((D_DN + D_MOBA, d), lambda i: (0, 0)),
                  row(d), row(d), pl.BlockSpec((d, LANES), lambda i: (0, 0)), row(LANES)],
        out_specs=(pl.BlockSpec((tm, d), lambda i: (i, 0)), pl.BlockSpec((tm, d), lambda i: (i, 0)),
                   pl.BlockSpec((tm, LANES), lambda i: (i, 0)), pl.BlockSpec((nsub, 1, LANES), lambda i: (i, 0, 0))),
        compiler_params=pltpu.CompilerParams(dimension_semantics=("parallel",), vmem_limit_bytes=VMEM_LIMIT),
        name="mix_route",
    )(y_dn, y_mb, x2, wo, g1, b1, rw, rb)


def _bucket_offsets_col(ohf):
    cnt = jnp.sum(ohf, axis=1, keepdims=True).astype(jnp.int32)
    pad = (((cnt + (GRAN - 1)) >> GRAN_SHIFT) << GRAN_SHIFT).astype(F32)
    r = lax.broadcasted_iota(jnp.int32, (LANES, LANES), 0)
    c = lax.broadcasted_iota(jnp.int32, (LANES, LANES), 1)
    before = jnp.where(c < r, 1.0, 0.0)
    return _dot(before, jnp.broadcast_to(pad, (LANES, LANES)))[:, 0:1]


def _moe_sort_kernel(gmap_ref, nvalid_ref, tail0_ref, taillen_ref, hb_ref, route_ref, lstrict_ref,
                     xg_ref, wsg_ref, xs_ref, ws_ref, zx_ref, zw_ref, sem):
    s = pl.program_id(0)
    nsteps = pl.num_programs(0)
    slot = s & 1
    ts = route_ref.shape[0]
    route = route_ref[...]
    rt = jnp.transpose(route)
    bucket_row = rt[ROUTE_BUCKET:ROUTE_BUCKET + 1, :].astype(jnp.int32)
    sub = lax.broadcasted_iota(jnp.int32, (LANES, ts), 0)
    ohf = jnp.where(sub == bucket_row, 1.0, 0.0)
    loff = _bucket_offsets_col(ohf)
    rank = lax.dot_general(ohf.astype(BF16), lstrict_ref[...], (((1,), (1,)), ((), ())),
                           preferred_element_type=F32)
    dest = jnp.sum(ohf * (loff + rank), axis=0, keepdims=True).astype(jnp.int32)
    rh, rl = _split2(route)
    wcat = jnp.concatenate([rh, rl], axis=1)
    rc = LROWS // PERM_CHUNKS
    for c in range(PERM_CHUNKS):
        rowi = lax.broadcasted_iota(jnp.int32, (rc, ts), 0) + c * rc
        perm = jnp.where(rowi == dest, 1.0, 0.0).astype(BF16)
        xs_ref[slot, c * rc:(c + 1) * rc, :] = jnp.dot(perm, hb_ref[...], preferred_element_type=F32).astype(BF16)
        wparts = jnp.dot(perm, wcat, preferred_element_type=F32)
        ws_ref[slot, c * rc:(c + 1) * rc, :] = wparts[:, 0:LANES] + wparts[:, LANES:2 * LANES]

    def copies(step, g):
        sl = step & 1
        src = pl.ds(pl.multiple_of(g * GRAN, GRAN), GRAN)
        dst = pl.ds(pl.multiple_of(gmap_ref[step * LGRAN + g] * GRAN, GRAN), GRAN)
        return (pltpu.make_async_copy(xs_ref.at[sl, src, :], xg_ref.at[dst, :], sem.at[0, sl]),
                pltpu.make_async_copy(ws_ref.at[sl, src, :], wsg_ref.at[dst, :], sem.at[1, sl]))

    def fill_copies(b, i):
        dst = pl.ds(pl.multiple_of((tail0_ref[b] + i) * GRAN, GRAN), GRAN)
        return (pltpu.make_async_copy(zx_ref.at[0:GRAN, :], xg_ref.at[dst, :], sem.at[2, 0]),
                pltpu.make_async_copy(zw_ref.at[0:GRAN, :], wsg_ref.at[dst, :], sem.at[2, 1]))

    def unused_tile_copies(t):
        dst = pl.ds(pl.multiple_of(t * MOE_TM, MOE_TM), MOE_TM)
        return (pltpu.make_async_copy(zx_ref, xg_ref.at[dst, :], sem.at[2, 0]),
                pltpu.make_async_copy(zw_ref, wsg_ref.at[dst, :], sem.at[2, 1]))

    def run(step, fn):
        def body(g, carry):
            for cp in copies(step, g):
                fn(cp)
            return carry
        lax.fori_loop(0, nvalid_ref[step], body, 0)

    def run_fill(fn):
        for b in range(N_BUCKETS):
            def body(i, carry, b=b):
                for cp in fill_copies(b, i):
                    fn(cp)
                return carry
            lax.fori_loop(0, taillen_ref[b], body, 0)

        def tile_body(t, carry):
            for cp in unused_tile_copies(t):
                fn(cp)
            return carry
        lax.fori_loop(tail0_ref[N_BUCKETS], xg_ref.shape[0] // MOE_TM, tile_body, 0)

    @pl.when(s == 0)
    def _():
        zx_ref[...] = jnp.zeros_like(zx_ref)
        zw_ref[...] = jnp.zeros_like(zw_ref)
        run_fill(lambda cp: cp.start())

    run(s, lambda cp: cp.start())

    @pl.when(s > 0)
    def _():
        run(s - 1, lambda cp: cp.wait())

    @pl.when(s == nsteps - 1)
    def _():
        run(s, lambda cp: cp.wait())
        run_fill(lambda cp: cp.wait())


def _moe_sort(plan, hb, route, lstrict):
    n, d = hb.shape
    ts = MOE_TS
    rows = plan["n_tiles"] * MOE_TM
    return pl.pallas_call(
        _moe_sort_kernel,
        out_shape=(jax.ShapeDtypeStruct((rows, d), BF16), jax.ShapeDtypeStruct((rows, LANES), F32)),
        grid_spec=pltpu.PrefetchScalarGridSpec(
            num_scalar_prefetch=4,
            grid=(n // ts,),
            in_specs=[pl.BlockSpec((ts, d), lambda s, *_: (s, 0)), pl.BlockSpec((ts, LANES), lambda s, *_: (s, 0)),
                      pl.BlockSpec((ts, ts), lambda s, *_: (0, 0))],
            out_specs=(pl.BlockSpec(memory_space=pl.ANY), pl.BlockSpec(memory_space=pl.ANY)),
            scratch_shapes=[pltpu.VMEM((2, LROWS, d), BF16), pltpu.VMEM((2, LROWS, LANES), F32),
                            pltpu.VMEM((MOE_TM, d), BF16), pltpu.VMEM((MOE_TM, LANES), F32),
                            pltpu.SemaphoreType.DMA((3, 2))]),
        compiler_params=pltpu.CompilerParams(dimension_semantics=("arbitrary",), vmem_limit_bytes=VMEM_LIMIT),
        name="moe_sort",
    )(plan["gmap"], plan["nvalid"], plan["tail0"], plan["taillen"], hb, route, lstrict)


def _moe_expert_kernel(xt_ref, elo_ref, ehi_ref, valid_ref, x_ref, w_ref, wg0, wu0, wd0, wg1, wu1, wd1, o_ref):
    t = pl.program_id(0)

    @pl.when(valid_ref[t] > 0)
    def _():
        x = x_ref[...]
        w = w_ref[...]
        gates = [jnp.dot(x, wg[0].astype(BF16), preferred_element_type=F32) for wg in (wg0, wg1)]
        ups = [jnp.dot(x, wu[0].astype(BF16), preferred_element_type=F32) for wu in (wu0, wu1)]
        hes = [(_silu(gates[i]) * ups[i] * w[:, lane:lane + 1]).astype(BF16)
               for i, lane in enumerate((ROUTE_W_LO, ROUTE_W_HI))]
        o_ref[...] = (jnp.dot(hes[0], wd0[0].astype(BF16), preferred_element_type=F32)
                      + jnp.dot(hes[1], wd1[0].astype(BF16), preferred_element_type=F32)).astype(o_ref.dtype)

    @pl.when(valid_ref[t] == 0)
    def _():
        o_ref[...] = jnp.zeros_like(o_ref)


def _moe_experts(plan, xg, wsg, wg, wu, wd):
    rows, d = xg.shape
    tm = MOE_TM
    tok = lambda width: pl.BlockSpec((tm, width), lambda t, xt, elo, ehi, valid: (xt[t], 0))
    lo3 = lambda shape: pl.BlockSpec(shape, lambda t, xt, elo, ehi, valid: (elo[t], 0, 0))
    hi3 = lambda shape: pl.BlockSpec(shape, lambda t, xt, elo, ehi, valid: (ehi[t], 0, 0))
    return pl.pallas_call(
        _moe_expert_kernel,
        out_shape=jax.ShapeDtypeStruct((rows, d), BF16),
        grid_spec=pltpu.PrefetchScalarGridSpec(
            num_scalar_prefetch=4,
            grid=(rows // tm,),
            in_specs=[tok(d), tok(LANES),
                      lo3((1, d, D_EXPERT)), lo3((1, d, D_EXPERT)), lo3((1, D_EXPERT, d)),
                      hi3((1, d, D_EXPERT)), hi3((1, d, D_EXPERT)), hi3((1, D_EXPERT, d))],
            out_specs=pl.BlockSpec((tm, d), lambda t, *_: (t, 0))),
        compiler_params=pltpu.CompilerParams(dimension_semantics=("arbitrary",), vmem_limit_bytes=VMEM_LIMIT),
        name="moe_experts",
    )(plan["xtile"], plan["elo"], plan["ehi"], plan["valid"], xg, wsg, wg, wu, wd, wg, wu, wd)


def _moe_unsort_kernel(gmap_ref, og_ref, route_ref, h_ref, lstrict_ref, g_ref, b_ref, out_ref, ol_ref, sem):
    s = pl.program_id(0)
    nsteps = pl.num_programs(0)
    slot = s & 1
    ts = route_ref.shape[0]

    def gather(step, fn):
        sl = step & 1

        def body(g, carry):
            src = pl.ds(pl.multiple_of(gmap_ref[step * LGRAN + g] * GRAN, GRAN), GRAN)
            dst = pl.ds(pl.multiple_of(g * GRAN, GRAN), GRAN)
            fn(pltpu.make_async_copy(og_ref.at[src, :], ol_ref.at[sl, dst, :], sem.at[sl]))
            return carry
        lax.fori_loop(0, LGRAN, body, 0)

    @pl.when(s == 0)
    def _():
        gather(s, lambda cp: cp.start())

    @pl.when(s + 1 < nsteps)
    def _():
        gather(s + 1, lambda cp: cp.start())

    route = route_ref[...]
    bucket_col = route[:, ROUTE_BUCKET:ROUTE_BUCKET + 1].astype(jnp.int32)
    lane = lax.broadcasted_iota(jnp.int32, (ts, LANES), 1)
    ohf = jnp.where(lane == bucket_col, 1.0, 0.0)
    cnt = jnp.sum(ohf, axis=0, keepdims=True).astype(jnp.int32)
    pad = (((cnt + (GRAN - 1)) >> GRAN_SHIFT) << GRAN_SHIFT).astype(F32)
    r = lax.broadcasted_iota(jnp.int32, (LANES, LANES), 0)
    c = lax.broadcasted_iota(jnp.int32, (LANES, LANES), 1)
    loff = _dot(jnp.broadcast_to(pad, (8, LANES)), jnp.where(r < c, 1.0, 0.0))[0:1, :]
    rank = jnp.dot(lstrict_ref[...], ohf.astype(BF16), preferred_element_type=F32)
    dest = jnp.sum(ohf * (loff + rank), axis=1, keepdims=True).astype(jnp.int32)
    gather(s, lambda cp: cp.wait())
    tc = ts // PERM_CHUNKS
    lrow = lax.broadcasted_iota(jnp.int32, (tc, LROWS), 1)
    for c in range(PERM_CHUNKS):
        rows = slice(c * tc, (c + 1) * tc)
        perm_t = jnp.where(lrow == dest[rows, :], 1.0, 0.0).astype(BF16)
        ffn = jnp.dot(perm_t, ol_ref[slot], preferred_element_type=F32)
        out_ref[rows, :] = _layer_norm(DEEPNORM_ALPHA * h_ref[rows, :] + ffn, g_ref[...], b_ref[...])


def _moe_unsort(plan, og, route, hf, lstrict, g2, b2):
    n, d = hf.shape
    ts = MOE_TS
    row = pl.BlockSpec((1, d), lambda s, *_: (0, 0))
    return pl.pallas_call(
        _moe_unsort_kernel,
        out_shape=jax.ShapeDtypeStruct((n, d), F32),
        grid_spec=pltpu.PrefetchScalarGridSpec(
            num_scalar_prefetch=1,
            grid=(n // ts,),
            in_specs=[pl.BlockSpec(memory_space=pl.ANY), pl.BlockSpec((ts, LANES), lambda s, *_: (s, 0)),
                      pl.BlockSpec((ts, d), lambda s, *_: (s, 0)), pl.BlockSpec((ts, ts), lambda s, *_: (0, 0)),
                      row, row],
            out_specs=pl.BlockSpec((ts, d), lambda s, *_: (s, 0)),
            scratch_shapes=[pltpu.VMEM((2, LROWS, d), BF16), pltpu.SemaphoreType.DMA((2,))]),
        compiler_params=pltpu.CompilerParams(dimension_semantics=("arbitrary",), vmem_limit_bytes=VMEM_LIMIT),
        name="moe_unsort",
    )(plan["gmap_back"], og, route, hf, lstrict, g2, b2)


def _moe_plan(cnt_half, n):
    nsrc = n // MOE_TS
    i32 = jnp.int32
    cnt = cnt_half.reshape(nsrc, -1, LANES).sum(axis=1)[:, :N_BUCKETS].astype(i32)
    run_g = (cnt + GRAN - 1) // GRAN
    nvalid = run_g.sum(axis=1)
    loff_g = jnp.cumsum(run_g, axis=1) - run_g
    bucket_g = run_g.sum(axis=0)
    gpt = MOE_TM // GRAN
    btiles = (bucket_g + gpt - 1) // gpt
    tend = jnp.cumsum(btiles)
    tstart = tend - btiles
    gofs = tstart[None, :] * gpt + jnp.cumsum(run_g, axis=0) - run_g
    n_tiles = -(-(n + nsrc * N_BUCKETS * (GRAN - 1)) // MOE_TM) + N_BUCKETS + 1
    g = jnp.arange(LGRAN, dtype=i32)[None, :, None]
    in_run = (g >= loff_g[:, None, :]) & (g < (loff_g + run_g)[:, None, :])
    gmap = jnp.arange(LGRAN, dtype=i32)[None, :] + jnp.sum(jnp.where(in_run, (gofs - loff_g)[:, None, :], 0), axis=2)
    is_valid = jnp.arange(LGRAN, dtype=i32)[None, :] < nvalid[:, None]
    zero_gran = (n_tiles - 1) * gpt
    t = jnp.arange(n_tiles, dtype=i32)
    tb = jnp.minimum(jnp.sum(t[:, None] >= tend[None, :], axis=1), N_BUCKETS - 1)
    valid = (t < tend[-1]).astype(i32)
    pairs = [(a, b) for a in range(EXPERTS_PER_GROUP) for b in range(a + 1, EXPERTS_PER_GROUP)]
    pidx = tb % N_PAIRS
    pair_a = sum(jnp.where(pidx == i, a, 0) for i, (a, _) in enumerate(pairs))
    pair_b = sum(jnp.where(pidx == i, b, 0) for i, (_, b) in enumerate(pairs))
    grp = tb // N_PAIRS
    return {
        "n_tiles": n_tiles,
        "gmap": jnp.where(is_valid, gmap, 0).reshape(-1).astype(i32),
        "gmap_back": jnp.where(is_valid, gmap, zero_gran).reshape(-1).astype(i32),
        "nvalid": nvalid.astype(i32),
        "tail0": jnp.concatenate([tstart * gpt + bucket_g, tend[-1:]]).astype(i32),
        "taillen": (btiles * gpt - bucket_g).astype(i32),
        "xtile": jnp.where(valid > 0, t, 0).astype(i32),
        "elo": (grp * EXPERTS_PER_GROUP + pair_a).astype(i32),
        "ehi": (grp * EXPERTS_PER_GROUP + pair_b).astype(i32),
        "valid": valid,
    }


def _pad_lanes(a, lane0=0):
    return jnp.zeros((1, LANES), F32).at[0, lane0:lane0 + a.shape[0]].set(a.astype(F32))


def _rope_tables(seq):
    half = HEAD_DIM // 2
    inv_freq = ROPE_THETA ** (-jnp.arange(half, dtype=F32) / half)
    ang = jnp.arange(seq).astype(F32)[:, None] * inv_freq[None, :]
    cos, sin = jnp.cos(ang), jnp.sin(ang)
    return jnp.concatenate([cos, cos], axis=-1), jnp.concatenate([-sin, sin], axis=-1)


def _layer(x, w_in, conv_w, a_log, dt_bias, dn_norm_w, w_out, ln1_g, ln1_b, router_w1, router_b1,
           router_w2, router_b2, w_gate, w_up, w_down, ln2_g, ln2_b):
    bsz, seq, d = x.shape
    n = bsz * seq
    x2 = x.reshape(n, d)

    o_b, o_mb = 4 * D_DN, 4 * D_DN + 2 * N_HEADS_DN
    w_dn = w_in[:, :o_b].astype(BF16)
    w_ba = jnp.pad(w_in[:, o_b:o_mb], ((0, 0), (0, LANES - 2 * N_HEADS_DN))).astype(BF16)
    w_mb = w_in[:, o_mb:].astype(BF16)

    cos_t, sin_t = _rope_tables(seq)
    nb = seq // MOBA_BLOCK
    dn_qkv, z, ba, q_r, k_r, vt, sel = _in_proj(x2, w_dn, w_ba, w_mb, conv_w, cos_t, sin_t,
                                                min(IN_PROJ_TM, seq), seq)

    y_dn = _deltanet(dn_qkv.reshape(bsz, seq, 3 * D_DN), z.reshape(bsz, seq, D_DN), ba.reshape(bsz, seq, LANES),
                     _pad_lanes(a_log, N_HEADS_DN), _pad_lanes(dt_bias, N_HEADS_DN),
                     dn_norm_w.astype(F32).reshape(1, HEAD_DIM))

    q_r, k_r = q_r.reshape(bsz, seq, D_MOBA), k_r.reshape(bsz, seq, D_MOBA)
    vt = vt.reshape(bsz, nb, D_MOBA, MOBA_BLOCK)
    sel = sel.reshape(bsz, nb, N_HEADS_MOBA * nb, MOBA_BLOCK)
    y_mb = _moba_attn(q_r, k_r, vt, sel)

    rw = jnp.concatenate([router_w1, jnp.transpose(router_w2, (1, 0, 2)).reshape(d, N_EXPERTS)], axis=1)
    rw = jnp.pad(rw, ((0, 0), (0, LANES - rw.shape[1])))
    rb = _pad_lanes(jnp.concatenate([router_b1, router_b2.reshape(-1)]))
    hf, hb, route, cnt = _mix_route(y_dn.reshape(n, D_DN), y_mb.reshape(n, D_MOBA), x2, w_out.astype(BF16),
                                    ln1_g.reshape(1, d), ln1_b.reshape(1, d), rw, rb, min(2 * MOE_TS, n))

    plan = _moe_plan(cnt, n)
    idx = jnp.arange(MOE_TS, dtype=jnp.int32)
    lstrict = (idx[None, :] < idx[:, None]).astype(BF16)
    xg, wsg = _moe_sort(plan, hb, route, lstrict)
    og = _moe_experts(plan, xg, wsg, w_gate, w_up, w_down)
    out = _moe_unsort(plan, og, route, hf, lstrict, ln2_g.reshape(1, d), ln2_b.reshape(1, d))
    return out.reshape(bsz, seq, d)


def kernel(x, w_in, conv_w, a_log, dt_bias, dn_norm_w, w_out, ln1_g, ln1_b, router_w1, router_b1, router_w2, router_b2, expert_w_gate, expert_w_up, expert_w_down, ln2_g, ln2_b):
    for l in range(DEPTH):
        x = _layer(x, w_in[l], conv_w[l], a_log[l], dt_bias[l], dn_norm_w[l], w_out[l], ln1_g[l], ln1_b[l],
                   router_w1[l], router_b1[l], router_w2[l], router_b2[l], expert_w_gate[l], expert_w_up[l],
                   expert_w_down[l], ln2_g[l], ln2_b[l])
    return x
```

```python
import functools

import jax
import jax.numpy as jnp
from jax import lax
from jax.experimental import pallas as pl
from jax.experimental.pallas import tpu as pltpu

F32 = jnp.float32
BF16 = jnp.bfloat16

HEAD_DIM = 128
N_HEADS_DN = 4
N_HEADS_MOBA = 4
D_DN = N_HEADS_DN * HEAD_DIM
D_MOBA = N_HEADS_MOBA * HEAD_DIM
CONV_K = 4
DN_CHUNK = 64
MOBA_BLOCK = 256
MOBA_TOPK = 3
MOBA_UNROLL = 4
ROPE_THETA = 10000.0
N_GROUPS = 4
EXPERTS_PER_GROUP = 4
N_EXPERTS = N_GROUPS * EXPERTS_PER_GROUP
D_EXPERT = 256
LN_EPS = 1e-5
RMS_EPS = 1e-6
L2_EPS = 1e-6
NEG_INF = -1e30
LOG2E = 1.4426950408889634
DEPTH = 1
DEEPNORM_ALPHA = (2 * DEPTH) ** 0.25

LANES = 128
IN_PROJ_TM = 512
IN_PROJ_GROUP = 512
DN_TILE = 256
DN_HEADS_PER_STEP = 4
DN_SEQS_PER_STEP = 2
GATE_LANE0 = N_GROUPS
N_PAIRS = EXPERTS_PER_GROUP * (EXPERTS_PER_GROUP - 1) // 2
N_BUCKETS = N_GROUPS * N_PAIRS
ROUTE_BUCKET, ROUTE_W_LO, ROUTE_W_HI = 0, 1, 2
MOE_TS = 512
MOE_TM = 512
GRAN = 16
GRAN_SHIFT = 4
LROWS = -(-(MOE_TS + N_BUCKETS * (GRAN - 1)) // LANES) * LANES
LGRAN = LROWS // GRAN
PERM_CHUNKS = 4
VMEM_LIMIT = 48 * 1024 * 1024


def _dot(a, b):
    return jnp.dot(a.astype(BF16), b.astype(BF16), preferred_element_type=F32)


def _dot_nt(a, b):
    return lax.dot_general(a.astype(BF16), b.astype(BF16), (((1,), (1,)), ((), ())),
                           preferred_element_type=F32)


def _split2(a):
    hi = a.astype(BF16)
    lo = (a - hi.astype(F32)).astype(BF16)
    return hi, lo


def _split3(a):
    hi = a.astype(BF16)
    r = a - hi.astype(F32)
    mid = r.astype(BF16)
    lo = (r - mid.astype(F32)).astype(BF16)
    return hi, mid, lo


def _dot3(a, b):
    ah, al = _split2(a)
    bh, bl = _split2(b)
    return (jnp.dot(ah, bh, preferred_element_type=F32) + jnp.dot(ah, bl, preferred_element_type=F32)
            + jnp.dot(al, bh, preferred_element_type=F32))


def _dot3_nt(a, b):
    ah, al = _split2(a)
    bh, bl = _split2(b)
    dn = (((1,), (1,)), ((), ()))
    return (lax.dot_general(ah, bh, dn, preferred_element_type=F32)
            + lax.dot_general(ah, bl, dn, preferred_element_type=F32)
            + lax.dot_general(al, bh, dn, preferred_element_type=F32))


def _dot_exact_lhs(a_bf16, b):
    bh, bm, bl = _split3(b)
    return (jnp.dot(a_bf16, bh, preferred_element_type=F32) + jnp.dot(a_bf16, bm, preferred_element_type=F32)
            + jnp.dot(a_bf16, bl, preferred_element_type=F32))


def _silu(x):
    return x * jax.nn.sigmoid(x)


def _softplus(x):
    return jnp.maximum(x, 0.0) + jnp.log1p(jnp.exp(-jnp.abs(x)))


def _layer_norm(t, g, b):
    mu = jnp.mean(t, axis=-1, keepdims=True)
    d = t - mu
    var = jnp.mean(d * d, axis=-1, keepdims=True)
    return d * lax.rsqrt(var + LN_EPS) * g + b


def _lane_pick(x, lane):
    ids = lax.broadcasted_iota(jnp.int32, x.shape, 1)
    return jnp.sum(jnp.where(ids == lane, x, 0.0), axis=1, keepdims=True)


def _in_proj_kernel(x_ref, w_ref, cw_ref, cos_ref, sin_ref, dn_ref, z_ref, ba_ref, q_ref, k_ref, vt_ref, sel_ref,
                    cb_ref, km_ref, *, tiles_per_seq, nb, topk):
    i = pl.program_id(0)
    tm = x_ref.shape[0]
    o0 = 3 * D_DN
    o1 = o0 + D_DN
    o2 = o1 + LANES
    seq_tile = i % tiles_per_seq
    spb = tm // MOBA_BLOCK

    @pl.when(seq_tile == 0)
    def _():
        cb_ref[0:8, :] = jnp.zeros((8, o0), F32)
        km_ref[...] = jnp.zeros_like(km_ref)

    half = HEAD_DIM // 2
    heads = [slice(h * HEAD_DIM, (h + 1) * HEAD_DIM) for h in range(N_HEADS_MOBA)]
    rows = [slice(sb * MOBA_BLOCK, (sb + 1) * MOBA_BLOCK) for sb in range(spb)]

    def rope(t, rs):
        return t * cos_ref[rs, :] + pltpu.roll(t, half, 1) * sin_ref[rs, :]

    xb = x_ref[...].astype(BF16)
    gw = IN_PROJ_GROUP
    qr = None
    for grp in range(3 * D_DN // gw):
        cs = slice(grp * gw, (grp + 1) * gw)
        u = jnp.dot(xb, w_ref[:, cs], preferred_element_type=F32)
        mb = jnp.dot(xb, w_ref[:, o2 + grp * gw:o2 + (grp + 1) * gw], preferred_element_type=F32)
        cb_ref[8:8 + tm, cs] = u
        acc = cw_ref[CONV_K - 1:CONV_K, cs] * u
        for s in range(1, CONV_K):
            acc = acc + cw_ref[CONV_K - 1 - s:CONV_K - s, cs] * cb_ref[8 - s:8 - s + tm, cs]
        cb_ref[0:8, cs] = u[tm - 8:tm, :]
        act = _silu(acc)
        if grp * gw < 2 * D_DN:
            outs = []
            for h in range(gw // HEAD_DIM):
                t = act[:, h * HEAD_DIM:(h + 1) * HEAD_DIM]
                t = t * lax.rsqrt(jnp.sum(t * t, axis=-1, keepdims=True) + L2_EPS)
                outs.append(t * (HEAD_DIM ** -0.5) if grp * gw < D_DN else t)
            act = jnp.concatenate(outs, axis=1)
        dn_ref[:, cs] = act

        if grp == 0:
            qr = [[rope(mb[rs, hs], rs) * (HEAD_DIM ** -0.5) for hs in heads] for rs in rows]
            for sb, rs in enumerate(rows):
                q_ref[rs, :] = jnp.concatenate([q * LOG2E for q in qr[sb]], axis=1).astype(q_ref.dtype)
        elif grp == 1:
            blk = lax.broadcasted_iota(jnp.int32, (nb, MOBA_BLOCK), 0)
            kmrow = lax.broadcasted_iota(jnp.int32, (nb, D_MOBA), 0)
            km = km_ref[...]
            for sb, rs in enumerate(rows):
                j = seq_tile * spb + sb
                kr = [rope(mb[rs, hs], rs) for hs in heads]
                k_ref[rs, :] = jnp.concatenate(kr, axis=1).astype(k_ref.dtype)
                for h, hs in enumerate(heads):
                    gate = _dot3_nt(km[:, hs], qr[sb][h])
                    gate = jnp.where(blk < j, gate, NEG_INF)
                    rank = jnp.zeros(gate.shape, F32)
                    for m in range(nb):
                        gm = gate[m:m + 1, :]
                        ahead = (gm > gate) | ((gm == gate) & (blk > m))
                        rank = rank + jnp.where(ahead, 1.0, 0.0)
                    sel = (blk < j) & (rank < topk)
                    sel_ref[sb, h * nb:(h + 1) * nb, :] = jnp.where(sel, 1.0, 0.0)
                kmean = jnp.concatenate([jnp.mean(t, axis=0, keepdims=True) for t in kr], axis=1)
                km = jnp.where(kmrow == j, kmean, km)
            km_ref[...] = km
        else:
            for sb, rs in enumerate(rows):
                vt_ref[sb] = jnp.transpose(mb[rs, :]).astype(vt_ref.dtype)
    z_ref[...] = _silu(jnp.dot(xb, w_ref[:, o0:o1], preferred_element_type=F32))
    ba_ref[...] = jnp.dot(xb, w_ref[:, o1:o2], preferred_element_type=F32)


def _in_proj(x2, w_all, conv_w, cos_t, sin_t, tm, seq):
    n, d = x2.shape
    wc = w_all.shape[1]
    nb = seq // MOBA_BLOCK
    spb = tm // MOBA_BLOCK
    tps = seq // tm
    kern = functools.partial(_in_proj_kernel, tiles_per_seq=tps, nb=nb, topk=min(MOBA_TOPK, nb))
    tok = lambda width: pl.BlockSpec((tm, width), lambda i: (i, 0))
    tab = pl.BlockSpec((tm, HEAD_DIM), lambda i: (i % tps, 0))
    return pl.pallas_call(
        kern,
        out_shape=(jax.ShapeDtypeStruct((n, 3 * D_DN), F32), jax.ShapeDtypeStruct((n, D_DN), F32),
                   jax.ShapeDtypeStruct((n, LANES), F32),
                   jax.ShapeDtypeStruct((n, D_MOBA), BF16), jax.ShapeDtypeStruct((n, D_MOBA), BF16),
                   jax.ShapeDtypeStruct((n // MOBA_BLOCK, D_MOBA, MOBA_BLOCK), BF16),
                   jax.ShapeDtypeStruct((n // MOBA_BLOCK, N_HEADS_MOBA * nb, MOBA_BLOCK), F32)),
        grid=(n // tm,),
        in_specs=[tok(d), pl.BlockSpec((d, wc), lambda i: (0, 0)),
                  pl.BlockSpec((CONV_K, 3 * D_DN), lambda i: (0, 0)), tab, tab],
        out_specs=(tok(3 * D_DN), tok(D_DN), tok(LANES), tok(D_MOBA), tok(D_MOBA),
                   pl.BlockSpec((spb, D_MOBA, MOBA_BLOCK), lambda i: (i, 0, 0)),
                   pl.BlockSpec((spb, N_HEADS_MOBA * nb, MOBA_BLOCK), lambda i: (i, 0, 0))),
        scratch_shapes=[pltpu.VMEM((8 + tm, 3 * D_DN), F32), pltpu.VMEM((nb, D_MOBA), F32)],
        compiler_params=pltpu.CompilerParams(dimension_semantics=("arbitrary",), vmem_limit_bytes=VMEM_LIMIT),
        name="in_proj",
    )(x2, w_all, conv_w, cos_t, sin_t)


def _deltanet_kernel(q_ref, k_ref, v_ref, z_ref, ba_ref, alog_ref, dtb_ref, normw_ref, y_ref,
                     s_ref, wq_s, u_s, qk_s, kdt_s, egl_s, *, hb):
    hg = pl.program_id(1)
    t = pl.program_id(2)
    tt = DN_TILE
    nchunk = tt // DN_CHUNK
    nbat = q_ref.shape[0]
    hs = range(nbat * hb)

    @pl.when(t == 0)
    def _():
        for ref in (s_ref, wq_s, u_s, qk_s, kdt_s, egl_s):
            ref[...] = jnp.zeros_like(ref)

    rd = t & 1
    wr = 1 - rd
    state = [s_ref[h] for h in hs]
    outs = [[] for _ in hs]

    pend = {}

    def chain_a(c):
        pend["r"] = [jnp.dot(wq_s[rd, h, c], state[h].astype(BF16), preferred_element_type=F32) for h in hs]

    def chain_b(c):
        lo, hi = c * DN_CHUNK, (c + 1) * DN_CHUNK
        r = pend["r"]
        vz = []
        for h in hs:
            parts = []
            if lo > 0:
                parts.append(jnp.zeros((lo, HEAD_DIM), F32))
            parts.append(u_s[rd, h, lo:hi, :] - r[h][0:DN_CHUNK, :])
            if hi < tt:
                parts.append(jnp.zeros((tt - hi, HEAD_DIM), F32))
            vz.append(jnp.concatenate(parts, axis=0).astype(BF16))
        for h in hs:
            outs[h].append(r[h][DN_CHUNK:2 * DN_CHUNK, :]
                           + jnp.dot(qk_s[rd, h, lo:hi, :], vz[h], preferred_element_type=F32))
        for h in hs:
            state[h] = (state[h] * egl_s[rd, h, 8 * c:8 * c + 1, :]
                        + jnp.dot(kdt_s[rd, h], vz[h], preferred_element_type=F32))

    chain_a(0)

    bas = [ba_ref[bb] for bb in range(nbat)]
    beta_all = [jax.nn.sigmoid(ba) for ba in bas]
    g_all = [-jnp.exp(alog_ref[...]) * _softplus(ba + dtb_ref[...]) for ba in bas]

    row = lax.broadcasted_iota(jnp.int32, (tt, tt), 0)
    col = lax.broadcasted_iota(jnp.int32, (tt, tt), 1)
    same = (row >> 6) == (col >> 6)
    incl = same & (row >= col)
    strict = same & (row > col)

    incl_b = incl.astype(BF16)
    gc_all = [_dot_exact_lhs(incl_b, g) for g in g_all]
    gct = [jnp.transpose(g) for g in gc_all]
    sub = lax.broadcasted_iota(jnp.int32, gct[0].shape, 0)
    bat = [vh // hb for vh in hs]
    sls = [slice((vh % hb) * HEAD_DIM, (vh % hb + 1) * HEAD_DIM) for vh in hs]
    heads = [hg * hb + vh % hb for vh in hs]
    q = [q_ref[bat[vh], :, sls[vh]] for vh in hs]
    k = [k_ref[bat[vh], :, sls[vh]] for vh in hs]
    v = [v_ref[bat[vh], :, sls[vh]] for vh in hs]
    beta = [_lane_pick(beta_all[bat[vh]], heads[vh]) for vh in hs]
    gcc = [_lane_pick(gc_all[bat[vh]], heads[vh] + N_HEADS_DN) for vh in hs]
    gcr = [jnp.sum(jnp.where(sub == heads[vh] + N_HEADS_DN, gct[bat[vh]], 0.0), axis=0, keepdims=True)
           for vh in hs]
    chain_b(0)

    decay = [jnp.where(incl, jnp.exp(jnp.where(incl, gcc[h] - gcr[h], 0.0)), 0.0) for h in hs]
    kb = [k[h] * beta[h] for h in hs]
    vb = [v[h] * beta[h] for h in hs]
    a_mat = [jnp.where(strict, _dot_nt(kb[h], k[h]) * decay[h], 0.0) for h in hs]
    chain_a(1)
    qk = [_dot_nt(q[h], k[h]) * decay[h] for h in hs]
    eye = (row == col).astype(F32)
    d8 = (row >> 3) == (col >> 3)
    a8 = [jnp.where(d8, a, 0.0) for a in a_mat]
    chain_b(1)
    a8_2 = [_dot(a, a) for a in a8]
    chain_a(2)
    a8_4 = [_dot(a, a) for a in a8_2]
    chain_b(2)
    x = [_dot(eye - a, eye + a2) for a, a2 in zip(a8, a8_2)]
    chain_a(3)
    x = [_dot(xi, eye + a4) for xi, a4 in zip(x, a8_4)]
    chain_b(3)
    s = 8
    while s < DN_CHUNK:
        sh = s.bit_length() - 1
        nblk = tt // (2 * s)
        second = lambda m: jnp.concatenate([m[b * 2 * s + s:(b + 1) * 2 * s, :] for b in range(nblk)], axis=0)
        off = second(((row >> (sh + 1)) == (col >> (sh + 1))) & ((row >> sh) != (col >> sh)))
        y_half = [_dot(jnp.where(off, second(a), 0.0), xi) for a, xi in zip(a_mat, x)]
        zeros = jnp.zeros((s, tt), F32)
        y = [jnp.concatenate([p for b in range(nblk) for p in (zeros, yh[b * s:(b + 1) * s, :])], axis=0)
             for yh in y_half]
        upd = [_dot(second(xi), yi) for xi, yi in zip(x, y)]
        x = [jnp.concatenate([p for b in range(nblk)
                              for p in (xi[b * 2 * s:b * 2 * s + s, :],
                                        xi[b * 2 * s + s:(b + 1) * 2 * s, :] - ud[b * s:(b + 1) * s, :])], axis=0)
             for xi, ud in zip(x, upd)]
        s *= 2
    tinv = x
    eg = [jnp.exp(g) for g in gcc]
    wu = [_dot(tinv[h], jnp.concatenate([kb[h] * eg[h], vb[h]], axis=1)) for h in hs]
    qd = [q[h] * eg[h] for h in hs]
    gl_rows = [[g[(c + 1) * DN_CHUNK - 1:(c + 1) * DN_CHUNK, :] for c in range(nchunk)] for g in gcc]
    gl_col = [jnp.concatenate([jnp.broadcast_to(g, (DN_CHUNK, 1)) for g in rows], axis=0) for rows in gl_rows]
    kdt = [jnp.transpose(k[h] * jnp.exp(gl_col[h] - gcc[h])) for h in hs]

    ys = []
    for h in hs:
        o = jnp.concatenate(outs[h], axis=0)
        o = o * lax.rsqrt(jnp.mean(o * o, axis=-1, keepdims=True) + RMS_EPS) * normw_ref[...]
        ys.append(o * z_ref[bat[h], :, sls[h]])
    for bb in range(nbat):
        y_ref[bb] = jnp.concatenate(ys[bb * hb:(bb + 1) * hb], axis=1).astype(y_ref.dtype)
    s_ref[...] = jnp.stack(state, axis=0)

    for h in hs:
        for c in range(nchunk):
            lo, hi = c * DN_CHUNK, (c + 1) * DN_CHUNK
            wq_s[wr, h, c] = jnp.concatenate([wu[h][lo:hi, 0:HEAD_DIM], qd[h][lo:hi, :]], axis=0).astype(BF16)
            egl_s[wr, h, 8 * c:8 * c + 8, :] = jnp.broadcast_to(jnp.exp(gl_rows[h][c]), (8, HEAD_DIM))
        u_s[wr, h] = wu[h][:, HEAD_DIM:2 * HEAD_DIM]
        qk_s[wr, h] = qk[h].astype(BF16)
        kdt_s[wr, h] = kdt[h].astype(BF16)


def _deltanet(dn_qkv, z, ba, alog_row, dtb_row, normw_row):
    bsz, seq, _ = dn_qkv.shape
    tt = DN_TILE
    nt = seq // tt
    hb = DN_HEADS_PER_STEP
    nbat = DN_SEQS_PER_STEP if bsz % DN_SEQS_PER_STEP == 0 else 1
    nv = nbat * hb
    ng = N_HEADS_DN // hb
    w = hb * HEAD_DIM
    nchunk = tt // DN_CHUNK

    def cur_spec(off, width):
        return pl.BlockSpec((nbat, tt, width), lambda b, g, t: (b, jnp.minimum(t, nt - 1), g + off))

    prev_spec = pl.BlockSpec((nbat, tt, w), lambda b, g, t: (b, jnp.maximum(t - 1, 0), g))
    row_spec = pl.BlockSpec((1, LANES), lambda b, g, t: (0, 0))
    return pl.pallas_call(
        functools.partial(_deltanet_kernel, hb=hb),
        out_shape=jax.ShapeDtypeStruct((bsz, seq, D_DN), BF16),
        grid=(bsz // nbat, ng, nt + 1),
        in_specs=[cur_spec(0, w), cur_spec(ng, w), cur_spec(2 * ng, w), prev_spec,
                  pl.BlockSpec((nbat, tt, LANES), lambda b, g, t: (b, jnp.minimum(t, nt - 1), 0)),
                  row_spec, row_spec, row_spec],
        out_specs=prev_spec,
        scratch_shapes=[pltpu.VMEM((nv, HEAD_DIM, HEAD_DIM), F32),
                        pltpu.VMEM((2, nv, nchunk, 2 * DN_CHUNK, HEAD_DIM), BF16),
                        pltpu.VMEM((2, nv, tt, HEAD_DIM), F32),
                        pltpu.VMEM((2, nv, tt, tt), BF16),
                        pltpu.VMEM((2, nv, HEAD_DIM, tt), BF16),
                        pltpu.VMEM((2, nv, 8 * nchunk, HEAD_DIM), F32)],
        compiler_params=pltpu.CompilerParams(dimension_semantics=("parallel", "parallel", "arbitrary"),
                                             vmem_limit_bytes=VMEM_LIMIT),
        name="deltanet",
    )(dn_qkv, dn_qkv, dn_qkv, z, ba, alog_row, dtb_row, normw_row)


def _moba_attn_kernel(q_ref, k_ref, vt_ref, sel_ref, o_ref, acc_ref, *, nb):
    j = pl.program_id(1)
    blk = MOBA_BLOCK
    nh = N_HEADS_MOBA
    dn = (((1,), (1,)), ((), ()))
    hsl = [slice(h * HEAD_DIM, (h + 1) * HEAD_DIM) for h in range(nh)]
    qs = [q_ref[0, :, hsl[h]] for h in range(nh)]

    ki = lax.broadcasted_iota(jnp.int32, (blk, blk), 0)
    qi = lax.broadcasted_iota(jnp.int32, (blk, blk), 1)
    own = pl.ds(pl.multiple_of(j * blk, blk), blk)
    s_own = [jnp.where(ki <= qi, lax.dot_general(k_ref[0, own, hsl[h]], qs[h], dn, preferred_element_type=F32),
                       NEG_INF) for h in range(nh)]

    def scores(n, h):
        kn = k_ref[0, pl.ds(pl.multiple_of(n * blk, blk), blk), hsl[h]]
        s = lax.dot_general(kn, qs[h], dn, preferred_element_type=F32)
        return jnp.where(sel_ref[0, 0, pl.ds(h * nb + n, 1), :] > 0.5, s, NEG_INF)

    ones8 = jnp.ones((8, blk), BF16)

    def softmax_pv(n, h, s, m, l):
        m_new = jnp.maximum(m, jnp.max(s, axis=0, keepdims=True))
        pb = jnp.exp2(s - m_new).astype(BF16)
        psum = jnp.dot(ones8, pb, preferred_element_type=F32)[0:1, :]
        pv = jnp.dot(vt_ref[0, n, hsl[h], :], pb, preferred_element_type=F32)
        return m_new, psum, pv

    ms, ls = [], []
    for h in range(nh):
        m, psum, pv = softmax_pv(j, h, s_own[h], jnp.full((1, blk), NEG_INF, F32), None)
        ms.append(m)
        ls.append(psum)
        acc_ref[h] = pv

    def body(step, carry):
        ms, ls = (list(t) for t in carry)
        n0 = MOBA_UNROLL * step
        cur = [scores(n0, h) for h in range(nh)]
        for i in range(MOBA_UNROLL):
            nxt = []
            for h in range(nh):
                if i + 1 < MOBA_UNROLL:
                    nxt.append(scores(n0 + i + 1, h))
                m_new, psum, pv = softmax_pv(n0 + i, h, cur[h], ms[h], ls[h])
                alpha = jnp.exp2(ms[h] - m_new)
                ls[h] = alpha * ls[h] + psum
                ms[h] = m_new
                acc_ref[h] = acc_ref[h] * alpha + pv
            cur = nxt
        return tuple(ms), tuple(ls)

    ms, ls = lax.fori_loop(0, (j + MOBA_UNROLL - 1) // MOBA_UNROLL, body, (tuple(ms), tuple(ls)))
    o_ref[0] = jnp.concatenate([jnp.transpose(acc_ref[h] / ls[h]) for h in range(nh)],
                               axis=1).astype(o_ref.dtype)


def _moba_attn(q_r, k_r, vt, sel):
    bsz, seq, _ = q_r.shape
    nb = seq // MOBA_BLOCK
    assert nb % MOBA_UNROLL == 0, (nb, MOBA_UNROLL)
    tok_spec = pl.BlockSpec((1, MOBA_BLOCK, D_MOBA), lambda b, j: (b, j, 0))
    return pl.pallas_call(
        functools.partial(_moba_attn_kernel, nb=nb),
        out_shape=jax.ShapeDtypeStruct((bsz, seq, D_MOBA), BF16),
        grid=(bsz, nb),
        in_specs=[tok_spec,
                  pl.BlockSpec((1, seq, D_MOBA), lambda b, j: (b, 0, 0)),
                  pl.BlockSpec((1, nb, D_MOBA, MOBA_BLOCK), lambda b, j: (b, 0, 0, 0)),
                  pl.BlockSpec((1, 1, N_HEADS_MOBA * nb, MOBA_BLOCK), lambda b, j: (b, j, 0, 0))],
        out_specs=tok_spec,
        scratch_shapes=[pltpu.VMEM((N_HEADS_MOBA, HEAD_DIM, MOBA_BLOCK), F32)],
        compiler_params=pltpu.CompilerParams(dimension_semantics=("parallel", "arbitrary"),
                                             vmem_limit_bytes=VMEM_LIMIT),
        name="moba_attn",
    )(q_r, k_r, vt, sel)


def _route_record(logits):
    lane = lax.broadcasted_iota(jnp.int32, logits.shape, 1)
    big = jnp.int32(LANES)

    def first_lane(mask):
        return jnp.min(jnp.where(mask, lane, big), axis=1, keepdims=True)

    is_g = lane < N_GROUPS
    m1 = jnp.max(jnp.where(is_g, logits, NEG_INF), axis=1, keepdims=True)
    s1 = jnp.sum(jnp.where(is_g, jnp.exp(logits - m1), 0.0), axis=1, keepdims=True)
    pg = 1.0 / s1
    gsel = first_lane(is_g & (logits == m1))

    in_grp = (lane >= GATE_LANE0) & (((lane - GATE_LANE0) >> 2) == gsel) & (lane < GATE_LANE0 + N_EXPERTS)
    m2 = jnp.max(jnp.where(in_grp, logits, NEG_INF), axis=1, keepdims=True)
    s2 = jnp.sum(jnp.where(in_grp, jnp.exp(logits - m2), 0.0), axis=1, keepdims=True)
    e1 = first_lane(in_grp & (logits == m2))
    rest = in_grp & (lane != e1)
    m2b = jnp.max(jnp.where(rest, logits, NEG_INF), axis=1, keepdims=True)
    e2 = first_lane(rest & (logits == m2b))
    pe1 = 1.0 / s2
    pe2 = jnp.exp(m2b - m2) / s2
    tot = pe1 + pe2
    w1 = pg * (pe1 / tot)
    w2 = pg * (pe2 / tot)
    first_lo = e1 < e2
    lo = jnp.minimum(e1, e2)
    hi = jnp.maximum(e1, e2)
    a = (lo - GATE_LANE0) & (EXPERTS_PER_GROUP - 1)
    b = (hi - GATE_LANE0) & (EXPERTS_PER_GROUP - 1)
    bucket = gsel * N_PAIRS + ((a * (2 * EXPERTS_PER_GROUP - 1 - a)) >> 1) + (b - a - 1)
    record = jnp.where(lane == ROUTE_BUCKET, bucket.astype(F32),
                       jnp.where(lane == ROUTE_W_LO, jnp.where(first_lo, w1, w2),
                                 jnp.where(lane == ROUTE_W_HI, jnp.where(first_lo, w2, w1), 0.0)))
    return record, jnp.sum(jnp.where(lane == bucket, 1.0, 0.0), axis=0, keepdims=True)


def _mix_route_kernel(ydn_ref, ymb_ref, x_ref, wo_ref, g_ref, b_ref, rw_ref, rb_ref, h_ref, hb_ref, route_ref,
                      cnt_ref):
    nsub = cnt_ref.shape[0]
    rows = [slice(i * MOE_TS, (i + 1) * MOE_TS) for i in range(nsub)]
    wh, wl = _split2(rw_ref[...])
    wcat = jnp.concatenate([wh, wl], axis=1)

    mixes = [jnp.dot(jnp.concatenate([ydn_ref[r, :], ymb_ref[r, :]], axis=1), wo_ref[...],
                     preferred_element_type=F32) for r in rows]
    logits = []
    for r, mix in zip(rows, mixes):
        hval = _layer_norm(DEEPNORM_ALPHA * x_ref[r, :] + mix, g_ref[...], b_ref[...])
        h_ref[r, :] = hval
        hb_ref[r, :] = hval.astype(BF16)
        hh, hl = _split2(hval)
        both = jnp.dot(hh, wcat, preferred_element_type=F32)
        logits.append(both[:, 0:LANES] + both[:, LANES:2 * LANES]
                      + jnp.dot(hl, wh, preferred_element_type=F32) + rb_ref[...])
    for i, r in enumerate(rows):
        record, counts = _route_record(logits[i])
        route_ref[r, :] = record
        cnt_ref[i] = counts


def _mix_route(y_dn, y_mb, x2, wo, g1, b1, rw, rb, tm):
    n, d = x2.shape
    nsub = tm // MOE_TS
    row = lambda w: pl.BlockSpec((1, w), lambda i: (0, 0))
    return pl.pallas_call(
        _mix_route_kernel,
        out_shape=(jax.ShapeDtypeStruct((n, d), F32), jax.ShapeDtypeStruct((n, d), BF16),
                   jax.ShapeDtypeStruct((n, LANES), F32), jax.ShapeDtypeStruct((n // MOE_TS, 1, LANES), F32)),
        grid=(n // tm,),
        in_specs=[pl.BlockSpec((tm, D_DN), lambda i: (i, 0)), pl.BlockSpec((tm, D_MOBA), lambda i: (i, 0)),
                  pl.BlockSpec((tm, d), lambda i: (i, 0)), pl.BlockSpec((D_DN + D_MOBA, d), lambda i: (0, 0)),
                  row(d), row(d), pl.BlockSpec((d, LANES), lambda i: (0, 0)), row(LANES)],
        out_specs=(pl.BlockSpec((tm, d), lambda i: (i, 0)), pl.BlockSpec((tm, d), lambda i: (i, 0)),
                   pl.BlockSpec((tm, LANES), lambda i: (i, 0)), pl.BlockSpec((nsub, 1, LANES), lambda i: (i, 0, 0))),
        compiler_params=pltpu.CompilerParams(dimension_semantics=("parallel",), vmem_limit_bytes=VMEM_LIMIT),
        name="mix_route",
    )(y_dn, y_mb, x2, wo, g1, b1, rw, rb)


def _bucket_offsets_col(ohf):
    cnt = jnp.sum(ohf, axis=1, keepdims=True).astype(jnp.int32)
    pad = (((cnt + (GRAN - 1)) >> GRAN_SHIFT) << GRAN_SHIFT).astype(F32)
    r = lax.broadcasted_iota(jnp.int32, (LANES, LANES), 0)
    c = lax.broadcasted_iota(jnp.int32, (LANES, LANES), 1)
    before = jnp.where(c < r, 1.0, 0.0)
    return _dot(before, jnp.broadcast_to(pad, (LANES, LANES)))[:, 0:1]


def _moe_sort_kernel(gmap_ref, nvalid_ref, tail0_ref, taillen_ref, hb_ref, route_ref, lstrict_ref,
                     xg_ref, wsg_ref, xs_ref, ws_ref, zx_ref, zw_ref, sem):
    s = pl.program_id(0)
    nsteps = pl.num_programs(0)
    slot = s & 1
    ts = route_ref.shape[0]
    route = route_ref[...]
    rt = jnp.transpose(route)
    bucket_row = rt[ROUTE_BUCKET:ROUTE_BUCKET + 1, :].astype(jnp.int32)
    sub = lax.broadcasted_iota(jnp.int32, (LANES, ts), 0)
    ohf = jnp.where(sub == bucket_row, 1.0, 0.0)
    loff = _bucket_offsets_col(ohf)
    rank = lax.dot_general(ohf.astype(BF16), lstrict_ref[...], (((1,), (1,)), ((), ())),
                           preferred_element_type=F32)
    dest = jnp.sum(ohf * (loff + rank), axis=0, keepdims=True).astype(jnp.int32)
    rh, rl = _split2(route)
    wcat = jnp.concatenate([rh, rl], axis=1)
    rc = LROWS // PERM_CHUNKS
    for c in range(PERM_CHUNKS):
        rowi = lax.broadcasted_iota(jnp.int32, (rc, ts), 0) + c * rc
        perm = jnp.where(rowi == dest, 1.0, 0.0).astype(BF16)
        xs_ref[slot, c * rc:(c + 1) * rc, :] = jnp.dot(perm, hb_ref[...], preferred_element_type=F32).astype(BF16)
        wparts = jnp.dot(perm, wcat, preferred_element_type=F32)
        ws_ref[slot, c * rc:(c + 1) * rc, :] = wparts[:, 0:LANES] + wparts[:, LANES:2 * LANES]

    def copies(step, g):
        sl = step & 1
        src = pl.ds(pl.multiple_of(g * GRAN, GRAN), GRAN)
        dst = pl.ds(pl.multiple_of(gmap_ref[step * LGRAN + g] * GRAN, GRAN), GRAN)
        return (pltpu.make_async_copy(xs_ref.at[sl, src, :], xg_ref.at[dst, :], sem.at[0, sl]),
                pltpu.make_async_copy(ws_ref.at[sl, src, :], wsg_ref.at[dst, :], sem.at[1, sl]))

    def fill_copies(b, i):
        dst = pl.ds(pl.multiple_of((tail0_ref[b] + i) * GRAN, GRAN), GRAN)
        return (pltpu.make_async_copy(zx_ref.at[0:GRAN, :], xg_ref.at[dst, :], sem.at[2, 0]),
                pltpu.make_async_copy(zw_ref.at[0:GRAN, :], wsg_ref.at[dst, :], sem.at[2, 1]))

    def unused_tile_copies(t):
        dst = pl.ds(pl.multiple_of(t * MOE_TM, MOE_TM), MOE_TM)
        return (pltpu.make_async_copy(zx_ref, xg_ref.at[dst, :], sem.at[2, 0]),
                pltpu.make_async_copy(zw_ref, wsg_ref.at[dst, :], sem.at[2, 1]))

    def run(step, fn):
        def body(g, carry):
            for cp in copies(step, g):
                fn(cp)
            return carry
        lax.fori_loop(0, nvalid_ref[step], body, 0)

    def run_fill(fn):
        for b in range(N_BUCKETS):
            def body(i, carry, b=b):
                for cp in fill_copies(b, i):
                    fn(cp)
                return carry
            lax.fori_loop(0, taillen_ref[b], body, 0)

        def tile_body(t, carry):
            for cp in unused_tile_copies(t):
                fn(cp)
            return carry
        lax.fori_loop(tail0_ref[N_BUCKETS], xg_ref.shape[0] // MOE_TM, tile_body, 0)

    @pl.when(s == 0)
    def _():
        zx_ref[...] = jnp.zeros_like(zx_ref)
        zw_ref[...] = jnp.zeros_like(zw_ref)
        run_fill(lambda cp: cp.start())

    run(s, lambda cp: cp.start())

    @pl.when(s > 0)
    def _():
        run(s - 1, lambda cp: cp.wait())

    @pl.when(s == nsteps - 1)
    def _():
        run(s, lambda cp: cp.wait())
        run_fill(lambda cp: cp.wait())


def _moe_sort(plan, hb, route, lstrict):
    n, d = hb.shape
    ts = MOE_TS
    rows = plan["n_tiles"] * MOE_TM
    return pl.pallas_call(
        _moe_sort_kernel,
        out_shape=(jax.ShapeDtypeStruct((rows, d), BF16), jax.ShapeDtypeStruct((rows, LANES), F32)),
        grid_spec=pltpu.PrefetchScalarGridSpec(
            num_scalar_prefetch=4,
            grid=(n // ts,),
            in_specs=[pl.BlockSpec((ts, d), lambda s, *_: (s, 0)), pl.BlockSpec((ts, LANES), lambda s, *_: (s, 0)),
                      pl.BlockSpec((ts, ts), lambda s, *_: (0, 0))],
            out_specs=(pl.BlockSpec(memory_space=pl.ANY), pl.BlockSpec(memory_space=pl.ANY)),
            scratch_shapes=[pltpu.VMEM((2, LROWS, d), BF16), pltpu.VMEM((2, LROWS, LANES), F32),
                            pltpu.VMEM((MOE_TM, d), BF16), pltpu.VMEM((MOE_TM, LANES), F32),
                            pltpu.SemaphoreType.DMA((3, 2))]),
        compiler_params=pltpu.CompilerParams(dimension_semantics=("arbitrary",), vmem_limit_bytes=VMEM_LIMIT),
        name="moe_sort",
    )(plan["gmap"], plan["nvalid"], plan["tail0"], plan["taillen"], hb, route, lstrict)


def _moe_expert_kernel(xt_ref, elo_ref, ehi_ref, valid_ref, x_ref, w_ref, wg0, wu0, wd0, wg1, wu1, wd1, o_ref):
    t = pl.program_id(0)

    @pl.when(valid_ref[t] > 0)
    def _():
        x = x_ref[...]
        w = w_ref[...]
        gates = [jnp.dot(x, wg[0].astype(BF16), preferred_element_type=F32) for wg in (wg0, wg1)]
        ups = [jnp.dot(x, wu[0].astype(BF16), preferred_element_type=F32) for wu in (wu0, wu1)]
        hes = [(_silu(gates[i]) * ups[i] * w[:, lane:lane + 1]).astype(BF16)
               for i, lane in enumerate((ROUTE_W_LO, ROUTE_W_HI))]
        o_ref[...] = (jnp.dot(hes[0], wd0[0].astype(BF16), preferred_element_type=F32)
                      + jnp.dot(hes[1], wd1[0].astype(BF16), preferred_element_type=F32)).astype(o_ref.dtype)

    @pl.when(valid_ref[t] == 0)
    def _():
        o_ref[...] = jnp.zeros_like(o_ref)


def _moe_experts(plan, xg, wsg, wg, wu, wd):
    rows, d = xg.shape
    tm = MOE_TM
    tok = lambda width: pl.BlockSpec((tm, width), lambda t, xt, elo, ehi, valid: (xt[t], 0))
    lo3 = lambda shape: pl.BlockSpec(shape, lambda t, xt, elo, ehi, valid: (elo[t], 0, 0))
    hi3 = lambda shape: pl.BlockSpec(shape, lambda t, xt, elo, ehi, valid: (ehi[t], 0, 0))
    return pl.pallas_call(
        _moe_expert_kernel,
        out_shape=jax.ShapeDtypeStruct((rows, d), BF16),
        grid_spec=pltpu.PrefetchScalarGridSpec(
            num_scalar_prefetch=4,
            grid=(rows // tm,),
            in_specs=[tok(d), tok(LANES),
                      lo3((1, d, D_EXPERT)), lo3((1, d, D_EXPERT)), lo3((1, D_EXPERT, d)),
                      hi3((1, d, D_EXPERT)), hi3((1, d, D_EXPERT)), hi3((1, D_EXPERT, d))],
            out_specs=pl.BlockSpec((tm, d), lambda t, *_: (t, 0))),
        compiler_params=pltpu.CompilerParams(dimension_semantics=("arbitrary",), vmem_limit_bytes=VMEM_LIMIT),
        name="moe_experts",
    )(plan["xtile"], plan["elo"], plan["ehi"], plan["valid"], xg, wsg, wg, wu, wd, wg, wu, wd)


def _moe_unsort_kernel(gmap_ref, og_ref, route_ref, h_ref, lstrict_ref, g_ref, b_ref, out_ref, ol_ref, sem):
    s = pl.program_id(0)
    nsteps = pl.num_programs(0)
    slot = s & 1
    ts = route_ref.shape[0]

    def gather(step, fn):
        sl = step & 1

        def body(g, carry):
            src = pl.ds(pl.multiple_of(gmap_ref[step * LGRAN + g] * GRAN, GRAN), GRAN)
            dst = pl.ds(pl.multiple_of(g * GRAN, GRAN), GRAN)
            fn(pltpu.make_async_copy(og_ref.at[src, :], ol_ref.at[sl, dst, :], sem.at[sl]))
            return carry
        lax.fori_loop(0, LGRAN, body, 0)

    @pl.when(s == 0)
    def _():
        gather(s, lambda cp: cp.start())

    @pl.when(s + 1 < nsteps)
    def _():
        gather(s + 1, lambda cp: cp.start())

    route = route_ref[...]
    bucket_col = route[:, ROUTE_BUCKET:ROUTE_BUCKET + 1].astype(jnp.int32)
    lane = lax.broadcasted_iota(jnp.int32, (ts, LANES), 1)
    ohf = jnp.where(lane == bucket_col, 1.0, 0.0)
    cnt = jnp.sum(ohf, axis=0, keepdims=True).astype(jnp.int32)
    pad = (((cnt + (GRAN - 1)) >> GRAN_SHIFT) << GRAN_SHIFT).astype(F32)
    r = lax.broadcasted_iota(jnp.int32, (LANES, LANES), 0)
    c = lax.broadcasted_iota(jnp.int32, (LANES, LANES), 1)
    loff = _dot(jnp.broadcast_to(pad, (8, LANES)), jnp.where(r < c, 1.0, 0.0))[0:1, :]
    rank = jnp.dot(lstrict_ref[...], ohf.astype(BF16), preferred_element_type=F32)
    dest = jnp.sum(ohf * (loff + rank), axis=1, keepdims=True).astype(jnp.int32)
    gather(s, lambda cp: cp.wait())
    tc = ts // PERM_CHUNKS
    lrow = lax.broadcasted_iota(jnp.int32, (tc, LROWS), 1)
    for c in range(PERM_CHUNKS):
        rows = slice(c * tc, (c + 1) * tc)
        perm_t = jnp.where(lrow == dest[rows, :], 1.0, 0.0).astype(BF16)
        ffn = jnp.dot(perm_t, ol_ref[slot], preferred_element_type=F32)
        out_ref[rows, :] = _layer_norm(DEEPNORM_ALPHA * h_ref[rows, :] + ffn, g_ref[...], b_ref[...])


def _moe_unsort(plan, og, route, hf, lstrict, g2, b2):
    n, d = hf.shape
    ts = MOE_TS
    row = pl.BlockSpec((1, d), lambda s, *_: (0, 0))
    return pl.pallas_call(
        _moe_unsort_kernel,
        out_shape=jax.ShapeDtypeStruct((n, d), F32),
        grid_spec=pltpu.PrefetchScalarGridSpec(
            num_scalar_prefetch=1,
            grid=(n // ts,),
            in_specs=[pl.BlockSpec(memory_space=pl.ANY), pl.BlockSpec((ts, LANES), lambda s, *_: (s, 0)),
                      pl.BlockSpec((ts, d), lambda s, *_: (s, 0)), pl.BlockSpec((ts, ts), lambda s, *_: (0, 0)),
                      row, row],
            out_specs=pl.BlockSpec((ts, d), lambda s, *_: (s, 0)),
            scratch_shapes=[pltpu.VMEM((2, LROWS, d), BF16), pltpu.SemaphoreType.DMA((2,))]),
        compiler_params=pltpu.CompilerParams(dimension_semantics=("arbitrary",), vmem_limit_bytes=VMEM_LIMIT),
        name="moe_unsort",
    )(plan["gmap_back"], og, route, hf, lstrict, g2, b2)


def _moe_plan(cnt_half, n):
    nsrc = n // MOE_TS
    i32 = jnp.int32
    cnt = cnt_half.reshape(nsrc, -1, LANES).sum(axis=1)[:, :N_BUCKETS].astype(i32)
    run_g = (cnt + GRAN - 1) // GRAN
    nvalid = run_g.sum(axis=1)
    loff_g = jnp.cumsum(run_g, axis=1) - run_g
    bucket_g = run_g.sum(axis=0)
    gpt = MOE_TM // GRAN
    btiles = (bucket_g + gpt - 1) // gpt
    tend = jnp.cumsum(btiles)
    tstart = tend - btiles
    gofs = tstart[None, :] * gpt + jnp.cumsum(run_g, axis=0) - run_g
    n_tiles = -(-(n + nsrc * N_BUCKETS * (GRAN - 1)) // MOE_TM) + N_BUCKETS + 1
    g = jnp.arange(LGRAN, dtype=i32)[None, :, None]
    in_run = (g >= loff_g[:, None, :]) & (g < (loff_g + run_g)[:, None, :])
    gmap = jnp.arange(LGRAN, dtype=i32)[None, :] + jnp.sum(jnp.where(in_run, (gofs - loff_g)[:, None, :], 0), axis=2)
    is_valid = jnp.arange(LGRAN, dtype=i32)[None, :] < nvalid[:, None]
    zero_gran = (n_tiles - 1) * gpt
    t = jnp.arange(n_tiles, dtype=i32)
    tb = jnp.minimum(jnp.sum(t[:, None] >= tend[None, :], axis=1), N_BUCKETS - 1)
    valid = (t < tend[-1]).astype(i32)
    pairs = [(a, b) for a in range(EXPERTS_PER_GROUP) for b in range(a + 1, EXPERTS_PER_GROUP)]
    pidx = tb % N_PAIRS
    pair_a = sum(jnp.where(pidx == i, a, 0) for i, (a, _) in enumerate(pairs))
    pair_b = sum(jnp.where(pidx == i, b, 0) for i, (_, b) in enumerate(pairs))
    grp = tb // N_PAIRS
    return {
        "n_tiles": n_tiles,
        "gmap": jnp.where(is_valid, gmap, 0).reshape(-1).astype(i32),
        "gmap_back": jnp.where(is_valid, gmap, zero_gran).reshape(-1).astype(i32),
        "nvalid": nvalid.astype(i32),
        "tail0": jnp.concatenate([tstart * gpt + bucket_g, tend[-1:]]).astype(i32),
        "taillen": (btiles * gpt - bucket_g).astype(i32),
        "xtile": jnp.where(valid > 0, t, 0).astype(i32),
        "elo": (grp * EXPERTS_PER_GROUP + pair_a).astype(i32),
        "ehi": (grp * EXPERTS_PER_GROUP + pair_b).astype(i32),
        "valid": valid,
    }


def _pad_lanes(a, lane0=0):
    return jnp.zeros((1, LANES), F32).at[0, lane0:lane0 + a.shape[0]].set(a.astype(F32))


def _rope_tables(seq):
    half = HEAD_DIM // 2
    inv_freq = ROPE_THETA ** (-jnp.arange(half, dtype=F32) / half)
    ang = jnp.arange(seq).astype(F32)[:, None] * inv_freq[None, :]
    cos, sin = jnp.cos(ang), jnp.sin(ang)
    return jnp.concatenate([cos, cos], axis=-1), jnp.concatenate([-sin, sin], axis=-1)


def _layer(x, w_in, conv_w, a_log, dt_bias, dn_norm_w, w_out, ln1_g, ln1_b, router_w1, router_b1,
           router_w2, router_b2, w_gate, w_up, w_down, ln2_g, ln2_b):
    bsz, seq, d = x.shape
    n = bsz * seq
    x2 = x.reshape(n, d)

    o_z, o_b, o_mb = 3 * D_DN, 4 * D_DN, 4 * D_DN + 2 * N_HEADS_DN
    w_ba = jnp.pad(w_in[:, o_b:o_mb], ((0, 0), (0, LANES - 2 * N_HEADS_DN)))
    w_all = jnp.concatenate([w_in[:, :o_z], w_in[:, o_z:o_b], w_ba, w_in[:, o_mb:]], axis=1).astype(BF16)

    cos_t, sin_t = _rope_tables(seq)
    nb = seq // MOBA_BLOCK
    dn_qkv, z, ba, q_r, k_r, vt, sel = _in_proj(x2, w_all, conv_w, cos_t, sin_t, min(IN_PROJ_TM, seq), seq)

    y_dn = _deltanet(dn_qkv.reshape(bsz, seq, 3 * D_DN), z.reshape(bsz, seq, D_DN), ba.reshape(bsz, seq, LANES),
                     _pad_lanes(a_log, N_HEADS_DN), _pad_lanes(dt_bias, N_HEADS_DN),
                     dn_norm_w.astype(F32).reshape(1, HEAD_DIM))

    q_r, k_r = q_r.reshape(bsz, seq, D_MOBA), k_r.reshape(bsz, seq, D_MOBA)
    vt = vt.reshape(bsz, nb, D_MOBA, MOBA_BLOCK)
    sel = sel.reshape(bsz, nb, N_HEADS_MOBA * nb, MOBA_BLOCK)
    y_mb = _moba_attn(q_r, k_r, vt, sel)

    rw = jnp.concatenate([router_w1, jnp.transpose(router_w2, (1, 0, 2)).reshape(d, N_EXPERTS)], axis=1)
    rw = jnp.pad(rw, ((0, 0), (0, LANES - rw.shape[1])))
    rb = _pad_lanes(jnp.concatenate([router_b1, router_b2.reshape(-1)]))
    hf, hb, route, cnt = _mix_route(y_dn.reshape(n, D_DN), y_mb.reshape(n, D_MOBA), x2, w_out.astype(BF16),
                                    ln1_g.reshape(1, d), ln1_b.reshape(1, d), rw, rb, min(2 * MOE_TS, n))

    plan = _moe_plan(cnt, n)
    idx = jnp.arange(MOE_TS, dtype=jnp.int32)
    lstrict = (idx[None, :] < idx[:, None]).astype(BF16)
    xg, wsg = _moe_sort(plan, hb, route, lstrict)
    og = _moe_experts(plan, xg, wsg, w_gate, w_up, w_down)
    out = _moe_unsort(plan, og, route, hf, lstrict, ln2_g.reshape(1, d), ln2_b.reshape(1, d))
    return out.reshape(bsz, seq, d)


def kernel(x, w_in, conv_w, a_log, dt_bias, dn_norm_w, w_out, ln1_g, ln1_b, router_w1, router_b1, router_w2, router_b2, expert_w_gate, expert_w_up, expert_w_down, ln2_g, ln2_b):
    for l in range(DEPTH):
        x = _layer(x, w_in[l], conv_w[l], a_log[l], dt_bias[l], dn_norm_w[l], w_out[l], ln1_g[l], ln1_b[l],
                   router_w1[l], router_b1[l], router_w2[l], router_b2[l], expert_w_gate[l], expert_w_up[l],
                   expert_w_down[l], ln2_g[l], ln2_b[l])
    return x
```

```python
import functools

import numpy as np
import jax
import jax.numpy as jnp
from jax import lax
from jax.experimental import pallas as pl
from jax.experimental.pallas import tpu as pltpu

F32 = jnp.float32
BF16 = jnp.bfloat16

HEAD_DIM = 128
N_HEADS_DN = 4
N_HEADS_MOBA = 4
D_DN = N_HEADS_DN * HEAD_DIM
D_MOBA = N_HEADS_MOBA * HEAD_DIM
CONV_K = 4
DN_CHUNK = 64
MOBA_BLOCK = 256
MOBA_TOPK = 3
MOBA_UNROLL = 4
ROPE_THETA = 10000.0
N_GROUPS = 4
EXPERTS_PER_GROUP = 4
N_EXPERTS = N_GROUPS * EXPERTS_PER_GROUP
D_EXPERT = 256
LN_EPS = 1e-5
RMS_EPS = 1e-6
L2_EPS = 1e-6
NEG_INF = -1e30
LOG2E = 1.4426950408889634
DEPTH = 1
DEEPNORM_ALPHA = (2 * DEPTH) ** 0.25

LANES = 128
IN_PROJ_TM = 512
IN_PROJ_GROUP = 512
DN_TILE = 256
DN_HEADS_PER_STEP = 4
DN_SEQS_PER_STEP = 2
GATE_LANE0 = N_GROUPS
N_PAIRS = EXPERTS_PER_GROUP * (EXPERTS_PER_GROUP - 1) // 2
N_BUCKETS = N_GROUPS * N_PAIRS
ROUTE_BUCKET, ROUTE_W_LO, ROUTE_W_HI = 0, 1, 2
MOE_TS = 512
MOE_TM = 512
GRAN = 16
GRAN_SHIFT = 4
LROWS = -(-(MOE_TS + N_BUCKETS * (GRAN - 1)) // LANES) * LANES
LGRAN = LROWS // GRAN
PERM_CHUNKS = 4
VMEM_LIMIT = 48 * 1024 * 1024


def _dot(a, b):
    return jnp.dot(a.astype(BF16), b.astype(BF16), preferred_element_type=F32)


def _dot_nt(a, b):
    return lax.dot_general(a.astype(BF16), b.astype(BF16), (((1,), (1,)), ((), ())),
                           preferred_element_type=F32)


def _split2(a):
    hi = a.astype(BF16)
    lo = (a - hi.astype(F32)).astype(BF16)
    return hi, lo


def _split3(a):
    hi = a.astype(BF16)
    r = a - hi.astype(F32)
    mid = r.astype(BF16)
    lo = (r - mid.astype(F32)).astype(BF16)
    return hi, mid, lo


def _dot3_nt(a, b):
    ah, al = _split2(a)
    bh, bl = _split2(b)
    dn = (((1,), (1,)), ((), ()))
    return (lax.dot_general(ah, bh, dn, preferred_element_type=F32)
            + lax.dot_general(ah, bl, dn, preferred_element_type=F32)
            + lax.dot_general(al, bh, dn, preferred_element_type=F32))


def _dot_exact_lhs(a_bf16, b):
    bh, bm, bl = _split3(b)
    return (jnp.dot(a_bf16, bh, preferred_element_type=F32) + jnp.dot(a_bf16, bm, preferred_element_type=F32)
            + jnp.dot(a_bf16, bl, preferred_element_type=F32))


def _silu(x):
    return x * jax.nn.sigmoid(x)


def _softplus(x):
    return jnp.maximum(x, 0.0) + jnp.log1p(jnp.exp(-jnp.abs(x)))


def _layer_norm(t, g, b):
    mu = jnp.mean(t, axis=-1, keepdims=True)
    d = t - mu
    var = jnp.mean(d * d, axis=-1, keepdims=True)
    return d * lax.rsqrt(var + LN_EPS) * g + b


def _lane_pick(x, lane):
    ids = lax.broadcasted_iota(jnp.int32, x.shape, 1)
    return jnp.sum(jnp.where(ids == lane, x, 0.0), axis=1, keepdims=True)


def _w_prep_kernel(w_ref, o_ref):
    o_b = 4 * D_DN
    o_mb = o_b + 2 * N_HEADS_DN
    o_ref[:, 0:o_b] = w_ref[:, 0:o_b].astype(o_ref.dtype)
    lane = lax.broadcasted_iota(jnp.int32, (w_ref.shape[0], LANES), 1)
    o_ref[:, o_b:o_b + LANES] = jnp.where(lane < 2 * N_HEADS_DN, w_ref[:, o_b:o_b + LANES], 0.0).astype(o_ref.dtype)
    o_ref[:, o_b + LANES:] = w_ref[:, o_mb:].astype(o_ref.dtype)


def _w_prep(w_in):
    d, cols = w_in.shape
    out_cols = cols - 2 * N_HEADS_DN + LANES
    rows = 256
    return pl.pallas_call(
        _w_prep_kernel,
        out_shape=jax.ShapeDtypeStruct((d, out_cols), BF16),
        grid=(d // rows,),
        in_specs=[pl.BlockSpec((rows, cols), lambda i: (i, 0))],
        out_specs=pl.BlockSpec((rows, out_cols), lambda i: (i, 0)),
        compiler_params=pltpu.CompilerParams(dimension_semantics=("parallel",), vmem_limit_bytes=VMEM_LIMIT),
        name="w_prep",
    )(w_in)


def _in_proj_kernel(x_ref, w_ref, cw_ref, cos_ref, sin_ref, dn_ref, z_ref, ba_ref, q_ref, k_ref, vt_ref, sel_ref,
                    cb_ref, km_ref, *, tiles_per_seq, nb, topk):
    i = pl.program_id(0)
    tm = x_ref.shape[0]
    o0 = 3 * D_DN
    o1 = o0 + D_DN
    o2 = o1 + LANES
    seq_tile = i % tiles_per_seq
    spb = tm // MOBA_BLOCK

    @pl.when(seq_tile == 0)
    def _():
        cb_ref[0:8, :] = jnp.zeros((8, o0), F32)
        km_ref[...] = jnp.zeros_like(km_ref)

    half = HEAD_DIM // 2
    heads = [slice(h * HEAD_DIM, (h + 1) * HEAD_DIM) for h in range(N_HEADS_MOBA)]
    rows = [slice(sb * MOBA_BLOCK, (sb + 1) * MOBA_BLOCK) for sb in range(spb)]

    def rope(t, rs):
        return t * cos_ref[rs, :] + pltpu.roll(t, half, 1) * sin_ref[rs, :]

    xb = x_ref[...].astype(BF16)
    gw = IN_PROJ_GROUP
    qr = None
    for grp in range(3 * D_DN // gw):
        cs = slice(grp * gw, (grp + 1) * gw)
        u = jnp.dot(xb, w_ref[:, cs], preferred_element_type=F32)
        mb = jnp.dot(xb, w_ref[:, o2 + grp * gw:o2 + (grp + 1) * gw], preferred_element_type=F32)
        cb_ref[8:8 + tm, cs] = u
        acc = cw_ref[CONV_K - 1:CONV_K, cs] * u
        for s in range(1, CONV_K):
            acc = acc + cw_ref[CONV_K - 1 - s:CONV_K - s, cs] * cb_ref[8 - s:8 - s + tm, cs]
        cb_ref[0:8, cs] = u[tm - 8:tm, :]
        act = _silu(acc)
        if grp * gw < 2 * D_DN:
            outs = []
            for h in range(gw // HEAD_DIM):
                t = act[:, h * HEAD_DIM:(h + 1) * HEAD_DIM]
                t = t * lax.rsqrt(jnp.sum(t * t, axis=-1, keepdims=True) + L2_EPS)
                outs.append(t * (HEAD_DIM ** -0.5) if grp * gw < D_DN else t)
            act = jnp.concatenate(outs, axis=1)
        dn_ref[:, cs] = act

        if grp == 0:
            qr = [[rope(mb[rs, hs], rs) * (HEAD_DIM ** -0.5) for hs in heads] for rs in rows]
            for sb, rs in enumerate(rows):
                q_ref[rs, :] = jnp.concatenate([q * LOG2E for q in qr[sb]], axis=1).astype(q_ref.dtype)
        elif grp == 1:
            blk = lax.broadcasted_iota(jnp.int32, (nb, MOBA_BLOCK), 0)
            kmrow = lax.broadcasted_iota(jnp.int32, (nb, D_MOBA), 0)
            km = km_ref[...]
            for sb, rs in enumerate(rows):
                j = seq_tile * spb + sb
                kr = [rope(mb[rs, hs], rs) for hs in heads]
                k_ref[rs, :] = jnp.concatenate(kr, axis=1).astype(k_ref.dtype)
                for h, hs in enumerate(heads):
                    gate = _dot3_nt(km[:, hs], qr[sb][h])
                    gate = jnp.where(blk < j, gate, NEG_INF)
                    rank = jnp.zeros(gate.shape, F32)
                    for m in range(nb):
                        gm = gate[m:m + 1, :]
                        ahead = (gm > gate) | ((gm == gate) & (blk > m))
                        rank = rank + jnp.where(ahead, 1.0, 0.0)
                    sel = (blk < j) & (rank < topk)
                    sel_ref[sb, h * nb:(h + 1) * nb, :] = jnp.where(sel, 1.0, 0.0)
                kmean = jnp.concatenate([jnp.mean(t, axis=0, keepdims=True) for t in kr], axis=1)
                km = jnp.where(kmrow == j, kmean, km)
            km_ref[...] = km
        else:
            for sb, rs in enumerate(rows):
                vt_ref[sb] = jnp.transpose(mb[rs, :]).astype(vt_ref.dtype)
    z_ref[...] = _silu(jnp.dot(xb, w_ref[:, o0:o1], preferred_element_type=F32))
    ba_ref[...] = jnp.dot(xb, w_ref[:, o1:o2], preferred_element_type=F32)


def _in_proj(x2, w_all, conv_w, cos_t, sin_t, tm, seq):
    n, d = x2.shape
    wc = w_all.shape[1]
    nb = seq // MOBA_BLOCK
    spb = tm // MOBA_BLOCK
    tps = seq // tm
    kern = functools.partial(_in_proj_kernel, tiles_per_seq=tps, nb=nb, topk=min(MOBA_TOPK, nb))
    tok = lambda width: pl.BlockSpec((tm, width), lambda i: (i, 0))
    tab = pl.BlockSpec((tm, HEAD_DIM), lambda i: (i % tps, 0))
    return pl.pallas_call(
        kern,
        out_shape=(jax.ShapeDtypeStruct((n, 3 * D_DN), F32), jax.ShapeDtypeStruct((n, D_DN), F32),
                   jax.ShapeDtypeStruct((n, LANES), F32),
                   jax.ShapeDtypeStruct((n, D_MOBA), BF16), jax.ShapeDtypeStruct((n, D_MOBA), BF16),
                   jax.ShapeDtypeStruct((n // MOBA_BLOCK, D_MOBA, MOBA_BLOCK), BF16),
                   jax.ShapeDtypeStruct((n // MOBA_BLOCK, N_HEADS_MOBA * nb, MOBA_BLOCK), F32)),
        grid=(n // tm,),
        in_specs=[tok(d), pl.BlockSpec((d, wc), lambda i: (0, 0)),
                  pl.BlockSpec((CONV_K, 3 * D_DN), lambda i: (0, 0)), tab, tab],
        out_specs=(tok(3 * D_DN), tok(D_DN), tok(LANES), tok(D_MOBA), tok(D_MOBA),
                   pl.BlockSpec((spb, D_MOBA, MOBA_BLOCK), lambda i: (i, 0, 0)),
                   pl.BlockSpec((spb, N_HEADS_MOBA * nb, MOBA_BLOCK), lambda i: (i, 0, 0))),
        scratch_shapes=[pltpu.VMEM((8 + tm, 3 * D_DN), F32), pltpu.VMEM((nb, D_MOBA), F32)],
        compiler_params=pltpu.CompilerParams(dimension_semantics=("arbitrary",), vmem_limit_bytes=VMEM_LIMIT),
        name="in_proj",
    )(x2, w_all, conv_w, cos_t, sin_t)


def _deltanet_kernel(q_ref, k_ref, v_ref, z_ref, ba_ref, alog_ref, dtb_ref, normw_ref, y_ref,
                     s_ref, wq_s, u_s, qk_s, kdt_s, egl_s, *, hb):
    hg = pl.program_id(1)
    t = pl.program_id(2)
    tt = DN_TILE
    nchunk = tt // DN_CHUNK
    nbat = q_ref.shape[0]
    hs = range(nbat * hb)

    @pl.when(t == 0)
    def _():
        for ref in (s_ref, wq_s, u_s, qk_s, kdt_s, egl_s):
            ref[...] = jnp.zeros_like(ref)

    rd = t & 1
    wr = 1 - rd
    state = [s_ref[h] for h in hs]
    outs = [[] for _ in hs]

    pend = {}

    def chain_a(c):
        pend["r"] = [jnp.dot(wq_s[rd, h, c], state[h].astype(BF16), preferred_element_type=F32) for h in hs]

    def chain_b(c):
        lo, hi = c * DN_CHUNK, (c + 1) * DN_CHUNK
        r = pend["r"]
        vz = []
        for h in hs:
            parts = []
            if lo > 0:
                parts.append(jnp.zeros((lo, HEAD_DIM), F32))
            parts.append(u_s[rd, h, lo:hi, :] - r[h][0:DN_CHUNK, :])
            if hi < tt:
                parts.append(jnp.zeros((tt - hi, HEAD_DIM), F32))
            vz.append(jnp.concatenate(parts, axis=0).astype(BF16))
        for h in hs:
            outs[h].append(r[h][DN_CHUNK:2 * DN_CHUNK, :]
                           + jnp.dot(qk_s[rd, h, lo:hi, :], vz[h], preferred_element_type=F32))
        for h in hs:
            state[h] = (state[h] * egl_s[rd, h, 8 * c:8 * c + 1, :]
                        + jnp.dot(kdt_s[rd, h], vz[h], preferred_element_type=F32))

    chain_a(0)

    bas = [ba_ref[bb] for bb in range(nbat)]
    beta_all = [jax.nn.sigmoid(ba) for ba in bas]
    g_all = [-jnp.exp(alog_ref[...]) * _softplus(ba + dtb_ref[...]) for ba in bas]

    row = lax.broadcasted_iota(jnp.int32, (tt, tt), 0)
    col = lax.broadcasted_iota(jnp.int32, (tt, tt), 1)
    same = (row >> 6) == (col >> 6)
    incl = same & (row >= col)
    strict = same & (row > col)

    incl_b = incl.astype(BF16)
    gc_all = [_dot_exact_lhs(incl_b, g) for g in g_all]
    gct = [jnp.transpose(g) for g in gc_all]
    sub = lax.broadcasted_iota(jnp.int32, gct[0].shape, 0)
    bat = [vh // hb for vh in hs]
    sls = [slice((vh % hb) * HEAD_DIM, (vh % hb + 1) * HEAD_DIM) for vh in hs]
    heads = [hg * hb + vh % hb for vh in hs]
    q = [q_ref[bat[vh], :, sls[vh]] for vh in hs]
    k = [k_ref[bat[vh], :, sls[vh]] for vh in hs]
    v = [v_ref[bat[vh], :, sls[vh]] for vh in hs]
    beta = [_lane_pick(beta_all[bat[vh]], heads[vh]) for vh in hs]
    gcc = [_lane_pick(gc_all[bat[vh]], heads[vh] + N_HEADS_DN) for vh in hs]
    gcr = [jnp.sum(jnp.where(sub == heads[vh] + N_HEADS_DN, gct[bat[vh]], 0.0), axis=0, keepdims=True)
           for vh in hs]
    chain_b(0)

    decay = [jnp.where(incl, jnp.exp(jnp.where(incl, gcc[h] - gcr[h], 0.0)), 0.0) for h in hs]
    kb = [k[h] * beta[h] for h in hs]
    vb = [v[h] * beta[h] for h in hs]
    a_mat = [jnp.where(strict, _dot_nt(kb[h], k[h]) * decay[h], 0.0) for h in hs]
    chain_a(1)
    qk = [_dot_nt(q[h], k[h]) * decay[h] for h in hs]
    eye = (row == col).astype(F32)
    d8 = (row >> 3) == (col >> 3)
    a8 = [jnp.where(d8, a, 0.0) for a in a_mat]
    chain_b(1)
    a8_2 = [_dot(a, a) for a in a8]
    chain_a(2)
    a8_4 = [_dot(a, a) for a in a8_2]
    chain_b(2)
    x = [_dot(eye - a, eye + a2) for a, a2 in zip(a8, a8_2)]
    chain_a(3)
    x = [_dot(xi, eye + a4) for xi, a4 in zip(x, a8_4)]
    chain_b(3)
    s = 8
    while s < DN_CHUNK:
        sh = s.bit_length() - 1
        nblk = tt // (2 * s)
        second = lambda m: jnp.concatenate([m[b * 2 * s + s:(b + 1) * 2 * s, :] for b in range(nblk)], axis=0)
        off = second(((row >> (sh + 1)) == (col >> (sh + 1))) & ((row >> sh) != (col >> sh)))
        y_half = [_dot(jnp.where(off, second(a), 0.0), xi) for a, xi in zip(a_mat, x)]
        zeros = jnp.zeros((s, tt), F32)
        y = [jnp.concatenate([p for b in range(nblk) for p in (zeros, yh[b * s:(b + 1) * s, :])], axis=0)
             for yh in y_half]
        upd = [_dot(second(xi), yi) for xi, yi in zip(x, y)]
        x = [jnp.concatenate([p for b in range(nblk)
                              for p in (xi[b * 2 * s:b * 2 * s + s, :],
                                        xi[b * 2 * s + s:(b + 1) * 2 * s, :] - ud[b * s:(b + 1) * s, :])], axis=0)
             for xi, ud in zip(x, upd)]
        s *= 2
    tinv = x
    eg = [jnp.exp(g) for g in gcc]
    wu = [_dot(tinv[h], jnp.concatenate([kb[h] * eg[h], vb[h]], axis=1)) for h in hs]
    qd = [q[h] * eg[h] for h in hs]
    gl_rows = [[g[(c + 1) * DN_CHUNK - 1:(c + 1) * DN_CHUNK, :] for c in range(nchunk)] for g in gcc]
    gl_col = [jnp.concatenate([jnp.broadcast_to(g, (DN_CHUNK, 1)) for g in rows], axis=0) for rows in gl_rows]
    kdt = [jnp.transpose(k[h] * jnp.exp(gl_col[h] - gcc[h])) for h in hs]

    ys = []
    for h in hs:
        o = jnp.concatenate(outs[h], axis=0)
        o = o * lax.rsqrt(jnp.mean(o * o, axis=-1, keepdims=True) + RMS_EPS) * normw_ref[...]
        ys.append(o * z_ref[bat[h], :, sls[h]])
    for bb in range(nbat):
        y_ref[bb] = jnp.concatenate(ys[bb * hb:(bb + 1) * hb], axis=1).astype(y_ref.dtype)
    s_ref[...] = jnp.stack(state, axis=0)

    for h in hs:
        for c in range(nchunk):
            lo, hi = c * DN_CHUNK, (c + 1) * DN_CHUNK
            wq_s[wr, h, c] = jnp.concatenate([wu[h][lo:hi, 0:HEAD_DIM], qd[h][lo:hi, :]], axis=0).astype(BF16)
            egl_s[wr, h, 8 * c:8 * c + 8, :] = jnp.broadcast_to(jnp.exp(gl_rows[h][c]), (8, HEAD_DIM))
        u_s[wr, h] = wu[h][:, HEAD_DIM:2 * HEAD_DIM]
        qk_s[wr, h] = qk[h].astype(BF16)
        kdt_s[wr, h] = kdt[h].astype(BF16)


def _deltanet(dn_qkv, z, ba, alog_row, dtb_row, normw_row):
    bsz, seq, _ = dn_qkv.shape
    tt = DN_TILE
    nt = seq // tt
    hb = DN_HEADS_PER_STEP
    nbat = DN_SEQS_PER_STEP if bsz % DN_SEQS_PER_STEP == 0 else 1
    nv = nbat * hb
    ng = N_HEADS_DN // hb
    w = hb * HEAD_DIM
    nchunk = tt // DN_CHUNK

    def cur_spec(off, width):
        return pl.BlockSpec((nbat, tt, width), lambda b, g, t: (b, jnp.minimum(t, nt - 1), g + off))

    prev_spec = pl.BlockSpec((nbat, tt, w), lambda b, g, t: (b, jnp.maximum(t - 1, 0), g))
    row_spec = pl.BlockSpec((1, LANES), lambda b, g, t: (0, 0))
    return pl.pallas_call(
        functools.partial(_deltanet_kernel, hb=hb),
        out_shape=jax.ShapeDtypeStruct((bsz, seq, D_DN), BF16),
        grid=(bsz // nbat, ng, nt + 1),
        in_specs=[cur_spec(0, w), cur_spec(ng, w), cur_spec(2 * ng, w), prev_spec,
                  pl.BlockSpec((nbat, tt, LANES), lambda b, g, t: (b, jnp.minimum(t, nt - 1), 0)),
                  row_spec, row_spec, row_spec],
        out_specs=prev_spec,
        scratch_shapes=[pltpu.VMEM((nv, HEAD_DIM, HEAD_DIM), F32),
                        pltpu.VMEM((2, nv, nchunk, 2 * DN_CHUNK, HEAD_DIM), BF16),
                        pltpu.VMEM((2, nv, tt, HEAD_DIM), F32),
                        pltpu.VMEM((2, nv, tt, tt), BF16),
                        pltpu.VMEM((2, nv, HEAD_DIM, tt), BF16),
                        pltpu.VMEM((2, nv, 8 * nchunk, HEAD_DIM), F32)],
        compiler_params=pltpu.CompilerParams(dimension_semantics=("parallel", "parallel", "arbitrary"),
                                             vmem_limit_bytes=VMEM_LIMIT),
        name="deltanet",
    )(dn_qkv, dn_qkv, dn_qkv, z, ba, alog_row, dtb_row, normw_row)


def _moba_attn_kernel(q_ref, k_ref, vt_ref, sel_ref, o_ref, acc_ref, *, nb):
    j = pl.program_id(1)
    blk = MOBA_BLOCK
    nh = N_HEADS_MOBA
    dn = (((1,), (1,)), ((), ()))
    hsl = [slice(h * HEAD_DIM, (h + 1) * HEAD_DIM) for h in range(nh)]
    qs = [q_ref[0, :, hsl[h]] for h in range(nh)]

    ki = lax.broadcasted_iota(jnp.int32, (blk, blk), 0)
    qi = lax.broadcasted_iota(jnp.int32, (blk, blk), 1)
    own = pl.ds(pl.multiple_of(j * blk, blk), blk)
    s_own = [jnp.where(ki <= qi, lax.dot_general(k_ref[0, own, hsl[h]], qs[h], dn, preferred_element_type=F32),
                       NEG_INF) for h in range(nh)]

    def scores(n, h):
        kn = k_ref[0, pl.ds(pl.multiple_of(n * blk, blk), blk), hsl[h]]
        s = lax.dot_general(kn, qs[h], dn, preferred_element_type=F32)
        return jnp.where(sel_ref[0, 0, pl.ds(h * nb + n, 1), :] > 0.5, s, NEG_INF)

    ones8 = jnp.ones((8, blk), BF16)

    def softmax_pv(n, h, s, m, l):
        m_new = jnp.maximum(m, jnp.max(s, axis=0, keepdims=True))
        pb = jnp.exp2(s - m_new).astype(BF16)
        psum = jnp.dot(ones8, pb, preferred_element_type=F32)[0:1, :]
        pv = jnp.dot(vt_ref[0, n, hsl[h], :], pb, preferred_element_type=F32)
        return m_new, psum, pv

    ms, ls = [], []
    for h in range(nh):
        m, psum, pv = softmax_pv(j, h, s_own[h], jnp.full((1, blk), NEG_INF, F32), None)
        ms.append(m)
        ls.append(psum)
        acc_ref[h] = pv

    def make_body(unroll, first):
        def body(step, carry):
            ms, ls = (list(t) for t in carry)
            n0 = first + unroll * step
            cur = [scores(n0, h) for h in range(nh)]
            for i in range(unroll):
                nxt = []
                for h in range(nh):
                    if i + 1 < unroll:
                        nxt.append(scores(n0 + i + 1, h))
                    m_new, psum, pv = softmax_pv(n0 + i, h, cur[h], ms[h], ls[h])
                    alpha = jnp.exp2(ms[h] - m_new)
                    ls[h] = alpha * ls[h] + psum
                    ms[h] = m_new
                    acc_ref[h] = acc_ref[h] * alpha + pv
                cur = nxt
            return tuple(ms), tuple(ls)
        return body

    half = MOBA_UNROLL // 2
    nfull = (j + 1) // MOBA_UNROLL
    nhalf = (j - nfull * MOBA_UNROLL + half - 1) // half
    carry = lax.fori_loop(0, nfull, make_body(MOBA_UNROLL, 0), (tuple(ms), tuple(ls)))
    ms, ls = lax.fori_loop(0, jnp.maximum(nhalf, 0), make_body(half, nfull * MOBA_UNROLL), carry)
    o_ref[0] = jnp.concatenate([jnp.transpose(acc_ref[h] / ls[h]) for h in range(nh)],
                               axis=1).astype(o_ref.dtype)


def _moba_attn(q_r, k_r, vt, sel):
    bsz, seq, _ = q_r.shape
    nb = seq // MOBA_BLOCK
    assert nb % MOBA_UNROLL == 0, (nb, MOBA_UNROLL)
    tok_spec = pl.BlockSpec((1, MOBA_BLOCK, D_MOBA), lambda b, j: (b, j, 0))
    return pl.pallas_call(
        functools.partial(_moba_attn_kernel, nb=nb),
        out_shape=jax.ShapeDtypeStruct((bsz, seq, D_MOBA), BF16),
        grid=(bsz, nb),
        in_specs=[tok_spec,
                  pl.BlockSpec((1, seq, D_MOBA), lambda b, j: (b, 0, 0)),
                  pl.BlockSpec((1, nb, D_MOBA, MOBA_BLOCK), lambda b, j: (b, 0, 0, 0)),
                  pl.BlockSpec((1, 1, N_HEADS_MOBA * nb, MOBA_BLOCK), lambda b, j: (b, j, 0, 0))],
        out_specs=tok_spec,
        scratch_shapes=[pltpu.VMEM((N_HEADS_MOBA, HEAD_DIM, MOBA_BLOCK), F32)],
        compiler_params=pltpu.CompilerParams(dimension_semantics=("parallel", "arbitrary"),
                                             vmem_limit_bytes=VMEM_LIMIT),
        name="moba_attn",
    )(q_r, k_r, vt, sel)


def _route_record(logits):
    lane = lax.broadcasted_iota(jnp.int32, logits.shape, 1)
    big = jnp.int32(LANES)

    def first_lane(mask):
        return jnp.min(jnp.where(mask, lane, big), axis=1, keepdims=True)

    is_g = lane < N_GROUPS
    m1 = jnp.max(jnp.where(is_g, logits, NEG_INF), axis=1, keepdims=True)
    s1 = jnp.sum(jnp.where(is_g, jnp.exp(logits - m1), 0.0), axis=1, keepdims=True)
    pg = 1.0 / s1
    gsel = first_lane(is_g & (logits == m1))

    in_grp = (lane >= GATE_LANE0) & (((lane - GATE_LANE0) >> 2) == gsel) & (lane < GATE_LANE0 + N_EXPERTS)
    m2 = jnp.max(jnp.where(in_grp, logits, NEG_INF), axis=1, keepdims=True)
    s2 = jnp.sum(jnp.where(in_grp, jnp.exp(logits - m2), 0.0), axis=1, keepdims=True)
    e1 = first_lane(in_grp & (logits == m2))
    rest = in_grp & (lane != e1)
    m2b = jnp.max(jnp.where(rest, logits, NEG_INF), axis=1, keepdims=True)
    e2 = first_lane(rest & (logits == m2b))
    pe1 = 1.0 / s2
    pe2 = jnp.exp(m2b - m2) / s2
    tot = pe1 + pe2
    w1 = pg * (pe1 / tot)
    w2 = pg * (pe2 / tot)
    first_lo = e1 < e2
    lo = jnp.minimum(e1, e2)
    hi = jnp.maximum(e1, e2)
    a = (lo - GATE_LANE0) & (EXPERTS_PER_GROUP - 1)
    b = (hi - GATE_LANE0) & (EXPERTS_PER_GROUP - 1)
    bucket = gsel * N_PAIRS + ((a * (2 * EXPERTS_PER_GROUP - 1 - a)) >> 1) + (b - a - 1)
    record = jnp.where(lane == ROUTE_BUCKET, bucket.astype(F32),
                       jnp.where(lane == ROUTE_W_LO, jnp.where(first_lo, w1, w2),
                                 jnp.where(lane == ROUTE_W_HI, jnp.where(first_lo, w2, w1), 0.0)))
    return record, jnp.sum(jnp.where(lane == bucket, 1.0, 0.0), axis=0, keepdims=True)


def _mix_route_kernel(ydn_ref, ymb_ref, x_ref, wo_ref, g_ref, b_ref, rw_ref, rb_ref, h_ref, hb_ref, route_ref,
                      cnt_ref):
    nsub = cnt_ref.shape[0]
    rows = [slice(i * MOE_TS, (i + 1) * MOE_TS) for i in range(nsub)]
    wh, wl = _split2(rw_ref[...])
    wcat = jnp.concatenate([wh, wl], axis=1)

    mixes = [jnp.dot(jnp.concatenate([ydn_ref[r, :], ymb_ref[r, :]], axis=1), wo_ref[...],
                     preferred_element_type=F32) for r in rows]
    logits = []
    for r, mix in zip(rows, mixes):
        hval = _layer_norm(DEEPNORM_ALPHA * x_ref[r, :] + mix, g_ref[...], b_ref[...])
        h_ref[r, :] = hval
        hb_ref[r, :] = hval.astype(BF16)
        hh, hl = _split2(hval)
        both = jnp.dot(hh, wcat, preferred_element_type=F32)
        logits.append(both[:, 0:LANES] + both[:, LANES:2 * LANES]
                      + jnp.dot(hl, wh, preferred_element_type=F32) + rb_ref[...])
    for i, r in enumerate(rows):
        record, counts = _route_record(logits[i])
        route_ref[r, :] = record
        cnt_ref[i] = counts


def _mix_route(y_dn, y_mb, x2, wo, g1, b1, rw, rb, tm):
    n, d = x2.shape
    nsub = tm // MOE_TS
    row = lambda w: pl.BlockSpec((1, w), lambda i: (0, 0))
    return pl.pallas_call(
        _mix_route_kernel,
        out_shape=(jax.ShapeDtypeStruct((n, d), F32), jax.ShapeDtypeStruct((n, d), BF16),
                   jax.ShapeDtypeStruct((n, LANES), F32), jax.ShapeDtypeStruct((n // MOE_TS, 1, LANES), F32)),
        grid=(n // tm,),
        in_specs=[pl.BlockSpec((tm, D_DN), lambda i: (i, 0)), pl.BlockSpec((tm, D_MOBA), lambda i: (i, 0)),
                  pl.BlockSpec((tm, d), lambda i: (i, 0)), pl.BlockSpec((D_DN + D_MOBA, d), lambda i: (0, 0)),
                  row(d), row(d), pl.BlockSpec((d, LANES), lambda i: (0, 0)), row(LANES)],
        out_specs=(pl.BlockSpec((tm, d), lambda i: (i, 0)), pl.BlockSpec((tm, d), lambda i: (i, 0)),
                   pl.BlockSpec((tm, LANES), lambda i: (i, 0)), pl.BlockSpec((nsub, 1, LANES), lambda i: (i, 0, 0))),
        compiler_params=pltpu.CompilerParams(dimension_semantics=("parallel",), vmem_limit_bytes=VMEM_LIMIT),
        name="mix_route",
    )(y_dn, y_mb, x2, wo, g1, b1, rw, rb)


def _bucket_offsets_col(ohf):
    cnt = jnp.sum(ohf, axis=1, keepdims=True).astype(jnp.int32)
    pad = (((cnt + (GRAN - 1)) >> GRAN_SHIFT) << GRAN_SHIFT).astype(F32)
    r = lax.broadcasted_iota(jnp.int32, (LANES, LANES), 0)
    c = lax.broadcasted_iota(jnp.int32, (LANES, LANES), 1)
    before = jnp.where(c < r, 1.0, 0.0)
    return _dot(before, jnp.broadcast_to(pad, (LANES, LANES)))[:, 0:1]


def _moe_sort_kernel(gmap_ref, nvalid_ref, tail0_ref, taillen_ref, hb_ref, route_ref, lstrict_ref,
                     xg_ref, wsg_ref, xs_ref, ws_ref, zx_ref, zw_ref, sem):
    s = pl.program_id(0)
    nsteps = pl.num_programs(0)
    slot = s & 1
    ts = route_ref.shape[0]
    route = route_ref[...]
    rt = jnp.transpose(route)
    bucket_row = rt[ROUTE_BUCKET:ROUTE_BUCKET + 1, :].astype(jnp.int32)
    sub = lax.broadcasted_iota(jnp.int32, (LANES, ts), 0)
    ohf = jnp.where(sub == bucket_row, 1.0, 0.0)
    loff = _bucket_offsets_col(ohf)
    rank = lax.dot_general(ohf.astype(BF16), lstrict_ref[...], (((1,), (1,)), ((), ())),
                           preferred_element_type=F32)
    dest = jnp.sum(ohf * (loff + rank), axis=0, keepdims=True).astype(jnp.int32)
    rh, rl = _split2(route)
    wcat = jnp.concatenate([rh, rl], axis=1)
    rc = LROWS // PERM_CHUNKS
    for c in range(PERM_CHUNKS):
        rowi = lax.broadcasted_iota(jnp.int32, (rc, ts), 0) + c * rc
        perm = jnp.where(rowi == dest, 1.0, 0.0).astype(BF16)
        xs_ref[slot, c * rc:(c + 1) * rc, :] = jnp.dot(perm, hb_ref[...], preferred_element_type=F32).astype(BF16)
        wparts = jnp.dot(perm, wcat, preferred_element_type=F32)
        ws_ref[slot, c * rc:(c + 1) * rc, :] = wparts[:, 0:LANES] + wparts[:, LANES:2 * LANES]

    def copies(step, g):
        sl = step & 1
        src = pl.ds(pl.multiple_of(g * GRAN, GRAN), GRAN)
        dst = pl.ds(pl.multiple_of(gmap_ref[step * LGRAN + g] * GRAN, GRAN), GRAN)
        return (pltpu.make_async_copy(xs_ref.at[sl, src, :], xg_ref.at[dst, :], sem.at[0, sl]),
                pltpu.make_async_copy(ws_ref.at[sl, src, :], wsg_ref.at[dst, :], sem.at[1, sl]))

    def fill_copies(b, i):
        dst = pl.ds(pl.multiple_of((tail0_ref[b] + i) * GRAN, GRAN), GRAN)
        return (pltpu.make_async_copy(zx_ref.at[0:GRAN, :], xg_ref.at[dst, :], sem.at[2, 0]),
                pltpu.make_async_copy(zw_ref.at[0:GRAN, :], wsg_ref.at[dst, :], sem.at[2, 1]))

    def unused_tile_copies(t):
        dst = pl.ds(pl.multiple_of(t * MOE_TM, MOE_TM), MOE_TM)
        return (pltpu.make_async_copy(zx_ref, xg_ref.at[dst, :], sem.at[2, 0]),
                pltpu.make_async_copy(zw_ref, wsg_ref.at[dst, :], sem.at[2, 1]))

    def run(step, fn):
        def body(g, carry):
            for cp in copies(step, g):
                fn(cp)
            return carry
        lax.fori_loop(0, nvalid_ref[step], body, 0)

    def run_fill(fn):
        for b in range(N_BUCKETS):
            def body(i, carry, b=b):
                for cp in fill_copies(b, i):
                    fn(cp)
                return carry
            lax.fori_loop(0, taillen_ref[b], body, 0)

        def tile_body(t, carry):
            for cp in unused_tile_copies(t):
                fn(cp)
            return carry
        lax.fori_loop(tail0_ref[N_BUCKETS], xg_ref.shape[0] // MOE_TM, tile_body, 0)

    @pl.when(s == 0)
    def _():
        zx_ref[...] = jnp.zeros_like(zx_ref)
        zw_ref[...] = jnp.zeros_like(zw_ref)
        run_fill(lambda cp: cp.start())

    run(s, lambda cp: cp.start())

    @pl.when(s > 0)
    def _():
        run(s - 1, lambda cp: cp.wait())

    @pl.when(s == nsteps - 1)
    def _():
        run(s, lambda cp: cp.wait())
        run_fill(lambda cp: cp.wait())


def _moe_sort(plan, hb, route, lstrict):
    n, d = hb.shape
    ts = MOE_TS
    rows = plan["n_tiles"] * MOE_TM
    return pl.pallas_call(
        _moe_sort_kernel,
        out_shape=(jax.ShapeDtypeStruct((rows, d), BF16), jax.ShapeDtypeStruct((rows, LANES), F32)),
        grid_spec=pltpu.PrefetchScalarGridSpec(
            num_scalar_prefetch=4,
            grid=(n // ts,),
            in_specs=[pl.BlockSpec((ts, d), lambda s, *_: (s, 0)), pl.BlockSpec((ts, LANES), lambda s, *_: (s, 0)),
                      pl.BlockSpec((ts, ts), lambda s, *_: (0, 0))],
            out_specs=(pl.BlockSpec(memory_space=pl.ANY), pl.BlockSpec(memory_space=pl.ANY)),
            scratch_shapes=[pltpu.VMEM((2, LROWS, d), BF16), pltpu.VMEM((2, LROWS, LANES), F32),
                            pltpu.VMEM((MOE_TM, d), BF16), pltpu.VMEM((MOE_TM, LANES), F32),
                            pltpu.SemaphoreType.DMA((3, 2))]),
        compiler_params=pltpu.CompilerParams(dimension_semantics=("arbitrary",), vmem_limit_bytes=VMEM_LIMIT),
        name="moe_sort",
    )(plan["gmap"], plan["nvalid"], plan["tail0"], plan["taillen"], hb, route, lstrict)


def _moe_expert_kernel(xt_ref, elo_ref, ehi_ref, valid_ref, x_ref, w_ref, wg0, wu0, wd0, wg1, wu1, wd1, o_ref):
    t = pl.program_id(0)

    @pl.when(valid_ref[t] > 0)
    def _():
        x = x_ref[...]
        w = w_ref[...]
        gates = [jnp.dot(x, wg[0].astype(BF16), preferred_element_type=F32) for wg in (wg0, wg1)]
        ups = [jnp.dot(x, wu[0].astype(BF16), preferred_element_type=F32) for wu in (wu0, wu1)]
        hes = [(_silu(gates[i]) * ups[i] * w[:, lane:lane + 1]).astype(BF16)
               for i, lane in enumerate((ROUTE_W_LO, ROUTE_W_HI))]
        o_ref[...] = (jnp.dot(hes[0], wd0[0].astype(BF16), preferred_element_type=F32)
                      + jnp.dot(hes[1], wd1[0].astype(BF16), preferred_element_type=F32)).astype(o_ref.dtype)

    @pl.when(valid_ref[t] == 0)
    def _():
        o_ref[...] = jnp.zeros_like(o_ref)


def _moe_experts(plan, xg, wsg, wg, wu, wd):
    rows, d = xg.shape
    tm = MOE_TM
    tok = lambda width: pl.BlockSpec((tm, width), lambda t, xt, elo, ehi, valid: (xt[t], 0))
    lo3 = lambda shape: pl.BlockSpec(shape, lambda t, xt, elo, ehi, valid: (elo[t], 0, 0))
    hi3 = lambda shape: pl.BlockSpec(shape, lambda t, xt, elo, ehi, valid: (ehi[t], 0, 0))
    return pl.pallas_call(
        _moe_expert_kernel,
        out_shape=jax.ShapeDtypeStruct((rows, d), BF16),
        grid_spec=pltpu.PrefetchScalarGridSpec(
            num_scalar_prefetch=4,
            grid=(rows // tm,),
            in_specs=[tok(d), tok(LANES),
                      lo3((1, d, D_EXPERT)), lo3((1, d, D_EXPERT)), lo3((1, D_EXPERT, d)),
                      hi3((1, d, D_EXPERT)), hi3((1, d, D_EXPERT)), hi3((1, D_EXPERT, d))],
            out_specs=pl.BlockSpec((tm, d), lambda t, *_: (t, 0))),
        compiler_params=pltpu.CompilerParams(dimension_semantics=("arbitrary",), vmem_limit_bytes=VMEM_LIMIT),
        name="moe_experts",
    )(plan["xtile"], plan["elo"], plan["ehi"], plan["valid"], xg, wsg, wg, wu, wd, wg, wu, wd)


def _moe_unsort_kernel(gmap_ref, og_ref, route_ref, h_ref, lstrict_ref, g_ref, b_ref, out_ref, ol_ref, sem):
    s = pl.program_id(0)
    nsteps = pl.num_programs(0)
    slot = s & 1
    ts = route_ref.shape[0]

    def gather(step, fn):
        sl = step & 1

        def body(g, carry):
            src = pl.ds(pl.multiple_of(gmap_ref[step * LGRAN + g] * GRAN, GRAN), GRAN)
            dst = pl.ds(pl.multiple_of(g * GRAN, GRAN), GRAN)
            fn(pltpu.make_async_copy(og_ref.at[src, :], ol_ref.at[sl, dst, :], sem.at[sl]))
            return carry
        lax.fori_loop(0, LGRAN, body, 0)

    @pl.when(s == 0)
    def _():
        gather(s, lambda cp: cp.start())

    @pl.when(s + 1 < nsteps)
    def _():
        gather(s + 1, lambda cp: cp.start())

    route = route_ref[...]
    bucket_col = route[:, ROUTE_BUCKET:ROUTE_BUCKET + 1].astype(jnp.int32)
    lane = lax.broadcasted_iota(jnp.int32, (ts, LANES), 1)
    ohf = jnp.where(lane == bucket_col, 1.0, 0.0)
    cnt = jnp.sum(ohf, axis=0, keepdims=True).astype(jnp.int32)
    pad = (((cnt + (GRAN - 1)) >> GRAN_SHIFT) << GRAN_SHIFT).astype(F32)
    r = lax.broadcasted_iota(jnp.int32, (LANES, LANES), 0)
    c = lax.broadcasted_iota(jnp.int32, (LANES, LANES), 1)
    loff = _dot(jnp.broadcast_to(pad, (8, LANES)), jnp.where(r < c, 1.0, 0.0))[0:1, :]
    rank = jnp.dot(lstrict_ref[...], ohf.astype(BF16), preferred_element_type=F32)
    dest = jnp.sum(ohf * (loff + rank), axis=1, keepdims=True).astype(jnp.int32)
    gather(s, lambda cp: cp.wait())
    tc = ts // PERM_CHUNKS
    lrow = lax.broadcasted_iota(jnp.int32, (tc, LROWS), 1)
    for c in range(PERM_CHUNKS):
        rows = slice(c * tc, (c + 1) * tc)
        perm_t = jnp.where(lrow == dest[rows, :], 1.0, 0.0).astype(BF16)
        ffn = jnp.dot(perm_t, ol_ref[slot], preferred_element_type=F32)
        out_ref[rows, :] = _layer_norm(DEEPNORM_ALPHA * h_ref[rows, :] + ffn, g_ref[...], b_ref[...])


def _moe_unsort(plan, og, route, hf, lstrict, g2, b2):
    n, d = hf.shape
    ts = MOE_TS
    row = pl.BlockSpec((1, d), lambda s, *_: (0, 0))
    return pl.pallas_call(
        _moe_unsort_kernel,
        out_shape=jax.ShapeDtypeStruct((n, d), F32),
        grid_spec=pltpu.PrefetchScalarGridSpec(
            num_scalar_prefetch=1,
            grid=(n // ts,),
            in_specs=[pl.BlockSpec(memory_space=pl.ANY), pl.BlockSpec((ts, LANES), lambda s, *_: (s, 0)),
                      pl.BlockSpec((ts, d), lambda s, *_: (s, 0)), pl.BlockSpec((ts, ts), lambda s, *_: (0, 0)),
                      row, row],
            out_specs=pl.BlockSpec((ts, d), lambda s, *_: (s, 0)),
            scratch_shapes=[pltpu.VMEM((2, LROWS, d), BF16), pltpu.SemaphoreType.DMA((2,))]),
        compiler_params=pltpu.CompilerParams(dimension_semantics=("arbitrary",), vmem_limit_bytes=VMEM_LIMIT),
        name="moe_unsort",
    )(plan["gmap_back"], og, route, hf, lstrict, g2, b2)


def _moe_plan(cnt_half, n):
    nsrc = n // MOE_TS
    i32 = jnp.int32
    cnt = cnt_half.reshape(nsrc, -1, LANES).sum(axis=1)[:, :N_BUCKETS].astype(i32)
    run_g = (cnt + GRAN - 1) // GRAN
    nvalid = run_g.sum(axis=1)
    loff_g = jnp.cumsum(run_g, axis=1) - run_g
    bucket_g = run_g.sum(axis=0)
    gpt = MOE_TM // GRAN
    btiles = (bucket_g + gpt - 1) // gpt
    tend = jnp.cumsum(btiles)
    tstart = tend - btiles
    gofs = tstart[None, :] * gpt + jnp.cumsum(run_g, axis=0) - run_g
    n_tiles = -(-(n + nsrc * N_BUCKETS * (GRAN - 1)) // MOE_TM) + N_BUCKETS + 1
    g = jnp.arange(LGRAN, dtype=i32)[None, :, None]
    in_run = (g >= loff_g[:, None, :]) & (g < (loff_g + run_g)[:, None, :])
    gmap = jnp.arange(LGRAN, dtype=i32)[None, :] + jnp.sum(jnp.where(in_run, (gofs - loff_g)[:, None, :], 0), axis=2)
    is_valid = jnp.arange(LGRAN, dtype=i32)[None, :] < nvalid[:, None]
    zero_gran = (n_tiles - 1) * gpt
    t = jnp.arange(n_tiles, dtype=i32)
    tb = jnp.minimum(jnp.sum(t[:, None] >= tend[None, :], axis=1), N_BUCKETS - 1)
    valid = (t < tend[-1]).astype(i32)
    pairs = [(a, b) for a in range(EXPERTS_PER_GROUP) for b in range(a + 1, EXPERTS_PER_GROUP)]
    pidx = tb % N_PAIRS
    pair_a = sum(jnp.where(pidx == i, a, 0) for i, (a, _) in enumerate(pairs))
    pair_b = sum(jnp.where(pidx == i, b, 0) for i, (_, b) in enumerate(pairs))
    grp = tb // N_PAIRS
    return {
        "n_tiles": n_tiles,
        "gmap": jnp.where(is_valid, gmap, 0).reshape(-1).astype(i32),
        "gmap_back": jnp.where(is_valid, gmap, zero_gran).reshape(-1).astype(i32),
        "nvalid": nvalid.astype(i32),
        "tail0": jnp.concatenate([tstart * gpt + bucket_g, tend[-1:]]).astype(i32),
        "taillen": (btiles * gpt - bucket_g).astype(i32),
        "xtile": jnp.where(valid > 0, t, 0).astype(i32),
        "elo": (grp * EXPERTS_PER_GROUP + pair_a).astype(i32),
        "ehi": (grp * EXPERTS_PER_GROUP + pair_b).astype(i32),
        "valid": valid,
    }


def _pad_lanes(a, lane0=0):
    return jnp.zeros((1, LANES), F32).at[0, lane0:lane0 + a.shape[0]].set(a.astype(F32))


def _rope_tables(seq):
    half = HEAD_DIM // 2
    inv_freq = ROPE_THETA ** (-np.arange(half, dtype=np.float64) / half)
    ang = np.arange(seq, dtype=np.float64)[:, None] * inv_freq[None, :]
    cos, sin = np.cos(ang), np.sin(ang)
    return (jnp.asarray(np.concatenate([cos, cos], axis=-1), F32),
            jnp.asarray(np.concatenate([-sin, sin], axis=-1), F32))


def _layer(x, w_in, conv_w, a_log, dt_bias, dn_norm_w, w_out, ln1_g, ln1_b, router_w1, router_b1,
           router_w2, router_b2, w_gate, w_up, w_down, ln2_g, ln2_b):
    bsz, seq, d = x.shape
    n = bsz * seq
    x2 = x.reshape(n, d)

    w_all = _w_prep(w_in)

    cos_t, sin_t = _rope_tables(seq)
    nb = seq // MOBA_BLOCK
    dn_qkv, z, ba, q_r, k_r, vt, sel = _in_proj(x2, w_all, conv_w, cos_t, sin_t, min(IN_PROJ_TM, seq), seq)

    y_dn = _deltanet(dn_qkv.reshape(bsz, seq, 3 * D_DN), z.reshape(bsz, seq, D_DN), ba.reshape(bsz, seq, LANES),
                     _pad_lanes(a_log, N_HEADS_DN), _pad_lanes(dt_bias, N_HEADS_DN),
                     dn_norm_w.astype(F32).reshape(1, HEAD_DIM))

    q_r, k_r = q_r.reshape(bsz, seq, D_MOBA), k_r.reshape(bsz, seq, D_MOBA)
    vt = vt.reshape(bsz, nb, D_MOBA, MOBA_BLOCK)
    sel = sel.reshape(bsz, nb, N_HEADS_MOBA * nb, MOBA_BLOCK)
    y_mb = _moba_attn(q_r, k_r, vt, sel)

    rw = jnp.concatenate([router_w1, jnp.transpose(router_w2, (1, 0, 2)).reshape(d, N_EXPERTS)], axis=1)
    rw = jnp.pad(rw, ((0, 0), (0, LANES - rw.shape[1])))
    rb = _pad_lanes(jnp.concatenate([router_b1, router_b2.reshape(-1)]))
    hf, hb, route, cnt = _mix_route(y_dn.reshape(n, D_DN), y_mb.reshape(n, D_MOBA), x2, w_out.astype(BF16),
                                    ln1_g.reshape(1, d), ln1_b.reshape(1, d), rw, rb, min(2 * MOE_TS, n))

    plan = _moe_plan(cnt, n)
    idx = jnp.arange(MOE_TS, dtype=jnp.int32)
    lstrict = (idx[None, :] < idx[:, None]).astype(BF16)
    xg, wsg = _moe_sort(plan, hb, route, lstrict)
    og = _moe_experts(plan, xg, wsg, w_gate, w_up, w_down)
    out = _moe_unsort(plan, og, route, hf, lstrict, ln2_g.reshape(1, d), ln2_b.reshape(1, d))
    return out.reshape(bsz, seq, d)


def kernel(x, w_in, conv_w, a_log, dt_bias, dn_norm_w, w_out, ln1_g, ln1_b, router_w1, router_b1, router_w2, router_b2, expert_w_gate, expert_w_up, expert_w_down, ln2_g, ln2_b):
    for l in range(DEPTH):
        x = _layer(x, w_in[l], conv_w[l], a_log[l], dt_bias[l], dn_norm_w[l], w_out[l], ln1_g[l], ln1_b[l],
                   router_w1[l], router_b1[l], router_w2[l], router_b2[l], expert_w_gate[l], expert_w_up[l],
                   expert_w_down[l], ln2_g[l], ln2_b[l])
    return x
```

```python
import functools

import numpy as np
import jax
import jax.numpy as jnp
from jax import lax
from jax.experimental import pallas as pl
from jax.experimental.pallas import tpu as pltpu

F32 = jnp.float32
BF16 = jnp.bfloat16

HEAD_DIM = 128
N_HEADS_DN = 4
N_HEADS_MOBA = 4
D_DN = N_HEADS_DN * HEAD_DIM
D_MOBA = N_HEADS_MOBA * HEAD_DIM
CONV_K = 4
DN_CHUNK = 64
MOBA_BLOCK = 256
MOBA_TOPK = 3
MOBA_UNROLL = 4
ROPE_THETA = 10000.0
N_GROUPS = 4
EXPERTS_PER_GROUP = 4
N_EXPERTS = N_GROUPS * EXPERTS_PER_GROUP
D_EXPERT = 256
LN_EPS = 1e-5
RMS_EPS = 1e-6
L2_EPS = 1e-6
NEG_INF = -1e30
LOG2E = 1.4426950408889634
DEPTH = 1
DEEPNORM_ALPHA = (2 * DEPTH) ** 0.25

LANES = 128
IN_PROJ_TM = 512
IN_PROJ_GROUP = 512
DN_TILE = 256
DN_HEADS_PER_STEP = 4
DN_SEQS_PER_STEP = 2
GATE_LANE0 = N_GROUPS
N_PAIRS = EXPERTS_PER_GROUP * (EXPERTS_PER_GROUP - 1) // 2
N_BUCKETS = N_GROUPS * N_PAIRS
ROUTE_BUCKET, ROUTE_W_LO, ROUTE_W_HI = 0, 1, 2
MOE_TS = 512
MOE_TM = 512
GRAN = 16
GRAN_SHIFT = 4
LROWS = -(-(MOE_TS + N_BUCKETS * (GRAN - 1)) // LANES) * LANES
LGRAN = LROWS // GRAN
PERM_CHUNKS = 4
VMEM_LIMIT = 48 * 1024 * 1024


def _dot(a, b):
    return jnp.dot(a.astype(BF16), b.astype(BF16), preferred_element_type=F32)


def _dot_nt(a, b):
    return lax.dot_general(a.astype(BF16), b.astype(BF16), (((1,), (1,)), ((), ())),
                           preferred_element_type=F32)


def _split2(a):
    hi = a.astype(BF16)
    lo = (a - hi.astype(F32)).astype(BF16)
    return hi, lo


def _split3(a):
    hi = a.astype(BF16)
    r = a - hi.astype(F32)
    mid = r.astype(BF16)
    lo = (r - mid.astype(F32)).astype(BF16)
    return hi, mid, lo


def _dot3_nt(a, b):
    ah, al = _split2(a)
    bh, bl = _split2(b)
    dn = (((1,), (1,)), ((), ()))
    return (lax.dot_general(ah, bh, dn, preferred_element_type=F32)
            + lax.dot_general(ah, bl, dn, preferred_element_type=F32)
            + lax.dot_general(al, bh, dn, preferred_element_type=F32))


def _dot_exact_lhs(a_bf16, b):
    bh, bm, bl = _split3(b)
    return (jnp.dot(a_bf16, bh, preferred_element_type=F32) + jnp.dot(a_bf16, bm, preferred_element_type=F32)
            + jnp.dot(a_bf16, bl, preferred_element_type=F32))


def _silu(x):
    return x * jax.nn.sigmoid(x)


def _softplus(x):
    return jnp.maximum(x, 0.0) + jnp.log1p(jnp.exp(-jnp.abs(x)))


def _layer_norm(t, g, b):
    mu = jnp.mean(t, axis=-1, keepdims=True)
    d = t - mu
    var = jnp.mean(d * d, axis=-1, keepdims=True)
    return d * lax.rsqrt(var + LN_EPS) * g + b


def _lane_pick(x, lane):
    ids = lax.broadcasted_iota(jnp.int32, x.shape, 1)
    return jnp.sum(jnp.where(ids == lane, x, 0.0), axis=1, keepdims=True)


def _w_prep_kernel(w_ref, o_ref):
    o_b = 4 * D_DN
    o_mb = o_b + 2 * N_HEADS_DN
    o_ref[:, 0:o_b] = w_ref[:, 0:o_b].astype(o_ref.dtype)
    lane = lax.broadcasted_iota(jnp.int32, (w_ref.shape[0], LANES), 1)
    o_ref[:, o_b:o_b + LANES] = jnp.where(lane < 2 * N_HEADS_DN, w_ref[:, o_b:o_b + LANES], 0.0).astype(o_ref.dtype)
    o_ref[:, o_b + LANES:] = w_ref[:, o_mb:].astype(o_ref.dtype)


def _w_prep(w_in):
    d, cols = w_in.shape
    out_cols = cols - 2 * N_HEADS_DN + LANES
    rows = 256
    return pl.pallas_call(
        _w_prep_kernel,
        out_shape=jax.ShapeDtypeStruct((d, out_cols), BF16),
        grid=(d // rows,),
        in_specs=[pl.BlockSpec((rows, cols), lambda i: (i, 0))],
        out_specs=pl.BlockSpec((rows, out_cols), lambda i: (i, 0)),
        compiler_params=pltpu.CompilerParams(dimension_semantics=("parallel",), vmem_limit_bytes=VMEM_LIMIT),
        name="w_prep",
    )(w_in)


def _in_proj_kernel(x_ref, w_ref, cw_ref, cos_ref, sin_ref, dn_ref, z_ref, ba_ref, q_ref, k_ref, vt_ref, sel_ref,
                    cb_ref, km_ref, *, tiles_per_seq, nb, topk):
    i = pl.program_id(0)
    tm = x_ref.shape[0]
    o0 = 3 * D_DN
    o1 = o0 + D_DN
    o2 = o1 + LANES
    seq_tile = i % tiles_per_seq
    spb = tm // MOBA_BLOCK

    @pl.when(seq_tile == 0)
    def _():
        cb_ref[0:8, :] = jnp.zeros((8, o0), F32)
        km_ref[...] = jnp.zeros_like(km_ref)

    half = HEAD_DIM // 2
    heads = [slice(h * HEAD_DIM, (h + 1) * HEAD_DIM) for h in range(N_HEADS_MOBA)]
    rows = [slice(sb * MOBA_BLOCK, (sb + 1) * MOBA_BLOCK) for sb in range(spb)]

    def rope(t, rs):
        return t * cos_ref[rs, :] + pltpu.roll(t, half, 1) * sin_ref[rs, :]

    xb = x_ref[...].astype(BF16)
    gw = IN_PROJ_GROUP
    qr = None
    for grp in range(3 * D_DN // gw):
        cs = slice(grp * gw, (grp + 1) * gw)
        u = jnp.dot(xb, w_ref[:, cs], preferred_element_type=F32)
        mb = jnp.dot(xb, w_ref[:, o2 + grp * gw:o2 + (grp + 1) * gw], preferred_element_type=F32)
        cb_ref[8:8 + tm, cs] = u
        acc = cw_ref[CONV_K - 1:CONV_K, cs] * u
        for s in range(1, CONV_K):
            acc = acc + cw_ref[CONV_K - 1 - s:CONV_K - s, cs] * cb_ref[8 - s:8 - s + tm, cs]
        cb_ref[0:8, cs] = u[tm - 8:tm, :]
        act = _silu(acc)
        if grp * gw < 2 * D_DN:
            outs = []
            for h in range(gw // HEAD_DIM):
                t = act[:, h * HEAD_DIM:(h + 1) * HEAD_DIM]
                t = t * lax.rsqrt(jnp.sum(t * t, axis=-1, keepdims=True) + L2_EPS)
                outs.append(t * (HEAD_DIM ** -0.5) if grp * gw < D_DN else t)
            act = jnp.concatenate(outs, axis=1)
        dn_ref[:, cs] = act

        if grp == 0:
            qr = [[rope(mb[rs, hs], rs) * (HEAD_DIM ** -0.5) for hs in heads] for rs in rows]
            for sb, rs in enumerate(rows):
                q_ref[rs, :] = jnp.concatenate([q * LOG2E for q in qr[sb]], axis=1).astype(q_ref.dtype)
        elif grp == 1:
            blk = lax.broadcasted_iota(jnp.int32, (nb, MOBA_BLOCK), 0)
            kmrow = lax.broadcasted_iota(jnp.int32, (nb, D_MOBA), 0)
            km = km_ref[...]
            for sb, rs in enumerate(rows):
                j = seq_tile * spb + sb
                kr = [rope(mb[rs, hs], rs) for hs in heads]
                k_ref[rs, :] = jnp.concatenate(kr, axis=1).astype(k_ref.dtype)
                for h, hs in enumerate(heads):
                    gate = _dot3_nt(km[:, hs], qr[sb][h])
                    gate = jnp.where(blk < j, gate, NEG_INF)
                    rank = jnp.zeros(gate.shape, F32)
                    for m in range(nb):
                        gm = gate[m:m + 1, :]
                        ahead = (gm > gate) | ((gm == gate) & (blk > m))
                        rank = rank + jnp.where(ahead, 1.0, 0.0)
                    sel = (blk < j) & (rank < topk)
                    sel_ref[sb, h * nb:(h + 1) * nb, :] = jnp.where(sel, 1.0, 0.0)
                kmean = jnp.concatenate([jnp.mean(t, axis=0, keepdims=True) for t in kr], axis=1)
                km = jnp.where(kmrow == j, kmean, km)
            km_ref[...] = km
        else:
            for sb, rs in enumerate(rows):
                vt_ref[sb] = jnp.transpose(mb[rs, :]).astype(vt_ref.dtype)
    z_ref[...] = _silu(jnp.dot(xb, w_ref[:, o0:o1], preferred_element_type=F32))
    ba_ref[...] = jnp.dot(xb, w_ref[:, o1:o2], preferred_element_type=F32)


def _in_proj(x2, w_all, conv_w, cos_t, sin_t, tm, seq):
    n, d = x2.shape
    wc = w_all.shape[1]
    nb = seq // MOBA_BLOCK
    spb = tm // MOBA_BLOCK
    tps = seq // tm
    kern = functools.partial(_in_proj_kernel, tiles_per_seq=tps, nb=nb, topk=min(MOBA_TOPK, nb))
    tok = lambda width: pl.BlockSpec((tm, width), lambda i: (i, 0))
    tab = pl.BlockSpec((tm, HEAD_DIM), lambda i: (i % tps, 0))
    return pl.pallas_call(
        kern,
        out_shape=(jax.ShapeDtypeStruct((n, 3 * D_DN), F32), jax.ShapeDtypeStruct((n, D_DN), F32),
                   jax.ShapeDtypeStruct((n, LANES), F32),
                   jax.ShapeDtypeStruct((n, D_MOBA), BF16), jax.ShapeDtypeStruct((n, D_MOBA), BF16),
                   jax.ShapeDtypeStruct((n // MOBA_BLOCK, D_MOBA, MOBA_BLOCK), BF16),
                   jax.ShapeDtypeStruct((n // MOBA_BLOCK, N_HEADS_MOBA * nb, MOBA_BLOCK), F32)),
        grid=(n // tm,),
        in_specs=[tok(d), pl.BlockSpec((d, wc), lambda i: (0, 0)),
                  pl.BlockSpec((CONV_K, 3 * D_DN), lambda i: (0, 0)), tab, tab],
        out_specs=(tok(3 * D_DN), tok(D_DN), tok(LANES), tok(D_MOBA), tok(D_MOBA),
                   pl.BlockSpec((spb, D_MOBA, MOBA_BLOCK), lambda i: (i, 0, 0)),
                   pl.BlockSpec((spb, N_HEADS_MOBA * nb, MOBA_BLOCK), lambda i: (i, 0, 0))),
        scratch_shapes=[pltpu.VMEM((8 + tm, 3 * D_DN), F32), pltpu.VMEM((nb, D_MOBA), F32)],
        compiler_params=pltpu.CompilerParams(dimension_semantics=("arbitrary",), vmem_limit_bytes=VMEM_LIMIT),
        name="in_proj",
    )(x2, w_all, conv_w, cos_t, sin_t)


def _deltanet_kernel(q_ref, k_ref, v_ref, z_ref, ba_ref, alog_ref, dtb_ref, normw_ref, y_ref,
                     s_ref, wq_s, u_s, qk_s, kdt_s, egl_s, *, hb):
    hg = pl.program_id(1)
    t = pl.program_id(2)
    tt = DN_TILE
    nchunk = tt // DN_CHUNK
    nbat = q_ref.shape[0]
    hs = range(nbat * hb)

    @pl.when(t == 0)
    def _():
        for ref in (s_ref, wq_s, u_s, qk_s, kdt_s, egl_s):
            ref[...] = jnp.zeros_like(ref)

    rd = t & 1
    wr = 1 - rd
    state = [s_ref[h] for h in hs]
    outs = [[] for _ in hs]

    pend = {}

    def chain_a(c):
        pend["r"] = [jnp.dot(wq_s[rd, h, c], state[h].astype(BF16), preferred_element_type=F32) for h in hs]

    def chain_b(c):
        lo, hi = c * DN_CHUNK, (c + 1) * DN_CHUNK
        r = pend["r"]
        vz = []
        for h in hs:
            parts = []
            if lo > 0:
                parts.append(jnp.zeros((lo, HEAD_DIM), F32))
            parts.append(u_s[rd, h, lo:hi, :] - r[h][0:DN_CHUNK, :])
            if hi < tt:
                parts.append(jnp.zeros((tt - hi, HEAD_DIM), F32))
            vz.append(jnp.concatenate(parts, axis=0).astype(BF16))
        for h in hs:
            outs[h].append(r[h][DN_CHUNK:2 * DN_CHUNK, :]
                           + jnp.dot(qk_s[rd, h, lo:hi, :], vz[h], preferred_element_type=F32))
        for h in hs:
            state[h] = (state[h] * egl_s[rd, h, 8 * c:8 * c + 1, :]
                        + jnp.dot(kdt_s[rd, h], vz[h], preferred_element_type=F32))

    chain_a(0)

    bas = [ba_ref[bb] for bb in range(nbat)]
    beta_all = [jax.nn.sigmoid(ba) for ba in bas]
    g_all = [-jnp.exp(alog_ref[...]) * _softplus(ba + dtb_ref[...]) for ba in bas]

    row = lax.broadcasted_iota(jnp.int32, (tt, tt), 0)
    col = lax.broadcasted_iota(jnp.int32, (tt, tt), 1)
    same = (row >> 6) == (col >> 6)
    incl = same & (row >= col)
    strict = same & (row > col)

    incl_b = incl.astype(BF16)
    gc_all = [_dot_exact_lhs(incl_b, g) for g in g_all]
    gct = [jnp.transpose(g) for g in gc_all]
    sub = lax.broadcasted_iota(jnp.int32, gct[0].shape, 0)
    bat = [vh // hb for vh in hs]
    sls = [slice((vh % hb) * HEAD_DIM, (vh % hb + 1) * HEAD_DIM) for vh in hs]
    heads = [hg * hb + vh % hb for vh in hs]
    q = [q_ref[bat[vh], :, sls[vh]] for vh in hs]
    k = [k_ref[bat[vh], :, sls[vh]] for vh in hs]
    v = [v_ref[bat[vh], :, sls[vh]] for vh in hs]
    beta = [_lane_pick(beta_all[bat[vh]], heads[vh]) for vh in hs]
    gcc = [_lane_pick(gc_all[bat[vh]], heads[vh] + N_HEADS_DN) for vh in hs]
    gcr = [jnp.sum(jnp.where(sub == heads[vh] + N_HEADS_DN, gct[bat[vh]], 0.0), axis=0, keepdims=True)
           for vh in hs]
    chain_b(0)

    decay = [jnp.where(incl, jnp.exp(jnp.where(incl, gcc[h] - gcr[h], 0.0)), 0.0) for h in hs]
    kb = [k[h] * beta[h] for h in hs]
    vb = [v[h] * beta[h] for h in hs]
    a_mat = [jnp.where(strict, _dot_nt(kb[h], k[h]) * decay[h], 0.0) for h in hs]
    chain_a(1)
    qk = [_dot_nt(q[h], k[h]) * decay[h] for h in hs]
    eye = (row == col).astype(F32)
    d8 = (row >> 3) == (col >> 3)
    a8 = [jnp.where(d8, a, 0.0) for a in a_mat]
    chain_b(1)
    a8_2 = [_dot(a, a) for a in a8]
    chain_a(2)
    a8_4 = [_dot(a, a) for a in a8_2]
    chain_b(2)
    x = [_dot(eye - a, eye + a2) for a, a2 in zip(a8, a8_2)]
    chain_a(3)
    x = [_dot(xi, eye + a4) for xi, a4 in zip(x, a8_4)]
    chain_b(3)
    s = 8
    while s < DN_CHUNK:
        sh = s.bit_length() - 1
        nblk = tt // (2 * s)
        second = lambda m: jnp.concatenate([m[b * 2 * s + s:(b + 1) * 2 * s, :] for b in range(nblk)], axis=0)
        off = second(((row >> (sh + 1)) == (col >> (sh + 1))) & ((row >> sh) != (col >> sh)))
        y_half = [_dot(jnp.where(off, second(a), 0.0), xi) for a, xi in zip(a_mat, x)]
        zeros = jnp.zeros((s, tt), F32)
        y = [jnp.concatenate([p for b in range(nblk) for p in (zeros, yh[b * s:(b + 1) * s, :])], axis=0)
             for yh in y_half]
        upd = [_dot(second(xi), yi) for xi, yi in zip(x, y)]
        x = [jnp.concatenate([p for b in range(nblk)
                              for p in (xi[b * 2 * s:b * 2 * s + s, :],
                                        xi[b * 2 * s + s:(b + 1) * 2 * s, :] - ud[b * s:(b + 1) * s, :])], axis=0)
             for xi, ud in zip(x, upd)]
        s *= 2
    tinv = x
    eg = [jnp.exp(g) for g in gcc]
    wu = [_dot(tinv[h], jnp.concatenate([kb[h] * eg[h], vb[h]], axis=1)) for h in hs]
    qd = [q[h] * eg[h] for h in hs]
    gl_rows = [[g[(c + 1) * DN_CHUNK - 1:(c + 1) * DN_CHUNK, :] for c in range(nchunk)] for g in gcc]
    gl_col = [jnp.concatenate([jnp.broadcast_to(g, (DN_CHUNK, 1)) for g in rows], axis=0) for rows in gl_rows]
    kdt = [jnp.transpose(k[h] * jnp.exp(gl_col[h] - gcc[h])) for h in hs]

    ys = []
    for h in hs:
        o = jnp.concatenate(outs[h], axis=0)
        o = o * lax.rsqrt(jnp.mean(o * o, axis=-1, keepdims=True) + RMS_EPS) * normw_ref[...]
        ys.append(o * z_ref[bat[h], :, sls[h]])
    for bb in range(nbat):
        y_ref[bb] = jnp.concatenate(ys[bb * hb:(bb + 1) * hb], axis=1).astype(y_ref.dtype)
    s_ref[...] = jnp.stack(state, axis=0)

    for h in hs:
        for c in range(nchunk):
            lo, hi = c * DN_CHUNK, (c + 1) * DN_CHUNK
            wq_s[wr, h, c] = jnp.concatenate([wu[h][lo:hi, 0:HEAD_DIM], qd[h][lo:hi, :]], axis=0).astype(BF16)
            egl_s[wr, h, 8 * c:8 * c + 8, :] = jnp.broadcast_to(jnp.exp(gl_rows[h][c]), (8, HEAD_DIM))
        u_s[wr, h] = wu[h][:, HEAD_DIM:2 * HEAD_DIM]
        qk_s[wr, h] = qk[h].astype(BF16)
        kdt_s[wr, h] = kdt[h].astype(BF16)


def _deltanet(dn_qkv, z, ba, alog_row, dtb_row, normw_row):
    bsz, seq, _ = dn_qkv.shape
    tt = DN_TILE
    nt = seq // tt
    hb = DN_HEADS_PER_STEP
    nbat = DN_SEQS_PER_STEP if bsz % DN_SEQS_PER_STEP == 0 else 1
    nv = nbat * hb
    ng = N_HEADS_DN // hb
    w = hb * HEAD_DIM
    nchunk = tt // DN_CHUNK

    def cur_spec(off, width):
        return pl.BlockSpec((nbat, tt, width), lambda b, g, t: (b, jnp.minimum(t, nt - 1), g + off))

    prev_spec = pl.BlockSpec((nbat, tt, w), lambda b, g, t: (b, jnp.maximum(t - 1, 0), g))
    row_spec = pl.BlockSpec((1, LANES), lambda b, g, t: (0, 0))
    return pl.pallas_call(
        functools.partial(_deltanet_kernel, hb=hb),
        out_shape=jax.ShapeDtypeStruct((bsz, seq, D_DN), BF16),
        grid=(bsz // nbat, ng, nt + 1),
        in_specs=[cur_spec(0, w), cur_spec(ng, w), cur_spec(2 * ng, w), prev_spec,
                  pl.BlockSpec((nbat, tt, LANES), lambda b, g, t: (b, jnp.minimum(t, nt - 1), 0)),
                  row_spec, row_spec, row_spec],
        out_specs=prev_spec,
        scratch_shapes=[pltpu.VMEM((nv, HEAD_DIM, HEAD_DIM), F32),
                        pltpu.VMEM((2, nv, nchunk, 2 * DN_CHUNK, HEAD_DIM), BF16),
                        pltpu.VMEM((2, nv, tt, HEAD_DIM), F32),
                        pltpu.VMEM((2, nv, tt, tt), BF16),
                        pltpu.VMEM((2, nv, HEAD_DIM, tt), BF16),
                        pltpu.VMEM((2, nv, 8 * nchunk, HEAD_DIM), F32)],
        compiler_params=pltpu.CompilerParams(dimension_semantics=("parallel", "parallel", "arbitrary"),
                                             vmem_limit_bytes=VMEM_LIMIT),
        name="deltanet",
    )(dn_qkv, dn_qkv, dn_qkv, z, ba, alog_row, dtb_row, normw_row)


def _moba_attn_kernel(q_ref, k_ref, vt_ref, sel_ref, o_ref, acc_ref, *, nb):
    j = pl.program_id(1)
    blk = MOBA_BLOCK
    nh = N_HEADS_MOBA
    dn = (((1,), (1,)), ((), ()))
    hsl = [slice(h * HEAD_DIM, (h + 1) * HEAD_DIM) for h in range(nh)]
    qs = [q_ref[0, :, hsl[h]] for h in range(nh)]

    ki = lax.broadcasted_iota(jnp.int32, (blk, blk), 0)
    qi = lax.broadcasted_iota(jnp.int32, (blk, blk), 1)
    own = pl.ds(pl.multiple_of(j * blk, blk), blk)
    s_own = [jnp.where(ki <= qi, lax.dot_general(k_ref[0, own, hsl[h]], qs[h], dn, preferred_element_type=F32),
                       NEG_INF) for h in range(nh)]

    def scores(n, h):
        kn = k_ref[0, pl.ds(pl.multiple_of(n * blk, blk), blk), hsl[h]]
        s = lax.dot_general(kn, qs[h], dn, preferred_element_type=F32)
        return jnp.where(sel_ref[0, 0, pl.ds(h * nb + n, 1), :] > 0.5, s, NEG_INF)

    ones8 = jnp.ones((8, blk), BF16)

    def softmax_pv(n, h, s, m, l):
        m_new = jnp.maximum(m, jnp.max(s, axis=0, keepdims=True))
        pb = jnp.exp2(s - m_new).astype(BF16)
        psum = jnp.dot(ones8, pb, preferred_element_type=F32)[0:1, :]
        pv = jnp.dot(vt_ref[0, n, hsl[h], :], pb, preferred_element_type=F32)
        return m_new, psum, pv

    ms, ls = [], []
    for h in range(nh):
        m, psum, pv = softmax_pv(j, h, s_own[h], jnp.full((1, blk), NEG_INF, F32), None)
        ms.append(m)
        ls.append(psum)
        acc_ref[h] = pv

    def make_body(unroll, first):
        def body(step, carry):
            ms, ls = (list(t) for t in carry)
            n0 = first + unroll * step
            cur = [scores(n0, h) for h in range(nh)]
            for i in range(unroll):
                nxt = []
                for h in range(nh):
                    if i + 1 < unroll:
                        nxt.append(scores(n0 + i + 1, h))
                    m_new, psum, pv = softmax_pv(n0 + i, h, cur[h], ms[h], ls[h])
                    alpha = jnp.exp2(ms[h] - m_new)
                    ls[h] = alpha * ls[h] + psum
                    ms[h] = m_new
                    acc_ref[h] = acc_ref[h] * alpha + pv
                cur = nxt
            return tuple(ms), tuple(ls)
        return body

    nfull = (j + 1) // MOBA_UNROLL
    rem = j - nfull * MOBA_UNROLL
    carry = lax.fori_loop(0, nfull, make_body(MOBA_UNROLL, 0), (tuple(ms), tuple(ls)))
    carry = lax.fori_loop(0, (rem == 2).astype(jnp.int32), make_body(2, nfull * MOBA_UNROLL), carry)
    ms, ls = lax.fori_loop(0, (rem == 1).astype(jnp.int32), make_body(1, nfull * MOBA_UNROLL), carry)
    o_ref[0] = jnp.concatenate([jnp.transpose(acc_ref[h] / ls[h]) for h in range(nh)],
                               axis=1).astype(o_ref.dtype)


def _moba_attn(q_r, k_r, vt, sel):
    bsz, seq, _ = q_r.shape
    nb = seq // MOBA_BLOCK
    assert nb % MOBA_UNROLL == 0 and MOBA_UNROLL == 4, (nb, MOBA_UNROLL)
    tok_spec = pl.BlockSpec((1, MOBA_BLOCK, D_MOBA), lambda b, j: (b, j, 0))
    return pl.pallas_call(
        functools.partial(_moba_attn_kernel, nb=nb),
        out_shape=jax.ShapeDtypeStruct((bsz, seq, D_MOBA), BF16),
        grid=(bsz, nb),
        in_specs=[tok_spec,
                  pl.BlockSpec((1, seq, D_MOBA), lambda b, j: (b, 0, 0)),
                  pl.BlockSpec((1, nb, D_MOBA, MOBA_BLOCK), lambda b, j: (b, 0, 0, 0)),
                  pl.BlockSpec((1, 1, N_HEADS_MOBA * nb, MOBA_BLOCK), lambda b, j: (b, j, 0, 0))],
        out_specs=tok_spec,
        scratch_shapes=[pltpu.VMEM((N_HEADS_MOBA, HEAD_DIM, MOBA_BLOCK), F32)],
        compiler_params=pltpu.CompilerParams(dimension_semantics=("parallel", "arbitrary"),
                                             vmem_limit_bytes=VMEM_LIMIT),
        name="moba_attn",
    )(q_r, k_r, vt, sel)


def _route_record(logits):
    lane = lax.broadcasted_iota(jnp.int32, logits.shape, 1)
    big = jnp.int32(LANES)

    def first_lane(mask):
        return jnp.min(jnp.where(mask, lane, big), axis=1, keepdims=True)

    is_g = lane < N_GROUPS
    m1 = jnp.max(jnp.where(is_g, logits, NEG_INF), axis=1, keepdims=True)
    s1 = jnp.sum(jnp.where(is_g, jnp.exp(logits - m1), 0.0), axis=1, keepdims=True)
    pg = 1.0 / s1
    gsel = first_lane(is_g & (logits == m1))

    in_grp = (lane >= GATE_LANE0) & (((lane - GATE_LANE0) >> 2) == gsel) & (lane < GATE_LANE0 + N_EXPERTS)
    m2 = jnp.max(jnp.where(in_grp, logits, NEG_INF), axis=1, keepdims=True)
    s2 = jnp.sum(jnp.where(in_grp, jnp.exp(logits - m2), 0.0), axis=1, keepdims=True)
    e1 = first_lane(in_grp & (logits == m2))
    rest = in_grp & (lane != e1)
    m2b = jnp.max(jnp.where(rest, logits, NEG_INF), axis=1, keepdims=True)
    e2 = first_lane(rest & (logits == m2b))
    pe1 = 1.0 / s2
    pe2 = jnp.exp(m2b - m2) / s2
    tot = pe1 + pe2
    w1 = pg * (pe1 / tot)
    w2 = pg * (pe2 / tot)
    first_lo = e1 < e2
    lo = jnp.minimum(e1, e2)
    hi = jnp.maximum(e1, e2)
    a = (lo - GATE_LANE0) & (EXPERTS_PER_GROUP - 1)
    b = (hi - GATE_LANE0) & (EXPERTS_PER_GROUP - 1)
    bucket = gsel * N_PAIRS + ((a * (2 * EXPERTS_PER_GROUP - 1 - a)) >> 1) + (b - a - 1)
    record = jnp.where(lane == ROUTE_BUCKET, bucket.astype(F32),
                       jnp.where(lane == ROUTE_W_LO, jnp.where(first_lo, w1, w2),
                                 jnp.where(lane == ROUTE_W_HI, jnp.where(first_lo, w2, w1), 0.0)))
    return record, jnp.sum(jnp.where(lane == bucket, 1.0, 0.0), axis=0, keepdims=True)


def _mix_route_kernel(ydn_ref, ymb_ref, x_ref, wo_ref, g_ref, b_ref, rw_ref, rb_ref, h_ref, hb_ref, route_ref,
                      cnt_ref):
    nsub = cnt_ref.shape[0]
    rows = [slice(i * MOE_TS, (i + 1) * MOE_TS) for i in range(nsub)]
    wh, wl = _split2(rw_ref[...])
    wcat = jnp.concatenate([wh, wl], axis=1)

    mixes = [jnp.dot(jnp.concatenate([ydn_ref[r, :], ymb_ref[r, :]], axis=1), wo_ref[...],
                     preferred_element_type=F32) for r in rows]
    logits = []
    for r, mix in zip(rows, mixes):
        hval = _layer_norm(DEEPNORM_ALPHA * x_ref[r, :] + mix, g_ref[...], b_ref[...])
        h_ref[r, :] = hval
        hb_ref[r, :] = hval.astype(BF16)
        hh, hl = _split2(hval)
        both = jnp.dot(hh, wcat, preferred_element_type=F32)
        logits.append(both[:, 0:LANES] + both[:, LANES:2 * LANES]
                      + jnp.dot(hl, wh, preferred_element_type=F32) + rb_ref[...])
    for i, r in enumerate(rows):
        record, counts = _route_record(logits[i])
        route_ref[r, :] = record
        cnt_ref[i] = counts


def _mix_route(y_dn, y_mb, x2, wo, g1, b1, rw, rb, tm):
    n, d = x2.shape
    nsub = tm // MOE_TS
    row = lambda w: pl.BlockSpec((1, w), lambda i: (0, 0))
    return pl.pallas_call(
        _mix_route_kernel,
        out_shape=(jax.ShapeDtypeStruct((n, d), F32), jax.ShapeDtypeStruct((n, d), BF16),
                   jax.ShapeDtypeStruct((n, LANES), F32), jax.ShapeDtypeStruct((n // MOE_TS, 1, LANES), F32)),
        grid=(n // tm,),
        in_specs=[pl.BlockSpec((tm, D_DN), lambda i: (i, 0)), pl.BlockSpec((tm, D_MOBA), lambda i: (i, 0)),
                  pl.BlockSpec((tm, d), lambda i: (i, 0)), pl.BlockSpec((D_DN + D_MOBA, d), lambda i: (0, 0)),
                  row(d), row(d), pl.BlockSpec((d, LANES), lambda i: (0, 0)), row(LANES)],
        out_specs=(pl.BlockSpec((tm, d), lambda i: (i, 0)), pl.BlockSpec((tm, d), lambda i: (i, 0)),
                   pl.BlockSpec((tm, LANES), lambda i: (i, 0)), pl.BlockSpec((nsub, 1, LANES), lambda i: (i, 0, 0))),
        compiler_params=pltpu.CompilerParams(dimension_semantics=("parallel",), vmem_limit_bytes=VMEM_LIMIT),
        name="mix_route",
    )(y_dn, y_mb, x2, wo, g1, b1, rw, rb)


def _bucket_offsets_col(ohf):
    cnt = jnp.sum(ohf, axis=1, keepdims=True).astype(jnp.int32)
    pad = (((cnt + (GRAN - 1)) >> GRAN_SHIFT) << GRAN_SHIFT).astype(F32)
    r = lax.broadcasted_iota(jnp.int32, (LANES, LANES), 0)
    c = lax.broadcasted_iota(jnp.int32, (LANES, LANES), 1)
    before = jnp.where(c < r, 1.0, 0.0)
    return _dot(before, jnp.broadcast_to(pad, (LANES, LANES)))[:, 0:1]


def _moe_sort_kernel(gmap_ref, nvalid_ref, tail0_ref, taillen_ref, hb_ref, route_ref, lstrict_ref,
                     xg_ref, wsg_ref, xs_ref, ws_ref, zx_ref, zw_ref, sem):
    s = pl.program_id(0)
    nsteps = pl.num_programs(0)
    slot = s & 1
    ts = route_ref.shape[0]
    route = route_ref[...]
    rt = jnp.transpose(route)
    bucket_row = rt[ROUTE_BUCKET:ROUTE_BUCKET + 1, :].astype(jnp.int32)
    sub = lax.broadcasted_iota(jnp.int32, (LANES, ts), 0)
    ohf = jnp.where(sub == bucket_row, 1.0, 0.0)
    loff = _bucket_offsets_col(ohf)
    rank = lax.dot_general(ohf.astype(BF16), lstrict_ref[...], (((1,), (1,)), ((), ())),
                           preferred_element_type=F32)
    dest = jnp.sum(ohf * (loff + rank), axis=0, keepdims=True).astype(jnp.int32)
    rh, rl = _split2(route)
    wcat = jnp.concatenate([rh, rl], axis=1)
    rc = LROWS // PERM_CHUNKS
    for c in range(PERM_CHUNKS):
        rowi = lax.broadcasted_iota(jnp.int32, (rc, ts), 0) + c * rc
        perm = jnp.where(rowi == dest, 1.0, 0.0).astype(BF16)
        xs_ref[slot, c * rc:(c + 1) * rc, :] = jnp.dot(perm, hb_ref[...], preferred_element_type=F32).astype(BF16)
        wparts = jnp.dot(perm, wcat, preferred_element_type=F32)
        ws_ref[slot, c * rc:(c + 1) * rc, :] = wparts[:, 0:LANES] + wparts[:, LANES:2 * LANES]

    def copies(step, g):
        sl = step & 1
        src = pl.ds(pl.multiple_of(g * GRAN, GRAN), GRAN)
        dst = pl.ds(pl.multiple_of(gmap_ref[step * LGRAN + g] * GRAN, GRAN), GRAN)
        return (pltpu.make_async_copy(xs_ref.at[sl, src, :], xg_ref.at[dst, :], sem.at[0, sl]),
                pltpu.make_async_copy(ws_ref.at[sl, src, :], wsg_ref.at[dst, :], sem.at[1, sl]))

    def fill_copies(b, i):
        dst = pl.ds(pl.multiple_of((tail0_ref[b] + i) * GRAN, GRAN), GRAN)
        return (pltpu.make_async_copy(zx_ref.at[0:GRAN, :], xg_ref.at[dst, :], sem.at[2, 0]),
                pltpu.make_async_copy(zw_ref.at[0:GRAN, :], wsg_ref.at[dst, :], sem.at[2, 1]))

    def unused_tile_copies(t):
        dst = pl.ds(pl.multiple_of(t * MOE_TM, MOE_TM), MOE_TM)
        return (pltpu.make_async_copy(zx_ref, xg_ref.at[dst, :], sem.at[2, 0]),
                pltpu.make_async_copy(zw_ref, wsg_ref.at[dst, :], sem.at[2, 1]))

    def run(step, fn):
        def body(g, carry):
            for cp in copies(step, g):
                fn(cp)
            return carry
        lax.fori_loop(0, nvalid_ref[step], body, 0)

    def run_fill(fn):
        for b in range(N_BUCKETS):
            def body(i, carry, b=b):
                for cp in fill_copies(b, i):
                    fn(cp)
                return carry
            lax.fori_loop(0, taillen_ref[b], body, 0)

        def tile_body(t, carry):
            for cp in unused_tile_copies(t):
                fn(cp)
            return carry
        lax.fori_loop(tail0_ref[N_BUCKETS], xg_ref.shape[0] // MOE_TM, tile_body, 0)

    @pl.when(s == 0)
    def _():
        zx_ref[...] = jnp.zeros_like(zx_ref)
        zw_ref[...] = jnp.zeros_like(zw_ref)
        run_fill(lambda cp: cp.start())

    run(s, lambda cp: cp.start())

    @pl.when(s > 0)
    def _():
        run(s - 1, lambda cp: cp.wait())

    @pl.when(s == nsteps - 1)
    def _():
        run(s, lambda cp: cp.wait())
        run_fill(lambda cp: cp.wait())


def _moe_sort(plan, hb, route, lstrict):
    n, d = hb.shape
    ts = MOE_TS
    rows = plan["n_tiles"] * MOE_TM
    return pl.pallas_call(
        _moe_sort_kernel,
        out_shape=(jax.ShapeDtypeStruct((rows, d), BF16), jax.ShapeDtypeStruct((rows, LANES), F32)),
        grid_spec=pltpu.PrefetchScalarGridSpec(
            num_scalar_prefetch=4,
            grid=(n // ts,),
            in_specs=[pl.BlockSpec((ts, d), lambda s, *_: (s, 0)), pl.BlockSpec((ts, LANES), lambda s, *_: (s, 0)),
                      pl.BlockSpec((ts, ts), lambda s, *_: (0, 0))],
            out_specs=(pl.BlockSpec(memory_space=pl.ANY), pl.BlockSpec(memory_space=pl.ANY)),
            scratch_shapes=[pltpu.VMEM((2, LROWS, d), BF16), pltpu.VMEM((2, LROWS, LANES), F32),
                            pltpu.VMEM((MOE_TM, d), BF16), pltpu.VMEM((MOE_TM, LANES), F32),
                            pltpu.SemaphoreType.DMA((3, 2))]),
        compiler_params=pltpu.CompilerParams(dimension_semantics=("arbitrary",), vmem_limit_bytes=VMEM_LIMIT),
        name="moe_sort",
    )(plan["gmap"], plan["nvalid"], plan["tail0"], plan["taillen"], hb, route, lstrict)


def _moe_expert_kernel(xt_ref, elo_ref, ehi_ref, valid_ref, x_ref, w_ref, wg0, wu0, wd0, wg1, wu1, wd1, o_ref):
    t = pl.program_id(0)

    @pl.when(valid_ref[t] > 0)
    def _():
        x = x_ref[...]
        w = w_ref[...]
        gates = [jnp.dot(x, wg[0].astype(BF16), preferred_element_type=F32) for wg in (wg0, wg1)]
        ups = [jnp.dot(x, wu[0].astype(BF16), preferred_element_type=F32) for wu in (wu0, wu1)]
        hes = [(_silu(gates[i]) * ups[i] * w[:, lane:lane + 1]).astype(BF16)
               for i, lane in enumerate((ROUTE_W_LO, ROUTE_W_HI))]
        o_ref[...] = (jnp.dot(hes[0], wd0[0].astype(BF16), preferred_element_type=F32)
                      + jnp.dot(hes[1], wd1[0].astype(BF16), preferred_element_type=F32)).astype(o_ref.dtype)

    @pl.when(valid_ref[t] == 0)
    def _():
        o_ref[...] = jnp.zeros_like(o_ref)


def _moe_experts(plan, xg, wsg, wg, wu, wd):
    rows, d = xg.shape
    tm = MOE_TM
    tok = lambda width: pl.BlockSpec((tm, width), lambda t, xt, elo, ehi, valid: (xt[t], 0))
    lo3 = lambda shape: pl.BlockSpec(shape, lambda t, xt, elo, ehi, valid: (elo[t], 0, 0))
    hi3 = lambda shape: pl.BlockSpec(shape, lambda t, xt, elo, ehi, valid: (ehi[t], 0, 0))
    return pl.pallas_call(
        _moe_expert_kernel,
        out_shape=jax.ShapeDtypeStruct((rows, d), BF16),
        grid_spec=pltpu.PrefetchScalarGridSpec(
            num_scalar_prefetch=4,
            grid=(rows // tm,),
            in_specs=[tok(d), tok(LANES),
                      lo3((1, d, D_EXPERT)), lo3((1, d, D_EXPERT)), lo3((1, D_EXPERT, d)),
                      hi3((1, d, D_EXPERT)), hi3((1, d, D_EXPERT)), hi3((1, D_EXPERT, d))],
            out_specs=pl.BlockSpec((tm, d), lambda t, *_: (t, 0))),
        compiler_params=pltpu.CompilerParams(dimension_semantics=("arbitrary",), vmem_limit_bytes=VMEM_LIMIT),
        name="moe_experts",
    )(plan["xtile"], plan["elo"], plan["ehi"], plan["valid"], xg, wsg, wg, wu, wd, wg, wu, wd)


def _moe_unsort_kernel(gmap_ref, og_ref, route_ref, h_ref, lstrict_ref, g_ref, b_ref, out_ref, ol_ref, sem):
    s = pl.program_id(0)
    nsteps = pl.num_programs(0)
    slot = s & 1
    ts = route_ref.shape[0]

    def gather(step, fn):
        sl = step & 1

        def body(g, carry):
            src = pl.ds(pl.multiple_of(gmap_ref[step * LGRAN + g] * GRAN, GRAN), GRAN)
            dst = pl.ds(pl.multiple_of(g * GRAN, GRAN), GRAN)
            fn(pltpu.make_async_copy(og_ref.at[src, :], ol_ref.at[sl, dst, :], sem.at[sl]))
            return carry
        lax.fori_loop(0, LGRAN, body, 0)

    @pl.when(s == 0)
    def _():
        gather(s, lambda cp: cp.start())

    @pl.when(s + 1 < nsteps)
    def _():
        gather(s + 1, lambda cp: cp.start())

    route = route_ref[...]
    bucket_col = route[:, ROUTE_BUCKET:ROUTE_BUCKET + 1].astype(jnp.int32)
    lane = lax.broadcasted_iota(jnp.int32, (ts, LANES), 1)
    ohf = jnp.where(lane == bucket_col, 1.0, 0.0)
    cnt = jnp.sum(ohf, axis=0, keepdims=True).astype(jnp.int32)
    pad = (((cnt + (GRAN - 1)) >> GRAN_SHIFT) << GRAN_SHIFT).astype(F32)
    r = lax.broadcasted_iota(jnp.int32, (LANES, LANES), 0)
    c = lax.broadcasted_iota(jnp.int32, (LANES, LANES), 1)
    loff = _dot(jnp.broadcast_to(pad, (8, LANES)), jnp.where(r < c, 1.0, 0.0))[0:1, :]
    rank = jnp.dot(lstrict_ref[...], ohf.astype(BF16), preferred_element_type=F32)
    dest = jnp.sum(ohf * (loff + rank), axis=1, keepdims=True).astype(jnp.int32)
    gather(s, lambda cp: cp.wait())
    tc = ts // PERM_CHUNKS
    lrow = lax.broadcasted_iota(jnp.int32, (tc, LROWS), 1)
    for c in range(PERM_CHUNKS):
        rows = slice(c * tc, (c + 1) * tc)
        perm_t = jnp.where(lrow == dest[rows, :], 1.0, 0.0).astype(BF16)
        ffn = jnp.dot(perm_t, ol_ref[slot], preferred_element_type=F32)
        out_ref[rows, :] = _layer_norm(DEEPNORM_ALPHA * h_ref[rows, :] + ffn, g_ref[...], b_ref[...])


def _moe_unsort(plan, og, route, hf, lstrict, g2, b2):
    n, d = hf.shape
    ts = MOE_TS
    row = pl.BlockSpec((1, d), lambda s, *_: (0, 0))
    return pl.pallas_call(
        _moe_unsort_kernel,
        out_shape=jax.ShapeDtypeStruct((n, d), F32),
        grid_spec=pltpu.PrefetchScalarGridSpec(
            num_scalar_prefetch=1,
            grid=(n // ts,),
            in_specs=[pl.BlockSpec(memory_space=pl.ANY), pl.BlockSpec((ts, LANES), lambda s, *_: (s, 0)),
                      pl.BlockSpec((ts, d), lambda s, *_: (s, 0)), pl.BlockSpec((ts, ts), lambda s, *_: (0, 0)),
                      row, row],
            out_specs=pl.BlockSpec((ts, d), lambda s, *_: (s, 0)),
            scratch_shapes=[pltpu.VMEM((2, LROWS, d), BF16), pltpu.SemaphoreType.DMA((2,))]),
        compiler_params=pltpu.CompilerParams(dimension_semantics=("arbitrary",), vmem_limit_bytes=VMEM_LIMIT),
        name="moe_unsort",
    )(plan["gmap_back"], og, route, hf, lstrict, g2, b2)


def _moe_plan(cnt_half, n):
    nsrc = n // MOE_TS
    i32 = jnp.int32
    cnt = cnt_half.reshape(nsrc, -1, LANES).sum(axis=1)[:, :N_BUCKETS].astype(i32)
    run_g = (cnt + GRAN - 1) // GRAN
    nvalid = run_g.sum(axis=1)
    loff_g = jnp.cumsum(run_g, axis=1) - run_g
    bucket_g = run_g.sum(axis=0)
    gpt = MOE_TM // GRAN
    btiles = (bucket_g + gpt - 1) // gpt
    tend = jnp.cumsum(btiles)
    tstart = tend - btiles
    gofs = tstart[None, :] * gpt + jnp.cumsum(run_g, axis=0) - run_g
    n_tiles = -(-(n + nsrc * N_BUCKETS * (GRAN - 1)) // MOE_TM) + N_BUCKETS + 1
    g = jnp.arange(LGRAN, dtype=i32)[None, :, None]
    in_run = (g >= loff_g[:, None, :]) & (g < (loff_g + run_g)[:, None, :])
    gmap = jnp.arange(LGRAN, dtype=i32)[None, :] + jnp.sum(jnp.where(in_run, (gofs - loff_g)[:, None, :], 0), axis=2)
    is_valid = jnp.arange(LGRAN, dtype=i32)[None, :] < nvalid[:, None]
    zero_gran = (n_tiles - 1) * gpt
    t = jnp.arange(n_tiles, dtype=i32)
    tb = jnp.minimum(jnp.sum(t[:, None] >= tend[None, :], axis=1), N_BUCKETS - 1)
    valid = (t < tend[-1]).astype(i32)
    pairs = [(a, b) for a in range(EXPERTS_PER_GROUP) for b in range(a + 1, EXPERTS_PER_GROUP)]
    pidx = tb % N_PAIRS
    pair_a = sum(jnp.where(pidx == i, a, 0) for i, (a, _) in enumerate(pairs))
    pair_b = sum(jnp.where(pidx == i, b, 0) for i, (_, b) in enumerate(pairs))
    grp = tb // N_PAIRS
    return {
        "n_tiles": n_tiles,
        "gmap": jnp.where(is_valid, gmap, 0).reshape(-1).astype(i32),
        "gmap_back": jnp.where(is_valid, gmap, zero_gran).reshape(-1).astype(i32),
        "nvalid": nvalid.astype(i32),
        "tail0": jnp.concatenate([tstart * gpt + bucket_g, tend[-1:]]).astype(i32),
        "taillen": (btiles * gpt - bucket_g).astype(i32),
        "xtile": jnp.where(valid > 0, t, 0).astype(i32),
        "elo": (grp * EXPERTS_PER_GROUP + pair_a).astype(i32),
        "ehi": (grp * EXPERTS_PER_GROUP + pair_b).astype(i32),
        "valid": valid,
    }


def _pad_lanes(a, lane0=0):
    return jnp.zeros((1, LANES), F32).at[0, lane0:lane0 + a.shape[0]].set(a.astype(F32))


def _rope_tables(seq):
    half = HEAD_DIM // 2
    inv_freq = ROPE_THETA ** (-np.arange(half, dtype=np.float64) / half)
    ang = np.arange(seq, dtype=np.float64)[:, None] * inv_freq[None, :]
    cos, sin = np.cos(ang), np.sin(ang)
    return (jnp.asarray(np.concatenate([cos, cos], axis=-1), F32),
            jnp.asarray(np.concatenate([-sin, sin], axis=-1), F32))


def _layer(x, w_in, conv_w, a_log, dt_bias, dn_norm_w, w_out, ln1_g, ln1_b, router_w1, router_b1,
           router_w2, router_b2, w_gate, w_up, w_down, ln2_g, ln2_b):
    bsz, seq, d = x.shape
    n = bsz * seq
    x2 = x.reshape(n, d)

    w_all = _w_prep(w_in)

    cos_t, sin_t = _rope_tables(seq)
    nb = seq // MOBA_BLOCK
    dn_qkv, z, ba, q_r, k_r, vt, sel = _in_proj(x2, w_all, conv_w, cos_t, sin_t, min(IN_PROJ_TM, seq), seq)

    y_dn = _deltanet(dn_qkv.reshape(bsz, seq, 3 * D_DN), z.reshape(bsz, seq, D_DN), ba.reshape(bsz, seq, LANES),
                     _pad_lanes(a_log, N_HEADS_DN), _pad_lanes(dt_bias, N_HEADS_DN),
                     dn_norm_w.astype(F32).reshape(1, HEAD_DIM))

    q_r, k_r = q_r.reshape(bsz, seq, D_MOBA), k_r.reshape(bsz, seq, D_MOBA)
    vt = vt.reshape(bsz, nb, D_MOBA, MOBA_BLOCK)
    sel = sel.reshape(bsz, nb, N_HEADS_MOBA * nb, MOBA_BLOCK)
    y_mb = _moba_attn(q_r, k_r, vt, sel)

    rw = jnp.concatenate([router_w1, jnp.transpose(router_w2, (1, 0, 2)).reshape(d, N_EXPERTS)], axis=1)
    rw = jnp.pad(rw, ((0, 0), (0, LANES - rw.shape[1])))
    rb = _pad_lanes(jnp.concatenate([router_b1, router_b2.reshape(-1)]))
    hf, hb, route, cnt = _mix_route(y_dn.reshape(n, D_DN), y_mb.reshape(n, D_MOBA), x2, w_out.astype(BF16),
                                    ln1_g.reshape(1, d), ln1_b.reshape(1, d), rw, rb, min(2 * MOE_TS, n))

    plan = _moe_plan(cnt, n)
    idx = jnp.arange(MOE_TS, dtype=jnp.int32)
    lstrict = (idx[None, :] < idx[:, None]).astype(BF16)
    xg, wsg = _moe_sort(plan, hb, route, lstrict)
    og = _moe_experts(plan, xg, wsg, w_gate, w_up, w_down)
    out = _moe_unsort(plan, og, route, hf, lstrict, ln2_g.reshape(1, d), ln2_b.reshape(1, d))
    return out.reshape(bsz, seq, d)


def kernel(x, w_in, conv_w, a_log, dt_bias, dn_norm_w, w_out, ln1_g, ln1_b, router_w1, router_b1, router_w2, router_b2, expert_w_gate, expert_w_up, expert_w_down, ln2_g, ln2_b):
    for l in range(DEPTH):
        x = _layer(x, w_in[l], conv_w[l], a_log[l], dt_bias[l], dn_norm_w[l], w_out[l], ln1_g[l], ln1_b[l],
                   router_w1[l], router_b1[l], router_w2[l], router_b2[l], expert_w_gate[l], expert_w_up[l],
                   expert_w_down[l], ln2_g[l], ln2_b[l])
    return x
```

```python
import functools

import numpy as np
import jax
import jax.numpy as jnp
from jax import lax
from jax.experimental import pallas as pl
from jax.experimental.pallas import tpu as pltpu

F32 = jnp.float32
BF16 = jnp.bfloat16

HEAD_DIM = 128
N_HEADS_DN = 4
N_HEADS_MOBA = 4
D_DN = N_HEADS_DN * HEAD_DIM
D_MOBA = N_HEADS_MOBA * HEAD_DIM
CONV_K = 4
DN_CHUNK = 64
MOBA_BLOCK = 256
MOBA_TOPK = 3
MOBA_UNROLL = 4
ROPE_THETA = 10000.0
N_GROUPS = 4
EXPERTS_PER_GROUP = 4
N_EXPERTS = N_GROUPS * EXPERTS_PER_GROUP
D_EXPERT = 256
LN_EPS = 1e-5
RMS_EPS = 1e-6
L2_EPS = 1e-6
NEG_INF = -1e30
LOG2E = 1.4426950408889634
DEPTH = 1
DEEPNORM_ALPHA = (2 * DEPTH) ** 0.25

LANES = 128
IN_PROJ_TM = 512
IN_PROJ_GROUP = 512
DN_TILE = 256
DN_HEADS_PER_STEP = 4
DN_SEQS_PER_STEP = 2
GATE_LANE0 = N_GROUPS
N_PAIRS = EXPERTS_PER_GROUP * (EXPERTS_PER_GROUP - 1) // 2
N_BUCKETS = N_GROUPS * N_PAIRS
ROUTE_BUCKET, ROUTE_W_LO, ROUTE_W_HI = 0, 1, 2
MOE_TS = 512
MOE_TM = 512
GRAN = 16
GRAN_SHIFT = 4
LROWS = -(-(MOE_TS + N_BUCKETS * (GRAN - 1)) // LANES) * LANES
LGRAN = LROWS // GRAN
PERM_CHUNKS = 4
VMEM_LIMIT = 48 * 1024 * 1024


def _dot(a, b):
    return jnp.dot(a.astype(BF16), b.astype(BF16), preferred_element_type=F32)


def _dot_nt(a, b):
    return lax.dot_general(a.astype(BF16), b.astype(BF16), (((1,), (1,)), ((), ())),
                           preferred_element_type=F32)


def _split2(a):
    hi = a.astype(BF16)
    lo = (a - hi.astype(F32)).astype(BF16)
    return hi, lo


def _split3(a):
    hi = a.astype(BF16)
    r = a - hi.astype(F32)
    mid = r.astype(BF16)
    lo = (r - mid.astype(F32)).astype(BF16)
    return hi, mid, lo


def _dot3_nt(a, b):
    ah, al = _split2(a)
    bh, bl = _split2(b)
    dn = (((1,), (1,)), ((), ()))
    return (lax.dot_general(ah, bh, dn, preferred_element_type=F32)
            + lax.dot_general(ah, bl, dn, preferred_element_type=F32)
            + lax.dot_general(al, bh, dn, preferred_element_type=F32))


def _dot_exact_lhs(a_bf16, b):
    bh, bm, bl = _split3(b)
    return (jnp.dot(a_bf16, bh, preferred_element_type=F32) + jnp.dot(a_bf16, bm, preferred_element_type=F32)
            + jnp.dot(a_bf16, bl, preferred_element_type=F32))


def _silu(x):
    return x * jax.nn.sigmoid(x)


def _softplus(x):
    return jnp.maximum(x, 0.0) + jnp.log1p(jnp.exp(-jnp.abs(x)))


def _layer_norm(t, g, b):
    mu = jnp.mean(t, axis=-1, keepdims=True)
    d = t - mu
    var = jnp.mean(d * d, axis=-1, keepdims=True)
    return d * lax.rsqrt(var + LN_EPS) * g + b


def _lane_pick(x, lane):
    ids = lax.broadcasted_iota(jnp.int32, x.shape, 1)
    return jnp.sum(jnp.where(ids == lane, x, 0.0), axis=1, keepdims=True)


def _w_prep_kernel(w_ref, o_ref):
    o_b = 4 * D_DN
    o_mb = o_b + 2 * N_HEADS_DN
    o_ref[:, 0:o_b] = w_ref[:, 0:o_b].astype(o_ref.dtype)
    lane = lax.broadcasted_iota(jnp.int32, (w_ref.shape[0], LANES), 1)
    o_ref[:, o_b:o_b + LANES] = jnp.where(lane < 2 * N_HEADS_DN, w_ref[:, o_b:o_b + LANES], 0.0).astype(o_ref.dtype)
    o_ref[:, o_b + LANES:] = w_ref[:, o_mb:].astype(o_ref.dtype)


def _w_prep(w_in):
    d, cols = w_in.shape
    out_cols = cols - 2 * N_HEADS_DN + LANES
    rows = 256
    return pl.pallas_call(
        _w_prep_kernel,
        out_shape=jax.ShapeDtypeStruct((d, out_cols), BF16),
        grid=(d // rows,),
        in_specs=[pl.BlockSpec((rows, cols), lambda i: (i, 0))],
        out_specs=pl.BlockSpec((rows, out_cols), lambda i: (i, 0)),
        compiler_params=pltpu.CompilerParams(dimension_semantics=("parallel",), vmem_limit_bytes=VMEM_LIMIT),
        name="w_prep",
    )(w_in)


def _in_proj_kernel(x_ref, w_ref, cw_ref, cos_ref, sin_ref, dn_ref, z_ref, ba_ref, q_ref, k_ref, vt_ref, sel_ref,
                    cb_ref, km_ref, *, tiles_per_seq, nb, topk):
    i = pl.program_id(0)
    tm = x_ref.shape[0]
    o0 = 3 * D_DN
    o1 = o0 + D_DN
    o2 = o1 + LANES
    seq_tile = i % tiles_per_seq
    spb = tm // MOBA_BLOCK

    @pl.when(seq_tile == 0)
    def _():
        cb_ref[0:8, :] = jnp.zeros((8, o0), F32)
        km_ref[...] = jnp.zeros_like(km_ref)

    half = HEAD_DIM // 2
    heads = [slice(h * HEAD_DIM, (h + 1) * HEAD_DIM) for h in range(N_HEADS_MOBA)]
    rows = [slice(sb * MOBA_BLOCK, (sb + 1) * MOBA_BLOCK) for sb in range(spb)]

    def rope(t, rs):
        return t * cos_ref[rs, :] + pltpu.roll(t, half, 1) * sin_ref[rs, :]

    xb = x_ref[...].astype(BF16)
    gw = IN_PROJ_GROUP
    qr = None
    for grp in range(3 * D_DN // gw):
        cs = slice(grp * gw, (grp + 1) * gw)
        u = jnp.dot(xb, w_ref[:, cs], preferred_element_type=F32)
        mb = jnp.dot(xb, w_ref[:, o2 + grp * gw:o2 + (grp + 1) * gw], preferred_element_type=F32)
        cb_ref[8:8 + tm, cs] = u
        acc = cw_ref[CONV_K - 1:CONV_K, cs] * u
        for s in range(1, CONV_K):
            acc = acc + cw_ref[CONV_K - 1 - s:CONV_K - s, cs] * cb_ref[8 - s:8 - s + tm, cs]
        cb_ref[0:8, cs] = u[tm - 8:tm, :]
        act = _silu(acc)
        if grp * gw < 2 * D_DN:
            outs = []
            for h in range(gw // HEAD_DIM):
                t = act[:, h * HEAD_DIM:(h + 1) * HEAD_DIM]
                t = t * lax.rsqrt(jnp.sum(t * t, axis=-1, keepdims=True) + L2_EPS)
                outs.append(t * (HEAD_DIM ** -0.5) if grp * gw < D_DN else t)
            act = jnp.concatenate(outs, axis=1)
        dn_ref[:, cs] = act

        if grp == 0:
            qr = [[rope(mb[rs, hs], rs) * (HEAD_DIM ** -0.5) for hs in heads] for rs in rows]
            for sb, rs in enumerate(rows):
                q_ref[rs, :] = jnp.concatenate([q * LOG2E for q in qr[sb]], axis=1).astype(q_ref.dtype)
        elif grp == 1:
            blk = lax.broadcasted_iota(jnp.int32, (nb, MOBA_BLOCK), 0)
            kmrow = lax.broadcasted_iota(jnp.int32, (nb, D_MOBA), 0)
            km = km_ref[...]
            for sb, rs in enumerate(rows):
                j = seq_tile * spb + sb
                kr = [rope(mb[rs, hs], rs) for hs in heads]
                k_ref[rs, :] = jnp.concatenate(kr, axis=1).astype(k_ref.dtype)
                for h, hs in enumerate(heads):
                    gate = _dot3_nt(km[:, hs], qr[sb][h])
                    gate = jnp.where(blk < j, gate, NEG_INF)
                    rank = jnp.zeros(gate.shape, F32)
                    for m in range(nb):
                        gm = gate[m:m + 1, :]
                        ahead = (gm > gate) | ((gm == gate) & (blk > m))
                        rank = rank + jnp.where(ahead, 1.0, 0.0)
                    sel = (blk < j) & (rank < topk)
                    sel_ref[sb, h * nb:(h + 1) * nb, :] = jnp.where(sel, 1.0, 0.0)
                kmean = jnp.concatenate([jnp.mean(t, axis=0, keepdims=True) for t in kr], axis=1)
                km = jnp.where(kmrow == j, kmean, km)
            km_ref[...] = km
        else:
            for sb, rs in enumerate(rows):
                vt_ref[sb] = jnp.transpose(mb[rs, :]).astype(vt_ref.dtype)
    z_ref[...] = _silu(jnp.dot(xb, w_ref[:, o0:o1], preferred_element_type=F32))
    ba_ref[...] = jnp.dot(xb, w_ref[:, o1:o2], preferred_element_type=F32)


def _in_proj(x2, w_all, conv_w, cos_t, sin_t, tm, seq):
    n, d = x2.shape
    wc = w_all.shape[1]
    nb = seq // MOBA_BLOCK
    spb = tm // MOBA_BLOCK
    tps = seq // tm
    kern = functools.partial(_in_proj_kernel, tiles_per_seq=tps, nb=nb, topk=min(MOBA_TOPK, nb))
    tok = lambda width: pl.BlockSpec((tm, width), lambda i: (i, 0))
    tab = pl.BlockSpec((tm, HEAD_DIM), lambda i: (i % tps, 0))
    return pl.pallas_call(
        kern,
        out_shape=(jax.ShapeDtypeStruct((n, 3 * D_DN), F32), jax.ShapeDtypeStruct((n, D_DN), F32),
                   jax.ShapeDtypeStruct((n, LANES), F32),
                   jax.ShapeDtypeStruct((n, D_MOBA), BF16), jax.ShapeDtypeStruct((n, D_MOBA), BF16),
                   jax.ShapeDtypeStruct((n // MOBA_BLOCK, D_MOBA, MOBA_BLOCK), BF16),
                   jax.ShapeDtypeStruct((n // MOBA_BLOCK, N_HEADS_MOBA * nb, MOBA_BLOCK), F32)),
        grid=(n // tm,),
        in_specs=[tok(d), pl.BlockSpec((d, wc), lambda i: (0, 0)),
                  pl.BlockSpec((CONV_K, 3 * D_DN), lambda i: (0, 0)), tab, tab],
        out_specs=(tok(3 * D_DN), tok(D_DN), tok(LANES), tok(D_MOBA), tok(D_MOBA),
                   pl.BlockSpec((spb, D_MOBA, MOBA_BLOCK), lambda i: (i, 0, 0)),
                   pl.BlockSpec((spb, N_HEADS_MOBA * nb, MOBA_BLOCK), lambda i: (i, 0, 0))),
        scratch_shapes=[pltpu.VMEM((8 + tm, 3 * D_DN), F32), pltpu.VMEM((nb, D_MOBA), F32)],
        compiler_params=pltpu.CompilerParams(dimension_semantics=("arbitrary",), vmem_limit_bytes=VMEM_LIMIT),
        name="in_proj",
    )(x2, w_all, conv_w, cos_t, sin_t)


def _deltanet_kernel(q_ref, k_ref, v_ref, z_ref, ba_ref, alog_ref, dtb_ref, normw_ref, y_ref,
                     s_ref, wq_s, u_s, qk_s, kdt_s, egl_s, *, hb):
    hg = pl.program_id(1)
    t = pl.program_id(2)
    tt = DN_TILE
    nchunk = tt // DN_CHUNK
    nbat = q_ref.shape[0]
    hs = range(nbat * hb)

    @pl.when(t == 0)
    def _():
        for ref in (s_ref, wq_s, u_s, qk_s, kdt_s, egl_s):
            ref[...] = jnp.zeros_like(ref)

    rd = t & 1
    wr = 1 - rd
    state = [s_ref[h] for h in hs]
    outs = [[] for _ in hs]

    pend = {}

    def chain_a(c):
        pend["r"] = [jnp.dot(wq_s[rd, h, c], state[h].astype(BF16), preferred_element_type=F32) for h in hs]

    def chain_b(c):
        lo, hi = c * DN_CHUNK, (c + 1) * DN_CHUNK
        r = pend["r"]
        vz = []
        for h in hs:
            parts = []
            if lo > 0:
                parts.append(jnp.zeros((lo, HEAD_DIM), F32))
            parts.append(u_s[rd, h, lo:hi, :] - r[h][0:DN_CHUNK, :])
            if hi < tt:
                parts.append(jnp.zeros((tt - hi, HEAD_DIM), F32))
            vz.append(jnp.concatenate(parts, axis=0).astype(BF16))
        for h in hs:
            outs[h].append(r[h][DN_CHUNK:2 * DN_CHUNK, :]
                           + jnp.dot(qk_s[rd, h, lo:hi, :], vz[h], preferred_element_type=F32))
        for h in hs:
            state[h] = (state[h] * egl_s[rd, h, 8 * c:8 * c + 1, :]
                        + jnp.dot(kdt_s[rd, h], vz[h], preferred_element_type=F32))

    chain_a(0)

    bas = [ba_ref[bb] for bb in range(nbat)]
    beta_all = [jax.nn.sigmoid(ba) for ba in bas]
    g_all = [-jnp.exp(alog_ref[...]) * _softplus(ba + dtb_ref[...]) for ba in bas]

    row = lax.broadcasted_iota(jnp.int32, (tt, tt), 0)
    col = lax.broadcasted_iota(jnp.int32, (tt, tt), 1)
    same = (row >> 6) == (col >> 6)
    incl = same & (row >= col)
    strict = same & (row > col)

    incl_b = incl.astype(BF16)
    gc_all = [_dot_exact_lhs(incl_b, g) for g in g_all]
    gct = [jnp.transpose(g) for g in gc_all]
    sub = lax.broadcasted_iota(jnp.int32, gct[0].shape, 0)
    bat = [vh // hb for vh in hs]
    sls = [slice((vh % hb) * HEAD_DIM, (vh % hb + 1) * HEAD_DIM) for vh in hs]
    heads = [hg * hb + vh % hb for vh in hs]
    q = [q_ref[bat[vh], :, sls[vh]] for vh in hs]
    k = [k_ref[bat[vh], :, sls[vh]] for vh in hs]
    v = [v_ref[bat[vh], :, sls[vh]] for vh in hs]
    beta = [_lane_pick(beta_all[bat[vh]], heads[vh]) for vh in hs]
    gcc = [_lane_pick(gc_all[bat[vh]], heads[vh] + N_HEADS_DN) for vh in hs]
    gcr = [jnp.sum(jnp.where(sub == heads[vh] + N_HEADS_DN, gct[bat[vh]], 0.0), axis=0, keepdims=True)
           for vh in hs]
    chain_b(0)

    decay = [jnp.where(incl, jnp.exp(jnp.where(incl, gcc[h] - gcr[h], 0.0)), 0.0) for h in hs]
    kb = [k[h] * beta[h] for h in hs]
    vb = [v[h] * beta[h] for h in hs]
    a_mat = [jnp.where(strict, _dot_nt(kb[h], k[h]) * decay[h], 0.0) for h in hs]
    chain_a(1)
    qk = [_dot_nt(q[h], k[h]) * decay[h] for h in hs]
    eye = (row == col).astype(F32)
    d8 = (row >> 3) == (col >> 3)
    a8 = [jnp.where(d8, a, 0.0) for a in a_mat]
    chain_b(1)
    a8_2 = [_dot(a, a) for a in a8]
    chain_a(2)
    a8_4 = [_dot(a, a) for a in a8_2]
    chain_b(2)
    x = [_dot(eye - a, eye + a2) for a, a2 in zip(a8, a8_2)]
    chain_a(3)
    x = [_dot(xi, eye + a4) for xi, a4 in zip(x, a8_4)]
    chain_b(3)
    s = 8
    while s < DN_CHUNK:
        sh = s.bit_length() - 1
        nblk = tt // (2 * s)
        second = lambda m: jnp.concatenate([m[b * 2 * s + s:(b + 1) * 2 * s, :] for b in range(nblk)], axis=0)
        off = second(((row >> (sh + 1)) == (col >> (sh + 1))) & ((row >> sh) != (col >> sh)))
        y_half = [_dot(jnp.where(off, second(a), 0.0), xi) for a, xi in zip(a_mat, x)]
        zeros = jnp.zeros((s, tt), F32)
        y = [jnp.concatenate([p for b in range(nblk) for p in (zeros, yh[b * s:(b + 1) * s, :])], axis=0)
             for yh in y_half]
        upd = [_dot(second(xi), yi) for xi, yi in zip(x, y)]
        x = [jnp.concatenate([p for b in range(nblk)
                              for p in (xi[b * 2 * s:b * 2 * s + s, :],
                                        xi[b * 2 * s + s:(b + 1) * 2 * s, :] - ud[b * s:(b + 1) * s, :])], axis=0)
             for xi, ud in zip(x, upd)]
        s *= 2
    tinv = x
    eg = [jnp.exp(g) for g in gcc]
    wu = [_dot(tinv[h], jnp.concatenate([kb[h] * eg[h], vb[h]], axis=1)) for h in hs]
    qd = [q[h] * eg[h] for h in hs]
    gl_rows = [[g[(c + 1) * DN_CHUNK - 1:(c + 1) * DN_CHUNK, :] for c in range(nchunk)] for g in gcc]
    gl_col = [jnp.concatenate([jnp.broadcast_to(g, (DN_CHUNK, 1)) for g in rows], axis=0) for rows in gl_rows]
    kdt = [jnp.transpose(k[h] * jnp.exp(gl_col[h] - gcc[h])) for h in hs]

    ys = []
    for h in hs:
        o = jnp.concatenate(outs[h], axis=0)
        o = o * lax.rsqrt(jnp.mean(o * o, axis=-1, keepdims=True) + RMS_EPS) * normw_ref[...]
        ys.append(o * z_ref[bat[h], :, sls[h]])
    for bb in range(nbat):
        y_ref[bb] = jnp.concatenate(ys[bb * hb:(bb + 1) * hb], axis=1).astype(y_ref.dtype)
    s_ref[...] = jnp.stack(state, axis=0)

    for h in hs:
        for c in range(nchunk):
            lo, hi = c * DN_CHUNK, (c + 1) * DN_CHUNK
            wq_s[wr, h, c] = jnp.concatenate([wu[h][lo:hi, 0:HEAD_DIM], qd[h][lo:hi, :]], axis=0).astype(BF16)
            egl_s[wr, h, 8 * c:8 * c + 8, :] = jnp.broadcast_to(jnp.exp(gl_rows[h][c]), (8, HEAD_DIM))
        u_s[wr, h] = wu[h][:, HEAD_DIM:2 * HEAD_DIM]
        qk_s[wr, h] = qk[h].astype(BF16)
        kdt_s[wr, h] = kdt[h].astype(BF16)


def _deltanet(dn_qkv, z, ba, alog_row, dtb_row, normw_row):
    bsz, seq, _ = dn_qkv.shape
    tt = DN_TILE
    nt = seq // tt
    hb = DN_HEADS_PER_STEP
    nbat = DN_SEQS_PER_STEP if bsz % DN_SEQS_PER_STEP == 0 else 1
    nv = nbat * hb
    ng = N_HEADS_DN // hb
    w = hb * HEAD_DIM
    nchunk = tt // DN_CHUNK

    def cur_spec(off, width):
        return pl.BlockSpec((nbat, tt, width), lambda b, g, t: (b, jnp.minimum(t, nt - 1), g + off))

    prev_spec = pl.BlockSpec((nbat, tt, w), lambda b, g, t: (b, jnp.maximum(t - 1, 0), g))
    row_spec = pl.BlockSpec((1, LANES), lambda b, g, t: (0, 0))
    return pl.pallas_call(
        functools.partial(_deltanet_kernel, hb=hb),
        out_shape=jax.ShapeDtypeStruct((bsz, seq, D_DN), BF16),
        grid=(bsz // nbat, ng, nt + 1),
        in_specs=[cur_spec(0, w), cur_spec(ng, w), cur_spec(2 * ng, w), prev_spec,
                  pl.BlockSpec((nbat, tt, LANES), lambda b, g, t: (b, jnp.minimum(t, nt - 1), 0)),
                  row_spec, row_spec, row_spec],
        out_specs=prev_spec,
        scratch_shapes=[pltpu.VMEM((nv, HEAD_DIM, HEAD_DIM), F32),
                        pltpu.VMEM((2, nv, nchunk, 2 * DN_CHUNK, HEAD_DIM), BF16),
                        pltpu.VMEM((2, nv, tt, HEAD_DIM), F32),
                        pltpu.VMEM((2, nv, tt, tt), BF16),
                        pltpu.VMEM((2, nv, HEAD_DIM, tt), BF16),
                        pltpu.VMEM((2, nv, 8 * nchunk, HEAD_DIM), F32)],
        compiler_params=pltpu.CompilerParams(dimension_semantics=("parallel", "parallel", "arbitrary"),
                                             vmem_limit_bytes=VMEM_LIMIT),
        name="deltanet",
    )(dn_qkv, dn_qkv, dn_qkv, z, ba, alog_row, dtb_row, normw_row)


def _moba_attn_kernel(q_ref, k_ref, vt_ref, sel_ref, o_ref, acc_ref, *, nb):
    j = pl.program_id(1)
    blk = MOBA_BLOCK
    nh = N_HEADS_MOBA
    dn = (((1,), (1,)), ((), ()))
    hsl = [slice(h * HEAD_DIM, (h + 1) * HEAD_DIM) for h in range(nh)]
    qs = [q_ref[0, :, hsl[h]] for h in range(nh)]

    ki = lax.broadcasted_iota(jnp.int32, (blk, blk), 0)
    qi = lax.broadcasted_iota(jnp.int32, (blk, blk), 1)
    own = pl.ds(pl.multiple_of(j * blk, blk), blk)
    s_own = [jnp.where(ki <= qi, lax.dot_general(k_ref[0, own, hsl[h]], qs[h], dn, preferred_element_type=F32),
                       NEG_INF) for h in range(nh)]

    def scores(n, h):
        kn = k_ref[0, pl.ds(pl.multiple_of(n * blk, blk), blk), hsl[h]]
        s = lax.dot_general(kn, qs[h], dn, preferred_element_type=F32)
        return jnp.where(sel_ref[0, 0, pl.ds(h * nb + n, 1), :] > 0.5, s, NEG_INF)

    ones8 = jnp.ones((8, blk), BF16)

    def softmax_pv(n, h, s, m, l):
        m_new = jnp.maximum(m, jnp.max(s, axis=0, keepdims=True))
        pb = jnp.exp2(s - m_new).astype(BF16)
        psum = jnp.dot(ones8, pb, preferred_element_type=F32)[0:1, :]
        pv = jnp.dot(vt_ref[0, n, hsl[h], :], pb, preferred_element_type=F32)
        return m_new, psum, pv

    ms, ls = [], []
    for h in range(nh):
        m, psum, pv = softmax_pv(j, h, s_own[h], jnp.full((1, blk), NEG_INF, F32), None)
        ms.append(m)
        ls.append(psum)
        acc_ref[h] = pv

    def make_body(unroll, first):
        def body(step, carry):
            ms, ls = (list(t) for t in carry)
            n0 = first + unroll * step
            cur = [scores(n0, h) for h in range(nh)]
            for i in range(unroll):
                nxt = []
                for h in range(nh):
                    if i + 1 < unroll:
                        nxt.append(scores(n0 + i + 1, h))
                    m_new, psum, pv = softmax_pv(n0 + i, h, cur[h], ms[h], ls[h])
                    alpha = jnp.exp2(ms[h] - m_new)
                    ls[h] = alpha * ls[h] + psum
                    ms[h] = m_new
                    acc_ref[h] = acc_ref[h] * alpha + pv
                cur = nxt
            return tuple(ms), tuple(ls)
        return body

    nfull = (j + 1) // MOBA_UNROLL
    rem = j - nfull * MOBA_UNROLL
    carry = lax.fori_loop(0, nfull, make_body(MOBA_UNROLL, 0), (tuple(ms), tuple(ls)))
    carry = lax.fori_loop(0, (rem == 2).astype(jnp.int32), make_body(2, nfull * MOBA_UNROLL), carry)
    ms, ls = lax.fori_loop(0, (rem == 1).astype(jnp.int32), make_body(1, nfull * MOBA_UNROLL), carry)
    o_ref[0] = jnp.concatenate([jnp.transpose(acc_ref[h] / ls[h]) for h in range(nh)],
                               axis=1).astype(o_ref.dtype)


def _moba_attn(q_r, k_r, vt, sel):
    bsz, seq, _ = q_r.shape
    nb = seq // MOBA_BLOCK
    assert nb % MOBA_UNROLL == 0 and MOBA_UNROLL == 4, (nb, MOBA_UNROLL)
    tok_spec = pl.BlockSpec((1, MOBA_BLOCK, D_MOBA), lambda b, j: (b, j, 0))
    return pl.pallas_call(
        functools.partial(_moba_attn_kernel, nb=nb),
        out_shape=jax.ShapeDtypeStruct((bsz, seq, D_MOBA), BF16),
        grid=(bsz, nb),
        in_specs=[tok_spec,
                  pl.BlockSpec((1, seq, D_MOBA), lambda b, j: (b, 0, 0)),
                  pl.BlockSpec((1, nb, D_MOBA, MOBA_BLOCK), lambda b, j: (b, 0, 0, 0)),
                  pl.BlockSpec((1, 1, N_HEADS_MOBA * nb, MOBA_BLOCK), lambda b, j: (b, j, 0, 0))],
        out_specs=tok_spec,
        scratch_shapes=[pltpu.VMEM((N_HEADS_MOBA, HEAD_DIM, MOBA_BLOCK), F32)],
        compiler_params=pltpu.CompilerParams(dimension_semantics=("parallel", "arbitrary"),
                                             vmem_limit_bytes=VMEM_LIMIT),
        name="moba_attn",
    )(q_r, k_r, vt, sel)


def _route_record(logits):
    lane = lax.broadcasted_iota(jnp.int32, logits.shape, 1)
    big = jnp.int32(LANES)

    def first_lane(mask):
        return jnp.min(jnp.where(mask, lane, big), axis=1, keepdims=True)

    is_g = lane < N_GROUPS
    m1 = jnp.max(jnp.where(is_g, logits, NEG_INF), axis=1, keepdims=True)
    s1 = jnp.sum(jnp.where(is_g, jnp.exp(logits - m1), 0.0), axis=1, keepdims=True)
    pg = 1.0 / s1
    gsel = first_lane(is_g & (logits == m1))

    in_grp = (lane >= GATE_LANE0) & (((lane - GATE_LANE0) >> 2) == gsel) & (lane < GATE_LANE0 + N_EXPERTS)
    m2 = jnp.max(jnp.where(in_grp, logits, NEG_INF), axis=1, keepdims=True)
    s2 = jnp.sum(jnp.where(in_grp, jnp.exp(logits - m2), 0.0), axis=1, keepdims=True)
    e1 = first_lane(in_grp & (logits == m2))
    rest = in_grp & (lane != e1)
    m2b = jnp.max(jnp.where(rest, logits, NEG_INF), axis=1, keepdims=True)
    e2 = first_lane(rest & (logits == m2b))
    pe1 = 1.0 / s2
    pe2 = jnp.exp(m2b - m2) / s2
    tot = pe1 + pe2
    w1 = pg * (pe1 / tot)
    w2 = pg * (pe2 / tot)
    first_lo = e1 < e2
    lo = jnp.minimum(e1, e2)
    hi = jnp.maximum(e1, e2)
    a = (lo - GATE_LANE0) & (EXPERTS_PER_GROUP - 1)
    b = (hi - GATE_LANE0) & (EXPERTS_PER_GROUP - 1)
    bucket = gsel * N_PAIRS + ((a * (2 * EXPERTS_PER_GROUP - 1 - a)) >> 1) + (b - a - 1)
    record = jnp.where(lane == ROUTE_BUCKET, bucket.astype(F32),
                       jnp.where(lane == ROUTE_W_LO, jnp.where(first_lo, w1, w2),
                                 jnp.where(lane == ROUTE_W_HI, jnp.where(first_lo, w2, w1), 0.0)))
    return record, jnp.sum(jnp.where(lane == bucket, 1.0, 0.0), axis=0, keepdims=True)


def _mix_route_kernel(ydn_ref, ymb_ref, x_ref, wo_ref, g_ref, b_ref, rw_ref, rb_ref, h_ref, hb_ref, route_ref,
                      cnt_ref):
    nsub = cnt_ref.shape[0]
    rows = [slice(i * MOE_TS, (i + 1) * MOE_TS) for i in range(nsub)]
    wh, wl = _split2(rw_ref[...])
    wcat = jnp.concatenate([wh, wl], axis=1)

    mixes = [jnp.dot(jnp.concatenate([ydn_ref[r, :], ymb_ref[r, :]], axis=1), wo_ref[...],
                     preferred_element_type=F32) for r in rows]
    logits = []
    for r, mix in zip(rows, mixes):
        hval = _layer_norm(DEEPNORM_ALPHA * x_ref[r, :] + mix, g_ref[...], b_ref[...])
        h_ref[r, :] = hval
        hb_ref[r, :] = hval.astype(BF16)
        hh, hl = _split2(hval)
        both = jnp.dot(hh, wcat, preferred_element_type=F32)
        logits.append(both[:, 0:LANES] + both[:, LANES:2 * LANES]
                      + jnp.dot(hl, wh, preferred_element_type=F32) + rb_ref[...])
    for i, r in enumerate(rows):
        record, counts = _route_record(logits[i])
        route_ref[r, :] = record
        cnt_ref[i] = counts


def _mix_route(y_dn, y_mb, x2, wo, g1, b1, rw, rb, tm):
    n, d = x2.shape
    nsub = tm // MOE_TS
    row = lambda w: pl.BlockSpec((1, w), lambda i: (0, 0))
    return pl.pallas_call(
        _mix_route_kernel,
        out_shape=(jax.ShapeDtypeStruct((n, d), F32), jax.ShapeDtypeStruct((n, d), BF16),
                   jax.ShapeDtypeStruct((n, LANES), F32), jax.ShapeDtypeStruct((n // MOE_TS, 1, LANES), F32)),
        grid=(n // tm,),
        in_specs=[pl.BlockSpec((tm, D_DN), lambda i: (i, 0)), pl.BlockSpec((tm, D_MOBA), lambda i: (i, 0)),
                  pl.BlockSpec((tm, d), lambda i: (i, 0)), pl.BlockSpec((D_DN + D_MOBA, d), lambda i: (0, 0)),
                  row(d), row(d), pl.BlockSpec((d, LANES), lambda i: (0, 0)), row(LANES)],
        out_specs=(pl.BlockSpec((tm, d), lambda i: (i, 0)), pl.BlockSpec((tm, d), lambda i: (i, 0)),
                   pl.BlockSpec((tm, LANES), lambda i: (i, 0)), pl.BlockSpec((nsub, 1, LANES), lambda i: (i, 0, 0))),
        compiler_params=pltpu.CompilerParams(dimension_semantics=("parallel",), vmem_limit_bytes=VMEM_LIMIT),
        name="mix_route",
    )(y_dn, y_mb, x2, wo, g1, b1, rw, rb)


def _bucket_offsets_col(ohf):
    cnt = jnp.sum(ohf, axis=1, keepdims=True).astype(jnp.int32)
    pad = (((cnt + (GRAN - 1)) >> GRAN_SHIFT) << GRAN_SHIFT).astype(F32)
    r = lax.broadcasted_iota(jnp.int32, (LANES, LANES), 0)
    c = lax.broadcasted_iota(jnp.int32, (LANES, LANES), 1)
    before = jnp.where(c < r, 1.0, 0.0)
    return _dot(before, jnp.broadcast_to(pad, (LANES, LANES)))[:, 0:1]


def _moe_sort_kernel(gmap_ref, nvalid_ref, tail0_ref, taillen_ref, hb_ref, route_ref, lstrict_ref,
                     xg_ref, wsg_ref, xs_ref, ws_ref, zx_ref, zw_ref, sem):
    s = pl.program_id(0)
    nsteps = pl.num_programs(0)
    slot = s & 1
    ts = route_ref.shape[0]
    route = route_ref[...]
    rt = jnp.transpose(route)
    bucket_row = rt[ROUTE_BUCKET:ROUTE_BUCKET + 1, :].astype(jnp.int32)
    sub = lax.broadcasted_iota(jnp.int32, (LANES, ts), 0)
    ohf = jnp.where(sub == bucket_row, 1.0, 0.0)
    loff = _bucket_offsets_col(ohf)
    rank = lax.dot_general(ohf.astype(BF16), lstrict_ref[...], (((1,), (1,)), ((), ())),
                           preferred_element_type=F32)
    dest = jnp.sum(ohf * (loff + rank), axis=0, keepdims=True).astype(jnp.int32)
    rh, rl = _split2(route)
    wcat = jnp.concatenate([rh, rl], axis=1)
    rc = LROWS // PERM_CHUNKS
    for c in range(PERM_CHUNKS):
        rowi = lax.broadcasted_iota(jnp.int32, (rc, ts), 0) + c * rc
        perm = jnp.where(rowi == dest, 1.0, 0.0).astype(BF16)
        xs_ref[slot, c * rc:(c + 1) * rc, :] = jnp.dot(perm, hb_ref[...], preferred_element_type=F32).astype(BF16)
        wparts = jnp.dot(perm, wcat, preferred_element_type=F32)
        ws_ref[slot, c * rc:(c + 1) * rc, :] = wparts[:, 0:LANES] + wparts[:, LANES:2 * LANES]

    def copies(step, g):
        sl = step & 1
        src = pl.ds(pl.multiple_of(g * GRAN, GRAN), GRAN)
        dst = pl.ds(pl.multiple_of(gmap_ref[step * LGRAN + g] * GRAN, GRAN), GRAN)
        return (pltpu.make_async_copy(xs_ref.at[sl, src, :], xg_ref.at[dst, :], sem.at[0, sl]),
                pltpu.make_async_copy(ws_ref.at[sl, src, :], wsg_ref.at[dst, :], sem.at[1, sl]))

    def fill_copies(b, i):
        dst = pl.ds(pl.multiple_of((tail0_ref[b] + i) * GRAN, GRAN), GRAN)
        return (pltpu.make_async_copy(zx_ref.at[0:GRAN, :], xg_ref.at[dst, :], sem.at[2, 0]),
                pltpu.make_async_copy(zw_ref.at[0:GRAN, :], wsg_ref.at[dst, :], sem.at[2, 1]))

    def unused_tile_copies(t):
        dst = pl.ds(pl.multiple_of(t * MOE_TM, MOE_TM), MOE_TM)
        return (pltpu.make_async_copy(zx_ref, xg_ref.at[dst, :], sem.at[2, 0]),
                pltpu.make_async_copy(zw_ref, wsg_ref.at[dst, :], sem.at[2, 1]))

    def run(step, fn):
        def body(g, carry):
            for cp in copies(step, g):
                fn(cp)
            return carry
        lax.fori_loop(0, nvalid_ref[step], body, 0)

    def run_fill(fn):
        for b in range(N_BUCKETS):
            def body(i, carry, b=b):
                for cp in fill_copies(b, i):
                    fn(cp)
                return carry
            lax.fori_loop(0, taillen_ref[b], body, 0)

        def tile_body(t, carry):
            for cp in unused_tile_copies(t):
                fn(cp)
            return carry
        lax.fori_loop(tail0_ref[N_BUCKETS], xg_ref.shape[0] // MOE_TM, tile_body, 0)

    @pl.when(s == 0)
    def _():
        zx_ref[...] = jnp.zeros_like(zx_ref)
        zw_ref[...] = jnp.zeros_like(zw_ref)
        run_fill(lambda cp: cp.start())

    run(s, lambda cp: cp.start())

    @pl.when(s > 0)
    def _():
        run(s - 1, lambda cp: cp.wait())

    @pl.when(s == nsteps - 1)
    def _():
        run(s, lambda cp: cp.wait())
        run_fill(lambda cp: cp.wait())


def _moe_sort(plan, hb, route, lstrict):
    n, d = hb.shape
    ts = MOE_TS
    rows = plan["n_tiles"] * MOE_TM
    return pl.pallas_call(
        _moe_sort_kernel,
        out_shape=(jax.ShapeDtypeStruct((rows, d), BF16), jax.ShapeDtypeStruct((rows, LANES), F32)),
        grid_spec=pltpu.PrefetchScalarGridSpec(
            num_scalar_prefetch=4,
            grid=(n // ts,),
            in_specs=[pl.BlockSpec((ts, d), lambda s, *_: (s, 0)), pl.BlockSpec((ts, LANES), lambda s, *_: (s, 0)),
                      pl.BlockSpec((ts, ts), lambda s, *_: (0, 0))],
            out_specs=(pl.BlockSpec(memory_space=pl.ANY), pl.BlockSpec(memory_space=pl.ANY)),
            scratch_shapes=[pltpu.VMEM((2, LROWS, d), BF16), pltpu.VMEM((2, LROWS, LANES), F32),
                            pltpu.VMEM((MOE_TM, d), BF16), pltpu.VMEM((MOE_TM, LANES), F32),
                            pltpu.SemaphoreType.DMA((3, 2))]),
        compiler_params=pltpu.CompilerParams(dimension_semantics=("arbitrary",), vmem_limit_bytes=VMEM_LIMIT),
        name="moe_sort",
    )(plan["gmap"], plan["nvalid"], plan["tail0"], plan["taillen"], hb, route, lstrict)


def _moe_expert_kernel(xt_ref, elo_ref, ehi_ref, valid_ref, x_ref, w_ref, wg0, wu0, wd0, wg1, wu1, wd1, o_ref):
    t = pl.program_id(0)

    @pl.when(valid_ref[t] > 0)
    def _():
        x = x_ref[...]
        w = w_ref[...]
        gates = [jnp.dot(x, wg[0].astype(BF16), preferred_element_type=F32) for wg in (wg0, wg1)]
        ups = [jnp.dot(x, wu[0].astype(BF16), preferred_element_type=F32) for wu in (wu0, wu1)]
        hes = [(_silu(gates[i]) * ups[i] * w[:, lane:lane + 1]).astype(BF16)
               for i, lane in enumerate((ROUTE_W_LO, ROUTE_W_HI))]
        o_ref[...] = (jnp.dot(hes[0], wd0[0].astype(BF16), preferred_element_type=F32)
                      + jnp.dot(hes[1], wd1[0].astype(BF16), preferred_element_type=F32)).astype(o_ref.dtype)

    @pl.when(valid_ref[t] == 0)
    def _():
        o_ref[...] = jnp.zeros_like(o_ref)


def _moe_experts(plan, xg, wsg, wg, wu, wd):
    rows, d = xg.shape
    tm = MOE_TM
    tok = lambda width: pl.BlockSpec((tm, width), lambda t, xt, elo, ehi, valid: (xt[t], 0))
    lo3 = lambda shape: pl.BlockSpec(shape, lambda t, xt, elo, ehi, valid: (elo[t], 0, 0))
    hi3 = lambda shape: pl.BlockSpec(shape, lambda t, xt, elo, ehi, valid: (ehi[t], 0, 0))
    return pl.pallas_call(
        _moe_expert_kernel,
        out_shape=jax.ShapeDtypeStruct((rows, d), BF16),
        grid_spec=pltpu.PrefetchScalarGridSpec(
            num_scalar_prefetch=4,
            grid=(rows // tm,),
            in_specs=[tok(d), tok(LANES),
                      lo3((1, d, D_EXPERT)), lo3((1, d, D_EXPERT)), lo3((1, D_EXPERT, d)),
                      hi3((1, d, D_EXPERT)), hi3((1, d, D_EXPERT)), hi3((1, D_EXPERT, d))],
            out_specs=pl.BlockSpec((tm, d), lambda t, *_: (t, 0))),
        compiler_params=pltpu.CompilerParams(dimension_semantics=("arbitrary",), vmem_limit_bytes=VMEM_LIMIT),
        name="moe_experts",
    )(plan["xtile"], plan["elo"], plan["ehi"], plan["valid"], xg, wsg, wg, wu, wd, wg, wu, wd)


def _moe_unsort_kernel(gmap_ref, og_ref, route_ref, h_ref, lstrict_ref, g_ref, b_ref, out_ref, ol_ref, sem):
    s = pl.program_id(0)
    nsteps = pl.num_programs(0)
    slot = s & 1
    ts = route_ref.shape[0]

    def gather(step, fn):
        sl = step & 1
        for g in range(LGRAN):
            src = pl.ds(pl.multiple_of(gmap_ref[step * LGRAN + g] * GRAN, GRAN), GRAN)
            fn(pltpu.make_async_copy(og_ref.at[src, :], ol_ref.at[sl, g * GRAN:(g + 1) * GRAN, :], sem.at[sl]), g)

    @pl.when(s == 0)
    def _():
        gather(s, lambda cp, g: cp.start(priority=g % 2))

    @pl.when(s + 1 < nsteps)
    def _():
        gather(s + 1, lambda cp, g: cp.start(priority=g % 2))

    route = route_ref[...]
    bucket_col = route[:, ROUTE_BUCKET:ROUTE_BUCKET + 1].astype(jnp.int32)
    lane = lax.broadcasted_iota(jnp.int32, (ts, LANES), 1)
    ohf = jnp.where(lane == bucket_col, 1.0, 0.0)
    cnt = jnp.sum(ohf, axis=0, keepdims=True).astype(jnp.int32)
    pad = (((cnt + (GRAN - 1)) >> GRAN_SHIFT) << GRAN_SHIFT).astype(F32)
    r = lax.broadcasted_iota(jnp.int32, (LANES, LANES), 0)
    c = lax.broadcasted_iota(jnp.int32, (LANES, LANES), 1)
    loff = _dot(jnp.broadcast_to(pad, (8, LANES)), jnp.where(r < c, 1.0, 0.0))[0:1, :]
    rank = jnp.dot(lstrict_ref[...], ohf.astype(BF16), preferred_element_type=F32)
    dest = jnp.sum(ohf * (loff + rank), axis=1, keepdims=True).astype(jnp.int32)
    gather(s, lambda cp, g: cp.wait())
    tc = ts // PERM_CHUNKS
    lrow = lax.broadcasted_iota(jnp.int32, (tc, LROWS), 1)
    for c in range(PERM_CHUNKS):
        rows = slice(c * tc, (c + 1) * tc)
        perm_t = jnp.where(lrow == dest[rows, :], 1.0, 0.0).astype(BF16)
        ffn = jnp.dot(perm_t, ol_ref[slot], preferred_element_type=F32)
        out_ref[rows, :] = _layer_norm(DEEPNORM_ALPHA * h_ref[rows, :] + ffn, g_ref[...], b_ref[...])


def _moe_unsort(plan, og, route, hf, lstrict, g2, b2):
    n, d = hf.shape
    ts = MOE_TS
    row = pl.BlockSpec((1, d), lambda s, *_: (0, 0))
    return pl.pallas_call(
        _moe_unsort_kernel,
        out_shape=jax.ShapeDtypeStruct((n, d), F32),
        grid_spec=pltpu.PrefetchScalarGridSpec(
            num_scalar_prefetch=1,
            grid=(n // ts,),
            in_specs=[pl.BlockSpec(memory_space=pl.ANY), pl.BlockSpec((ts, LANES), lambda s, *_: (s, 0)),
                      pl.BlockSpec((ts, d), lambda s, *_: (s, 0)), pl.BlockSpec((ts, ts), lambda s, *_: (0, 0)),
                      row, row],
            out_specs=pl.BlockSpec((ts, d), lambda s, *_: (s, 0)),
            scratch_shapes=[pltpu.VMEM((2, LROWS, d), BF16), pltpu.SemaphoreType.DMA((2,))]),
        compiler_params=pltpu.CompilerParams(dimension_semantics=("arbitrary",), vmem_limit_bytes=VMEM_LIMIT),
        name="moe_unsort",
    )(plan["gmap_back"], og, route, hf, lstrict, g2, b2)


def _moe_plan(cnt_half, n):
    nsrc = n // MOE_TS
    i32 = jnp.int32
    cnt = cnt_half.reshape(nsrc, -1, LANES).sum(axis=1)[:, :N_BUCKETS].astype(i32)
    run_g = (cnt + GRAN - 1) // GRAN
    nvalid = run_g.sum(axis=1)
    loff_g = jnp.cumsum(run_g, axis=1) - run_g
    bucket_g = run_g.sum(axis=0)
    gpt = MOE_TM // GRAN
    btiles = (bucket_g + gpt - 1) // gpt
    tend = jnp.cumsum(btiles)
    tstart = tend - btiles
    gofs = tstart[None, :] * gpt + jnp.cumsum(run_g, axis=0) - run_g
    n_tiles = -(-(n + nsrc * N_BUCKETS * (GRAN - 1)) // MOE_TM) + N_BUCKETS + 1
    g = jnp.arange(LGRAN, dtype=i32)[None, :, None]
    in_run = (g >= loff_g[:, None, :]) & (g < (loff_g + run_g)[:, None, :])
    gmap = jnp.arange(LGRAN, dtype=i32)[None, :] + jnp.sum(jnp.where(in_run, (gofs - loff_g)[:, None, :], 0), axis=2)
    is_valid = jnp.arange(LGRAN, dtype=i32)[None, :] < nvalid[:, None]
    zero_gran = (n_tiles - 1) * gpt
    t = jnp.arange(n_tiles, dtype=i32)
    tb = jnp.minimum(jnp.sum(t[:, None] >= tend[None, :], axis=1), N_BUCKETS - 1)
    valid = (t < tend[-1]).astype(i32)
    pairs = [(a, b) for a in range(EXPERTS_PER_GROUP) for b in range(a + 1, EXPERTS_PER_GROUP)]
    pidx = tb % N_PAIRS
    pair_a = sum(jnp.where(pidx == i, a, 0) for i, (a, _) in enumerate(pairs))
    pair_b = sum(jnp.where(pidx == i, b, 0) for i, (_, b) in enumerate(pairs))
    grp = tb // N_PAIRS
    return {
        "n_tiles": n_tiles,
        "gmap": jnp.where(is_valid, gmap, 0).reshape(-1).astype(i32),
        "gmap_back": jnp.where(is_valid, gmap, zero_gran).reshape(-1).astype(i32),
        "nvalid": nvalid.astype(i32),
        "tail0": jnp.concatenate([tstart * gpt + bucket_g, tend[-1:]]).astype(i32),
        "taillen": (btiles * gpt - bucket_g).astype(i32),
        "xtile": jnp.where(valid > 0, t, 0).astype(i32),
        "elo": (grp * EXPERTS_PER_GROUP + pair_a).astype(i32),
        "ehi": (grp * EXPERTS_PER_GROUP + pair_b).astype(i32),
        "valid": valid,
    }


def _pad_lanes(a, lane0=0):
    return jnp.zeros((1, LANES), F32).at[0, lane0:lane0 + a.shape[0]].set(a.astype(F32))


def _rope_tables(seq):
    half = HEAD_DIM // 2
    inv_freq = ROPE_THETA ** (-np.arange(half, dtype=np.float64) / half)
    ang = np.arange(seq, dtype=np.float64)[:, None] * inv_freq[None, :]
    cos, sin = np.cos(ang), np.sin(ang)
    return (jnp.asarray(np.concatenate([cos, cos], axis=-1), F32),
            jnp.asarray(np.concatenate([-sin, sin], axis=-1), F32))


def _layer(x, w_in, conv_w, a_log, dt_bias, dn_norm_w, w_out, ln1_g, ln1_b, router_w1, router_b1,
           router_w2, router_b2, w_gate, w_up, w_down, ln2_g, ln2_b):
    bsz, seq, d = x.shape
    n = bsz * seq
    x2 = x.reshape(n, d)

    w_all = _w_prep(w_in)

    cos_t, sin_t = _rope_tables(seq)
    nb = seq // MOBA_BLOCK
    dn_qkv, z, ba, q_r, k_r, vt, sel = _in_proj(x2, w_all, conv_w, cos_t, sin_t, min(IN_PROJ_TM, seq), seq)

    y_dn = _deltanet(dn_qkv.reshape(bsz, seq, 3 * D_DN), z.reshape(bsz, seq, D_DN), ba.reshape(bsz, seq, LANES),
                     _pad_lanes(a_log, N_HEADS_DN), _pad_lanes(dt_bias, N_HEADS_DN),
                     dn_norm_w.astype(F32).reshape(1, HEAD_DIM))

    q_r, k_r = q_r.reshape(bsz, seq, D_MOBA), k_r.reshape(bsz, seq, D_MOBA)
    vt = vt.reshape(bsz, nb, D_MOBA, MOBA_BLOCK)
    sel = sel.reshape(bsz, nb, N_HEADS_MOBA * nb, MOBA_BLOCK)
    y_mb = _moba_attn(q_r, k_r, vt, sel)

    rw = jnp.concatenate([router_w1, jnp.transpose(router_w2, (1, 0, 2)).reshape(d, N_EXPERTS)], axis=1)
    rw = jnp.pad(rw, ((0, 0), (0, LANES - rw.shape[1])))
    rb = _pad_lanes(jnp.concatenate([router_b1, router_b2.reshape(-1)]))
    hf, hb, route, cnt = _mix_route(y_dn.reshape(n, D_DN), y_mb.reshape(n, D_MOBA), x2, w_out.astype(BF16),
                                    ln1_g.reshape(1, d), ln1_b.reshape(1, d), rw, rb, min(2 * MOE_TS, n))

    plan = _moe_plan(cnt, n)
    idx = jnp.arange(MOE_TS, dtype=jnp.int32)
    lstrict = (idx[None, :] < idx[:, None]).astype(BF16)
    xg, wsg = _moe_sort(plan, hb, route, lstrict)
    og = _moe_experts(plan, xg, wsg, w_gate, w_up, w_down)
    out = _moe_unsort(plan, og, route, hf, lstrict, ln2_g.reshape(1, d), ln2_b.reshape(1, d))
    return out.reshape(bsz, seq, d)


def kernel(x, w_in, conv_w, a_log, dt_bias, dn_norm_w, w_out, ln1_g, ln1_b, router_w1, router_b1, router_w2, router_b2, expert_w_gate, expert_w_up, expert_w_down, ln2_g, ln2_b):
    for l in range(DEPTH):
        x = _layer(x, w_in[l], conv_w[l], a_log[l], dt_bias[l], dn_norm_w[l], w_out[l], ln1_g[l], ln1_b[l],
                   router_w1[l], router_b1[l], router_w2[l], router_b2[l], expert_w_gate[l], expert_w_up[l],
                   expert_w_down[l], ln2_g[l], ln2_b[l])
    return x
```
